```python
import math
import jax, jax.numpy as jnp
from jax import lax
import numpy as np

D_MODEL = 1024
BATCH = 8
SEQ = 8192
DEPTH = 2

RMS_EPS = 1e-6
LN_EPS = 1e-5
MIX_WIDTH = D_MODEL
POOL_WIDTH = MIX_WIDTH // 2
POOL_WINDOWS = (2, 4, 8, 16)
POOL_GROUPS = len(POOL_WINDOWS)
POOL_GROUP_DIM = POOL_WIDTH // POOL_GROUPS
SSM_WIDTH = MIX_WIDTH - POOL_WIDTH
SSM_GROUP_DIM = 16
SSM_GROUPS = SSM_WIDTH // SSM_GROUP_DIM
SSM_STATE = 64
DT_MIN = 0.001
DT_MAX = 0.1
CONV_CHANNELS = D_MODEL
CONV_KERNEL = 31
N_EVEN = (DEPTH + 1) // 2
N_ODD = DEPTH // 2

kernel_name = "hybrid_pool_s5_conformer_gated"


def _rmsnorm(x, g):
    x32 = x.astype(jnp.float32)
    y = x32 * lax.rsqrt(jnp.mean(x32 * x32, axis=-1, keepdims=True) + RMS_EPS)
    return (y * g.astype(jnp.float32)).astype(x.dtype)


def _multiscale_pool(u, pool_w, pool_scale):
    b, s, _ = u.shape
    u32 = u.astype(jnp.float32).reshape(b, s, POOL_GROUPS, POOL_GROUP_DIM)
    csum = jnp.cumsum(u32, axis=1)
    pos = jnp.arange(1, s + 1, dtype=jnp.float32)[None, :, None]
    outs = []
    for g, w in enumerate(POOL_WINDOWS):
        c = csum[:, :, g, :]
        lag = jnp.pad(c, ((0, 0), (w, 0), (0, 0)))[:, :s, :]
        outs.append((c - lag) / jnp.minimum(pos, float(w)) - u32[:, :, g, :])
    pooled = jnp.stack(outs, axis=2)
    mixed = jnp.einsum('bsgc,gcd->bsgd', pooled, pool_w.astype(jnp.float32))
    return (mixed.reshape(b, s, POOL_WIDTH) * pool_scale.astype(jnp.float32)).astype(u.dtype)


def _scan_combine(left, right):
    a1r, a1i, b1r, b1i = left
    a2r, a2i, b2r, b2i = right
    ar = a2r * a1r - a2i * a1i
    ai = a2r * a1i + a2i * a1r
    br = a2r * b1r - a2i * b1i + b2r
    bi = a2r * b1i + a2i * b1r + b2i
    return (ar, ai, br, bi)


def _s5(u, log_dt, a_re, a_im, b_re, b_im, c_re, c_im, d_skip, w_glu):
    bsz, s, _ = u.shape
    f32 = jnp.float32
    u32 = u.astype(f32).reshape(bsz, s, SSM_GROUPS, SSM_GROUP_DIM)
    dt = jnp.exp(log_dt.astype(f32))[:, None]
    ar = a_re.astype(f32)
    ai = a_im.astype(f32)
    mag = jnp.exp(ar * dt)
    ang = ai * dt
    abar_re = mag * jnp.cos(ang)
    abar_im = mag * jnp.sin(ang)
    den = ar * ar + ai * ai
    nr = abar_re - 1.0
    ni = abar_im
    k_re = (nr * ar + ni * ai) / den
    k_im = (ni * ar - nr * ai) / den
    br = b_re.astype(f32)
    bi = b_im.astype(f32)
    bb_re = k_re[..., None] * br - k_im[..., None] * bi
    bb_im = k_re[..., None] * bi + k_im[..., None] * br
    bu_re = jnp.einsum('bsgh,gph->bsgp', u32, bb_re)
    bu_im = jnp.einsum('bsgh,gph->bsgp', u32, bb_im)
    shape = bu_re.shape
    _, _, x_re, x_im = lax.associative_scan(
        _scan_combine,
        (jnp.broadcast_to(abar_re, shape), jnp.broadcast_to(abar_im, shape), bu_re, bu_im),
        axis=1)
    y = (jnp.einsum('bsgp,ghp->bsgh', x_re, c_re.astype(f32))
         - jnp.einsum('bsgp,ghp->bsgh', x_im, c_im.astype(f32)))
    y = y.reshape(bsz, s, SSM_WIDTH) + d_skip.astype(f32) * u32.reshape(bsz, s, SSM_WIDTH)
    gv = jnp.einsum('bsc,ce->bse', y, w_glu.astype(f32))
    val, gate = jnp.split(gv, 2, axis=-1)
    return (val * jax.nn.sigmoid(gate)).astype(u.dtype)


def _even_mixer(h, w_in, pool_w, pool_scale, log_dt, a_re, a_im, b_re, b_im,
                c_re, c_im, d_skip, w_glu, w_out):
    proj = jnp.einsum('bsd,de->bse', h, w_in)
    u_pool = proj[..., :POOL_WIDTH]
    u_ssm = proj[..., POOL_WIDTH:MIX_WIDTH]
    z = proj[..., MIX_WIDTH:]
    y_pool = _multiscale_pool(u_pool, pool_w, pool_scale)
    y_ssm = _s5(u_ssm, log_dt, a_re, a_im, b_re, b_im, c_re, c_im, d_skip, w_glu)
    y = jnp.concatenate([y_pool, y_ssm], axis=-1) * jax.nn.silu(z)
    return jnp.einsum('bse,ed->bsd', y, w_out)


def _odd_mixer(h, w_in, conv_w, conv_b, ln_g, ln_b, w_out):
    proj = jnp.einsum('bsd,de->bse', h, w_in)
    val = proj[..., :CONV_CHANNELS]
    gt = proj[..., CONV_CHANNELS:2 * CONV_CHANNELS]
    z = proj[..., 2 * CONV_CHANNELS:]
    g = (val * jax.nn.sigmoid(gt)).astype(jnp.float32)
    kern = conv_w.astype(jnp.float32)[:, None, :]
    c = lax.conv_general_dilated(
        g, kern, window_strides=(1,), padding=[(CONV_KERNEL - 1, 0)],
        dimension_numbers=('NWC', 'WIO', 'NWC'), feature_group_count=CONV_CHANNELS)
    c = c + conv_b.astype(jnp.float32)
    mu = jnp.mean(c, axis=-1, keepdims=True)
    cc = c - mu
    var = jnp.mean(cc * cc, axis=-1, keepdims=True)
    c = cc * lax.rsqrt(var + LN_EPS) * ln_g.astype(jnp.float32) + ln_b.astype(jnp.float32)
    y = (jax.nn.silu(c) * jax.nn.silu(z.astype(jnp.float32))).astype(h.dtype)
    return jnp.einsum('bse,ed->bsd', y, w_out)


def _fwd_setup_inputs(seed: int = 0) -> dict:
    key = jax.random.key(seed)
    ks = jax.random.split(key, 24)
    f32 = jnp.float32

    def nrm(k, shape, scale):
        return jax.random.normal(k, shape, f32) * scale

    n_idx = jnp.arange(SSM_STATE, dtype=f32)
    a_im0 = jnp.broadcast_to(math.pi * n_idx, (N_EVEN, SSM_GROUPS, SSM_STATE))
    return {
        "x": jax.random.normal(ks[0], (BATCH, SEQ, D_MODEL), f32),
        "even_norm": 1.0 + nrm(ks[1], (N_EVEN, D_MODEL), 0.05),
        "even_w_in": nrm(ks[2], (N_EVEN, D_MODEL, 2 * MIX_WIDTH), D_MODEL ** -0.5),
        "pool_w": nrm(ks[3], (N_EVEN, POOL_GROUPS, POOL_GROUP_DIM, POOL_GROUP_DIM), POOL_GROUP_DIM ** -0.5),
        "pool_scale": 1.0 + nrm(ks[4], (N_EVEN, POOL_WIDTH), 0.05),
        "ssm_log_dt": jax.random.uniform(ks[5], (N_EVEN, SSM_GROUPS), f32,
                                         math.log(DT_MIN), math.log(DT_MAX)),
        "ssm_a_re": -0.5 * jnp.exp(nrm(ks[6], (N_EVEN, SSM_GROUPS, SSM_STATE), 0.05)),
        "ssm_a_im": a_im0 + nrm(ks[7], (N_EVEN, SSM_GROUPS, SSM_STATE), 0.01),
        "ssm_b_re": nrm(ks[8], (N_EVEN, SSM_GROUPS, SSM_STATE, SSM_GROUP_DIM), (2 * SSM_GROUP_DIM) ** -0.5),
        "ssm_b_im": nrm(ks[9], (N_EVEN, SSM_GROUPS, SSM_STATE, SSM_GROUP_DIM), (2 * SSM_GROUP_DIM) ** -0.5),
        "ssm_c_re": nrm(ks[10], (N_EVEN, SSM_GROUPS, SSM_GROUP_DIM, SSM_STATE), SSM_STATE ** -0.5),
        "ssm_c_im": nrm(ks[11], (N_EVEN, SSM_GROUPS, SSM_GROUP_DIM, SSM_STATE), SSM_STATE ** -0.5),
        "ssm_d": nrm(ks[12], (N_EVEN, SSM_WIDTH), 1.0),
        "ssm_w_glu": nrm(ks[13], (N_EVEN, SSM_WIDTH, 2 * SSM_WIDTH), SSM_WIDTH ** -0.5),
        "even_w_out": nrm(ks[14], (N_EVEN, MIX_WIDTH, D_MODEL), MIX_WIDTH ** -0.5),
        "odd_norm": 1.0 + nrm(ks[15], (N_ODD, D_MODEL), 0.05),
        "odd_w_in": nrm(ks[16], (N_ODD, D_MODEL, 3 * CONV_CHANNELS), D_MODEL ** -0.5),
        "conv_w": nrm(ks[17], (N_ODD, CONV_KERNEL, CONV_CHANNELS), CONV_KERNEL ** -0.5),
        "conv_b": nrm(ks[18], (N_ODD, CONV_CHANNELS), 0.02),
        "conv_ln_g": 1.0 + nrm(ks[19], (N_ODD, CONV_CHANNELS), 0.05),
        "conv_ln_b": nrm(ks[20], (N_ODD, CONV_CHANNELS), 0.02),
        "odd_w_out": nrm(ks[21], (N_ODD, CONV_CHANNELS, D_MODEL), CONV_CHANNELS ** -0.5),
        "final_norm": 1.0 + nrm(ks[22], (D_MODEL,), 0.05),
    }


def _fwd_reference(x, even_norm, even_w_in, pool_w, pool_scale, ssm_log_dt, ssm_a_re, ssm_a_im,
              ssm_b_re, ssm_b_im, ssm_c_re, ssm_c_im, ssm_d, ssm_w_glu, even_w_out,
              odd_norm, odd_w_in, conv_w, conv_b, conv_ln_g, conv_ln_b, odd_w_out, final_norm):
    for i in range(DEPTH):
        j = i // 2
        if i % 2 == 0:
            h = _rmsnorm(x, even_norm[j])
            x = x + _even_mixer(h, even_w_in[j], pool_w[j], pool_scale[j], ssm_log_dt[j],
                                ssm_a_re[j], ssm_a_im[j], ssm_b_re[j], ssm_b_im[j],
                                ssm_c_re[j], ssm_c_im[j], ssm_d[j], ssm_w_glu[j], even_w_out[j])
        else:
            h = _rmsnorm(x, odd_norm[j])
            x = x + _odd_mixer(h, odd_w_in[j], conv_w[j], conv_b[j], conv_ln_g[j],
                               conv_ln_b[j], odd_w_out[j])
    return _rmsnorm(x, final_norm)


import jax as _jax
import jax.numpy as _jnp

TWIN_FORMAT = 'train_step'
FWD_PARAMS = ['x', 'even_norm', 'even_w_in', 'pool_w', 'pool_scale', 'ssm_log_dt', 'ssm_a_re', 'ssm_a_im', 'ssm_b_re', 'ssm_b_im', 'ssm_c_re', 'ssm_c_im', 'ssm_d', 'ssm_w_glu', 'even_w_out', 'odd_norm', 'odd_w_in', 'conv_w', 'conv_b', 'conv_ln_g', 'conv_ln_b', 'odd_w_out', 'final_norm']
TWIN_WEIGHTS = ['even_norm', 'even_w_in', 'pool_w', 'pool_scale', 'ssm_log_dt', 'ssm_a_re', 'ssm_a_im', 'ssm_b_re', 'ssm_b_im', 'ssm_c_re', 'ssm_c_im', 'ssm_d', 'ssm_w_glu', 'even_w_out', 'odd_norm', 'odd_w_in', 'conv_w', 'conv_b', 'conv_ln_g', 'conv_ln_b', 'odd_w_out', 'final_norm']
TWIN_DIFF_INPUT = 'x'
TWIN_INPUTS = ['x', 'even_norm', 'even_w_in', 'pool_w', 'pool_scale', 'ssm_log_dt', 'ssm_a_re', 'ssm_a_im', 'ssm_b_re', 'ssm_b_im', 'ssm_c_re', 'ssm_c_im', 'ssm_d', 'ssm_w_glu', 'even_w_out', 'odd_norm', 'odd_w_in', 'conv_w', 'conv_b', 'conv_ln_g', 'conv_ln_b', 'odd_w_out', 'final_norm', 'loss_target', 'm_even_norm', 'm_even_w_in', 'm_pool_w', 'm_pool_scale', 'm_ssm_log_dt', 'm_ssm_a_re', 'm_ssm_a_im', 'm_ssm_b_re', 'm_ssm_b_im', 'm_ssm_c_re', 'm_ssm_c_im', 'm_ssm_d', 'm_ssm_w_glu', 'm_even_w_out', 'm_odd_norm', 'm_odd_w_in', 'm_conv_w', 'm_conv_b', 'm_conv_ln_g', 'm_conv_ln_b', 'm_odd_w_out', 'm_final_norm', 'v_even_norm', 'v_even_w_in', 'v_pool_w', 'v_pool_scale', 'v_ssm_log_dt', 'v_ssm_a_re', 'v_ssm_a_im', 'v_ssm_b_re', 'v_ssm_b_im', 'v_ssm_c_re', 'v_ssm_c_im', 'v_ssm_d', 'v_ssm_w_glu', 'v_even_w_out', 'v_odd_norm', 'v_odd_w_in', 'v_conv_w', 'v_conv_b', 'v_conv_ln_g', 'v_conv_ln_b', 'v_odd_w_out', 'v_final_norm']
TWIN_OUTPUTS = ['loss', 'grad_x', 'grad_even_norm', 'grad_even_w_in', 'grad_pool_w', 'grad_pool_scale', 'grad_ssm_log_dt', 'grad_ssm_a_re', 'grad_ssm_a_im', 'grad_ssm_b_re', 'grad_ssm_b_im', 'grad_ssm_c_re', 'grad_ssm_c_im', 'grad_ssm_d', 'grad_ssm_w_glu', 'grad_even_w_out', 'grad_odd_norm', 'grad_odd_w_in', 'grad_conv_w', 'grad_conv_b', 'grad_conv_ln_g', 'grad_conv_ln_b', 'grad_odd_w_out', 'grad_final_norm', 'delta_even_norm', 'delta_even_w_in', 'delta_pool_w', 'delta_pool_scale', 'delta_ssm_log_dt', 'delta_ssm_a_re', 'delta_ssm_a_im', 'delta_ssm_b_re', 'delta_ssm_b_im', 'delta_ssm_c_re', 'delta_ssm_c_im', 'delta_ssm_d', 'delta_ssm_w_glu', 'delta_even_w_out', 'delta_odd_norm', 'delta_odd_w_in', 'delta_conv_w', 'delta_conv_b', 'delta_conv_ln_g', 'delta_conv_ln_b', 'delta_odd_w_out', 'delta_final_norm', 'new_m_even_norm', 'new_m_even_w_in', 'new_m_pool_w', 'new_m_pool_scale', 'new_m_ssm_log_dt', 'new_m_ssm_a_re', 'new_m_ssm_a_im', 'new_m_ssm_b_re', 'new_m_ssm_b_im', 'new_m_ssm_c_re', 'new_m_ssm_c_im', 'new_m_ssm_d', 'new_m_ssm_w_glu', 'new_m_even_w_out', 'new_m_odd_norm', 'new_m_odd_w_in', 'new_m_conv_w', 'new_m_conv_b', 'new_m_conv_ln_g', 'new_m_conv_ln_b', 'new_m_odd_w_out', 'new_m_final_norm', 'new_v_even_norm', 'new_v_even_w_in', 'new_v_pool_w', 'new_v_pool_scale', 'new_v_ssm_log_dt', 'new_v_ssm_a_re', 'new_v_ssm_a_im', 'new_v_ssm_b_re', 'new_v_ssm_b_im', 'new_v_ssm_c_re', 'new_v_ssm_c_im', 'new_v_ssm_d', 'new_v_ssm_w_glu', 'new_v_even_w_out', 'new_v_odd_norm', 'new_v_odd_w_in', 'new_v_conv_w', 'new_v_conv_b', 'new_v_conv_ln_g', 'new_v_conv_ln_b', 'new_v_odd_w_out', 'new_v_final_norm']
TWIN_LEAF_KINDS = {'loss': 'loss', 'grad_x': 'grad_x', 'grad_even_norm': 'grad_w', 'grad_even_w_in': 'grad_w', 'grad_pool_w': 'grad_w', 'grad_pool_scale': 'grad_w', 'grad_ssm_log_dt': 'grad_w', 'grad_ssm_a_re': 'grad_w', 'grad_ssm_a_im': 'grad_w', 'grad_ssm_b_re': 'grad_w', 'grad_ssm_b_im': 'grad_w', 'grad_ssm_c_re': 'grad_w', 'grad_ssm_c_im': 'grad_w', 'grad_ssm_d': 'grad_w', 'grad_ssm_w_glu': 'grad_w', 'grad_even_w_out': 'grad_w', 'grad_odd_norm': 'grad_w', 'grad_odd_w_in': 'grad_w', 'grad_conv_w': 'grad_w', 'grad_conv_b': 'grad_w', 'grad_conv_ln_g': 'grad_w', 'grad_conv_ln_b': 'grad_w', 'grad_odd_w_out': 'grad_w', 'grad_final_norm': 'grad_w', 'delta_even_norm': 'delta_w', 'delta_even_w_in': 'delta_w', 'delta_pool_w': 'delta_w', 'delta_pool_scale': 'delta_w', 'delta_ssm_log_dt': 'delta_w', 'delta_ssm_a_re': 'delta_w', 'delta_ssm_a_im': 'delta_w', 'delta_ssm_b_re': 'delta_w', 'delta_ssm_b_im': 'delta_w', 'delta_ssm_c_re': 'delta_w', 'delta_ssm_c_im': 'delta_w', 'delta_ssm_d': 'delta_w', 'delta_ssm_w_glu': 'delta_w', 'delta_even_w_out': 'delta_w', 'delta_odd_norm': 'delta_w', 'delta_odd_w_in': 'delta_w', 'delta_conv_w': 'delta_w', 'delta_conv_b': 'delta_w', 'delta_conv_ln_g': 'delta_w', 'delta_conv_ln_b': 'delta_w', 'delta_odd_w_out': 'delta_w', 'delta_final_norm': 'delta_w', 'new_m_even_norm': 'new_m', 'new_m_even_w_in': 'new_m', 'new_m_pool_w': 'new_m', 'new_m_pool_scale': 'new_m', 'new_m_ssm_log_dt': 'new_m', 'new_m_ssm_a_re': 'new_m', 'new_m_ssm_a_im': 'new_m', 'new_m_ssm_b_re': 'new_m', 'new_m_ssm_b_im': 'new_m', 'new_m_ssm_c_re': 'new_m', 'new_m_ssm_c_im': 'new_m', 'new_m_ssm_d': 'new_m', 'new_m_ssm_w_glu': 'new_m', 'new_m_even_w_out': 'new_m', 'new_m_odd_norm': 'new_m', 'new_m_odd_w_in': 'new_m', 'new_m_conv_w': 'new_m', 'new_m_conv_b': 'new_m', 'new_m_conv_ln_g': 'new_m', 'new_m_conv_ln_b': 'new_m', 'new_m_odd_w_out': 'new_m', 'new_m_final_norm': 'new_m', 'new_v_even_norm': 'new_v', 'new_v_even_w_in': 'new_v', 'new_v_pool_w': 'new_v', 'new_v_pool_scale': 'new_v', 'new_v_ssm_log_dt': 'new_v', 'new_v_ssm_a_re': 'new_v', 'new_v_ssm_a_im': 'new_v', 'new_v_ssm_b_re': 'new_v', 'new_v_ssm_b_im': 'new_v', 'new_v_ssm_c_re': 'new_v', 'new_v_ssm_c_im': 'new_v', 'new_v_ssm_d': 'new_v', 'new_v_ssm_w_glu': 'new_v', 'new_v_even_w_out': 'new_v', 'new_v_odd_norm': 'new_v', 'new_v_odd_w_in': 'new_v', 'new_v_conv_w': 'new_v', 'new_v_conv_b': 'new_v', 'new_v_conv_ln_g': 'new_v', 'new_v_conv_ln_b': 'new_v', 'new_v_odd_w_out': 'new_v', 'new_v_final_norm': 'new_v'}


def _forward(args):
    return _fwd_reference(*[args[k] for k in FWD_PARAMS])


def _output_shape():
    def fwd():
        inp = _fwd_setup_inputs(0)
        return _fwd_reference(*[inp[k] for k in FWD_PARAMS])
    out = _jax.eval_shape(fwd)
    return out.shape, out.dtype

N_MICROBATCH = 1
ADAM_LR = 0.001
ADAM_B1 = 0.9
ADAM_B2 = 0.999
ADAM_EPS = 1e-08
ADAM_WD = 0.01
ADAM_STEP = 10
PER_EXAMPLE_BATCH_AXIS = {'x': 0, 'loss_target': 0}
SHARED_INPUTS = []
_WEIGHT_DTYPES = {'even_norm': _jnp.float32, 'even_w_in': _jnp.float32, 'pool_w': _jnp.float32, 'pool_scale': _jnp.float32, 'ssm_log_dt': _jnp.float32, 'ssm_a_re': _jnp.float32, 'ssm_a_im': _jnp.float32, 'ssm_b_re': _jnp.float32, 'ssm_b_im': _jnp.float32, 'ssm_c_re': _jnp.float32, 'ssm_c_im': _jnp.float32, 'ssm_d': _jnp.float32, 'ssm_w_glu': _jnp.float32, 'even_w_out': _jnp.float32, 'odd_norm': _jnp.float32, 'odd_w_in': _jnp.float32, 'conv_w': _jnp.float32, 'conv_b': _jnp.float32, 'conv_ln_g': _jnp.float32, 'conv_ln_b': _jnp.float32, 'odd_w_out': _jnp.float32, 'final_norm': _jnp.float32}
MOMENT_SCALE = {'even_norm': 1.509390e-01, 'even_w_in': 1.097264e-01, 'pool_w': 1.299582e-01, 'pool_scale': 1.338410e-01, 'ssm_log_dt': 4.570713e+00, 'ssm_a_re': 7.292588e-03, 'ssm_a_im': 6.324018e-03, 'ssm_b_re': 4.660744e-03, 'ssm_b_im': 4.629485e-03, 'ssm_c_re': 6.801478e-03, 'ssm_c_im': 6.546309e-03, 'ssm_d': 9.365220e-02, 'ssm_w_glu': 5.945121e-02, 'even_w_out': 1.081113e-01, 'odd_norm': 1.191033e-01, 'odd_w_in': 6.946213e-02, 'conv_w': 8.223528e-02, 'conv_b': 1.832774e-01, 'conv_ln_g': 9.620846e-02, 'conv_ln_b': 8.460236e-02, 'odd_w_out': 8.070551e-02, 'final_norm': 6.408756e+01}


def _to_microbatches(a, axis):
    t = _jnp.moveaxis(a, axis, 0)
    t = t.reshape((N_MICROBATCH, t.shape[0] // N_MICROBATCH) + t.shape[1:])
    return _jnp.moveaxis(t, 1, axis + 1)


def setup_inputs(seed: int = 0) -> dict:
    inp = _fwd_setup_inputs(seed)
    key = _jax.random.fold_in(_jax.random.key(seed), 7919)
    shape, _ = _output_shape()
    out = dict(inp)
    out["loss_target"] = _jax.random.normal(_jax.random.fold_in(key, 0), shape, _jnp.float32)
    for i, name in enumerate(TWIN_WEIGHTS):
        w = inp[name].astype(_jnp.float32)
        if MOMENT_SCALE is None:
            s = _jnp.sqrt(_jnp.mean(_jnp.square(w)) + 1e-30)
        else:
            s = MOMENT_SCALE[name]
        km, kv = _jax.random.split(_jax.random.fold_in(key, i + 1))
        out[name] = w
        out["m_" + name] = s * _jax.random.normal(km, w.shape, _jnp.float32)
        out["v_" + name] = (s * s) * _jax.random.uniform(kv, w.shape, _jnp.float32, 0.5, 1.5)
    if N_MICROBATCH > 1:
        for name, axis in PER_EXAMPLE_BATCH_AXIS.items():
            out[name] = _to_microbatches(out[name], axis)
    return {'x': out['x'], 'even_norm': out['even_norm'], 'even_w_in': out['even_w_in'], 'pool_w': out['pool_w'], 'pool_scale': out['pool_scale'], 'ssm_log_dt': out['ssm_log_dt'], 'ssm_a_re': out['ssm_a_re'], 'ssm_a_im': out['ssm_a_im'], 'ssm_b_re': out['ssm_b_re'], 'ssm_b_im': out['ssm_b_im'], 'ssm_c_re': out['ssm_c_re'], 'ssm_c_im': out['ssm_c_im'], 'ssm_d': out['ssm_d'], 'ssm_w_glu': out['ssm_w_glu'], 'even_w_out': out['even_w_out'], 'odd_norm': out['odd_norm'], 'odd_w_in': out['odd_w_in'], 'conv_w': out['conv_w'], 'conv_b': out['conv_b'], 'conv_ln_g': out['conv_ln_g'], 'conv_ln_b': out['conv_ln_b'], 'odd_w_out': out['odd_w_out'], 'final_norm': out['final_norm'], 'loss_target': out['loss_target'], 'm_even_norm': out['m_even_norm'], 'm_even_w_in': out['m_even_w_in'], 'm_pool_w': out['m_pool_w'], 'm_pool_scale': out['m_pool_scale'], 'm_ssm_log_dt': out['m_ssm_log_dt'], 'm_ssm_a_re': out['m_ssm_a_re'], 'm_ssm_a_im': out['m_ssm_a_im'], 'm_ssm_b_re': out['m_ssm_b_re'], 'm_ssm_b_im': out['m_ssm_b_im'], 'm_ssm_c_re': out['m_ssm_c_re'], 'm_ssm_c_im': out['m_ssm_c_im'], 'm_ssm_d': out['m_ssm_d'], 'm_ssm_w_glu': out['m_ssm_w_glu'], 'm_even_w_out': out['m_even_w_out'], 'm_odd_norm': out['m_odd_norm'], 'm_odd_w_in': out['m_odd_w_in'], 'm_conv_w': out['m_conv_w'], 'm_conv_b': out['m_conv_b'], 'm_conv_ln_g': out['m_conv_ln_g'], 'm_conv_ln_b': out['m_conv_ln_b'], 'm_odd_w_out': out['m_odd_w_out'], 'm_final_norm': out['m_final_norm'], 'v_even_norm': out['v_even_norm'], 'v_even_w_in': out['v_even_w_in'], 'v_pool_w': out['v_pool_w'], 'v_pool_scale': out['v_pool_scale'], 'v_ssm_log_dt': out['v_ssm_log_dt'], 'v_ssm_a_re': out['v_ssm_a_re'], 'v_ssm_a_im': out['v_ssm_a_im'], 'v_ssm_b_re': out['v_ssm_b_re'], 'v_ssm_b_im': out['v_ssm_b_im'], 'v_ssm_c_re': out['v_ssm_c_re'], 'v_ssm_c_im': out['v_ssm_c_im'], 'v_ssm_d': out['v_ssm_d'], 'v_ssm_w_glu': out['v_ssm_w_glu'], 'v_even_w_out': out['v_even_w_out'], 'v_odd_norm': out['v_odd_norm'], 'v_odd_w_in': out['v_odd_w_in'], 'v_conv_w': out['v_conv_w'], 'v_conv_b': out['v_conv_b'], 'v_conv_ln_g': out['v_conv_ln_g'], 'v_conv_ln_b': out['v_conv_ln_b'], 'v_odd_w_out': out['v_odd_w_out'], 'v_final_norm': out['v_final_norm']}


def _loss(weights, diff, rest, loss_target):
    with _jax.named_scope("forward"):
        args = {**rest, TWIN_DIFF_INPUT: diff, **{k: w.astype(_WEIGHT_DTYPES[k]) for k, w in weights.items()}}
        y = _forward(args)
    with _jax.named_scope("loss_head"):
        err = _jnp.square(y.astype(_jnp.float32) - loss_target)
        return 0.5 * _jnp.sum(_jnp.mean(err, axis=-1)) if err.ndim else 0.5 * err


def _adamw(w, g, m, v):
    m = ADAM_B1 * m + (1.0 - ADAM_B1) * g
    v = ADAM_B2 * v + (1.0 - ADAM_B2) * _jnp.square(g)
    m_hat = m / (1.0 - ADAM_B1 ** ADAM_STEP)
    v_hat = v / (1.0 - ADAM_B2 ** ADAM_STEP)
    delta = -ADAM_LR * (m_hat / (_jnp.sqrt(v_hat) + ADAM_EPS) + ADAM_WD * w)
    return delta, m, v


def reference(x, even_norm, even_w_in, pool_w, pool_scale, ssm_log_dt, ssm_a_re, ssm_a_im, ssm_b_re, ssm_b_im, ssm_c_re, ssm_c_im, ssm_d, ssm_w_glu, even_w_out, odd_norm, odd_w_in, conv_w, conv_b, conv_ln_g, conv_ln_b, odd_w_out, final_norm, loss_target, m_even_norm, m_even_w_in, m_pool_w, m_pool_scale, m_ssm_log_dt, m_ssm_a_re, m_ssm_a_im, m_ssm_b_re, m_ssm_b_im, m_ssm_c_re, m_ssm_c_im, m_ssm_d, m_ssm_w_glu, m_even_w_out, m_odd_norm, m_odd_w_in, m_conv_w, m_conv_b, m_conv_ln_g, m_conv_ln_b, m_odd_w_out, m_final_norm, v_even_norm, v_even_w_in, v_pool_w, v_pool_scale, v_ssm_log_dt, v_ssm_a_re, v_ssm_a_im, v_ssm_b_re, v_ssm_b_im, v_ssm_c_re, v_ssm_c_im, v_ssm_d, v_ssm_w_glu, v_even_w_out, v_odd_norm, v_odd_w_in, v_conv_w, v_conv_b, v_conv_ln_g, v_conv_ln_b, v_odd_w_out, v_final_norm):
    given = dict(x=x, even_norm=even_norm, even_w_in=even_w_in, pool_w=pool_w, pool_scale=pool_scale, ssm_log_dt=ssm_log_dt, ssm_a_re=ssm_a_re, ssm_a_im=ssm_a_im, ssm_b_re=ssm_b_re, ssm_b_im=ssm_b_im, ssm_c_re=ssm_c_re, ssm_c_im=ssm_c_im, ssm_d=ssm_d, ssm_w_glu=ssm_w_glu, even_w_out=even_w_out, odd_norm=odd_norm, odd_w_in=odd_w_in, conv_w=conv_w, conv_b=conv_b, conv_ln_g=conv_ln_g, conv_ln_b=conv_ln_b, odd_w_out=odd_w_out, final_norm=final_norm, loss_target=loss_target, m_even_norm=m_even_norm, m_even_w_in=m_even_w_in, m_pool_w=m_pool_w, m_pool_scale=m_pool_scale, m_ssm_log_dt=m_ssm_log_dt, m_ssm_a_re=m_ssm_a_re, m_ssm_a_im=m_ssm_a_im, m_ssm_b_re=m_ssm_b_re, m_ssm_b_im=m_ssm_b_im, m_ssm_c_re=m_ssm_c_re, m_ssm_c_im=m_ssm_c_im, m_ssm_d=m_ssm_d, m_ssm_w_glu=m_ssm_w_glu, m_even_w_out=m_even_w_out, m_odd_norm=m_odd_norm, m_odd_w_in=m_odd_w_in, m_conv_w=m_conv_w, m_conv_b=m_conv_b, m_conv_ln_g=m_conv_ln_g, m_conv_ln_b=m_conv_ln_b, m_odd_w_out=m_odd_w_out, m_final_norm=m_final_norm, v_even_norm=v_even_norm, v_even_w_in=v_even_w_in, v_pool_w=v_pool_w, v_pool_scale=v_pool_scale, v_ssm_log_dt=v_ssm_log_dt, v_ssm_a_re=v_ssm_a_re, v_ssm_a_im=v_ssm_a_im, v_ssm_b_re=v_ssm_b_re, v_ssm_b_im=v_ssm_b_im, v_ssm_c_re=v_ssm_c_re, v_ssm_c_im=v_ssm_c_im, v_ssm_d=v_ssm_d, v_ssm_w_glu=v_ssm_w_glu, v_even_w_out=v_even_w_out, v_odd_norm=v_odd_norm, v_odd_w_in=v_odd_w_in, v_conv_w=v_conv_w, v_conv_b=v_conv_b, v_conv_ln_g=v_conv_ln_g, v_conv_ln_b=v_conv_ln_b, v_odd_w_out=v_odd_w_out, v_final_norm=v_final_norm)
    weights = {n: given[n] for n in TWIN_WEIGHTS}
    shared = {n: given[n] for n in SHARED_INPUTS}
    per_example = {n: given[n] for n in ['x']}
    grad_fn = _jax.value_and_grad(_loss, argnums=(0, 1))

    def one_microbatch(ex, loss_target):
        ex = dict(ex)
        diff = ex.pop(TWIN_DIFF_INPUT)
        return grad_fn(weights, diff, {**shared, **ex}, loss_target)

    if N_MICROBATCH == 1:
        loss, (grad_w, grad_x) = one_microbatch(per_example, given["loss_target"])
    else:
        def body(carry, xs):
            loss_sum, grad_sum = carry
            l_k, (gw_k, gx_k) = one_microbatch(xs[0], xs[1])
            with _jax.named_scope("update"):
                return (loss_sum + l_k, _jax.tree.map(_jnp.add, grad_sum, gw_k)), gx_k

        init = (_jnp.zeros((), _jnp.float32), _jax.tree.map(_jnp.zeros_like, weights))
        (loss, grad_w), grad_x = _jax.lax.scan(body, init, (per_example, given["loss_target"]))
    with _jax.named_scope("update"):
        delta_w, new_m, new_v = {}, {}, {}
        for n in TWIN_WEIGHTS:
            delta_w[n], new_m[n], new_v[n] = _adamw(weights[n], grad_w[n], given["m_" + n], given["v_" + n])
    return (loss, grad_x, *[grad_w[n] for n in TWIN_WEIGHTS], *[delta_w[n] for n in TWIN_WEIGHTS],
            *[new_m[n] for n in TWIN_WEIGHTS], *[new_v[n] for n in TWIN_WEIGHTS])
```

```python
import functools

import jax
import jax.numpy as jnp
from jax import lax
from jax.experimental import pallas as pl
from jax.experimental.pallas import tpu as pltpu

F32 = jnp.float32
BF16 = jnp.bfloat16
MESH = pl.DeviceIdType.MESH

D_MODEL = 1024
RMS_EPS = 1e-6
LN_EPS = 1e-5
N_GROUPS = 32
GROUP_DIM = 16
N_STATE = 64
STATES = N_GROUPS * N_STATE
SSM_W = 512
POOL_W = 512
CONV_K = 31
HALO = 32
POOL_HALO = 16

ADAM_LR = 0.001
ADAM_B1 = 0.9
ADAM_B2 = 0.999
ADAM_EPS = 1e-08
ADAM_WD = 0.01
ADAM_STEP = 10

TM = 256
SUB = 8
LCH = 512
VMEM_LIMIT = 56 * 1024 * 1024

ROWS_BIG = 1920
ROWS_SMALL = 16
ROWS_PACK = 2048
REP_ROWS = 200


def _params(n_axes=1):
    return pltpu.CompilerParams(dimension_semantics=("arbitrary",) * n_axes, vmem_limit_bytes=VMEM_LIMIT)


def _rows(w, cb=0, rev=None, tm=TM):
    if rev is None:
        return pl.BlockSpec((tm, w), lambda i: (i, cb))
    return pl.BlockSpec((tm, w), lambda i: (rev - 1 - i, cb))


def _full(shape):
    n = len(shape)
    return pl.BlockSpec(shape, lambda i: (0,) * n)


def _prev(hr, w, cb=0, tm=TM):
    r = tm // hr
    return pl.BlockSpec((hr, w), lambda i: (jnp.maximum(i * r - 1, 0), cb))


def _next(hr, w, nrows, cb=0, tm=TM):
    r = tm // hr
    last = nrows // hr - 1
    return pl.BlockSpec((hr, w), lambda i: (jnp.minimum((i + 1) * r, last), cb))


def _dot(a, b):
    return jnp.dot(a, b, preferred_element_type=F32)


def _dot_nt(a, b):
    return lax.dot_general(a, b, (((1,), (1,)), ((), ())), preferred_element_type=F32)


def _dot_tn(a, b):
    return lax.dot_general(a, b, (((0,), (0,)), ((), ())), preferred_element_type=F32)


def _sig(x):
    return 1.0 / (1.0 + jnp.exp(-x))


def _zero_at_first(i, *refs):
    @pl.when(i == 0)
    def _():
        for r in refs:
            r[...] = jnp.zeros_like(r)


def _norm_in(x, g, w, name):
    t, n = x.shape[0], w.shape[1]

    def body(x_ref, g_ref, w_ref, o_ref):
        xv = x_ref[...]
        r = lax.rsqrt(jnp.mean(xv * xv, axis=-1, keepdims=True) + RMS_EPS)
        o_ref[...] = _dot((xv * r * g_ref[...]).astype(BF16), w_ref[...])

    return pl.pallas_call(
        body, name=name, grid=(t // TM,),
        in_specs=[_rows(D_MODEL), _full((1, D_MODEL)), _full(w.shape)],
        out_specs=_rows(n), out_shape=jax.ShapeDtypeStruct((t, n), F32),
        compiler_params=_params())(x, g, w)


def _pool_sums(ext, g, forward):
    n = ext.shape[0]
    s = ext
    for step in range(g + 1):
        k = 1 << step
        s = s + pltpu.roll(s, k if forward else n - k, 0)
    return s


def _pool_fwd(proj, wp, ps):
    t = proj.shape[0]

    def body(u_ref, h_ref, wp_ref, ps_ref, y_ref):
        i = pl.program_id(0)
        pos = (i * TM + 1 + lax.broadcasted_iota(jnp.int32, (TM, 1), 0)).astype(F32)
        for g in range(4):
            sl = slice(128 * g, 128 * (g + 1))
            u = u_ref[:, sl]
            halo = jnp.where(i == 0, 0.0, h_ref[:, sl])
            s = _pool_sums(jnp.concatenate([halo, u], axis=0), g, True)[POOL_HALO:, :]
            pooled = s / jnp.minimum(pos, float(2 << g)) - u
            y_ref[:, sl] = _dot(pooled.astype(BF16), wp_ref[g]) * ps_ref[:, sl]

    return pl.pallas_call(
        body, name="pool_fwd", grid=(t // TM,),
        in_specs=[_rows(POOL_W, 0), _prev(POOL_HALO, POOL_W, 0), _full((4, 128, 128)), _full((1, POOL_W))],
        out_specs=_rows(POOL_W), out_shape=jax.ShapeDtypeStruct((t, POOL_W), F32),
        compiler_params=_params())(proj, proj, wp, ps)


def _scan_fwd_block(xs_re, xs_im, xp_re, xp_im, pw_re, pw_im, car_re, car_im):
    row = lax.broadcasted_iota(jnp.int32, (SUB, LCH), 0)
    for j in range(STATES // LCH):
        sl = slice(LCH * j, LCH * (j + 1))
        pre, pim = pw_re[:, sl], pw_im[:, sl]

        def sub(k, carry, sl=sl, pre=pre, pim=pim):
            cre, cim = carry
            r0 = pl.multiple_of(k * SUB, SUB)
            vre = xs_re[pl.ds(r0, SUB), sl]
            vim = xs_im[pl.ds(r0, SUB), sl]
            for sh in (1, 2, 4):
                are, aim = pre[sh - 1:sh, :], pim[sh - 1:sh, :]
                sre = jnp.where(row >= sh, pltpu.roll(vre, sh, 0), 0.0)
                sim = jnp.where(row >= sh, pltpu.roll(vim, sh, 0), 0.0)
                vre, vim = vre + are * sre - aim * sim, vim + are * sim + aim * sre
            vre, vim = vre + pre * cre - pim * cim, vim + pre * cim + pim * cre
            xs_re[pl.ds(r0, SUB), sl] = vre
            xs_im[pl.ds(r0, SUB), sl] = vim
            if xp_re is not None:
                xp_re[pl.ds(r0, SUB), sl] = jnp.where(row >= 1, pltpu.roll(vre, 1, 0), cre)
                xp_im[pl.ds(r0, SUB), sl] = jnp.where(row >= 1, pltpu.roll(vim, 1, 0), cim)
            return (jnp.broadcast_to(vre[SUB - 1:SUB, :], (SUB, LCH)),
                    jnp.broadcast_to(vim[SUB - 1:SUB, :], (SUB, LCH)))

        cre, cim = lax.fori_loop(0, TM // SUB, sub, (car_re[:, sl], car_im[:, sl]), unroll=2)
        car_re[:, sl] = cre
        car_im[:, sl] = cim


def _ssm_fwd(proj, mb_re, mb_im, pw_re, pw_im, cm_re, cm_im, dskip, wglu):
    t = proj.shape[0]
    nblk = t // TM

    def body(u_ref, mbre, mbim, pwre, pwim, cmre, cmim, d_ref, wg_ref,
             y_ref, cre_ref, cim_ref, xs_re, xs_im, car_re, car_im, ysk):
        i = pl.program_id(0)
        _zero_at_first(i, car_re, car_im)
        cre_ref[0] = car_re[...]
        cim_ref[0] = car_im[...]
        us = u_ref[...]
        usb = us.astype(BF16)
        for j in range(4):
            xs_re[:, LCH * j:LCH * (j + 1)] = _dot(usb[:, 128 * j:128 * (j + 1)], mbre[j])
            xs_im[:, LCH * j:LCH * (j + 1)] = _dot(usb[:, 128 * j:128 * (j + 1)], mbim[j])
        _scan_fwd_block(xs_re, xs_im, None, None, pwre, pwim, car_re, car_im)
        for j in range(4):
            sl = slice(LCH * j, LCH * (j + 1))
            ysk[:, 128 * j:128 * (j + 1)] = (_dot_nt(xs_re[:, sl].astype(BF16), cmre[j])
                                             - _dot_nt(xs_im[:, sl].astype(BF16), cmim[j]))
        yv = ysk[...] + d_ref[...] * us
        gv = _dot(yv.astype(BF16), wg_ref[...])
        y_ref[...] = gv[:, :SSM_W] * _sig(gv[:, SSM_W:])

    blk = (4, 128, LCH)
    return pl.pallas_call(
        body, name="ssm_fwd", grid=(nblk,),
        in_specs=[_rows(SSM_W, 1), _full(blk), _full(blk), _full((SUB, STATES)), _full((SUB, STATES)),
                  _full(blk), _full(blk), _full((1, SSM_W)), _full((SSM_W, 2 * SSM_W))],
        out_specs=[_rows(SSM_W), pl.BlockSpec((1, SUB, STATES), lambda i: (i, 0, 0)),
                   pl.BlockSpec((1, SUB, STATES), lambda i: (i, 0, 0))],
        out_shape=[jax.ShapeDtypeStruct((t, SSM_W), F32), jax.ShapeDtypeStruct((nblk, SUB, STATES), F32),
                   jax.ShapeDtypeStruct((nblk, SUB, STATES), F32)],
        scratch_shapes=[pltpu.VMEM((TM, STATES), F32), pltpu.VMEM((TM, STATES), F32),
                        pltpu.VMEM((SUB, STATES), F32), pltpu.VMEM((SUB, STATES), F32),
                        pltpu.VMEM((TM, SSM_W), F32)],
        compiler_params=_params())(proj, mb_re, mb_im, pw_re, pw_im, cm_re, cm_im, dskip, wglu)


def _even_out(yp, ys, proj, x, w):
    t = x.shape[0]

    def body(yp_ref, ys_ref, z_ref, x_ref, w_ref, x1_ref, yg_ref):
        z = z_ref[...]
        sz = z * _sig(z)
        gp = (yp_ref[...] * sz[:, :POOL_W]).astype(BF16)
        gs = (ys_ref[...] * sz[:, POOL_W:]).astype(BF16)
        yg_ref[:, :POOL_W] = gp
        yg_ref[:, POOL_W:] = gs
        x1_ref[...] = x_ref[...] + _dot(gp, w_ref[:POOL_W, :]) + _dot(gs, w_ref[POOL_W:, :])

    return pl.pallas_call(
        body, name="even_out", grid=(t // TM,),
        in_specs=[_rows(POOL_W), _rows(SSM_W), _rows(D_MODEL, 1), _rows(D_MODEL), _full((D_MODEL, D_MODEL))],
        out_specs=[_rows(D_MODEL), _rows(D_MODEL)],
        out_shape=[jax.ShapeDtypeStruct((t, D_MODEL), F32), jax.ShapeDtypeStruct((t, D_MODEL), BF16)],
        compiler_params=_params())(yp, ys, proj, x, w)


def _conv_taps(ext, w_ref, first, nrows, sl, init):
    acc = init
    for k in range(CONV_K):
        acc = acc + w_ref[k:k + 1, sl] * ext[pl.ds(first(k), nrows), sl]
    return acc


def _conv_fwd(q, cw, cb, lg, lb):
    t = q.shape[0]

    def body(v_ref, g_ref, hv_ref, hg_ref, z_ref, w_ref, b_ref, lg_ref, lb_ref, y_ref, cv_ref, ext):
        i = pl.program_id(0)
        ext[0:HALO, :] = jnp.where(i == 0, 0.0, hv_ref[...] * _sig(hg_ref[...]))
        ext[HALO:, :] = v_ref[...] * _sig(g_ref[...])
        for c in range(D_MODEL // 128):
            sl = slice(128 * c, 128 * (c + 1))
            cv_ref[:, sl] = _conv_taps(ext, w_ref, lambda k: k + 2, TM, sl,
                                       jnp.broadcast_to(b_ref[:, sl], (TM, 128)))
        cv = cv_ref[...]
        cc = cv - jnp.mean(cv, axis=-1, keepdims=True)
        rstd = lax.rsqrt(jnp.mean(cc * cc, axis=-1, keepdims=True) + LN_EPS)
        cl = cc * rstd * lg_ref[...] + lb_ref[...]
        z = z_ref[...]
        y_ref[...] = (cl * _sig(cl) * z * _sig(z)).astype(BF16)

    vec = _full((1, D_MODEL))
    return pl.pallas_call(
        body, name="conv_fwd", grid=(t // TM,),
        in_specs=[_rows(D_MODEL, 0), _rows(D_MODEL, 1), _prev(HALO, D_MODEL, 0), _prev(HALO, D_MODEL, 1),
                  _rows(D_MODEL, 2), _full((HALO, D_MODEL)), vec, vec, vec],
        out_specs=[_rows(D_MODEL), _rows(D_MODEL)],
        out_shape=[jax.ShapeDtypeStruct((t, D_MODEL), BF16), jax.ShapeDtypeStruct((t, D_MODEL), F32)],
        scratch_shapes=[pltpu.VMEM((TM + HALO, D_MODEL), F32)],
        compiler_params=_params())(q, q, q, q, q, cw, cb, lg, lb)


def _odd_out_loss(y2, x1, w, gf, tgt):
    t = x1.shape[0]

    def body(y_ref, x_ref, w_ref, g_ref, t_ref, dx_ref, loss_ref, dg_ref):
        i = pl.program_id(0)
        _zero_at_first(i, loss_ref, dg_ref)
        x2 = x_ref[...] + _dot(y_ref[...], w_ref[...])
        r = lax.rsqrt(jnp.mean(x2 * x2, axis=-1, keepdims=True) + RMS_EPS)
        n = x2 * r
        e = n * g_ref[...] - t_ref[...]
        loss_ref[...] += jnp.sum(e * e, axis=0, keepdims=True) * (0.5 / D_MODEL)
        dout = e * (1.0 / D_MODEL)
        dg_ref[...] += jnp.sum(dout * n, axis=0, keepdims=True)
        dn = dout * g_ref[...]
        dx_ref[...] = r * (dn - n * jnp.mean(dn * n, axis=-1, keepdims=True))

    vec = _full((1, D_MODEL))
    return pl.pallas_call(
        body, name="odd_out_loss", grid=(t // TM,),
        in_specs=[_rows(D_MODEL), _rows(D_MODEL), _full((D_MODEL, D_MODEL)), vec, _rows(D_MODEL)],
        out_specs=[_rows(D_MODEL), vec, vec],
        out_shape=[jax.ShapeDtypeStruct((t, D_MODEL), F32), jax.ShapeDtypeStruct((1, D_MODEL), F32),
                   jax.ShapeDtypeStruct((1, D_MODEL), F32)],
        compiler_params=_params())(y2, x1, w, gf, tgt)


def _dsilu(z):
    s = _sig(z)
    return z * s, s * (1.0 + z * (1.0 - s))


def _odd_bwd_out(dx2, w, y2, cv, q, lg, lb):
    t = dx2.shape[0]

    def body(dx_ref, w_ref, y_ref, cv_ref, z_ref, lg_ref, lb_ref, dcv_ref, dz_ref, dw_ref, dlg_ref, dlb_ref):
        i = pl.program_id(0)
        _zero_at_first(i, dw_ref, dlg_ref, dlb_ref)
        dxb = dx_ref[...].astype(BF16)
        dy = _dot_nt(dxb, w_ref[...])
        dw_ref[...] += _dot_tn(y_ref[...], dxb)
        cv = cv_ref[...]
        cc = cv - jnp.mean(cv, axis=-1, keepdims=True)
        rstd = lax.rsqrt(jnp.mean(cc * cc, axis=-1, keepdims=True) + LN_EPS)
        cn = cc * rstd
        silu_c, dsilu_c = _dsilu(cn * lg_ref[...] + lb_ref[...])
        silu_z, dsilu_z = _dsilu(z_ref[...])
        dcl = dy * silu_z * dsilu_c
        dz_ref[...] = dy * silu_c * dsilu_z
        dlg_ref[...] += jnp.sum(dcl * cn, axis=0, keepdims=True)
        dlb_ref[...] += jnp.sum(dcl, axis=0, keepdims=True)
        dcn = dcl * lg_ref[...]
        dcv_ref[...] = rstd * (dcn - jnp.mean(dcn, axis=-1, keepdims=True)
                               - cn * jnp.mean(dcn * cn, axis=-1, keepdims=True))

    vec = _full((1, D_MODEL))
    mat = _full((D_MODEL, D_MODEL))
    return pl.pallas_call(
        body, name="odd_bwd_out", grid=(t // TM,),
        in_specs=[_rows(D_MODEL), mat, _rows(D_MODEL), _rows(D_MODEL), _rows(D_MODEL, 2), vec, vec],
        out_specs=[_rows(D_MODEL), _rows(D_MODEL), mat, vec, vec],
        out_shape=[jax.ShapeDtypeStruct((t, D_MODEL), F32), jax.ShapeDtypeStruct((t, D_MODEL), F32),
                   jax.ShapeDtypeStruct((D_MODEL, D_MODEL), F32), jax.ShapeDtypeStruct((1, D_MODEL), F32),
                   jax.ShapeDtypeStruct((1, D_MODEL), F32)],
        compiler_params=_params())(dx2, w, y2, cv, q, lg, lb)


def _conv_bwd(dcv, q, cw):
    t = dcv.shape[0]

    def body(d_ref, dn_ref, v_ref, g_ref, hv_ref, hg_ref, w_ref,
             dv_ref, dgt_ref, dw_ref, db_ref, gext, dext, dgl):
        i = pl.program_id(0)
        last = pl.num_programs(0) - 1
        _zero_at_first(i, dw_ref, db_ref)
        v = v_ref[...]
        sg = _sig(g_ref[...])
        gext[0:HALO, :] = jnp.where(i == 0, 0.0, hv_ref[...] * _sig(hg_ref[...]))
        gext[HALO:, :] = v * sg
        d = d_ref[...]
        dext[0:TM, :] = d
        dext[TM:, :] = jnp.where(i == last, 0.0, dn_ref[...])
        db_ref[...] += jnp.sum(d, axis=0, keepdims=True)
        for c in range(D_MODEL // 128):
            sl = slice(128 * c, 128 * (c + 1))
            dgl[:, sl] = _conv_taps(dext, w_ref, lambda k: 30 - k, TM, sl, jnp.zeros((TM, 128), F32))
            dc = d[:, sl]
            for k in range(CONV_K):
                dw_ref[k:k + 1, sl] += jnp.sum(dc * gext[pl.ds(k + 2, TM), sl], axis=0, keepdims=True)
        dg = dgl[...]
        dv_ref[...] = dg * sg
        dgt_ref[...] = dg * v * sg * (1.0 - sg)

    return pl.pallas_call(
        body, name="conv_bwd", grid=(t // TM,),
        in_specs=[_rows(D_MODEL), _next(HALO, D_MODEL, t), _rows(D_MODEL, 0), _rows(D_MODEL, 1),
                  _prev(HALO, D_MODEL, 0), _prev(HALO, D_MODEL, 1), _full((HALO, D_MODEL))],
        out_specs=[_rows(D_MODEL), _rows(D_MODEL), _full((HALO, D_MODEL)), _full((1, D_MODEL))],
        out_shape=[jax.ShapeDtypeStruct((t, D_MODEL), F32), jax.ShapeDtypeStruct((t, D_MODEL), F32),
                   jax.ShapeDtypeStruct((HALO, D_MODEL), F32), jax.ShapeDtypeStruct((1, D_MODEL), F32)],
        scratch_shapes=[pltpu.VMEM((TM + HALO, D_MODEL), F32), pltpu.VMEM((TM + HALO, D_MODEL), F32),
                        pltpu.VMEM((TM, D_MODEL), F32)],
        compiler_params=_params())(dcv, dcv, q, q, q, q, cw)


def _in_bwd(dparts, w, x, g, dres, name):
    t = x.shape[0]
    widths = [p.shape[1] for p in dparts]
    npart = len(dparts)

    def body(*refs):
        d_refs = refs[:npart]
        w_ref, x_ref, g_ref, r_ref, dx_ref, dg_ref = refs[npart:]
        i = pl.program_id(0)
        _zero_at_first(i, dg_ref)
        off = 0
        dh = None
        for d_ref, wd in zip(d_refs, widths):
            part = _dot_nt(d_ref[...].astype(BF16), w_ref[:, off:off + wd])
            dh = part if dh is None else dh + part
            off += wd
        xv = x_ref[...]
        r = lax.rsqrt(jnp.mean(xv * xv, axis=-1, keepdims=True) + RMS_EPS)
        n = xv * r
        dg_ref[...] += jnp.sum(dh * n, axis=0, keepdims=True)
        dn = dh * g_ref[...]
        dx_ref[...] = r_ref[...] + r * (dn - n * jnp.mean(dn * n, axis=-1, keepdims=True))

    vec = _full((1, D_MODEL))
    return pl.pallas_call(
        body, name=name, grid=(t // TM,),
        in_specs=[_rows(wd) for wd in widths] + [_full(w.shape), _rows(D_MODEL), vec, _rows(D_MODEL)],
        out_specs=[_rows(D_MODEL), vec],
        out_shape=[jax.ShapeDtypeStruct((t, D_MODEL), F32), jax.ShapeDtypeStruct((1, D_MODEL), F32)],
        compiler_params=_params())(*dparts, w, x, g, dres)


def _in_wgrad(x, g, dparts, name):
    t = x.shape[0]
    widths = [p.shape[1] for p in dparts]
    n = sum(widths)
    npart = len(dparts)

    def body(*refs):
        x_ref, g_ref = refs[:2]
        d_refs = refs[2:2 + npart]
        dw_ref = refs[2 + npart]
        i = pl.program_id(0)
        _zero_at_first(i, dw_ref)
        xv = x_ref[...]
        r = lax.rsqrt(jnp.mean(xv * xv, axis=-1, keepdims=True) + RMS_EPS)
        h = (xv * r * g_ref[...]).astype(BF16)
        off = 0
        for d_ref, wd in zip(d_refs, widths):
            dw_ref[:, off:off + wd] += _dot_tn(h, d_ref[...].astype(BF16))
            off += wd

    return pl.pallas_call(
        body, name=name, grid=(t // TM,),
        in_specs=[_rows(D_MODEL), _full((1, D_MODEL))] + [_rows(wd) for wd in widths],
        out_specs=_full((D_MODEL, n)), out_shape=jax.ShapeDtypeStruct((D_MODEL, n), F32),
        compiler_params=_params())(x, g, *dparts)


def _even_bwd_out(dx1, w, yg, yp, ys, proj):
    t = dx1.shape[0]

    def body(dx_ref, w_ref, yg_ref, yp_ref, ys_ref, z_ref, dy_ref, dz_ref, dw_ref):
        i = pl.program_id(0)
        _zero_at_first(i, dw_ref)
        dxb = dx_ref[...].astype(BF16)
        dyg = _dot_nt(dxb, w_ref[...])
        dw_ref[...] += _dot_tn(yg_ref[...], dxb)
        silu_z, dsilu_z = _dsilu(z_ref[...])
        dy_ref[...] = dyg * silu_z
        dz_ref[:, :POOL_W] = dyg[:, :POOL_W] * yp_ref[...] * dsilu_z[:, :POOL_W]
        dz_ref[:, POOL_W:] = dyg[:, POOL_W:] * ys_ref[...] * dsilu_z[:, POOL_W:]

    mat = _full((D_MODEL, D_MODEL))
    return pl.pallas_call(
        body, name="even_bwd_out", grid=(t // TM,),
        in_specs=[_rows(D_MODEL), mat, _rows(D_MODEL), _rows(POOL_W), _rows(SSM_W), _rows(D_MODEL, 1)],
        out_specs=[_rows(D_MODEL), _rows(D_MODEL), mat],
        out_shape=[jax.ShapeDtypeStruct((t, D_MODEL), F32), jax.ShapeDtypeStruct((t, D_MODEL), F32),
                   jax.ShapeDtypeStruct((D_MODEL, D_MODEL), F32)],
        compiler_params=_params())(dx1, w, yg, yp, ys, proj)


def _pool_bwd(dycat, proj, wp, ps):
    t = proj.shape[0]

    def body(dy_ref, dyn_ref, u_ref, h_ref, wp_ref, ps_ref, du_ref, dwp_ref, dps_ref):
        i = pl.program_id(0)
        last = pl.num_programs(0) - 1
        _zero_at_first(i, dwp_ref, dps_ref)
        pos = (i * TM + 1 + lax.broadcasted_iota(jnp.int32, (TM, 1), 0)).astype(F32)
        pos_ext = (i * TM + 1 + lax.broadcasted_iota(jnp.int32, (TM + POOL_HALO, 1), 0)).astype(F32)
        for g in range(4):
            sl = slice(128 * g, 128 * (g + 1))
            w = float(2 << g)
            u = u_ref[:, sl]
            halo = jnp.where(i == 0, 0.0, h_ref[:, sl])
            s = _pool_sums(jnp.concatenate([halo, u], axis=0), g, True)[POOL_HALO:, :]
            pooled = (s / jnp.minimum(pos, w) - u).astype(BF16)
            dy = dy_ref[:, sl]
            dps_ref[:, sl] += jnp.sum(dy * _dot(pooled, wp_ref[g]), axis=0, keepdims=True)
            dy_ext = jnp.concatenate([dy, jnp.where(i == last, 0.0, dyn_ref[:, sl])], axis=0)
            dmix = (dy_ext * ps_ref[:, sl]).astype(BF16)
            dwp_ref[g] += _dot_tn(pooled, dmix[:TM, :])
            dpool = _dot_nt(dmix, wp_ref[g])
            lead = _pool_sums(dpool / jnp.minimum(pos_ext, w), g, False)
            du_ref[:, sl] = lead[:TM, :] - dpool[:TM, :]

    return pl.pallas_call(
        body, name="pool_bwd", grid=(t // TM,),
        in_specs=[_rows(POOL_W, 0), _next(POOL_HALO, POOL_W, t, 0), _rows(POOL_W, 0), _prev(POOL_HALO, POOL_W, 0),
                  _full((4, 128, 128)), _full((1, POOL_W))],
        out_specs=[_rows(POOL_W), _full((4, 128, 128)), _full((1, POOL_W))],
        out_shape=[jax.ShapeDtypeStruct((t, POOL_W), F32), jax.ShapeDtypeStruct((4, 128, 128), F32),
                   jax.ShapeDtypeStruct((1, POOL_W), F32)],
        compiler_params=_params())(dycat, dycat, proj, proj, wp, ps)


def _ssm_bwd(dycat, proj, car_in_re, car_in_im, mb_re, mb_im, pw_re, pw_im, pr_re, pr_im,
             cm_re, cm_im, dskip, wglu):
    t = proj.shape[0]
    nblk = t // TM

    def body(dy_ref, u_ref, cin_re, cin_im, mbre, mbim, pwre, pwim, prre, prim, cmre, cmim, d_ref, wg_ref,
             du_ref, dmbre, dmbim, dcmre, dcmim, dare, daim, dd_ref, dwg_ref,
             xs_re, xs_im, xp_re, xp_im, gs_re, gs_im, car_re, car_im, gcar_re, gcar_im, ysk, dysk):
        i = pl.program_id(0)
        _zero_at_first(i, dmbre, dmbim, dcmre, dcmim, dare, daim, dd_ref, dwg_ref, gcar_re, gcar_im)
        us = u_ref[...]
        usb = us.astype(BF16)
        for j in range(4):
            xs_re[:, LCH * j:LCH * (j + 1)] = _dot(usb[:, 128 * j:128 * (j + 1)], mbre[j])
            xs_im[:, LCH * j:LCH * (j + 1)] = _dot(usb[:, 128 * j:128 * (j + 1)], mbim[j])
        car_re[...] = cin_re[0]
        car_im[...] = cin_im[0]
        _scan_fwd_block(xs_re, xs_im, xp_re, xp_im, pwre, pwim, car_re, car_im)
        for j in range(4):
            sl = slice(LCH * j, LCH * (j + 1))
            ysk[:, 128 * j:128 * (j + 1)] = (_dot_nt(xs_re[:, sl].astype(BF16), cmre[j])
                                             - _dot_nt(xs_im[:, sl].astype(BF16), cmim[j]))
        yvb = (ysk[...] + d_ref[...] * us).astype(BF16)
        gv = _dot(yvb, wg_ref[...])
        sg = _sig(gv[:, SSM_W:])
        dyss = dy_ref[...]
        dval = (dyss * sg).astype(BF16)
        dgate = (dyss * gv[:, :SSM_W] * sg * (1.0 - sg)).astype(BF16)
        dy = _dot_nt(dval, wg_ref[:, :SSM_W]) + _dot_nt(dgate, wg_ref[:, SSM_W:])
        dwg_ref[:, :SSM_W] += _dot_tn(yvb, dval)
        dwg_ref[:, SSM_W:] += _dot_tn(yvb, dgate)
        dd_ref[...] += jnp.sum(dy * us, axis=0, keepdims=True)
        dysk[...] = dy
        for j in range(4):
            sl = slice(LCH * j, LCH * (j + 1))
            dyj = dy[:, 128 * j:128 * (j + 1)].astype(BF16)
            gs_re[:, sl] = _dot(dyj, cmre[j])
            gs_im[:, sl] = -_dot(dyj, cmim[j])
            dcmre[j] += _dot_tn(dyj, xs_re[:, sl].astype(BF16))
            dcmim[j] -= _dot_tn(dyj, xs_im[:, sl].astype(BF16))
        row = lax.broadcasted_iota(jnp.int32, (SUB, LCH), 0)
        nsub = TM // SUB
        for j in range(STATES // LCH):
            sl = slice(LCH * j, LCH * (j + 1))
            pre, pim = prre[:, sl], prim[:, sl]

            def sub(k, carry, sl=sl, pre=pre, pim=pim):
                cre, cim, are_acc, aim_acc = carry
                r0 = pl.multiple_of((nsub - 1 - k) * SUB, SUB)
                vre = gs_re[pl.ds(r0, SUB), sl]
                vim = gs_im[pl.ds(r0, SUB), sl]
                for sh in (1, 2, 4):
                    are, aim = pre[SUB - sh:SUB - sh + 1, :], pim[SUB - sh:SUB - sh + 1, :]
                    sre = jnp.where(row < SUB - sh, pltpu.roll(vre, SUB - sh, 0), 0.0)
                    sim = jnp.where(row < SUB - sh, pltpu.roll(vim, SUB - sh, 0), 0.0)
                    vre, vim = vre + are * sre - aim * sim, vim + are * sim + aim * sre
                vre, vim = vre + pre * cre - pim * cim, vim + pre * cim + pim * cre
                gs_re[pl.ds(r0, SUB), sl] = vre
                gs_im[pl.ds(r0, SUB), sl] = vim
                qre = xp_re[pl.ds(r0, SUB), sl]
                qim = xp_im[pl.ds(r0, SUB), sl]
                return (jnp.broadcast_to(vre[0:1, :], (SUB, LCH)), jnp.broadcast_to(vim[0:1, :], (SUB, LCH)),
                        are_acc + vre * qre + vim * qim, aim_acc + vim * qre - vre * qim)

            cre, cim, are_acc, aim_acc = lax.fori_loop(
                0, nsub, sub, (gcar_re[:, sl], gcar_im[:, sl], dare[:, sl], daim[:, sl]), unroll=2)
            gcar_re[:, sl] = cre
            gcar_im[:, sl] = cim
            dare[:, sl] = are_acc
            daim[:, sl] = aim_acc
        for j in range(4):
            sl = slice(LCH * j, LCH * (j + 1))
            c4 = slice(128 * j, 128 * (j + 1))
            gre = gs_re[:, sl].astype(BF16)
            gim = gs_im[:, sl].astype(BF16)
            dmbre[j] += _dot_tn(usb[:, c4], gre)
            dmbim[j] += _dot_tn(usb[:, c4], gim)
            du_ref[:, c4] = _dot_nt(gre, mbre[j]) + _dot_nt(gim, mbim[j]) + dysk[:, c4] * d_ref[:, c4]

    blk = (4, 128, LCH)
    pw = _full((SUB, STATES))
    car = pl.BlockSpec((1, SUB, STATES), lambda i: (nblk - 1 - i, 0, 0))
    big = lambda: pltpu.VMEM((TM, STATES), F32)
    small = lambda: pltpu.VMEM((SUB, STATES), F32)
    return pl.pallas_call(
        body, name="ssm_bwd", grid=(nblk,),
        in_specs=[_rows(SSM_W, 1, rev=nblk), _rows(SSM_W, 1, rev=nblk), car, car, _full(blk), _full(blk),
                  pw, pw, pw, pw, _full(blk), _full(blk), _full((1, SSM_W)), _full((SSM_W, 2 * SSM_W))],
        out_specs=[_rows(SSM_W, 0, rev=nblk), _full(blk), _full(blk), _full(blk), _full(blk), pw, pw,
                   _full((1, SSM_W)), _full((SSM_W, 2 * SSM_W))],
        out_shape=[jax.ShapeDtypeStruct((t, SSM_W), F32)] + [jax.ShapeDtypeStruct(blk, F32)] * 4
        + [jax.ShapeDtypeStruct((SUB, STATES), F32)] * 2
        + [jax.ShapeDtypeStruct((1, SSM_W), F32), jax.ShapeDtypeStruct((SSM_W, 2 * SSM_W), F32)],
        scratch_shapes=[big(), big(), big(), big(), big(), big(), small(), small(), small(), small(),
                        pltpu.VMEM((TM, SSM_W), F32), pltpu.VMEM((TM, SSM_W), F32)],
        compiler_params=_params())(dycat, proj, car_in_re, car_in_im, mb_re, mb_im, pw_re, pw_im, pr_re, pr_im,
                                   cm_re, cm_im, dskip, wglu)


def _adamw(w, g, m, v, name):
    rows = w.shape[0]
    tr = 256 if rows % 256 == 0 else rows
    c1 = 1.0 / (1.0 - ADAM_B1 ** ADAM_STEP)
    c2 = 1.0 / (1.0 - ADAM_B2 ** ADAM_STEP)

    def body(w_ref, g_ref, m_ref, v_ref, d_ref, nm_ref, nv_ref):
        gv = g_ref[...]
        m = ADAM_B1 * m_ref[...] + (1.0 - ADAM_B1) * gv
        v = ADAM_B2 * v_ref[...] + (1.0 - ADAM_B2) * (gv * gv)
        nm_ref[...] = m
        nv_ref[...] = v
        d_ref[...] = -ADAM_LR * ((m * c1) / (jnp.sqrt(v * c2) + ADAM_EPS) + ADAM_WD * w_ref[...])

    spec = pl.BlockSpec((tr, D_MODEL), lambda i: (i, 0))
    shp = jax.ShapeDtypeStruct((rows, D_MODEL), F32)
    return pl.pallas_call(
        body, name=name, grid=(rows // tr,), in_specs=[spec] * 4, out_specs=[spec] * 3, out_shape=[shp] * 3,
        compiler_params=_params())(w, g, m, v)


def _add2(a, b, name):
    n, rows, _ = a.shape
    spec = pl.BlockSpec((1, 256, D_MODEL), lambda i, j: (i, j, 0))

    def body(a_ref, b_ref, o_ref):
        o_ref[...] = a_ref[...] + b_ref[...]

    return pl.pallas_call(
        body, name=name, grid=(n, rows // 256), in_specs=[spec, spec], out_specs=spec,
        out_shape=jax.ShapeDtypeStruct(a.shape, F32), compiler_params=_params(2))(a, b)


def _sum_lead(a, name):
    n, rows, _ = a.shape
    tr = 256 if rows % 256 == 0 else rows

    def body(a_ref, o_ref):
        acc = a_ref[0]
        for k in range(1, n):
            acc = acc + a_ref[k]
        o_ref[...] = acc

    return pl.pallas_call(
        body, name=name, grid=(rows // tr,),
        in_specs=[pl.BlockSpec((n, tr, D_MODEL), lambda i: (0, i, 0))],
        out_specs=pl.BlockSpec((tr, D_MODEL), lambda i: (i, 0)),
        out_shape=jax.ShapeDtypeStruct((rows, D_MODEL), F32), compiler_params=_params())(a)


ANY = pl.BlockSpec(memory_space=pl.ANY)


def _mesh_pos():
    return lax.axis_index("x"), lax.axis_index("y"), lax.axis_index("c")


def _gather_weights(wb):
    rows = wb.shape[0]
    half = rows // 2

    def body(w_ref, o_ref, send_sems, recv_sems, local_sem):
        x, y, c = _mesh_pos()
        me = 2 * x + y
        sibling = (x, y, 1 - c)
        chips = [(1 - x, y), (x, 1 - y), (1 - x, 1 - y)]

        def slab(q, h):
            return o_ref.at[q, pl.ds(h * half, half), :]

        def copy(k, q, h, to, src=None):
            return pltpu.make_async_remote_copy(
                src_ref=slab(q, h) if src is None else src, dst_ref=slab(q, h),
                send_sem=send_sems.at[k], recv_sem=recv_sems.at[k], device_id=to, device_id_type=MESH)

        mine = pltpu.make_async_copy(w_ref, o_ref.at[me], local_sem)
        mine.start()
        first = [copy(j, me, c, (*chip, c), src=w_ref.at[pl.ds(c * half, half), :])
                 for j, chip in enumerate(chips)]
        for cp in first:
            cp.start()
        passed = [copy(3 + j, 2 * chip[0] + chip[1], c, sibling) for j, chip in enumerate(chips)]
        for j, chip in enumerate(chips):
            copy(j, 2 * chip[0] + chip[1], c, (x, y, c)).wait_recv()
            passed[j].start()
        for j, chip in enumerate(chips):
            copy(3 + j, 2 * chip[0] + chip[1], 1 - c, (x, y, c)).wait_recv()
        for cp in first + passed:
            cp.wait_send()
        mine.wait()

    return pl.pallas_call(
        body, name="gather_weights", in_specs=[ANY], out_specs=ANY,
        out_shape=jax.ShapeDtypeStruct((4, rows, D_MODEL), wb.dtype),
        scratch_shapes=[pltpu.SemaphoreType.DMA((6,)), pltpu.SemaphoreType.DMA((6,)), pltpu.SemaphoreType.DMA],
    )(wb)


def _gather_all(v):
    def body(v_ref, o_ref, send_sems, recv_sems, local_sem):
        x, y, c = _mesh_pos()
        sibling = (x, y, 1 - c)
        chips = [(1 - x, y), (x, 1 - y), (1 - x, 1 - y)]

        def blk(px, py, pc):
            return o_ref.at[4 * px + 2 * py + pc]

        def copy(k, block, to, src=None):
            return pltpu.make_async_remote_copy(
                src_ref=blk(*block) if src is None else src, dst_ref=blk(*block),
                send_sem=send_sems.at[k], recv_sem=recv_sems.at[k], device_id=to, device_id_type=MESH)

        mine = pltpu.make_async_copy(v_ref, blk(x, y, c), local_sem)
        mine.start()
        first = [copy(0, (x, y, c), sibling, src=v_ref)]
        first += [copy(1 + j, (x, y, c), (*chip, c), src=v_ref) for j, chip in enumerate(chips)]
        for cp in first:
            cp.start()
        passed = [copy(4 + j, (*chip, c), sibling) for j, chip in enumerate(chips)]
        for j, chip in enumerate(chips):
            copy(1 + j, (*chip, c), (x, y, c)).wait_recv()
            passed[j].start()
        copy(0, (x, y, 1 - c), (x, y, c)).wait_recv()
        for j, chip in enumerate(chips):
            copy(4 + j, (*chip, 1 - c), (x, y, c)).wait_recv()
        for cp in first + passed:
            cp.wait_send()
        mine.wait()

    return pl.pallas_call(
        body, name="gather_all", in_specs=[ANY], out_specs=ANY,
        out_shape=jax.ShapeDtypeStruct((8,) + v.shape, v.dtype),
        scratch_shapes=[pltpu.SemaphoreType.DMA((7,)), pltpu.SemaphoreType.DMA((7,)), pltpu.SemaphoreType.DMA],
    )(v)


def _pair_split(g):
    n, rows, _ = g.shape
    half = rows // 2

    def body(g_ref, mine_ref, theirs_ref, send_sem, recv_sem, local_sem):
        x, y, c = _mesh_pos()
        keep = pltpu.make_async_copy(g_ref.at[:, pl.ds(c * half, half), :], mine_ref, local_sem)
        keep.start()
        give = pltpu.make_async_remote_copy(
            src_ref=g_ref.at[:, pl.ds((1 - c) * half, half), :], dst_ref=theirs_ref,
            send_sem=send_sem, recv_sem=recv_sem, device_id=(x, y, 1 - c), device_id_type=MESH)
        give.start()
        give.wait()
        keep.wait()

    shp = jax.ShapeDtypeStruct((n, half, D_MODEL), g.dtype)
    return pl.pallas_call(
        body, name="pair_split", in_specs=[ANY], out_specs=[ANY, ANY], out_shape=[shp, shp],
        scratch_shapes=[pltpu.SemaphoreType.DMA, pltpu.SemaphoreType.DMA, pltpu.SemaphoreType.DMA],
    )(g)


def _chip_scatter(p):
    def body(p_ref, o_ref, send_sems, recv_sems, local_sem):
        x, y, c = _mesh_pos()
        me = 2 * x + y
        chips = [(1 - x, y), (x, 1 - y), (1 - x, 1 - y)]
        keep = pltpu.make_async_copy(p_ref.at[me], o_ref.at[me], local_sem)
        keep.start()
        sends = [pltpu.make_async_remote_copy(
            src_ref=p_ref.at[2 * chip[0] + chip[1]], dst_ref=o_ref.at[me],
            send_sem=send_sems.at[j], recv_sem=recv_sems.at[j], device_id=(*chip, c), device_id_type=MESH)
            for j, chip in enumerate(chips)]
        for cp in sends:
            cp.start()
        for j, chip in enumerate(chips):
            q = 2 * chip[0] + chip[1]
            pltpu.make_async_remote_copy(
                src_ref=p_ref.at[q], dst_ref=o_ref.at[q], send_sem=send_sems.at[j], recv_sem=recv_sems.at[j],
                device_id=(*chip, c), device_id_type=MESH).wait_recv()
        for cp in sends:
            cp.wait_send()
        keep.wait()

    return pl.pallas_call(
        body, name="chip_scatter", in_specs=[ANY], out_specs=ANY, out_shape=jax.ShapeDtypeStruct(p.shape, p.dtype),
        scratch_shapes=[pltpu.SemaphoreType.DMA((3,)), pltpu.SemaphoreType.DMA((3,)), pltpu.SemaphoreType.DMA],
    )(p)


def _pair_join(r):
    rows = r.shape[0]

    def body(r_ref, o_ref, send_sem, recv_sem, local_sem):
        x, y, c = _mesh_pos()
        keep = pltpu.make_async_copy(r_ref, o_ref.at[pl.ds(c * rows, rows), :], local_sem)
        keep.start()
        give = pltpu.make_async_remote_copy(
            src_ref=r_ref, dst_ref=o_ref.at[pl.ds(c * rows, rows), :],
            send_sem=send_sem, recv_sem=recv_sem, device_id=(x, y, 1 - c), device_id_type=MESH)
        give.start()
        give.wait_send()
        pltpu.make_async_remote_copy(
            src_ref=r_ref, dst_ref=o_ref.at[pl.ds((1 - c) * rows, rows), :],
            send_sem=send_sem, recv_sem=recv_sem, device_id=(x, y, 1 - c), device_id_type=MESH).wait_recv()
        keep.wait()

    return pl.pallas_call(
        body, name="pair_join", in_specs=[ANY], out_specs=ANY,
        out_shape=jax.ShapeDtypeStruct((2 * rows, D_MODEL), r.dtype),
        scratch_shapes=[pltpu.SemaphoreType.DMA, pltpu.SemaphoreType.DMA, pltpu.SemaphoreType.DMA],
    )(r)


SHARD_BIG = (("even_w_in", (1024, 512)), ("ssm_w_glu", (512, 256)), ("even_w_out", (256, 1024)),
             ("odd_w_in", (1024, 768)), ("odd_w_out", (256, 1024)))
SHARD_SMALL = (("odd_norm", 1), ("conv_w", CONV_K), ("conv_b", 1), ("conv_ln_g", 1), ("conv_ln_b", 1))
REP_NAMES = (("even_norm", (1024,)), ("pool_w", (4, 128, 128)), ("pool_scale", (512,)), ("ssm_log_dt", (32,)),
             ("ssm_a_re", (32, 64)), ("ssm_a_im", (32, 64)), ("ssm_b_re", (32, 64, 16)), ("ssm_b_im", (32, 64, 16)),
             ("ssm_c_re", (32, 16, 64)), ("ssm_c_im", (32, 16, 64)), ("ssm_d", (512,)), ("final_norm", (1024,)))


def _pack_shard(d):
    big = [d[n].reshape(-1, D_MODEL) for n, _ in SHARD_BIG]
    small = jnp.concatenate([d[n].reshape(r, 256) for n, r in SHARD_SMALL], axis=0).reshape(-1)
    small = jnp.pad(small, (0, ROWS_SMALL * D_MODEL - small.shape[0])).reshape(ROWS_SMALL, D_MODEL)
    pad = jnp.zeros((ROWS_PACK - ROWS_BIG - ROWS_SMALL, D_MODEL), F32)
    return jnp.concatenate(big + [small, pad], axis=0)


def _unpack_shard(buf):
    out = {}
    off = 0
    for n, shp in SHARD_BIG:
        rows = shp[0] * shp[1] // D_MODEL
        out[n] = buf[off:off + rows].reshape(shp)
        off += rows
    small = buf[ROWS_BIG:ROWS_BIG + ROWS_SMALL].reshape(-1)[:35 * 256].reshape(35, 256)
    off = 0
    for n, r in SHARD_SMALL:
        out[n] = small[off:off + r].reshape((r, 256) if r > 1 else (256,))
        off += r
    return out


def _pack_rep(d):
    flat = jnp.concatenate([d[n].reshape(-1) for n, _ in REP_NAMES])
    return jnp.pad(flat, (0, REP_ROWS * D_MODEL - flat.shape[0])).reshape(REP_ROWS, D_MODEL)


def _unpack_rep(buf):
    flat = buf.reshape(-1)
    out = {}
    off = 0
    for n, shp in REP_NAMES:
        size = 1
        for s in shp:
            size *= s
        out[n] = flat[off:off + size].reshape(shp)
        off += size
    return out


def _cols_full(g4, rows, cols):
    return g4.reshape(4, rows, cols).transpose(1, 0, 2).reshape(rows, 4 * cols)


def _cols_split(full, cols):
    rows = full.shape[0]
    return full.reshape(rows, 4, cols).transpose(1, 0, 2).reshape(4, -1, D_MODEL)


def _block_diag(a):
    a = a.reshape(4, 8, GROUP_DIM, N_STATE)
    eye = jnp.eye(8, dtype=a.dtype)
    return (a[:, :, :, None, :] * eye[None, :, None, :, None]).reshape(4, 128, LCH)


def _block_diag_take(m):
    m = m.reshape(4, 8, GROUP_DIM, 8, N_STATE)
    eye = jnp.eye(8, dtype=m.dtype)
    return jnp.sum(m * eye[None, :, None, :, None], axis=3).reshape(N_GROUPS, GROUP_DIM, N_STATE)


def _ssm_discretise(log_dt, a_re, a_im, b_re, b_im):
    dt = jnp.exp(log_dt)[:, None]
    mag = jnp.exp(a_re * dt)
    ang = a_im * dt
    abar_re = mag * jnp.cos(ang)
    abar_im = mag * jnp.sin(ang)
    den = a_re * a_re + a_im * a_im
    nr = abar_re - 1.0
    ni = abar_im
    k_re = (nr * a_re + ni * a_im) / den
    k_im = (ni * a_re - nr * a_im) / den
    bb_re = k_re[..., None] * b_re - k_im[..., None] * b_im
    bb_im = k_re[..., None] * b_im + k_im[..., None] * b_re
    return abar_re, abar_im, bb_re, bb_im


def _powers(log_dt, a_re, a_im):
    dt = jnp.exp(log_dt)[:, None]
    k = jnp.arange(1, SUB + 1, dtype=F32)[:, None, None]
    mag = jnp.exp(k * (a_re * dt)[None])
    ang = k * (a_im * dt)[None]
    return (mag * jnp.cos(ang)).reshape(SUB, STATES), (mag * jnp.sin(ang)).reshape(SUB, STATES)


def _local_step(x, tgt, w):
    row = lambda a: a.reshape(1, -1)
    e_w_in = w["even_w_in"].astype(BF16)
    e_w_out = w["even_w_out"].astype(BF16)
    o_w_in = w["odd_w_in"].astype(BF16)
    o_w_out = w["odd_w_out"].astype(BF16)
    wglu = w["ssm_w_glu"].astype(BF16)
    wp = w["pool_w"].astype(BF16)
    ssm_in = (w["ssm_log_dt"], w["ssm_a_re"], w["ssm_a_im"], w["ssm_b_re"], w["ssm_b_im"])
    (abar_re, abar_im, bb_re, bb_im), ssm_vjp = jax.vjp(_ssm_discretise, *ssm_in)
    mb_re = _block_diag(bb_re.transpose(0, 2, 1)).astype(BF16)
    mb_im = _block_diag(bb_im.transpose(0, 2, 1)).astype(BF16)
    cm_re = _block_diag(w["ssm_c_re"]).astype(BF16)
    cm_im = _block_diag(w["ssm_c_im"]).astype(BF16)
    pw_re, pw_im = _powers(w["ssm_log_dt"], w["ssm_a_re"], w["ssm_a_im"])
    pr_re, pr_im = pw_re[::-1], -pw_im[::-1]
    cw = jnp.pad(w["conv_w"], ((0, HALO - CONV_K), (0, 0)))
    g0, g1, gf = row(w["even_norm"]), row(w["odd_norm"]), row(w["final_norm"])
    ps, dskip = row(w["pool_scale"]), row(w["ssm_d"])
    cb, lg, lb = row(w["conv_b"]), row(w["conv_ln_g"]), row(w["conv_ln_b"])

    proj = _norm_in(x, g0, e_w_in, "even_in")
    yp = _pool_fwd(proj, wp, ps)
    ys, car_re, car_im = _ssm_fwd(proj, mb_re, mb_im, pw_re, pw_im, cm_re, cm_im, dskip, wglu)
    x1, yg = _even_out(yp, ys, proj, x, e_w_out)
    q = _norm_in(x1, g1, o_w_in, "odd_in")
    y2, cv = _conv_fwd(q, cw, cb, lg, lb)
    dx2, loss_lanes, d_gf = _odd_out_loss(y2, x1, o_w_out, gf, tgt)

    dcv, dz2, d_o_w_out, d_lg, d_lb = _odd_bwd_out(dx2, o_w_out, y2, cv, q, lg, lb)
    dval, dgate, d_cw, d_cb = _conv_bwd(dcv, q, cw)
    dq = [dval, dgate, dz2]
    dx1, d_g1 = _in_bwd(dq, o_w_in, x1, g1, dx2, "odd_in_bwd")
    d_o_w_in = _in_wgrad(x1, g1, dq, "odd_in_wgrad")
    dycat, dz, d_e_w_out = _even_bwd_out(dx1, e_w_out, yg, yp, ys, proj)
    dup, d_wp, d_ps = _pool_bwd(dycat, proj, wp, ps)
    (dus, d_mb_re, d_mb_im, d_cm_re, d_cm_im, da_re, da_im, d_dskip, d_wglu) = _ssm_bwd(
        dycat, proj, car_re, car_im, mb_re, mb_im, pw_re, pw_im, pr_re, pr_im, cm_re, cm_im, dskip, wglu)
    dp = [dup, dus, dz]
    dx, d_g0 = _in_bwd(dp, e_w_in, x, g0, dx1, "even_in_bwd")
    d_e_w_in = _in_wgrad(x, g0, dp, "even_in_wgrad")

    d_abar_re = jnp.sum(da_re, axis=0).reshape(N_GROUPS, N_STATE)
    d_abar_im = jnp.sum(da_im, axis=0).reshape(N_GROUPS, N_STATE)
    d_bb_re = _block_diag_take(d_mb_re).transpose(0, 2, 1)
    d_bb_im = _block_diag_take(d_mb_im).transpose(0, 2, 1)
    d_log_dt, d_a_re, d_a_im, d_b_re, d_b_im = ssm_vjp((d_abar_re, d_abar_im, d_bb_re, d_bb_im))

    grads = {
        "even_norm": d_g0.reshape(-1), "even_w_in": d_e_w_in, "pool_w": d_wp, "pool_scale": d_ps.reshape(-1),
        "ssm_log_dt": d_log_dt, "ssm_a_re": d_a_re, "ssm_a_im": d_a_im, "ssm_b_re": d_b_re, "ssm_b_im": d_b_im,
        "ssm_c_re": _block_diag_take(d_cm_re), "ssm_c_im": _block_diag_take(d_cm_im),
        "ssm_d": d_dskip.reshape(-1), "ssm_w_glu": d_wglu, "even_w_out": d_e_w_out, "odd_norm": d_g1.reshape(-1),
        "odd_w_in": d_o_w_in, "conv_w": d_cw[:CONV_K], "conv_b": d_cb.reshape(-1), "conv_ln_g": d_lg.reshape(-1),
        "conv_ln_b": d_lb.reshape(-1), "odd_w_out": d_o_w_out, "final_norm": d_gf.reshape(-1),
    }
    return jnp.sum(loss_lanes), dx, grads


WEIGHT_NAMES = ("even_norm", "even_w_in", "pool_w", "pool_scale", "ssm_log_dt", "ssm_a_re", "ssm_a_im",
                "ssm_b_re", "ssm_b_im", "ssm_c_re", "ssm_c_im", "ssm_d", "ssm_w_glu", "even_w_out", "odd_norm",
                "odd_w_in", "conv_w", "conv_b", "conv_ln_g", "conv_ln_b", "odd_w_out", "final_norm")
SHARDED = tuple(n for n, _ in SHARD_BIG) + tuple(n for n, _ in SHARD_SMALL)


def _full_weights(shard, rep):
    pack = _pack_shard(shard)
    big = pack[:ROWS_BIG].astype(BF16)
    small = lax.bitcast_convert_type(pack[ROWS_BIG:ROWS_BIG + ROWS_SMALL], BF16).reshape(2 * ROWS_SMALL, D_MODEL)
    got = _gather_weights(jnp.concatenate([big, small], axis=0))
    gb = got[:, :ROWS_BIG]
    gs = lax.bitcast_convert_type(got[:, ROWS_BIG:].reshape(4, ROWS_SMALL, D_MODEL, 2), F32)
    gs = gs.reshape(4, -1)[:, :35 * 256].reshape(4, 35, 256)
    w = dict(rep)
    w["even_w_in"] = _cols_full(gb[:, 0:512], 1024, 512)
    w["ssm_w_glu"] = _cols_full(gb[:, 512:640], 512, 256)
    w["even_w_out"] = gb[:, 640:896].reshape(1024, 1024)
    w["odd_w_in"] = _cols_full(gb[:, 896:1664], 1024, 768)
    w["odd_w_out"] = gb[:, 1664:1920].reshape(1024, 1024)
    w["odd_norm"] = gs[:, 0].reshape(-1)
    w["conv_w"] = gs[:, 1:32].transpose(1, 0, 2).reshape(CONV_K, 1024)
    w["conv_b"] = gs[:, 32].reshape(-1)
    w["conv_ln_g"] = gs[:, 33].reshape(-1)
    w["conv_ln_b"] = gs[:, 34].reshape(-1)
    return w


def _pack_grads(g):
    small = jnp.concatenate([g["odd_norm"].reshape(4, 1, 256), g["conv_w"].reshape(CONV_K, 4, 256).transpose(1, 0, 2),
                             g["conv_b"].reshape(4, 1, 256), g["conv_ln_g"].reshape(4, 1, 256),
                             g["conv_ln_b"].reshape(4, 1, 256)], axis=1).reshape(4, -1)
    small = jnp.pad(small, ((0, 0), (0, ROWS_SMALL * D_MODEL - small.shape[1]))).reshape(4, ROWS_SMALL, D_MODEL)
    parts = [_cols_split(g["even_w_in"], 512), _cols_split(g["ssm_w_glu"], 256), g["even_w_out"].reshape(4, 256, 1024),
             _cols_split(g["odd_w_in"], 768), g["odd_w_out"].reshape(4, 256, 1024), small,
             jnp.zeros((4, ROWS_PACK - ROWS_BIG - ROWS_SMALL, D_MODEL), F32)]
    return jnp.concatenate(parts, axis=1)


def kernel(x, even_norm, even_w_in, pool_w, pool_scale, ssm_log_dt, ssm_a_re, ssm_a_im, ssm_b_re, ssm_b_im, ssm_c_re, ssm_c_im, ssm_d, ssm_w_glu, even_w_out, odd_norm, odd_w_in, conv_w, conv_b, conv_ln_g, conv_ln_b, odd_w_out, final_norm, loss_target, m_even_norm, m_even_w_in, m_pool_w, m_pool_scale, m_ssm_log_dt, m_ssm_a_re, m_ssm_a_im, m_ssm_b_re, m_ssm_b_im, m_ssm_c_re, m_ssm_c_im, m_ssm_d, m_ssm_w_glu, m_even_w_out, m_odd_norm, m_odd_w_in, m_conv_w, m_conv_b, m_conv_ln_g, m_conv_ln_b, m_odd_w_out, m_final_norm, v_even_norm, v_even_w_in, v_pool_w, v_pool_scale, v_ssm_log_dt, v_ssm_a_re, v_ssm_a_im, v_ssm_b_re, v_ssm_b_im, v_ssm_c_re, v_ssm_c_im, v_ssm_d, v_ssm_w_glu, v_even_w_out, v_odd_norm, v_odd_w_in, v_conv_w, v_conv_b, v_conv_ln_g, v_conv_ln_b, v_odd_w_out, v_final_norm):
    ws = dict(zip(WEIGHT_NAMES, (even_norm, even_w_in, pool_w, pool_scale, ssm_log_dt, ssm_a_re, ssm_a_im, ssm_b_re,
                                 ssm_b_im, ssm_c_re, ssm_c_im, ssm_d, ssm_w_glu, even_w_out, odd_norm, odd_w_in,
                                 conv_w, conv_b, conv_ln_g, conv_ln_b, odd_w_out, final_norm)))
    ms = dict(zip(WEIGHT_NAMES, (m_even_norm, m_even_w_in, m_pool_w, m_pool_scale, m_ssm_log_dt, m_ssm_a_re,
                                 m_ssm_a_im, m_ssm_b_re, m_ssm_b_im, m_ssm_c_re, m_ssm_c_im, m_ssm_d, m_ssm_w_glu,
                                 m_even_w_out, m_odd_norm, m_odd_w_in, m_conv_w, m_conv_b, m_conv_ln_g, m_conv_ln_b,
                                 m_odd_w_out, m_final_norm)))
    vs = dict(zip(WEIGHT_NAMES, (v_even_norm, v_even_w_in, v_pool_w, v_pool_scale, v_ssm_log_dt, v_ssm_a_re,
                                 v_ssm_a_im, v_ssm_b_re, v_ssm_b_im, v_ssm_c_re, v_ssm_c_im, v_ssm_d, v_ssm_w_glu,
                                 v_even_w_out, v_odd_norm, v_odd_w_in, v_conv_w, v_conv_b, v_conv_ln_g, v_conv_ln_b,
                                 v_odd_w_out, v_final_norm)))
    lead = {n: a.shape for n, a in ws.items()}
    drop = lambda d: {n: (a[0] if n != "final_norm" else a) for n, a in d.items()}
    ws, ms, vs = drop(ws), drop(ms), drop(vs)

    shard = {n: ws[n] for n in SHARDED}
    rep = {n: ws[n] for n, _ in REP_NAMES}
    w_full = _full_weights(shard, rep)
    loss_part, grad_x, grads = _local_step(x[0], loss_target[0], w_full)
    loss = lax.psum(loss_part, ("x", "y", "c"))

    mine, theirs = _pair_split(_pack_grads(grads))
    got = _chip_scatter(_add2(mine, theirs, "pair_add"))
    g_shard = _pair_join(_sum_lead(got, "chip_sum"))
    d_shard, m_shard, v_shard = _adamw(_pack_shard(shard), g_shard, _pack_shard({n: ms[n] for n in SHARDED}),
                                       _pack_shard({n: vs[n] for n in SHARDED}), "adamw_shard")
    g_rep = _sum_lead(_gather_all(_pack_rep({n: grads[n] for n, _ in REP_NAMES})), "rep_sum")
    d_rep, m_rep, v_rep = _adamw(_pack_rep(rep), g_rep, _pack_rep({n: ms[n] for n, _ in REP_NAMES}),
                                 _pack_rep({n: vs[n] for n, _ in REP_NAMES}), "adamw_rep")

    def unpack(shard_buf, rep_buf):
        d = {**_unpack_shard(shard_buf), **_unpack_rep(rep_buf)}
        return [d[n].reshape(lead[n]) for n in WEIGHT_NAMES]

    return (loss, grad_x[None], *unpack(g_shard, g_rep), *unpack(d_shard, d_rep),
            *unpack(m_shard, m_rep), *unpack(v_shard, v_rep))
```

```python
import functools

import jax
import jax.numpy as jnp
from jax import lax
from jax.experimental import pallas as pl
from jax.experimental.pallas import tpu as pltpu

F32 = jnp.float32
BF16 = jnp.bfloat16
MESH = pl.DeviceIdType.MESH

D_MODEL = 1024
RMS_EPS = 1e-6
LN_EPS = 1e-5
N_GROUPS = 32
GROUP_DIM = 16
N_STATE = 64
STATES = N_GROUPS * N_STATE
SSM_W = 512
POOL_W = 512
CONV_K = 31
HALO = 32
POOL_HALO = 16

ADAM_LR = 0.001
ADAM_B1 = 0.9
ADAM_B2 = 0.999
ADAM_EPS = 1e-08
ADAM_WD = 0.01
ADAM_STEP = 10

TM = 256
SUB = 8
LCH = 512
VMEM_LIMIT = 56 * 1024 * 1024

ROWS_BIG = 1920
ROWS_SMALL = 16
ROWS_PACK = 2048
REP_ROWS = 200
COMM_CHUNKS = 4


def _params(n_axes=1):
    return pltpu.CompilerParams(dimension_semantics=("arbitrary",) * n_axes, vmem_limit_bytes=VMEM_LIMIT)


def _rows(w, cb=0, rev=None, tm=TM):
    if rev is None:
        return pl.BlockSpec((tm, w), lambda i: (i, cb))
    return pl.BlockSpec((tm, w), lambda i: (rev - 1 - i, cb))


def _full(shape):
    n = len(shape)
    return pl.BlockSpec(shape, lambda i: (0,) * n)


def _prev(hr, w, cb=0, tm=TM):
    r = tm // hr
    return pl.BlockSpec((hr, w), lambda i: (jnp.maximum(i * r - 1, 0), cb))


def _next(hr, w, nrows, cb=0, tm=TM):
    r = tm // hr
    last = nrows // hr - 1
    return pl.BlockSpec((hr, w), lambda i: (jnp.minimum((i + 1) * r, last), cb))


def _dot(a, b):
    return jnp.dot(a, b, preferred_element_type=F32)


def _dot_nt(a, b):
    return lax.dot_general(a, b, (((1,), (1,)), ((), ())), preferred_element_type=F32)


def _dot_tn(a, b):
    return lax.dot_general(a, b, (((0,), (0,)), ((), ())), preferred_element_type=F32)


def _sig(x):
    return 1.0 / (1.0 + jnp.exp(-x))


def _zero_at_first(i, *refs):
    @pl.when(i == 0)
    def _():
        for r in refs:
            r[...] = jnp.zeros_like(r)


def _norm_in(x, g, w, name):
    t, n = x.shape[0], w.shape[1]

    def body(x_ref, g_ref, w_ref, o_ref):
        xv = x_ref[...]
        r = lax.rsqrt(jnp.mean(xv * xv, axis=-1, keepdims=True) + RMS_EPS)
        o_ref[...] = _dot((xv * r * g_ref[...]).astype(BF16), w_ref[...]).astype(BF16)

    return pl.pallas_call(
        body, name=name, grid=(t // TM,),
        in_specs=[_rows(D_MODEL), _full((1, D_MODEL)), _full(w.shape)],
        out_specs=_rows(n), out_shape=jax.ShapeDtypeStruct((t, n), BF16),
        compiler_params=_params())(x, g, w)


def _pool_sums(ext, g, forward):
    n = ext.shape[0]
    s = ext
    for step in range(g + 1):
        k = 1 << step
        s = s + pltpu.roll(s, k if forward else n - k, 0)
    return s


def _pool_fwd(proj, wp, ps):
    t = proj.shape[0]

    def body(u_ref, h_ref, wp_ref, ps_ref, y_ref):
        i = pl.program_id(0)
        pos = (i * TM + 1 + lax.broadcasted_iota(jnp.int32, (TM, 1), 0)).astype(F32)
        for g in range(4):
            sl = slice(128 * g, 128 * (g + 1))
            u = u_ref[:, sl].astype(F32)
            halo = jnp.where(i == 0, 0.0, h_ref[:, sl].astype(F32))
            s = _pool_sums(jnp.concatenate([halo, u], axis=0), g, True)[POOL_HALO:, :]
            pooled = s / jnp.minimum(pos, float(2 << g)) - u
            y_ref[:, sl] = _dot(pooled.astype(BF16), wp_ref[g]) * ps_ref[:, sl]

    return pl.pallas_call(
        body, name="pool_fwd", grid=(t // TM,),
        in_specs=[_rows(POOL_W, 0), _prev(POOL_HALO, POOL_W, 0), _full((4, 128, 128)), _full((1, POOL_W))],
        out_specs=_rows(POOL_W), out_shape=jax.ShapeDtypeStruct((t, POOL_W), F32),
        compiler_params=_params())(proj, proj, wp, ps)


def _scan_fwd_block(xs_re, xs_im, xp_re, xp_im, pw_re, pw_im, car_re, car_im):
    row = lax.broadcasted_iota(jnp.int32, (SUB, LCH), 0)
    for j in range(STATES // LCH):
        sl = slice(LCH * j, LCH * (j + 1))
        pre, pim = pw_re[:, sl], pw_im[:, sl]

        def sub(k, carry, sl=sl, pre=pre, pim=pim):
            cre, cim = carry
            r0 = pl.multiple_of(k * SUB, SUB)
            vre = xs_re[pl.ds(r0, SUB), sl]
            vim = xs_im[pl.ds(r0, SUB), sl]
            for sh in (1, 2, 4):
                are, aim = pre[sh - 1:sh, :], pim[sh - 1:sh, :]
                sre = jnp.where(row >= sh, pltpu.roll(vre, sh, 0), 0.0)
                sim = jnp.where(row >= sh, pltpu.roll(vim, sh, 0), 0.0)
                vre, vim = vre + are * sre - aim * sim, vim + are * sim + aim * sre
            vre, vim = vre + pre * cre - pim * cim, vim + pre * cim + pim * cre
            xs_re[pl.ds(r0, SUB), sl] = vre
            xs_im[pl.ds(r0, SUB), sl] = vim
            if xp_re is not None:
                xp_re[pl.ds(r0, SUB), sl] = jnp.where(row >= 1, pltpu.roll(vre, 1, 0), cre)
                xp_im[pl.ds(r0, SUB), sl] = jnp.where(row >= 1, pltpu.roll(vim, 1, 0), cim)
            return (jnp.broadcast_to(vre[SUB - 1:SUB, :], (SUB, LCH)),
                    jnp.broadcast_to(vim[SUB - 1:SUB, :], (SUB, LCH)))

        cre, cim = lax.fori_loop(0, TM // SUB, sub, (car_re[:, sl], car_im[:, sl]), unroll=2)
        car_re[:, sl] = cre
        car_im[:, sl] = cim


def _ssm_fwd(proj, mb_re, mb_im, pw_re, pw_im, cm_re, cm_im, dskip, wglu):
    t = proj.shape[0]
    nblk = t // TM

    def body(u_ref, mbre, mbim, pwre, pwim, cmre, cmim, d_ref, wg_ref,
             y_ref, cre_ref, cim_ref, xs_re, xs_im, car_re, car_im, ysk):
        i = pl.program_id(0)
        _zero_at_first(i, car_re, car_im)
        cre_ref[0] = car_re[...]
        cim_ref[0] = car_im[...]
        usb = u_ref[...]
        us = usb.astype(F32)
        for j in range(4):
            xs_re[:, LCH * j:LCH * (j + 1)] = _dot(usb[:, 128 * j:128 * (j + 1)], mbre[j])
            xs_im[:, LCH * j:LCH * (j + 1)] = _dot(usb[:, 128 * j:128 * (j + 1)], mbim[j])
        _scan_fwd_block(xs_re, xs_im, None, None, pwre, pwim, car_re, car_im)
        for j in range(4):
            sl = slice(LCH * j, LCH * (j + 1))
            ysk[:, 128 * j:128 * (j + 1)] = (_dot_nt(xs_re[:, sl].astype(BF16), cmre[j])
                                             - _dot_nt(xs_im[:, sl].astype(BF16), cmim[j]))
        yv = ysk[...] + d_ref[...] * us
        gv = _dot(yv.astype(BF16), wg_ref[...])
        y_ref[...] = gv[:, :SSM_W] * _sig(gv[:, SSM_W:])

    blk = (4, 128, LCH)
    return pl.pallas_call(
        body, name="ssm_fwd", grid=(nblk,),
        in_specs=[_rows(SSM_W, 1), _full(blk), _full(blk), _full((SUB, STATES)), _full((SUB, STATES)),
                  _full(blk), _full(blk), _full((1, SSM_W)), _full((SSM_W, 2 * SSM_W))],
        out_specs=[_rows(SSM_W), pl.BlockSpec((1, SUB, STATES), lambda i: (i, 0, 0)),
                   pl.BlockSpec((1, SUB, STATES), lambda i: (i, 0, 0))],
        out_shape=[jax.ShapeDtypeStruct((t, SSM_W), F32), jax.ShapeDtypeStruct((nblk, SUB, STATES), F32),
                   jax.ShapeDtypeStruct((nblk, SUB, STATES), F32)],
        scratch_shapes=[pltpu.VMEM((TM, STATES), F32), pltpu.VMEM((TM, STATES), F32),
                        pltpu.VMEM((SUB, STATES), F32), pltpu.VMEM((SUB, STATES), F32),
                        pltpu.VMEM((TM, SSM_W), F32)],
        compiler_params=_params())(proj, mb_re, mb_im, pw_re, pw_im, cm_re, cm_im, dskip, wglu)


def _even_out(yp, ys, proj, x, w):
    t = x.shape[0]

    def body(yp_ref, ys_ref, z_ref, x_ref, w_ref, x1_ref, yg_ref):
        z = z_ref[...].astype(F32)
        sz = z * _sig(z)
        gp = (yp_ref[...] * sz[:, :POOL_W]).astype(BF16)
        gs = (ys_ref[...] * sz[:, POOL_W:]).astype(BF16)
        yg_ref[:, :POOL_W] = gp
        yg_ref[:, POOL_W:] = gs
        x1_ref[...] = x_ref[...] + _dot(gp, w_ref[:POOL_W, :]) + _dot(gs, w_ref[POOL_W:, :])

    return pl.pallas_call(
        body, name="even_out", grid=(t // TM,),
        in_specs=[_rows(POOL_W), _rows(SSM_W), _rows(D_MODEL, 1), _rows(D_MODEL), _full((D_MODEL, D_MODEL))],
        out_specs=[_rows(D_MODEL), _rows(D_MODEL)],
        out_shape=[jax.ShapeDtypeStruct((t, D_MODEL), F32), jax.ShapeDtypeStruct((t, D_MODEL), BF16)],
        compiler_params=_params())(yp, ys, proj, x, w)


def _phase_copies(ext, cp):
    n = cp.shape[1]
    for j in range(1, SUB):
        cp[j - 1] = ext[pl.ds(j, n), :]


def _shifted(ext, cp, off, nrows, sl):
    q, j = divmod(off, SUB)
    if j == 0:
        return ext[pl.ds(SUB * q, nrows), sl]
    return cp[j - 1, pl.ds(SUB * q, nrows), sl]


def _conv_taps(ext, cp, w_ref, first, nrows, sl, init):
    acc = init
    for k in range(CONV_K):
        acc = acc + w_ref[k:k + 1, sl] * _shifted(ext, cp, first(k), nrows, sl)
    return acc


def _conv_fwd(q, cw, cb, lg, lb):
    t = q.shape[0]

    def body(v_ref, g_ref, hv_ref, hg_ref, z_ref, w_ref, b_ref, lg_ref, lb_ref, y_ref, cv_ref, ext, cp):
        i = pl.program_id(0)
        ext[0:HALO, :] = jnp.where(i == 0, 0.0, hv_ref[...].astype(F32) * _sig(hg_ref[...].astype(F32)))
        ext[HALO:, :] = v_ref[...].astype(F32) * _sig(g_ref[...].astype(F32))
        _phase_copies(ext, cp)
        for c in range(D_MODEL // 128):
            sl = slice(128 * c, 128 * (c + 1))
            cv_ref[:, sl] = _conv_taps(ext, cp, w_ref, lambda k: k + 2, TM, sl,
                                       jnp.broadcast_to(b_ref[:, sl], (TM, 128)))
        cv = cv_ref[...]
        cc = cv - jnp.mean(cv, axis=-1, keepdims=True)
        rstd = lax.rsqrt(jnp.mean(cc * cc, axis=-1, keepdims=True) + LN_EPS)
        cl = cc * rstd * lg_ref[...] + lb_ref[...]
        z = z_ref[...].astype(F32)
        y_ref[...] = (cl * _sig(cl) * z * _sig(z)).astype(BF16)

    vec = _full((1, D_MODEL))
    return pl.pallas_call(
        body, name="conv_fwd", grid=(t // TM,),
        in_specs=[_rows(D_MODEL, 0), _rows(D_MODEL, 1), _prev(HALO, D_MODEL, 0), _prev(HALO, D_MODEL, 1),
                  _rows(D_MODEL, 2), _full((HALO, D_MODEL)), vec, vec, vec],
        out_specs=[_rows(D_MODEL), _rows(D_MODEL)],
        out_shape=[jax.ShapeDtypeStruct((t, D_MODEL), BF16), jax.ShapeDtypeStruct((t, D_MODEL), F32)],
        scratch_shapes=[pltpu.VMEM((TM + HALO, D_MODEL), F32),
                        pltpu.VMEM((SUB - 1, TM + HALO - SUB, D_MODEL), F32)],
        compiler_params=_params())(q, q, q, q, q, cw, cb, lg, lb)


def _odd_out_loss(y2, x1, w, gf, tgt):
    t = x1.shape[0]

    def body(y_ref, x_ref, w_ref, g_ref, t_ref, dx_ref, loss_ref, dg_ref):
        i = pl.program_id(0)
        _zero_at_first(i, loss_ref, dg_ref)
        x2 = x_ref[...] + _dot(y_ref[...], w_ref[...])
        r = lax.rsqrt(jnp.mean(x2 * x2, axis=-1, keepdims=True) + RMS_EPS)
        n = x2 * r
        e = n * g_ref[...] - t_ref[...]
        loss_ref[...] += jnp.sum(e * e, axis=0, keepdims=True) * (0.5 / D_MODEL)
        dout = e * (1.0 / D_MODEL)
        dg_ref[...] += jnp.sum(dout * n, axis=0, keepdims=True)
        dn = dout * g_ref[...]
        dx_ref[...] = r * (dn - n * jnp.mean(dn * n, axis=-1, keepdims=True))

    vec = _full((1, D_MODEL))
    return pl.pallas_call(
        body, name="odd_out_loss", grid=(t // TM,),
        in_specs=[_rows(D_MODEL), _rows(D_MODEL), _full((D_MODEL, D_MODEL)), vec, _rows(D_MODEL)],
        out_specs=[_rows(D_MODEL), vec, vec],
        out_shape=[jax.ShapeDtypeStruct((t, D_MODEL), F32), jax.ShapeDtypeStruct((1, D_MODEL), F32),
                   jax.ShapeDtypeStruct((1, D_MODEL), F32)],
        compiler_params=_params())(y2, x1, w, gf, tgt)


def _dsilu(z):
    s = _sig(z)
    return z * s, s * (1.0 + z * (1.0 - s))


def _odd_bwd_out(dx2, w, y2, cv, q, lg, lb):
    t = dx2.shape[0]

    def body(dx_ref, w_ref, y_ref, cv_ref, z_ref, lg_ref, lb_ref, dcv_ref, dz_ref, dw_ref, dlg_ref, dlb_ref):
        i = pl.program_id(0)
        _zero_at_first(i, dw_ref, dlg_ref, dlb_ref)
        dxb = dx_ref[...].astype(BF16)
        dy = _dot_nt(dxb, w_ref[...])
        dw_ref[...] += _dot_tn(y_ref[...], dxb)
        cv = cv_ref[...]
        cc = cv - jnp.mean(cv, axis=-1, keepdims=True)
        rstd = lax.rsqrt(jnp.mean(cc * cc, axis=-1, keepdims=True) + LN_EPS)
        cn = cc * rstd
        silu_c, dsilu_c = _dsilu(cn * lg_ref[...] + lb_ref[...])
        silu_z, dsilu_z = _dsilu(z_ref[...].astype(F32))
        dcl = dy * silu_z * dsilu_c
        dz_ref[...] = (dy * silu_c * dsilu_z).astype(BF16)
        dlg_ref[...] += jnp.sum(dcl * cn, axis=0, keepdims=True)
        dlb_ref[...] += jnp.sum(dcl, axis=0, keepdims=True)
        dcn = dcl * lg_ref[...]
        dcv_ref[...] = rstd * (dcn - jnp.mean(dcn, axis=-1, keepdims=True)
                               - cn * jnp.mean(dcn * cn, axis=-1, keepdims=True))

    vec = _full((1, D_MODEL))
    mat = _full((D_MODEL, D_MODEL))
    return pl.pallas_call(
        body, name="odd_bwd_out", grid=(t // TM,),
        in_specs=[_rows(D_MODEL), mat, _rows(D_MODEL), _rows(D_MODEL), _rows(D_MODEL, 2), vec, vec],
        out_specs=[_rows(D_MODEL), _rows(D_MODEL), mat, vec, vec],
        out_shape=[jax.ShapeDtypeStruct((t, D_MODEL), F32), jax.ShapeDtypeStruct((t, D_MODEL), BF16),
                   jax.ShapeDtypeStruct((D_MODEL, D_MODEL), F32), jax.ShapeDtypeStruct((1, D_MODEL), F32),
                   jax.ShapeDtypeStruct((1, D_MODEL), F32)],
        compiler_params=_params())(dx2, w, y2, cv, q, lg, lb)


def _conv_bwd(dcv, q, cw):
    t = dcv.shape[0]
    nblk = t // TM

    def body(d_ref, dn_ref, v_ref, g_ref, hv_ref, hg_ref, w_ref,
             dv_ref, dgt_ref, dw_ref, db_ref, gext, dext, dgl, gcp, dcp):
        i = pl.program_id(0)
        last = nblk - 1
        _zero_at_first(i, dw_ref, db_ref)
        v = v_ref[...].astype(F32)
        sg = _sig(g_ref[...].astype(F32))
        gext[0:HALO, :] = jnp.where(i == 0, 0.0, hv_ref[...].astype(F32) * _sig(hg_ref[...].astype(F32)))
        gext[HALO:, :] = v * sg
        d = d_ref[...]
        dext[0:TM, :] = d
        dext[TM:, :] = jnp.where(i == last, 0.0, dn_ref[...])
        _phase_copies(gext, gcp)
        _phase_copies(dext, dcp)
        db_ref[...] += jnp.sum(d, axis=0, keepdims=True)
        for c in range(D_MODEL // 128):
            sl = slice(128 * c, 128 * (c + 1))
            dgl[:, sl] = _conv_taps(dext, dcp, w_ref, lambda k: 30 - k, TM, sl, jnp.zeros((TM, 128), F32))
            dc = d[:, sl]
            for k in range(CONV_K):
                dw_ref[k:k + 1, sl] += jnp.sum(dc * _shifted(gext, gcp, k + 2, TM, sl), axis=0, keepdims=True)
        dg = dgl[...]
        dv_ref[...] = (dg * sg).astype(BF16)
        dgt_ref[...] = (dg * v * sg * (1.0 - sg)).astype(BF16)

    return pl.pallas_call(
        body, name="conv_bwd", grid=(t // TM,),
        in_specs=[_rows(D_MODEL), _next(HALO, D_MODEL, t), _rows(D_MODEL, 0), _rows(D_MODEL, 1),
                  _prev(HALO, D_MODEL, 0), _prev(HALO, D_MODEL, 1), _full((HALO, D_MODEL))],
        out_specs=[_rows(D_MODEL), _rows(D_MODEL), _full((HALO, D_MODEL)), _full((1, D_MODEL))],
        out_shape=[jax.ShapeDtypeStruct((t, D_MODEL), BF16), jax.ShapeDtypeStruct((t, D_MODEL), BF16),
                   jax.ShapeDtypeStruct((HALO, D_MODEL), F32), jax.ShapeDtypeStruct((1, D_MODEL), F32)],
        scratch_shapes=[pltpu.VMEM((TM + HALO, D_MODEL), F32), pltpu.VMEM((TM + HALO, D_MODEL), F32),
                        pltpu.VMEM((TM, D_MODEL), F32),
                        pltpu.VMEM((SUB - 1, TM + HALO - SUB, D_MODEL), F32),
                        pltpu.VMEM((SUB - 1, TM + HALO - SUB, D_MODEL), F32)],
        compiler_params=_params())(dcv, dcv, q, q, q, q, cw)


def _in_bwd(dparts, w, x, g, dres, name):
    t = x.shape[0]
    widths = [p.shape[1] for p in dparts]
    npart = len(dparts)

    def body(*refs):
        d_refs = refs[:npart]
        w_ref, x_ref, g_ref, r_ref, dx_ref, dg_ref, dw_ref = refs[npart:]
        i = pl.program_id(0)
        _zero_at_first(i, dg_ref, dw_ref)
        xv = x_ref[...]
        r = lax.rsqrt(jnp.mean(xv * xv, axis=-1, keepdims=True) + RMS_EPS)
        n = xv * r
        h = (n * g_ref[...]).astype(BF16)
        off = 0
        dh = None
        for d_ref, wd in zip(d_refs, widths):
            d = d_ref[...]
            part = _dot_nt(d, w_ref[:, off:off + wd])
            dh = part if dh is None else dh + part
            dw_ref[:, off:off + wd] += _dot_tn(h, d)
            off += wd
        dg_ref[...] += jnp.sum(dh * n, axis=0, keepdims=True)
        dn = dh * g_ref[...]
        dx_ref[...] = r_ref[...] + r * (dn - n * jnp.mean(dn * n, axis=-1, keepdims=True))

    vec = _full((1, D_MODEL))
    once = pl.BlockSpec(w.shape, lambda i: (0, 0), pipeline_mode=pl.Buffered(1))
    return pl.pallas_call(
        body, name=name, grid=(t // TM,),
        in_specs=[_rows(wd) for wd in widths] + [once, _rows(D_MODEL), vec, _rows(D_MODEL)],
        out_specs=[_rows(D_MODEL), vec, once],
        out_shape=[jax.ShapeDtypeStruct((t, D_MODEL), F32), jax.ShapeDtypeStruct((1, D_MODEL), F32),
                   jax.ShapeDtypeStruct(w.shape, F32)],
        compiler_params=_params())(*dparts, w, x, g, dres)


def _even_bwd_out(dx1, w, yg, yp, ys, proj):
    t = dx1.shape[0]

    def body(dx_ref, w_ref, yg_ref, yp_ref, ys_ref, z_ref, dy_ref, dz_ref, dw_ref):
        i = pl.program_id(0)
        _zero_at_first(i, dw_ref)
        dxb = dx_ref[...].astype(BF16)
        dyg = _dot_nt(dxb, w_ref[...])
        dw_ref[...] += _dot_tn(yg_ref[...], dxb)
        silu_z, dsilu_z = _dsilu(z_ref[...].astype(F32))
        dy_ref[...] = dyg * silu_z
        dz_ref[:, :POOL_W] = (dyg[:, :POOL_W] * yp_ref[...] * dsilu_z[:, :POOL_W]).astype(BF16)
        dz_ref[:, POOL_W:] = (dyg[:, POOL_W:] * ys_ref[...] * dsilu_z[:, POOL_W:]).astype(BF16)

    mat = _full((D_MODEL, D_MODEL))
    return pl.pallas_call(
        body, name="even_bwd_out", grid=(t // TM,),
        in_specs=[_rows(D_MODEL), mat, _rows(D_MODEL), _rows(POOL_W), _rows(SSM_W), _rows(D_MODEL, 1)],
        out_specs=[_rows(D_MODEL), _rows(D_MODEL), mat],
        out_shape=[jax.ShapeDtypeStruct((t, D_MODEL), F32), jax.ShapeDtypeStruct((t, D_MODEL), BF16),
                   jax.ShapeDtypeStruct((D_MODEL, D_MODEL), F32)],
        compiler_params=_params())(dx1, w, yg, yp, ys, proj)


def _pool_bwd(dycat, proj, wp, ps):
    t = proj.shape[0]

    def body(dy_ref, dyn_ref, u_ref, h_ref, wp_ref, ps_ref, du_ref, dwp_ref, dps_ref):
        i = pl.program_id(0)
        last = t // TM - 1
        _zero_at_first(i, dwp_ref, dps_ref)
        pos = (i * TM + 1 + lax.broadcasted_iota(jnp.int32, (TM, 1), 0)).astype(F32)
        pos_ext = (i * TM + 1 + lax.broadcasted_iota(jnp.int32, (TM + POOL_HALO, 1), 0)).astype(F32)
        for g in range(4):
            sl = slice(128 * g, 128 * (g + 1))
            w = float(2 << g)
            u = u_ref[:, sl].astype(F32)
            halo = jnp.where(i == 0, 0.0, h_ref[:, sl].astype(F32))
            s = _pool_sums(jnp.concatenate([halo, u], axis=0), g, True)[POOL_HALO:, :]
            pooled = (s / jnp.minimum(pos, w) - u).astype(BF16)
            dy = dy_ref[:, sl]
            dps_ref[:, sl] += jnp.sum(dy * _dot(pooled, wp_ref[g]), axis=0, keepdims=True)
            dy_ext = jnp.concatenate([dy, jnp.where(i == last, 0.0, dyn_ref[:, sl])], axis=0)
            dmix = (dy_ext * ps_ref[:, sl]).astype(BF16)
            dwp_ref[g] += _dot_tn(pooled, dmix[:TM, :])
            dpool = _dot_nt(dmix, wp_ref[g])
            lead = _pool_sums(dpool / jnp.minimum(pos_ext, w), g, False)
            du_ref[:, sl] = (lead[:TM, :] - dpool[:TM, :]).astype(BF16)

    return pl.pallas_call(
        body, name="pool_bwd", grid=(t // TM,),
        in_specs=[_rows(POOL_W, 0), _next(POOL_HALO, POOL_W, t, 0), _rows(POOL_W, 0), _prev(POOL_HALO, POOL_W, 0),
                  _full((4, 128, 128)), _full((1, POOL_W))],
        out_specs=[_rows(POOL_W), _full((4, 128, 128)), _full((1, POOL_W))],
        out_shape=[jax.ShapeDtypeStruct((t, POOL_W), BF16), jax.ShapeDtypeStruct((4, 128, 128), F32),
                   jax.ShapeDtypeStruct((1, POOL_W), F32)],
        compiler_params=_params())(dycat, dycat, proj, proj, wp, ps)


def _ssm_bwd(dycat, proj, car_in_re, car_in_im, mb_re, mb_im, pw_re, pw_im, pr_re, pr_im,
             cm_re, cm_im, dskip, wglu):
    t = proj.shape[0]
    nblk = t // TM

    def body(dy_ref, u_ref, cin_re, cin_im, mbre, mbim, pwre, pwim, prre, prim, cmre, cmim, d_ref, wg_ref,
             du_ref, dmbre, dmbim, dcmre, dcmim, dare, daim, dd_ref, dwg_ref,
             xs_re, xs_im, xp_re, xp_im, gs_re, gs_im, car_re, car_im, gcar_re, gcar_im, ysk, dysk):
        i = pl.program_id(0)
        _zero_at_first(i, dmbre, dmbim, dcmre, dcmim, dare, daim, dd_ref, dwg_ref, gcar_re, gcar_im)
        usb = u_ref[...]
        us = usb.astype(F32)
        for j in range(4):
            xs_re[:, LCH * j:LCH * (j + 1)] = _dot(usb[:, 128 * j:128 * (j + 1)], mbre[j])
            xs_im[:, LCH * j:LCH * (j + 1)] = _dot(usb[:, 128 * j:128 * (j + 1)], mbim[j])
        car_re[...] = cin_re[0]
        car_im[...] = cin_im[0]
        _scan_fwd_block(xs_re, xs_im, xp_re, xp_im, pwre, pwim, car_re, car_im)
        for j in range(4):
            sl = slice(LCH * j, LCH * (j + 1))
            ysk[:, 128 * j:128 * (j + 1)] = (_dot_nt(xs_re[:, sl].astype(BF16), cmre[j])
                                             - _dot_nt(xs_im[:, sl].astype(BF16), cmim[j]))
        yvb = (ysk[...] + d_ref[...] * us).astype(BF16)
        gv = _dot(yvb, wg_ref[...])
        sg = _sig(gv[:, SSM_W:])
        dyss = dy_ref[...]
        dval = (dyss * sg).astype(BF16)
        dgate = (dyss * gv[:, :SSM_W] * sg * (1.0 - sg)).astype(BF16)
        dy = _dot_nt(dval, wg_ref[:, :SSM_W]) + _dot_nt(dgate, wg_ref[:, SSM_W:])
        dwg_ref[:, :SSM_W] += _dot_tn(yvb, dval)
        dwg_ref[:, SSM_W:] += _dot_tn(yvb, dgate)
        dd_ref[...] += jnp.sum(dy * us, axis=0, keepdims=True)
        dysk[...] = dy
        for j in range(4):
            sl = slice(LCH * j, LCH * (j + 1))
            dyj = dy[:, 128 * j:128 * (j + 1)].astype(BF16)
            gs_re[:, sl] = _dot(dyj, cmre[j])
            gs_im[:, sl] = -_dot(dyj, cmim[j])
            dcmre[j] += _dot_tn(dyj, xs_re[:, sl].astype(BF16))
            dcmim[j] -= _dot_tn(dyj, xs_im[:, sl].astype(BF16))
        row = lax.broadcasted_iota(jnp.int32, (SUB, LCH), 0)
        nsub = TM // SUB
        for j in range(STATES // LCH):
            sl = slice(LCH * j, LCH * (j + 1))
            pre, pim = prre[:, sl], prim[:, sl]

            def sub(k, carry, sl=sl, pre=pre, pim=pim):
                cre, cim, are_acc, aim_acc = carry
                r0 = pl.multiple_of((nsub - 1 - k) * SUB, SUB)
                vre = gs_re[pl.ds(r0, SUB), sl]
                vim = gs_im[pl.ds(r0, SUB), sl]
                for sh in (1, 2, 4):
                    are, aim = pre[SUB - sh:SUB - sh + 1, :], pim[SUB - sh:SUB - sh + 1, :]
                    sre = jnp.where(row < SUB - sh, pltpu.roll(vre, SUB - sh, 0), 0.0)
                    sim = jnp.where(row < SUB - sh, pltpu.roll(vim, SUB - sh, 0), 0.0)
                    vre, vim = vre + are * sre - aim * sim, vim + are * sim + aim * sre
                vre, vim = vre + pre * cre - pim * cim, vim + pre * cim + pim * cre
                gs_re[pl.ds(r0, SUB), sl] = vre
                gs_im[pl.ds(r0, SUB), sl] = vim
                qre = xp_re[pl.ds(r0, SUB), sl]
                qim = xp_im[pl.ds(r0, SUB), sl]
                return (jnp.broadcast_to(vre[0:1, :], (SUB, LCH)), jnp.broadcast_to(vim[0:1, :], (SUB, LCH)),
                        are_acc + vre * qre + vim * qim, aim_acc + vim * qre - vre * qim)

            cre, cim, are_acc, aim_acc = lax.fori_loop(
                0, nsub, sub, (gcar_re[:, sl], gcar_im[:, sl], dare[:, sl], daim[:, sl]), unroll=2)
            gcar_re[:, sl] = cre
            gcar_im[:, sl] = cim
            dare[:, sl] = are_acc
            daim[:, sl] = aim_acc
        for j in range(4):
            sl = slice(LCH * j, LCH * (j + 1))
            c4 = slice(128 * j, 128 * (j + 1))
            gre = gs_re[:, sl].astype(BF16)
            gim = gs_im[:, sl].astype(BF16)
            dmbre[j] += _dot_tn(usb[:, c4], gre)
            dmbim[j] += _dot_tn(usb[:, c4], gim)
            du_ref[:, c4] = (_dot_nt(gre, mbre[j]) + _dot_nt(gim, mbim[j]) + dysk[:, c4] * d_ref[:, c4]).astype(BF16)

    blk = (4, 128, LCH)
    pw = _full((SUB, STATES))
    car = pl.BlockSpec((1, SUB, STATES), lambda i: (nblk - 1 - i, 0, 0))
    big = lambda: pltpu.VMEM((TM, STATES), F32)
    small = lambda: pltpu.VMEM((SUB, STATES), F32)
    return pl.pallas_call(
        body, name="ssm_bwd", grid=(nblk,),
        in_specs=[_rows(SSM_W, 1, rev=nblk), _rows(SSM_W, 1, rev=nblk), car, car, _full(blk), _full(blk),
                  pw, pw, pw, pw, _full(blk), _full(blk), _full((1, SSM_W)), _full((SSM_W, 2 * SSM_W))],
        out_specs=[_rows(SSM_W, 0, rev=nblk), _full(blk), _full(blk), _full(blk), _full(blk), pw, pw,
                   _full((1, SSM_W)), _full((SSM_W, 2 * SSM_W))],
        out_shape=[jax.ShapeDtypeStruct((t, SSM_W), BF16)] + [jax.ShapeDtypeStruct(blk, F32)] * 4
        + [jax.ShapeDtypeStruct((SUB, STATES), F32)] * 2
        + [jax.ShapeDtypeStruct((1, SSM_W), F32), jax.ShapeDtypeStruct((SSM_W, 2 * SSM_W), F32)],
        scratch_shapes=[big(), big(), big(), big(), big(), big(), small(), small(), small(), small(),
                        pltpu.VMEM((TM, SSM_W), F32), pltpu.VMEM((TM, SSM_W), F32)],
        compiler_params=_params())(dycat, proj, car_in_re, car_in_im, mb_re, mb_im, pw_re, pw_im, pr_re, pr_im,
                                   cm_re, cm_im, dskip, wglu)


def _adamw(w, g, m, v, name):
    rows = w.shape[0]
    tr = 256 if rows % 256 == 0 else rows
    c1 = 1.0 / (1.0 - ADAM_B1 ** ADAM_STEP)
    c2 = 1.0 / (1.0 - ADAM_B2 ** ADAM_STEP)

    def body(w_ref, g_ref, m_ref, v_ref, d_ref, nm_ref, nv_ref):
        gv = g_ref[...]
        m = ADAM_B1 * m_ref[...] + (1.0 - ADAM_B1) * gv
        v = ADAM_B2 * v_ref[...] + (1.0 - ADAM_B2) * (gv * gv)
        nm_ref[...] = m
        nv_ref[...] = v
        d_ref[...] = -ADAM_LR * ((m * c1) / (jnp.sqrt(v * c2) + ADAM_EPS) + ADAM_WD * w_ref[...])

    spec = pl.BlockSpec((tr, D_MODEL), lambda i: (i, 0))
    shp = jax.ShapeDtypeStruct((rows, D_MODEL), F32)
    return pl.pallas_call(
        body, name=name, grid=(rows // tr,), in_specs=[spec] * 4, out_specs=[spec] * 3, out_shape=[shp] * 3,
        compiler_params=_params())(w, g, m, v)


def _add2(a, b, name, out_dtype):
    n, rows, _ = a.shape
    spec = pl.BlockSpec((1, 256, D_MODEL), lambda i, j: (i, j, 0))

    def body(a_ref, b_ref, o_ref):
        o_ref[...] = (a_ref[...] + b_ref[...]).astype(out_dtype)

    return pl.pallas_call(
        body, name=name, grid=(n, rows // 256), in_specs=[spec, spec], out_specs=spec,
        out_shape=jax.ShapeDtypeStruct(a.shape, out_dtype), compiler_params=_params(2))(a, b)


def _sum_lead(a, name):
    n, rows, _ = a.shape
    tr = 256 if rows % 256 == 0 else rows

    def body(a_ref, o_ref):
        acc = a_ref[0].astype(F32)
        for k in range(1, n):
            acc = acc + a_ref[k].astype(F32)
        o_ref[...] = acc

    return pl.pallas_call(
        body, name=name, grid=(rows // tr,),
        in_specs=[pl.BlockSpec((n, tr, D_MODEL), lambda i: (0, i, 0))],
        out_specs=pl.BlockSpec((tr, D_MODEL), lambda i: (i, 0)),
        out_shape=jax.ShapeDtypeStruct((rows, D_MODEL), F32), compiler_params=_params())(a)


ANY = pl.BlockSpec(memory_space=pl.ANY)


def _mesh_pos():
    return lax.axis_index("x"), lax.axis_index("y"), lax.axis_index("c")


def _gather_weights(wb):
    rows = wb.shape[0]
    half = rows // 2
    ch = half // COMM_CHUNKS
    ncopy = 3 * COMM_CHUNKS

    def body(w_ref, o_ref, send_sems, recv_sems, local_sem):
        x, y, c = _mesh_pos()
        me = 2 * x + y
        sibling = (x, y, 1 - c)
        chips = [(1 - x, y), (x, 1 - y), (1 - x, 1 - y)]
        ids = [2 * chip[0] + chip[1] for chip in chips]

        def piece(q, h, k):
            return o_ref.at[q, pl.ds(h * half + k * ch, ch), :]

        def copy(s, q, h, k, to, src=None):
            return pltpu.make_async_remote_copy(
                src_ref=piece(q, h, k) if src is None else src, dst_ref=piece(q, h, k),
                send_sem=send_sems.at[s], recv_sem=recv_sems.at[s], device_id=to, device_id_type=MESH)

        mine = pltpu.make_async_copy(w_ref, o_ref.at[me], local_sem)
        mine.start()
        first = [copy(j * COMM_CHUNKS + k, me, c, k, (*chip, c), src=w_ref.at[pl.ds(c * half + k * ch, ch), :])
                 for j, chip in enumerate(chips) for k in range(COMM_CHUNKS)]
        for cp in first:
            cp.start()
        passed = []
        for j in range(3):
            for k in range(COMM_CHUNKS):
                s = j * COMM_CHUNKS + k
                copy(s, ids[j], c, k, (x, y, c)).wait_recv()
                fwd = copy(ncopy + s, ids[j], c, k, sibling)
                fwd.start()
                passed.append(fwd)
        for j in range(3):
            for k in range(COMM_CHUNKS):
                copy(ncopy + j * COMM_CHUNKS + k, ids[j], 1 - c, k, (x, y, c)).wait_recv()
        for cp in first + passed:
            cp.wait_send()
        mine.wait()

    return pl.pallas_call(
        body, name="gather_weights", in_specs=[ANY], out_specs=ANY,
        out_shape=jax.ShapeDtypeStruct((4, rows, D_MODEL), wb.dtype),
        scratch_shapes=[pltpu.SemaphoreType.DMA((2 * ncopy,)), pltpu.SemaphoreType.DMA((2 * ncopy,)),
                        pltpu.SemaphoreType.DMA],
    )(wb)


def _gather_all(v):
    def body(v_ref, o_ref, send_sems, recv_sems, local_sem):
        x, y, c = _mesh_pos()
        sibling = (x, y, 1 - c)
        chips = [(1 - x, y), (x, 1 - y), (1 - x, 1 - y)]

        def blk(px, py, pc):
            return o_ref.at[4 * px + 2 * py + pc]

        def copy(k, block, to, src=None):
            return pltpu.make_async_remote_copy(
                src_ref=blk(*block) if src is None else src, dst_ref=blk(*block),
                send_sem=send_sems.at[k], recv_sem=recv_sems.at[k], device_id=to, device_id_type=MESH)

        mine = pltpu.make_async_copy(v_ref, blk(x, y, c), local_sem)
        mine.start()
        first = [copy(0, (x, y, c), sibling, src=v_ref)]
        first += [copy(1 + j, (x, y, c), (*chip, c), src=v_ref) for j, chip in enumerate(chips)]
        for cp in first:
            cp.start()
        passed = [copy(4 + j, (*chip, c), sibling) for j, chip in enumerate(chips)]
        for j, chip in enumerate(chips):
            copy(1 + j, (*chip, c), (x, y, c)).wait_recv()
            passed[j].start()
        copy(0, (x, y, 1 - c), (x, y, c)).wait_recv()
        for j, chip in enumerate(chips):
            copy(4 + j, (*chip, 1 - c), (x, y, c)).wait_recv()
        for cp in first + passed:
            cp.wait_send()
        mine.wait()

    return pl.pallas_call(
        body, name="gather_all", in_specs=[ANY], out_specs=ANY,
        out_shape=jax.ShapeDtypeStruct((8,) + v.shape, v.dtype),
        scratch_shapes=[pltpu.SemaphoreType.DMA((7,)), pltpu.SemaphoreType.DMA((7,)), pltpu.SemaphoreType.DMA],
    )(v)


def _pair_split(g):
    n, rows, _ = g.shape
    half = rows // 2
    ch = half // COMM_CHUNKS

    def body(g_ref, mine_ref, theirs_ref, send_sems, recv_sems, local_sems):
        x, y, c = _mesh_pos()
        keeps = [pltpu.make_async_copy(g_ref.at[q, pl.ds(c * half, half), :], mine_ref.at[q], local_sems.at[q])
                 for q in range(n)]
        for cp in keeps:
            cp.start()
        gives = [pltpu.make_async_remote_copy(
            src_ref=g_ref.at[q, pl.ds((1 - c) * half + k * ch, ch), :],
            dst_ref=theirs_ref.at[q, pl.ds(k * ch, ch), :],
            send_sem=send_sems.at[q * COMM_CHUNKS + k], recv_sem=recv_sems.at[q * COMM_CHUNKS + k],
            device_id=(x, y, 1 - c), device_id_type=MESH) for q in range(n) for k in range(COMM_CHUNKS)]
        for cp in gives:
            cp.start()
        for cp in gives:
            cp.wait()
        for cp in keeps:
            cp.wait()

    shp = jax.ShapeDtypeStruct((n, half, D_MODEL), g.dtype)
    return pl.pallas_call(
        body, name="pair_split", in_specs=[ANY], out_specs=[ANY, ANY], out_shape=[shp, shp],
        scratch_shapes=[pltpu.SemaphoreType.DMA((n * COMM_CHUNKS,)), pltpu.SemaphoreType.DMA((n * COMM_CHUNKS,)),
                        pltpu.SemaphoreType.DMA((n,))],
    )(g)


def _chip_scatter(p):
    def body(p_ref, o_ref, send_sems, recv_sems, local_sem):
        x, y, c = _mesh_pos()
        me = 2 * x + y
        chips = [(1 - x, y), (x, 1 - y), (1 - x, 1 - y)]
        keep = pltpu.make_async_copy(p_ref.at[me], o_ref.at[me], local_sem)
        keep.start()
        sends = [pltpu.make_async_remote_copy(
            src_ref=p_ref.at[2 * chip[0] + chip[1]], dst_ref=o_ref.at[me],
            send_sem=send_sems.at[j], recv_sem=recv_sems.at[j], device_id=(*chip, c), device_id_type=MESH)
            for j, chip in enumerate(chips)]
        for cp in sends:
            cp.start()
        for j, chip in enumerate(chips):
            q = 2 * chip[0] + chip[1]
            pltpu.make_async_remote_copy(
                src_ref=p_ref.at[q], dst_ref=o_ref.at[q], send_sem=send_sems.at[j], recv_sem=recv_sems.at[j],
                device_id=(*chip, c), device_id_type=MESH).wait_recv()
        for cp in sends:
            cp.wait_send()
        keep.wait()

    return pl.pallas_call(
        body, name="chip_scatter", in_specs=[ANY], out_specs=ANY, out_shape=jax.ShapeDtypeStruct(p.shape, p.dtype),
        scratch_shapes=[pltpu.SemaphoreType.DMA((3,)), pltpu.SemaphoreType.DMA((3,)), pltpu.SemaphoreType.DMA],
    )(p)


def _pair_join(r):
    rows = r.shape[0]
    ch = rows // COMM_CHUNKS

    def body(r_ref, o_ref, send_sems, recv_sems, local_sem):
        x, y, c = _mesh_pos()
        keep = pltpu.make_async_copy(r_ref, o_ref.at[pl.ds(c * rows, rows), :], local_sem)
        keep.start()

        def copy(k, h):
            return pltpu.make_async_remote_copy(
                src_ref=r_ref.at[pl.ds(k * ch, ch), :], dst_ref=o_ref.at[pl.ds(h * rows + k * ch, ch), :],
                send_sem=send_sems.at[k], recv_sem=recv_sems.at[k], device_id=(x, y, 1 - c), device_id_type=MESH)

        gives = [copy(k, c) for k in range(COMM_CHUNKS)]
        for cp in gives:
            cp.start()
        for cp in gives:
            cp.wait_send()
        for k in range(COMM_CHUNKS):
            copy(k, 1 - c).wait_recv()
        keep.wait()

    return pl.pallas_call(
        body, name="pair_join", in_specs=[ANY], out_specs=ANY,
        out_shape=jax.ShapeDtypeStruct((2 * rows, D_MODEL), r.dtype),
        scratch_shapes=[pltpu.SemaphoreType.DMA((COMM_CHUNKS,)), pltpu.SemaphoreType.DMA((COMM_CHUNKS,)),
                        pltpu.SemaphoreType.DMA],
    )(r)


SHARD_BIG = (("even_w_in", (1024, 512)), ("ssm_w_glu", (512, 256)), ("even_w_out", (256, 1024)),
             ("odd_w_in", (1024, 768)), ("odd_w_out", (256, 1024)))
SHARD_SMALL = (("odd_norm", 1), ("conv_w", CONV_K), ("conv_b", 1), ("conv_ln_g", 1), ("conv_ln_b", 1))
REP_NAMES = (("even_norm", (1024,)), ("pool_w", (4, 128, 128)), ("pool_scale", (512,)), ("ssm_log_dt", (32,)),
             ("ssm_a_re", (32, 64)), ("ssm_a_im", (32, 64)), ("ssm_b_re", (32, 64, 16)), ("ssm_b_im", (32, 64, 16)),
             ("ssm_c_re", (32, 16, 64)), ("ssm_c_im", (32, 16, 64)), ("ssm_d", (512,)), ("final_norm", (1024,)))


def _pack_shard(d):
    big = [d[n].reshape(-1, D_MODEL) for n, _ in SHARD_BIG]
    small = jnp.concatenate([d[n].reshape(r, 256) for n, r in SHARD_SMALL], axis=0).reshape(-1)
    small = jnp.pad(small, (0, ROWS_SMALL * D_MODEL - small.shape[0])).reshape(ROWS_SMALL, D_MODEL)
    pad = jnp.zeros((ROWS_PACK - ROWS_BIG - ROWS_SMALL, D_MODEL), F32)
    return jnp.concatenate(big + [small, pad], axis=0)


def _unpack_shard(buf):
    out = {}
    off = 0
    for n, shp in SHARD_BIG:
        rows = shp[0] * shp[1] // D_MODEL
        out[n] = buf[off:off + rows].reshape(shp)
        off += rows
    small = buf[ROWS_BIG:ROWS_BIG + ROWS_SMALL].reshape(-1)[:35 * 256].reshape(35, 256)
    off = 0
    for n, r in SHARD_SMALL:
        out[n] = small[off:off + r].reshape((r, 256) if r > 1 else (256,))
        off += r
    return out


def _pack_rep(d):
    flat = jnp.concatenate([d[n].reshape(-1) for n, _ in REP_NAMES])
    return jnp.pad(flat, (0, REP_ROWS * D_MODEL - flat.shape[0])).reshape(REP_ROWS, D_MODEL)


def _unpack_rep(buf):
    flat = buf.reshape(-1)
    out = {}
    off = 0
    for n, shp in REP_NAMES:
        size = 1
        for s in shp:
            size *= s
        out[n] = flat[off:off + size].reshape(shp)
        off += size
    return out


def _cols_full(g4, rows, cols):
    return g4.reshape(4, rows, cols).transpose(1, 0, 2).reshape(rows, 4 * cols)


def _cols_split(full, cols):
    rows = full.shape[0]
    return full.reshape(rows, 4, cols).transpose(1, 0, 2).reshape(4, -1, D_MODEL)


def _block_diag(a):
    a = a.reshape(4, 8, GROUP_DIM, N_STATE)
    eye = jnp.eye(8, dtype=a.dtype)
    return (a[:, :, :, None, :] * eye[None, :, None, :, None]).reshape(4, 128, LCH)


def _block_diag_take(m):
    m = m.reshape(4, 8, GROUP_DIM, 8, N_STATE)
    eye = jnp.eye(8, dtype=m.dtype)
    return jnp.sum(m * eye[None, :, None, :, None], axis=3).reshape(N_GROUPS, GROUP_DIM, N_STATE)


def _ssm_discretise(log_dt, a_re, a_im, b_re, b_im):
    dt = jnp.exp(log_dt)[:, None]
    mag = jnp.exp(a_re * dt)
    ang = a_im * dt
    abar_re = mag * jnp.cos(ang)
    abar_im = mag * jnp.sin(ang)
    den = a_re * a_re + a_im * a_im
    nr = abar_re - 1.0
    ni = abar_im
    k_re = (nr * a_re + ni * a_im) / den
    k_im = (ni * a_re - nr * a_im) / den
    bb_re = k_re[..., None] * b_re - k_im[..., None] * b_im
    bb_im = k_re[..., None] * b_im + k_im[..., None] * b_re
    return abar_re, abar_im, bb_re, bb_im


def _powers(log_dt, a_re, a_im):
    dt = jnp.exp(log_dt)[:, None]
    k = jnp.arange(1, SUB + 1, dtype=F32)[:, None, None]
    mag = jnp.exp(k * (a_re * dt)[None])
    ang = k * (a_im * dt)[None]
    return (mag * jnp.cos(ang)).reshape(SUB, STATES), (mag * jnp.sin(ang)).reshape(SUB, STATES)


def _local_step(x, tgt, w):
    row = lambda a: a.reshape(1, -1)
    e_w_in = w["even_w_in"].astype(BF16)
    e_w_out = w["even_w_out"].astype(BF16)
    o_w_in = w["odd_w_in"].astype(BF16)
    o_w_out = w["odd_w_out"].astype(BF16)
    wglu = w["ssm_w_glu"].astype(BF16)
    wp = w["pool_w"].astype(BF16)
    ssm_in = (w["ssm_log_dt"], w["ssm_a_re"], w["ssm_a_im"], w["ssm_b_re"], w["ssm_b_im"])
    (abar_re, abar_im, bb_re, bb_im), ssm_vjp = jax.vjp(_ssm_discretise, *ssm_in)
    mb_re = _block_diag(bb_re.transpose(0, 2, 1)).astype(BF16)
    mb_im = _block_diag(bb_im.transpose(0, 2, 1)).astype(BF16)
    cm_re = _block_diag(w["ssm_c_re"]).astype(BF16)
    cm_im = _block_diag(w["ssm_c_im"]).astype(BF16)
    pw_re, pw_im = _powers(w["ssm_log_dt"], w["ssm_a_re"], w["ssm_a_im"])
    pr_re, pr_im = pw_re[::-1], -pw_im[::-1]
    cw = jnp.pad(w["conv_w"], ((0, HALO - CONV_K), (0, 0)))
    g0, g1, gf = row(w["even_norm"]), row(w["odd_norm"]), row(w["final_norm"])
    ps, dskip = row(w["pool_scale"]), row(w["ssm_d"])
    cb, lg, lb = row(w["conv_b"]), row(w["conv_ln_g"]), row(w["conv_ln_b"])

    proj = _norm_in(x, g0, e_w_in, "even_in")
    yp = _pool_fwd(proj, wp, ps)
    ys, car_re, car_im = _ssm_fwd(proj, mb_re, mb_im, pw_re, pw_im, cm_re, cm_im, dskip, wglu)
    x1, yg = _even_out(yp, ys, proj, x, e_w_out)
    q = _norm_in(x1, g1, o_w_in, "odd_in")
    y2, cv = _conv_fwd(q, cw, cb, lg, lb)
    dx2, loss_lanes, d_gf = _odd_out_loss(y2, x1, o_w_out, gf, tgt)

    dcv, dz2, d_o_w_out, d_lg, d_lb = _odd_bwd_out(dx2, o_w_out, y2, cv, q, lg, lb)
    dval, dgate, d_cw, d_cb = _conv_bwd(dcv, q, cw)
    dx1, d_g1, d_o_w_in = _in_bwd([dval, dgate, dz2], o_w_in, x1, g1, dx2, "odd_in_bwd")
    dycat, dz, d_e_w_out = _even_bwd_out(dx1, e_w_out, yg, yp, ys, proj)
    dup, d_wp, d_ps = _pool_bwd(dycat, proj, wp, ps)
    (dus, d_mb_re, d_mb_im, d_cm_re, d_cm_im, da_re, da_im, d_dskip, d_wglu) = _ssm_bwd(
        dycat, proj, car_re, car_im, mb_re, mb_im, pw_re, pw_im, pr_re, pr_im, cm_re, cm_im, dskip, wglu)
    dx, d_g0, d_e_w_in = _in_bwd([dup, dus, dz], e_w_in, x, g0, dx1, "even_in_bwd")

    d_abar_re = jnp.sum(da_re, axis=0).reshape(N_GROUPS, N_STATE)
    d_abar_im = jnp.sum(da_im, axis=0).reshape(N_GROUPS, N_STATE)
    d_bb_re = _block_diag_take(d_mb_re).transpose(0, 2, 1)
    d_bb_im = _block_diag_take(d_mb_im).transpose(0, 2, 1)
    d_log_dt, d_a_re, d_a_im, d_b_re, d_b_im = ssm_vjp((d_abar_re, d_abar_im, d_bb_re, d_bb_im))

    grads = {
        "even_norm": d_g0.reshape(-1), "even_w_in": d_e_w_in, "pool_w": d_wp, "pool_scale": d_ps.reshape(-1),
        "ssm_log_dt": d_log_dt, "ssm_a_re": d_a_re, "ssm_a_im": d_a_im, "ssm_b_re": d_b_re, "ssm_b_im": d_b_im,
        "ssm_c_re": _block_diag_take(d_cm_re), "ssm_c_im": _block_diag_take(d_cm_im),
        "ssm_d": d_dskip.reshape(-1), "ssm_w_glu": d_wglu, "even_w_out": d_e_w_out, "odd_norm": d_g1.reshape(-1),
        "odd_w_in": d_o_w_in, "conv_w": d_cw[:CONV_K], "conv_b": d_cb.reshape(-1), "conv_ln_g": d_lg.reshape(-1),
        "conv_ln_b": d_lb.reshape(-1), "odd_w_out": d_o_w_out, "final_norm": d_gf.reshape(-1),
    }
    return jnp.sum(loss_lanes), dx, grads


WEIGHT_NAMES = ("even_norm", "even_w_in", "pool_w", "pool_scale", "ssm_log_dt", "ssm_a_re", "ssm_a_im",
                "ssm_b_re", "ssm_b_im", "ssm_c_re", "ssm_c_im", "ssm_d", "ssm_w_glu", "even_w_out", "odd_norm",
                "odd_w_in", "conv_w", "conv_b", "conv_ln_g", "conv_ln_b", "odd_w_out", "final_norm")
SHARDED = tuple(n for n, _ in SHARD_BIG) + tuple(n for n, _ in SHARD_SMALL)


def _full_weights(shard, rep):
    pack = _pack_shard(shard)
    big = pack[:ROWS_BIG].astype(BF16)
    small = lax.bitcast_convert_type(pack[ROWS_BIG:ROWS_BIG + ROWS_SMALL], BF16).reshape(2 * ROWS_SMALL, D_MODEL)
    pad = jnp.zeros((ROWS_PACK - ROWS_BIG - 2 * ROWS_SMALL, D_MODEL), BF16)
    got = _gather_weights(jnp.concatenate([big, small, pad], axis=0))
    gb = got[:, :ROWS_BIG]
    gs = lax.bitcast_convert_type(got[:, ROWS_BIG:ROWS_BIG + 2 * ROWS_SMALL].reshape(4, ROWS_SMALL, D_MODEL, 2), F32)
    gs = gs.reshape(4, -1)[:, :35 * 256].reshape(4, 35, 256)
    w = dict(rep)
    w["even_w_in"] = _cols_full(gb[:, 0:512], 1024, 512)
    w["ssm_w_glu"] = _cols_full(gb[:, 512:640], 512, 256)
    w["even_w_out"] = gb[:, 640:896].reshape(1024, 1024)
    w["odd_w_in"] = _cols_full(gb[:, 896:1664], 1024, 768)
    w["odd_w_out"] = gb[:, 1664:1920].reshape(1024, 1024)
    w["odd_norm"] = gs[:, 0].reshape(-1)
    w["conv_w"] = gs[:, 1:32].transpose(1, 0, 2).reshape(CONV_K, 1024)
    w["conv_b"] = gs[:, 32].reshape(-1)
    w["conv_ln_g"] = gs[:, 33].reshape(-1)
    w["conv_ln_b"] = gs[:, 34].reshape(-1)
    return w


def _pack_grads(g):
    small = jnp.concatenate([g["odd_norm"].reshape(4, 1, 256), g["conv_w"].reshape(CONV_K, 4, 256).transpose(1, 0, 2),
                             g["conv_b"].reshape(4, 1, 256), g["conv_ln_g"].reshape(4, 1, 256),
                             g["conv_ln_b"].reshape(4, 1, 256)], axis=1).reshape(4, -1)
    small = jnp.pad(small, ((0, 0), (0, ROWS_SMALL * D_MODEL - small.shape[1]))).reshape(4, ROWS_SMALL, D_MODEL)
    parts = [_cols_split(g["even_w_in"], 512), _cols_split(g["ssm_w_glu"], 256), g["even_w_out"].reshape(4, 256, 1024),
             _cols_split(g["odd_w_in"], 768), g["odd_w_out"].reshape(4, 256, 1024), small,
             jnp.zeros((4, ROWS_PACK - ROWS_BIG - ROWS_SMALL, D_MODEL), F32)]
    return jnp.concatenate(parts, axis=1)


def kernel(x, even_norm, even_w_in, pool_w, pool_scale, ssm_log_dt, ssm_a_re, ssm_a_im, ssm_b_re, ssm_b_im, ssm_c_re, ssm_c_im, ssm_d, ssm_w_glu, even_w_out, odd_norm, odd_w_in, conv_w, conv_b, conv_ln_g, conv_ln_b, odd_w_out, final_norm, loss_target, m_even_norm, m_even_w_in, m_pool_w, m_pool_scale, m_ssm_log_dt, m_ssm_a_re, m_ssm_a_im, m_ssm_b_re, m_ssm_b_im, m_ssm_c_re, m_ssm_c_im, m_ssm_d, m_ssm_w_glu, m_even_w_out, m_odd_norm, m_odd_w_in, m_conv_w, m_conv_b, m_conv_ln_g, m_conv_ln_b, m_odd_w_out, m_final_norm, v_even_norm, v_even_w_in, v_pool_w, v_pool_scale, v_ssm_log_dt, v_ssm_a_re, v_ssm_a_im, v_ssm_b_re, v_ssm_b_im, v_ssm_c_re, v_ssm_c_im, v_ssm_d, v_ssm_w_glu, v_even_w_out, v_odd_norm, v_odd_w_in, v_conv_w, v_conv_b, v_conv_ln_g, v_conv_ln_b, v_odd_w_out, v_final_norm):
    ws = dict(zip(WEIGHT_NAMES, (even_norm, even_w_in, pool_w, pool_scale, ssm_log_dt, ssm_a_re, ssm_a_im, ssm_b_re,
                                 ssm_b_im, ssm_c_re, ssm_c_im, ssm_d, ssm_w_glu, even_w_out, odd_norm, odd_w_in,
                                 conv_w, conv_b, conv_ln_g, conv_ln_b, odd_w_out, final_norm)))
    ms = dict(zip(WEIGHT_NAMES, (m_even_norm, m_even_w_in, m_pool_w, m_pool_scale, m_ssm_log_dt, m_ssm_a_re,
                                 m_ssm_a_im, m_ssm_b_re, m_ssm_b_im, m_ssm_c_re, m_ssm_c_im, m_ssm_d, m_ssm_w_glu,
                                 m_even_w_out, m_odd_norm, m_odd_w_in, m_conv_w, m_conv_b, m_conv_ln_g, m_conv_ln_b,
                                 m_odd_w_out, m_final_norm)))
    vs = dict(zip(WEIGHT_NAMES, (v_even_norm, v_even_w_in, v_pool_w, v_pool_scale, v_ssm_log_dt, v_ssm_a_re,
                                 v_ssm_a_im, v_ssm_b_re, v_ssm_b_im, v_ssm_c_re, v_ssm_c_im, v_ssm_d, v_ssm_w_glu,
                                 v_even_w_out, v_odd_norm, v_odd_w_in, v_conv_w, v_conv_b, v_conv_ln_g, v_conv_ln_b,
                                 v_odd_w_out, v_final_norm)))
    lead = {n: a.shape for n, a in ws.items()}
    drop = lambda d: {n: (a[0] if n != "final_norm" else a) for n, a in d.items()}
    ws, ms, vs = drop(ws), drop(ms), drop(vs)

    shard = {n: ws[n] for n in SHARDED}
    rep = {n: ws[n] for n, _ in REP_NAMES}
    w_full = _full_weights(shard, rep)
    loss_part, grad_x, grads = _local_step(x[0], loss_target[0], w_full)
    loss = lax.psum(loss_part, ("x", "y", "c"))

    mine, theirs = _pair_split(_pack_grads(grads))
    got = _chip_scatter(_add2(mine, theirs, "pair_add", BF16))
    g_shard = _pair_join(_sum_lead(got, "chip_sum"))
    d_shard, m_shard, v_shard = _adamw(_pack_shard(shard), g_shard, _pack_shard({n: ms[n] for n in SHARDED}),
                                       _pack_shard({n: vs[n] for n in SHARDED}), "adamw_shard")
    g_rep = _sum_lead(_gather_all(_pack_rep({n: grads[n] for n, _ in REP_NAMES})), "rep_sum")
    d_rep, m_rep, v_rep = _adamw(_pack_rep(rep), g_rep, _pack_rep({n: ms[n] for n, _ in REP_NAMES}),
                                 _pack_rep({n: vs[n] for n, _ in REP_NAMES}), "adamw_rep")

    def unpack(shard_buf, rep_buf):
        d = {**_unpack_shard(shard_buf), **_unpack_rep(rep_buf)}
        return [d[n].reshape(lead[n]) for n in WEIGHT_NAMES]

    return (loss, grad_x[None], *unpack(g_shard, g_rep), *unpack(d_shard, d_rep),
            *unpack(m_shard, m_rep), *unpack(v_shard, v_rep))
```

```python
import functools

import jax
import jax.numpy as jnp
from jax import lax
from jax.experimental import pallas as pl
from jax.experimental.pallas import tpu as pltpu

F32 = jnp.float32
BF16 = jnp.bfloat16
MESH = pl.DeviceIdType.MESH

D_MODEL = 1024
RMS_EPS = 1e-6
LN_EPS = 1e-5
N_GROUPS = 32
GROUP_DIM = 16
N_STATE = 64
STATES = N_GROUPS * N_STATE
SSM_W = 512
POOL_W = 512
CONV_K = 31
HALO = 32
POOL_HALO = 16

ADAM_LR = 0.001
ADAM_B1 = 0.9
ADAM_B2 = 0.999
ADAM_EPS = 1e-08
ADAM_WD = 0.01
ADAM_STEP = 10

TM = 256
SUB = 8
LCH = 512
VMEM_LIMIT = 56 * 1024 * 1024

ROWS_BIG = 1920
ROWS_SMALL = 16
ROWS_PACK = 2048
REP_ROWS = 200
COMM_CHUNKS = 4


def _params(n_axes=1):
    return pltpu.CompilerParams(dimension_semantics=("arbitrary",) * n_axes, vmem_limit_bytes=VMEM_LIMIT)


def _rows(w, cb=0, rev=None, tm=TM):
    if rev is None:
        return pl.BlockSpec((tm, w), lambda i: (i, cb))
    return pl.BlockSpec((tm, w), lambda i: (rev - 1 - i, cb))


def _full(shape):
    n = len(shape)
    return pl.BlockSpec(shape, lambda i: (0,) * n)


def _prev(hr, w, cb=0, tm=TM):
    r = tm // hr
    return pl.BlockSpec((hr, w), lambda i: (jnp.maximum(i * r - 1, 0), cb))


def _next(hr, w, nrows, cb=0, tm=TM):
    r = tm // hr
    last = nrows // hr - 1
    return pl.BlockSpec((hr, w), lambda i: (jnp.minimum((i + 1) * r, last), cb))


def _dot(a, b):
    return jnp.dot(a, b, preferred_element_type=F32)


def _dot_nt(a, b):
    return lax.dot_general(a, b, (((1,), (1,)), ((), ())), preferred_element_type=F32)


def _dot_tn(a, b):
    return lax.dot_general(a, b, (((0,), (0,)), ((), ())), preferred_element_type=F32)


def _sig(x):
    return 1.0 / (1.0 + jnp.exp(-x))


def _zero_at_first(i, *refs):
    @pl.when(i == 0)
    def _():
        for r in refs:
            r[...] = jnp.zeros_like(r)


def _norm_in(x, g, w, name):
    t, n = x.shape[0], w.shape[1]

    def body(x_ref, g_ref, w_ref, o_ref):
        xv = x_ref[...]
        r = lax.rsqrt(jnp.mean(xv * xv, axis=-1, keepdims=True) + RMS_EPS)
        o_ref[...] = _dot((xv * r * g_ref[...]).astype(BF16), w_ref[...]).astype(BF16)

    return pl.pallas_call(
        body, name=name, grid=(t // TM,),
        in_specs=[_rows(D_MODEL), _full((1, D_MODEL)), _full(w.shape)],
        out_specs=_rows(n), out_shape=jax.ShapeDtypeStruct((t, n), BF16),
        compiler_params=_params())(x, g, w)


def _pool_sums(ext, g, forward):
    n = ext.shape[0]
    s = ext
    for step in range(g + 1):
        k = 1 << step
        s = s + pltpu.roll(s, k if forward else n - k, 0)
    return s


def _pool_fwd(proj, wp, ps):
    t = proj.shape[0]

    def body(u_ref, h_ref, wp_ref, ps_ref, y_ref):
        i = pl.program_id(0)
        pos = (i * TM + 1 + lax.broadcasted_iota(jnp.int32, (TM, 1), 0)).astype(F32)
        for g in range(4):
            sl = slice(128 * g, 128 * (g + 1))
            u = u_ref[:, sl].astype(F32)
            halo = jnp.where(i == 0, 0.0, h_ref[:, sl].astype(F32))
            s = _pool_sums(jnp.concatenate([halo, u], axis=0), g, True)[POOL_HALO:, :]
            pooled = s / jnp.minimum(pos, float(2 << g)) - u
            y_ref[:, sl] = _dot(pooled.astype(BF16), wp_ref[g]) * ps_ref[:, sl]

    return pl.pallas_call(
        body, name="pool_fwd", grid=(t // TM,),
        in_specs=[_rows(POOL_W, 0), _prev(POOL_HALO, POOL_W, 0), _full((4, 128, 128)), _full((1, POOL_W))],
        out_specs=_rows(POOL_W), out_shape=jax.ShapeDtypeStruct((t, POOL_W), F32),
        compiler_params=_params())(proj, proj, wp, ps)


SEG_LEN = TM // SUB


def _perm_matrix():
    p = jnp.arange(TM)
    src = (p % SUB) * SEG_LEN + p // SUB
    return (src[:, None] == jnp.arange(TM)[None, :]).astype(BF16)


def _cmul_add(are, aim, vre, vim, bre, bim):
    return are * vre - aim * vim + bre, are * vim + aim * vre + bim


def _segment_chain(ere, eim, qre, qim, cin_re, cin_im, row, up):
    for sh in (1, 2, 4):
        mre, mim = (qre[SUB - sh:SUB - sh + 1, :], qim[SUB - sh:SUB - sh + 1, :]) if up else \
                   (qre[sh - 1:sh, :], qim[sh - 1:sh, :])
        keep = (row < SUB - sh) if up else (row >= sh)
        sre = jnp.where(keep, pltpu.roll(ere, SUB - sh if up else sh, 0), 0.0)
        sim = jnp.where(keep, pltpu.roll(eim, SUB - sh if up else sh, 0), 0.0)
        ere, eim = _cmul_add(mre, mim, sre, sim, ere, eim)
    ere, eim = _cmul_add(qre, qim, cin_re, cin_im, ere, eim)
    keep = (row < SUB - 1) if up else (row >= 1)
    ent_re = jnp.where(keep, pltpu.roll(ere, SUB - 1 if up else 1, 0), cin_re)
    ent_im = jnp.where(keep, pltpu.roll(eim, SUB - 1 if up else 1, 0), cin_im)
    return ere, eim, ent_re, ent_im


def _scan_fwd_block(xs_re, xs_im, p8_re, p8_im, q_re, q_im, car_re, car_im, ent_re_ref, ent_im_ref):
    row = lax.broadcasted_iota(jnp.int32, (SUB, LCH), 0)
    for j in range(STATES // LCH):
        sl = slice(LCH * j, LCH * (j + 1))
        are, aim = p8_re[0:SUB, sl], p8_im[0:SUB, sl]

        def totals(i, v, sl=sl, are=are, aim=aim):
            r0 = pl.multiple_of(i * SUB, SUB)
            vre, vim = _cmul_add(are, aim, v[0], v[1], xs_re[pl.ds(r0, SUB), sl], xs_im[pl.ds(r0, SUB), sl])
            xs_re[pl.ds(r0, SUB), sl] = vre
            xs_im[pl.ds(r0, SUB), sl] = vim
            return vre, vim

        ere, eim = lax.fori_loop(1, SEG_LEN, totals, (xs_re[0:SUB, sl], xs_im[0:SUB, sl]), unroll=2)
        ere, eim, cre, cim = _segment_chain(ere, eim, q_re[:, sl], q_im[:, sl],
                                            car_re[:, sl], car_im[:, sl], row, False)
        car_re[:, sl] = jnp.broadcast_to(ere[SUB - 1:SUB, :], (SUB, LCH))
        car_im[:, sl] = jnp.broadcast_to(eim[SUB - 1:SUB, :], (SUB, LCH))
        if ent_re_ref is not None:
            ent_re_ref[:, sl] = cre
            ent_im_ref[:, sl] = cim

        def fix(i, c, sl=sl, cre=cre, cim=cim):
            r0 = pl.multiple_of(i * SUB, SUB)
            vre, vim = _cmul_add(p8_re[pl.ds(r0, SUB), sl], p8_im[pl.ds(r0, SUB), sl], cre, cim,
                                 xs_re[pl.ds(r0, SUB), sl], xs_im[pl.ds(r0, SUB), sl])
            xs_re[pl.ds(r0, SUB), sl] = vre
            xs_im[pl.ds(r0, SUB), sl] = vim
            return c

        lax.fori_loop(0, SEG_LEN, fix, 0, unroll=2)


def _unpermute(pmt_ref, v):
    hi = v.astype(BF16)
    lo = (v - hi.astype(F32)).astype(BF16)
    return _dot(pmt_ref[...], hi) + _dot(pmt_ref[...], lo)


def _ssm_fwd(proj, pm, pmt, mb_re, mb_im, p8_re, p8_im, q_re, q_im, cm_re, cm_im, dskip, wglu):
    t = proj.shape[0]
    nblk = t // TM

    def body(u_ref, pm_ref, pmt_ref, mbre, mbim, p8re, p8im, qre, qim, cmre, cmim, d_ref, wg_ref,
             y_ref, cre_ref, cim_ref, xs_re, xs_im, car_re, car_im, ysk):
        i = pl.program_id(0)
        _zero_at_first(i, car_re, car_im)
        cre_ref[0] = car_re[...]
        cim_ref[0] = car_im[...]
        us = _dot(pm_ref[...], u_ref[...])
        usb = us.astype(BF16)
        for j in range(4):
            xs_re[:, LCH * j:LCH * (j + 1)] = _dot(usb[:, 128 * j:128 * (j + 1)], mbre[j])
            xs_im[:, LCH * j:LCH * (j + 1)] = _dot(usb[:, 128 * j:128 * (j + 1)], mbim[j])
        _scan_fwd_block(xs_re, xs_im, p8re, p8im, qre, qim, car_re, car_im, None, None)
        for j in range(4):
            sl = slice(LCH * j, LCH * (j + 1))
            ysk[:, 128 * j:128 * (j + 1)] = (_dot_nt(xs_re[:, sl].astype(BF16), cmre[j])
                                             - _dot_nt(xs_im[:, sl].astype(BF16), cmim[j]))
        yv = ysk[...] + d_ref[...] * us
        gv = _dot(yv.astype(BF16), wg_ref[...])
        y_ref[...] = _unpermute(pmt_ref, gv[:, :SSM_W] * _sig(gv[:, SSM_W:]))

    blk = (4, 128, LCH)
    return pl.pallas_call(
        body, name="ssm_fwd", grid=(nblk,),
        in_specs=[_rows(SSM_W, 1), _full((TM, TM)), _full((TM, TM)), _full(blk), _full(blk),
                  _full((TM, STATES)), _full((TM, STATES)), _full((SUB, STATES)), _full((SUB, STATES)),
                  _full(blk), _full(blk), _full((1, SSM_W)), _full((SSM_W, 2 * SSM_W))],
        out_specs=[_rows(SSM_W), pl.BlockSpec((1, SUB, STATES), lambda i: (i, 0, 0)),
                   pl.BlockSpec((1, SUB, STATES), lambda i: (i, 0, 0))],
        out_shape=[jax.ShapeDtypeStruct((t, SSM_W), F32), jax.ShapeDtypeStruct((nblk, SUB, STATES), F32),
                   jax.ShapeDtypeStruct((nblk, SUB, STATES), F32)],
        scratch_shapes=[pltpu.VMEM((TM, STATES), F32), pltpu.VMEM((TM, STATES), F32),
                        pltpu.VMEM((SUB, STATES), F32), pltpu.VMEM((SUB, STATES), F32),
                        pltpu.VMEM((TM, SSM_W), F32)],
        compiler_params=_params())(proj, pm, pmt, mb_re, mb_im, p8_re, p8_im, q_re, q_im, cm_re, cm_im, dskip, wglu)


def _even_out(yp, ys, proj, x, w):
    t = x.shape[0]

    def body(yp_ref, ys_ref, z_ref, x_ref, w_ref, x1_ref, yg_ref):
        z = z_ref[...].astype(F32)
        sz = z * _sig(z)
        gp = (yp_ref[...] * sz[:, :POOL_W]).astype(BF16)
        gs = (ys_ref[...] * sz[:, POOL_W:]).astype(BF16)
        yg_ref[:, :POOL_W] = gp
        yg_ref[:, POOL_W:] = gs
        x1_ref[...] = x_ref[...] + _dot(gp, w_ref[:POOL_W, :]) + _dot(gs, w_ref[POOL_W:, :])

    return pl.pallas_call(
        body, name="even_out", grid=(t // TM,),
        in_specs=[_rows(POOL_W), _rows(SSM_W), _rows(D_MODEL, 1), _rows(D_MODEL), _full((D_MODEL, D_MODEL))],
        out_specs=[_rows(D_MODEL), _rows(D_MODEL)],
        out_shape=[jax.ShapeDtypeStruct((t, D_MODEL), F32), jax.ShapeDtypeStruct((t, D_MODEL), BF16)],
        compiler_params=_params())(yp, ys, proj, x, w)


def _phase_copies(ext, cp):
    n = cp.shape[1]
    for j in range(1, SUB):
        cp[j - 1] = ext[pl.ds(j, n), :]


def _shifted(ext, cp, off, nrows, sl):
    q, j = divmod(off, SUB)
    if j == 0:
        return ext[pl.ds(SUB * q, nrows), sl]
    return cp[j - 1, pl.ds(SUB * q, nrows), sl]


def _conv_taps(ext, cp, w_ref, first, nrows, sl, init):
    acc = init
    for k in range(CONV_K):
        acc = acc + w_ref[k:k + 1, sl] * _shifted(ext, cp, first(k), nrows, sl)
    return acc


def _conv_fwd(q, cw, cb, lg, lb):
    t = q.shape[0]

    def body(v_ref, g_ref, hv_ref, hg_ref, z_ref, w_ref, b_ref, lg_ref, lb_ref, y_ref, cv_ref, ext, cp):
        i = pl.program_id(0)
        ext[0:HALO, :] = jnp.where(i == 0, 0.0, hv_ref[...].astype(F32) * _sig(hg_ref[...].astype(F32)))
        ext[HALO:, :] = v_ref[...].astype(F32) * _sig(g_ref[...].astype(F32))
        _phase_copies(ext, cp)
        for c in range(D_MODEL // 128):
            sl = slice(128 * c, 128 * (c + 1))
            cv_ref[:, sl] = _conv_taps(ext, cp, w_ref, lambda k: k + 2, TM, sl,
                                       jnp.broadcast_to(b_ref[:, sl], (TM, 128)))
        cv = cv_ref[...]
        cc = cv - jnp.mean(cv, axis=-1, keepdims=True)
        rstd = lax.rsqrt(jnp.mean(cc * cc, axis=-1, keepdims=True) + LN_EPS)
        cl = cc * rstd * lg_ref[...] + lb_ref[...]
        z = z_ref[...].astype(F32)
        y_ref[...] = (cl * _sig(cl) * z * _sig(z)).astype(BF16)

    vec = _full((1, D_MODEL))
    return pl.pallas_call(
        body, name="conv_fwd", grid=(t // TM,),
        in_specs=[_rows(D_MODEL, 0), _rows(D_MODEL, 1), _prev(HALO, D_MODEL, 0), _prev(HALO, D_MODEL, 1),
                  _rows(D_MODEL, 2), _full((HALO, D_MODEL)), vec, vec, vec],
        out_specs=[_rows(D_MODEL), _rows(D_MODEL)],
        out_shape=[jax.ShapeDtypeStruct((t, D_MODEL), BF16), jax.ShapeDtypeStruct((t, D_MODEL), F32)],
        scratch_shapes=[pltpu.VMEM((TM + HALO, D_MODEL), F32),
                        pltpu.VMEM((SUB - 1, TM + HALO - SUB, D_MODEL), F32)],
        compiler_params=_params())(q, q, q, q, q, cw, cb, lg, lb)


def _odd_out_loss(y2, x1, w, gf, tgt):
    t = x1.shape[0]

    def body(y_ref, x_ref, w_ref, g_ref, t_ref, dx_ref, loss_ref, dg_ref):
        i = pl.program_id(0)
        _zero_at_first(i, loss_ref, dg_ref)
        x2 = x_ref[...] + _dot(y_ref[...], w_ref[...])
        r = lax.rsqrt(jnp.mean(x2 * x2, axis=-1, keepdims=True) + RMS_EPS)
        n = x2 * r
        e = n * g_ref[...] - t_ref[...]
        loss_ref[...] += jnp.sum(e * e, axis=0, keepdims=True) * (0.5 / D_MODEL)
        dout = e * (1.0 / D_MODEL)
        dg_ref[...] += jnp.sum(dout * n, axis=0, keepdims=True)
        dn = dout * g_ref[...]
        dx_ref[...] = r * (dn - n * jnp.mean(dn * n, axis=-1, keepdims=True))

    vec = _full((1, D_MODEL))
    return pl.pallas_call(
        body, name="odd_out_loss", grid=(t // TM,),
        in_specs=[_rows(D_MODEL), _rows(D_MODEL), _full((D_MODEL, D_MODEL)), vec, _rows(D_MODEL)],
        out_specs=[_rows(D_MODEL), vec, vec],
        out_shape=[jax.ShapeDtypeStruct((t, D_MODEL), F32), jax.ShapeDtypeStruct((1, D_MODEL), F32),
                   jax.ShapeDtypeStruct((1, D_MODEL), F32)],
        compiler_params=_params())(y2, x1, w, gf, tgt)


def _dsilu(z):
    s = _sig(z)
    return z * s, s * (1.0 + z * (1.0 - s))


def _odd_bwd_out(dx2, w, y2, cv, q, lg, lb):
    t = dx2.shape[0]

    def body(dx_ref, w_ref, y_ref, cv_ref, z_ref, lg_ref, lb_ref, dcv_ref, dz_ref, dw_ref, dlg_ref, dlb_ref):
        i = pl.program_id(0)
        _zero_at_first(i, dw_ref, dlg_ref, dlb_ref)
        dxb = dx_ref[...].astype(BF16)
        dy = _dot_nt(dxb, w_ref[...])
        dw_ref[...] += _dot_tn(y_ref[...], dxb)
        cv = cv_ref[...]
        cc = cv - jnp.mean(cv, axis=-1, keepdims=True)
        rstd = lax.rsqrt(jnp.mean(cc * cc, axis=-1, keepdims=True) + LN_EPS)
        cn = cc * rstd
        silu_c, dsilu_c = _dsilu(cn * lg_ref[...] + lb_ref[...])
        silu_z, dsilu_z = _dsilu(z_ref[...].astype(F32))
        dcl = dy * silu_z * dsilu_c
        dz_ref[...] = (dy * silu_c * dsilu_z).astype(BF16)
        dlg_ref[...] += jnp.sum(dcl * cn, axis=0, keepdims=True)
        dlb_ref[...] += jnp.sum(dcl, axis=0, keepdims=True)
        dcn = dcl * lg_ref[...]
        dcv_ref[...] = rstd * (dcn - jnp.mean(dcn, axis=-1, keepdims=True)
                               - cn * jnp.mean(dcn * cn, axis=-1, keepdims=True))

    vec = _full((1, D_MODEL))
    mat = _full((D_MODEL, D_MODEL))
    return pl.pallas_call(
        body, name="odd_bwd_out", grid=(t // TM,),
        in_specs=[_rows(D_MODEL), mat, _rows(D_MODEL), _rows(D_MODEL), _rows(D_MODEL, 2), vec, vec],
        out_specs=[_rows(D_MODEL), _rows(D_MODEL), mat, vec, vec],
        out_shape=[jax.ShapeDtypeStruct((t, D_MODEL), F32), jax.ShapeDtypeStruct((t, D_MODEL), BF16),
                   jax.ShapeDtypeStruct((D_MODEL, D_MODEL), F32), jax.ShapeDtypeStruct((1, D_MODEL), F32),
                   jax.ShapeDtypeStruct((1, D_MODEL), F32)],
        compiler_params=_params())(dx2, w, y2, cv, q, lg, lb)


def _conv_bwd(dcv, q, cw):
    t = dcv.shape[0]
    nblk = t // TM

    def body(d_ref, dn_ref, v_ref, g_ref, hv_ref, hg_ref, w_ref,
             dv_ref, dgt_ref, dw_ref, db_ref, gext, dext, dgl, gcp, dcp):
        i = pl.program_id(0)
        last = nblk - 1
        _zero_at_first(i, dw_ref, db_ref)
        v = v_ref[...].astype(F32)
        sg = _sig(g_ref[...].astype(F32))
        gext[0:HALO, :] = jnp.where(i == 0, 0.0, hv_ref[...].astype(F32) * _sig(hg_ref[...].astype(F32)))
        gext[HALO:, :] = v * sg
        d = d_ref[...]
        dext[0:TM, :] = d
        dext[TM:, :] = jnp.where(i == last, 0.0, dn_ref[...])
        _phase_copies(gext, gcp)
        _phase_copies(dext, dcp)
        db_ref[...] += jnp.sum(d, axis=0, keepdims=True)
        for c in range(D_MODEL // 128):
            sl = slice(128 * c, 128 * (c + 1))
            dgl[:, sl] = _conv_taps(dext, dcp, w_ref, lambda k: 30 - k, TM, sl, jnp.zeros((TM, 128), F32))
            dc = d[:, sl]
            for k in range(CONV_K):
                dw_ref[k:k + 1, sl] += jnp.sum(dc * _shifted(gext, gcp, k + 2, TM, sl), axis=0, keepdims=True)
        dg = dgl[...]
        dv_ref[...] = (dg * sg).astype(BF16)
        dgt_ref[...] = (dg * v * sg * (1.0 - sg)).astype(BF16)

    return pl.pallas_call(
        body, name="conv_bwd", grid=(t // TM,),
        in_specs=[_rows(D_MODEL), _next(HALO, D_MODEL, t), _rows(D_MODEL, 0), _rows(D_MODEL, 1),
                  _prev(HALO, D_MODEL, 0), _prev(HALO, D_MODEL, 1), _full((HALO, D_MODEL))],
        out_specs=[_rows(D_MODEL), _rows(D_MODEL), _full((HALO, D_MODEL)), _full((1, D_MODEL))],
        out_shape=[jax.ShapeDtypeStruct((t, D_MODEL), BF16), jax.ShapeDtypeStruct((t, D_MODEL), BF16),
                   jax.ShapeDtypeStruct((HALO, D_MODEL), F32), jax.ShapeDtypeStruct((1, D_MODEL), F32)],
        scratch_shapes=[pltpu.VMEM((TM + HALO, D_MODEL), F32), pltpu.VMEM((TM + HALO, D_MODEL), F32),
                        pltpu.VMEM((TM, D_MODEL), F32),
                        pltpu.VMEM((SUB - 1, TM + HALO - SUB, D_MODEL), F32),
                        pltpu.VMEM((SUB - 1, TM + HALO - SUB, D_MODEL), F32)],
        compiler_params=_params())(dcv, dcv, q, q, q, q, cw)


def _in_bwd(dparts, w, x, g, dres, name):
    t = x.shape[0]
    widths = [p.shape[1] for p in dparts]
    npart = len(dparts)

    def body(*refs):
        d_refs = refs[:npart]
        w_ref, x_ref, g_ref, r_ref, dx_ref, dg_ref, dw_ref = refs[npart:]
        i = pl.program_id(0)
        _zero_at_first(i, dg_ref, dw_ref)
        xv = x_ref[...]
        r = lax.rsqrt(jnp.mean(xv * xv, axis=-1, keepdims=True) + RMS_EPS)
        n = xv * r
        h = (n * g_ref[...]).astype(BF16)
        off = 0
        dh = None
        for d_ref, wd in zip(d_refs, widths):
            d = d_ref[...]
            part = _dot_nt(d, w_ref[:, off:off + wd])
            dh = part if dh is None else dh + part
            dw_ref[:, off:off + wd] += _dot_tn(h, d)
            off += wd
        dg_ref[...] += jnp.sum(dh * n, axis=0, keepdims=True)
        dn = dh * g_ref[...]
        dx_ref[...] = r_ref[...] + r * (dn - n * jnp.mean(dn * n, axis=-1, keepdims=True))

    vec = _full((1, D_MODEL))
    once = pl.BlockSpec(w.shape, lambda i: (0, 0), pipeline_mode=pl.Buffered(1))
    return pl.pallas_call(
        body, name=name, grid=(t // TM,),
        in_specs=[_rows(wd) for wd in widths] + [once, _rows(D_MODEL), vec, _rows(D_MODEL)],
        out_specs=[_rows(D_MODEL), vec, once],
        out_shape=[jax.ShapeDtypeStruct((t, D_MODEL), F32), jax.ShapeDtypeStruct((1, D_MODEL), F32),
                   jax.ShapeDtypeStruct(w.shape, F32)],
        compiler_params=_params())(*dparts, w, x, g, dres)


def _even_bwd_out(dx1, w, yg, yp, ys, proj):
    t = dx1.shape[0]

    def body(dx_ref, w_ref, yg_ref, yp_ref, ys_ref, z_ref, dy_ref, dz_ref, dw_ref):
        i = pl.program_id(0)
        _zero_at_first(i, dw_ref)
        dxb = dx_ref[...].astype(BF16)
        dyg = _dot_nt(dxb, w_ref[...])
        dw_ref[...] += _dot_tn(yg_ref[...], dxb)
        silu_z, dsilu_z = _dsilu(z_ref[...].astype(F32))
        dy_ref[...] = (dyg * silu_z).astype(BF16)
        dz_ref[:, :POOL_W] = (dyg[:, :POOL_W] * yp_ref[...] * dsilu_z[:, :POOL_W]).astype(BF16)
        dz_ref[:, POOL_W:] = (dyg[:, POOL_W:] * ys_ref[...] * dsilu_z[:, POOL_W:]).astype(BF16)

    mat = _full((D_MODEL, D_MODEL))
    return pl.pallas_call(
        body, name="even_bwd_out", grid=(t // TM,),
        in_specs=[_rows(D_MODEL), mat, _rows(D_MODEL), _rows(POOL_W), _rows(SSM_W), _rows(D_MODEL, 1)],
        out_specs=[_rows(D_MODEL), _rows(D_MODEL), mat],
        out_shape=[jax.ShapeDtypeStruct((t, D_MODEL), BF16), jax.ShapeDtypeStruct((t, D_MODEL), BF16),
                   jax.ShapeDtypeStruct((D_MODEL, D_MODEL), F32)],
        compiler_params=_params())(dx1, w, yg, yp, ys, proj)


def _pool_bwd(dycat, proj, wp, ps):
    t = proj.shape[0]

    def body(dy_ref, dyn_ref, u_ref, h_ref, wp_ref, ps_ref, du_ref, dwp_ref, dps_ref):
        i = pl.program_id(0)
        last = t // TM - 1
        _zero_at_first(i, dwp_ref, dps_ref)
        pos = (i * TM + 1 + lax.broadcasted_iota(jnp.int32, (TM, 1), 0)).astype(F32)
        pos_ext = (i * TM + 1 + lax.broadcasted_iota(jnp.int32, (TM + POOL_HALO, 1), 0)).astype(F32)
        for g in range(4):
            sl = slice(128 * g, 128 * (g + 1))
            w = float(2 << g)
            u = u_ref[:, sl].astype(F32)
            halo = jnp.where(i == 0, 0.0, h_ref[:, sl].astype(F32))
            s = _pool_sums(jnp.concatenate([halo, u], axis=0), g, True)[POOL_HALO:, :]
            pooled = (s / jnp.minimum(pos, w) - u).astype(BF16)
            dy = dy_ref[:, sl].astype(F32)
            dps_ref[:, sl] += jnp.sum(dy * _dot(pooled, wp_ref[g]), axis=0, keepdims=True)
            dy_ext = jnp.concatenate([dy, jnp.where(i == last, 0.0, dyn_ref[:, sl].astype(F32))], axis=0)
            dmix = (dy_ext * ps_ref[:, sl]).astype(BF16)
            dwp_ref[g] += _dot_tn(pooled, dmix[:TM, :])
            dpool = _dot_nt(dmix, wp_ref[g])
            lead = _pool_sums(dpool / jnp.minimum(pos_ext, w), g, False)
            du_ref[:, sl] = (lead[:TM, :] - dpool[:TM, :]).astype(BF16)

    return pl.pallas_call(
        body, name="pool_bwd", grid=(t // TM,),
        in_specs=[_rows(POOL_W, 0), _next(POOL_HALO, POOL_W, t, 0), _rows(POOL_W, 0), _prev(POOL_HALO, POOL_W, 0),
                  _full((4, 128, 128)), _full((1, POOL_W))],
        out_specs=[_rows(POOL_W), _full((4, 128, 128)), _full((1, POOL_W))],
        out_shape=[jax.ShapeDtypeStruct((t, POOL_W), BF16), jax.ShapeDtypeStruct((4, 128, 128), F32),
                   jax.ShapeDtypeStruct((1, POOL_W), F32)],
        compiler_params=_params())(dycat, dycat, proj, proj, wp, ps)


def _ssm_bwd(dycat, proj, car_in_re, car_in_im, pm, pmt, mb_re, mb_im, p8_re, p8_im, q_re, q_im, qr_re, qr_im,
             cm_re, cm_im, dskip, wglu):
    t = proj.shape[0]
    nblk = t // TM

    def body(dy_ref, u_ref, cin_re, cin_im, pm_ref, pmt_ref, mbre, mbim, p8re, p8im, qre, qim, qrre, qrim,
             cmre, cmim, d_ref, wg_ref,
             du_ref, dmbre, dmbim, dcmre, dcmim, dare, daim, dd_ref, dwg_ref,
             xs_re, xs_im, gs_re, gs_im, car_re, car_im, ent_re, ent_im, gcar_re, gcar_im, ysk, dysk):
        i = pl.program_id(0)
        _zero_at_first(i, dmbre, dmbim, dcmre, dcmim, dare, daim, dd_ref, dwg_ref, gcar_re, gcar_im)
        us = _dot(pm_ref[...], u_ref[...])
        usb = us.astype(BF16)
        for j in range(4):
            xs_re[:, LCH * j:LCH * (j + 1)] = _dot(usb[:, 128 * j:128 * (j + 1)], mbre[j])
            xs_im[:, LCH * j:LCH * (j + 1)] = _dot(usb[:, 128 * j:128 * (j + 1)], mbim[j])
        car_re[...] = cin_re[0]
        car_im[...] = cin_im[0]
        _scan_fwd_block(xs_re, xs_im, p8re, p8im, qre, qim, car_re, car_im, ent_re, ent_im)
        for j in range(4):
            sl = slice(LCH * j, LCH * (j + 1))
            ysk[:, 128 * j:128 * (j + 1)] = (_dot_nt(xs_re[:, sl].astype(BF16), cmre[j])
                                             - _dot_nt(xs_im[:, sl].astype(BF16), cmim[j]))
        yvb = (ysk[...] + d_ref[...] * us).astype(BF16)
        gv = _dot(yvb, wg_ref[...])
        sg = _sig(gv[:, SSM_W:])
        dyss = _dot(pm_ref[...], dy_ref[...])
        dval = (dyss * sg).astype(BF16)
        dgate = (dyss * gv[:, :SSM_W] * sg * (1.0 - sg)).astype(BF16)
        dy = _dot_nt(dval, wg_ref[:, :SSM_W]) + _dot_nt(dgate, wg_ref[:, SSM_W:])
        dwg_ref[:, :SSM_W] += _dot_tn(yvb, dval)
        dwg_ref[:, SSM_W:] += _dot_tn(yvb, dgate)
        dd_ref[...] += jnp.sum(dy * us, axis=0, keepdims=True)
        dysk[...] = dy
        for j in range(4):
            sl = slice(LCH * j, LCH * (j + 1))
            dyj = dy[:, 128 * j:128 * (j + 1)].astype(BF16)
            gs_re[:, sl] = _dot(dyj, cmre[j])
            gs_im[:, sl] = -_dot(dyj, cmim[j])
            dcmre[j] += _dot_tn(dyj, xs_re[:, sl].astype(BF16))
            dcmim[j] -= _dot_tn(dyj, xs_im[:, sl].astype(BF16))
        row = lax.broadcasted_iota(jnp.int32, (SUB, LCH), 0)
        for j in range(STATES // LCH):
            sl = slice(LCH * j, LCH * (j + 1))
            are, aim = p8re[0:SUB, sl], -p8im[0:SUB, sl]

            def totals(k, v, sl=sl, are=are, aim=aim):
                r0 = pl.multiple_of((SEG_LEN - 2 - k) * SUB, SUB)
                vre, vim = _cmul_add(are, aim, v[0], v[1], gs_re[pl.ds(r0, SUB), sl], gs_im[pl.ds(r0, SUB), sl])
                gs_re[pl.ds(r0, SUB), sl] = vre
                gs_im[pl.ds(r0, SUB), sl] = vim
                return vre, vim

            top = (SEG_LEN - 1) * SUB
            fre, fim = lax.fori_loop(0, SEG_LEN - 1, totals,
                                     (gs_re[top:top + SUB, sl], gs_im[top:top + SUB, sl]), unroll=2)
            fre, fim, nre, nim = _segment_chain(fre, fim, qrre[:, sl], -qrim[:, sl],
                                                gcar_re[:, sl], gcar_im[:, sl], row, True)
            gcar_re[:, sl] = jnp.broadcast_to(fre[0:1, :], (SUB, LCH))
            gcar_im[:, sl] = jnp.broadcast_to(fim[0:1, :], (SUB, LCH))

            def fix(i2, acc, sl=sl, nre=nre, nim=nim):
                r0 = pl.multiple_of(i2 * SUB, SUB)
                rb = pl.multiple_of((SEG_LEN - 1 - i2) * SUB, SUB)
                gre, gim = _cmul_add(p8re[pl.ds(rb, SUB), sl], -p8im[pl.ds(rb, SUB), sl], nre, nim,
                                     gs_re[pl.ds(r0, SUB), sl], gs_im[pl.ds(r0, SUB), sl])
                gs_re[pl.ds(r0, SUB), sl] = gre
                gs_im[pl.ds(r0, SUB), sl] = gim
                rp = pl.multiple_of((i2 - 1) * SUB, SUB)
                xre, xim = xs_re[pl.ds(rp, SUB), sl], xs_im[pl.ds(rp, SUB), sl]
                return acc[0] + gre * xre + gim * xim, acc[1] + gim * xre - gre * xim

            g0re, g0im = _cmul_add(p8re[top:top + SUB, sl], -p8im[top:top + SUB, sl], nre, nim,
                                   gs_re[0:SUB, sl], gs_im[0:SUB, sl])
            gs_re[0:SUB, sl] = g0re
            gs_im[0:SUB, sl] = g0im
            ere, eim = ent_re[:, sl], ent_im[:, sl]
            acc0 = (dare[:, sl] + g0re * ere + g0im * eim, daim[:, sl] + g0im * ere - g0re * eim)
            are_acc, aim_acc = lax.fori_loop(1, SEG_LEN, fix, acc0, unroll=2)
            dare[:, sl] = are_acc
            daim[:, sl] = aim_acc
        for j in range(4):
            sl = slice(LCH * j, LCH * (j + 1))
            c4 = slice(128 * j, 128 * (j + 1))
            gre = gs_re[:, sl].astype(BF16)
            gim = gs_im[:, sl].astype(BF16)
            dmbre[j] += _dot_tn(usb[:, c4], gre)
            dmbim[j] += _dot_tn(usb[:, c4], gim)
            dysk[:, c4] = _dot_nt(gre, mbre[j]) + _dot_nt(gim, mbim[j]) + dysk[:, c4] * d_ref[:, c4]
        du_ref[...] = _dot(pmt_ref[...], dysk[...].astype(BF16)).astype(BF16)

    blk = (4, 128, LCH)
    pw = _full((SUB, STATES))
    p8 = _full((TM, STATES))
    perm = _full((TM, TM))
    car = pl.BlockSpec((1, SUB, STATES), lambda i: (nblk - 1 - i, 0, 0))
    big = lambda: pltpu.VMEM((TM, STATES), F32)
    small = lambda: pltpu.VMEM((SUB, STATES), F32)
    return pl.pallas_call(
        body, name="ssm_bwd", grid=(nblk,),
        in_specs=[_rows(SSM_W, 1, rev=nblk), _rows(SSM_W, 1, rev=nblk), car, car, perm, perm, _full(blk), _full(blk),
                  p8, p8, pw, pw, pw, pw, _full(blk), _full(blk), _full((1, SSM_W)), _full((SSM_W, 2 * SSM_W))],
        out_specs=[_rows(SSM_W, 0, rev=nblk), _full(blk), _full(blk), _full(blk), _full(blk), pw, pw,
                   _full((1, SSM_W)), _full((SSM_W, 2 * SSM_W))],
        out_shape=[jax.ShapeDtypeStruct((t, SSM_W), BF16)] + [jax.ShapeDtypeStruct(blk, F32)] * 4
        + [jax.ShapeDtypeStruct((SUB, STATES), F32)] * 2
        + [jax.ShapeDtypeStruct((1, SSM_W), F32), jax.ShapeDtypeStruct((SSM_W, 2 * SSM_W), F32)],
        scratch_shapes=[big(), big(), big(), big(), small(), small(), small(), small(), small(), small(),
                        pltpu.VMEM((TM, SSM_W), F32), pltpu.VMEM((TM, SSM_W), F32)],
        compiler_params=_params())(dycat, proj, car_in_re, car_in_im, pm, pmt, mb_re, mb_im, p8_re, p8_im,
                                   q_re, q_im, qr_re, qr_im, cm_re, cm_im, dskip, wglu)


def _adamw(w, g, m, v, name):
    rows = w.shape[0]
    tr = 256 if rows % 256 == 0 else rows
    c1 = 1.0 / (1.0 - ADAM_B1 ** ADAM_STEP)
    c2 = 1.0 / (1.0 - ADAM_B2 ** ADAM_STEP)

    def body(w_ref, g_ref, m_ref, v_ref, d_ref, nm_ref, nv_ref):
        gv = g_ref[...]
        m = ADAM_B1 * m_ref[...] + (1.0 - ADAM_B1) * gv
        v = ADAM_B2 * v_ref[...] + (1.0 - ADAM_B2) * (gv * gv)
        nm_ref[...] = m
        nv_ref[...] = v
        d_ref[...] = -ADAM_LR * ((m * c1) / (jnp.sqrt(v * c2) + ADAM_EPS) + ADAM_WD * w_ref[...])

    spec = pl.BlockSpec((tr, D_MODEL), lambda i: (i, 0))
    shp = jax.ShapeDtypeStruct((rows, D_MODEL), F32)
    return pl.pallas_call(
        body, name=name, grid=(rows // tr,), in_specs=[spec] * 4, out_specs=[spec] * 3, out_shape=[shp] * 3,
        compiler_params=_params())(w, g, m, v)


def _core_index():
    return lax.axis_index("c").astype(jnp.int32).reshape(1)


def _pair_add(g, theirs, out_dtype):
    n, half, _ = theirs.shape
    nb = half // 256

    def body(c_ref, a_ref, b_ref, o_ref):
        o_ref[...] = (a_ref[...] + b_ref[...]).astype(out_dtype)

    spec = pl.BlockSpec((1, 256, D_MODEL), lambda i, j, c: (i, j, 0))
    grid_spec = pltpu.PrefetchScalarGridSpec(
        num_scalar_prefetch=1, grid=(n, nb),
        in_specs=[pl.BlockSpec((1, 256, D_MODEL), lambda i, j, c: (i, c[0] * nb + j, 0)), spec], out_specs=spec)
    return pl.pallas_call(
        body, name="pair_add", grid_spec=grid_spec, out_shape=jax.ShapeDtypeStruct(theirs.shape, out_dtype),
        compiler_params=_params(2))(_core_index(), g, theirs)


def _adamw_halves(w, g_mine, g_theirs, m, v):
    half = g_mine.shape[0]
    nb = half // 256
    c1 = 1.0 / (1.0 - ADAM_B1 ** ADAM_STEP)
    c2 = 1.0 / (1.0 - ADAM_B2 ** ADAM_STEP)

    def body(c_ref, w_ref, gm_ref, gt_ref, m_ref, v_ref, g_ref, d_ref, nm_ref, nv_ref):
        gv = jnp.where(pl.program_id(0) // nb == c_ref[0], gm_ref[...], gt_ref[...])
        m = ADAM_B1 * m_ref[...] + (1.0 - ADAM_B1) * gv
        v = ADAM_B2 * v_ref[...] + (1.0 - ADAM_B2) * (gv * gv)
        g_ref[...] = gv
        nm_ref[...] = m
        nv_ref[...] = v
        d_ref[...] = -ADAM_LR * ((m * c1) / (jnp.sqrt(v * c2) + ADAM_EPS) + ADAM_WD * w_ref[...])

    spec = pl.BlockSpec((256, D_MODEL), lambda i, c: (i, 0))
    part = pl.BlockSpec((256, D_MODEL), lambda i, c: (i % nb, 0))
    shp = jax.ShapeDtypeStruct((2 * half, D_MODEL), F32)
    grid_spec = pltpu.PrefetchScalarGridSpec(
        num_scalar_prefetch=1, grid=(2 * nb,), in_specs=[spec, part, part, spec, spec], out_specs=[spec] * 4)
    return pl.pallas_call(
        body, name="adamw_shard", grid_spec=grid_spec, out_shape=[shp] * 4,
        compiler_params=_params())(_core_index(), w, g_mine, g_theirs, m, v)


def _sum_lead(a, name):
    n, rows, _ = a.shape
    tr = 256 if rows % 256 == 0 else rows

    def body(a_ref, o_ref):
        acc = a_ref[0].astype(F32)
        for k in range(1, n):
            acc = acc + a_ref[k].astype(F32)
        o_ref[...] = acc

    return pl.pallas_call(
        body, name=name, grid=(rows // tr,),
        in_specs=[pl.BlockSpec((n, tr, D_MODEL), lambda i: (0, i, 0))],
        out_specs=pl.BlockSpec((tr, D_MODEL), lambda i: (i, 0)),
        out_shape=jax.ShapeDtypeStruct((rows, D_MODEL), F32), compiler_params=_params())(a)


ANY = pl.BlockSpec(memory_space=pl.ANY)


def _mesh_pos():
    return lax.axis_index("x"), lax.axis_index("y"), lax.axis_index("c")


def _gather_weights(wb):
    rows = wb.shape[0]
    half = rows // 2
    ch = half // COMM_CHUNKS
    ncopy = 3 * COMM_CHUNKS

    def body(w_ref, o_ref, bounce, send_sems, recv_sems, local_sems):
        x, y, c = _mesh_pos()
        me = 2 * x + y
        sibling = (x, y, 1 - c)
        chips = [(1 - x, y), (x, 1 - y), (1 - x, 1 - y)]
        ids = [2 * chip[0] + chip[1] for chip in chips]

        def piece(q, h, k):
            return o_ref.at[q, pl.ds(h * half + k * ch, ch), :]

        def copy(s, q, h, k, to, src=None):
            return pltpu.make_async_remote_copy(
                src_ref=piece(q, h, k) if src is None else src, dst_ref=piece(q, h, k),
                send_sem=send_sems.at[s], recv_sem=recv_sems.at[s], device_id=to, device_id_type=MESH)

        load = pltpu.make_async_copy(w_ref, bounce, local_sems.at[0])
        store = pltpu.make_async_copy(bounce, o_ref.at[me], local_sems.at[1])
        load.start()
        first = [copy(j * COMM_CHUNKS + k, me, c, k, (*chip, c), src=w_ref.at[pl.ds(c * half + k * ch, ch), :])
                 for j, chip in enumerate(chips) for k in range(COMM_CHUNKS)]
        for cp in first:
            cp.start()
        load.wait()
        store.start()
        passed = []
        for j in range(3):
            for k in range(COMM_CHUNKS):
                s = j * COMM_CHUNKS + k
                copy(s, ids[j], c, k, (x, y, c)).wait_recv()
                fwd = copy(ncopy + s, ids[j], c, k, sibling)
                fwd.start()
                passed.append(fwd)
        for j in range(3):
            for k in range(COMM_CHUNKS):
                copy(ncopy + j * COMM_CHUNKS + k, ids[j], 1 - c, k, (x, y, c)).wait_recv()
        for cp in first + passed:
            cp.wait_send()
        store.wait()

    return pl.pallas_call(
        body, name="gather_weights", in_specs=[ANY], out_specs=ANY,
        out_shape=jax.ShapeDtypeStruct((4, rows, D_MODEL), wb.dtype),
        scratch_shapes=[pltpu.VMEM(wb.shape, wb.dtype), pltpu.SemaphoreType.DMA((2 * ncopy,)),
                        pltpu.SemaphoreType.DMA((2 * ncopy,)), pltpu.SemaphoreType.DMA((2,))],
        compiler_params=pltpu.CompilerParams(vmem_limit_bytes=VMEM_LIMIT),
    )(wb)


def _gather_all(v):
    def body(v_ref, o_ref, bounce, send_sems, recv_sems, local_sems):
        x, y, c = _mesh_pos()
        sibling = (x, y, 1 - c)
        chips = [(1 - x, y), (x, 1 - y), (1 - x, 1 - y)]

        def blk(px, py, pc):
            return o_ref.at[4 * px + 2 * py + pc]

        def copy(k, block, to, src=None):
            return pltpu.make_async_remote_copy(
                src_ref=blk(*block) if src is None else src, dst_ref=blk(*block),
                send_sem=send_sems.at[k], recv_sem=recv_sems.at[k], device_id=to, device_id_type=MESH)

        load = pltpu.make_async_copy(v_ref, bounce, local_sems.at[0])
        store = pltpu.make_async_copy(bounce, blk(x, y, c), local_sems.at[1])
        load.start()
        first = [copy(0, (x, y, c), sibling, src=v_ref)]
        first += [copy(1 + j, (x, y, c), (*chip, c), src=v_ref) for j, chip in enumerate(chips)]
        for cp in first:
            cp.start()
        load.wait()
        store.start()
        passed = [copy(4 + j, (*chip, c), sibling) for j, chip in enumerate(chips)]
        for j, chip in enumerate(chips):
            copy(1 + j, (*chip, c), (x, y, c)).wait_recv()
            passed[j].start()
        copy(0, (x, y, 1 - c), (x, y, c)).wait_recv()
        for j, chip in enumerate(chips):
            copy(4 + j, (*chip, 1 - c), (x, y, c)).wait_recv()
        for cp in first + passed:
            cp.wait_send()
        store.wait()

    return pl.pallas_call(
        body, name="gather_all", in_specs=[ANY], out_specs=ANY,
        out_shape=jax.ShapeDtypeStruct((8,) + v.shape, v.dtype),
        scratch_shapes=[pltpu.VMEM(v.shape, v.dtype), pltpu.SemaphoreType.DMA((7,)), pltpu.SemaphoreType.DMA((7,)),
                        pltpu.SemaphoreType.DMA((2,))],
    )(v)


def _pair_split(g):
    n, rows, _ = g.shape
    half = rows // 2
    ch = half // COMM_CHUNKS

    def body(g_ref, theirs_ref, send_sems, recv_sems):
        x, y, c = _mesh_pos()
        gives = [pltpu.make_async_remote_copy(
            src_ref=g_ref.at[q, pl.ds((1 - c) * half + k * ch, ch), :],
            dst_ref=theirs_ref.at[q, pl.ds(k * ch, ch), :],
            send_sem=send_sems.at[q * COMM_CHUNKS + k], recv_sem=recv_sems.at[q * COMM_CHUNKS + k],
            device_id=(x, y, 1 - c), device_id_type=MESH) for q in range(n) for k in range(COMM_CHUNKS)]
        for cp in gives:
            cp.start()
        for cp in gives:
            cp.wait()

    return pl.pallas_call(
        body, name="pair_split", in_specs=[ANY], out_specs=ANY,
        out_shape=jax.ShapeDtypeStruct((n, half, D_MODEL), g.dtype),
        scratch_shapes=[pltpu.SemaphoreType.DMA((n * COMM_CHUNKS,)), pltpu.SemaphoreType.DMA((n * COMM_CHUNKS,))],
    )(g)


def _chip_scatter(p):
    def body(p_ref, o_ref, bounce, send_sems, recv_sems, local_sems):
        x, y, c = _mesh_pos()
        me = 2 * x + y
        chips = [(1 - x, y), (x, 1 - y), (1 - x, 1 - y)]
        load = pltpu.make_async_copy(p_ref.at[me], bounce, local_sems.at[0])
        keep = pltpu.make_async_copy(bounce, o_ref.at[me], local_sems.at[1])
        load.start()
        sends = [pltpu.make_async_remote_copy(
            src_ref=p_ref.at[2 * chip[0] + chip[1]], dst_ref=o_ref.at[me],
            send_sem=send_sems.at[j], recv_sem=recv_sems.at[j], device_id=(*chip, c), device_id_type=MESH)
            for j, chip in enumerate(chips)]
        for cp in sends:
            cp.start()
        load.wait()
        keep.start()
        for j, chip in enumerate(chips):
            q = 2 * chip[0] + chip[1]
            pltpu.make_async_remote_copy(
                src_ref=p_ref.at[q], dst_ref=o_ref.at[q], send_sem=send_sems.at[j], recv_sem=recv_sems.at[j],
                device_id=(*chip, c), device_id_type=MESH).wait_recv()
        for cp in sends:
            cp.wait_send()
        keep.wait()

    return pl.pallas_call(
        body, name="chip_scatter", in_specs=[ANY], out_specs=ANY, out_shape=jax.ShapeDtypeStruct(p.shape, p.dtype),
        scratch_shapes=[pltpu.VMEM(p.shape[1:], p.dtype), pltpu.SemaphoreType.DMA((3,)),
                        pltpu.SemaphoreType.DMA((3,)), pltpu.SemaphoreType.DMA((2,))],
    )(p)


def _pair_join(r):
    rows = r.shape[0]
    ch = rows // COMM_CHUNKS

    def body(r_ref, o_ref, send_sems, recv_sems):
        x, y, c = _mesh_pos()
        gives = [pltpu.make_async_remote_copy(
            src_ref=r_ref.at[pl.ds(k * ch, ch), :], dst_ref=o_ref.at[pl.ds(k * ch, ch), :],
            send_sem=send_sems.at[k], recv_sem=recv_sems.at[k], device_id=(x, y, 1 - c), device_id_type=MESH)
            for k in range(COMM_CHUNKS)]
        for cp in gives:
            cp.start()
        for cp in gives:
            cp.wait()

    return pl.pallas_call(
        body, name="pair_join", in_specs=[ANY], out_specs=ANY, out_shape=jax.ShapeDtypeStruct(r.shape, r.dtype),
        scratch_shapes=[pltpu.SemaphoreType.DMA((COMM_CHUNKS,)), pltpu.SemaphoreType.DMA((COMM_CHUNKS,))],
    )(r)


SHARD_BIG = (("even_w_in", (1024, 512)), ("ssm_w_glu", (512, 256)), ("even_w_out", (256, 1024)),
             ("odd_w_in", (1024, 768)), ("odd_w_out", (256, 1024)))
SHARD_SMALL = (("odd_norm", 1), ("conv_w", CONV_K), ("conv_b", 1), ("conv_ln_g", 1), ("conv_ln_b", 1))
REP_NAMES = (("even_norm", (1024,)), ("pool_w", (4, 128, 128)), ("pool_scale", (512,)), ("ssm_log_dt", (32,)),
             ("ssm_a_re", (32, 64)), ("ssm_a_im", (32, 64)), ("ssm_b_re", (32, 64, 16)), ("ssm_b_im", (32, 64, 16)),
             ("ssm_c_re", (32, 16, 64)), ("ssm_c_im", (32, 16, 64)), ("ssm_d", (512,)), ("final_norm", (1024,)))


def _pack_shard(d):
    big = [d[n].reshape(-1, D_MODEL) for n, _ in SHARD_BIG]
    small = jnp.concatenate([d[n].reshape(r, 256) for n, r in SHARD_SMALL], axis=0).reshape(-1)
    small = jnp.pad(small, (0, ROWS_SMALL * D_MODEL - small.shape[0])).reshape(ROWS_SMALL, D_MODEL)
    pad = jnp.zeros((ROWS_PACK - ROWS_BIG - ROWS_SMALL, D_MODEL), F32)
    return jnp.concatenate(big + [small, pad], axis=0)


def _unpack_shard(buf):
    out = {}
    off = 0
    for n, shp in SHARD_BIG:
        rows = shp[0] * shp[1] // D_MODEL
        out[n] = buf[off:off + rows].reshape(shp)
        off += rows
    small = buf[ROWS_BIG:ROWS_BIG + ROWS_SMALL].reshape(-1)[:35 * 256].reshape(35, 256)
    off = 0
    for n, r in SHARD_SMALL:
        out[n] = small[off:off + r].reshape((r, 256) if r > 1 else (256,))
        off += r
    return out


def _pack_rep(d):
    flat = jnp.concatenate([d[n].reshape(-1) for n, _ in REP_NAMES])
    return jnp.pad(flat, (0, REP_ROWS * D_MODEL - flat.shape[0])).reshape(REP_ROWS, D_MODEL)


def _unpack_rep(buf):
    flat = buf.reshape(-1)
    out = {}
    off = 0
    for n, shp in REP_NAMES:
        size = 1
        for s in shp:
            size *= s
        out[n] = flat[off:off + size].reshape(shp)
        off += size
    return out


def _cols_full(g4, rows, cols):
    return g4.reshape(4, rows, cols).transpose(1, 0, 2).reshape(rows, 4 * cols)


def _cols_split(full, cols):
    rows = full.shape[0]
    return full.reshape(rows, 4, cols).transpose(1, 0, 2).reshape(4, -1, D_MODEL)


def _block_diag(a):
    a = a.reshape(4, 8, GROUP_DIM, N_STATE)
    eye = jnp.eye(8, dtype=a.dtype)
    return (a[:, :, :, None, :] * eye[None, :, None, :, None]).reshape(4, 128, LCH)


def _block_diag_take(m):
    m = m.reshape(4, 8, GROUP_DIM, 8, N_STATE)
    eye = jnp.eye(8, dtype=m.dtype)
    return jnp.sum(m * eye[None, :, None, :, None], axis=3).reshape(N_GROUPS, GROUP_DIM, N_STATE)


def _ssm_discretise(log_dt, a_re, a_im, b_re, b_im):
    dt = jnp.exp(log_dt)[:, None]
    mag = jnp.exp(a_re * dt)
    ang = a_im * dt
    abar_re = mag * jnp.cos(ang)
    abar_im = mag * jnp.sin(ang)
    den = a_re * a_re + a_im * a_im
    nr = abar_re - 1.0
    ni = abar_im
    k_re = (nr * a_re + ni * a_im) / den
    k_im = (ni * a_re - nr * a_im) / den
    bb_re = k_re[..., None] * b_re - k_im[..., None] * b_im
    bb_im = k_re[..., None] * b_im + k_im[..., None] * b_re
    return abar_re, abar_im, bb_re, bb_im


def _cumulative_powers(re, im, n):
    def mul(a, b):
        return a[0] * b[0] - a[1] * b[1], a[0] * b[1] + a[1] * b[0]
    return lax.associative_scan(mul, (jnp.broadcast_to(re, (n,) + re.shape), jnp.broadcast_to(im, (n,) + im.shape)))


def _scan_tables(abar_re, abar_im):
    p_re, p_im = _cumulative_powers(abar_re.reshape(-1), abar_im.reshape(-1), SEG_LEN)
    q_re, q_im = _cumulative_powers(p_re[-1], p_im[-1], SUB)
    return jnp.repeat(p_re, SUB, axis=0), jnp.repeat(p_im, SUB, axis=0), q_re, q_im


def _local_step(x, tgt, w):
    row = lambda a: a.reshape(1, -1)
    e_w_in = w["even_w_in"].astype(BF16)
    e_w_out = w["even_w_out"].astype(BF16)
    o_w_in = w["odd_w_in"].astype(BF16)
    o_w_out = w["odd_w_out"].astype(BF16)
    wglu = w["ssm_w_glu"].astype(BF16)
    wp = w["pool_w"].astype(BF16)
    ssm_in = (w["ssm_log_dt"], w["ssm_a_re"], w["ssm_a_im"], w["ssm_b_re"], w["ssm_b_im"])
    (abar_re, abar_im, bb_re, bb_im), ssm_vjp = jax.vjp(_ssm_discretise, *ssm_in)
    mb_re = _block_diag(bb_re.transpose(0, 2, 1)).astype(BF16)
    mb_im = _block_diag(bb_im.transpose(0, 2, 1)).astype(BF16)
    cm_re = _block_diag(w["ssm_c_re"]).astype(BF16)
    cm_im = _block_diag(w["ssm_c_im"]).astype(BF16)
    p8_re, p8_im, q_re, q_im = _scan_tables(abar_re, abar_im)
    qr_re, qr_im = q_re[::-1], q_im[::-1]
    pm = _perm_matrix()
    pmt = pm.T
    cw = jnp.pad(w["conv_w"], ((0, HALO - CONV_K), (0, 0)))
    g0, g1, gf = row(w["even_norm"]), row(w["odd_norm"]), row(w["final_norm"])
    ps, dskip = row(w["pool_scale"]), row(w["ssm_d"])
    cb, lg, lb = row(w["conv_b"]), row(w["conv_ln_g"]), row(w["conv_ln_b"])

    proj = _norm_in(x, g0, e_w_in, "even_in")
    yp = _pool_fwd(proj, wp, ps)
    ys, car_re, car_im = _ssm_fwd(proj, pm, pmt, mb_re, mb_im, p8_re, p8_im, q_re, q_im, cm_re, cm_im, dskip, wglu)
    x1, yg = _even_out(yp, ys, proj, x, e_w_out)
    q = _norm_in(x1, g1, o_w_in, "odd_in")
    y2, cv = _conv_fwd(q, cw, cb, lg, lb)
    dx2, loss_lanes, d_gf = _odd_out_loss(y2, x1, o_w_out, gf, tgt)

    dcv, dz2, d_o_w_out, d_lg, d_lb = _odd_bwd_out(dx2, o_w_out, y2, cv, q, lg, lb)
    dval, dgate, d_cw, d_cb = _conv_bwd(dcv, q, cw)
    dx1, d_g1, d_o_w_in = _in_bwd([dval, dgate, dz2], o_w_in, x1, g1, dx2, "odd_in_bwd")
    dycat, dz, d_e_w_out = _even_bwd_out(dx1, e_w_out, yg, yp, ys, proj)
    dup, d_wp, d_ps = _pool_bwd(dycat, proj, wp, ps)
    (dus, d_mb_re, d_mb_im, d_cm_re, d_cm_im, da_re, da_im, d_dskip, d_wglu) = _ssm_bwd(
        dycat, proj, car_re, car_im, pm, pmt, mb_re, mb_im, p8_re, p8_im, q_re, q_im, qr_re, qr_im,
        cm_re, cm_im, dskip, wglu)
    dx, d_g0, d_e_w_in = _in_bwd([dup, dus, dz], e_w_in, x, g0, dx1, "even_in_bwd")

    d_abar_re = jnp.sum(da_re, axis=0).reshape(N_GROUPS, N_STATE)
    d_abar_im = jnp.sum(da_im, axis=0).reshape(N_GROUPS, N_STATE)
    d_bb_re = _block_diag_take(d_mb_re).transpose(0, 2, 1)
    d_bb_im = _block_diag_take(d_mb_im).transpose(0, 2, 1)
    d_log_dt, d_a_re, d_a_im, d_b_re, d_b_im = ssm_vjp((d_abar_re, d_abar_im, d_bb_re, d_bb_im))

    grads = {
        "even_norm": d_g0.reshape(-1), "even_w_in": d_e_w_in, "pool_w": d_wp, "pool_scale": d_ps.reshape(-1),
        "ssm_log_dt": d_log_dt, "ssm_a_re": d_a_re, "ssm_a_im": d_a_im, "ssm_b_re": d_b_re, "ssm_b_im": d_b_im,
        "ssm_c_re": _block_diag_take(d_cm_re), "ssm_c_im": _block_diag_take(d_cm_im),
        "ssm_d": d_dskip.reshape(-1), "ssm_w_glu": d_wglu, "even_w_out": d_e_w_out, "odd_norm": d_g1.reshape(-1),
        "odd_w_in": d_o_w_in, "conv_w": d_cw[:CONV_K], "conv_b": d_cb.reshape(-1), "conv_ln_g": d_lg.reshape(-1),
        "conv_ln_b": d_lb.reshape(-1), "odd_w_out": d_o_w_out, "final_norm": d_gf.reshape(-1),
    }
    return jnp.sum(loss_lanes), dx, grads


WEIGHT_NAMES = ("even_norm", "even_w_in", "pool_w", "pool_scale", "ssm_log_dt", "ssm_a_re", "ssm_a_im",
                "ssm_b_re", "ssm_b_im", "ssm_c_re", "ssm_c_im", "ssm_d", "ssm_w_glu", "even_w_out", "odd_norm",
                "odd_w_in", "conv_w", "conv_b", "conv_ln_g", "conv_ln_b", "odd_w_out", "final_norm")
SHARDED = tuple(n for n, _ in SHARD_BIG) + tuple(n for n, _ in SHARD_SMALL)


def _full_weights(shard, rep):
    pack = _pack_shard(shard)
    big = pack[:ROWS_BIG].astype(BF16)
    small = lax.bitcast_convert_type(pack[ROWS_BIG:ROWS_BIG + ROWS_SMALL], BF16).reshape(2 * ROWS_SMALL, D_MODEL)
    pad = jnp.zeros((ROWS_PACK - ROWS_BIG - 2 * ROWS_SMALL, D_MODEL), BF16)
    got = _gather_weights(jnp.concatenate([big, small, pad], axis=0))
    gb = got[:, :ROWS_BIG]
    gs = lax.bitcast_convert_type(got[:, ROWS_BIG:ROWS_BIG + 2 * ROWS_SMALL].reshape(4, ROWS_SMALL, D_MODEL, 2), F32)
    gs = gs.reshape(4, -1)[:, :35 * 256].reshape(4, 35, 256)
    w = dict(rep)
    w["even_w_in"] = _cols_full(gb[:, 0:512], 1024, 512)
    w["ssm_w_glu"] = _cols_full(gb[:, 512:640], 512, 256)
    w["even_w_out"] = gb[:, 640:896].reshape(1024, 1024)
    w["odd_w_in"] = _cols_full(gb[:, 896:1664], 1024, 768)
    w["odd_w_out"] = gb[:, 1664:1920].reshape(1024, 1024)
    w["odd_norm"] = gs[:, 0].reshape(-1)
    w["conv_w"] = gs[:, 1:32].transpose(1, 0, 2).reshape(CONV_K, 1024)
    w["conv_b"] = gs[:, 32].reshape(-1)
    w["conv_ln_g"] = gs[:, 33].reshape(-1)
    w["conv_ln_b"] = gs[:, 34].reshape(-1)
    return w


def _pack_grads(g):
    small = jnp.concatenate([g["odd_norm"].reshape(4, 1, 256), g["conv_w"].reshape(CONV_K, 4, 256).transpose(1, 0, 2),
                             g["conv_b"].reshape(4, 1, 256), g["conv_ln_g"].reshape(4, 1, 256),
                             g["conv_ln_b"].reshape(4, 1, 256)], axis=1).reshape(4, -1)
    small = jnp.pad(small, ((0, 0), (0, ROWS_SMALL * D_MODEL - small.shape[1]))).reshape(4, ROWS_SMALL, D_MODEL)
    parts = [_cols_split(g["even_w_in"], 512), _cols_split(g["ssm_w_glu"], 256), g["even_w_out"].reshape(4, 256, 1024),
             _cols_split(g["odd_w_in"], 768), g["odd_w_out"].reshape(4, 256, 1024), small,
             jnp.zeros((4, ROWS_PACK - ROWS_BIG - ROWS_SMALL, D_MODEL), F32)]
    return jnp.concatenate(parts, axis=1)


def kernel(x, even_norm, even_w_in, pool_w, pool_scale, ssm_log_dt, ssm_a_re, ssm_a_im, ssm_b_re, ssm_b_im, ssm_c_re, ssm_c_im, ssm_d, ssm_w_glu, even_w_out, odd_norm, odd_w_in, conv_w, conv_b, conv_ln_g, conv_ln_b, odd_w_out, final_norm, loss_target, m_even_norm, m_even_w_in, m_pool_w, m_pool_scale, m_ssm_log_dt, m_ssm_a_re, m_ssm_a_im, m_ssm_b_re, m_ssm_b_im, m_ssm_c_re, m_ssm_c_im, m_ssm_d, m_ssm_w_glu, m_even_w_out, m_odd_norm, m_odd_w_in, m_conv_w, m_conv_b, m_conv_ln_g, m_conv_ln_b, m_odd_w_out, m_final_norm, v_even_norm, v_even_w_in, v_pool_w, v_pool_scale, v_ssm_log_dt, v_ssm_a_re, v_ssm_a_im, v_ssm_b_re, v_ssm_b_im, v_ssm_c_re, v_ssm_c_im, v_ssm_d, v_ssm_w_glu, v_even_w_out, v_odd_norm, v_odd_w_in, v_conv_w, v_conv_b, v_conv_ln_g, v_conv_ln_b, v_odd_w_out, v_final_norm):
    ws = dict(zip(WEIGHT_NAMES, (even_norm, even_w_in, pool_w, pool_scale, ssm_log_dt, ssm_a_re, ssm_a_im, ssm_b_re,
                                 ssm_b_im, ssm_c_re, ssm_c_im, ssm_d, ssm_w_glu, even_w_out, odd_norm, odd_w_in,
                                 conv_w, conv_b, conv_ln_g, conv_ln_b, odd_w_out, final_norm)))
    ms = dict(zip(WEIGHT_NAMES, (m_even_norm, m_even_w_in, m_pool_w, m_pool_scale, m_ssm_log_dt, m_ssm_a_re,
                                 m_ssm_a_im, m_ssm_b_re, m_ssm_b_im, m_ssm_c_re, m_ssm_c_im, m_ssm_d, m_ssm_w_glu,
                                 m_even_w_out, m_odd_norm, m_odd_w_in, m_conv_w, m_conv_b, m_conv_ln_g, m_conv_ln_b,
                                 m_odd_w_out, m_final_norm)))
    vs = dict(zip(WEIGHT_NAMES, (v_even_norm, v_even_w_in, v_pool_w, v_pool_scale, v_ssm_log_dt, v_ssm_a_re,
                                 v_ssm_a_im, v_ssm_b_re, v_ssm_b_im, v_ssm_c_re, v_ssm_c_im, v_ssm_d, v_ssm_w_glu,
                                 v_even_w_out, v_odd_norm, v_odd_w_in, v_conv_w, v_conv_b, v_conv_ln_g, v_conv_ln_b,
                                 v_odd_w_out, v_final_norm)))
    lead = {n: a.shape for n, a in ws.items()}
    drop = lambda d: {n: (a[0] if n != "final_norm" else a) for n, a in d.items()}
    ws, ms, vs = drop(ws), drop(ms), drop(vs)

    shard = {n: ws[n] for n in SHARDED}
    rep = {n: ws[n] for n, _ in REP_NAMES}
    w_full = _full_weights(shard, rep)
    loss_part, grad_x, grads = _local_step(x[0], loss_target[0], w_full)
    loss = lax.psum(loss_part, ("x", "y", "c"))

    g_pack = _pack_grads(grads)
    got = _chip_scatter(_pair_add(g_pack, _pair_split(g_pack), BF16))
    g_mine = _sum_lead(got, "chip_sum")
    g_shard, d_shard, m_shard, v_shard = _adamw_halves(
        _pack_shard(shard), g_mine, _pair_join(g_mine), _pack_shard({n: ms[n] for n in SHARDED}),
        _pack_shard({n: vs[n] for n in SHARDED}))
    g_rep = _sum_lead(_gather_all(_pack_rep({n: grads[n] for n, _ in REP_NAMES})), "rep_sum")
    d_rep, m_rep, v_rep = _adamw(_pack_rep(rep), g_rep, _pack_rep({n: ms[n] for n, _ in REP_NAMES}),
                                 _pack_rep({n: vs[n] for n, _ in REP_NAMES}), "adamw_rep")

    def unpack(shard_buf, rep_buf):
        d = {**_unpack_shard(shard_buf), **_unpack_rep(rep_buf)}
        return [d[n].reshape(lead[n]) for n in WEIGHT_NAMES]

    return (loss, grad_x[None], *unpack(g_shard, g_rep), *unpack(d_shard, d_rep),
            *unpack(m_shard, m_rep), *unpack(v_shard, v_rep))
```

```python
import functools

import jax
import jax.numpy as jnp
from jax import lax
from jax.experimental import pallas as pl
from jax.experimental.pallas import tpu as pltpu

F32 = jnp.float32
BF16 = jnp.bfloat16
MESH = pl.DeviceIdType.MESH

D_MODEL = 1024
RMS_EPS = 1e-6
LN_EPS = 1e-5
N_GROUPS = 32
GROUP_DIM = 16
N_STATE = 64
STATES = N_GROUPS * N_STATE
SSM_W = 512
POOL_W = 512
CONV_K = 31
HALO = 32
POOL_HALO = 16

ADAM_LR = 0.001
ADAM_B1 = 0.9
ADAM_B2 = 0.999
ADAM_EPS = 1e-08
ADAM_WD = 0.01
ADAM_STEP = 10

TM = 256
SUB = 8
LCH = 512
SCAN_L = 1024
VMEM_LIMIT = 56 * 1024 * 1024

ROWS_BIG = 1920
ROWS_SMALL = 16
ROWS_PACK = 2048
REP_ROWS = 200
COMM_CHUNKS = 4


def _params(n_axes=1):
    return pltpu.CompilerParams(dimension_semantics=("arbitrary",) * n_axes, vmem_limit_bytes=VMEM_LIMIT)


def _rows(w, cb=0, rev=None, tm=TM):
    if rev is None:
        return pl.BlockSpec((tm, w), lambda i: (i, cb))
    return pl.BlockSpec((tm, w), lambda i: (rev - 1 - i, cb))


def _full(shape):
    n = len(shape)
    return pl.BlockSpec(shape, lambda i: (0,) * n)


def _prev(hr, w, cb=0, tm=TM):
    r = tm // hr
    return pl.BlockSpec((hr, w), lambda i: (jnp.maximum(i * r - 1, 0), cb))


def _next(hr, w, nrows, cb=0, tm=TM):
    r = tm // hr
    last = nrows // hr - 1
    return pl.BlockSpec((hr, w), lambda i: (jnp.minimum((i + 1) * r, last), cb))


def _dot(a, b):
    return jnp.dot(a, b, preferred_element_type=F32)


def _dot_nt(a, b):
    return lax.dot_general(a, b, (((1,), (1,)), ((), ())), preferred_element_type=F32)


def _dot_tn(a, b):
    return lax.dot_general(a, b, (((0,), (0,)), ((), ())), preferred_element_type=F32)


def _sig(x):
    return 1.0 / (1.0 + jnp.exp(-x))


def _zero_at_first(i, *refs):
    @pl.when(i == 0)
    def _():
        for r in refs:
            r[...] = jnp.zeros_like(r)


def _norm_in(x, g, w, name):
    t, n = x.shape[0], w.shape[1]

    def body(x_ref, g_ref, w_ref, o_ref):
        xv = x_ref[...]
        r = lax.rsqrt(jnp.mean(xv * xv, axis=-1, keepdims=True) + RMS_EPS)
        o_ref[...] = _dot((xv * r * g_ref[...]).astype(BF16), w_ref[...]).astype(BF16)

    return pl.pallas_call(
        body, name=name, grid=(t // TM,),
        in_specs=[_rows(D_MODEL), _full((1, D_MODEL)), _full(w.shape)],
        out_specs=_rows(n), out_shape=jax.ShapeDtypeStruct((t, n), BF16),
        compiler_params=_params())(x, g, w)


def _pool_sums(ext, g, forward):
    n = ext.shape[0]
    s = ext
    for step in range(g + 1):
        k = 1 << step
        s = s + pltpu.roll(s, k if forward else n - k, 0)
    return s


def _pool_fwd(proj, wp, ps):
    t = proj.shape[0]

    def body(u_ref, h_ref, wp_ref, ps_ref, y_ref):
        i = pl.program_id(0)
        pos = (i * TM + 1 + lax.broadcasted_iota(jnp.int32, (TM, 1), 0)).astype(F32)
        for g in range(4):
            sl = slice(128 * g, 128 * (g + 1))
            u = u_ref[:, sl].astype(F32)
            halo = jnp.where(i == 0, 0.0, h_ref[:, sl].astype(F32))
            s = _pool_sums(jnp.concatenate([halo, u], axis=0), g, True)[POOL_HALO:, :]
            pooled = s / jnp.minimum(pos, float(2 << g)) - u
            y_ref[:, sl] = _dot(pooled.astype(BF16), wp_ref[g]) * ps_ref[:, sl]

    return pl.pallas_call(
        body, name="pool_fwd", grid=(t // TM,),
        in_specs=[_rows(POOL_W, 0), _prev(POOL_HALO, POOL_W, 0), _full((4, 128, 128)), _full((1, POOL_W))],
        out_specs=_rows(POOL_W), out_shape=jax.ShapeDtypeStruct((t, POOL_W), F32),
        compiler_params=_params())(proj, proj, wp, ps)


SEG_LEN = TM // SUB


def _perm_matrix():
    p = jnp.arange(TM)
    src = (p % SUB) * SEG_LEN + p // SUB
    return (src[:, None] == jnp.arange(TM)[None, :]).astype(BF16)


def _cmul_add(are, aim, vre, vim, bre, bim):
    return are * vre - aim * vim + bre, are * vim + aim * vre + bim


def _segment_chain(ere, eim, qre, qim, cin_re, cin_im, row, up):
    for sh in (1, 2, 4):
        mre, mim = (qre[SUB - sh:SUB - sh + 1, :], qim[SUB - sh:SUB - sh + 1, :]) if up else \
                   (qre[sh - 1:sh, :], qim[sh - 1:sh, :])
        keep = (row < SUB - sh) if up else (row >= sh)
        sre = jnp.where(keep, pltpu.roll(ere, SUB - sh if up else sh, 0), 0.0)
        sim = jnp.where(keep, pltpu.roll(eim, SUB - sh if up else sh, 0), 0.0)
        ere, eim = _cmul_add(mre, mim, sre, sim, ere, eim)
    ere, eim = _cmul_add(qre, qim, cin_re, cin_im, ere, eim)
    keep = (row < SUB - 1) if up else (row >= 1)
    ent_re = jnp.where(keep, pltpu.roll(ere, SUB - 1 if up else 1, 0), cin_re)
    ent_im = jnp.where(keep, pltpu.roll(eim, SUB - 1 if up else 1, 0), cin_im)
    return ere, eim, ent_re, ent_im


def _scan_fwd_block(xs_re, xs_im, p8_re, p8_im, q_re, q_im, car_re, car_im, ent_re_ref, ent_im_ref):
    row = lax.broadcasted_iota(jnp.int32, (SUB, SCAN_L), 0)
    for j in range(STATES // SCAN_L):
        sl = slice(SCAN_L * j, SCAN_L * (j + 1))
        are, aim = p8_re[0:SUB, sl], p8_im[0:SUB, sl]

        def totals(i, v, sl=sl, are=are, aim=aim):
            r0 = pl.multiple_of(i * SUB, SUB)
            vre, vim = _cmul_add(are, aim, v[0], v[1], xs_re[pl.ds(r0, SUB), sl], xs_im[pl.ds(r0, SUB), sl])
            xs_re[pl.ds(r0, SUB), sl] = vre
            xs_im[pl.ds(r0, SUB), sl] = vim
            return vre, vim

        ere, eim = lax.fori_loop(1, SEG_LEN, totals, (xs_re[0:SUB, sl], xs_im[0:SUB, sl]), unroll=2)
        ere, eim, cre, cim = _segment_chain(ere, eim, q_re[:, sl], q_im[:, sl],
                                            car_re[:, sl], car_im[:, sl], row, False)
        car_re[:, sl] = jnp.broadcast_to(ere[SUB - 1:SUB, :], (SUB, SCAN_L))
        car_im[:, sl] = jnp.broadcast_to(eim[SUB - 1:SUB, :], (SUB, SCAN_L))
        if ent_re_ref is not None:
            ent_re_ref[:, sl] = cre
            ent_im_ref[:, sl] = cim

        def fix(i, c, sl=sl, cre=cre, cim=cim):
            r0 = pl.multiple_of(i * SUB, SUB)
            vre, vim = _cmul_add(p8_re[pl.ds(r0, SUB), sl], p8_im[pl.ds(r0, SUB), sl], cre, cim,
                                 xs_re[pl.ds(r0, SUB), sl], xs_im[pl.ds(r0, SUB), sl])
            xs_re[pl.ds(r0, SUB), sl] = vre
            xs_im[pl.ds(r0, SUB), sl] = vim
            return c

        lax.fori_loop(0, SEG_LEN, fix, 0, unroll=2)


def _unpermute(pmt_ref, v):
    hi = v.astype(BF16)
    lo = (v - hi.astype(F32)).astype(BF16)
    return _dot(pmt_ref[...], hi) + _dot(pmt_ref[...], lo)


def _ssm_fwd(proj, pm, pmt, mb_re, mb_im, p8_re, p8_im, q_re, q_im, cm_re, cm_im, dskip, wglu):
    t = proj.shape[0]
    nblk = t // TM

    def body(u_ref, pm_ref, pmt_ref, mbre, mbim, p8re, p8im, qre, qim, cmre, cmim, d_ref, wg_ref,
             y_ref, cre_ref, cim_ref, xs_re, xs_im, car_re, car_im, ysk):
        i = pl.program_id(0)
        _zero_at_first(i, car_re, car_im)
        cre_ref[0] = car_re[...]
        cim_ref[0] = car_im[...]
        us = _dot(pm_ref[...], u_ref[...])
        usb = us.astype(BF16)
        for j in range(4):
            xs_re[:, LCH * j:LCH * (j + 1)] = _dot(usb[:, 128 * j:128 * (j + 1)], mbre[j])
            xs_im[:, LCH * j:LCH * (j + 1)] = _dot(usb[:, 128 * j:128 * (j + 1)], mbim[j])
        _scan_fwd_block(xs_re, xs_im, p8re, p8im, qre, qim, car_re, car_im, None, None)
        for j in range(4):
            sl = slice(LCH * j, LCH * (j + 1))
            ysk[:, 128 * j:128 * (j + 1)] = (_dot_nt(xs_re[:, sl].astype(BF16), cmre[j])
                                             - _dot_nt(xs_im[:, sl].astype(BF16), cmim[j]))
        yv = ysk[...] + d_ref[...] * us
        gv = _dot(yv.astype(BF16), wg_ref[...])
        y_ref[...] = _unpermute(pmt_ref, gv[:, :SSM_W] * _sig(gv[:, SSM_W:]))

    blk = (4, 128, LCH)
    return pl.pallas_call(
        body, name="ssm_fwd", grid=(nblk,),
        in_specs=[_rows(SSM_W, 1), _full((TM, TM)), _full((TM, TM)), _full(blk), _full(blk),
                  _full((TM, STATES)), _full((TM, STATES)), _full((SUB, STATES)), _full((SUB, STATES)),
                  _full(blk), _full(blk), _full((1, SSM_W)), _full((SSM_W, 2 * SSM_W))],
        out_specs=[_rows(SSM_W), pl.BlockSpec((1, SUB, STATES), lambda i: (i, 0, 0)),
                   pl.BlockSpec((1, SUB, STATES), lambda i: (i, 0, 0))],
        out_shape=[jax.ShapeDtypeStruct((t, SSM_W), F32), jax.ShapeDtypeStruct((nblk, SUB, STATES), F32),
                   jax.ShapeDtypeStruct((nblk, SUB, STATES), F32)],
        scratch_shapes=[pltpu.VMEM((TM, STATES), F32), pltpu.VMEM((TM, STATES), F32),
                        pltpu.VMEM((SUB, STATES), F32), pltpu.VMEM((SUB, STATES), F32),
                        pltpu.VMEM((TM, SSM_W), F32)],
        compiler_params=_params())(proj, pm, pmt, mb_re, mb_im, p8_re, p8_im, q_re, q_im, cm_re, cm_im, dskip, wglu)


def _even_out(yp, ys, proj, x, w):
    t = x.shape[0]

    def body(yp_ref, ys_ref, z_ref, x_ref, w_ref, x1_ref, yg_ref):
        z = z_ref[...].astype(F32)
        sz = z * _sig(z)
        gp = (yp_ref[...] * sz[:, :POOL_W]).astype(BF16)
        gs = (ys_ref[...] * sz[:, POOL_W:]).astype(BF16)
        yg_ref[:, :POOL_W] = gp
        yg_ref[:, POOL_W:] = gs
        x1_ref[...] = x_ref[...] + _dot(gp, w_ref[:POOL_W, :]) + _dot(gs, w_ref[POOL_W:, :])

    return pl.pallas_call(
        body, name="even_out", grid=(t // TM,),
        in_specs=[_rows(POOL_W), _rows(SSM_W), _rows(D_MODEL, 1), _rows(D_MODEL), _full((D_MODEL, D_MODEL))],
        out_specs=[_rows(D_MODEL), _rows(D_MODEL)],
        out_shape=[jax.ShapeDtypeStruct((t, D_MODEL), F32), jax.ShapeDtypeStruct((t, D_MODEL), BF16)],
        compiler_params=_params())(yp, ys, proj, x, w)


def _phase_copies(ext, cp):
    n = cp.shape[1]
    for j in range(1, SUB):
        cp[j - 1] = ext[pl.ds(j, n), :]


def _shifted(ext, cp, off, nrows, sl):
    q, j = divmod(off, SUB)
    if j == 0:
        return ext[pl.ds(SUB * q, nrows), sl]
    return cp[j - 1, pl.ds(SUB * q, nrows), sl]


def _conv_taps(ext, cp, w_ref, first, nrows, sl, init):
    acc = init
    for k in range(CONV_K):
        acc = acc + w_ref[k:k + 1, sl] * _shifted(ext, cp, first(k), nrows, sl)
    return acc


def _conv_fwd(q, cw, cb, lg, lb):
    t = q.shape[0]

    def body(v_ref, g_ref, hv_ref, hg_ref, z_ref, w_ref, b_ref, lg_ref, lb_ref, y_ref, cv_ref, ext, cp):
        i = pl.program_id(0)
        ext[0:HALO, :] = jnp.where(i == 0, 0.0, hv_ref[...].astype(F32) * _sig(hg_ref[...].astype(F32)))
        ext[HALO:, :] = v_ref[...].astype(F32) * _sig(g_ref[...].astype(F32))
        _phase_copies(ext, cp)

        def lanes(c, carry):
            sl = pl.ds(pl.multiple_of(c * 128, 128), 128)
            cv_ref[:, sl] = _conv_taps(ext, cp, w_ref, lambda k: k + 2, TM, sl,
                                       jnp.broadcast_to(b_ref[:, sl], (TM, 128)))
            return carry

        lax.fori_loop(0, D_MODEL // 128, lanes, 0)
        cv = cv_ref[...]
        cc = cv - jnp.mean(cv, axis=-1, keepdims=True)
        rstd = lax.rsqrt(jnp.mean(cc * cc, axis=-1, keepdims=True) + LN_EPS)
        cl = cc * rstd * lg_ref[...] + lb_ref[...]
        z = z_ref[...].astype(F32)
        y_ref[...] = (cl * _sig(cl) * z * _sig(z)).astype(BF16)

    vec = _full((1, D_MODEL))
    return pl.pallas_call(
        body, name="conv_fwd", grid=(t // TM,),
        in_specs=[_rows(D_MODEL, 0), _rows(D_MODEL, 1), _prev(HALO, D_MODEL, 0), _prev(HALO, D_MODEL, 1),
                  _rows(D_MODEL, 2), _full((HALO, D_MODEL)), vec, vec, vec],
        out_specs=[_rows(D_MODEL), _rows(D_MODEL)],
        out_shape=[jax.ShapeDtypeStruct((t, D_MODEL), BF16), jax.ShapeDtypeStruct((t, D_MODEL), F32)],
        scratch_shapes=[pltpu.VMEM((TM + HALO, D_MODEL), F32),
                        pltpu.VMEM((SUB - 1, TM + HALO - SUB, D_MODEL), F32)],
        compiler_params=_params())(q, q, q, q, q, cw, cb, lg, lb)


def _odd_out_loss(y2, x1, w, gf, tgt):
    t = x1.shape[0]

    def body(y_ref, x_ref, w_ref, g_ref, t_ref, dx_ref, loss_ref, dg_ref):
        i = pl.program_id(0)
        _zero_at_first(i, loss_ref, dg_ref)
        x2 = x_ref[...] + _dot(y_ref[...], w_ref[...])
        r = lax.rsqrt(jnp.mean(x2 * x2, axis=-1, keepdims=True) + RMS_EPS)
        n = x2 * r
        e = n * g_ref[...] - t_ref[...]
        loss_ref[...] += jnp.sum(e * e, axis=0, keepdims=True) * (0.5 / D_MODEL)
        dout = e * (1.0 / D_MODEL)
        dg_ref[...] += jnp.sum(dout * n, axis=0, keepdims=True)
        dn = dout * g_ref[...]
        dx_ref[...] = r * (dn - n * jnp.mean(dn * n, axis=-1, keepdims=True))

    vec = _full((1, D_MODEL))
    return pl.pallas_call(
        body, name="odd_out_loss", grid=(t // TM,),
        in_specs=[_rows(D_MODEL), _rows(D_MODEL), _full((D_MODEL, D_MODEL)), vec, _rows(D_MODEL)],
        out_specs=[_rows(D_MODEL), vec, vec],
        out_shape=[jax.ShapeDtypeStruct((t, D_MODEL), F32), jax.ShapeDtypeStruct((1, D_MODEL), F32),
                   jax.ShapeDtypeStruct((1, D_MODEL), F32)],
        compiler_params=_params())(y2, x1, w, gf, tgt)


def _dsilu(z):
    s = _sig(z)
    return z * s, s * (1.0 + z * (1.0 - s))


def _odd_bwd_out(dx2, w, y2, cv, q, lg, lb):
    t = dx2.shape[0]

    def body(dx_ref, w_ref, y_ref, cv_ref, z_ref, lg_ref, lb_ref, dcv_ref, dz_ref, dw_ref, dlg_ref, dlb_ref):
        i = pl.program_id(0)
        _zero_at_first(i, dw_ref, dlg_ref, dlb_ref)
        dxb = dx_ref[...].astype(BF16)
        dy = _dot_nt(dxb, w_ref[...])
        dw_ref[...] += _dot_tn(y_ref[...], dxb)
        cv = cv_ref[...]
        cc = cv - jnp.mean(cv, axis=-1, keepdims=True)
        rstd = lax.rsqrt(jnp.mean(cc * cc, axis=-1, keepdims=True) + LN_EPS)
        cn = cc * rstd
        silu_c, dsilu_c = _dsilu(cn * lg_ref[...] + lb_ref[...])
        silu_z, dsilu_z = _dsilu(z_ref[...].astype(F32))
        dcl = dy * silu_z * dsilu_c
        dz_ref[...] = (dy * silu_c * dsilu_z).astype(BF16)
        dlg_ref[...] += jnp.sum(dcl * cn, axis=0, keepdims=True)
        dlb_ref[...] += jnp.sum(dcl, axis=0, keepdims=True)
        dcn = dcl * lg_ref[...]
        dcv_ref[...] = rstd * (dcn - jnp.mean(dcn, axis=-1, keepdims=True)
                               - cn * jnp.mean(dcn * cn, axis=-1, keepdims=True))

    vec = _full((1, D_MODEL))
    mat = _full((D_MODEL, D_MODEL))
    return pl.pallas_call(
        body, name="odd_bwd_out", grid=(t // TM,),
        in_specs=[_rows(D_MODEL), mat, _rows(D_MODEL), _rows(D_MODEL), _rows(D_MODEL, 2), vec, vec],
        out_specs=[_rows(D_MODEL), _rows(D_MODEL), mat, vec, vec],
        out_shape=[jax.ShapeDtypeStruct((t, D_MODEL), F32), jax.ShapeDtypeStruct((t, D_MODEL), BF16),
                   jax.ShapeDtypeStruct((D_MODEL, D_MODEL), F32), jax.ShapeDtypeStruct((1, D_MODEL), F32),
                   jax.ShapeDtypeStruct((1, D_MODEL), F32)],
        compiler_params=_params())(dx2, w, y2, cv, q, lg, lb)


def _conv_bwd(dcv, q, cw):
    t = dcv.shape[0]
    nblk = t // TM

    def body(d_ref, dn_ref, v_ref, g_ref, hv_ref, hg_ref, w_ref,
             dv_ref, dgt_ref, dw_ref, db_ref, gext, dext, dgl, gcp, dcp):
        i = pl.program_id(0)
        last = nblk - 1
        _zero_at_first(i, dw_ref, db_ref)
        v = v_ref[...].astype(F32)
        sg = _sig(g_ref[...].astype(F32))
        gext[0:HALO, :] = jnp.where(i == 0, 0.0, hv_ref[...].astype(F32) * _sig(hg_ref[...].astype(F32)))
        gext[HALO:, :] = v * sg
        d = d_ref[...]
        dext[0:TM, :] = d
        dext[TM:, :] = jnp.where(i == last, 0.0, dn_ref[...])
        _phase_copies(gext, gcp)
        _phase_copies(dext, dcp)
        db_ref[...] += jnp.sum(d, axis=0, keepdims=True)
        def lanes(c, carry):
            sl = pl.ds(pl.multiple_of(c * 128, 128), 128)
            dgl[:, sl] = _conv_taps(dext, dcp, w_ref, lambda k: 30 - k, TM, sl, jnp.zeros((TM, 128), F32))
            dc = d_ref[:, sl]
            for k in range(CONV_K):
                prod = dc * _shifted(gext, gcp, k + 2, TM, sl)
                dw_ref[SUB * k:SUB * (k + 1), sl] += prod.reshape(TM // SUB, SUB, 128).sum(axis=0)
            return carry

        lax.fori_loop(0, D_MODEL // 128, lanes, 0)
        dg = dgl[...]
        dv_ref[...] = (dg * sg).astype(BF16)
        dgt_ref[...] = (dg * v * sg * (1.0 - sg)).astype(BF16)

    return pl.pallas_call(
        body, name="conv_bwd", grid=(t // TM,),
        in_specs=[_rows(D_MODEL), _next(HALO, D_MODEL, t), _rows(D_MODEL, 0), _rows(D_MODEL, 1),
                  _prev(HALO, D_MODEL, 0), _prev(HALO, D_MODEL, 1), _full((HALO, D_MODEL))],
        out_specs=[_rows(D_MODEL), _rows(D_MODEL), _full((HALO * SUB, D_MODEL)), _full((1, D_MODEL))],
        out_shape=[jax.ShapeDtypeStruct((t, D_MODEL), BF16), jax.ShapeDtypeStruct((t, D_MODEL), BF16),
                   jax.ShapeDtypeStruct((HALO * SUB, D_MODEL), F32), jax.ShapeDtypeStruct((1, D_MODEL), F32)],
        scratch_shapes=[pltpu.VMEM((TM + HALO, D_MODEL), F32), pltpu.VMEM((TM + HALO, D_MODEL), F32),
                        pltpu.VMEM((TM, D_MODEL), F32),
                        pltpu.VMEM((SUB - 1, TM + HALO - SUB, D_MODEL), F32),
                        pltpu.VMEM((SUB - 1, TM + HALO - SUB, D_MODEL), F32)],
        compiler_params=_params())(dcv, dcv, q, q, q, q, cw)


def _in_bwd(dparts, w, x, g, dres, name):
    t = x.shape[0]
    widths = [p.shape[1] for p in dparts]
    npart = len(dparts)

    def body(*refs):
        d_refs = refs[:npart]
        w_ref, x_ref, g_ref, r_ref, dx_ref, dg_ref, dw_ref = refs[npart:]
        i = pl.program_id(0)
        _zero_at_first(i, dg_ref, dw_ref)
        xv = x_ref[...]
        r = lax.rsqrt(jnp.mean(xv * xv, axis=-1, keepdims=True) + RMS_EPS)
        n = xv * r
        h = (n * g_ref[...]).astype(BF16)
        off = 0
        dh = None
        for d_ref, wd in zip(d_refs, widths):
            d = d_ref[...]
            part = _dot_nt(d, w_ref[:, off:off + wd])
            dh = part if dh is None else dh + part
            dw_ref[:, off:off + wd] += _dot_tn(h, d)
            off += wd
        dg_ref[...] += jnp.sum(dh * n, axis=0, keepdims=True)
        dn = dh * g_ref[...]
        dx_ref[...] = r_ref[...] + r * (dn - n * jnp.mean(dn * n, axis=-1, keepdims=True))

    vec = _full((1, D_MODEL))
    once = pl.BlockSpec(w.shape, lambda i: (0, 0), pipeline_mode=pl.Buffered(1))
    return pl.pallas_call(
        body, name=name, grid=(t // TM,),
        in_specs=[_rows(wd) for wd in widths] + [once, _rows(D_MODEL), vec, _rows(D_MODEL)],
        out_specs=[_rows(D_MODEL), vec, once],
        out_shape=[jax.ShapeDtypeStruct((t, D_MODEL), F32), jax.ShapeDtypeStruct((1, D_MODEL), F32),
                   jax.ShapeDtypeStruct(w.shape, F32)],
        compiler_params=_params())(*dparts, w, x, g, dres)


def _even_bwd_out(dx1, w, yg, yp, ys, proj):
    t = dx1.shape[0]

    def body(dx_ref, w_ref, yg_ref, yp_ref, ys_ref, z_ref, dy_ref, dz_ref, dw_ref):
        i = pl.program_id(0)
        _zero_at_first(i, dw_ref)
        dxb = dx_ref[...].astype(BF16)
        dyg = _dot_nt(dxb, w_ref[...])
        dw_ref[...] += _dot_tn(yg_ref[...], dxb)
        silu_z, dsilu_z = _dsilu(z_ref[...].astype(F32))
        dy_ref[...] = (dyg * silu_z).astype(BF16)
        dz_ref[:, :POOL_W] = (dyg[:, :POOL_W] * yp_ref[...] * dsilu_z[:, :POOL_W]).astype(BF16)
        dz_ref[:, POOL_W:] = (dyg[:, POOL_W:] * ys_ref[...] * dsilu_z[:, POOL_W:]).astype(BF16)

    mat = _full((D_MODEL, D_MODEL))
    return pl.pallas_call(
        body, name="even_bwd_out", grid=(t // TM,),
        in_specs=[_rows(D_MODEL), mat, _rows(D_MODEL), _rows(POOL_W), _rows(SSM_W), _rows(D_MODEL, 1)],
        out_specs=[_rows(D_MODEL), _rows(D_MODEL), mat],
        out_shape=[jax.ShapeDtypeStruct((t, D_MODEL), BF16), jax.ShapeDtypeStruct((t, D_MODEL), BF16),
                   jax.ShapeDtypeStruct((D_MODEL, D_MODEL), F32)],
        compiler_params=_params())(dx1, w, yg, yp, ys, proj)


def _pool_bwd(dycat, proj, wp, ps):
    t = proj.shape[0]

    def body(dy_ref, dyn_ref, u_ref, h_ref, wp_ref, ps_ref, du_ref, dwp_ref, dps_ref):
        i = pl.program_id(0)
        last = t // TM - 1
        _zero_at_first(i, dwp_ref, dps_ref)
        pos = (i * TM + 1 + lax.broadcasted_iota(jnp.int32, (TM, 1), 0)).astype(F32)
        pos_ext = (i * TM + 1 + lax.broadcasted_iota(jnp.int32, (TM + POOL_HALO, 1), 0)).astype(F32)
        for g in range(4):
            sl = slice(128 * g, 128 * (g + 1))
            w = float(2 << g)
            u = u_ref[:, sl].astype(F32)
            halo = jnp.where(i == 0, 0.0, h_ref[:, sl].astype(F32))
            s = _pool_sums(jnp.concatenate([halo, u], axis=0), g, True)[POOL_HALO:, :]
            pooled = (s / jnp.minimum(pos, w) - u).astype(BF16)
            dy = dy_ref[:, sl].astype(F32)
            dps_ref[:, sl] += jnp.sum(dy * _dot(pooled, wp_ref[g]), axis=0, keepdims=True)
            dy_ext = jnp.concatenate([dy, jnp.where(i == last, 0.0, dyn_ref[:, sl].astype(F32))], axis=0)
            dmix = (dy_ext * ps_ref[:, sl]).astype(BF16)
            dwp_ref[g] += _dot_tn(pooled, dmix[:TM, :])
            dpool = _dot_nt(dmix, wp_ref[g])
            lead = _pool_sums(dpool / jnp.minimum(pos_ext, w), g, False)
            du_ref[:, sl] = (lead[:TM, :] - dpool[:TM, :]).astype(BF16)

    return pl.pallas_call(
        body, name="pool_bwd", grid=(t // TM,),
        in_specs=[_rows(POOL_W, 0), _next(POOL_HALO, POOL_W, t, 0), _rows(POOL_W, 0), _prev(POOL_HALO, POOL_W, 0),
                  _full((4, 128, 128)), _full((1, POOL_W))],
        out_specs=[_rows(POOL_W), _full((4, 128, 128)), _full((1, POOL_W))],
        out_shape=[jax.ShapeDtypeStruct((t, POOL_W), BF16), jax.ShapeDtypeStruct((4, 128, 128), F32),
                   jax.ShapeDtypeStruct((1, POOL_W), F32)],
        compiler_params=_params())(dycat, dycat, proj, proj, wp, ps)


def _ssm_bwd(dycat, proj, car_in_re, car_in_im, pm, pmt, mb_re, mb_im, p8_re, p8_im, q_re, q_im, qr_re, qr_im,
             cm_re, cm_im, dskip, wglu):
    t = proj.shape[0]
    nblk = t // TM

    def body(dy_ref, u_ref, cin_re, cin_im, pm_ref, pmt_ref, mbre, mbim, p8re, p8im, qre, qim, qrre, qrim,
             cmre, cmim, d_ref, wg_ref,
             du_ref, dmbre, dmbim, dcmre, dcmim, dare, daim, dd_ref, dwg_ref,
             xs_re, xs_im, gs_re, gs_im, car_re, car_im, ent_re, ent_im, gcar_re, gcar_im, ysk, dysk):
        i = pl.program_id(0)
        _zero_at_first(i, dmbre, dmbim, dcmre, dcmim, dare, daim, dd_ref, dwg_ref, gcar_re, gcar_im)
        us = _dot(pm_ref[...], u_ref[...])
        usb = us.astype(BF16)
        for j in range(4):
            xs_re[:, LCH * j:LCH * (j + 1)] = _dot(usb[:, 128 * j:128 * (j + 1)], mbre[j])
            xs_im[:, LCH * j:LCH * (j + 1)] = _dot(usb[:, 128 * j:128 * (j + 1)], mbim[j])
        car_re[...] = cin_re[0]
        car_im[...] = cin_im[0]
        _scan_fwd_block(xs_re, xs_im, p8re, p8im, qre, qim, car_re, car_im, ent_re, ent_im)
        for j in range(4):
            sl = slice(LCH * j, LCH * (j + 1))
            ysk[:, 128 * j:128 * (j + 1)] = (_dot_nt(xs_re[:, sl].astype(BF16), cmre[j])
                                             - _dot_nt(xs_im[:, sl].astype(BF16), cmim[j]))
        yvb = (ysk[...] + d_ref[...] * us).astype(BF16)
        gv = _dot(yvb, wg_ref[...])
        sg = _sig(gv[:, SSM_W:])
        dyss = _dot(pm_ref[...], dy_ref[...])
        dval = (dyss * sg).astype(BF16)
        dgate = (dyss * gv[:, :SSM_W] * sg * (1.0 - sg)).astype(BF16)
        dy = _dot_nt(dval, wg_ref[:, :SSM_W]) + _dot_nt(dgate, wg_ref[:, SSM_W:])
        dwg_ref[:, :SSM_W] += _dot_tn(yvb, dval)
        dwg_ref[:, SSM_W:] += _dot_tn(yvb, dgate)
        dd_ref[...] += jnp.sum(dy * us, axis=0, keepdims=True)
        dysk[...] = dy
        for j in range(4):
            sl = slice(LCH * j, LCH * (j + 1))
            dyj = dy[:, 128 * j:128 * (j + 1)].astype(BF16)
            gs_re[:, sl] = _dot(dyj, cmre[j])
            gs_im[:, sl] = -_dot(dyj, cmim[j])
            dcmre[j] += _dot_tn(dyj, xs_re[:, sl].astype(BF16))
            dcmim[j] -= _dot_tn(dyj, xs_im[:, sl].astype(BF16))
        row = lax.broadcasted_iota(jnp.int32, (SUB, SCAN_L), 0)
        for j in range(STATES // SCAN_L):
            sl = slice(SCAN_L * j, SCAN_L * (j + 1))
            are, aim = p8re[0:SUB, sl], -p8im[0:SUB, sl]

            def totals(k, v, sl=sl, are=are, aim=aim):
                r0 = pl.multiple_of((SEG_LEN - 2 - k) * SUB, SUB)
                vre, vim = _cmul_add(are, aim, v[0], v[1], gs_re[pl.ds(r0, SUB), sl], gs_im[pl.ds(r0, SUB), sl])
                gs_re[pl.ds(r0, SUB), sl] = vre
                gs_im[pl.ds(r0, SUB), sl] = vim
                return vre, vim

            top = (SEG_LEN - 1) * SUB
            fre, fim = lax.fori_loop(0, SEG_LEN - 1, totals,
                                     (gs_re[top:top + SUB, sl], gs_im[top:top + SUB, sl]), unroll=2)
            fre, fim, nre, nim = _segment_chain(fre, fim, qrre[:, sl], -qrim[:, sl],
                                                gcar_re[:, sl], gcar_im[:, sl], row, True)
            gcar_re[:, sl] = jnp.broadcast_to(fre[0:1, :], (SUB, SCAN_L))
            gcar_im[:, sl] = jnp.broadcast_to(fim[0:1, :], (SUB, SCAN_L))

            def fix(i2, acc, sl=sl, nre=nre, nim=nim):
                r0 = pl.multiple_of(i2 * SUB, SUB)
                rb = pl.multiple_of((SEG_LEN - 1 - i2) * SUB, SUB)
                gre, gim = _cmul_add(p8re[pl.ds(rb, SUB), sl], -p8im[pl.ds(rb, SUB), sl], nre, nim,
                                     gs_re[pl.ds(r0, SUB), sl], gs_im[pl.ds(r0, SUB), sl])
                gs_re[pl.ds(r0, SUB), sl] = gre
                gs_im[pl.ds(r0, SUB), sl] = gim
                rp = pl.multiple_of((i2 - 1) * SUB, SUB)
                xre, xim = xs_re[pl.ds(rp, SUB), sl], xs_im[pl.ds(rp, SUB), sl]
                return acc[0] + gre * xre + gim * xim, acc[1] + gim * xre - gre * xim

            g0re, g0im = _cmul_add(p8re[top:top + SUB, sl], -p8im[top:top + SUB, sl], nre, nim,
                                   gs_re[0:SUB, sl], gs_im[0:SUB, sl])
            gs_re[0:SUB, sl] = g0re
            gs_im[0:SUB, sl] = g0im
            ere, eim = ent_re[:, sl], ent_im[:, sl]
            acc0 = (dare[:, sl] + g0re * ere + g0im * eim, daim[:, sl] + g0im * ere - g0re * eim)
            are_acc, aim_acc = lax.fori_loop(1, SEG_LEN, fix, acc0, unroll=2)
            dare[:, sl] = are_acc
            daim[:, sl] = aim_acc
        for j in range(4):
            sl = slice(LCH * j, LCH * (j + 1))
            c4 = slice(128 * j, 128 * (j + 1))
            gre = gs_re[:, sl].astype(BF16)
            gim = gs_im[:, sl].astype(BF16)
            dmbre[j] += _dot_tn(usb[:, c4], gre)
            dmbim[j] += _dot_tn(usb[:, c4], gim)
            dysk[:, c4] = _dot_nt(gre, mbre[j]) + _dot_nt(gim, mbim[j]) + dysk[:, c4] * d_ref[:, c4]
        du_ref[...] = _dot(pmt_ref[...], dysk[...].astype(BF16)).astype(BF16)

    blk = (4, 128, LCH)
    pw = _full((SUB, STATES))
    p8 = _full((TM, STATES))
    perm = _full((TM, TM))
    car = pl.BlockSpec((1, SUB, STATES), lambda i: (nblk - 1 - i, 0, 0))
    big = lambda: pltpu.VMEM((TM, STATES), F32)
    small = lambda: pltpu.VMEM((SUB, STATES), F32)
    return pl.pallas_call(
        body, name="ssm_bwd", grid=(nblk,),
        in_specs=[_rows(SSM_W, 1, rev=nblk), _rows(SSM_W, 1, rev=nblk), car, car, perm, perm, _full(blk), _full(blk),
                  p8, p8, pw, pw, pw, pw, _full(blk), _full(blk), _full((1, SSM_W)), _full((SSM_W, 2 * SSM_W))],
        out_specs=[_rows(SSM_W, 0, rev=nblk), _full(blk), _full(blk), _full(blk), _full(blk), pw, pw,
                   _full((1, SSM_W)), _full((SSM_W, 2 * SSM_W))],
        out_shape=[jax.ShapeDtypeStruct((t, SSM_W), BF16)] + [jax.ShapeDtypeStruct(blk, F32)] * 4
        + [jax.ShapeDtypeStruct((SUB, STATES), F32)] * 2
        + [jax.ShapeDtypeStruct((1, SSM_W), F32), jax.ShapeDtypeStruct((SSM_W, 2 * SSM_W), F32)],
        scratch_shapes=[big(), big(), big(), big(), small(), small(), small(), small(), small(), small(),
                        pltpu.VMEM((TM, SSM_W), F32), pltpu.VMEM((TM, SSM_W), F32)],
        compiler_params=_params())(dycat, proj, car_in_re, car_in_im, pm, pmt, mb_re, mb_im, p8_re, p8_im,
                                   q_re, q_im, qr_re, qr_im, cm_re, cm_im, dskip, wglu)


def _adamw(w, g, m, v, name):
    rows = w.shape[0]
    tr = 256 if rows % 256 == 0 else rows
    c1 = 1.0 / (1.0 - ADAM_B1 ** ADAM_STEP)
    c2 = 1.0 / (1.0 - ADAM_B2 ** ADAM_STEP)

    def body(w_ref, g_ref, m_ref, v_ref, d_ref, nm_ref, nv_ref):
        gv = g_ref[...]
        m = ADAM_B1 * m_ref[...] + (1.0 - ADAM_B1) * gv
        v = ADAM_B2 * v_ref[...] + (1.0 - ADAM_B2) * (gv * gv)
        nm_ref[...] = m
        nv_ref[...] = v
        d_ref[...] = -ADAM_LR * ((m * c1) / (jnp.sqrt(v * c2) + ADAM_EPS) + ADAM_WD * w_ref[...])

    spec = pl.BlockSpec((tr, D_MODEL), lambda i: (i, 0))
    shp = jax.ShapeDtypeStruct((rows, D_MODEL), F32)
    return pl.pallas_call(
        body, name=name, grid=(rows // tr,), in_specs=[spec] * 4, out_specs=[spec] * 3, out_shape=[shp] * 3,
        compiler_params=_params())(w, g, m, v)


def _core_index():
    return lax.axis_index("c").astype(jnp.int32).reshape(1)


def _pair_add(g, theirs, out_dtype):
    n, half, _ = theirs.shape
    nb = half // 256

    def body(c_ref, a_ref, b_ref, o_ref):
        o_ref[...] = (a_ref[...] + b_ref[...]).astype(out_dtype)

    spec = pl.BlockSpec((1, 256, D_MODEL), lambda i, j, c: (i, j, 0))
    grid_spec = pltpu.PrefetchScalarGridSpec(
        num_scalar_prefetch=1, grid=(n, nb),
        in_specs=[pl.BlockSpec((1, 256, D_MODEL), lambda i, j, c: (i, c[0] * nb + j, 0)), spec], out_specs=spec)
    return pl.pallas_call(
        body, name="pair_add", grid_spec=grid_spec, out_shape=jax.ShapeDtypeStruct(theirs.shape, out_dtype),
        compiler_params=_params(2))(_core_index(), g, theirs)


def _adamw_halves(w, g_mine, g_theirs, m, v):
    half = g_mine.shape[0]
    nb = half // 256
    c1 = 1.0 / (1.0 - ADAM_B1 ** ADAM_STEP)
    c2 = 1.0 / (1.0 - ADAM_B2 ** ADAM_STEP)

    def body(c_ref, w_ref, gm_ref, gt_ref, m_ref, v_ref, g_ref, d_ref, nm_ref, nv_ref):
        gv = jnp.where(pl.program_id(0) // nb == c_ref[0], gm_ref[...], gt_ref[...])
        m = ADAM_B1 * m_ref[...] + (1.0 - ADAM_B1) * gv
        v = ADAM_B2 * v_ref[...] + (1.0 - ADAM_B2) * (gv * gv)
        g_ref[...] = gv
        nm_ref[...] = m
        nv_ref[...] = v
        d_ref[...] = -ADAM_LR * ((m * c1) / (jnp.sqrt(v * c2) + ADAM_EPS) + ADAM_WD * w_ref[...])

    spec = pl.BlockSpec((256, D_MODEL), lambda i, c: (i, 0))
    part = pl.BlockSpec((256, D_MODEL), lambda i, c: (i % nb, 0))
    shp = jax.ShapeDtypeStruct((2 * half, D_MODEL), F32)
    grid_spec = pltpu.PrefetchScalarGridSpec(
        num_scalar_prefetch=1, grid=(2 * nb,), in_specs=[spec, part, part, spec, spec], out_specs=[spec] * 4)
    return pl.pallas_call(
        body, name="adamw_shard", grid_spec=grid_spec, out_shape=[shp] * 4,
        compiler_params=_params())(_core_index(), w, g_mine, g_theirs, m, v)


def _sum_lead(a, name):
    n, rows, _ = a.shape
    tr = 256 if rows % 256 == 0 else rows

    def body(a_ref, o_ref):
        acc = a_ref[0].astype(F32)
        for k in range(1, n):
            acc = acc + a_ref[k].astype(F32)
        o_ref[...] = acc

    return pl.pallas_call(
        body, name=name, grid=(rows // tr,),
        in_specs=[pl.BlockSpec((n, tr, D_MODEL), lambda i: (0, i, 0))],
        out_specs=pl.BlockSpec((tr, D_MODEL), lambda i: (i, 0)),
        out_shape=jax.ShapeDtypeStruct((rows, D_MODEL), F32), compiler_params=_params())(a)


ANY = pl.BlockSpec(memory_space=pl.ANY)


def _mesh_pos():
    return lax.axis_index("x"), lax.axis_index("y"), lax.axis_index("c")


def _gather_weights(wb):
    rows = wb.shape[0]
    half = rows // 2
    ch = half // COMM_CHUNKS
    ncopy = 3 * COMM_CHUNKS

    def body(w_ref, o_ref, bounce, send_sems, recv_sems, local_sems):
        x, y, c = _mesh_pos()
        me = 2 * x + y
        sibling = (x, y, 1 - c)
        chips = [(1 - x, y), (x, 1 - y), (1 - x, 1 - y)]
        ids = [2 * chip[0] + chip[1] for chip in chips]

        def piece(q, h, k):
            return o_ref.at[q, pl.ds(h * half + k * ch, ch), :]

        def copy(s, q, h, k, to, src=None):
            return pltpu.make_async_remote_copy(
                src_ref=piece(q, h, k) if src is None else src, dst_ref=piece(q, h, k),
                send_sem=send_sems.at[s], recv_sem=recv_sems.at[s], device_id=to, device_id_type=MESH)

        load = pltpu.make_async_copy(w_ref, bounce, local_sems.at[0])
        store = pltpu.make_async_copy(bounce, o_ref.at[me], local_sems.at[1])
        load.start()
        first = [copy(j * COMM_CHUNKS + k, me, c, k, (*chip, c), src=w_ref.at[pl.ds(c * half + k * ch, ch), :])
                 for j, chip in enumerate(chips) for k in range(COMM_CHUNKS)]
        for cp in first:
            cp.start()
        load.wait()
        store.start()
        passed = []
        for j in range(3):
            for k in range(COMM_CHUNKS):
                s = j * COMM_CHUNKS + k
                copy(s, ids[j], c, k, (x, y, c)).wait_recv()
                fwd = copy(ncopy + s, ids[j], c, k, sibling)
                fwd.start()
                passed.append(fwd)
        for j in range(3):
            for k in range(COMM_CHUNKS):
                copy(ncopy + j * COMM_CHUNKS + k, ids[j], 1 - c, k, (x, y, c)).wait_recv()
        for cp in first + passed:
            cp.wait_send()
        store.wait()

    return pl.pallas_call(
        body, name="gather_weights", in_specs=[ANY], out_specs=ANY,
        out_shape=jax.ShapeDtypeStruct((4, rows, D_MODEL), wb.dtype),
        scratch_shapes=[pltpu.VMEM(wb.shape, wb.dtype), pltpu.SemaphoreType.DMA((2 * ncopy,)),
                        pltpu.SemaphoreType.DMA((2 * ncopy,)), pltpu.SemaphoreType.DMA((2,))],
        compiler_params=pltpu.CompilerParams(vmem_limit_bytes=VMEM_LIMIT),
    )(wb)


def _gather_all(v):
    def body(v_ref, o_ref, bounce, send_sems, recv_sems, local_sems):
        x, y, c = _mesh_pos()
        sibling = (x, y, 1 - c)
        chips = [(1 - x, y), (x, 1 - y), (1 - x, 1 - y)]

        def blk(px, py, pc):
            return o_ref.at[4 * px + 2 * py + pc]

        def copy(k, block, to, src=None):
            return pltpu.make_async_remote_copy(
                src_ref=blk(*block) if src is None else src, dst_ref=blk(*block),
                send_sem=send_sems.at[k], recv_sem=recv_sems.at[k], device_id=to, device_id_type=MESH)

        load = pltpu.make_async_copy(v_ref, bounce, local_sems.at[0])
        store = pltpu.make_async_copy(bounce, blk(x, y, c), local_sems.at[1])
        load.start()
        first = [copy(0, (x, y, c), sibling, src=v_ref)]
        first += [copy(1 + j, (x, y, c), (*chip, c), src=v_ref) for j, chip in enumerate(chips)]
        for cp in first:
            cp.start()
        load.wait()
        store.start()
        passed = [copy(4 + j, (*chip, c), sibling) for j, chip in enumerate(chips)]
        for j, chip in enumerate(chips):
            copy(1 + j, (*chip, c), (x, y, c)).wait_recv()
            passed[j].start()
        copy(0, (x, y, 1 - c), (x, y, c)).wait_recv()
        for j, chip in enumerate(chips):
            copy(4 + j, (*chip, 1 - c), (x, y, c)).wait_recv()
        for cp in first + passed:
            cp.wait_send()
        store.wait()

    return pl.pallas_call(
        body, name="gather_all", in_specs=[ANY], out_specs=ANY,
        out_shape=jax.ShapeDtypeStruct((8,) + v.shape, v.dtype),
        scratch_shapes=[pltpu.VMEM(v.shape, v.dtype), pltpu.SemaphoreType.DMA((7,)), pltpu.SemaphoreType.DMA((7,)),
                        pltpu.SemaphoreType.DMA((2,))],
    )(v)


def _pair_split(g):
    n, rows, _ = g.shape
    half = rows // 2
    ch = half // COMM_CHUNKS

    def body(g_ref, theirs_ref, send_sems, recv_sems):
        x, y, c = _mesh_pos()
        gives = [pltpu.make_async_remote_copy(
            src_ref=g_ref.at[q, pl.ds((1 - c) * half + k * ch, ch), :],
            dst_ref=theirs_ref.at[q, pl.ds(k * ch, ch), :],
            send_sem=send_sems.at[q * COMM_CHUNKS + k], recv_sem=recv_sems.at[q * COMM_CHUNKS + k],
            device_id=(x, y, 1 - c), device_id_type=MESH) for q in range(n) for k in range(COMM_CHUNKS)]
        for cp in gives:
            cp.start()
        for cp in gives:
            cp.wait()

    return pl.pallas_call(
        body, name="pair_split", in_specs=[ANY], out_specs=ANY,
        out_shape=jax.ShapeDtypeStruct((n, half, D_MODEL), g.dtype),
        scratch_shapes=[pltpu.SemaphoreType.DMA((n * COMM_CHUNKS,)), pltpu.SemaphoreType.DMA((n * COMM_CHUNKS,))],
    )(g)


def _chip_scatter(p):
    def body(p_ref, o_ref, bounce, send_sems, recv_sems, local_sems):
        x, y, c = _mesh_pos()
        me = 2 * x + y
        chips = [(1 - x, y), (x, 1 - y), (1 - x, 1 - y)]
        load = pltpu.make_async_copy(p_ref.at[me], bounce, local_sems.at[0])
        keep = pltpu.make_async_copy(bounce, o_ref.at[me], local_sems.at[1])
        load.start()
        sends = [pltpu.make_async_remote_copy(
            src_ref=p_ref.at[2 * chip[0] + chip[1]], dst_ref=o_ref.at[me],
            send_sem=send_sems.at[j], recv_sem=recv_sems.at[j], device_id=(*chip, c), device_id_type=MESH)
            for j, chip in enumerate(chips)]
        for cp in sends:
            cp.start()
        load.wait()
        keep.start()
        for j, chip in enumerate(chips):
            q = 2 * chip[0] + chip[1]
            pltpu.make_async_remote_copy(
                src_ref=p_ref.at[q], dst_ref=o_ref.at[q], send_sem=send_sems.at[j], recv_sem=recv_sems.at[j],
                device_id=(*chip, c), device_id_type=MESH).wait_recv()
        for cp in sends:
            cp.wait_send()
        keep.wait()

    return pl.pallas_call(
        body, name="chip_scatter", in_specs=[ANY], out_specs=ANY, out_shape=jax.ShapeDtypeStruct(p.shape, p.dtype),
        scratch_shapes=[pltpu.VMEM(p.shape[1:], p.dtype), pltpu.SemaphoreType.DMA((3,)),
                        pltpu.SemaphoreType.DMA((3,)), pltpu.SemaphoreType.DMA((2,))],
    )(p)


def _pair_join(r):
    rows = r.shape[0]
    ch = rows // COMM_CHUNKS

    def body(r_ref, o_ref, send_sems, recv_sems):
        x, y, c = _mesh_pos()
        gives = [pltpu.make_async_remote_copy(
            src_ref=r_ref.at[pl.ds(k * ch, ch), :], dst_ref=o_ref.at[pl.ds(k * ch, ch), :],
            send_sem=send_sems.at[k], recv_sem=recv_sems.at[k], device_id=(x, y, 1 - c), device_id_type=MESH)
            for k in range(COMM_CHUNKS)]
        for cp in gives:
            cp.start()
        for cp in gives:
            cp.wait()

    return pl.pallas_call(
        body, name="pair_join", in_specs=[ANY], out_specs=ANY, out_shape=jax.ShapeDtypeStruct(r.shape, r.dtype),
        scratch_shapes=[pltpu.SemaphoreType.DMA((COMM_CHUNKS,)), pltpu.SemaphoreType.DMA((COMM_CHUNKS,))],
    )(r)


SHARD_BIG = (("even_w_in", (1024, 512)), ("ssm_w_glu", (512, 256)), ("even_w_out", (256, 1024)),
             ("odd_w_in", (1024, 768)), ("odd_w_out", (256, 1024)))
SHARD_SMALL = (("odd_norm", 1), ("conv_w", CONV_K), ("conv_b", 1), ("conv_ln_g", 1), ("conv_ln_b", 1))
REP_NAMES = (("even_norm", (1024,)), ("pool_w", (4, 128, 128)), ("pool_scale", (512,)), ("ssm_log_dt", (32,)),
             ("ssm_a_re", (32, 64)), ("ssm_a_im", (32, 64)), ("ssm_b_re", (32, 64, 16)), ("ssm_b_im", (32, 64, 16)),
             ("ssm_c_re", (32, 16, 64)), ("ssm_c_im", (32, 16, 64)), ("ssm_d", (512,)), ("final_norm", (1024,)))


def _pack_shard(d):
    big = [d[n].reshape(-1, D_MODEL) for n, _ in SHARD_BIG]
    small = jnp.concatenate([d[n].reshape(r, 256) for n, r in SHARD_SMALL], axis=0).reshape(-1)
    small = jnp.pad(small, (0, ROWS_SMALL * D_MODEL - small.shape[0])).reshape(ROWS_SMALL, D_MODEL)
    pad = jnp.zeros((ROWS_PACK - ROWS_BIG - ROWS_SMALL, D_MODEL), F32)
    return jnp.concatenate(big + [small, pad], axis=0)


def _unpack_shard(buf):
    out = {}
    off = 0
    for n, shp in SHARD_BIG:
        rows = shp[0] * shp[1] // D_MODEL
        out[n] = buf[off:off + rows].reshape(shp)
        off += rows
    small = buf[ROWS_BIG:ROWS_BIG + ROWS_SMALL].reshape(-1)[:35 * 256].reshape(35, 256)
    off = 0
    for n, r in SHARD_SMALL:
        out[n] = small[off:off + r].reshape((r, 256) if r > 1 else (256,))
        off += r
    return out


def _pack_rep(d):
    flat = jnp.concatenate([d[n].reshape(-1) for n, _ in REP_NAMES])
    return jnp.pad(flat, (0, REP_ROWS * D_MODEL - flat.shape[0])).reshape(REP_ROWS, D_MODEL)


def _unpack_rep(buf):
    flat = buf.reshape(-1)
    out = {}
    off = 0
    for n, shp in REP_NAMES:
        size = 1
        for s in shp:
            size *= s
        out[n] = flat[off:off + size].reshape(shp)
        off += size
    return out


def _cols_full(g4, rows, cols):
    return g4.reshape(4, rows, cols).transpose(1, 0, 2).reshape(rows, 4 * cols)


def _cols_split(full, cols):
    rows = full.shape[0]
    return full.reshape(rows, 4, cols).transpose(1, 0, 2).reshape(4, -1, D_MODEL)


def _block_diag(a):
    a = a.reshape(4, 8, GROUP_DIM, N_STATE)
    eye = jnp.eye(8, dtype=a.dtype)
    return (a[:, :, :, None, :] * eye[None, :, None, :, None]).reshape(4, 128, LCH)


def _block_diag_take(m):
    m = m.reshape(4, 8, GROUP_DIM, 8, N_STATE)
    eye = jnp.eye(8, dtype=m.dtype)
    return jnp.sum(m * eye[None, :, None, :, None], axis=3).reshape(N_GROUPS, GROUP_DIM, N_STATE)


def _ssm_discretise(log_dt, a_re, a_im, b_re, b_im):
    dt = jnp.exp(log_dt)[:, None]
    mag = jnp.exp(a_re * dt)
    ang = a_im * dt
    abar_re = mag * jnp.cos(ang)
    abar_im = mag * jnp.sin(ang)
    den = a_re * a_re + a_im * a_im
    nr = abar_re - 1.0
    ni = abar_im
    k_re = (nr * a_re + ni * a_im) / den
    k_im = (ni * a_re - nr * a_im) / den
    bb_re = k_re[..., None] * b_re - k_im[..., None] * b_im
    bb_im = k_re[..., None] * b_im + k_im[..., None] * b_re
    return abar_re, abar_im, bb_re, bb_im


def _cumulative_powers(re, im, n):
    def mul(a, b):
        return a[0] * b[0] - a[1] * b[1], a[0] * b[1] + a[1] * b[0]
    return lax.associative_scan(mul, (jnp.broadcast_to(re, (n,) + re.shape), jnp.broadcast_to(im, (n,) + im.shape)))


def _scan_tables(abar_re, abar_im):
    p_re, p_im = _cumulative_powers(abar_re.reshape(-1), abar_im.reshape(-1), SEG_LEN)
    q_re, q_im = _cumulative_powers(p_re[-1], p_im[-1], SUB)
    return jnp.repeat(p_re, SUB, axis=0), jnp.repeat(p_im, SUB, axis=0), q_re, q_im


def _local_step(x, tgt, w):
    row = lambda a: a.reshape(1, -1)
    e_w_in = w["even_w_in"].astype(BF16)
    e_w_out = w["even_w_out"].astype(BF16)
    o_w_in = w["odd_w_in"].astype(BF16)
    o_w_out = w["odd_w_out"].astype(BF16)
    wglu = w["ssm_w_glu"].astype(BF16)
    wp = w["pool_w"].astype(BF16)
    ssm_in = (w["ssm_log_dt"], w["ssm_a_re"], w["ssm_a_im"], w["ssm_b_re"], w["ssm_b_im"])
    (abar_re, abar_im, bb_re, bb_im), ssm_vjp = jax.vjp(_ssm_discretise, *ssm_in)
    mb_re = _block_diag(bb_re.transpose(0, 2, 1)).astype(BF16)
    mb_im = _block_diag(bb_im.transpose(0, 2, 1)).astype(BF16)
    cm_re = _block_diag(w["ssm_c_re"]).astype(BF16)
    cm_im = _block_diag(w["ssm_c_im"]).astype(BF16)
    p8_re, p8_im, q_re, q_im = _scan_tables(abar_re, abar_im)
    qr_re, qr_im = q_re[::-1], q_im[::-1]
    pm = _perm_matrix()
    pmt = pm.T
    cw = jnp.pad(w["conv_w"], ((0, HALO - CONV_K), (0, 0)))
    g0, g1, gf = row(w["even_norm"]), row(w["odd_norm"]), row(w["final_norm"])
    ps, dskip = row(w["pool_scale"]), row(w["ssm_d"])
    cb, lg, lb = row(w["conv_b"]), row(w["conv_ln_g"]), row(w["conv_ln_b"])

    proj = _norm_in(x, g0, e_w_in, "even_in")
    yp = _pool_fwd(proj, wp, ps)
    ys, car_re, car_im = _ssm_fwd(proj, pm, pmt, mb_re, mb_im, p8_re, p8_im, q_re, q_im, cm_re, cm_im, dskip, wglu)
    x1, yg = _even_out(yp, ys, proj, x, e_w_out)
    q = _norm_in(x1, g1, o_w_in, "odd_in")
    y2, cv = _conv_fwd(q, cw, cb, lg, lb)
    dx2, loss_lanes, d_gf = _odd_out_loss(y2, x1, o_w_out, gf, tgt)

    dcv, dz2, d_o_w_out, d_lg, d_lb = _odd_bwd_out(dx2, o_w_out, y2, cv, q, lg, lb)
    dval, dgate, d_cw, d_cb = _conv_bwd(dcv, q, cw)
    dx1, d_g1, d_o_w_in = _in_bwd([dval, dgate, dz2], o_w_in, x1, g1, dx2, "odd_in_bwd")
    dycat, dz, d_e_w_out = _even_bwd_out(dx1, e_w_out, yg, yp, ys, proj)
    dup, d_wp, d_ps = _pool_bwd(dycat, proj, wp, ps)
    (dus, d_mb_re, d_mb_im, d_cm_re, d_cm_im, da_re, da_im, d_dskip, d_wglu) = _ssm_bwd(
        dycat, proj, car_re, car_im, pm, pmt, mb_re, mb_im, p8_re, p8_im, q_re, q_im, qr_re, qr_im,
        cm_re, cm_im, dskip, wglu)
    dx, d_g0, d_e_w_in = _in_bwd([dup, dus, dz], e_w_in, x, g0, dx1, "even_in_bwd")

    d_abar_re = jnp.sum(da_re, axis=0).reshape(N_GROUPS, N_STATE)
    d_abar_im = jnp.sum(da_im, axis=0).reshape(N_GROUPS, N_STATE)
    d_bb_re = _block_diag_take(d_mb_re).transpose(0, 2, 1)
    d_bb_im = _block_diag_take(d_mb_im).transpose(0, 2, 1)
    d_log_dt, d_a_re, d_a_im, d_b_re, d_b_im = ssm_vjp((d_abar_re, d_abar_im, d_bb_re, d_bb_im))

    grads = {
        "even_norm": d_g0.reshape(-1), "even_w_in": d_e_w_in, "pool_w": d_wp, "pool_scale": d_ps.reshape(-1),
        "ssm_log_dt": d_log_dt, "ssm_a_re": d_a_re, "ssm_a_im": d_a_im, "ssm_b_re": d_b_re, "ssm_b_im": d_b_im,
        "ssm_c_re": _block_diag_take(d_cm_re), "ssm_c_im": _block_diag_take(d_cm_im),
        "ssm_d": d_dskip.reshape(-1), "ssm_w_glu": d_wglu, "even_w_out": d_e_w_out, "odd_norm": d_g1.reshape(-1),
        "odd_w_in": d_o_w_in, "conv_w": d_cw.reshape(HALO, SUB, D_MODEL).sum(axis=1)[:CONV_K], "conv_b": d_cb.reshape(-1), "conv_ln_g": d_lg.reshape(-1),
        "conv_ln_b": d_lb.reshape(-1), "odd_w_out": d_o_w_out, "final_norm": d_gf.reshape(-1),
    }
    return jnp.sum(loss_lanes), dx, grads


WEIGHT_NAMES = ("even_norm", "even_w_in", "pool_w", "pool_scale", "ssm_log_dt", "ssm_a_re", "ssm_a_im",
                "ssm_b_re", "ssm_b_im", "ssm_c_re", "ssm_c_im", "ssm_d", "ssm_w_glu", "even_w_out", "odd_norm",
                "odd_w_in", "conv_w", "conv_b", "conv_ln_g", "conv_ln_b", "odd_w_out", "final_norm")
SHARDED = tuple(n for n, _ in SHARD_BIG) + tuple(n for n, _ in SHARD_SMALL)


def _full_weights(shard, rep):
    pack = _pack_shard(shard)
    big = pack[:ROWS_BIG].astype(BF16)
    small = lax.bitcast_convert_type(pack[ROWS_BIG:ROWS_BIG + ROWS_SMALL], BF16).reshape(2 * ROWS_SMALL, D_MODEL)
    pad = jnp.zeros((ROWS_PACK - ROWS_BIG - 2 * ROWS_SMALL, D_MODEL), BF16)
    got = _gather_weights(jnp.concatenate([big, small, pad], axis=0))
    gb = got[:, :ROWS_BIG]
    gs = lax.bitcast_convert_type(got[:, ROWS_BIG:ROWS_BIG + 2 * ROWS_SMALL].reshape(4, ROWS_SMALL, D_MODEL, 2), F32)
    gs = gs.reshape(4, -1)[:, :35 * 256].reshape(4, 35, 256)
    w = dict(rep)
    w["even_w_in"] = _cols_full(gb[:, 0:512], 1024, 512)
    w["ssm_w_glu"] = _cols_full(gb[:, 512:640], 512, 256)
    w["even_w_out"] = gb[:, 640:896].reshape(1024, 1024)
    w["odd_w_in"] = _cols_full(gb[:, 896:1664], 1024, 768)
    w["odd_w_out"] = gb[:, 1664:1920].reshape(1024, 1024)
    w["odd_norm"] = gs[:, 0].reshape(-1)
    w["conv_w"] = gs[:, 1:32].transpose(1, 0, 2).reshape(CONV_K, 1024)
    w["conv_b"] = gs[:, 32].reshape(-1)
    w["conv_ln_g"] = gs[:, 33].reshape(-1)
    w["conv_ln_b"] = gs[:, 34].reshape(-1)
    return w


def _pack_grads(g):
    small = jnp.concatenate([g["odd_norm"].reshape(4, 1, 256), g["conv_w"].reshape(CONV_K, 4, 256).transpose(1, 0, 2),
                             g["conv_b"].reshape(4, 1, 256), g["conv_ln_g"].reshape(4, 1, 256),
                             g["conv_ln_b"].reshape(4, 1, 256)], axis=1).reshape(4, -1)
    small = jnp.pad(small, ((0, 0), (0, ROWS_SMALL * D_MODEL - small.shape[1]))).reshape(4, ROWS_SMALL, D_MODEL)
    parts = [_cols_split(g["even_w_in"], 512), _cols_split(g["ssm_w_glu"], 256), g["even_w_out"].reshape(4, 256, 1024),
             _cols_split(g["odd_w_in"], 768), g["odd_w_out"].reshape(4, 256, 1024), small,
             jnp.zeros((4, ROWS_PACK - ROWS_BIG - ROWS_SMALL, D_MODEL), F32)]
    return jnp.concatenate(parts, axis=1)


def kernel(x, even_norm, even_w_in, pool_w, pool_scale, ssm_log_dt, ssm_a_re, ssm_a_im, ssm_b_re, ssm_b_im, ssm_c_re, ssm_c_im, ssm_d, ssm_w_glu, even_w_out, odd_norm, odd_w_in, conv_w, conv_b, conv_ln_g, conv_ln_b, odd_w_out, final_norm, loss_target, m_even_norm, m_even_w_in, m_pool_w, m_pool_scale, m_ssm_log_dt, m_ssm_a_re, m_ssm_a_im, m_ssm_b_re, m_ssm_b_im, m_ssm_c_re, m_ssm_c_im, m_ssm_d, m_ssm_w_glu, m_even_w_out, m_odd_norm, m_odd_w_in, m_conv_w, m_conv_b, m_conv_ln_g, m_conv_ln_b, m_odd_w_out, m_final_norm, v_even_norm, v_even_w_in, v_pool_w, v_pool_scale, v_ssm_log_dt, v_ssm_a_re, v_ssm_a_im, v_ssm_b_re, v_ssm_b_im, v_ssm_c_re, v_ssm_c_im, v_ssm_d, v_ssm_w_glu, v_even_w_out, v_odd_norm, v_odd_w_in, v_conv_w, v_conv_b, v_conv_ln_g, v_conv_ln_b, v_odd_w_out, v_final_norm):
    ws = dict(zip(WEIGHT_NAMES, (even_norm, even_w_in, pool_w, pool_scale, ssm_log_dt, ssm_a_re, ssm_a_im, ssm_b_re,
                                 ssm_b_im, ssm_c_re, ssm_c_im, ssm_d, ssm_w_glu, even_w_out, odd_norm, odd_w_in,
                                 conv_w, conv_b, conv_ln_g, conv_ln_b, odd_w_out, final_norm)))
    ms = dict(zip(WEIGHT_NAMES, (m_even_norm, m_even_w_in, m_pool_w, m_pool_scale, m_ssm_log_dt, m_ssm_a_re,
                                 m_ssm_a_im, m_ssm_b_re, m_ssm_b_im, m_ssm_c_re, m_ssm_c_im, m_ssm_d, m_ssm_w_glu,
                                 m_even_w_out, m_odd_norm, m_odd_w_in, m_conv_w, m_conv_b, m_conv_ln_g, m_conv_ln_b,
                                 m_odd_w_out, m_final_norm)))
    vs = dict(zip(WEIGHT_NAMES, (v_even_norm, v_even_w_in, v_pool_w, v_pool_scale, v_ssm_log_dt, v_ssm_a_re,
                                 v_ssm_a_im, v_ssm_b_re, v_ssm_b_im, v_ssm_c_re, v_ssm_c_im, v_ssm_d, v_ssm_w_glu,
                                 v_even_w_out, v_odd_norm, v_odd_w_in, v_conv_w, v_conv_b, v_conv_ln_g, v_conv_ln_b,
                                 v_odd_w_out, v_final_norm)))
    lead = {n: a.shape for n, a in ws.items()}
    drop = lambda d: {n: (a[0] if n != "final_norm" else a) for n, a in d.items()}
    ws, ms, vs = drop(ws), drop(ms), drop(vs)

    shard = {n: ws[n] for n in SHARDED}
    rep = {n: ws[n] for n, _ in REP_NAMES}
    w_full = _full_weights(shard, rep)
    loss_part, grad_x, grads = _local_step(x[0], loss_target[0], w_full)
    loss = lax.psum(loss_part, ("x", "y", "c"))

    g_pack = _pack_grads(grads)
    got = _chip_scatter(_pair_add(g_pack, _pair_split(g_pack), BF16))
    g_mine = _sum_lead(got, "chip_sum")
    g_shard, d_shard, m_shard, v_shard = _adamw_halves(
        _pack_shard(shard), g_mine, _pair_join(g_mine), _pack_shard({n: ms[n] for n in SHARDED}),
        _pack_shard({n: vs[n] for n in SHARDED}))
    g_rep = _sum_lead(_gather_all(_pack_rep({n: grads[n] for n, _ in REP_NAMES})), "rep_sum")
    d_rep, m_rep, v_rep = _adamw(_pack_rep(rep), g_rep, _pack_rep({n: ms[n] for n, _ in REP_NAMES}),
                                 _pack_rep({n: vs[n] for n, _ in REP_NAMES}), "adamw_rep")

    def unpack(shard_buf, rep_buf):
        d = {**_unpack_shard(shard_buf), **_unpack_rep(rep_buf)}
        return [d[n].reshape(lead[n]) for n in WEIGHT_NAMES]

    return (loss, grad_x[None], *unpack(g_shard, g_rep), *unpack(d_shard, d_rep),
            *unpack(m_shard, m_rep), *unpack(v_shard, v_rep))
```

```python
import functools

import jax
import jax.numpy as jnp
from jax import lax
from jax.experimental import pallas as pl
from jax.experimental.pallas import tpu as pltpu

F32 = jnp.float32
BF16 = jnp.bfloat16
MESH = pl.DeviceIdType.MESH

D_MODEL = 1024
RMS_EPS = 1e-6
LN_EPS = 1e-5
N_GROUPS = 32
GROUP_DIM = 16
N_STATE = 64
STATES = N_GROUPS * N_STATE
SSM_W = 512
POOL_W = 512
CONV_K = 31
HALO = 32
POOL_HALO = 16

ADAM_LR = 0.001
ADAM_B1 = 0.9
ADAM_B2 = 0.999
ADAM_EPS = 1e-08
ADAM_WD = 0.01
ADAM_STEP = 10

TM = 256
TM_MM = 512
SUB = 8
LCH = 512
SCAN_L = 1024
VMEM_LIMIT = 56 * 1024 * 1024

ROWS_BIG = 1920
ROWS_SMALL = 16
ROWS_PACK = 2048
REP_ROWS = 200
COMM_CHUNKS = 4


def _params(n_axes=1):
    return pltpu.CompilerParams(dimension_semantics=("arbitrary",) * n_axes, vmem_limit_bytes=VMEM_LIMIT)


def _rows(w, cb=0, rev=None, tm=TM):
    if rev is None:
        return pl.BlockSpec((tm, w), lambda i: (i, cb))
    return pl.BlockSpec((tm, w), lambda i: (rev - 1 - i, cb))


def _mm_rows(w, cb=0):
    return _rows(w, cb, tm=TM_MM)


def _full(shape):
    n = len(shape)
    return pl.BlockSpec(shape, lambda i: (0,) * n)


def _prev(hr, w, cb=0, tm=TM):
    r = tm // hr
    return pl.BlockSpec((hr, w), lambda i: (jnp.maximum(i * r - 1, 0), cb))


def _next(hr, w, nrows, cb=0, tm=TM):
    r = tm // hr
    last = nrows // hr - 1
    return pl.BlockSpec((hr, w), lambda i: (jnp.minimum((i + 1) * r, last), cb))


def _dot(a, b):
    return jnp.dot(a, b, preferred_element_type=F32)


def _dot_nt(a, b):
    return lax.dot_general(a, b, (((1,), (1,)), ((), ())), preferred_element_type=F32)


def _dot_tn(a, b):
    return lax.dot_general(a, b, (((0,), (0,)), ((), ())), preferred_element_type=F32)


def _sig(x):
    return 1.0 / (1.0 + jnp.exp(-x))


def _zero_at_first(i, *refs):
    @pl.when(i == 0)
    def _():
        for r in refs:
            r[...] = jnp.zeros_like(r)


def _norm_in(x, g, w, name):
    t, n = x.shape[0], w.shape[1]

    def body(x_ref, g_ref, w_ref, o_ref):
        xv = x_ref[...]
        r = lax.rsqrt(jnp.mean(xv * xv, axis=-1, keepdims=True) + RMS_EPS)
        o_ref[...] = _dot((xv * r * g_ref[...]).astype(BF16), w_ref[...]).astype(BF16)

    return pl.pallas_call(
        body, name=name, grid=(t // TM_MM,),
        in_specs=[_mm_rows(D_MODEL), _full((1, D_MODEL)), _full(w.shape)],
        out_specs=_mm_rows(n), out_shape=jax.ShapeDtypeStruct((t, n), BF16),
        compiler_params=_params())(x, g, w)


def _pool_sums(ext, g, forward):
    n = ext.shape[0]
    s = ext
    for step in range(g + 1):
        k = 1 << step
        s = s + pltpu.roll(s, k if forward else n - k, 0)
    return s


def _pool_fwd(proj, wp, ps):
    t = proj.shape[0]

    def body(u_ref, h_ref, wp_ref, ps_ref, y_ref):
        i = pl.program_id(0)
        pos = (i * TM + 1 + lax.broadcasted_iota(jnp.int32, (TM, 1), 0)).astype(F32)
        for g in range(4):
            sl = slice(128 * g, 128 * (g + 1))
            u = u_ref[:, sl].astype(F32)
            halo = jnp.where(i == 0, 0.0, h_ref[:, sl].astype(F32))
            s = _pool_sums(jnp.concatenate([halo, u], axis=0), g, True)[POOL_HALO:, :]
            pooled = s / jnp.minimum(pos, float(2 << g)) - u
            y_ref[:, sl] = _dot(pooled.astype(BF16), wp_ref[g]) * ps_ref[:, sl]

    return pl.pallas_call(
        body, name="pool_fwd", grid=(t // TM,),
        in_specs=[_rows(POOL_W, 0), _prev(POOL_HALO, POOL_W, 0), _full((4, 128, 128)), _full((1, POOL_W))],
        out_specs=_rows(POOL_W), out_shape=jax.ShapeDtypeStruct((t, POOL_W), F32),
        compiler_params=_params())(proj, proj, wp, ps)


SEG_LEN = TM // SUB


def _perm_matrix():
    p = jnp.arange(TM)
    src = (p % SUB) * SEG_LEN + p // SUB
    return (src[:, None] == jnp.arange(TM)[None, :]).astype(BF16)


def _cmul_add(are, aim, vre, vim, bre, bim):
    return are * vre - aim * vim + bre, are * vim + aim * vre + bim


def _segment_chain(ere, eim, qre, qim, cin_re, cin_im, row, up):
    for sh in (1, 2, 4):
        mre, mim = (qre[SUB - sh:SUB - sh + 1, :], qim[SUB - sh:SUB - sh + 1, :]) if up else \
                   (qre[sh - 1:sh, :], qim[sh - 1:sh, :])
        keep = (row < SUB - sh) if up else (row >= sh)
        sre = jnp.where(keep, pltpu.roll(ere, SUB - sh if up else sh, 0), 0.0)
        sim = jnp.where(keep, pltpu.roll(eim, SUB - sh if up else sh, 0), 0.0)
        ere, eim = _cmul_add(mre, mim, sre, sim, ere, eim)
    ere, eim = _cmul_add(qre, qim, cin_re, cin_im, ere, eim)
    keep = (row < SUB - 1) if up else (row >= 1)
    ent_re = jnp.where(keep, pltpu.roll(ere, SUB - 1 if up else 1, 0), cin_re)
    ent_im = jnp.where(keep, pltpu.roll(eim, SUB - 1 if up else 1, 0), cin_im)
    return ere, eim, ent_re, ent_im


def _scan_fwd_block(xs_re, xs_im, p8_re, p8_im, q_re, q_im, car_re, car_im, ent_re_ref, ent_im_ref):
    row = lax.broadcasted_iota(jnp.int32, (SUB, SCAN_L), 0)
    for j in range(STATES // SCAN_L):
        sl = slice(SCAN_L * j, SCAN_L * (j + 1))
        are, aim = p8_re[0:SUB, sl], p8_im[0:SUB, sl]

        def totals(i, v, sl=sl, are=are, aim=aim):
            r0 = pl.multiple_of(i * SUB, SUB)
            vre, vim = _cmul_add(are, aim, v[0], v[1], xs_re[pl.ds(r0, SUB), sl], xs_im[pl.ds(r0, SUB), sl])
            xs_re[pl.ds(r0, SUB), sl] = vre
            xs_im[pl.ds(r0, SUB), sl] = vim
            return vre, vim

        ere, eim = lax.fori_loop(1, SEG_LEN, totals, (xs_re[0:SUB, sl], xs_im[0:SUB, sl]), unroll=2)
        ere, eim, cre, cim = _segment_chain(ere, eim, q_re[:, sl], q_im[:, sl],
                                            car_re[:, sl], car_im[:, sl], row, False)
        car_re[:, sl] = jnp.broadcast_to(ere[SUB - 1:SUB, :], (SUB, SCAN_L))
        car_im[:, sl] = jnp.broadcast_to(eim[SUB - 1:SUB, :], (SUB, SCAN_L))
        if ent_re_ref is not None:
            ent_re_ref[:, sl] = cre
            ent_im_ref[:, sl] = cim

        def fix(i, c, sl=sl, cre=cre, cim=cim):
            r0 = pl.multiple_of(i * SUB, SUB)
            vre, vim = _cmul_add(p8_re[pl.ds(r0, SUB), sl], p8_im[pl.ds(r0, SUB), sl], cre, cim,
                                 xs_re[pl.ds(r0, SUB), sl], xs_im[pl.ds(r0, SUB), sl])
            xs_re[pl.ds(r0, SUB), sl] = vre
            xs_im[pl.ds(r0, SUB), sl] = vim
            return c

        lax.fori_loop(0, SEG_LEN, fix, 0, unroll=2)


def _unpermute(pmt_ref, v):
    hi = v.astype(BF16)
    lo = (v - hi.astype(F32)).astype(BF16)
    return _dot(pmt_ref[...], hi) + _dot(pmt_ref[...], lo)


def _ssm_fwd(proj, pm, pmt, mb_re, mb_im, p8_re, p8_im, q_re, q_im, cm_re, cm_im, dskip, wglu):
    t = proj.shape[0]
    nblk = t // TM

    def body(u_ref, pm_ref, pmt_ref, mbre, mbim, p8re, p8im, qre, qim, cmre, cmim, d_ref, wg_ref,
             y_ref, cre_ref, cim_ref, xs_re, xs_im, car_re, car_im, ysk):
        i = pl.program_id(0)
        _zero_at_first(i, car_re, car_im)
        cre_ref[0] = car_re[...]
        cim_ref[0] = car_im[...]
        us = _dot(pm_ref[...], u_ref[...])
        usb = us.astype(BF16)
        for j in range(4):
            xs_re[:, LCH * j:LCH * (j + 1)] = _dot(usb[:, 128 * j:128 * (j + 1)], mbre[j])
            xs_im[:, LCH * j:LCH * (j + 1)] = _dot(usb[:, 128 * j:128 * (j + 1)], mbim[j])
        _scan_fwd_block(xs_re, xs_im, p8re, p8im, qre, qim, car_re, car_im, None, None)
        for j in range(4):
            sl = slice(LCH * j, LCH * (j + 1))
            ysk[:, 128 * j:128 * (j + 1)] = (_dot_nt(xs_re[:, sl].astype(BF16), cmre[j])
                                             - _dot_nt(xs_im[:, sl].astype(BF16), cmim[j]))
        yv = ysk[...] + d_ref[...] * us
        gv = _dot(yv.astype(BF16), wg_ref[...])
        y_ref[...] = _unpermute(pmt_ref, gv[:, :SSM_W] * _sig(gv[:, SSM_W:]))

    blk = (4, 128, LCH)
    return pl.pallas_call(
        body, name="ssm_fwd", grid=(nblk,),
        in_specs=[_rows(SSM_W, 1), _full((TM, TM)), _full((TM, TM)), _full(blk), _full(blk),
                  _full((TM, STATES)), _full((TM, STATES)), _full((SUB, STATES)), _full((SUB, STATES)),
                  _full(blk), _full(blk), _full((1, SSM_W)), _full((SSM_W, 2 * SSM_W))],
        out_specs=[_rows(SSM_W), pl.BlockSpec((1, SUB, STATES), lambda i: (i, 0, 0)),
                   pl.BlockSpec((1, SUB, STATES), lambda i: (i, 0, 0))],
        out_shape=[jax.ShapeDtypeStruct((t, SSM_W), F32), jax.ShapeDtypeStruct((nblk, SUB, STATES), F32),
                   jax.ShapeDtypeStruct((nblk, SUB, STATES), F32)],
        scratch_shapes=[pltpu.VMEM((TM, STATES), F32), pltpu.VMEM((TM, STATES), F32),
                        pltpu.VMEM((SUB, STATES), F32), pltpu.VMEM((SUB, STATES), F32),
                        pltpu.VMEM((TM, SSM_W), F32)],
        compiler_params=_params())(proj, pm, pmt, mb_re, mb_im, p8_re, p8_im, q_re, q_im, cm_re, cm_im, dskip, wglu)


def _even_out(yp, ys, proj, x, w):
    t = x.shape[0]

    def body(yp_ref, ys_ref, z_ref, x_ref, w_ref, x1_ref, yg_ref):
        z = z_ref[...].astype(F32)
        sz = z * _sig(z)
        gp = (yp_ref[...] * sz[:, :POOL_W]).astype(BF16)
        gs = (ys_ref[...] * sz[:, POOL_W:]).astype(BF16)
        yg_ref[:, :POOL_W] = gp
        yg_ref[:, POOL_W:] = gs
        x1_ref[...] = x_ref[...] + _dot(gp, w_ref[:POOL_W, :]) + _dot(gs, w_ref[POOL_W:, :])

    return pl.pallas_call(
        body, name="even_out", grid=(t // TM_MM,),
        in_specs=[_mm_rows(POOL_W), _mm_rows(SSM_W), _mm_rows(D_MODEL, 1), _mm_rows(D_MODEL),
                  _full((D_MODEL, D_MODEL))],
        out_specs=[_mm_rows(D_MODEL), _mm_rows(D_MODEL)],
        out_shape=[jax.ShapeDtypeStruct((t, D_MODEL), F32), jax.ShapeDtypeStruct((t, D_MODEL), BF16)],
        compiler_params=_params())(yp, ys, proj, x, w)


def _phase_copies(ext, cp):
    n = cp.shape[1]
    for j in range(1, SUB):
        cp[j - 1] = ext[pl.ds(j, n), :]


def _shifted(ext, cp, off, nrows, sl, row0=0):
    q, j = divmod(off, SUB)
    if j == 0:
        return ext[pl.ds(row0 + SUB * q, nrows), sl]
    return cp[j - 1, pl.ds(row0 + SUB * q, nrows), sl]


def _conv_taps(ext, cp, w_ref, first, nrows, sl, init, row0=0):
    acc = init
    for k in range(CONV_K):
        acc = acc + w_ref[k:k + 1, sl] * _shifted(ext, cp, first(k), nrows, sl, row0)
    return acc


def _conv_fwd(q, cw, cb, lg, lb):
    t = q.shape[0]

    def body(v_ref, g_ref, hv_ref, hg_ref, z_ref, w_ref, b_ref, lg_ref, lb_ref, y_ref, cv_ref, ext, cp):
        i = pl.program_id(0)
        ext[0:HALO, :] = jnp.where(i == 0, 0.0, hv_ref[...].astype(F32) * _sig(hg_ref[...].astype(F32)))
        ext[HALO:, :] = v_ref[...].astype(F32) * _sig(g_ref[...].astype(F32))
        _phase_copies(ext, cp)

        def lanes(c, carry):
            sl = pl.ds(pl.multiple_of(c * 128, 128), 128)
            cv_ref[:, sl] = _conv_taps(ext, cp, w_ref, lambda k: k + 2, TM, sl,
                                       jnp.broadcast_to(b_ref[:, sl], (TM, 128)))
            return carry

        lax.fori_loop(0, D_MODEL // 128, lanes, 0)
        cv = cv_ref[...]
        cc = cv - jnp.mean(cv, axis=-1, keepdims=True)
        rstd = lax.rsqrt(jnp.mean(cc * cc, axis=-1, keepdims=True) + LN_EPS)
        cl = cc * rstd * lg_ref[...] + lb_ref[...]
        z = z_ref[...].astype(F32)
        y_ref[...] = (cl * _sig(cl) * z * _sig(z)).astype(BF16)

    vec = _full((1, D_MODEL))
    return pl.pallas_call(
        body, name="conv_fwd", grid=(t // TM,),
        in_specs=[_rows(D_MODEL, 0), _rows(D_MODEL, 1), _prev(HALO, D_MODEL, 0), _prev(HALO, D_MODEL, 1),
                  _rows(D_MODEL, 2), _full((HALO, D_MODEL)), vec, vec, vec],
        out_specs=[_rows(D_MODEL), _rows(D_MODEL)],
        out_shape=[jax.ShapeDtypeStruct((t, D_MODEL), BF16), jax.ShapeDtypeStruct((t, D_MODEL), F32)],
        scratch_shapes=[pltpu.VMEM((TM + HALO, D_MODEL), F32),
                        pltpu.VMEM((SUB - 1, TM + HALO - SUB, D_MODEL), F32)],
        compiler_params=_params())(q, q, q, q, q, cw, cb, lg, lb)


def _odd_out_loss(y2, x1, w, gf, tgt):
    t = x1.shape[0]

    def body(y_ref, x_ref, w_ref, g_ref, t_ref, dx_ref, loss_ref, dg_ref):
        i = pl.program_id(0)
        _zero_at_first(i, loss_ref, dg_ref)
        x2 = x_ref[...] + _dot(y_ref[...], w_ref[...])
        r = lax.rsqrt(jnp.mean(x2 * x2, axis=-1, keepdims=True) + RMS_EPS)
        n = x2 * r
        e = n * g_ref[...] - t_ref[...]
        loss_ref[...] += jnp.sum(e * e, axis=0, keepdims=True) * (0.5 / D_MODEL)
        dout = e * (1.0 / D_MODEL)
        dg_ref[...] += jnp.sum(dout * n, axis=0, keepdims=True)
        dn = dout * g_ref[...]
        dx_ref[...] = r * (dn - n * jnp.mean(dn * n, axis=-1, keepdims=True))

    vec = _full((1, D_MODEL))
    return pl.pallas_call(
        body, name="odd_out_loss", grid=(t // TM_MM,),
        in_specs=[_mm_rows(D_MODEL), _mm_rows(D_MODEL), _full((D_MODEL, D_MODEL)), vec, _mm_rows(D_MODEL)],
        out_specs=[_mm_rows(D_MODEL), vec, vec],
        out_shape=[jax.ShapeDtypeStruct((t, D_MODEL), F32), jax.ShapeDtypeStruct((1, D_MODEL), F32),
                   jax.ShapeDtypeStruct((1, D_MODEL), F32)],
        compiler_params=_params())(y2, x1, w, gf, tgt)


def _dsilu(z):
    s = _sig(z)
    return z * s, s * (1.0 + z * (1.0 - s))


def _odd_bwd_out(dx2, w, y2, cv, q, lg, lb):
    t = dx2.shape[0]

    def body(dx_ref, w_ref, y_ref, cv_ref, z_ref, lg_ref, lb_ref, dcv_ref, dz_ref, dw_ref, dlg_ref, dlb_ref):
        i = pl.program_id(0)
        _zero_at_first(i, dw_ref, dlg_ref, dlb_ref)
        dxb = dx_ref[...].astype(BF16)
        dy = _dot_nt(dxb, w_ref[...])
        dw_ref[...] += _dot_tn(y_ref[...], dxb)
        cv = cv_ref[...]
        cc = cv - jnp.mean(cv, axis=-1, keepdims=True)
        rstd = lax.rsqrt(jnp.mean(cc * cc, axis=-1, keepdims=True) + LN_EPS)
        cn = cc * rstd
        silu_c, dsilu_c = _dsilu(cn * lg_ref[...] + lb_ref[...])
        silu_z, dsilu_z = _dsilu(z_ref[...].astype(F32))
        dcl = dy * silu_z * dsilu_c
        dz_ref[...] = (dy * silu_c * dsilu_z).astype(BF16)
        dlg_ref[...] += jnp.sum(dcl * cn, axis=0, keepdims=True)
        dlb_ref[...] += jnp.sum(dcl, axis=0, keepdims=True)
        dcn = dcl * lg_ref[...]
        dcv_ref[...] = rstd * (dcn - jnp.mean(dcn, axis=-1, keepdims=True)
                               - cn * jnp.mean(dcn * cn, axis=-1, keepdims=True))

    vec = _full((1, D_MODEL))
    mat = _full((D_MODEL, D_MODEL))
    return pl.pallas_call(
        body, name="odd_bwd_out", grid=(t // TM_MM,),
        in_specs=[_mm_rows(D_MODEL), mat, _mm_rows(D_MODEL), _mm_rows(D_MODEL), _mm_rows(D_MODEL, 2), vec, vec],
        out_specs=[_mm_rows(D_MODEL), _mm_rows(D_MODEL), mat, vec, vec],
        out_shape=[jax.ShapeDtypeStruct((t, D_MODEL), F32), jax.ShapeDtypeStruct((t, D_MODEL), BF16),
                   jax.ShapeDtypeStruct((D_MODEL, D_MODEL), F32), jax.ShapeDtypeStruct((1, D_MODEL), F32),
                   jax.ShapeDtypeStruct((1, D_MODEL), F32)],
        compiler_params=_params())(dx2, w, y2, cv, q, lg, lb)


def _conv_bwd(dcv, q, cw):
    t = dcv.shape[0]
    nblk = t // TM

    def body(d_ref, dn_ref, v_ref, g_ref, hv_ref, hg_ref, w_ref,
             dv_ref, dgt_ref, dw_ref, db_ref, gext, dext, dgl, gcp, dcp):
        i = pl.program_id(0)
        last = nblk - 1
        _zero_at_first(i, dw_ref, db_ref)
        v = v_ref[...].astype(F32)
        sg = _sig(g_ref[...].astype(F32))
        gext[0:HALO, :] = jnp.where(i == 0, 0.0, hv_ref[...].astype(F32) * _sig(hg_ref[...].astype(F32)))
        gext[HALO:, :] = v * sg
        d = d_ref[...]
        dext[0:TM, :] = d
        dext[TM:, :] = jnp.where(i == last, 0.0, dn_ref[...])
        _phase_copies(gext, gcp)
        _phase_copies(dext, dcp)
        db_ref[...] += jnp.sum(d, axis=0, keepdims=True)
        def lanes(c, carry):
            sl = pl.ds(pl.multiple_of(c * 128, 128), 128)
            dgl[:, sl] = _conv_taps(dext, dcp, w_ref, lambda k: 30 - k, TM, sl, jnp.zeros((TM, 128), F32))
            return carry

        def lanes_w(c, carry):
            sl = pl.ds(pl.multiple_of(c * 128, 128), 128)
            sums = [None] * CONV_K
            for r0 in range(0, TM, SUB):
                dt = d_ref[r0:r0 + SUB, sl]
                for k in range(CONV_K):
                    prod = dt * _shifted(gext, gcp, k + 2, SUB, sl, r0)
                    sums[k] = prod if r0 == 0 else sums[k] + prod
            for k in range(CONV_K):
                dw_ref[SUB * k:SUB * (k + 1), sl] += sums[k]
            return carry

        lax.fori_loop(0, D_MODEL // 128, lanes, 0)
        lax.fori_loop(0, D_MODEL // 128, lanes_w, 0)
        dg = dgl[...]
        dv_ref[...] = (dg * sg).astype(BF16)
        dgt_ref[...] = (dg * v * sg * (1.0 - sg)).astype(BF16)

    return pl.pallas_call(
        body, name="conv_bwd", grid=(t // TM,),
        in_specs=[_rows(D_MODEL), _next(HALO, D_MODEL, t), _rows(D_MODEL, 0), _rows(D_MODEL, 1),
                  _prev(HALO, D_MODEL, 0), _prev(HALO, D_MODEL, 1), _full((HALO, D_MODEL))],
        out_specs=[_rows(D_MODEL), _rows(D_MODEL), _full((HALO * SUB, D_MODEL)), _full((1, D_MODEL))],
        out_shape=[jax.ShapeDtypeStruct((t, D_MODEL), BF16), jax.ShapeDtypeStruct((t, D_MODEL), BF16),
                   jax.ShapeDtypeStruct((HALO * SUB, D_MODEL), F32), jax.ShapeDtypeStruct((1, D_MODEL), F32)],
        scratch_shapes=[pltpu.VMEM((TM + HALO, D_MODEL), F32), pltpu.VMEM((TM + HALO, D_MODEL), F32),
                        pltpu.VMEM((TM, D_MODEL), F32),
                        pltpu.VMEM((SUB - 1, TM + HALO - SUB, D_MODEL), F32),
                        pltpu.VMEM((SUB - 1, TM + HALO - SUB, D_MODEL), F32)],
        compiler_params=_params())(dcv, dcv, q, q, q, q, cw)


def _in_bwd(dparts, w, x, g, dres, name):
    t = x.shape[0]
    widths = [p.shape[1] for p in dparts]
    npart = len(dparts)

    def body(*refs):
        d_refs = refs[:npart]
        w_ref, x_ref, g_ref, r_ref, dx_ref, dg_ref, dw_ref = refs[npart:]
        i = pl.program_id(0)
        _zero_at_first(i, dg_ref, dw_ref)
        xv = x_ref[...]
        r = lax.rsqrt(jnp.mean(xv * xv, axis=-1, keepdims=True) + RMS_EPS)
        n = xv * r
        h = (n * g_ref[...]).astype(BF16)
        off = 0
        dh = None
        for d_ref, wd in zip(d_refs, widths):
            d = d_ref[...]
            part = _dot_nt(d, w_ref[:, off:off + wd])
            dh = part if dh is None else dh + part
            dw_ref[:, off:off + wd] += _dot_tn(h, d)
            off += wd
        dg_ref[...] += jnp.sum(dh * n, axis=0, keepdims=True)
        dn = dh * g_ref[...]
        dx_ref[...] = r_ref[...] + r * (dn - n * jnp.mean(dn * n, axis=-1, keepdims=True))

    vec = _full((1, D_MODEL))
    once = pl.BlockSpec(w.shape, lambda i: (0, 0), pipeline_mode=pl.Buffered(1))
    return pl.pallas_call(
        body, name=name, grid=(t // TM_MM,),
        in_specs=[_mm_rows(wd) for wd in widths] + [once, _mm_rows(D_MODEL), vec, _mm_rows(D_MODEL)],
        out_specs=[_mm_rows(D_MODEL), vec, once],
        out_shape=[jax.ShapeDtypeStruct((t, D_MODEL), F32), jax.ShapeDtypeStruct((1, D_MODEL), F32),
                   jax.ShapeDtypeStruct(w.shape, F32)],
        compiler_params=_params())(*dparts, w, x, g, dres)


def _even_bwd_out(dx1, w, yg, yp, ys, proj):
    t = dx1.shape[0]

    def body(dx_ref, w_ref, yg_ref, yp_ref, ys_ref, z_ref, dy_ref, dz_ref, dw_ref):
        i = pl.program_id(0)
        _zero_at_first(i, dw_ref)
        dxb = dx_ref[...].astype(BF16)
        dyg = _dot_nt(dxb, w_ref[...])
        dw_ref[...] += _dot_tn(yg_ref[...], dxb)
        silu_z, dsilu_z = _dsilu(z_ref[...].astype(F32))
        dy_ref[...] = (dyg * silu_z).astype(BF16)
        dz_ref[:, :POOL_W] = (dyg[:, :POOL_W] * yp_ref[...] * dsilu_z[:, :POOL_W]).astype(BF16)
        dz_ref[:, POOL_W:] = (dyg[:, POOL_W:] * ys_ref[...] * dsilu_z[:, POOL_W:]).astype(BF16)

    mat = _full((D_MODEL, D_MODEL))
    return pl.pallas_call(
        body, name="even_bwd_out", grid=(t // TM_MM,),
        in_specs=[_mm_rows(D_MODEL), mat, _mm_rows(D_MODEL), _mm_rows(POOL_W), _mm_rows(SSM_W), _mm_rows(D_MODEL, 1)],
        out_specs=[_mm_rows(D_MODEL), _mm_rows(D_MODEL), mat],
        out_shape=[jax.ShapeDtypeStruct((t, D_MODEL), BF16), jax.ShapeDtypeStruct((t, D_MODEL), BF16),
                   jax.ShapeDtypeStruct((D_MODEL, D_MODEL), F32)],
        compiler_params=_params())(dx1, w, yg, yp, ys, proj)


def _pool_bwd(dycat, proj, wp, ps):
    t = proj.shape[0]

    def body(dy_ref, dyn_ref, u_ref, h_ref, wp_ref, ps_ref, du_ref, dwp_ref, dps_ref):
        i = pl.program_id(0)
        last = t // TM - 1
        _zero_at_first(i, dwp_ref, dps_ref)
        pos = (i * TM + 1 + lax.broadcasted_iota(jnp.int32, (TM, 1), 0)).astype(F32)
        pos_ext = (i * TM + 1 + lax.broadcasted_iota(jnp.int32, (TM + POOL_HALO, 1), 0)).astype(F32)
        for g in range(4):
            sl = slice(128 * g, 128 * (g + 1))
            w = float(2 << g)
            u = u_ref[:, sl].astype(F32)
            halo = jnp.where(i == 0, 0.0, h_ref[:, sl].astype(F32))
            s = _pool_sums(jnp.concatenate([halo, u], axis=0), g, True)[POOL_HALO:, :]
            pooled = (s / jnp.minimum(pos, w) - u).astype(BF16)
            dy = dy_ref[:, sl].astype(F32)
            dps_ref[:, sl] += jnp.sum(dy * _dot(pooled, wp_ref[g]), axis=0, keepdims=True)
            dy_ext = jnp.concatenate([dy, jnp.where(i == last, 0.0, dyn_ref[:, sl].astype(F32))], axis=0)
            dmix = (dy_ext * ps_ref[:, sl]).astype(BF16)
            dwp_ref[g] += _dot_tn(pooled, dmix[:TM, :])
            dpool = _dot_nt(dmix, wp_ref[g])
            lead = _pool_sums(dpool / jnp.minimum(pos_ext, w), g, False)
            du_ref[:, sl] = (lead[:TM, :] - dpool[:TM, :]).astype(BF16)

    return pl.pallas_call(
        body, name="pool_bwd", grid=(t // TM,),
        in_specs=[_rows(POOL_W, 0), _next(POOL_HALO, POOL_W, t, 0), _rows(POOL_W, 0), _prev(POOL_HALO, POOL_W, 0),
                  _full((4, 128, 128)), _full((1, POOL_W))],
        out_specs=[_rows(POOL_W), _full((4, 128, 128)), _full((1, POOL_W))],
        out_shape=[jax.ShapeDtypeStruct((t, POOL_W), BF16), jax.ShapeDtypeStruct((4, 128, 128), F32),
                   jax.ShapeDtypeStruct((1, POOL_W), F32)],
        compiler_params=_params())(dycat, dycat, proj, proj, wp, ps)


def _ssm_bwd(dycat, proj, car_in_re, car_in_im, pm, pmt, mb_re, mb_im, p8_re, p8_im, q_re, q_im, qr_re, qr_im,
             cm_re, cm_im, dskip, wglu):
    t = proj.shape[0]
    nblk = t // TM

    def body(dy_ref, u_ref, cin_re, cin_im, pm_ref, pmt_ref, mbre, mbim, p8re, p8im, qre, qim, qrre, qrim,
             cmre, cmim, d_ref, wg_ref,
             du_ref, dmbre, dmbim, dcmre, dcmim, dare, daim, dd_ref, dwg_ref,
             xs_re, xs_im, gs_re, gs_im, car_re, car_im, ent_re, ent_im, gcar_re, gcar_im, ysk, dysk):
        i = pl.program_id(0)
        _zero_at_first(i, dmbre, dmbim, dcmre, dcmim, dare, daim, dd_ref, dwg_ref, gcar_re, gcar_im)
        us = _dot(pm_ref[...], u_ref[...])
        usb = us.astype(BF16)
        for j in range(4):
            xs_re[:, LCH * j:LCH * (j + 1)] = _dot(usb[:, 128 * j:128 * (j + 1)], mbre[j])
            xs_im[:, LCH * j:LCH * (j + 1)] = _dot(usb[:, 128 * j:128 * (j + 1)], mbim[j])
        car_re[...] = cin_re[0]
        car_im[...] = cin_im[0]
        _scan_fwd_block(xs_re, xs_im, p8re, p8im, qre, qim, car_re, car_im, ent_re, ent_im)
        for j in range(4):
            sl = slice(LCH * j, LCH * (j + 1))
            ysk[:, 128 * j:128 * (j + 1)] = (_dot_nt(xs_re[:, sl].astype(BF16), cmre[j])
                                             - _dot_nt(xs_im[:, sl].astype(BF16), cmim[j]))
        yvb = (ysk[...] + d_ref[...] * us).astype(BF16)
        gv = _dot(yvb, wg_ref[...])
        sg = _sig(gv[:, SSM_W:])
        dyss = _dot(pm_ref[...], dy_ref[...])
        dval = (dyss * sg).astype(BF16)
        dgate = (dyss * gv[:, :SSM_W] * sg * (1.0 - sg)).astype(BF16)
        dy = _dot_nt(dval, wg_ref[:, :SSM_W]) + _dot_nt(dgate, wg_ref[:, SSM_W:])
        dwg_ref[:, :SSM_W] += _dot_tn(yvb, dval)
        dwg_ref[:, SSM_W:] += _dot_tn(yvb, dgate)
        dd_ref[...] += jnp.sum(dy * us, axis=0, keepdims=True)
        dysk[...] = dy
        for j in range(4):
            sl = slice(LCH * j, LCH * (j + 1))
            dyj = dy[:, 128 * j:128 * (j + 1)].astype(BF16)
            gs_re[:, sl] = _dot(dyj, cmre[j])
            gs_im[:, sl] = -_dot(dyj, cmim[j])
            dcmre[j] += _dot_tn(dyj, xs_re[:, sl].astype(BF16))
            dcmim[j] -= _dot_tn(dyj, xs_im[:, sl].astype(BF16))
        row = lax.broadcasted_iota(jnp.int32, (SUB, SCAN_L), 0)
        for j in range(STATES // SCAN_L):
            sl = slice(SCAN_L * j, SCAN_L * (j + 1))
            are, aim = p8re[0:SUB, sl], -p8im[0:SUB, sl]

            def totals(k, v, sl=sl, are=are, aim=aim):
                r0 = pl.multiple_of((SEG_LEN - 2 - k) * SUB, SUB)
                vre, vim = _cmul_add(are, aim, v[0], v[1], gs_re[pl.ds(r0, SUB), sl], gs_im[pl.ds(r0, SUB), sl])
                gs_re[pl.ds(r0, SUB), sl] = vre
                gs_im[pl.ds(r0, SUB), sl] = vim
                return vre, vim

            top = (SEG_LEN - 1) * SUB
            fre, fim = lax.fori_loop(0, SEG_LEN - 1, totals,
                                     (gs_re[top:top + SUB, sl], gs_im[top:top + SUB, sl]), unroll=2)
            fre, fim, nre, nim = _segment_chain(fre, fim, qrre[:, sl], -qrim[:, sl],
                                                gcar_re[:, sl], gcar_im[:, sl], row, True)
            gcar_re[:, sl] = jnp.broadcast_to(fre[0:1, :], (SUB, SCAN_L))
            gcar_im[:, sl] = jnp.broadcast_to(fim[0:1, :], (SUB, SCAN_L))

            def fix(i2, acc, sl=sl, nre=nre, nim=nim):
                r0 = pl.multiple_of(i2 * SUB, SUB)
                rb = pl.multiple_of((SEG_LEN - 1 - i2) * SUB, SUB)
                gre, gim = _cmul_add(p8re[pl.ds(rb, SUB), sl], -p8im[pl.ds(rb, SUB), sl], nre, nim,
                                     gs_re[pl.ds(r0, SUB), sl], gs_im[pl.ds(r0, SUB), sl])
                gs_re[pl.ds(r0, SUB), sl] = gre
                gs_im[pl.ds(r0, SUB), sl] = gim
                rp = pl.multiple_of((i2 - 1) * SUB, SUB)
                xre, xim = xs_re[pl.ds(rp, SUB), sl], xs_im[pl.ds(rp, SUB), sl]
                return acc[0] + gre * xre + gim * xim, acc[1] + gim * xre - gre * xim

            g0re, g0im = _cmul_add(p8re[top:top + SUB, sl], -p8im[top:top + SUB, sl], nre, nim,
                                   gs_re[0:SUB, sl], gs_im[0:SUB, sl])
            gs_re[0:SUB, sl] = g0re
            gs_im[0:SUB, sl] = g0im
            ere, eim = ent_re[:, sl], ent_im[:, sl]
            acc0 = (dare[:, sl] + g0re * ere + g0im * eim, daim[:, sl] + g0im * ere - g0re * eim)
            are_acc, aim_acc = lax.fori_loop(1, SEG_LEN, fix, acc0, unroll=2)
            dare[:, sl] = are_acc
            daim[:, sl] = aim_acc
        for j in range(4):
            sl = slice(LCH * j, LCH * (j + 1))
            c4 = slice(128 * j, 128 * (j + 1))
            gre = gs_re[:, sl].astype(BF16)
            gim = gs_im[:, sl].astype(BF16)
            dmbre[j] += _dot_tn(usb[:, c4], gre)
            dmbim[j] += _dot_tn(usb[:, c4], gim)
            dysk[:, c4] = _dot_nt(gre, mbre[j]) + _dot_nt(gim, mbim[j]) + dysk[:, c4] * d_ref[:, c4]
        du_ref[...] = _dot(pmt_ref[...], dysk[...].astype(BF16)).astype(BF16)

    blk = (4, 128, LCH)
    pw = _full((SUB, STATES))
    p8 = _full((TM, STATES))
    perm = _full((TM, TM))
    car = pl.BlockSpec((1, SUB, STATES), lambda i: (nblk - 1 - i, 0, 0))
    big = lambda: pltpu.VMEM((TM, STATES), F32)
    small = lambda: pltpu.VMEM((SUB, STATES), F32)
    return pl.pallas_call(
        body, name="ssm_bwd", grid=(nblk,),
        in_specs=[_rows(SSM_W, 1, rev=nblk), _rows(SSM_W, 1, rev=nblk), car, car, perm, perm, _full(blk), _full(blk),
                  p8, p8, pw, pw, pw, pw, _full(blk), _full(blk), _full((1, SSM_W)), _full((SSM_W, 2 * SSM_W))],
        out_specs=[_rows(SSM_W, 0, rev=nblk), _full(blk), _full(blk), _full(blk), _full(blk), pw, pw,
                   _full((1, SSM_W)), _full((SSM_W, 2 * SSM_W))],
        out_shape=[jax.ShapeDtypeStruct((t, SSM_W), BF16)] + [jax.ShapeDtypeStruct(blk, F32)] * 4
        + [jax.ShapeDtypeStruct((SUB, STATES), F32)] * 2
        + [jax.ShapeDtypeStruct((1, SSM_W), F32), jax.ShapeDtypeStruct((SSM_W, 2 * SSM_W), F32)],
        scratch_shapes=[big(), big(), big(), big(), small(), small(), small(), small(), small(), small(),
                        pltpu.VMEM((TM, SSM_W), F32), pltpu.VMEM((TM, SSM_W), F32)],
        compiler_params=_params())(dycat, proj, car_in_re, car_in_im, pm, pmt, mb_re, mb_im, p8_re, p8_im,
                                   q_re, q_im, qr_re, qr_im, cm_re, cm_im, dskip, wglu)


def _adamw(w, g, m, v, name):
    rows = w.shape[0]
    tr = 256 if rows % 256 == 0 else rows
    c1 = 1.0 / (1.0 - ADAM_B1 ** ADAM_STEP)
    c2 = 1.0 / (1.0 - ADAM_B2 ** ADAM_STEP)

    def body(w_ref, g_ref, m_ref, v_ref, d_ref, nm_ref, nv_ref):
        gv = g_ref[...]
        m = ADAM_B1 * m_ref[...] + (1.0 - ADAM_B1) * gv
        v = ADAM_B2 * v_ref[...] + (1.0 - ADAM_B2) * (gv * gv)
        nm_ref[...] = m
        nv_ref[...] = v
        d_ref[...] = -ADAM_LR * ((m * c1) / (jnp.sqrt(v * c2) + ADAM_EPS) + ADAM_WD * w_ref[...])

    spec = pl.BlockSpec((tr, D_MODEL), lambda i: (i, 0))
    shp = jax.ShapeDtypeStruct((rows, D_MODEL), F32)
    return pl.pallas_call(
        body, name=name, grid=(rows // tr,), in_specs=[spec] * 4, out_specs=[spec] * 3, out_shape=[shp] * 3,
        compiler_params=_params())(w, g, m, v)


def _core_index():
    return lax.axis_index("c").astype(jnp.int32).reshape(1)


def _pair_add(g, theirs, out_dtype):
    n, half, _ = theirs.shape
    nb = half // 256

    def body(c_ref, a_ref, b_ref, o_ref):
        o_ref[...] = (a_ref[...] + b_ref[...]).astype(out_dtype)

    spec = pl.BlockSpec((1, 256, D_MODEL), lambda i, j, c: (i, j, 0))
    grid_spec = pltpu.PrefetchScalarGridSpec(
        num_scalar_prefetch=1, grid=(n, nb),
        in_specs=[pl.BlockSpec((1, 256, D_MODEL), lambda i, j, c: (i, c[0] * nb + j, 0)), spec], out_specs=spec)
    return pl.pallas_call(
        body, name="pair_add", grid_spec=grid_spec, out_shape=jax.ShapeDtypeStruct(theirs.shape, out_dtype),
        compiler_params=_params(2))(_core_index(), g, theirs)


def _adamw_halves(w, g_mine, g_theirs, m, v):
    half = g_mine.shape[0]
    nb = half // 256
    c1 = 1.0 / (1.0 - ADAM_B1 ** ADAM_STEP)
    c2 = 1.0 / (1.0 - ADAM_B2 ** ADAM_STEP)

    def body(c_ref, w_ref, gm_ref, gt_ref, m_ref, v_ref, g_ref, d_ref, nm_ref, nv_ref):
        gv = jnp.where(pl.program_id(0) // nb == c_ref[0], gm_ref[...], gt_ref[...])
        m = ADAM_B1 * m_ref[...] + (1.0 - ADAM_B1) * gv
        v = ADAM_B2 * v_ref[...] + (1.0 - ADAM_B2) * (gv * gv)
        g_ref[...] = gv
        nm_ref[...] = m
        nv_ref[...] = v
        d_ref[...] = -ADAM_LR * ((m * c1) / (jnp.sqrt(v * c2) + ADAM_EPS) + ADAM_WD * w_ref[...])

    spec = pl.BlockSpec((256, D_MODEL), lambda i, c: (i, 0))
    part = pl.BlockSpec((256, D_MODEL), lambda i, c: (i % nb, 0))
    shp = jax.ShapeDtypeStruct((2 * half, D_MODEL), F32)
    grid_spec = pltpu.PrefetchScalarGridSpec(
        num_scalar_prefetch=1, grid=(2 * nb,), in_specs=[spec, part, part, spec, spec], out_specs=[spec] * 4)
    return pl.pallas_call(
        body, name="adamw_shard", grid_spec=grid_spec, out_shape=[shp] * 4,
        compiler_params=_params())(_core_index(), w, g_mine, g_theirs, m, v)


def _sum_lead(a, name):
    n, rows, _ = a.shape
    tr = 256 if rows % 256 == 0 else rows

    def body(a_ref, o_ref):
        acc = a_ref[0].astype(F32)
        for k in range(1, n):
            acc = acc + a_ref[k].astype(F32)
        o_ref[...] = acc

    return pl.pallas_call(
        body, name=name, grid=(rows // tr,),
        in_specs=[pl.BlockSpec((n, tr, D_MODEL), lambda i: (0, i, 0))],
        out_specs=pl.BlockSpec((tr, D_MODEL), lambda i: (i, 0)),
        out_shape=jax.ShapeDtypeStruct((rows, D_MODEL), F32), compiler_params=_params())(a)


ANY = pl.BlockSpec(memory_space=pl.ANY)


def _mesh_pos():
    return lax.axis_index("x"), lax.axis_index("y"), lax.axis_index("c")


def _gather_weights(wb):
    rows = wb.shape[0]
    half = rows // 2
    ch = half // COMM_CHUNKS
    ncopy = 3 * COMM_CHUNKS

    def body(w_ref, o_ref, bounce, send_sems, recv_sems, local_sems):
        x, y, c = _mesh_pos()
        me = 2 * x + y
        sibling = (x, y, 1 - c)
        chips = [(1 - x, y), (x, 1 - y), (1 - x, 1 - y)]
        ids = [2 * chip[0] + chip[1] for chip in chips]

        def piece(q, h, k):
            return o_ref.at[q, pl.ds(h * half + k * ch, ch), :]

        def copy(s, q, h, k, to, src=None):
            return pltpu.make_async_remote_copy(
                src_ref=piece(q, h, k) if src is None else src, dst_ref=piece(q, h, k),
                send_sem=send_sems.at[s], recv_sem=recv_sems.at[s], device_id=to, device_id_type=MESH)

        load = pltpu.make_async_copy(w_ref, bounce, local_sems.at[0])
        store = pltpu.make_async_copy(bounce, o_ref.at[me], local_sems.at[1])
        load.start()
        first = [copy(j * COMM_CHUNKS + k, me, c, k, (*chip, c), src=w_ref.at[pl.ds(c * half + k * ch, ch), :])
                 for j, chip in enumerate(chips) for k in range(COMM_CHUNKS)]
        for cp in first:
            cp.start()
        load.wait()
        store.start()
        passed = []
        for j in range(3):
            for k in range(COMM_CHUNKS):
                s = j * COMM_CHUNKS + k
                copy(s, ids[j], c, k, (x, y, c)).wait_recv()
                fwd = copy(ncopy + s, ids[j], c, k, sibling)
                fwd.start()
                passed.append(fwd)
        for j in range(3):
            for k in range(COMM_CHUNKS):
                copy(ncopy + j * COMM_CHUNKS + k, ids[j], 1 - c, k, (x, y, c)).wait_recv()
        for cp in first + passed:
            cp.wait_send()
        store.wait()

    return pl.pallas_call(
        body, name="gather_weights", in_specs=[ANY], out_specs=ANY,
        out_shape=jax.ShapeDtypeStruct((4, rows, D_MODEL), wb.dtype),
        scratch_shapes=[pltpu.VMEM(wb.shape, wb.dtype), pltpu.SemaphoreType.DMA((2 * ncopy,)),
                        pltpu.SemaphoreType.DMA((2 * ncopy,)), pltpu.SemaphoreType.DMA((2,))],
        compiler_params=pltpu.CompilerParams(vmem_limit_bytes=VMEM_LIMIT),
    )(wb)


def _gather_all(v):
    def body(v_ref, o_ref, bounce, send_sems, recv_sems, local_sems):
        x, y, c = _mesh_pos()
        sibling = (x, y, 1 - c)
        chips = [(1 - x, y), (x, 1 - y), (1 - x, 1 - y)]

        def blk(px, py, pc):
            return o_ref.at[4 * px + 2 * py + pc]

        def copy(k, block, to, src=None):
            return pltpu.make_async_remote_copy(
                src_ref=blk(*block) if src is None else src, dst_ref=blk(*block),
                send_sem=send_sems.at[k], recv_sem=recv_sems.at[k], device_id=to, device_id_type=MESH)

        load = pltpu.make_async_copy(v_ref, bounce, local_sems.at[0])
        store = pltpu.make_async_copy(bounce, blk(x, y, c), local_sems.at[1])
        load.start()
        first = [copy(0, (x, y, c), sibling, src=v_ref)]
        first += [copy(1 + j, (x, y, c), (*chip, c), src=v_ref) for j, chip in enumerate(chips)]
        for cp in first:
            cp.start()
        load.wait()
        store.start()
        passed = [copy(4 + j, (*chip, c), sibling) for j, chip in enumerate(chips)]
        for j, chip in enumerate(chips):
            copy(1 + j, (*chip, c), (x, y, c)).wait_recv()
            passed[j].start()
        copy(0, (x, y, 1 - c), (x, y, c)).wait_recv()
        for j, chip in enumerate(chips):
            copy(4 + j, (*chip, 1 - c), (x, y, c)).wait_recv()
        for cp in first + passed:
            cp.wait_send()
        store.wait()

    return pl.pallas_call(
        body, name="gather_all", in_specs=[ANY], out_specs=ANY,
        out_shape=jax.ShapeDtypeStruct((8,) + v.shape, v.dtype),
        scratch_shapes=[pltpu.VMEM(v.shape, v.dtype), pltpu.SemaphoreType.DMA((7,)), pltpu.SemaphoreType.DMA((7,)),
                        pltpu.SemaphoreType.DMA((2,))],
    )(v)


def _pair_split(g):
    n, rows, _ = g.shape
    half = rows // 2
    ch = half // COMM_CHUNKS

    def body(g_ref, theirs_ref, send_sems, recv_sems):
        x, y, c = _mesh_pos()
        gives = [pltpu.make_async_remote_copy(
            src_ref=g_ref.at[q, pl.ds((1 - c) * half + k * ch, ch), :],
            dst_ref=theirs_ref.at[q, pl.ds(k * ch, ch), :],
            send_sem=send_sems.at[q * COMM_CHUNKS + k], recv_sem=recv_sems.at[q * COMM_CHUNKS + k],
            device_id=(x, y, 1 - c), device_id_type=MESH) for q in range(n) for k in range(COMM_CHUNKS)]
        for cp in gives:
            cp.start()
        for cp in gives:
            cp.wait()

    return pl.pallas_call(
        body, name="pair_split", in_specs=[ANY], out_specs=ANY,
        out_shape=jax.ShapeDtypeStruct((n, half, D_MODEL), g.dtype),
        scratch_shapes=[pltpu.SemaphoreType.DMA((n * COMM_CHUNKS,)), pltpu.SemaphoreType.DMA((n * COMM_CHUNKS,))],
    )(g)


def _chip_scatter(p):
    def body(p_ref, o_ref, bounce, send_sems, recv_sems, local_sems):
        x, y, c = _mesh_pos()
        me = 2 * x + y
        chips = [(1 - x, y), (x, 1 - y), (1 - x, 1 - y)]
        load = pltpu.make_async_copy(p_ref.at[me], bounce, local_sems.at[0])
        keep = pltpu.make_async_copy(bounce, o_ref.at[me], local_sems.at[1])
        load.start()
        sends = [pltpu.make_async_remote_copy(
            src_ref=p_ref.at[2 * chip[0] + chip[1]], dst_ref=o_ref.at[me],
            send_sem=send_sems.at[j], recv_sem=recv_sems.at[j], device_id=(*chip, c), device_id_type=MESH)
            for j, chip in enumerate(chips)]
        for cp in sends:
            cp.start()
        load.wait()
        keep.start()
        for j, chip in enumerate(chips):
            q = 2 * chip[0] + chip[1]
            pltpu.make_async_remote_copy(
                src_ref=p_ref.at[q], dst_ref=o_ref.at[q], send_sem=send_sems.at[j], recv_sem=recv_sems.at[j],
                device_id=(*chip, c), device_id_type=MESH).wait_recv()
        for cp in sends:
            cp.wait_send()
        keep.wait()

    return pl.pallas_call(
        body, name="chip_scatter", in_specs=[ANY], out_specs=ANY, out_shape=jax.ShapeDtypeStruct(p.shape, p.dtype),
        scratch_shapes=[pltpu.VMEM(p.shape[1:], p.dtype), pltpu.SemaphoreType.DMA((3,)),
                        pltpu.SemaphoreType.DMA((3,)), pltpu.SemaphoreType.DMA((2,))],
    )(p)


def _pair_join(r):
    rows = r.shape[0]
    ch = rows // COMM_CHUNKS

    def body(r_ref, o_ref, send_sems, recv_sems):
        x, y, c = _mesh_pos()
        gives = [pltpu.make_async_remote_copy(
            src_ref=r_ref.at[pl.ds(k * ch, ch), :], dst_ref=o_ref.at[pl.ds(k * ch, ch), :],
            send_sem=send_sems.at[k], recv_sem=recv_sems.at[k], device_id=(x, y, 1 - c), device_id_type=MESH)
            for k in range(COMM_CHUNKS)]
        for cp in gives:
            cp.start()
        for cp in gives:
            cp.wait()

    return pl.pallas_call(
        body, name="pair_join", in_specs=[ANY], out_specs=ANY, out_shape=jax.ShapeDtypeStruct(r.shape, r.dtype),
        scratch_shapes=[pltpu.SemaphoreType.DMA((COMM_CHUNKS,)), pltpu.SemaphoreType.DMA((COMM_CHUNKS,))],
    )(r)


SHARD_BIG = (("even_w_in", (1024, 512)), ("ssm_w_glu", (512, 256)), ("even_w_out", (256, 1024)),
             ("odd_w_in", (1024, 768)), ("odd_w_out", (256, 1024)))
SHARD_SMALL = (("odd_norm", 1), ("conv_w", CONV_K), ("conv_b", 1), ("conv_ln_g", 1), ("conv_ln_b", 1))
REP_NAMES = (("even_norm", (1024,)), ("pool_w", (4, 128, 128)), ("pool_scale", (512,)), ("ssm_log_dt", (32,)),
             ("ssm_a_re", (32, 64)), ("ssm_a_im", (32, 64)), ("ssm_b_re", (32, 64, 16)), ("ssm_b_im", (32, 64, 16)),
             ("ssm_c_re", (32, 16, 64)), ("ssm_c_im", (32, 16, 64)), ("ssm_d", (512,)), ("final_norm", (1024,)))


def _pack_shard(d):
    big = [d[n].reshape(-1, D_MODEL) for n, _ in SHARD_BIG]
    small = jnp.concatenate([d[n].reshape(r, 256) for n, r in SHARD_SMALL], axis=0).reshape(-1)
    small = jnp.pad(small, (0, ROWS_SMALL * D_MODEL - small.shape[0])).reshape(ROWS_SMALL, D_MODEL)
    pad = jnp.zeros((ROWS_PACK - ROWS_BIG - ROWS_SMALL, D_MODEL), F32)
    return jnp.concatenate(big + [small, pad], axis=0)


def _unpack_shard(buf):
    out = {}
    off = 0
    for n, shp in SHARD_BIG:
        rows = shp[0] * shp[1] // D_MODEL
        out[n] = buf[off:off + rows].reshape(shp)
        off += rows
    small = buf[ROWS_BIG:ROWS_BIG + ROWS_SMALL].reshape(-1)[:35 * 256].reshape(35, 256)
    off = 0
    for n, r in SHARD_SMALL:
        out[n] = small[off:off + r].reshape((r, 256) if r > 1 else (256,))
        off += r
    return out


def _pack_rep(d):
    flat = jnp.concatenate([d[n].reshape(-1) for n, _ in REP_NAMES])
    return jnp.pad(flat, (0, REP_ROWS * D_MODEL - flat.shape[0])).reshape(REP_ROWS, D_MODEL)


def _unpack_rep(buf):
    flat = buf.reshape(-1)
    out = {}
    off = 0
    for n, shp in REP_NAMES:
        size = 1
        for s in shp:
            size *= s
        out[n] = flat[off:off + size].reshape(shp)
        off += size
    return out


def _cols_full(g4, rows, cols):
    return g4.reshape(4, rows, cols).transpose(1, 0, 2).reshape(rows, 4 * cols)


def _cols_split(full, cols):
    rows = full.shape[0]
    return full.reshape(rows, 4, cols).transpose(1, 0, 2).reshape(4, -1, D_MODEL)


def _block_diag(a):
    a = a.reshape(4, 8, GROUP_DIM, N_STATE)
    eye = jnp.eye(8, dtype=a.dtype)
    return (a[:, :, :, None, :] * eye[None, :, None, :, None]).reshape(4, 128, LCH)


def _block_diag_take(m):
    m = m.reshape(4, 8, GROUP_DIM, 8, N_STATE)
    eye = jnp.eye(8, dtype=m.dtype)
    return jnp.sum(m * eye[None, :, None, :, None], axis=3).reshape(N_GROUPS, GROUP_DIM, N_STATE)


def _ssm_discretise(log_dt, a_re, a_im, b_re, b_im):
    dt = jnp.exp(log_dt)[:, None]
    mag = jnp.exp(a_re * dt)
    ang = a_im * dt
    abar_re = mag * jnp.cos(ang)
    abar_im = mag * jnp.sin(ang)
    den = a_re * a_re + a_im * a_im
    nr = abar_re - 1.0
    ni = abar_im
    k_re = (nr * a_re + ni * a_im) / den
    k_im = (ni * a_re - nr * a_im) / den
    bb_re = k_re[..., None] * b_re - k_im[..., None] * b_im
    bb_im = k_re[..., None] * b_im + k_im[..., None] * b_re
    return abar_re, abar_im, bb_re, bb_im


def _scan_tables(log_dt, a_re, a_im):
    dt = jnp.exp(log_dt)[:, None]
    lam_re = (a_re * dt).reshape(1, STATES)
    lam_im = (a_im * dt).reshape(1, STATES)

    def powers(k):
        mag = jnp.exp(k * lam_re)
        return mag * jnp.cos(k * lam_im), mag * jnp.sin(k * lam_im)

    p_re, p_im = powers((1 + jnp.arange(TM) // SUB).astype(F32)[:, None])
    q_re, q_im = powers((SEG_LEN * (1 + jnp.arange(SUB))).astype(F32)[:, None])
    return p_re, p_im, q_re, q_im


def _local_step(x, tgt, w):
    row = lambda a: a.reshape(1, -1)
    e_w_in = w["even_w_in"].astype(BF16)
    e_w_out = w["even_w_out"].astype(BF16)
    o_w_in = w["odd_w_in"].astype(BF16)
    o_w_out = w["odd_w_out"].astype(BF16)
    wglu = w["ssm_w_glu"].astype(BF16)
    wp = w["pool_w"].astype(BF16)
    ssm_in = (w["ssm_log_dt"], w["ssm_a_re"], w["ssm_a_im"], w["ssm_b_re"], w["ssm_b_im"])
    (abar_re, abar_im, bb_re, bb_im), ssm_vjp = jax.vjp(_ssm_discretise, *ssm_in)
    mb_re = _block_diag(bb_re.transpose(0, 2, 1)).astype(BF16)
    mb_im = _block_diag(bb_im.transpose(0, 2, 1)).astype(BF16)
    cm_re = _block_diag(w["ssm_c_re"]).astype(BF16)
    cm_im = _block_diag(w["ssm_c_im"]).astype(BF16)
    p8_re, p8_im, q_re, q_im = _scan_tables(w["ssm_log_dt"], w["ssm_a_re"], w["ssm_a_im"])
    qr_re, qr_im = q_re[::-1], q_im[::-1]
    pm = _perm_matrix()
    pmt = pm.T
    cw = jnp.pad(w["conv_w"], ((0, HALO - CONV_K), (0, 0)))
    g0, g1, gf = row(w["even_norm"]), row(w["odd_norm"]), row(w["final_norm"])
    ps, dskip = row(w["pool_scale"]), row(w["ssm_d"])
    cb, lg, lb = row(w["conv_b"]), row(w["conv_ln_g"]), row(w["conv_ln_b"])

    proj = _norm_in(x, g0, e_w_in, "even_in")
    yp = _pool_fwd(proj, wp, ps)
    ys, car_re, car_im = _ssm_fwd(proj, pm, pmt, mb_re, mb_im, p8_re, p8_im, q_re, q_im, cm_re, cm_im, dskip, wglu)
    x1, yg = _even_out(yp, ys, proj, x, e_w_out)
    q = _norm_in(x1, g1, o_w_in, "odd_in")
    y2, cv = _conv_fwd(q, cw, cb, lg, lb)
    dx2, loss_lanes, d_gf = _odd_out_loss(y2, x1, o_w_out, gf, tgt)

    dcv, dz2, d_o_w_out, d_lg, d_lb = _odd_bwd_out(dx2, o_w_out, y2, cv, q, lg, lb)
    dval, dgate, d_cw, d_cb = _conv_bwd(dcv, q, cw)
    dx1, d_g1, d_o_w_in = _in_bwd([dval, dgate, dz2], o_w_in, x1, g1, dx2, "odd_in_bwd")
    dycat, dz, d_e_w_out = _even_bwd_out(dx1, e_w_out, yg, yp, ys, proj)
    dup, d_wp, d_ps = _pool_bwd(dycat, proj, wp, ps)
    (dus, d_mb_re, d_mb_im, d_cm_re, d_cm_im, da_re, da_im, d_dskip, d_wglu) = _ssm_bwd(
        dycat, proj, car_re, car_im, pm, pmt, mb_re, mb_im, p8_re, p8_im, q_re, q_im, qr_re, qr_im,
        cm_re, cm_im, dskip, wglu)
    dx, d_g0, d_e_w_in = _in_bwd([dup, dus, dz], e_w_in, x, g0, dx1, "even_in_bwd")

    d_abar_re = jnp.sum(da_re, axis=0).reshape(N_GROUPS, N_STATE)
    d_abar_im = jnp.sum(da_im, axis=0).reshape(N_GROUPS, N_STATE)
    d_bb_re = _block_diag_take(d_mb_re).transpose(0, 2, 1)
    d_bb_im = _block_diag_take(d_mb_im).transpose(0, 2, 1)
    d_log_dt, d_a_re, d_a_im, d_b_re, d_b_im = ssm_vjp((d_abar_re, d_abar_im, d_bb_re, d_bb_im))

    grads = {
        "even_norm": d_g0.reshape(-1), "even_w_in": d_e_w_in, "pool_w": d_wp, "pool_scale": d_ps.reshape(-1),
        "ssm_log_dt": d_log_dt, "ssm_a_re": d_a_re, "ssm_a_im": d_a_im, "ssm_b_re": d_b_re, "ssm_b_im": d_b_im,
        "ssm_c_re": _block_diag_take(d_cm_re), "ssm_c_im": _block_diag_take(d_cm_im),
        "ssm_d": d_dskip.reshape(-1), "ssm_w_glu": d_wglu, "even_w_out": d_e_w_out, "odd_norm": d_g1.reshape(-1),
        "odd_w_in": d_o_w_in, "conv_w": d_cw.reshape(HALO, SUB, D_MODEL).sum(axis=1)[:CONV_K], "conv_b": d_cb.reshape(-1), "conv_ln_g": d_lg.reshape(-1),
        "conv_ln_b": d_lb.reshape(-1), "odd_w_out": d_o_w_out, "final_norm": d_gf.reshape(-1),
    }
    return jnp.sum(loss_lanes), dx, grads


WEIGHT_NAMES = ("even_norm", "even_w_in", "pool_w", "pool_scale", "ssm_log_dt", "ssm_a_re", "ssm_a_im",
                "ssm_b_re", "ssm_b_im", "ssm_c_re", "ssm_c_im", "ssm_d", "ssm_w_glu", "even_w_out", "odd_norm",
                "odd_w_in", "conv_w", "conv_b", "conv_ln_g", "conv_ln_b", "odd_w_out", "final_norm")
SHARDED = tuple(n for n, _ in SHARD_BIG) + tuple(n for n, _ in SHARD_SMALL)


def _full_weights(shard, rep):
    pack = _pack_shard(shard)
    big = pack[:ROWS_BIG].astype(BF16)
    small = lax.bitcast_convert_type(pack[ROWS_BIG:ROWS_BIG + ROWS_SMALL], BF16).reshape(2 * ROWS_SMALL, D_MODEL)
    pad = jnp.zeros((ROWS_PACK - ROWS_BIG - 2 * ROWS_SMALL, D_MODEL), BF16)
    got = _gather_weights(jnp.concatenate([big, small, pad], axis=0))
    gb = got[:, :ROWS_BIG]
    gs = lax.bitcast_convert_type(got[:, ROWS_BIG:ROWS_BIG + 2 * ROWS_SMALL].reshape(4, ROWS_SMALL, D_MODEL, 2), F32)
    gs = gs.reshape(4, -1)[:, :35 * 256].reshape(4, 35, 256)
    w = dict(rep)
    w["even_w_in"] = _cols_full(gb[:, 0:512], 1024, 512)
    w["ssm_w_glu"] = _cols_full(gb[:, 512:640], 512, 256)
    w["even_w_out"] = gb[:, 640:896].reshape(1024, 1024)
    w["odd_w_in"] = _cols_full(gb[:, 896:1664], 1024, 768)
    w["odd_w_out"] = gb[:, 1664:1920].reshape(1024, 1024)
    w["odd_norm"] = gs[:, 0].reshape(-1)
    w["conv_w"] = gs[:, 1:32].transpose(1, 0, 2).reshape(CONV_K, 1024)
    w["conv_b"] = gs[:, 32].reshape(-1)
    w["conv_ln_g"] = gs[:, 33].reshape(-1)
    w["conv_ln_b"] = gs[:, 34].reshape(-1)
    return w


def _pack_grads(g):
    small = jnp.concatenate([g["odd_norm"].reshape(4, 1, 256), g["conv_w"].reshape(CONV_K, 4, 256).transpose(1, 0, 2),
                             g["conv_b"].reshape(4, 1, 256), g["conv_ln_g"].reshape(4, 1, 256),
                             g["conv_ln_b"].reshape(4, 1, 256)], axis=1).reshape(4, -1)
    small = jnp.pad(small, ((0, 0), (0, ROWS_SMALL * D_MODEL - small.shape[1]))).reshape(4, ROWS_SMALL, D_MODEL)
    parts = [_cols_split(g["even_w_in"], 512), _cols_split(g["ssm_w_glu"], 256), g["even_w_out"].reshape(4, 256, 1024),
             _cols_split(g["odd_w_in"], 768), g["odd_w_out"].reshape(4, 256, 1024), small,
             jnp.zeros((4, ROWS_PACK - ROWS_BIG - ROWS_SMALL, D_MODEL), F32)]
    return jnp.concatenate(parts, axis=1)


def kernel(x, even_norm, even_w_in, pool_w, pool_scale, ssm_log_dt, ssm_a_re, ssm_a_im, ssm_b_re, ssm_b_im, ssm_c_re, ssm_c_im, ssm_d, ssm_w_glu, even_w_out, odd_norm, odd_w_in, conv_w, conv_b, conv_ln_g, conv_ln_b, odd_w_out, final_norm, loss_target, m_even_norm, m_even_w_in, m_pool_w, m_pool_scale, m_ssm_log_dt, m_ssm_a_re, m_ssm_a_im, m_ssm_b_re, m_ssm_b_im, m_ssm_c_re, m_ssm_c_im, m_ssm_d, m_ssm_w_glu, m_even_w_out, m_odd_norm, m_odd_w_in, m_conv_w, m_conv_b, m_conv_ln_g, m_conv_ln_b, m_odd_w_out, m_final_norm, v_even_norm, v_even_w_in, v_pool_w, v_pool_scale, v_ssm_log_dt, v_ssm_a_re, v_ssm_a_im, v_ssm_b_re, v_ssm_b_im, v_ssm_c_re, v_ssm_c_im, v_ssm_d, v_ssm_w_glu, v_even_w_out, v_odd_norm, v_odd_w_in, v_conv_w, v_conv_b, v_conv_ln_g, v_conv_ln_b, v_odd_w_out, v_final_norm):
    ws = dict(zip(WEIGHT_NAMES, (even_norm, even_w_in, pool_w, pool_scale, ssm_log_dt, ssm_a_re, ssm_a_im, ssm_b_re,
                                 ssm_b_im, ssm_c_re, ssm_c_im, ssm_d, ssm_w_glu, even_w_out, odd_norm, odd_w_in,
                                 conv_w, conv_b, conv_ln_g, conv_ln_b, odd_w_out, final_norm)))
    ms = dict(zip(WEIGHT_NAMES, (m_even_norm, m_even_w_in, m_pool_w, m_pool_scale, m_ssm_log_dt, m_ssm_a_re,
                                 m_ssm_a_im, m_ssm_b_re, m_ssm_b_im, m_ssm_c_re, m_ssm_c_im, m_ssm_d, m_ssm_w_glu,
                                 m_even_w_out, m_odd_norm, m_odd_w_in, m_conv_w, m_conv_b, m_conv_ln_g, m_conv_ln_b,
                                 m_odd_w_out, m_final_norm)))
    vs = dict(zip(WEIGHT_NAMES, (v_even_norm, v_even_w_in, v_pool_w, v_pool_scale, v_ssm_log_dt, v_ssm_a_re,
                                 v_ssm_a_im, v_ssm_b_re, v_ssm_b_im, v_ssm_c_re, v_ssm_c_im, v_ssm_d, v_ssm_w_glu,
                                 v_even_w_out, v_odd_norm, v_odd_w_in, v_conv_w, v_conv_b, v_conv_ln_g, v_conv_ln_b,
                                 v_odd_w_out, v_final_norm)))
    lead = {n: a.shape for n, a in ws.items()}
    drop = lambda d: {n: (a[0] if n != "final_norm" else a) for n, a in d.items()}
    ws, ms, vs = drop(ws), drop(ms), drop(vs)

    shard = {n: ws[n] for n in SHARDED}
    rep = {n: ws[n] for n, _ in REP_NAMES}
    w_full = _full_weights(shard, rep)
    loss_part, grad_x, grads = _local_step(x[0], loss_target[0], w_full)
    loss = lax.psum(loss_part, ("x", "y", "c"))

    g_pack = _pack_grads(grads)
    got = _chip_scatter(_pair_add(g_pack, _pair_split(g_pack), BF16))
    g_mine = _sum_lead(got, "chip_sum")
    g_shard, d_shard, m_shard, v_shard = _adamw_halves(
        _pack_shard(shard), g_mine, _pair_join(g_mine), _pack_shard({n: ms[n] for n in SHARDED}),
        _pack_shard({n: vs[n] for n in SHARDED}))
    g_rep = _sum_lead(_gather_all(_pack_rep({n: grads[n] for n, _ in REP_NAMES})), "rep_sum")
    d_rep, m_rep, v_rep = _adamw(_pack_rep(rep), g_rep, _pack_rep({n: ms[n] for n, _ in REP_NAMES}),
                                 _pack_rep({n: vs[n] for n, _ in REP_NAMES}), "adamw_rep")

    def unpack(shard_buf, rep_buf):
        d = {**_unpack_shard(shard_buf), **_unpack_rep(rep_buf)}
        return [d[n].reshape(lead[n]) for n in WEIGHT_NAMES]

    return (loss, grad_x[None], *unpack(g_shard, g_rep), *unpack(d_shard, d_rep),
            *unpack(m_shard, m_rep), *unpack(v_shard, v_rep))
```

```python
import functools

import jax
import jax.numpy as jnp
from jax import lax
from jax.experimental import pallas as pl
from jax.experimental.pallas import tpu as pltpu

F32 = jnp.float32
BF16 = jnp.bfloat16
MESH = pl.DeviceIdType.MESH

D_MODEL = 1024
RMS_EPS = 1e-6
LN_EPS = 1e-5
N_GROUPS = 32
GROUP_DIM = 16
N_STATE = 64
STATES = N_GROUPS * N_STATE
SSM_W = 512
POOL_W = 512
CONV_K = 31
HALO = 32
POOL_HALO = 16

ADAM_LR = 0.001
ADAM_B1 = 0.9
ADAM_B2 = 0.999
ADAM_EPS = 1e-08
ADAM_WD = 0.01
ADAM_STEP = 10

TM = 256
TM_MM = 512
SUB = 8
LCH = 512
SCAN_L = 1024
VMEM_LIMIT = 56 * 1024 * 1024

ROWS_BIG = 1920
ROWS_SMALL = 16
ROWS_PACK = 2048
REP_ROWS = 200
COMM_CHUNKS = 4


def _params(n_axes=1):
    return pltpu.CompilerParams(dimension_semantics=("arbitrary",) * n_axes, vmem_limit_bytes=VMEM_LIMIT)


def _rows(w, cb=0, rev=None, tm=TM):
    if rev is None:
        return pl.BlockSpec((tm, w), lambda i: (i, cb))
    return pl.BlockSpec((tm, w), lambda i: (rev - 1 - i, cb))


def _mm_rows(w, cb=0):
    return _rows(w, cb, tm=TM_MM)


def _full(shape):
    n = len(shape)
    return pl.BlockSpec(shape, lambda i: (0,) * n)


def _prev(hr, w, cb=0, tm=TM):
    r = tm // hr
    return pl.BlockSpec((hr, w), lambda i: (jnp.maximum(i * r - 1, 0), cb))


def _next(hr, w, nrows, cb=0, tm=TM):
    r = tm // hr
    last = nrows // hr - 1
    return pl.BlockSpec((hr, w), lambda i: (jnp.minimum((i + 1) * r, last), cb))


def _dot(a, b):
    return jnp.dot(a, b, preferred_element_type=F32)


def _dot_nt(a, b):
    return lax.dot_general(a, b, (((1,), (1,)), ((), ())), preferred_element_type=F32)


def _dot_tn(a, b):
    return lax.dot_general(a, b, (((0,), (0,)), ((), ())), preferred_element_type=F32)


def _sig(x):
    return 1.0 / (1.0 + jnp.exp(-x))


def _zero_at_first(i, *refs):
    @pl.when(i == 0)
    def _():
        for r in refs:
            r[...] = jnp.zeros_like(r)


def _norm_in(x, g, w, name):
    t, ns = x.shape[0], w.shape[2]
    n = 4 * ns

    def body(x_ref, g_ref, w_ref, o_ref):
        xv = x_ref[...]
        r = lax.rsqrt(jnp.mean(xv * xv, axis=-1, keepdims=True) + RMS_EPS)
        h = (xv * r * g_ref[...]).astype(BF16)
        for s in range(4):
            o_ref[:, s * ns:(s + 1) * ns] = _dot(h, w_ref[s]).astype(BF16)

    return pl.pallas_call(
        body, name=name, grid=(t // TM_MM,),
        in_specs=[_mm_rows(D_MODEL), _full((1, D_MODEL)), _full(w.shape)],
        out_specs=_mm_rows(n), out_shape=jax.ShapeDtypeStruct((t, n), BF16),
        compiler_params=_params())(x, g, w)


def _pool_sums(ext, g, forward):
    n = ext.shape[0]
    s = ext
    for step in range(g + 1):
        k = 1 << step
        s = s + pltpu.roll(s, k if forward else n - k, 0)
    return s


def _pool_fwd(proj, wp, ps):
    t = proj.shape[0]

    def body(u_ref, h_ref, wp_ref, ps_ref, y_ref):
        i = pl.program_id(0)
        pos = (i * TM + 1 + lax.broadcasted_iota(jnp.int32, (TM, 1), 0)).astype(F32)
        for g in range(4):
            sl = slice(128 * g, 128 * (g + 1))
            u = u_ref[:, sl].astype(F32)
            halo = jnp.where(i == 0, 0.0, h_ref[:, sl].astype(F32))
            s = _pool_sums(jnp.concatenate([halo, u], axis=0), g, True)[POOL_HALO:, :]
            pooled = s / jnp.minimum(pos, float(2 << g)) - u
            y_ref[:, sl] = _dot(pooled.astype(BF16), wp_ref[g]) * ps_ref[:, sl]

    return pl.pallas_call(
        body, name="pool_fwd", grid=(t // TM,),
        in_specs=[_rows(POOL_W, 0), _prev(POOL_HALO, POOL_W, 0), _full((4, 128, 128)), _full((1, POOL_W))],
        out_specs=_rows(POOL_W), out_shape=jax.ShapeDtypeStruct((t, POOL_W), F32),
        compiler_params=_params())(proj, proj, wp, ps)


SEG_LEN = TM // SUB


def _perm_matrix():
    p = jnp.arange(TM)
    src = (p % SUB) * SEG_LEN + p // SUB
    return (src[:, None] == jnp.arange(TM)[None, :]).astype(BF16)


def _cmul_add(are, aim, vre, vim, bre, bim):
    return are * vre - aim * vim + bre, are * vim + aim * vre + bim


def _segment_chain(ere, eim, qre, qim, cin_re, cin_im, row, up):
    for sh in (1, 2, 4):
        mre, mim = (qre[SUB - sh:SUB - sh + 1, :], qim[SUB - sh:SUB - sh + 1, :]) if up else \
                   (qre[sh - 1:sh, :], qim[sh - 1:sh, :])
        keep = (row < SUB - sh) if up else (row >= sh)
        sre = jnp.where(keep, pltpu.roll(ere, SUB - sh if up else sh, 0), 0.0)
        sim = jnp.where(keep, pltpu.roll(eim, SUB - sh if up else sh, 0), 0.0)
        ere, eim = _cmul_add(mre, mim, sre, sim, ere, eim)
    ere, eim = _cmul_add(qre, qim, cin_re, cin_im, ere, eim)
    keep = (row < SUB - 1) if up else (row >= 1)
    ent_re = jnp.where(keep, pltpu.roll(ere, SUB - 1 if up else 1, 0), cin_re)
    ent_im = jnp.where(keep, pltpu.roll(eim, SUB - 1 if up else 1, 0), cin_im)
    return ere, eim, ent_re, ent_im


def _scan_fwd_block(xs_re, xs_im, p8_re, p8_im, q_re, q_im, car_re, car_im, ent_re_ref, ent_im_ref):
    row = lax.broadcasted_iota(jnp.int32, (SUB, SCAN_L), 0)
    for j in range(STATES // SCAN_L):
        sl = slice(SCAN_L * j, SCAN_L * (j + 1))
        are, aim = p8_re[0:SUB, sl], p8_im[0:SUB, sl]

        def totals(i, v, sl=sl, are=are, aim=aim):
            r0 = pl.multiple_of(i * SUB, SUB)
            vre, vim = _cmul_add(are, aim, v[0], v[1], xs_re[pl.ds(r0, SUB), sl], xs_im[pl.ds(r0, SUB), sl])
            xs_re[pl.ds(r0, SUB), sl] = vre
            xs_im[pl.ds(r0, SUB), sl] = vim
            return vre, vim

        ere, eim = lax.fori_loop(1, SEG_LEN, totals, (xs_re[0:SUB, sl], xs_im[0:SUB, sl]), unroll=2)
        ere, eim, cre, cim = _segment_chain(ere, eim, q_re[:, sl], q_im[:, sl],
                                            car_re[:, sl], car_im[:, sl], row, False)
        car_re[:, sl] = jnp.broadcast_to(ere[SUB - 1:SUB, :], (SUB, SCAN_L))
        car_im[:, sl] = jnp.broadcast_to(eim[SUB - 1:SUB, :], (SUB, SCAN_L))
        if ent_re_ref is not None:
            ent_re_ref[:, sl] = cre
            ent_im_ref[:, sl] = cim

        def fix(i, c, sl=sl, cre=cre, cim=cim):
            r0 = pl.multiple_of(i * SUB, SUB)
            vre, vim = _cmul_add(p8_re[pl.ds(r0, SUB), sl], p8_im[pl.ds(r0, SUB), sl], cre, cim,
                                 xs_re[pl.ds(r0, SUB), sl], xs_im[pl.ds(r0, SUB), sl])
            xs_re[pl.ds(r0, SUB), sl] = vre
            xs_im[pl.ds(r0, SUB), sl] = vim
            return c

        lax.fori_loop(0, SEG_LEN, fix, 0, unroll=2)


def _unpermute(pmt_ref, v):
    hi = v.astype(BF16)
    lo = (v - hi.astype(F32)).astype(BF16)
    return _dot(pmt_ref[...], hi) + _dot(pmt_ref[...], lo)


def _ssm_fwd(proj, pm, pmt, mb_re, mb_im, p8_re, p8_im, q_re, q_im, cm_re, cm_im, dskip, wglu):
    t = proj.shape[0]
    nblk = t // TM

    def body(u_ref, pm_ref, pmt_ref, mbre, mbim, p8re, p8im, qre, qim, cmre, cmim, d_ref, wg_ref,
             y_ref, cre_ref, cim_ref, xs_re, xs_im, car_re, car_im, ysk):
        i = pl.program_id(0)
        _zero_at_first(i, car_re, car_im)
        cre_ref[0] = car_re[...]
        cim_ref[0] = car_im[...]
        us = _dot(pm_ref[...], u_ref[...])
        usb = us.astype(BF16)
        for j in range(4):
            xs_re[:, LCH * j:LCH * (j + 1)] = _dot(usb[:, 128 * j:128 * (j + 1)], mbre[j])
            xs_im[:, LCH * j:LCH * (j + 1)] = _dot(usb[:, 128 * j:128 * (j + 1)], mbim[j])
        _scan_fwd_block(xs_re, xs_im, p8re, p8im, qre, qim, car_re, car_im, None, None)
        for j in range(4):
            sl = slice(LCH * j, LCH * (j + 1))
            ysk[:, 128 * j:128 * (j + 1)] = (_dot_nt(xs_re[:, sl].astype(BF16), cmre[j])
                                             - _dot_nt(xs_im[:, sl].astype(BF16), cmim[j]))
        yv = ysk[...] + d_ref[...] * us
        gv = _dot(yv.astype(BF16), wg_ref[...])
        y_ref[...] = _unpermute(pmt_ref, gv[:, :SSM_W] * _sig(gv[:, SSM_W:]))

    blk = (4, 128, LCH)
    return pl.pallas_call(
        body, name="ssm_fwd", grid=(nblk,),
        in_specs=[_rows(SSM_W, 1), _full((TM, TM)), _full((TM, TM)), _full(blk), _full(blk),
                  _full((TM, STATES)), _full((TM, STATES)), _full((SUB, STATES)), _full((SUB, STATES)),
                  _full(blk), _full(blk), _full((1, SSM_W)), _full((SSM_W, 2 * SSM_W))],
        out_specs=[_rows(SSM_W), pl.BlockSpec((1, SUB, STATES), lambda i: (i, 0, 0)),
                   pl.BlockSpec((1, SUB, STATES), lambda i: (i, 0, 0))],
        out_shape=[jax.ShapeDtypeStruct((t, SSM_W), F32), jax.ShapeDtypeStruct((nblk, SUB, STATES), F32),
                   jax.ShapeDtypeStruct((nblk, SUB, STATES), F32)],
        scratch_shapes=[pltpu.VMEM((TM, STATES), F32), pltpu.VMEM((TM, STATES), F32),
                        pltpu.VMEM((SUB, STATES), F32), pltpu.VMEM((SUB, STATES), F32),
                        pltpu.VMEM((TM, SSM_W), F32)],
        compiler_params=_params())(proj, pm, pmt, mb_re, mb_im, p8_re, p8_im, q_re, q_im, cm_re, cm_im, dskip, wglu)


def _even_out(yp, ys, proj, x, w):
    t = x.shape[0]

    def body(yp_ref, ys_ref, z_ref, x_ref, w_ref, x1_ref, yg_ref):
        z = z_ref[...].astype(F32)
        sz = z * _sig(z)
        gp = (yp_ref[...] * sz[:, :POOL_W]).astype(BF16)
        gs = (ys_ref[...] * sz[:, POOL_W:]).astype(BF16)
        yg_ref[:, :POOL_W] = gp
        yg_ref[:, POOL_W:] = gs
        x1_ref[...] = x_ref[...] + _dot(gp, w_ref[:POOL_W, :]) + _dot(gs, w_ref[POOL_W:, :])

    return pl.pallas_call(
        body, name="even_out", grid=(t // TM_MM,),
        in_specs=[_mm_rows(POOL_W), _mm_rows(SSM_W), _mm_rows(D_MODEL, 1), _mm_rows(D_MODEL),
                  _full((D_MODEL, D_MODEL))],
        out_specs=[_mm_rows(D_MODEL), _mm_rows(D_MODEL)],
        out_shape=[jax.ShapeDtypeStruct((t, D_MODEL), F32), jax.ShapeDtypeStruct((t, D_MODEL), BF16)],
        compiler_params=_params())(yp, ys, proj, x, w)


def _phase_copies(ext, cp):
    n = cp.shape[1]
    for j in range(1, SUB):
        cp[j - 1] = ext[pl.ds(j, n), :]


def _shifted(ext, cp, off, nrows, sl, row0=0):
    q, j = divmod(off, SUB)
    if j == 0:
        return ext[pl.ds(row0 + SUB * q, nrows), sl]
    return cp[j - 1, pl.ds(row0 + SUB * q, nrows), sl]


def _conv_taps(ext, cp, w_ref, first, nrows, sl, init, row0=0):
    acc = init
    for k in range(CONV_K):
        acc = acc + w_ref[k:k + 1, sl] * _shifted(ext, cp, first(k), nrows, sl, row0)
    return acc


def _conv_fwd(q, cw, cb, lg, lb):
    t = q.shape[0]

    def body(v_ref, g_ref, hv_ref, hg_ref, z_ref, w_ref, b_ref, lg_ref, lb_ref, y_ref, cv_ref, ext, cp):
        i = pl.program_id(0)
        ext[0:HALO, :] = jnp.where(i == 0, 0.0, hv_ref[...].astype(F32) * _sig(hg_ref[...].astype(F32)))
        ext[HALO:, :] = v_ref[...].astype(F32) * _sig(g_ref[...].astype(F32))
        _phase_copies(ext, cp)

        def lanes(c, carry):
            sl = pl.ds(pl.multiple_of(c * 128, 128), 128)
            cv_ref[:, sl] = _conv_taps(ext, cp, w_ref, lambda k: k + 2, TM, sl,
                                       jnp.broadcast_to(b_ref[:, sl], (TM, 128)))
            return carry

        lax.fori_loop(0, D_MODEL // 128, lanes, 0)
        cv = cv_ref[...]
        cc = cv - jnp.mean(cv, axis=-1, keepdims=True)
        rstd = lax.rsqrt(jnp.mean(cc * cc, axis=-1, keepdims=True) + LN_EPS)
        cl = cc * rstd * lg_ref[...] + lb_ref[...]
        z = z_ref[...].astype(F32)
        y_ref[...] = (cl * _sig(cl) * z * _sig(z)).astype(BF16)

    vec = _full((1, D_MODEL))
    return pl.pallas_call(
        body, name="conv_fwd", grid=(t // TM,),
        in_specs=[_rows(D_MODEL, 0), _rows(D_MODEL, 1), _prev(HALO, D_MODEL, 0), _prev(HALO, D_MODEL, 1),
                  _rows(D_MODEL, 2), _full((HALO, D_MODEL)), vec, vec, vec],
        out_specs=[_rows(D_MODEL), _rows(D_MODEL)],
        out_shape=[jax.ShapeDtypeStruct((t, D_MODEL), BF16), jax.ShapeDtypeStruct((t, D_MODEL), F32)],
        scratch_shapes=[pltpu.VMEM((TM + HALO, D_MODEL), F32),
                        pltpu.VMEM((SUB - 1, TM + HALO - SUB, D_MODEL), F32)],
        compiler_params=_params())(q, q, q, q, q, cw, cb, lg, lb)


def _odd_out_loss(y2, x1, w, gf, tgt):
    t = x1.shape[0]

    def body(y_ref, x_ref, w_ref, g_ref, t_ref, dx_ref, loss_ref, dg_ref):
        i = pl.program_id(0)
        _zero_at_first(i, loss_ref, dg_ref)
        x2 = x_ref[...] + _dot(y_ref[...], w_ref[...])
        r = lax.rsqrt(jnp.mean(x2 * x2, axis=-1, keepdims=True) + RMS_EPS)
        n = x2 * r
        e = n * g_ref[...] - t_ref[...]
        loss_ref[...] += jnp.sum(e * e, axis=0, keepdims=True) * (0.5 / D_MODEL)
        dout = e * (1.0 / D_MODEL)
        dg_ref[...] += jnp.sum(dout * n, axis=0, keepdims=True)
        dn = dout * g_ref[...]
        dx_ref[...] = r * (dn - n * jnp.mean(dn * n, axis=-1, keepdims=True))

    vec = _full((1, D_MODEL))
    return pl.pallas_call(
        body, name="odd_out_loss", grid=(t // TM_MM,),
        in_specs=[_mm_rows(D_MODEL), _mm_rows(D_MODEL), _full((D_MODEL, D_MODEL)), vec, _mm_rows(D_MODEL)],
        out_specs=[_mm_rows(D_MODEL), vec, vec],
        out_shape=[jax.ShapeDtypeStruct((t, D_MODEL), F32), jax.ShapeDtypeStruct((1, D_MODEL), F32),
                   jax.ShapeDtypeStruct((1, D_MODEL), F32)],
        compiler_params=_params())(y2, x1, w, gf, tgt)


def _dsilu(z):
    s = _sig(z)
    return z * s, s * (1.0 + z * (1.0 - s))


def _odd_bwd_out(dx2, w, y2, cv, q, lg, lb):
    t = dx2.shape[0]

    def body(dx_ref, w_ref, y_ref, cv_ref, z_ref, lg_ref, lb_ref, dcv_ref, dz_ref, dw_ref, dlg_ref, dlb_ref):
        i = pl.program_id(0)
        _zero_at_first(i, dw_ref, dlg_ref, dlb_ref)
        dxb = dx_ref[...].astype(BF16)
        dy = _dot_nt(dxb, w_ref[...])
        dw_ref[...] += _dot_tn(y_ref[...], dxb)
        cv = cv_ref[...]
        cc = cv - jnp.mean(cv, axis=-1, keepdims=True)
        rstd = lax.rsqrt(jnp.mean(cc * cc, axis=-1, keepdims=True) + LN_EPS)
        cn = cc * rstd
        silu_c, dsilu_c = _dsilu(cn * lg_ref[...] + lb_ref[...])
        silu_z, dsilu_z = _dsilu(z_ref[...].astype(F32))
        dcl = dy * silu_z * dsilu_c
        dz_ref[...] = (dy * silu_c * dsilu_z).astype(BF16)
        dlg_ref[...] += jnp.sum(dcl * cn, axis=0, keepdims=True)
        dlb_ref[...] += jnp.sum(dcl, axis=0, keepdims=True)
        dcn = dcl * lg_ref[...]
        dcv_ref[...] = rstd * (dcn - jnp.mean(dcn, axis=-1, keepdims=True)
                               - cn * jnp.mean(dcn * cn, axis=-1, keepdims=True))

    vec = _full((1, D_MODEL))
    mat = _full((D_MODEL, D_MODEL))
    return pl.pallas_call(
        body, name="odd_bwd_out", grid=(t // TM_MM,),
        in_specs=[_mm_rows(D_MODEL), mat, _mm_rows(D_MODEL), _mm_rows(D_MODEL), _mm_rows(D_MODEL, 2), vec, vec],
        out_specs=[_mm_rows(D_MODEL), _mm_rows(D_MODEL), mat, vec, vec],
        out_shape=[jax.ShapeDtypeStruct((t, D_MODEL), F32), jax.ShapeDtypeStruct((t, D_MODEL), BF16),
                   jax.ShapeDtypeStruct((D_MODEL, D_MODEL), F32), jax.ShapeDtypeStruct((1, D_MODEL), F32),
                   jax.ShapeDtypeStruct((1, D_MODEL), F32)],
        compiler_params=_params())(dx2, w, y2, cv, q, lg, lb)


def _conv_bwd(dcv, q, cw):
    t = dcv.shape[0]
    nblk = t // TM

    def body(d_ref, dn_ref, v_ref, g_ref, hv_ref, hg_ref, w_ref,
             dv_ref, dgt_ref, dw_ref, db_ref, gext, dext, dgl, gcp, dcp):
        i = pl.program_id(0)
        last = nblk - 1
        _zero_at_first(i, dw_ref, db_ref)
        v = v_ref[...].astype(F32)
        sg = _sig(g_ref[...].astype(F32))
        gext[0:HALO, :] = jnp.where(i == 0, 0.0, hv_ref[...].astype(F32) * _sig(hg_ref[...].astype(F32)))
        gext[HALO:, :] = v * sg
        d = d_ref[...]
        dext[0:TM, :] = d
        dext[TM:, :] = jnp.where(i == last, 0.0, dn_ref[...])
        _phase_copies(gext, gcp)
        _phase_copies(dext, dcp)
        db_ref[...] += jnp.sum(d, axis=0, keepdims=True)
        def lanes(c, carry):
            sl = pl.ds(pl.multiple_of(c * 128, 128), 128)
            dgl[:, sl] = _conv_taps(dext, dcp, w_ref, lambda k: 30 - k, TM, sl, jnp.zeros((TM, 128), F32))
            return carry

        def lanes_w(c, carry):
            sl = pl.ds(pl.multiple_of(c * 128, 128), 128)
            sums = [None] * CONV_K
            for r0 in range(0, TM, SUB):
                dt = d_ref[r0:r0 + SUB, sl]
                for k in range(CONV_K):
                    prod = dt * _shifted(gext, gcp, k + 2, SUB, sl, r0)
                    sums[k] = prod if r0 == 0 else sums[k] + prod
            for k in range(CONV_K):
                dw_ref[SUB * k:SUB * (k + 1), sl] += sums[k]
            return carry

        lax.fori_loop(0, D_MODEL // 128, lanes, 0)
        lax.fori_loop(0, D_MODEL // 128, lanes_w, 0)
        dg = dgl[...]
        dv_ref[...] = (dg * sg).astype(BF16)
        dgt_ref[...] = (dg * v * sg * (1.0 - sg)).astype(BF16)

    return pl.pallas_call(
        body, name="conv_bwd", grid=(t // TM,),
        in_specs=[_rows(D_MODEL), _next(HALO, D_MODEL, t), _rows(D_MODEL, 0), _rows(D_MODEL, 1),
                  _prev(HALO, D_MODEL, 0), _prev(HALO, D_MODEL, 1), _full((HALO, D_MODEL))],
        out_specs=[_rows(D_MODEL), _rows(D_MODEL), _full((HALO * SUB, D_MODEL)), _full((1, D_MODEL))],
        out_shape=[jax.ShapeDtypeStruct((t, D_MODEL), BF16), jax.ShapeDtypeStruct((t, D_MODEL), BF16),
                   jax.ShapeDtypeStruct((HALO * SUB, D_MODEL), F32), jax.ShapeDtypeStruct((1, D_MODEL), F32)],
        scratch_shapes=[pltpu.VMEM((TM + HALO, D_MODEL), F32), pltpu.VMEM((TM + HALO, D_MODEL), F32),
                        pltpu.VMEM((TM, D_MODEL), F32),
                        pltpu.VMEM((SUB - 1, TM + HALO - SUB, D_MODEL), F32),
                        pltpu.VMEM((SUB - 1, TM + HALO - SUB, D_MODEL), F32)],
        compiler_params=_params())(dcv, dcv, q, q, q, q, cw)


def _column_segments(widths, ns):
    segs = []
    col = 0
    for p, wd in enumerate(widths):
        a = 0
        while a < wd:
            s, lo = divmod(col + a, ns)
            ln = min(wd - a, ns - lo)
            segs.append((p, a, a + ln, s, lo, lo + ln))
            a += ln
        col += wd
    return segs


def _in_bwd(dparts, w, x, g, dres, name):
    t = x.shape[0]
    widths = [p.shape[1] for p in dparts]
    npart = len(dparts)
    segs = _column_segments(widths, w.shape[2])

    def body(*refs):
        d_refs = refs[:npart]
        w_ref, x_ref, g_ref, r_ref, dx_ref, dg_ref, dw_ref = refs[npart:]
        i = pl.program_id(0)
        _zero_at_first(i, dg_ref, dw_ref)
        xv = x_ref[...]
        r = lax.rsqrt(jnp.mean(xv * xv, axis=-1, keepdims=True) + RMS_EPS)
        n = xv * r
        h = (n * g_ref[...]).astype(BF16)
        dh = None
        for p, lo, hi, s, slo, shi in segs:
            d = d_refs[p][:, lo:hi]
            part = _dot_nt(d, w_ref[s, :, slo:shi])
            dh = part if dh is None else dh + part
            dw_ref[s, :, slo:shi] += _dot_tn(h, d)
        dg_ref[...] += jnp.sum(dh * n, axis=0, keepdims=True)
        dn = dh * g_ref[...]
        dx_ref[...] = r_ref[...] + r * (dn - n * jnp.mean(dn * n, axis=-1, keepdims=True))

    vec = _full((1, D_MODEL))
    once = pl.BlockSpec(w.shape, lambda i: (0, 0, 0), pipeline_mode=pl.Buffered(1))
    return pl.pallas_call(
        body, name=name, grid=(t // TM_MM,),
        in_specs=[_mm_rows(wd) for wd in widths] + [once, _mm_rows(D_MODEL), vec, _mm_rows(D_MODEL)],
        out_specs=[_mm_rows(D_MODEL), vec, once],
        out_shape=[jax.ShapeDtypeStruct((t, D_MODEL), F32), jax.ShapeDtypeStruct((1, D_MODEL), F32),
                   jax.ShapeDtypeStruct(w.shape, F32)],
        compiler_params=_params())(*dparts, w, x, g, dres)


def _even_bwd_out(dx1, w, yg, yp, ys, proj):
    t = dx1.shape[0]

    def body(dx_ref, w_ref, yg_ref, yp_ref, ys_ref, z_ref, dy_ref, dz_ref, dw_ref):
        i = pl.program_id(0)
        _zero_at_first(i, dw_ref)
        dxb = dx_ref[...].astype(BF16)
        dyg = _dot_nt(dxb, w_ref[...])
        dw_ref[...] += _dot_tn(yg_ref[...], dxb)
        silu_z, dsilu_z = _dsilu(z_ref[...].astype(F32))
        dy_ref[...] = (dyg * silu_z).astype(BF16)
        dz_ref[:, :POOL_W] = (dyg[:, :POOL_W] * yp_ref[...] * dsilu_z[:, :POOL_W]).astype(BF16)
        dz_ref[:, POOL_W:] = (dyg[:, POOL_W:] * ys_ref[...] * dsilu_z[:, POOL_W:]).astype(BF16)

    mat = _full((D_MODEL, D_MODEL))
    return pl.pallas_call(
        body, name="even_bwd_out", grid=(t // TM_MM,),
        in_specs=[_mm_rows(D_MODEL), mat, _mm_rows(D_MODEL), _mm_rows(POOL_W), _mm_rows(SSM_W), _mm_rows(D_MODEL, 1)],
        out_specs=[_mm_rows(D_MODEL), _mm_rows(D_MODEL), mat],
        out_shape=[jax.ShapeDtypeStruct((t, D_MODEL), BF16), jax.ShapeDtypeStruct((t, D_MODEL), BF16),
                   jax.ShapeDtypeStruct((D_MODEL, D_MODEL), F32)],
        compiler_params=_params())(dx1, w, yg, yp, ys, proj)


def _pool_bwd(dycat, proj, wp, ps):
    t = proj.shape[0]

    def body(dy_ref, dyn_ref, u_ref, h_ref, wp_ref, ps_ref, du_ref, dwp_ref, dps_ref):
        i = pl.program_id(0)
        last = t // TM - 1
        _zero_at_first(i, dwp_ref, dps_ref)
        pos = (i * TM + 1 + lax.broadcasted_iota(jnp.int32, (TM, 1), 0)).astype(F32)
        pos_ext = (i * TM + 1 + lax.broadcasted_iota(jnp.int32, (TM + POOL_HALO, 1), 0)).astype(F32)
        for g in range(4):
            sl = slice(128 * g, 128 * (g + 1))
            w = float(2 << g)
            u = u_ref[:, sl].astype(F32)
            halo = jnp.where(i == 0, 0.0, h_ref[:, sl].astype(F32))
            s = _pool_sums(jnp.concatenate([halo, u], axis=0), g, True)[POOL_HALO:, :]
            pooled = (s / jnp.minimum(pos, w) - u).astype(BF16)
            dy = dy_ref[:, sl].astype(F32)
            dps_ref[:, sl] += jnp.sum(dy * _dot(pooled, wp_ref[g]), axis=0, keepdims=True)
            dy_ext = jnp.concatenate([dy, jnp.where(i == last, 0.0, dyn_ref[:, sl].astype(F32))], axis=0)
            dmix = (dy_ext * ps_ref[:, sl]).astype(BF16)
            dwp_ref[g] += _dot_tn(pooled, dmix[:TM, :])
            dpool = _dot_nt(dmix, wp_ref[g])
            lead = _pool_sums(dpool / jnp.minimum(pos_ext, w), g, False)
            du_ref[:, sl] = (lead[:TM, :] - dpool[:TM, :]).astype(BF16)

    return pl.pallas_call(
        body, name="pool_bwd", grid=(t // TM,),
        in_specs=[_rows(POOL_W, 0), _next(POOL_HALO, POOL_W, t, 0), _rows(POOL_W, 0), _prev(POOL_HALO, POOL_W, 0),
                  _full((4, 128, 128)), _full((1, POOL_W))],
        out_specs=[_rows(POOL_W), _full((4, 128, 128)), _full((1, POOL_W))],
        out_shape=[jax.ShapeDtypeStruct((t, POOL_W), BF16), jax.ShapeDtypeStruct((4, 128, 128), F32),
                   jax.ShapeDtypeStruct((1, POOL_W), F32)],
        compiler_params=_params())(dycat, dycat, proj, proj, wp, ps)


def _ssm_bwd(dycat, proj, car_in_re, car_in_im, pm, pmt, mb_re, mb_im, p8_re, p8_im, q_re, q_im, qr_re, qr_im,
             cm_re, cm_im, dskip, wglu):
    t = proj.shape[0]
    nblk = t // TM

    def body(dy_ref, u_ref, cin_re, cin_im, pm_ref, pmt_ref, mbre, mbim, p8re, p8im, qre, qim, qrre, qrim,
             cmre, cmim, d_ref, wg_ref,
             du_ref, dmbre, dmbim, dcmre, dcmim, dare, daim, dd_ref, dwg_ref,
             xs_re, xs_im, gs_re, gs_im, car_re, car_im, ent_re, ent_im, gcar_re, gcar_im, ysk, dysk):
        i = pl.program_id(0)
        _zero_at_first(i, dmbre, dmbim, dcmre, dcmim, dare, daim, dd_ref, dwg_ref, gcar_re, gcar_im)
        us = _dot(pm_ref[...], u_ref[...])
        usb = us.astype(BF16)
        for j in range(4):
            xs_re[:, LCH * j:LCH * (j + 1)] = _dot(usb[:, 128 * j:128 * (j + 1)], mbre[j])
            xs_im[:, LCH * j:LCH * (j + 1)] = _dot(usb[:, 128 * j:128 * (j + 1)], mbim[j])
        car_re[...] = cin_re[0]
        car_im[...] = cin_im[0]
        _scan_fwd_block(xs_re, xs_im, p8re, p8im, qre, qim, car_re, car_im, ent_re, ent_im)
        for j in range(4):
            sl = slice(LCH * j, LCH * (j + 1))
            ysk[:, 128 * j:128 * (j + 1)] = (_dot_nt(xs_re[:, sl].astype(BF16), cmre[j])
                                             - _dot_nt(xs_im[:, sl].astype(BF16), cmim[j]))
        yvb = (ysk[...] + d_ref[...] * us).astype(BF16)
        gv = _dot(yvb, wg_ref[...])
        sg = _sig(gv[:, SSM_W:])
        dyss = _dot(pm_ref[...], dy_ref[...])
        dval = (dyss * sg).astype(BF16)
        dgate = (dyss * gv[:, :SSM_W] * sg * (1.0 - sg)).astype(BF16)
        dy = _dot_nt(dval, wg_ref[:, :SSM_W]) + _dot_nt(dgate, wg_ref[:, SSM_W:])
        dwg_ref[:, :SSM_W] += _dot_tn(yvb, dval)
        dwg_ref[:, SSM_W:] += _dot_tn(yvb, dgate)
        dd_ref[...] += jnp.sum(dy * us, axis=0, keepdims=True)
        dysk[...] = dy
        for j in range(4):
            sl = slice(LCH * j, LCH * (j + 1))
            dyj = dy[:, 128 * j:128 * (j + 1)].astype(BF16)
            gs_re[:, sl] = _dot(dyj, cmre[j])
            gs_im[:, sl] = -_dot(dyj, cmim[j])
            dcmre[j] += _dot_tn(dyj, xs_re[:, sl].astype(BF16))
            dcmim[j] -= _dot_tn(dyj, xs_im[:, sl].astype(BF16))
        row = lax.broadcasted_iota(jnp.int32, (SUB, SCAN_L), 0)
        for j in range(STATES // SCAN_L):
            sl = slice(SCAN_L * j, SCAN_L * (j + 1))
            are, aim = p8re[0:SUB, sl], -p8im[0:SUB, sl]

            def totals(k, v, sl=sl, are=are, aim=aim):
                r0 = pl.multiple_of((SEG_LEN - 2 - k) * SUB, SUB)
                vre, vim = _cmul_add(are, aim, v[0], v[1], gs_re[pl.ds(r0, SUB), sl], gs_im[pl.ds(r0, SUB), sl])
                gs_re[pl.ds(r0, SUB), sl] = vre
                gs_im[pl.ds(r0, SUB), sl] = vim
                return vre, vim

            top = (SEG_LEN - 1) * SUB
            fre, fim = lax.fori_loop(0, SEG_LEN - 1, totals,
                                     (gs_re[top:top + SUB, sl], gs_im[top:top + SUB, sl]), unroll=2)
            fre, fim, nre, nim = _segment_chain(fre, fim, qrre[:, sl], -qrim[:, sl],
                                                gcar_re[:, sl], gcar_im[:, sl], row, True)
            gcar_re[:, sl] = jnp.broadcast_to(fre[0:1, :], (SUB, SCAN_L))
            gcar_im[:, sl] = jnp.broadcast_to(fim[0:1, :], (SUB, SCAN_L))

            def fix(i2, acc, sl=sl, nre=nre, nim=nim):
                r0 = pl.multiple_of(i2 * SUB, SUB)
                rb = pl.multiple_of((SEG_LEN - 1 - i2) * SUB, SUB)
                gre, gim = _cmul_add(p8re[pl.ds(rb, SUB), sl], -p8im[pl.ds(rb, SUB), sl], nre, nim,
                                     gs_re[pl.ds(r0, SUB), sl], gs_im[pl.ds(r0, SUB), sl])
                gs_re[pl.ds(r0, SUB), sl] = gre
                gs_im[pl.ds(r0, SUB), sl] = gim
                rp = pl.multiple_of((i2 - 1) * SUB, SUB)
                xre, xim = xs_re[pl.ds(rp, SUB), sl], xs_im[pl.ds(rp, SUB), sl]
                return acc[0] + gre * xre + gim * xim, acc[1] + gim * xre - gre * xim

            g0re, g0im = _cmul_add(p8re[top:top + SUB, sl], -p8im[top:top + SUB, sl], nre, nim,
                                   gs_re[0:SUB, sl], gs_im[0:SUB, sl])
            gs_re[0:SUB, sl] = g0re
            gs_im[0:SUB, sl] = g0im
            ere, eim = ent_re[:, sl], ent_im[:, sl]
            acc0 = (dare[:, sl] + g0re * ere + g0im * eim, daim[:, sl] + g0im * ere - g0re * eim)
            are_acc, aim_acc = lax.fori_loop(1, SEG_LEN, fix, acc0, unroll=2)
            dare[:, sl] = are_acc
            daim[:, sl] = aim_acc
        for j in range(4):
            sl = slice(LCH * j, LCH * (j + 1))
            c4 = slice(128 * j, 128 * (j + 1))
            gre = gs_re[:, sl].astype(BF16)
            gim = gs_im[:, sl].astype(BF16)
            dmbre[j] += _dot_tn(usb[:, c4], gre)
            dmbim[j] += _dot_tn(usb[:, c4], gim)
            dysk[:, c4] = _dot_nt(gre, mbre[j]) + _dot_nt(gim, mbim[j]) + dysk[:, c4] * d_ref[:, c4]
        du_ref[...] = _dot(pmt_ref[...], dysk[...].astype(BF16)).astype(BF16)

    blk = (4, 128, LCH)
    pw = _full((SUB, STATES))
    p8 = _full((TM, STATES))
    perm = _full((TM, TM))
    car = pl.BlockSpec((1, SUB, STATES), lambda i: (nblk - 1 - i, 0, 0))
    big = lambda: pltpu.VMEM((TM, STATES), F32)
    small = lambda: pltpu.VMEM((SUB, STATES), F32)
    return pl.pallas_call(
        body, name="ssm_bwd", grid=(nblk,),
        in_specs=[_rows(SSM_W, 1, rev=nblk), _rows(SSM_W, 1, rev=nblk), car, car, perm, perm, _full(blk), _full(blk),
                  p8, p8, pw, pw, pw, pw, _full(blk), _full(blk), _full((1, SSM_W)), _full((SSM_W, 2 * SSM_W))],
        out_specs=[_rows(SSM_W, 0, rev=nblk), _full(blk), _full(blk), _full(blk), _full(blk), pw, pw,
                   _full((1, SSM_W)), _full((SSM_W, 2 * SSM_W))],
        out_shape=[jax.ShapeDtypeStruct((t, SSM_W), BF16)] + [jax.ShapeDtypeStruct(blk, F32)] * 4
        + [jax.ShapeDtypeStruct((SUB, STATES), F32)] * 2
        + [jax.ShapeDtypeStruct((1, SSM_W), F32), jax.ShapeDtypeStruct((SSM_W, 2 * SSM_W), F32)],
        scratch_shapes=[big(), big(), big(), big(), small(), small(), small(), small(), small(), small(),
                        pltpu.VMEM((TM, SSM_W), F32), pltpu.VMEM((TM, SSM_W), F32)],
        compiler_params=_params())(dycat, proj, car_in_re, car_in_im, pm, pmt, mb_re, mb_im, p8_re, p8_im,
                                   q_re, q_im, qr_re, qr_im, cm_re, cm_im, dskip, wglu)


def _adamw(w, g, m, v, name):
    rows = w.shape[0]
    tr = 256 if rows % 256 == 0 else rows
    c1 = 1.0 / (1.0 - ADAM_B1 ** ADAM_STEP)
    c2 = 1.0 / (1.0 - ADAM_B2 ** ADAM_STEP)

    def body(w_ref, g_ref, m_ref, v_ref, d_ref, nm_ref, nv_ref):
        gv = g_ref[...]
        m = ADAM_B1 * m_ref[...] + (1.0 - ADAM_B1) * gv
        v = ADAM_B2 * v_ref[...] + (1.0 - ADAM_B2) * (gv * gv)
        nm_ref[...] = m
        nv_ref[...] = v
        d_ref[...] = -ADAM_LR * ((m * c1) / (jnp.sqrt(v * c2) + ADAM_EPS) + ADAM_WD * w_ref[...])

    spec = pl.BlockSpec((tr, D_MODEL), lambda i: (i, 0))
    shp = jax.ShapeDtypeStruct((rows, D_MODEL), F32)
    return pl.pallas_call(
        body, name=name, grid=(rows // tr,), in_specs=[spec] * 4, out_specs=[spec] * 3, out_shape=[shp] * 3,
        compiler_params=_params())(w, g, m, v)


def _core_index():
    return lax.axis_index("c").astype(jnp.int32).reshape(1)


def _pair_add(g, theirs, out_dtype):
    n, half, _ = theirs.shape
    nb = half // 256

    def body(c_ref, a_ref, b_ref, o_ref):
        o_ref[...] = (a_ref[...] + b_ref[...]).astype(out_dtype)

    spec = pl.BlockSpec((1, 256, D_MODEL), lambda i, j, c: (i, j, 0))
    grid_spec = pltpu.PrefetchScalarGridSpec(
        num_scalar_prefetch=1, grid=(n, nb),
        in_specs=[pl.BlockSpec((1, 256, D_MODEL), lambda i, j, c: (i, c[0] * nb + j, 0)), spec], out_specs=spec)
    return pl.pallas_call(
        body, name="pair_add", grid_spec=grid_spec, out_shape=jax.ShapeDtypeStruct(theirs.shape, out_dtype),
        compiler_params=_params(2))(_core_index(), g, theirs)


def _adamw_rows(w, m, v, g_mine, g_theirs, row0, br, name):
    rows = w.shape[0]
    b0 = row0 // br
    per_half = g_mine.shape[0] // br
    c1 = 1.0 / (1.0 - ADAM_B1 ** ADAM_STEP)
    c2 = 1.0 / (1.0 - ADAM_B2 ** ADAM_STEP)

    def body(c_ref, w_ref, gm_ref, gt_ref, m_ref, v_ref, g_ref, d_ref, nm_ref, nv_ref):
        gv = jnp.where((b0 + pl.program_id(0)) // per_half == c_ref[0], gm_ref[...], gt_ref[...])
        m = ADAM_B1 * m_ref[...] + (1.0 - ADAM_B1) * gv
        v = ADAM_B2 * v_ref[...] + (1.0 - ADAM_B2) * (gv * gv)
        g_ref[...] = gv
        nm_ref[...] = m
        nv_ref[...] = v
        d_ref[...] = -ADAM_LR * ((m * c1) / (jnp.sqrt(v * c2) + ADAM_EPS) + ADAM_WD * w_ref[...])

    spec = pl.BlockSpec((br, D_MODEL), lambda i, c: (i, 0))
    part = pl.BlockSpec((br, D_MODEL), lambda i, c: ((b0 + i) % per_half, 0))
    shp = jax.ShapeDtypeStruct((rows, D_MODEL), F32)
    grid_spec = pltpu.PrefetchScalarGridSpec(
        num_scalar_prefetch=1, grid=(rows // br,), in_specs=[spec, part, part, spec, spec], out_specs=[spec] * 4)
    return pl.pallas_call(
        body, name=name, grid_spec=grid_spec, out_shape=[shp] * 4,
        compiler_params=_params())(_core_index(), w, g_mine, g_theirs, m, v)


def _sum_lead(a, name):
    n, rows, _ = a.shape
    tr = 256 if rows % 256 == 0 else rows

    def body(a_ref, o_ref):
        acc = a_ref[0].astype(F32)
        for k in range(1, n):
            acc = acc + a_ref[k].astype(F32)
        o_ref[...] = acc

    return pl.pallas_call(
        body, name=name, grid=(rows // tr,),
        in_specs=[pl.BlockSpec((n, tr, D_MODEL), lambda i: (0, i, 0))],
        out_specs=pl.BlockSpec((tr, D_MODEL), lambda i: (i, 0)),
        out_shape=jax.ShapeDtypeStruct((rows, D_MODEL), F32), compiler_params=_params())(a)


ANY = pl.BlockSpec(memory_space=pl.ANY)


def _mesh_pos():
    return lax.axis_index("x"), lax.axis_index("y"), lax.axis_index("c")


def _gather_weights(arrs):
    na = len(arrs)
    halves = [a.shape[0] // 2 for a in arrs]
    ncopy = 3 * na

    def body(*refs):
        in_refs, out_refs, bounces = refs[:na], refs[na:2 * na], refs[2 * na:3 * na]
        send_sems, recv_sems, local_sems = refs[3 * na:]
        x, y, c = _mesh_pos()
        me = 2 * x + y
        sibling = (x, y, 1 - c)
        chips = [(1 - x, y), (x, 1 - y), (1 - x, 1 - y)]
        ids = [2 * chip[0] + chip[1] for chip in chips]

        def piece(a, q, h):
            return out_refs[a].at[q, pl.ds(h * halves[a], halves[a]), :]

        def copy(s, a, q, h, to, src=None):
            return pltpu.make_async_remote_copy(
                src_ref=piece(a, q, h) if src is None else src, dst_ref=piece(a, q, h),
                send_sem=send_sems.at[s], recv_sem=recv_sems.at[s], device_id=to, device_id_type=MESH)

        loads = [pltpu.make_async_copy(in_refs[a], bounces[a], local_sems.at[a]) for a in range(na)]
        stores = [pltpu.make_async_copy(bounces[a], out_refs[a].at[me], local_sems.at[na + a]) for a in range(na)]
        for cp in loads:
            cp.start()
        first = [copy(j * na + a, a, me, c, (*chip, c), src=in_refs[a].at[pl.ds(c * halves[a], halves[a]), :])
                 for j, chip in enumerate(chips) for a in range(na)]
        for cp in first:
            cp.start()
        for a in range(na):
            loads[a].wait()
            stores[a].start()
        passed = []
        for j in range(3):
            for a in range(na):
                s = j * na + a
                copy(s, a, ids[j], c, (x, y, c)).wait_recv()
                fwd = copy(ncopy + s, a, ids[j], c, sibling)
                fwd.start()
                passed.append(fwd)
        for j in range(3):
            for a in range(na):
                copy(ncopy + j * na + a, a, ids[j], 1 - c, (x, y, c)).wait_recv()
        for cp in first + passed:
            cp.wait_send()
        for cp in stores:
            cp.wait()

    return pl.pallas_call(
        body, name="gather_weights", in_specs=[ANY] * na, out_specs=[ANY] * na,
        out_shape=[jax.ShapeDtypeStruct((4,) + a.shape, a.dtype) for a in arrs],
        scratch_shapes=[pltpu.VMEM(a.shape, a.dtype) for a in arrs]
        + [pltpu.SemaphoreType.DMA((2 * ncopy,)), pltpu.SemaphoreType.DMA((2 * ncopy,)),
           pltpu.SemaphoreType.DMA((2 * na,))],
        compiler_params=pltpu.CompilerParams(vmem_limit_bytes=VMEM_LIMIT),
    )(*arrs)


def _gather_all(v):
    def body(v_ref, o_ref, bounce, send_sems, recv_sems, local_sems):
        x, y, c = _mesh_pos()
        sibling = (x, y, 1 - c)
        chips = [(1 - x, y), (x, 1 - y), (1 - x, 1 - y)]

        def blk(px, py, pc):
            return o_ref.at[4 * px + 2 * py + pc]

        def copy(k, block, to, src=None):
            return pltpu.make_async_remote_copy(
                src_ref=blk(*block) if src is None else src, dst_ref=blk(*block),
                send_sem=send_sems.at[k], recv_sem=recv_sems.at[k], device_id=to, device_id_type=MESH)

        load = pltpu.make_async_copy(v_ref, bounce, local_sems.at[0])
        store = pltpu.make_async_copy(bounce, blk(x, y, c), local_sems.at[1])
        load.start()
        first = [copy(0, (x, y, c), sibling, src=v_ref)]
        first += [copy(1 + j, (x, y, c), (*chip, c), src=v_ref) for j, chip in enumerate(chips)]
        for cp in first:
            cp.start()
        load.wait()
        store.start()
        passed = [copy(4 + j, (*chip, c), sibling) for j, chip in enumerate(chips)]
        for j, chip in enumerate(chips):
            copy(1 + j, (*chip, c), (x, y, c)).wait_recv()
            passed[j].start()
        copy(0, (x, y, 1 - c), (x, y, c)).wait_recv()
        for j, chip in enumerate(chips):
            copy(4 + j, (*chip, 1 - c), (x, y, c)).wait_recv()
        for cp in first + passed:
            cp.wait_send()
        store.wait()

    return pl.pallas_call(
        body, name="gather_all", in_specs=[ANY], out_specs=ANY,
        out_shape=jax.ShapeDtypeStruct((8,) + v.shape, v.dtype),
        scratch_shapes=[pltpu.VMEM(v.shape, v.dtype), pltpu.SemaphoreType.DMA((7,)), pltpu.SemaphoreType.DMA((7,)),
                        pltpu.SemaphoreType.DMA((2,))],
    )(v)


def _pair_split(g):
    n, rows, _ = g.shape
    half = rows // 2
    ch = half // COMM_CHUNKS

    def body(g_ref, theirs_ref, send_sems, recv_sems):
        x, y, c = _mesh_pos()
        gives = [pltpu.make_async_remote_copy(
            src_ref=g_ref.at[q, pl.ds((1 - c) * half + k * ch, ch), :],
            dst_ref=theirs_ref.at[q, pl.ds(k * ch, ch), :],
            send_sem=send_sems.at[q * COMM_CHUNKS + k], recv_sem=recv_sems.at[q * COMM_CHUNKS + k],
            device_id=(x, y, 1 - c), device_id_type=MESH) for q in range(n) for k in range(COMM_CHUNKS)]
        for cp in gives:
            cp.start()
        for cp in gives:
            cp.wait()

    return pl.pallas_call(
        body, name="pair_split", in_specs=[ANY], out_specs=ANY,
        out_shape=jax.ShapeDtypeStruct((n, half, D_MODEL), g.dtype),
        scratch_shapes=[pltpu.SemaphoreType.DMA((n * COMM_CHUNKS,)), pltpu.SemaphoreType.DMA((n * COMM_CHUNKS,))],
    )(g)


def _chip_scatter(p):
    def body(p_ref, o_ref, bounce, send_sems, recv_sems, local_sems):
        x, y, c = _mesh_pos()
        me = 2 * x + y
        chips = [(1 - x, y), (x, 1 - y), (1 - x, 1 - y)]
        load = pltpu.make_async_copy(p_ref.at[me], bounce, local_sems.at[0])
        keep = pltpu.make_async_copy(bounce, o_ref.at[me], local_sems.at[1])
        load.start()
        sends = [pltpu.make_async_remote_copy(
            src_ref=p_ref.at[2 * chip[0] + chip[1]], dst_ref=o_ref.at[me],
            send_sem=send_sems.at[j], recv_sem=recv_sems.at[j], device_id=(*chip, c), device_id_type=MESH)
            for j, chip in enumerate(chips)]
        for cp in sends:
            cp.start()
        load.wait()
        keep.start()
        for j, chip in enumerate(chips):
            q = 2 * chip[0] + chip[1]
            pltpu.make_async_remote_copy(
                src_ref=p_ref.at[q], dst_ref=o_ref.at[q], send_sem=send_sems.at[j], recv_sem=recv_sems.at[j],
                device_id=(*chip, c), device_id_type=MESH).wait_recv()
        for cp in sends:
            cp.wait_send()
        keep.wait()

    return pl.pallas_call(
        body, name="chip_scatter", in_specs=[ANY], out_specs=ANY, out_shape=jax.ShapeDtypeStruct(p.shape, p.dtype),
        scratch_shapes=[pltpu.VMEM(p.shape[1:], p.dtype), pltpu.SemaphoreType.DMA((3,)),
                        pltpu.SemaphoreType.DMA((3,)), pltpu.SemaphoreType.DMA((2,))],
    )(p)


def _pair_join(r):
    rows = r.shape[0]
    ch = rows // COMM_CHUNKS

    def body(r_ref, o_ref, send_sems, recv_sems):
        x, y, c = _mesh_pos()
        gives = [pltpu.make_async_remote_copy(
            src_ref=r_ref.at[pl.ds(k * ch, ch), :], dst_ref=o_ref.at[pl.ds(k * ch, ch), :],
            send_sem=send_sems.at[k], recv_sem=recv_sems.at[k], device_id=(x, y, 1 - c), device_id_type=MESH)
            for k in range(COMM_CHUNKS)]
        for cp in gives:
            cp.start()
        for cp in gives:
            cp.wait()

    return pl.pallas_call(
        body, name="pair_join", in_specs=[ANY], out_specs=ANY, out_shape=jax.ShapeDtypeStruct(r.shape, r.dtype),
        scratch_shapes=[pltpu.SemaphoreType.DMA((COMM_CHUNKS,)), pltpu.SemaphoreType.DMA((COMM_CHUNKS,))],
    )(r)


SHARD_BIG = (("even_w_in", (1024, 512)), ("ssm_w_glu", (512, 256)), ("even_w_out", (256, 1024)),
             ("odd_w_in", (1024, 768)), ("odd_w_out", (256, 1024)))
SHARD_SMALL = (("odd_norm", 1), ("conv_w", CONV_K), ("conv_b", 1), ("conv_ln_g", 1), ("conv_ln_b", 1))
REP_NAMES = (("even_norm", (1024,)), ("pool_w", (4, 128, 128)), ("pool_scale", (512,)), ("ssm_log_dt", (32,)),
             ("ssm_a_re", (32, 64)), ("ssm_a_im", (32, 64)), ("ssm_b_re", (32, 64, 16)), ("ssm_b_im", (32, 64, 16)),
             ("ssm_c_re", (32, 16, 64)), ("ssm_c_im", (32, 16, 64)), ("ssm_d", (512,)), ("final_norm", (1024,)))


def _pack_rep(d):
    flat = jnp.concatenate([d[n].reshape(-1) for n, _ in REP_NAMES])
    return jnp.pad(flat, (0, REP_ROWS * D_MODEL - flat.shape[0])).reshape(REP_ROWS, D_MODEL)


def _unpack_rep(buf):
    flat = buf.reshape(-1)
    out = {}
    off = 0
    for n, shp in REP_NAMES:
        size = 1
        for s in shp:
            size *= s
        out[n] = flat[off:off + size].reshape(shp)
        off += size
    return out


def _cols_split(full, cols):
    rows = full.shape[0]
    return full.reshape(rows, 4, cols).transpose(1, 0, 2).reshape(4, -1, D_MODEL)


def _block_diag(a):
    a = a.reshape(4, 8, GROUP_DIM, N_STATE)
    eye = jnp.eye(8, dtype=a.dtype)
    return (a[:, :, :, None, :] * eye[None, :, None, :, None]).reshape(4, 128, LCH)


def _block_diag_take(m):
    m = m.reshape(4, 8, GROUP_DIM, 8, N_STATE)
    eye = jnp.eye(8, dtype=m.dtype)
    return jnp.sum(m * eye[None, :, None, :, None], axis=3).reshape(N_GROUPS, GROUP_DIM, N_STATE)


def _ssm_discretise(log_dt, a_re, a_im, b_re, b_im):
    dt = jnp.exp(log_dt)[:, None]
    mag = jnp.exp(a_re * dt)
    ang = a_im * dt
    abar_re = mag * jnp.cos(ang)
    abar_im = mag * jnp.sin(ang)
    den = a_re * a_re + a_im * a_im
    nr = abar_re - 1.0
    ni = abar_im
    k_re = (nr * a_re + ni * a_im) / den
    k_im = (ni * a_re - nr * a_im) / den
    bb_re = k_re[..., None] * b_re - k_im[..., None] * b_im
    bb_im = k_re[..., None] * b_im + k_im[..., None] * b_re
    return abar_re, abar_im, bb_re, bb_im


def _scan_tables(log_dt, a_re, a_im):
    dt = jnp.exp(log_dt)[:, None]
    lam_re = (a_re * dt).reshape(1, STATES)
    lam_im = (a_im * dt).reshape(1, STATES)

    def powers(k):
        mag = jnp.exp(k * lam_re)
        return mag * jnp.cos(k * lam_im), mag * jnp.sin(k * lam_im)

    p_re, p_im = powers((1 + jnp.arange(TM) // SUB).astype(F32)[:, None])
    q_re, q_im = powers((SEG_LEN * (1 + jnp.arange(SUB))).astype(F32)[:, None])
    return p_re, p_im, q_re, q_im


def _local_step(x, tgt, w):
    row = lambda a: a.reshape(1, -1)
    e_w_in, e_w_out, o_w_in, o_w_out, wglu = (w["even_w_in"], w["even_w_out"], w["odd_w_in"], w["odd_w_out"],
                                              w["ssm_w_glu"])
    wp = w["pool_w"].astype(BF16)
    sm = w["odd_small"]
    ssm_in = (w["ssm_log_dt"], w["ssm_a_re"], w["ssm_a_im"], w["ssm_b_re"], w["ssm_b_im"])
    (abar_re, abar_im, bb_re, bb_im), ssm_vjp = jax.vjp(_ssm_discretise, *ssm_in)
    mb_re = _block_diag(bb_re.transpose(0, 2, 1)).astype(BF16)
    mb_im = _block_diag(bb_im.transpose(0, 2, 1)).astype(BF16)
    cm_re = _block_diag(w["ssm_c_re"]).astype(BF16)
    cm_im = _block_diag(w["ssm_c_im"]).astype(BF16)
    p8_re, p8_im, q_re, q_im = _scan_tables(w["ssm_log_dt"], w["ssm_a_re"], w["ssm_a_im"])
    qr_re, qr_im = q_re[::-1], q_im[::-1]
    pm = _perm_matrix()
    pmt = pm.T
    cw = sm[1:1 + HALO]
    g0, g1, gf = row(w["even_norm"]), sm[0:1], row(w["final_norm"])
    ps, dskip = row(w["pool_scale"]), row(w["ssm_d"])
    cb, lg, lb = sm[32:33], sm[33:34], sm[34:35]

    proj = _norm_in(x, g0, e_w_in, "even_in")
    yp = _pool_fwd(proj, wp, ps)
    ys, car_re, car_im = _ssm_fwd(proj, pm, pmt, mb_re, mb_im, p8_re, p8_im, q_re, q_im, cm_re, cm_im, dskip, wglu)
    x1, yg = _even_out(yp, ys, proj, x, e_w_out)
    q = _norm_in(x1, g1, o_w_in, "odd_in")
    y2, cv = _conv_fwd(q, cw, cb, lg, lb)
    dx2, loss_lanes, d_gf = _odd_out_loss(y2, x1, o_w_out, gf, tgt)

    dcv, dz2, d_o_w_out, d_lg, d_lb = _odd_bwd_out(dx2, o_w_out, y2, cv, q, lg, lb)
    dval, dgate, d_cw, d_cb = _conv_bwd(dcv, q, cw)
    dx1, d_g1, d_o_w_in = _in_bwd([dval, dgate, dz2], o_w_in, x1, g1, dx2, "odd_in_bwd")
    dycat, dz, d_e_w_out = _even_bwd_out(dx1, e_w_out, yg, yp, ys, proj)
    dup, d_wp, d_ps = _pool_bwd(dycat, proj, wp, ps)
    (dus, d_mb_re, d_mb_im, d_cm_re, d_cm_im, da_re, da_im, d_dskip, d_wglu) = _ssm_bwd(
        dycat, proj, car_re, car_im, pm, pmt, mb_re, mb_im, p8_re, p8_im, q_re, q_im, qr_re, qr_im,
        cm_re, cm_im, dskip, wglu)
    dx, d_g0, d_e_w_in = _in_bwd([dup, dus, dz], e_w_in, x, g0, dx1, "even_in_bwd")

    d_abar_re = jnp.sum(da_re, axis=0).reshape(N_GROUPS, N_STATE)
    d_abar_im = jnp.sum(da_im, axis=0).reshape(N_GROUPS, N_STATE)
    d_bb_re = _block_diag_take(d_mb_re).transpose(0, 2, 1)
    d_bb_im = _block_diag_take(d_mb_im).transpose(0, 2, 1)
    d_log_dt, d_a_re, d_a_im, d_b_re, d_b_im = ssm_vjp((d_abar_re, d_abar_im, d_bb_re, d_bb_im))

    grads = {
        "even_norm": d_g0.reshape(-1), "even_w_in": d_e_w_in, "pool_w": d_wp, "pool_scale": d_ps.reshape(-1),
        "ssm_log_dt": d_log_dt, "ssm_a_re": d_a_re, "ssm_a_im": d_a_im, "ssm_b_re": d_b_re, "ssm_b_im": d_b_im,
        "ssm_c_re": _block_diag_take(d_cm_re), "ssm_c_im": _block_diag_take(d_cm_im),
        "ssm_d": d_dskip.reshape(-1), "ssm_w_glu": d_wglu, "even_w_out": d_e_w_out, "odd_norm": d_g1.reshape(-1),
        "odd_w_in": d_o_w_in, "conv_w": d_cw.reshape(HALO, SUB, D_MODEL).sum(axis=1)[:CONV_K], "conv_b": d_cb.reshape(-1), "conv_ln_g": d_lg.reshape(-1),
        "conv_ln_b": d_lb.reshape(-1), "odd_w_out": d_o_w_out, "final_norm": d_gf.reshape(-1),
    }
    return jnp.sum(loss_lanes), dx, grads


WEIGHT_NAMES = ("even_norm", "even_w_in", "pool_w", "pool_scale", "ssm_log_dt", "ssm_a_re", "ssm_a_im",
                "ssm_b_re", "ssm_b_im", "ssm_c_re", "ssm_c_im", "ssm_d", "ssm_w_glu", "even_w_out", "odd_norm",
                "odd_w_in", "conv_w", "conv_b", "conv_ln_g", "conv_ln_b", "odd_w_out", "final_norm")
SHARDED = tuple(n for n, _ in SHARD_BIG) + tuple(n for n, _ in SHARD_SMALL)


SMALL_ROWS = 64


def _full_weights(shard, rep):
    small = jnp.concatenate([shard[n].reshape(r, 256) for n, r in SHARD_SMALL], axis=0)
    small = jnp.pad(small, ((0, SMALL_ROWS - small.shape[0]), (0, 0)))
    arrs = [shard[n].astype(BF16) for n, _ in SHARD_BIG] + [small]
    g_in, g_glu, g_eout, g_oin, g_oout, g_small = _gather_weights(arrs)
    w = dict(rep)
    w["even_w_in"] = g_in
    w["ssm_w_glu"] = g_glu.transpose(1, 0, 2).reshape(SSM_W, 2 * SSM_W)
    w["even_w_out"] = g_eout.reshape(D_MODEL, D_MODEL)
    w["odd_w_in"] = g_oin
    w["odd_w_out"] = g_oout.reshape(D_MODEL, D_MODEL)
    w["odd_small"] = g_small.transpose(1, 0, 2).reshape(SMALL_ROWS, D_MODEL)
    return w


def _pack_small(d):
    small = jnp.concatenate([d[n].reshape(r, -1) for n, r in SHARD_SMALL], axis=0)
    if small.shape[1] == D_MODEL:
        small = small.reshape(35, 4, 256).transpose(1, 0, 2)
    small = small.reshape(-1, 35 * 256)
    small = jnp.pad(small, ((0, 0), (0, ROWS_SMALL * D_MODEL - 35 * 256)))
    return small.reshape(-1, ROWS_SMALL, D_MODEL)


def _unpack_small(buf):
    small = buf.reshape(-1)[:35 * 256].reshape(35, 256)
    out = {}
    off = 0
    for n, r in SHARD_SMALL:
        out[n] = small[off:off + r].reshape((r, 256) if r > 1 else (256,))
        off += r
    return out


def _pack_grads(g):
    parts = [g["even_w_in"].reshape(4, -1, D_MODEL), _cols_split(g["ssm_w_glu"], 256),
             g["even_w_out"].reshape(4, -1, D_MODEL), g["odd_w_in"].reshape(4, -1, D_MODEL),
             g["odd_w_out"].reshape(4, -1, D_MODEL), _pack_small(g),
             jnp.zeros((4, ROWS_PACK - ROWS_BIG - ROWS_SMALL, D_MODEL), F32)]
    return jnp.concatenate(parts, axis=1)


def kernel(x, even_norm, even_w_in, pool_w, pool_scale, ssm_log_dt, ssm_a_re, ssm_a_im, ssm_b_re, ssm_b_im, ssm_c_re, ssm_c_im, ssm_d, ssm_w_glu, even_w_out, odd_norm, odd_w_in, conv_w, conv_b, conv_ln_g, conv_ln_b, odd_w_out, final_norm, loss_target, m_even_norm, m_even_w_in, m_pool_w, m_pool_scale, m_ssm_log_dt, m_ssm_a_re, m_ssm_a_im, m_ssm_b_re, m_ssm_b_im, m_ssm_c_re, m_ssm_c_im, m_ssm_d, m_ssm_w_glu, m_even_w_out, m_odd_norm, m_odd_w_in, m_conv_w, m_conv_b, m_conv_ln_g, m_conv_ln_b, m_odd_w_out, m_final_norm, v_even_norm, v_even_w_in, v_pool_w, v_pool_scale, v_ssm_log_dt, v_ssm_a_re, v_ssm_a_im, v_ssm_b_re, v_ssm_b_im, v_ssm_c_re, v_ssm_c_im, v_ssm_d, v_ssm_w_glu, v_even_w_out, v_odd_norm, v_odd_w_in, v_conv_w, v_conv_b, v_conv_ln_g, v_conv_ln_b, v_odd_w_out, v_final_norm):
    ws = dict(zip(WEIGHT_NAMES, (even_norm, even_w_in, pool_w, pool_scale, ssm_log_dt, ssm_a_re, ssm_a_im, ssm_b_re,
                                 ssm_b_im, ssm_c_re, ssm_c_im, ssm_d, ssm_w_glu, even_w_out, odd_norm, odd_w_in,
                                 conv_w, conv_b, conv_ln_g, conv_ln_b, odd_w_out, final_norm)))
    ms = dict(zip(WEIGHT_NAMES, (m_even_norm, m_even_w_in, m_pool_w, m_pool_scale, m_ssm_log_dt, m_ssm_a_re,
                                 m_ssm_a_im, m_ssm_b_re, m_ssm_b_im, m_ssm_c_re, m_ssm_c_im, m_ssm_d, m_ssm_w_glu,
                                 m_even_w_out, m_odd_norm, m_odd_w_in, m_conv_w, m_conv_b, m_conv_ln_g, m_conv_ln_b,
                                 m_odd_w_out, m_final_norm)))
    vs = dict(zip(WEIGHT_NAMES, (v_even_norm, v_even_w_in, v_pool_w, v_pool_scale, v_ssm_log_dt, v_ssm_a_re,
                                 v_ssm_a_im, v_ssm_b_re, v_ssm_b_im, v_ssm_c_re, v_ssm_c_im, v_ssm_d, v_ssm_w_glu,
                                 v_even_w_out, v_odd_norm, v_odd_w_in, v_conv_w, v_conv_b, v_conv_ln_g, v_conv_ln_b,
                                 v_odd_w_out, v_final_norm)))
    lead = {n: a.shape for n, a in ws.items()}
    drop = lambda d: {n: (a[0] if n != "final_norm" else a) for n, a in d.items()}
    ws, ms, vs = drop(ws), drop(ms), drop(vs)

    shard = {n: ws[n] for n in SHARDED}
    rep = {n: ws[n] for n, _ in REP_NAMES}
    w_full = _full_weights(shard, rep)
    loss_part, grad_x, grads = _local_step(x[0], loss_target[0], w_full)
    loss = lax.psum(loss_part, ("x", "y", "c"))

    g_pack = _pack_grads(grads)
    got = _chip_scatter(_pair_add(g_pack, _pair_split(g_pack), BF16))
    g_mine = _sum_lead(got, "chip_sum")
    g_theirs = _pair_join(g_mine)
    outs = [{}, {}, {}, {}]
    row0 = 0
    for n, shp in SHARD_BIG:
        rows = shp[0] * shp[1] // D_MODEL
        view = lambda a: a.reshape(rows, D_MODEL)
        res = _adamw_rows(view(ws[n]), view(ms[n]), view(vs[n]), g_mine, g_theirs, row0, 128, "adamw_" + n)
        for o, r in zip(outs, res):
            o[n] = r
        row0 += rows
    small = lambda d: _pack_small({n: d[n] for n, _ in SHARD_SMALL})[0]
    res = _adamw_rows(small(ws), small(ms), small(vs), g_mine, g_theirs, ROWS_BIG, ROWS_SMALL, "adamw_small")
    for o, r in zip(outs, res):
        o.update(_unpack_small(r))
    g_rep = _sum_lead(_gather_all(_pack_rep({n: grads[n] for n, _ in REP_NAMES})), "rep_sum")
    res = _adamw(_pack_rep(rep), g_rep, _pack_rep({n: ms[n] for n, _ in REP_NAMES}),
                 _pack_rep({n: vs[n] for n, _ in REP_NAMES}), "adamw_rep")
    for o, r in zip(outs, (g_rep,) + tuple(res)):
        o.update(_unpack_rep(r))

    leaves = [[o[n].reshape(lead[n]) for n in WEIGHT_NAMES] for o in outs]
    return (loss, grad_x[None], *leaves[0], *leaves[1], *leaves[2], *leaves[3])
```

```python
import functools

import jax
import jax.numpy as jnp
from jax import lax
from jax.experimental import pallas as pl
from jax.experimental.pallas import tpu as pltpu

F32 = jnp.float32
BF16 = jnp.bfloat16
MESH = pl.DeviceIdType.MESH

D_MODEL = 1024
RMS_EPS = 1e-6
LN_EPS = 1e-5
N_GROUPS = 32
GROUP_DIM = 16
N_STATE = 64
STATES = N_GROUPS * N_STATE
SSM_W = 512
POOL_W = 512
CONV_K = 31
HALO = 32
POOL_HALO = 16

ADAM_LR = 0.001
ADAM_B1 = 0.9
ADAM_B2 = 0.999
ADAM_EPS = 1e-08
ADAM_WD = 0.01
ADAM_STEP = 10

TM = 256
TM_MM = 512
SUB = 8
LCH = 512
SCAN_L = 1024
VMEM_LIMIT = 56 * 1024 * 1024

ROWS_BIG = 1920
ROWS_SMALL = 16
ROWS_PACK = 2048
REP_ROWS = 200
COMM_CHUNKS = 4


def _params(n_axes=1):
    return pltpu.CompilerParams(dimension_semantics=("arbitrary",) * n_axes, vmem_limit_bytes=VMEM_LIMIT)


def _rows(w, cb=0, rev=None, tm=TM):
    if rev is None:
        return pl.BlockSpec((tm, w), lambda i: (i, cb))
    return pl.BlockSpec((tm, w), lambda i: (rev - 1 - i, cb))


def _mm_rows(w, cb=0):
    return _rows(w, cb, tm=TM_MM)


def _full(shape):
    n = len(shape)
    return pl.BlockSpec(shape, lambda i: (0,) * n)


def _prev(hr, w, cb=0, tm=TM):
    r = tm // hr
    return pl.BlockSpec((hr, w), lambda i: (jnp.maximum(i * r - 1, 0), cb))


def _next(hr, w, nrows, cb=0, tm=TM):
    r = tm // hr
    last = nrows // hr - 1
    return pl.BlockSpec((hr, w), lambda i: (jnp.minimum((i + 1) * r, last), cb))


def _dot(a, b):
    return jnp.dot(a, b, preferred_element_type=F32)


def _dot_nt(a, b):
    return lax.dot_general(a, b, (((1,), (1,)), ((), ())), preferred_element_type=F32)


def _dot_tn(a, b):
    return lax.dot_general(a, b, (((0,), (0,)), ((), ())), preferred_element_type=F32)


def _sig(x):
    return 1.0 / (1.0 + jnp.exp(-x))


def _zero_at_first(i, *refs):
    @pl.when(i == 0)
    def _():
        for r in refs:
            r[...] = jnp.zeros_like(r)


def _norm_in(x, g, w, name):
    t, ns = x.shape[0], w.shape[2]
    n = 4 * ns

    def body(x_ref, g_ref, w_ref, o_ref):
        xv = x_ref[...]
        r = lax.rsqrt(jnp.mean(xv * xv, axis=-1, keepdims=True) + RMS_EPS)
        h = (xv * r * g_ref[...]).astype(BF16)
        for s in range(4):
            o_ref[:, s * ns:(s + 1) * ns] = _dot(h, w_ref[s]).astype(BF16)

    return pl.pallas_call(
        body, name=name, grid=(t // TM_MM,),
        in_specs=[_mm_rows(D_MODEL), _full((1, D_MODEL)), _full(w.shape)],
        out_specs=_mm_rows(n), out_shape=jax.ShapeDtypeStruct((t, n), BF16),
        compiler_params=_params())(x, g, w)


def _pool_sums(ext, g, forward):
    n = ext.shape[0]
    s = ext
    for step in range(g + 1):
        k = 1 << step
        s = s + pltpu.roll(s, k if forward else n - k, 0)
    return s


def _pool_fwd(proj, wp, ps):
    t = proj.shape[0]

    def body(u_ref, h_ref, wp_ref, ps_ref, y_ref):
        i = pl.program_id(0)
        pos = (i * TM + 1 + lax.broadcasted_iota(jnp.int32, (TM, 1), 0)).astype(F32)
        for g in range(4):
            sl = slice(128 * g, 128 * (g + 1))
            u = u_ref[:, sl].astype(F32)
            halo = jnp.where(i == 0, 0.0, h_ref[:, sl].astype(F32))
            s = _pool_sums(jnp.concatenate([halo, u], axis=0), g, True)[POOL_HALO:, :]
            pooled = s / jnp.minimum(pos, float(2 << g)) - u
            y_ref[:, sl] = _dot(pooled.astype(BF16), wp_ref[g]) * ps_ref[:, sl]

    return pl.pallas_call(
        body, name="pool_fwd", grid=(t // TM,),
        in_specs=[_rows(POOL_W, 0), _prev(POOL_HALO, POOL_W, 0), _full((4, 128, 128)), _full((1, POOL_W))],
        out_specs=_rows(POOL_W), out_shape=jax.ShapeDtypeStruct((t, POOL_W), F32),
        compiler_params=_params())(proj, proj, wp, ps)


SEG_LEN = TM // SUB


def _perm_matrix():
    p = jnp.arange(TM)
    src = (p % SUB) * SEG_LEN + p // SUB
    return (src[:, None] == jnp.arange(TM)[None, :]).astype(BF16)


def _cmul_add(are, aim, vre, vim, bre, bim):
    return are * vre - aim * vim + bre, are * vim + aim * vre + bim


def _segment_chain(ere, eim, qre, qim, cin_re, cin_im, row, up):
    for sh in (1, 2, 4):
        mre, mim = (qre[SUB - sh:SUB - sh + 1, :], qim[SUB - sh:SUB - sh + 1, :]) if up else \
                   (qre[sh - 1:sh, :], qim[sh - 1:sh, :])
        keep = (row < SUB - sh) if up else (row >= sh)
        sre = jnp.where(keep, pltpu.roll(ere, SUB - sh if up else sh, 0), 0.0)
        sim = jnp.where(keep, pltpu.roll(eim, SUB - sh if up else sh, 0), 0.0)
        ere, eim = _cmul_add(mre, mim, sre, sim, ere, eim)
    ere, eim = _cmul_add(qre, qim, cin_re, cin_im, ere, eim)
    keep = (row < SUB - 1) if up else (row >= 1)
    ent_re = jnp.where(keep, pltpu.roll(ere, SUB - 1 if up else 1, 0), cin_re)
    ent_im = jnp.where(keep, pltpu.roll(eim, SUB - 1 if up else 1, 0), cin_im)
    return ere, eim, ent_re, ent_im


def _scan_fwd_block(xs_re, xs_im, p8_re, p8_im, q_re, q_im, car_re, car_im, ent_re_ref, ent_im_ref):
    row = lax.broadcasted_iota(jnp.int32, (SUB, SCAN_L), 0)
    for j in range(STATES // SCAN_L):
        sl = slice(SCAN_L * j, SCAN_L * (j + 1))
        are, aim = p8_re[0:SUB, sl], p8_im[0:SUB, sl]

        def totals(i, v, sl=sl, are=are, aim=aim):
            r0 = pl.multiple_of(i * SUB, SUB)
            vre, vim = _cmul_add(are, aim, v[0], v[1], xs_re[pl.ds(r0, SUB), sl], xs_im[pl.ds(r0, SUB), sl])
            xs_re[pl.ds(r0, SUB), sl] = vre
            xs_im[pl.ds(r0, SUB), sl] = vim
            return vre, vim

        ere, eim = lax.fori_loop(1, SEG_LEN, totals, (xs_re[0:SUB, sl], xs_im[0:SUB, sl]), unroll=2)
        ere, eim, cre, cim = _segment_chain(ere, eim, q_re[:, sl], q_im[:, sl],
                                            car_re[:, sl], car_im[:, sl], row, False)
        car_re[:, sl] = jnp.broadcast_to(ere[SUB - 1:SUB, :], (SUB, SCAN_L))
        car_im[:, sl] = jnp.broadcast_to(eim[SUB - 1:SUB, :], (SUB, SCAN_L))
        if ent_re_ref is not None:
            ent_re_ref[:, sl] = cre
            ent_im_ref[:, sl] = cim

        def fix(i, c, sl=sl, cre=cre, cim=cim):
            r0 = pl.multiple_of(i * SUB, SUB)
            vre, vim = _cmul_add(p8_re[pl.ds(r0, SUB), sl], p8_im[pl.ds(r0, SUB), sl], cre, cim,
                                 xs_re[pl.ds(r0, SUB), sl], xs_im[pl.ds(r0, SUB), sl])
            xs_re[pl.ds(r0, SUB), sl] = vre
            xs_im[pl.ds(r0, SUB), sl] = vim
            return c

        lax.fori_loop(0, SEG_LEN, fix, 0, unroll=2)


def _unpermute(pmt_ref, v):
    hi = v.astype(BF16)
    lo = (v - hi.astype(F32)).astype(BF16)
    return _dot(pmt_ref[...], hi) + _dot(pmt_ref[...], lo)


def _ssm_fwd(proj, pm, pmt, mb_re, mb_im, p8_re, p8_im, q_re, q_im, cm_re, cm_im, dskip, wglu):
    t = proj.shape[0]
    nblk = t // TM

    def body(u_ref, pm_ref, pmt_ref, mbre, mbim, p8re, p8im, qre, qim, cmre, cmim, d_ref, wg_ref,
             y_ref, cre_ref, cim_ref, xs_re, xs_im, car_re, car_im, ysk):
        i = pl.program_id(0)
        _zero_at_first(i, car_re, car_im)
        cre_ref[0] = car_re[...]
        cim_ref[0] = car_im[...]
        us = _dot(pm_ref[...], u_ref[...])
        usb = us.astype(BF16)
        for j in range(4):
            xs_re[:, LCH * j:LCH * (j + 1)] = _dot(usb[:, 128 * j:128 * (j + 1)], mbre[j])
            xs_im[:, LCH * j:LCH * (j + 1)] = _dot(usb[:, 128 * j:128 * (j + 1)], mbim[j])
        _scan_fwd_block(xs_re, xs_im, p8re, p8im, qre, qim, car_re, car_im, None, None)
        for j in range(4):
            sl = slice(LCH * j, LCH * (j + 1))
            ysk[:, 128 * j:128 * (j + 1)] = (_dot_nt(xs_re[:, sl].astype(BF16), cmre[j])
                                             - _dot_nt(xs_im[:, sl].astype(BF16), cmim[j]))
        yv = ysk[...] + d_ref[...] * us
        gv = _dot(yv.astype(BF16), wg_ref[...])
        y_ref[...] = _unpermute(pmt_ref, gv[:, :SSM_W] * _sig(gv[:, SSM_W:]))

    blk = (4, 128, LCH)
    return pl.pallas_call(
        body, name="ssm_fwd", grid=(nblk,),
        in_specs=[_rows(SSM_W, 1), _full((TM, TM)), _full((TM, TM)), _full(blk), _full(blk),
                  _full((TM, STATES)), _full((TM, STATES)), _full((SUB, STATES)), _full((SUB, STATES)),
                  _full(blk), _full(blk), _full((1, SSM_W)), _full((SSM_W, 2 * SSM_W))],
        out_specs=[_rows(SSM_W), pl.BlockSpec((1, SUB, STATES), lambda i: (i, 0, 0)),
                   pl.BlockSpec((1, SUB, STATES), lambda i: (i, 0, 0))],
        out_shape=[jax.ShapeDtypeStruct((t, SSM_W), F32), jax.ShapeDtypeStruct((nblk, SUB, STATES), F32),
                   jax.ShapeDtypeStruct((nblk, SUB, STATES), F32)],
        scratch_shapes=[pltpu.VMEM((TM, STATES), F32), pltpu.VMEM((TM, STATES), F32),
                        pltpu.VMEM((SUB, STATES), F32), pltpu.VMEM((SUB, STATES), F32),
                        pltpu.VMEM((TM, SSM_W), F32)],
        compiler_params=_params())(proj, pm, pmt, mb_re, mb_im, p8_re, p8_im, q_re, q_im, cm_re, cm_im, dskip, wglu)


def _even_out(yp, ys, proj, x, w):
    t = x.shape[0]

    def body(yp_ref, ys_ref, z_ref, x_ref, w_ref, x1_ref, yg_ref):
        z = z_ref[...].astype(F32)
        sz = z * _sig(z)
        gp = (yp_ref[...] * sz[:, :POOL_W]).astype(BF16)
        gs = (ys_ref[...] * sz[:, POOL_W:]).astype(BF16)
        yg_ref[:, :POOL_W] = gp
        yg_ref[:, POOL_W:] = gs
        x1_ref[...] = x_ref[...] + _dot(gp, w_ref[:POOL_W, :]) + _dot(gs, w_ref[POOL_W:, :])

    return pl.pallas_call(
        body, name="even_out", grid=(t // TM_MM,),
        in_specs=[_mm_rows(POOL_W), _mm_rows(SSM_W), _mm_rows(D_MODEL, 1), _mm_rows(D_MODEL),
                  _full((D_MODEL, D_MODEL))],
        out_specs=[_mm_rows(D_MODEL), _mm_rows(D_MODEL)],
        out_shape=[jax.ShapeDtypeStruct((t, D_MODEL), F32), jax.ShapeDtypeStruct((t, D_MODEL), BF16)],
        compiler_params=_params())(yp, ys, proj, x, w)


def _phase_copies(ext, cp):
    n = cp.shape[1]
    for j in range(1, SUB):
        cp[j - 1] = ext[pl.ds(j, n), :]


def _shifted(ext, cp, off, nrows, sl, row0=0):
    q, j = divmod(off, SUB)
    if j == 0:
        return ext[pl.ds(row0 + SUB * q, nrows), sl]
    return cp[j - 1, pl.ds(row0 + SUB * q, nrows), sl]


def _conv_taps(ext, cp, w_ref, first, nrows, sl, init, row0=0):
    acc = init
    for k in range(CONV_K):
        acc = acc + w_ref[k:k + 1, sl] * _shifted(ext, cp, first(k), nrows, sl, row0)
    return acc


def _conv_fwd(q, cw, cb, lg, lb):
    t = q.shape[0]

    def body(v_ref, g_ref, hv_ref, hg_ref, z_ref, w_ref, b_ref, lg_ref, lb_ref, y_ref, cv_ref, ext, cp):
        i = pl.program_id(0)
        ext[0:HALO, :] = jnp.where(i == 0, 0.0, hv_ref[...].astype(F32) * _sig(hg_ref[...].astype(F32)))
        ext[HALO:, :] = v_ref[...].astype(F32) * _sig(g_ref[...].astype(F32))
        _phase_copies(ext, cp)

        def lanes(c, carry):
            sl = pl.ds(pl.multiple_of(c * 128, 128), 128)
            cv_ref[:, sl] = _conv_taps(ext, cp, w_ref, lambda k: k + 2, TM, sl,
                                       jnp.broadcast_to(b_ref[:, sl], (TM, 128)))
            return carry

        lax.fori_loop(0, D_MODEL // 128, lanes, 0)
        cv = cv_ref[...]
        cc = cv - jnp.mean(cv, axis=-1, keepdims=True)
        rstd = lax.rsqrt(jnp.mean(cc * cc, axis=-1, keepdims=True) + LN_EPS)
        cl = cc * rstd * lg_ref[...] + lb_ref[...]
        z = z_ref[...].astype(F32)
        y_ref[...] = (cl * _sig(cl) * z * _sig(z)).astype(BF16)

    vec = _full((1, D_MODEL))
    return pl.pallas_call(
        body, name="conv_fwd", grid=(t // TM,),
        in_specs=[_rows(D_MODEL, 0), _rows(D_MODEL, 1), _prev(HALO, D_MODEL, 0), _prev(HALO, D_MODEL, 1),
                  _rows(D_MODEL, 2), _full((HALO, D_MODEL)), vec, vec, vec],
        out_specs=[_rows(D_MODEL), _rows(D_MODEL)],
        out_shape=[jax.ShapeDtypeStruct((t, D_MODEL), BF16), jax.ShapeDtypeStruct((t, D_MODEL), F32)],
        scratch_shapes=[pltpu.VMEM((TM + HALO, D_MODEL), F32),
                        pltpu.VMEM((SUB - 1, TM + HALO - SUB, D_MODEL), F32)],
        compiler_params=_params())(q, q, q, q, q, cw, cb, lg, lb)


def _odd_out_loss(y2, x1, w, gf, tgt):
    t = x1.shape[0]

    def body(y_ref, x_ref, w_ref, g_ref, t_ref, dx_ref, loss_ref, dg_ref):
        i = pl.program_id(0)
        _zero_at_first(i, loss_ref, dg_ref)
        x2 = x_ref[...] + _dot(y_ref[...], w_ref[...])
        r = lax.rsqrt(jnp.mean(x2 * x2, axis=-1, keepdims=True) + RMS_EPS)
        n = x2 * r
        e = n * g_ref[...] - t_ref[...]
        loss_ref[...] += jnp.sum(e * e, axis=0, keepdims=True) * (0.5 / D_MODEL)
        dout = e * (1.0 / D_MODEL)
        dg_ref[...] += jnp.sum(dout * n, axis=0, keepdims=True)
        dn = dout * g_ref[...]
        dx_ref[...] = r * (dn - n * jnp.mean(dn * n, axis=-1, keepdims=True))

    vec = _full((1, D_MODEL))
    return pl.pallas_call(
        body, name="odd_out_loss", grid=(t // TM_MM,),
        in_specs=[_mm_rows(D_MODEL), _mm_rows(D_MODEL), _full((D_MODEL, D_MODEL)), vec, _mm_rows(D_MODEL)],
        out_specs=[_mm_rows(D_MODEL), vec, vec],
        out_shape=[jax.ShapeDtypeStruct((t, D_MODEL), F32), jax.ShapeDtypeStruct((1, D_MODEL), F32),
                   jax.ShapeDtypeStruct((1, D_MODEL), F32)],
        compiler_params=_params())(y2, x1, w, gf, tgt)


def _dsilu(z):
    s = _sig(z)
    return z * s, s * (1.0 + z * (1.0 - s))


def _odd_bwd_out(dx2, w, y2, cv, q, lg, lb):
    t = dx2.shape[0]

    def body(dx_ref, w_ref, y_ref, cv_ref, z_ref, lg_ref, lb_ref, dcv_ref, dz_ref, dw_ref, dlg_ref, dlb_ref):
        i = pl.program_id(0)
        _zero_at_first(i, dw_ref, dlg_ref, dlb_ref)
        dxb = dx_ref[...].astype(BF16)
        dy = _dot_nt(dxb, w_ref[...])
        dw_ref[...] += _dot_tn(y_ref[...], dxb)
        cv = cv_ref[...]
        cc = cv - jnp.mean(cv, axis=-1, keepdims=True)
        rstd = lax.rsqrt(jnp.mean(cc * cc, axis=-1, keepdims=True) + LN_EPS)
        cn = cc * rstd
        silu_c, dsilu_c = _dsilu(cn * lg_ref[...] + lb_ref[...])
        silu_z, dsilu_z = _dsilu(z_ref[...].astype(F32))
        dcl = dy * silu_z * dsilu_c
        dz_ref[...] = (dy * silu_c * dsilu_z).astype(BF16)
        dlg_ref[...] += jnp.sum(dcl * cn, axis=0, keepdims=True)
        dlb_ref[...] += jnp.sum(dcl, axis=0, keepdims=True)
        dcn = dcl * lg_ref[...]
        dcv_ref[...] = rstd * (dcn - jnp.mean(dcn, axis=-1, keepdims=True)
                               - cn * jnp.mean(dcn * cn, axis=-1, keepdims=True))

    vec = _full((1, D_MODEL))
    mat = _full((D_MODEL, D_MODEL))
    return pl.pallas_call(
        body, name="odd_bwd_out", grid=(t // TM_MM,),
        in_specs=[_mm_rows(D_MODEL), mat, _mm_rows(D_MODEL), _mm_rows(D_MODEL), _mm_rows(D_MODEL, 2), vec, vec],
        out_specs=[_mm_rows(D_MODEL), _mm_rows(D_MODEL), mat, vec, vec],
        out_shape=[jax.ShapeDtypeStruct((t, D_MODEL), F32), jax.ShapeDtypeStruct((t, D_MODEL), BF16),
                   jax.ShapeDtypeStruct((D_MODEL, D_MODEL), F32), jax.ShapeDtypeStruct((1, D_MODEL), F32),
                   jax.ShapeDtypeStruct((1, D_MODEL), F32)],
        compiler_params=_params())(dx2, w, y2, cv, q, lg, lb)


def _conv_bwd(dcv, q, cw):
    t = dcv.shape[0]
    nblk = t // TM

    def body(d_ref, dn_ref, v_ref, g_ref, hv_ref, hg_ref, w_ref,
             dv_ref, dgt_ref, dw_ref, db_ref, gext, dext, dgl, gcp, dcp):
        i = pl.program_id(0)
        last = nblk - 1
        _zero_at_first(i, dw_ref, db_ref)
        v = v_ref[...].astype(F32)
        sg = _sig(g_ref[...].astype(F32))
        gext[0:HALO, :] = jnp.where(i == 0, 0.0, hv_ref[...].astype(F32) * _sig(hg_ref[...].astype(F32)))
        gext[HALO:, :] = v * sg
        d = d_ref[...]
        dext[0:TM, :] = d
        dext[TM:, :] = jnp.where(i == last, 0.0, dn_ref[...])
        _phase_copies(gext, gcp)
        _phase_copies(dext, dcp)
        db_ref[...] += jnp.sum(d, axis=0, keepdims=True)
        def lanes(c, carry):
            sl = pl.ds(pl.multiple_of(c * 128, 128), 128)
            dgl[:, sl] = _conv_taps(dext, dcp, w_ref, lambda k: 30 - k, TM, sl, jnp.zeros((TM, 128), F32))
            return carry

        def lanes_w(c, carry):
            sl = pl.ds(pl.multiple_of(c * 128, 128), 128)
            ntile = TM // SUB
            dts = [d_ref[SUB * r:SUB * (r + 1), sl] for r in range(ntile)]
            for j in range(SUB):
                taps = [(q, SUB * q + j - 2) for q in range(5) if 0 <= SUB * q + j - 2 < CONV_K]
                sums = {k: None for _, k in taps}
                for rt in range(ntile + 4):
                    need = [(q, k) for q, k in taps if 0 <= rt - q < ntile]
                    if not need:
                        continue
                    src = gext[SUB * rt:SUB * (rt + 1), sl] if j == 0 else gcp[j - 1, SUB * rt:SUB * (rt + 1), sl]
                    for q, k in need:
                        prod = dts[rt - q] * src
                        sums[k] = prod if sums[k] is None else sums[k] + prod
                for _, k in taps:
                    dw_ref[SUB * k:SUB * (k + 1), sl] += sums[k]
            return carry

        lax.fori_loop(0, D_MODEL // 128, lanes, 0)
        lax.fori_loop(0, D_MODEL // 128, lanes_w, 0)
        dg = dgl[...]
        dv_ref[...] = (dg * sg).astype(BF16)
        dgt_ref[...] = (dg * v * sg * (1.0 - sg)).astype(BF16)

    return pl.pallas_call(
        body, name="conv_bwd", grid=(t // TM,),
        in_specs=[_rows(D_MODEL), _next(HALO, D_MODEL, t), _rows(D_MODEL, 0), _rows(D_MODEL, 1),
                  _prev(HALO, D_MODEL, 0), _prev(HALO, D_MODEL, 1), _full((HALO, D_MODEL))],
        out_specs=[_rows(D_MODEL), _rows(D_MODEL), _full((HALO * SUB, D_MODEL)), _full((1, D_MODEL))],
        out_shape=[jax.ShapeDtypeStruct((t, D_MODEL), BF16), jax.ShapeDtypeStruct((t, D_MODEL), BF16),
                   jax.ShapeDtypeStruct((HALO * SUB, D_MODEL), F32), jax.ShapeDtypeStruct((1, D_MODEL), F32)],
        scratch_shapes=[pltpu.VMEM((TM + HALO, D_MODEL), F32), pltpu.VMEM((TM + HALO, D_MODEL), F32),
                        pltpu.VMEM((TM, D_MODEL), F32),
                        pltpu.VMEM((SUB - 1, TM + HALO - SUB, D_MODEL), F32),
                        pltpu.VMEM((SUB - 1, TM + HALO - SUB, D_MODEL), F32)],
        compiler_params=_params())(dcv, dcv, q, q, q, q, cw)


def _column_segments(widths, ns):
    segs = []
    col = 0
    for p, wd in enumerate(widths):
        a = 0
        while a < wd:
            s, lo = divmod(col + a, ns)
            ln = min(wd - a, ns - lo)
            segs.append((p, a, a + ln, s, lo, lo + ln))
            a += ln
        col += wd
    return segs


def _in_bwd(dparts, w, x, g, dres, name):
    t = x.shape[0]
    widths = [p.shape[1] for p in dparts]
    npart = len(dparts)
    segs = _column_segments(widths, w.shape[2])

    def body(*refs):
        d_refs = refs[:npart]
        w_ref, x_ref, g_ref, r_ref, dx_ref, dg_ref, dw_ref = refs[npart:]
        i = pl.program_id(0)
        _zero_at_first(i, dg_ref, dw_ref)
        xv = x_ref[...]
        r = lax.rsqrt(jnp.mean(xv * xv, axis=-1, keepdims=True) + RMS_EPS)
        n = xv * r
        h = (n * g_ref[...]).astype(BF16)
        dh = None
        for p, lo, hi, s, slo, shi in segs:
            d = d_refs[p][:, lo:hi]
            part = _dot_nt(d, w_ref[s, :, slo:shi])
            dh = part if dh is None else dh + part
            dw_ref[s, :, slo:shi] += _dot_tn(h, d)
        dg_ref[...] += jnp.sum(dh * n, axis=0, keepdims=True)
        dn = dh * g_ref[...]
        dx_ref[...] = r_ref[...] + r * (dn - n * jnp.mean(dn * n, axis=-1, keepdims=True))

    vec = _full((1, D_MODEL))
    once = pl.BlockSpec(w.shape, lambda i: (0, 0, 0), pipeline_mode=pl.Buffered(1))
    return pl.pallas_call(
        body, name=name, grid=(t // TM_MM,),
        in_specs=[_mm_rows(wd) for wd in widths] + [once, _mm_rows(D_MODEL), vec, _mm_rows(D_MODEL)],
        out_specs=[_mm_rows(D_MODEL), vec, once],
        out_shape=[jax.ShapeDtypeStruct((t, D_MODEL), F32), jax.ShapeDtypeStruct((1, D_MODEL), F32),
                   jax.ShapeDtypeStruct(w.shape, F32)],
        compiler_params=_params())(*dparts, w, x, g, dres)


def _even_bwd_out(dx1, w, yg, yp, ys, proj):
    t = dx1.shape[0]

    def body(dx_ref, w_ref, yg_ref, yp_ref, ys_ref, z_ref, dy_ref, dz_ref, dw_ref):
        i = pl.program_id(0)
        _zero_at_first(i, dw_ref)
        dxb = dx_ref[...].astype(BF16)
        dyg = _dot_nt(dxb, w_ref[...])
        dw_ref[...] += _dot_tn(yg_ref[...], dxb)
        silu_z, dsilu_z = _dsilu(z_ref[...].astype(F32))
        dy_ref[...] = (dyg * silu_z).astype(BF16)
        dz_ref[:, :POOL_W] = (dyg[:, :POOL_W] * yp_ref[...] * dsilu_z[:, :POOL_W]).astype(BF16)
        dz_ref[:, POOL_W:] = (dyg[:, POOL_W:] * ys_ref[...] * dsilu_z[:, POOL_W:]).astype(BF16)

    mat = _full((D_MODEL, D_MODEL))
    return pl.pallas_call(
        body, name="even_bwd_out", grid=(t // TM_MM,),
        in_specs=[_mm_rows(D_MODEL), mat, _mm_rows(D_MODEL), _mm_rows(POOL_W), _mm_rows(SSM_W), _mm_rows(D_MODEL, 1)],
        out_specs=[_mm_rows(D_MODEL), _mm_rows(D_MODEL), mat],
        out_shape=[jax.ShapeDtypeStruct((t, D_MODEL), BF16), jax.ShapeDtypeStruct((t, D_MODEL), BF16),
                   jax.ShapeDtypeStruct((D_MODEL, D_MODEL), F32)],
        compiler_params=_params())(dx1, w, yg, yp, ys, proj)


def _pool_bwd(dycat, proj, wp, ps):
    t = proj.shape[0]

    def body(dy_ref, dyn_ref, u_ref, h_ref, wp_ref, ps_ref, du_ref, dwp_ref, dps_ref):
        i = pl.program_id(0)
        last = t // TM - 1
        _zero_at_first(i, dwp_ref, dps_ref)
        pos = (i * TM + 1 + lax.broadcasted_iota(jnp.int32, (TM, 1), 0)).astype(F32)
        pos_ext = (i * TM + 1 + lax.broadcasted_iota(jnp.int32, (TM + POOL_HALO, 1), 0)).astype(F32)
        for g in range(4):
            sl = slice(128 * g, 128 * (g + 1))
            w = float(2 << g)
            u = u_ref[:, sl].astype(F32)
            halo = jnp.where(i == 0, 0.0, h_ref[:, sl].astype(F32))
            s = _pool_sums(jnp.concatenate([halo, u], axis=0), g, True)[POOL_HALO:, :]
            pooled = (s / jnp.minimum(pos, w) - u).astype(BF16)
            dy = dy_ref[:, sl].astype(F32)
            dps_ref[:, sl] += jnp.sum(dy * _dot(pooled, wp_ref[g]), axis=0, keepdims=True)
            dy_ext = jnp.concatenate([dy, jnp.where(i == last, 0.0, dyn_ref[:, sl].astype(F32))], axis=0)
            dmix = (dy_ext * ps_ref[:, sl]).astype(BF16)
            dwp_ref[g] += _dot_tn(pooled, dmix[:TM, :])
            dpool = _dot_nt(dmix, wp_ref[g])
            lead = _pool_sums(dpool / jnp.minimum(pos_ext, w), g, False)
            du_ref[:, sl] = (lead[:TM, :] - dpool[:TM, :]).astype(BF16)

    return pl.pallas_call(
        body, name="pool_bwd", grid=(t // TM,),
        in_specs=[_rows(POOL_W, 0), _next(POOL_HALO, POOL_W, t, 0), _rows(POOL_W, 0), _prev(POOL_HALO, POOL_W, 0),
                  _full((4, 128, 128)), _full((1, POOL_W))],
        out_specs=[_rows(POOL_W), _full((4, 128, 128)), _full((1, POOL_W))],
        out_shape=[jax.ShapeDtypeStruct((t, POOL_W), BF16), jax.ShapeDtypeStruct((4, 128, 128), F32),
                   jax.ShapeDtypeStruct((1, POOL_W), F32)],
        compiler_params=_params())(dycat, dycat, proj, proj, wp, ps)


def _ssm_bwd(dycat, proj, car_in_re, car_in_im, pm, pmt, mb_re, mb_im, p8_re, p8_im, q_re, q_im, qr_re, qr_im,
             cm_re, cm_im, dskip, wglu):
    t = proj.shape[0]
    nblk = t // TM

    def body(dy_ref, u_ref, cin_re, cin_im, pm_ref, pmt_ref, mbre, mbim, p8re, p8im, qre, qim, qrre, qrim,
             cmre, cmim, d_ref, wg_ref,
             du_ref, dmbre, dmbim, dcmre, dcmim, dare, daim, dd_ref, dwg_ref,
             xs_re, xs_im, gs_re, gs_im, car_re, car_im, ent_re, ent_im, gcar_re, gcar_im, ysk, dysk):
        i = pl.program_id(0)
        _zero_at_first(i, dmbre, dmbim, dcmre, dcmim, dare, daim, dd_ref, dwg_ref, gcar_re, gcar_im)
        us = _dot(pm_ref[...], u_ref[...])
        usb = us.astype(BF16)
        for j in range(4):
            xs_re[:, LCH * j:LCH * (j + 1)] = _dot(usb[:, 128 * j:128 * (j + 1)], mbre[j])
            xs_im[:, LCH * j:LCH * (j + 1)] = _dot(usb[:, 128 * j:128 * (j + 1)], mbim[j])
        car_re[...] = cin_re[0]
        car_im[...] = cin_im[0]
        _scan_fwd_block(xs_re, xs_im, p8re, p8im, qre, qim, car_re, car_im, ent_re, ent_im)
        for j in range(4):
            sl = slice(LCH * j, LCH * (j + 1))
            ysk[:, 128 * j:128 * (j + 1)] = (_dot_nt(xs_re[:, sl].astype(BF16), cmre[j])
                                             - _dot_nt(xs_im[:, sl].astype(BF16), cmim[j]))
        yvb = (ysk[...] + d_ref[...] * us).astype(BF16)
        gv = _dot(yvb, wg_ref[...])
        sg = _sig(gv[:, SSM_W:])
        dyss = _dot(pm_ref[...], dy_ref[...])
        dval = (dyss * sg).astype(BF16)
        dgate = (dyss * gv[:, :SSM_W] * sg * (1.0 - sg)).astype(BF16)
        dy = _dot_nt(dval, wg_ref[:, :SSM_W]) + _dot_nt(dgate, wg_ref[:, SSM_W:])
        dwg_ref[:, :SSM_W] += _dot_tn(yvb, dval)
        dwg_ref[:, SSM_W:] += _dot_tn(yvb, dgate)
        dd_ref[...] += jnp.sum(dy * us, axis=0, keepdims=True)
        dysk[...] = dy
        for j in range(4):
            sl = slice(LCH * j, LCH * (j + 1))
            dyj = dy[:, 128 * j:128 * (j + 1)].astype(BF16)
            gs_re[:, sl] = _dot(dyj, cmre[j])
            gs_im[:, sl] = -_dot(dyj, cmim[j])
            dcmre[j] += _dot_tn(dyj, xs_re[:, sl].astype(BF16))
            dcmim[j] -= _dot_tn(dyj, xs_im[:, sl].astype(BF16))
        row = lax.broadcasted_iota(jnp.int32, (SUB, SCAN_L), 0)
        for j in range(STATES // SCAN_L):
            sl = slice(SCAN_L * j, SCAN_L * (j + 1))
            are, aim = p8re[0:SUB, sl], -p8im[0:SUB, sl]

            def totals(k, v, sl=sl, are=are, aim=aim):
                r0 = pl.multiple_of((SEG_LEN - 2 - k) * SUB, SUB)
                vre, vim = _cmul_add(are, aim, v[0], v[1], gs_re[pl.ds(r0, SUB), sl], gs_im[pl.ds(r0, SUB), sl])
                gs_re[pl.ds(r0, SUB), sl] = vre
                gs_im[pl.ds(r0, SUB), sl] = vim
                return vre, vim

            top = (SEG_LEN - 1) * SUB
            fre, fim = lax.fori_loop(0, SEG_LEN - 1, totals,
                                     (gs_re[top:top + SUB, sl], gs_im[top:top + SUB, sl]), unroll=2)
            fre, fim, nre, nim = _segment_chain(fre, fim, qrre[:, sl], -qrim[:, sl],
                                                gcar_re[:, sl], gcar_im[:, sl], row, True)
            gcar_re[:, sl] = jnp.broadcast_to(fre[0:1, :], (SUB, SCAN_L))
            gcar_im[:, sl] = jnp.broadcast_to(fim[0:1, :], (SUB, SCAN_L))

            def fix(i2, acc, sl=sl, nre=nre, nim=nim):
                r0 = pl.multiple_of(i2 * SUB, SUB)
                rb = pl.multiple_of((SEG_LEN - 1 - i2) * SUB, SUB)
                gre, gim = _cmul_add(p8re[pl.ds(rb, SUB), sl], -p8im[pl.ds(rb, SUB), sl], nre, nim,
                                     gs_re[pl.ds(r0, SUB), sl], gs_im[pl.ds(r0, SUB), sl])
                gs_re[pl.ds(r0, SUB), sl] = gre
                gs_im[pl.ds(r0, SUB), sl] = gim
                rp = pl.multiple_of((i2 - 1) * SUB, SUB)
                xre, xim = xs_re[pl.ds(rp, SUB), sl], xs_im[pl.ds(rp, SUB), sl]
                return acc[0] + gre * xre + gim * xim, acc[1] + gim * xre - gre * xim

            g0re, g0im = _cmul_add(p8re[top:top + SUB, sl], -p8im[top:top + SUB, sl], nre, nim,
                                   gs_re[0:SUB, sl], gs_im[0:SUB, sl])
            gs_re[0:SUB, sl] = g0re
            gs_im[0:SUB, sl] = g0im
            ere, eim = ent_re[:, sl], ent_im[:, sl]
            acc0 = (dare[:, sl] + g0re * ere + g0im * eim, daim[:, sl] + g0im * ere - g0re * eim)
            are_acc, aim_acc = lax.fori_loop(1, SEG_LEN, fix, acc0, unroll=2)
            dare[:, sl] = are_acc
            daim[:, sl] = aim_acc
        for j in range(4):
            sl = slice(LCH * j, LCH * (j + 1))
            c4 = slice(128 * j, 128 * (j + 1))
            gre = gs_re[:, sl].astype(BF16)
            gim = gs_im[:, sl].astype(BF16)
            dmbre[j] += _dot_tn(usb[:, c4], gre)
            dmbim[j] += _dot_tn(usb[:, c4], gim)
            dysk[:, c4] = _dot_nt(gre, mbre[j]) + _dot_nt(gim, mbim[j]) + dysk[:, c4] * d_ref[:, c4]
        du_ref[...] = _dot(pmt_ref[...], dysk[...].astype(BF16)).astype(BF16)

    blk = (4, 128, LCH)
    pw = _full((SUB, STATES))
    p8 = _full((TM, STATES))
    perm = _full((TM, TM))
    car = pl.BlockSpec((1, SUB, STATES), lambda i: (nblk - 1 - i, 0, 0))
    big = lambda: pltpu.VMEM((TM, STATES), F32)
    small = lambda: pltpu.VMEM((SUB, STATES), F32)
    return pl.pallas_call(
        body, name="ssm_bwd", grid=(nblk,),
        in_specs=[_rows(SSM_W, 1, rev=nblk), _rows(SSM_W, 1, rev=nblk), car, car, perm, perm, _full(blk), _full(blk),
                  p8, p8, pw, pw, pw, pw, _full(blk), _full(blk), _full((1, SSM_W)), _full((SSM_W, 2 * SSM_W))],
        out_specs=[_rows(SSM_W, 0, rev=nblk), _full(blk), _full(blk), _full(blk), _full(blk), pw, pw,
                   _full((1, SSM_W)), _full((SSM_W, 2 * SSM_W))],
        out_shape=[jax.ShapeDtypeStruct((t, SSM_W), BF16)] + [jax.ShapeDtypeStruct(blk, F32)] * 4
        + [jax.ShapeDtypeStruct((SUB, STATES), F32)] * 2
        + [jax.ShapeDtypeStruct((1, SSM_W), F32), jax.ShapeDtypeStruct((SSM_W, 2 * SSM_W), F32)],
        scratch_shapes=[big(), big(), big(), big(), small(), small(), small(), small(), small(), small(),
                        pltpu.VMEM((TM, SSM_W), F32), pltpu.VMEM((TM, SSM_W), F32)],
        compiler_params=_params())(dycat, proj, car_in_re, car_in_im, pm, pmt, mb_re, mb_im, p8_re, p8_im,
                                   q_re, q_im, qr_re, qr_im, cm_re, cm_im, dskip, wglu)


def _adamw(w, g, m, v, name):
    rows = w.shape[0]
    tr = 256 if rows % 256 == 0 else rows
    c1 = 1.0 / (1.0 - ADAM_B1 ** ADAM_STEP)
    c2 = 1.0 / (1.0 - ADAM_B2 ** ADAM_STEP)

    def body(w_ref, g_ref, m_ref, v_ref, d_ref, nm_ref, nv_ref):
        gv = g_ref[...]
        m = ADAM_B1 * m_ref[...] + (1.0 - ADAM_B1) * gv
        v = ADAM_B2 * v_ref[...] + (1.0 - ADAM_B2) * (gv * gv)
        nm_ref[...] = m
        nv_ref[...] = v
        d_ref[...] = -ADAM_LR * ((m * c1) / (jnp.sqrt(v * c2) + ADAM_EPS) + ADAM_WD * w_ref[...])

    spec = pl.BlockSpec((tr, D_MODEL), lambda i: (i, 0))
    shp = jax.ShapeDtypeStruct((rows, D_MODEL), F32)
    return pl.pallas_call(
        body, name=name, grid=(rows // tr,), in_specs=[spec] * 4, out_specs=[spec] * 3, out_shape=[shp] * 3,
        compiler_params=_params())(w, g, m, v)


def _core_index():
    return lax.axis_index("c").astype(jnp.int32).reshape(1)


def _pair_add(g, theirs, out_dtype):
    n, half, _ = theirs.shape
    nb = half // 256

    def body(c_ref, a_ref, b_ref, o_ref):
        o_ref[...] = (a_ref[...] + b_ref[...]).astype(out_dtype)

    spec = pl.BlockSpec((1, 256, D_MODEL), lambda i, j, c: (i, j, 0))
    grid_spec = pltpu.PrefetchScalarGridSpec(
        num_scalar_prefetch=1, grid=(n, nb),
        in_specs=[pl.BlockSpec((1, 256, D_MODEL), lambda i, j, c: (i, c[0] * nb + j, 0)), spec], out_specs=spec)
    return pl.pallas_call(
        body, name="pair_add", grid_spec=grid_spec, out_shape=jax.ShapeDtypeStruct(theirs.shape, out_dtype),
        compiler_params=_params(2))(_core_index(), g, theirs)


def _adamw_rows(w, m, v, g_mine, g_theirs, row0, br, name):
    rows = w.shape[0]
    b0 = row0 // br
    per_half = g_mine.shape[0] // br
    c1 = 1.0 / (1.0 - ADAM_B1 ** ADAM_STEP)
    c2 = 1.0 / (1.0 - ADAM_B2 ** ADAM_STEP)

    def body(c_ref, w_ref, gm_ref, gt_ref, m_ref, v_ref, g_ref, d_ref, nm_ref, nv_ref):
        gv = jnp.where((b0 + pl.program_id(0)) // per_half == c_ref[0], gm_ref[...], gt_ref[...])
        m = ADAM_B1 * m_ref[...] + (1.0 - ADAM_B1) * gv
        v = ADAM_B2 * v_ref[...] + (1.0 - ADAM_B2) * (gv * gv)
        g_ref[...] = gv
        nm_ref[...] = m
        nv_ref[...] = v
        d_ref[...] = -ADAM_LR * ((m * c1) / (jnp.sqrt(v * c2) + ADAM_EPS) + ADAM_WD * w_ref[...])

    spec = pl.BlockSpec((br, D_MODEL), lambda i, c: (i, 0))
    part = pl.BlockSpec((br, D_MODEL), lambda i, c: ((b0 + i) % per_half, 0))
    shp = jax.ShapeDtypeStruct((rows, D_MODEL), F32)
    grid_spec = pltpu.PrefetchScalarGridSpec(
        num_scalar_prefetch=1, grid=(rows // br,), in_specs=[spec, part, part, spec, spec], out_specs=[spec] * 4)
    return pl.pallas_call(
        body, name=name, grid_spec=grid_spec, out_shape=[shp] * 4,
        compiler_params=_params())(_core_index(), w, g_mine, g_theirs, m, v)


def _sum_lead(a, name):
    n, rows, _ = a.shape
    tr = 256 if rows % 256 == 0 else rows

    def body(a_ref, o_ref):
        acc = a_ref[0].astype(F32)
        for k in range(1, n):
            acc = acc + a_ref[k].astype(F32)
        o_ref[...] = acc

    return pl.pallas_call(
        body, name=name, grid=(rows // tr,),
        in_specs=[pl.BlockSpec((n, tr, D_MODEL), lambda i: (0, i, 0))],
        out_specs=pl.BlockSpec((tr, D_MODEL), lambda i: (i, 0)),
        out_shape=jax.ShapeDtypeStruct((rows, D_MODEL), F32), compiler_params=_params())(a)


ANY = pl.BlockSpec(memory_space=pl.ANY)


def _mesh_pos():
    return lax.axis_index("x"), lax.axis_index("y"), lax.axis_index("c")


def _gather_weights(arrs):
    na = len(arrs)
    halves = [a.shape[0] // 2 for a in arrs]
    ncopy = 3 * na

    def body(*refs):
        in_refs, out_refs, bounces = refs[:na], refs[na:2 * na], refs[2 * na:3 * na]
        send_sems, recv_sems, local_sems = refs[3 * na:]
        x, y, c = _mesh_pos()
        me = 2 * x + y
        sibling = (x, y, 1 - c)
        chips = [(1 - x, y), (x, 1 - y), (1 - x, 1 - y)]
        ids = [2 * chip[0] + chip[1] for chip in chips]

        def piece(a, q, h):
            return out_refs[a].at[q, pl.ds(h * halves[a], halves[a]), :]

        def copy(s, a, q, h, to, src=None):
            return pltpu.make_async_remote_copy(
                src_ref=piece(a, q, h) if src is None else src, dst_ref=piece(a, q, h),
                send_sem=send_sems.at[s], recv_sem=recv_sems.at[s], device_id=to, device_id_type=MESH)

        loads = [pltpu.make_async_copy(in_refs[a], bounces[a], local_sems.at[a]) for a in range(na)]
        stores = [pltpu.make_async_copy(bounces[a], out_refs[a].at[me], local_sems.at[na + a]) for a in range(na)]
        for cp in loads:
            cp.start()
        first = [copy(j * na + a, a, me, c, (*chip, c), src=in_refs[a].at[pl.ds(c * halves[a], halves[a]), :])
                 for j, chip in enumerate(chips) for a in range(na)]
        for cp in first:
            cp.start()
        for a in range(na):
            loads[a].wait()
            stores[a].start()
        passed = []
        for j in range(3):
            for a in range(na):
                s = j * na + a
                copy(s, a, ids[j], c, (x, y, c)).wait_recv()
                fwd = copy(ncopy + s, a, ids[j], c, sibling)
                fwd.start()
                passed.append(fwd)
        for j in range(3):
            for a in range(na):
                copy(ncopy + j * na + a, a, ids[j], 1 - c, (x, y, c)).wait_recv()
        for cp in first + passed:
            cp.wait_send()
        for cp in stores:
            cp.wait()

    return pl.pallas_call(
        body, name="gather_weights", in_specs=[ANY] * na, out_specs=[ANY] * na,
        out_shape=[jax.ShapeDtypeStruct((4,) + a.shape, a.dtype) for a in arrs],
        scratch_shapes=[pltpu.VMEM(a.shape, a.dtype) for a in arrs]
        + [pltpu.SemaphoreType.DMA((2 * ncopy,)), pltpu.SemaphoreType.DMA((2 * ncopy,)),
           pltpu.SemaphoreType.DMA((2 * na,))],
        compiler_params=pltpu.CompilerParams(vmem_limit_bytes=VMEM_LIMIT),
    )(*arrs)


def _gather_all(v):
    def body(v_ref, o_ref, bounce, send_sems, recv_sems, local_sems):
        x, y, c = _mesh_pos()
        sibling = (x, y, 1 - c)
        chips = [(1 - x, y), (x, 1 - y), (1 - x, 1 - y)]

        def blk(px, py, pc):
            return o_ref.at[4 * px + 2 * py + pc]

        def copy(k, block, to, src=None):
            return pltpu.make_async_remote_copy(
                src_ref=blk(*block) if src is None else src, dst_ref=blk(*block),
                send_sem=send_sems.at[k], recv_sem=recv_sems.at[k], device_id=to, device_id_type=MESH)

        load = pltpu.make_async_copy(v_ref, bounce, local_sems.at[0])
        store = pltpu.make_async_copy(bounce, blk(x, y, c), local_sems.at[1])
        load.start()
        first = [copy(0, (x, y, c), sibling, src=v_ref)]
        first += [copy(1 + j, (x, y, c), (*chip, c), src=v_ref) for j, chip in enumerate(chips)]
        for cp in first:
            cp.start()
        load.wait()
        store.start()
        passed = [copy(4 + j, (*chip, c), sibling) for j, chip in enumerate(chips)]
        for j, chip in enumerate(chips):
            copy(1 + j, (*chip, c), (x, y, c)).wait_recv()
            passed[j].start()
        copy(0, (x, y, 1 - c), (x, y, c)).wait_recv()
        for j, chip in enumerate(chips):
            copy(4 + j, (*chip, 1 - c), (x, y, c)).wait_recv()
        for cp in first + passed:
            cp.wait_send()
        store.wait()

    return pl.pallas_call(
        body, name="gather_all", in_specs=[ANY], out_specs=ANY,
        out_shape=jax.ShapeDtypeStruct((8,) + v.shape, v.dtype),
        scratch_shapes=[pltpu.VMEM(v.shape, v.dtype), pltpu.SemaphoreType.DMA((7,)), pltpu.SemaphoreType.DMA((7,)),
                        pltpu.SemaphoreType.DMA((2,))],
    )(v)


def _pair_split(g):
    n, rows, _ = g.shape
    half = rows // 2
    ch = half // COMM_CHUNKS

    def body(g_ref, theirs_ref, send_sems, recv_sems):
        x, y, c = _mesh_pos()
        gives = [pltpu.make_async_remote_copy(
            src_ref=g_ref.at[q, pl.ds((1 - c) * half + k * ch, ch), :],
            dst_ref=theirs_ref.at[q, pl.ds(k * ch, ch), :],
            send_sem=send_sems.at[q * COMM_CHUNKS + k], recv_sem=recv_sems.at[q * COMM_CHUNKS + k],
            device_id=(x, y, 1 - c), device_id_type=MESH) for q in range(n) for k in range(COMM_CHUNKS)]
        for cp in gives:
            cp.start()
        for cp in gives:
            cp.wait()

    return pl.pallas_call(
        body, name="pair_split", in_specs=[ANY], out_specs=ANY,
        out_shape=jax.ShapeDtypeStruct((n, half, D_MODEL), g.dtype),
        scratch_shapes=[pltpu.SemaphoreType.DMA((n * COMM_CHUNKS,)), pltpu.SemaphoreType.DMA((n * COMM_CHUNKS,))],
    )(g)


def _chip_scatter(p):
    def body(p_ref, o_ref, bounce, send_sems, recv_sems, local_sems):
        x, y, c = _mesh_pos()
        me = 2 * x + y
        chips = [(1 - x, y), (x, 1 - y), (1 - x, 1 - y)]
        load = pltpu.make_async_copy(p_ref.at[me], bounce, local_sems.at[0])
        keep = pltpu.make_async_copy(bounce, o_ref.at[me], local_sems.at[1])
        load.start()
        sends = [pltpu.make_async_remote_copy(
            src_ref=p_ref.at[2 * chip[0] + chip[1]], dst_ref=o_ref.at[me],
            send_sem=send_sems.at[j], recv_sem=recv_sems.at[j], device_id=(*chip, c), device_id_type=MESH)
            for j, chip in enumerate(chips)]
        for cp in sends:
            cp.start()
        load.wait()
        keep.start()
        for j, chip in enumerate(chips):
            q = 2 * chip[0] + chip[1]
            pltpu.make_async_remote_copy(
                src_ref=p_ref.at[q], dst_ref=o_ref.at[q], send_sem=send_sems.at[j], recv_sem=recv_sems.at[j],
                device_id=(*chip, c), device_id_type=MESH).wait_recv()
        for cp in sends:
            cp.wait_send()
        keep.wait()

    return pl.pallas_call(
        body, name="chip_scatter", in_specs=[ANY], out_specs=ANY, out_shape=jax.ShapeDtypeStruct(p.shape, p.dtype),
        scratch_shapes=[pltpu.VMEM(p.shape[1:], p.dtype), pltpu.SemaphoreType.DMA((3,)),
                        pltpu.SemaphoreType.DMA((3,)), pltpu.SemaphoreType.DMA((2,))],
    )(p)


def _pair_join(r):
    rows = r.shape[0]
    ch = rows // COMM_CHUNKS

    def body(r_ref, o_ref, send_sems, recv_sems):
        x, y, c = _mesh_pos()
        gives = [pltpu.make_async_remote_copy(
            src_ref=r_ref.at[pl.ds(k * ch, ch), :], dst_ref=o_ref.at[pl.ds(k * ch, ch), :],
            send_sem=send_sems.at[k], recv_sem=recv_sems.at[k], device_id=(x, y, 1 - c), device_id_type=MESH)
            for k in range(COMM_CHUNKS)]
        for cp in gives:
            cp.start()
        for cp in gives:
            cp.wait()

    return pl.pallas_call(
        body, name="pair_join", in_specs=[ANY], out_specs=ANY, out_shape=jax.ShapeDtypeStruct(r.shape, r.dtype),
        scratch_shapes=[pltpu.SemaphoreType.DMA((COMM_CHUNKS,)), pltpu.SemaphoreType.DMA((COMM_CHUNKS,))],
    )(r)


SHARD_BIG = (("even_w_in", (1024, 512)), ("ssm_w_glu", (512, 256)), ("even_w_out", (256, 1024)),
             ("odd_w_in", (1024, 768)), ("odd_w_out", (256, 1024)))
SHARD_SMALL = (("odd_norm", 1), ("conv_w", CONV_K), ("conv_b", 1), ("conv_ln_g", 1), ("conv_ln_b", 1))
REP_NAMES = (("even_norm", (1024,)), ("pool_w", (4, 128, 128)), ("pool_scale", (512,)), ("ssm_log_dt", (32,)),
             ("ssm_a_re", (32, 64)), ("ssm_a_im", (32, 64)), ("ssm_b_re", (32, 64, 16)), ("ssm_b_im", (32, 64, 16)),
             ("ssm_c_re", (32, 16, 64)), ("ssm_c_im", (32, 16, 64)), ("ssm_d", (512,)), ("final_norm", (1024,)))


def _pack_rep(d):
    flat = jnp.concatenate([d[n].reshape(-1) for n, _ in REP_NAMES])
    return jnp.pad(flat, (0, REP_ROWS * D_MODEL - flat.shape[0])).reshape(REP_ROWS, D_MODEL)


def _unpack_rep(buf):
    flat = buf.reshape(-1)
    out = {}
    off = 0
    for n, shp in REP_NAMES:
        size = 1
        for s in shp:
            size *= s
        out[n] = flat[off:off + size].reshape(shp)
        off += size
    return out


def _cols_split(full, cols):
    rows = full.shape[0]
    return full.reshape(rows, 4, cols).transpose(1, 0, 2).reshape(4, -1, D_MODEL)


def _block_diag(a):
    a = a.reshape(4, 8, GROUP_DIM, N_STATE)
    eye = jnp.eye(8, dtype=a.dtype)
    return (a[:, :, :, None, :] * eye[None, :, None, :, None]).reshape(4, 128, LCH)


def _block_diag_take(m):
    m = m.reshape(4, 8, GROUP_DIM, 8, N_STATE)
    eye = jnp.eye(8, dtype=m.dtype)
    return jnp.sum(m * eye[None, :, None, :, None], axis=3).reshape(N_GROUPS, GROUP_DIM, N_STATE)


def _ssm_discretise(log_dt, a_re, a_im, b_re, b_im):
    dt = jnp.exp(log_dt)[:, None]
    mag = jnp.exp(a_re * dt)
    ang = a_im * dt
    abar_re = mag * jnp.cos(ang)
    abar_im = mag * jnp.sin(ang)
    den = a_re * a_re + a_im * a_im
    nr = abar_re - 1.0
    ni = abar_im
    k_re = (nr * a_re + ni * a_im) / den
    k_im = (ni * a_re - nr * a_im) / den
    bb_re = k_re[..., None] * b_re - k_im[..., None] * b_im
    bb_im = k_re[..., None] * b_im + k_im[..., None] * b_re
    return abar_re, abar_im, bb_re, bb_im


def _scan_tables(log_dt, a_re, a_im):
    dt = jnp.exp(log_dt)[:, None]
    lam_re = (a_re * dt).reshape(1, STATES)
    lam_im = (a_im * dt).reshape(1, STATES)

    def powers(k):
        mag = jnp.exp(k * lam_re)
        return mag * jnp.cos(k * lam_im), mag * jnp.sin(k * lam_im)

    p_re, p_im = powers((1 + jnp.arange(TM) // SUB).astype(F32)[:, None])
    q_re, q_im = powers((SEG_LEN * (1 + jnp.arange(SUB))).astype(F32)[:, None])
    return p_re, p_im, q_re, q_im


def _local_step(x, tgt, w):
    row = lambda a: a.reshape(1, -1)
    e_w_in, e_w_out, o_w_in, o_w_out, wglu = (w["even_w_in"], w["even_w_out"], w["odd_w_in"], w["odd_w_out"],
                                              w["ssm_w_glu"])
    wp = w["pool_w"].astype(BF16)
    sm = w["odd_small"]
    ssm_in = (w["ssm_log_dt"], w["ssm_a_re"], w["ssm_a_im"], w["ssm_b_re"], w["ssm_b_im"])
    (abar_re, abar_im, bb_re, bb_im), ssm_vjp = jax.vjp(_ssm_discretise, *ssm_in)
    mb_re = _block_diag(bb_re.transpose(0, 2, 1)).astype(BF16)
    mb_im = _block_diag(bb_im.transpose(0, 2, 1)).astype(BF16)
    cm_re = _block_diag(w["ssm_c_re"]).astype(BF16)
    cm_im = _block_diag(w["ssm_c_im"]).astype(BF16)
    p8_re, p8_im, q_re, q_im = _scan_tables(w["ssm_log_dt"], w["ssm_a_re"], w["ssm_a_im"])
    qr_re, qr_im = q_re[::-1], q_im[::-1]
    pm = _perm_matrix()
    pmt = pm.T
    cw = sm[1:1 + HALO]
    g0, g1, gf = row(w["even_norm"]), sm[0:1], row(w["final_norm"])
    ps, dskip = row(w["pool_scale"]), row(w["ssm_d"])
    cb, lg, lb = sm[32:33], sm[33:34], sm[34:35]

    proj = _norm_in(x, g0, e_w_in, "even_in")
    yp = _pool_fwd(proj, wp, ps)
    ys, car_re, car_im = _ssm_fwd(proj, pm, pmt, mb_re, mb_im, p8_re, p8_im, q_re, q_im, cm_re, cm_im, dskip, wglu)
    x1, yg = _even_out(yp, ys, proj, x, e_w_out)
    q = _norm_in(x1, g1, o_w_in, "odd_in")
    y2, cv = _conv_fwd(q, cw, cb, lg, lb)
    dx2, loss_lanes, d_gf = _odd_out_loss(y2, x1, o_w_out, gf, tgt)

    dcv, dz2, d_o_w_out, d_lg, d_lb = _odd_bwd_out(dx2, o_w_out, y2, cv, q, lg, lb)
    dval, dgate, d_cw, d_cb = _conv_bwd(dcv, q, cw)
    dx1, d_g1, d_o_w_in = _in_bwd([dval, dgate, dz2], o_w_in, x1, g1, dx2, "odd_in_bwd")
    dycat, dz, d_e_w_out = _even_bwd_out(dx1, e_w_out, yg, yp, ys, proj)
    dup, d_wp, d_ps = _pool_bwd(dycat, proj, wp, ps)
    (dus, d_mb_re, d_mb_im, d_cm_re, d_cm_im, da_re, da_im, d_dskip, d_wglu) = _ssm_bwd(
        dycat, proj, car_re, car_im, pm, pmt, mb_re, mb_im, p8_re, p8_im, q_re, q_im, qr_re, qr_im,
        cm_re, cm_im, dskip, wglu)
    dx, d_g0, d_e_w_in = _in_bwd([dup, dus, dz], e_w_in, x, g0, dx1, "even_in_bwd")

    d_abar_re = jnp.sum(da_re, axis=0).reshape(N_GROUPS, N_STATE)
    d_abar_im = jnp.sum(da_im, axis=0).reshape(N_GROUPS, N_STATE)
    d_bb_re = _block_diag_take(d_mb_re).transpose(0, 2, 1)
    d_bb_im = _block_diag_take(d_mb_im).transpose(0, 2, 1)
    d_log_dt, d_a_re, d_a_im, d_b_re, d_b_im = ssm_vjp((d_abar_re, d_abar_im, d_bb_re, d_bb_im))

    grads = {
        "even_norm": d_g0.reshape(-1), "even_w_in": d_e_w_in, "pool_w": d_wp, "pool_scale": d_ps.reshape(-1),
        "ssm_log_dt": d_log_dt, "ssm_a_re": d_a_re, "ssm_a_im": d_a_im, "ssm_b_re": d_b_re, "ssm_b_im": d_b_im,
        "ssm_c_re": _block_diag_take(d_cm_re), "ssm_c_im": _block_diag_take(d_cm_im),
        "ssm_d": d_dskip.reshape(-1), "ssm_w_glu": d_wglu, "even_w_out": d_e_w_out, "odd_norm": d_g1.reshape(-1),
        "odd_w_in": d_o_w_in, "conv_w": d_cw.reshape(HALO, SUB, D_MODEL).sum(axis=1)[:CONV_K], "conv_b": d_cb.reshape(-1), "conv_ln_g": d_lg.reshape(-1),
        "conv_ln_b": d_lb.reshape(-1), "odd_w_out": d_o_w_out, "final_norm": d_gf.reshape(-1),
    }
    return jnp.sum(loss_lanes), dx, grads


WEIGHT_NAMES = ("even_norm", "even_w_in", "pool_w", "pool_scale", "ssm_log_dt", "ssm_a_re", "ssm_a_im",
                "ssm_b_re", "ssm_b_im", "ssm_c_re", "ssm_c_im", "ssm_d", "ssm_w_glu", "even_w_out", "odd_norm",
                "odd_w_in", "conv_w", "conv_b", "conv_ln_g", "conv_ln_b", "odd_w_out", "final_norm")
SHARDED = tuple(n for n, _ in SHARD_BIG) + tuple(n for n, _ in SHARD_SMALL)


SMALL_ROWS = 64


def _full_weights(shard, rep):
    small = jnp.concatenate([shard[n].reshape(r, 256) for n, r in SHARD_SMALL], axis=0)
    small = jnp.pad(small, ((0, SMALL_ROWS - small.shape[0]), (0, 0)))
    arrs = [shard[n].astype(BF16) for n, _ in SHARD_BIG] + [small]
    g_in, g_glu, g_eout, g_oin, g_oout, g_small = _gather_weights(arrs)
    w = dict(rep)
    w["even_w_in"] = g_in
    w["ssm_w_glu"] = g_glu.transpose(1, 0, 2).reshape(SSM_W, 2 * SSM_W)
    w["even_w_out"] = g_eout.reshape(D_MODEL, D_MODEL)
    w["odd_w_in"] = g_oin
    w["odd_w_out"] = g_oout.reshape(D_MODEL, D_MODEL)
    w["odd_small"] = g_small.transpose(1, 0, 2).reshape(SMALL_ROWS, D_MODEL)
    return w


def _pack_small(d):
    small = jnp.concatenate([d[n].reshape(r, -1) for n, r in SHARD_SMALL], axis=0)
    if small.shape[1] == D_MODEL:
        small = small.reshape(35, 4, 256).transpose(1, 0, 2)
    small = small.reshape(-1, 35 * 256)
    small = jnp.pad(small, ((0, 0), (0, ROWS_SMALL * D_MODEL - 35 * 256)))
    return small.reshape(-1, ROWS_SMALL, D_MODEL)


def _unpack_small(buf):
    small = buf.reshape(-1)[:35 * 256].reshape(35, 256)
    out = {}
    off = 0
    for n, r in SHARD_SMALL:
        out[n] = small[off:off + r].reshape((r, 256) if r > 1 else (256,))
        off += r
    return out


def _pack_grads(g):
    parts = [g["even_w_in"].reshape(4, -1, D_MODEL), _cols_split(g["ssm_w_glu"], 256),
             g["even_w_out"].reshape(4, -1, D_MODEL), g["odd_w_in"].reshape(4, -1, D_MODEL),
             g["odd_w_out"].reshape(4, -1, D_MODEL), _pack_small(g),
             jnp.zeros((4, ROWS_PACK - ROWS_BIG - ROWS_SMALL, D_MODEL), F32)]
    return jnp.concatenate(parts, axis=1)


def kernel(x, even_norm, even_w_in, pool_w, pool_scale, ssm_log_dt, ssm_a_re, ssm_a_im, ssm_b_re, ssm_b_im, ssm_c_re, ssm_c_im, ssm_d, ssm_w_glu, even_w_out, odd_norm, odd_w_in, conv_w, conv_b, conv_ln_g, conv_ln_b, odd_w_out, final_norm, loss_target, m_even_norm, m_even_w_in, m_pool_w, m_pool_scale, m_ssm_log_dt, m_ssm_a_re, m_ssm_a_im, m_ssm_b_re, m_ssm_b_im, m_ssm_c_re, m_ssm_c_im, m_ssm_d, m_ssm_w_glu, m_even_w_out, m_odd_norm, m_odd_w_in, m_conv_w, m_conv_b, m_conv_ln_g, m_conv_ln_b, m_odd_w_out, m_final_norm, v_even_norm, v_even_w_in, v_pool_w, v_pool_scale, v_ssm_log_dt, v_ssm_a_re, v_ssm_a_im, v_ssm_b_re, v_ssm_b_im, v_ssm_c_re, v_ssm_c_im, v_ssm_d, v_ssm_w_glu, v_even_w_out, v_odd_norm, v_odd_w_in, v_conv_w, v_conv_b, v_conv_ln_g, v_conv_ln_b, v_odd_w_out, v_final_norm):
    ws = dict(zip(WEIGHT_NAMES, (even_norm, even_w_in, pool_w, pool_scale, ssm_log_dt, ssm_a_re, ssm_a_im, ssm_b_re,
                                 ssm_b_im, ssm_c_re, ssm_c_im, ssm_d, ssm_w_glu, even_w_out, odd_norm, odd_w_in,
                                 conv_w, conv_b, conv_ln_g, conv_ln_b, odd_w_out, final_norm)))
    ms = dict(zip(WEIGHT_NAMES, (m_even_norm, m_even_w_in, m_pool_w, m_pool_scale, m_ssm_log_dt, m_ssm_a_re,
                                 m_ssm_a_im, m_ssm_b_re, m_ssm_b_im, m_ssm_c_re, m_ssm_c_im, m_ssm_d, m_ssm_w_glu,
                                 m_even_w_out, m_odd_norm, m_odd_w_in, m_conv_w, m_conv_b, m_conv_ln_g, m_conv_ln_b,
                                 m_odd_w_out, m_final_norm)))
    vs = dict(zip(WEIGHT_NAMES, (v_even_norm, v_even_w_in, v_pool_w, v_pool_scale, v_ssm_log_dt, v_ssm_a_re,
                                 v_ssm_a_im, v_ssm_b_re, v_ssm_b_im, v_ssm_c_re, v_ssm_c_im, v_ssm_d, v_ssm_w_glu,
                                 v_even_w_out, v_odd_norm, v_odd_w_in, v_conv_w, v_conv_b, v_conv_ln_g, v_conv_ln_b,
                                 v_odd_w_out, v_final_norm)))
    lead = {n: a.shape for n, a in ws.items()}
    drop = lambda d: {n: (a[0] if n != "final_norm" else a) for n, a in d.items()}
    ws, ms, vs = drop(ws), drop(ms), drop(vs)

    shard = {n: ws[n] for n in SHARDED}
    rep = {n: ws[n] for n, _ in REP_NAMES}
    w_full = _full_weights(shard, rep)
    loss_part, grad_x, grads = _local_step(x[0], loss_target[0], w_full)
    loss = lax.psum(loss_part, ("x", "y", "c"))

    g_pack = _pack_grads(grads)
    got = _chip_scatter(_pair_add(g_pack, _pair_split(g_pack), BF16))
    g_mine = _sum_lead(got, "chip_sum")
    g_theirs = _pair_join(g_mine)
    outs = [{}, {}, {}, {}]
    row0 = 0
    for n, shp in SHARD_BIG:
        rows = shp[0] * shp[1] // D_MODEL
        view = lambda a: a.reshape(rows, D_MODEL)
        res = _adamw_rows(view(ws[n]), view(ms[n]), view(vs[n]), g_mine, g_theirs, row0, 128, "adamw_" + n)
        for o, r in zip(outs, res):
            o[n] = r
        row0 += rows
    small = lambda d: _pack_small({n: d[n] for n, _ in SHARD_SMALL})[0]
    res = _adamw_rows(small(ws), small(ms), small(vs), g_mine, g_theirs, ROWS_BIG, ROWS_SMALL, "adamw_small")
    for o, r in zip(outs, res):
        o.update(_unpack_small(r))
    g_rep = _sum_lead(_gather_all(_pack_rep({n: grads[n] for n, _ in REP_NAMES})), "rep_sum")
    res = _adamw(_pack_rep(rep), g_rep, _pack_rep({n: ms[n] for n, _ in REP_NAMES}),
                 _pack_rep({n: vs[n] for n, _ in REP_NAMES}), "adamw_rep")
    for o, r in zip(outs, (g_rep,) + tuple(res)):
        o.update(_unpack_rep(r))

    leaves = [[o[n].reshape(lead[n]) for n in WEIGHT_NAMES] for o in outs]
    return (loss, grad_x[None], *leaves[0], *leaves[1], *leaves[2], *leaves[3])
```

```python
import functools

import jax
import jax.numpy as jnp
from jax import lax
from jax.experimental import pallas as pl
from jax.experimental.pallas import tpu as pltpu

F32 = jnp.float32
BF16 = jnp.bfloat16
MESH = pl.DeviceIdType.MESH

D_MODEL = 1024
RMS_EPS = 1e-6
LN_EPS = 1e-5
N_GROUPS = 32
GROUP_DIM = 16
N_STATE = 64
STATES = N_GROUPS * N_STATE
SSM_W = 512
POOL_W = 512
CONV_K = 31
HALO = 32
POOL_HALO = 16

ADAM_LR = 0.001
ADAM_B1 = 0.9
ADAM_B2 = 0.999
ADAM_EPS = 1e-08
ADAM_WD = 0.01
ADAM_STEP = 10

TM = 256
TM_MM = 512
SUB = 8
LCH = 512
SCAN_L = 1024
VMEM_LIMIT = 56 * 1024 * 1024

ROWS_BIG = 1920
ROWS_SMALL = 16
ROWS_PACK = 2048
REP_ROWS = 200
COMM_CHUNKS = 4


def _params(n_axes=1):
    return pltpu.CompilerParams(dimension_semantics=("arbitrary",) * n_axes, vmem_limit_bytes=VMEM_LIMIT)


def _rows(w, cb=0, rev=None, tm=TM):
    if rev is None:
        return pl.BlockSpec((tm, w), lambda i: (i, cb))
    return pl.BlockSpec((tm, w), lambda i: (rev - 1 - i, cb))


def _mm_rows(w, cb=0):
    return _rows(w, cb, tm=TM_MM)


def _full(shape):
    n = len(shape)
    return pl.BlockSpec(shape, lambda i: (0,) * n)


def _prev(hr, w, cb=0, tm=TM):
    r = tm // hr
    return pl.BlockSpec((hr, w), lambda i: (jnp.maximum(i * r - 1, 0), cb))


def _next(hr, w, nrows, cb=0, tm=TM):
    r = tm // hr
    last = nrows // hr - 1
    return pl.BlockSpec((hr, w), lambda i: (jnp.minimum((i + 1) * r, last), cb))


def _dot(a, b):
    return jnp.dot(a, b, preferred_element_type=F32)


def _dot_nt(a, b):
    return lax.dot_general(a, b, (((1,), (1,)), ((), ())), preferred_element_type=F32)


def _dot_tn(a, b):
    return lax.dot_general(a, b, (((0,), (0,)), ((), ())), preferred_element_type=F32)


def _sig(x):
    return 1.0 / (1.0 + jnp.exp(-x))


def _zero_at_first(i, *refs):
    @pl.when(i == 0)
    def _():
        for r in refs:
            r[...] = jnp.zeros_like(r)


def _norm_in(x, g, w, name):
    t, ns = x.shape[0], w.shape[2]
    n = 4 * ns

    def body(x_ref, g_ref, w_ref, o_ref):
        xv = x_ref[...]
        r = lax.rsqrt(jnp.mean(xv * xv, axis=-1, keepdims=True) + RMS_EPS)
        h = (xv * r * g_ref[...]).astype(BF16)
        for s in range(4):
            o_ref[:, s * ns:(s + 1) * ns] = _dot(h, w_ref[s]).astype(BF16)

    return pl.pallas_call(
        body, name=name, grid=(t // TM_MM,),
        in_specs=[_mm_rows(D_MODEL), _full((1, D_MODEL)), _full(w.shape)],
        out_specs=_mm_rows(n), out_shape=jax.ShapeDtypeStruct((t, n), BF16),
        compiler_params=_params())(x, g, w)


def _pool_sums(ext, g, forward):
    n = ext.shape[0]
    s = ext
    for step in range(g + 1):
        k = 1 << step
        s = s + pltpu.roll(s, k if forward else n - k, 0)
    return s


def _pool_fwd(proj, wp, ps):
    t = proj.shape[0]

    def body(u_ref, h_ref, wp_ref, ps_ref, y_ref):
        i = pl.program_id(0)
        pos = (i * TM + 1 + lax.broadcasted_iota(jnp.int32, (TM, 1), 0)).astype(F32)
        for g in range(4):
            sl = slice(128 * g, 128 * (g + 1))
            u = u_ref[:, sl].astype(F32)
            halo = jnp.where(i == 0, 0.0, h_ref[:, sl].astype(F32))
            s = _pool_sums(jnp.concatenate([halo, u], axis=0), g, True)[POOL_HALO:, :]
            pooled = s / jnp.minimum(pos, float(2 << g)) - u
            y_ref[:, sl] = _dot(pooled.astype(BF16), wp_ref[g]) * ps_ref[:, sl]

    return pl.pallas_call(
        body, name="pool_fwd", grid=(t // TM,),
        in_specs=[_rows(POOL_W, 0), _prev(POOL_HALO, POOL_W, 0), _full((4, 128, 128)), _full((1, POOL_W))],
        out_specs=_rows(POOL_W), out_shape=jax.ShapeDtypeStruct((t, POOL_W), F32),
        compiler_params=_params())(proj, proj, wp, ps)


SEG_LEN = TM // SUB


def _perm_matrix():
    p = jnp.arange(TM)
    src = (p % SUB) * SEG_LEN + p // SUB
    return (src[:, None] == jnp.arange(TM)[None, :]).astype(BF16)


def _cmul_add(are, aim, vre, vim, bre, bim):
    return are * vre - aim * vim + bre, are * vim + aim * vre + bim


def _segment_chain(ere, eim, qre, qim, cin_re, cin_im, row, up):
    for sh in (1, 2, 4):
        mre, mim = (qre[SUB - sh:SUB - sh + 1, :], qim[SUB - sh:SUB - sh + 1, :]) if up else \
                   (qre[sh - 1:sh, :], qim[sh - 1:sh, :])
        keep = (row < SUB - sh) if up else (row >= sh)
        sre = jnp.where(keep, pltpu.roll(ere, SUB - sh if up else sh, 0), 0.0)
        sim = jnp.where(keep, pltpu.roll(eim, SUB - sh if up else sh, 0), 0.0)
        ere, eim = _cmul_add(mre, mim, sre, sim, ere, eim)
    ere, eim = _cmul_add(qre, qim, cin_re, cin_im, ere, eim)
    keep = (row < SUB - 1) if up else (row >= 1)
    ent_re = jnp.where(keep, pltpu.roll(ere, SUB - 1 if up else 1, 0), cin_re)
    ent_im = jnp.where(keep, pltpu.roll(eim, SUB - 1 if up else 1, 0), cin_im)
    return ere, eim, ent_re, ent_im


def _scan_fwd_block(xs_re, xs_im, p8_re, p8_im, q_re, q_im, car_re, car_im, ent_re_ref, ent_im_ref):
    row = lax.broadcasted_iota(jnp.int32, (SUB, SCAN_L), 0)
    for j in range(STATES // SCAN_L):
        sl = slice(SCAN_L * j, SCAN_L * (j + 1))
        are, aim = p8_re[0:SUB, sl], p8_im[0:SUB, sl]

        def totals(i, v, sl=sl, are=are, aim=aim):
            r0 = pl.multiple_of(i * SUB, SUB)
            vre, vim = _cmul_add(are, aim, v[0], v[1], xs_re[pl.ds(r0, SUB), sl], xs_im[pl.ds(r0, SUB), sl])
            xs_re[pl.ds(r0, SUB), sl] = vre
            xs_im[pl.ds(r0, SUB), sl] = vim
            return vre, vim

        ere, eim = lax.fori_loop(1, SEG_LEN, totals, (xs_re[0:SUB, sl], xs_im[0:SUB, sl]), unroll=2)
        ere, eim, cre, cim = _segment_chain(ere, eim, q_re[:, sl], q_im[:, sl],
                                            car_re[:, sl], car_im[:, sl], row, False)
        car_re[:, sl] = jnp.broadcast_to(ere[SUB - 1:SUB, :], (SUB, SCAN_L))
        car_im[:, sl] = jnp.broadcast_to(eim[SUB - 1:SUB, :], (SUB, SCAN_L))
        if ent_re_ref is not None:
            ent_re_ref[:, sl] = cre
            ent_im_ref[:, sl] = cim

        def fix(i, c, sl=sl, cre=cre, cim=cim):
            r0 = pl.multiple_of(i * SUB, SUB)
            vre, vim = _cmul_add(p8_re[pl.ds(r0, SUB), sl], p8_im[pl.ds(r0, SUB), sl], cre, cim,
                                 xs_re[pl.ds(r0, SUB), sl], xs_im[pl.ds(r0, SUB), sl])
            xs_re[pl.ds(r0, SUB), sl] = vre
            xs_im[pl.ds(r0, SUB), sl] = vim
            return c

        lax.fori_loop(0, SEG_LEN, fix, 0, unroll=2)


def _unpermute(pmt_ref, v):
    hi = v.astype(BF16)
    lo = (v - hi.astype(F32)).astype(BF16)
    return _dot(pmt_ref[...], hi) + _dot(pmt_ref[...], lo)


def _ssm_fwd(proj, pm, pmt, mb_re, mb_im, p8_re, p8_im, q_re, q_im, cm_re, cm_im, dskip, wglu, ride=()):
    t = proj.shape[0]
    nblk = t // TM

    ng = len(ride)

    def body(u_ref, pm_ref, pmt_ref, mbre, mbim, p8re, p8im, qre, qim, cmre, cmim, d_ref, wg_ref, *rest):
        ride_in, (y_ref, cre_ref, cim_ref), ride_out = rest[:ng], rest[ng:ng + 3], rest[ng + 3:2 * ng + 3]
        xs_re, xs_im, car_re, car_im, ysk = rest[2 * ng + 3:2 * ng + 8]
        i = pl.program_id(0)
        if ng:
            start, forward, finish = _gather_phases(ride_in, ride_out, rest[2 * ng + 8:3 * ng + 8],
                                                    *rest[3 * ng + 8:])
            pl.when(i == 0)(start)
            pl.when(i == nblk // 2)(forward)
        _zero_at_first(i, car_re, car_im)
        cre_ref[0] = car_re[...]
        cim_ref[0] = car_im[...]
        us = _dot(pm_ref[...], u_ref[...])
        usb = us.astype(BF16)
        for j in range(4):
            xs_re[:, LCH * j:LCH * (j + 1)] = _dot(usb[:, 128 * j:128 * (j + 1)], mbre[j])
            xs_im[:, LCH * j:LCH * (j + 1)] = _dot(usb[:, 128 * j:128 * (j + 1)], mbim[j])
        _scan_fwd_block(xs_re, xs_im, p8re, p8im, qre, qim, car_re, car_im, None, None)
        for j in range(4):
            sl = slice(LCH * j, LCH * (j + 1))
            ysk[:, 128 * j:128 * (j + 1)] = (_dot_nt(xs_re[:, sl].astype(BF16), cmre[j])
                                             - _dot_nt(xs_im[:, sl].astype(BF16), cmim[j]))
        yv = ysk[...] + d_ref[...] * us
        gv = _dot(yv.astype(BF16), wg_ref[...])
        y_ref[...] = _unpermute(pmt_ref, gv[:, :SSM_W] * _sig(gv[:, SSM_W:]))
        if ng:
            pl.when(i == nblk - 1)(finish)

    blk = (4, 128, LCH)
    res = pl.pallas_call(
        body, name="ssm_fwd", grid=(nblk,),
        in_specs=[_rows(SSM_W, 1), _full((TM, TM)), _full((TM, TM)), _full(blk), _full(blk),
                  _full((TM, STATES)), _full((TM, STATES)), _full((SUB, STATES)), _full((SUB, STATES)),
                  _full(blk), _full(blk), _full((1, SSM_W)), _full((SSM_W, 2 * SSM_W))] + [ANY] * ng,
        out_specs=[_rows(SSM_W), pl.BlockSpec((1, SUB, STATES), lambda i: (i, 0, 0)),
                   pl.BlockSpec((1, SUB, STATES), lambda i: (i, 0, 0))] + [ANY] * ng,
        out_shape=[jax.ShapeDtypeStruct((t, SSM_W), F32), jax.ShapeDtypeStruct((nblk, SUB, STATES), F32),
                   jax.ShapeDtypeStruct((nblk, SUB, STATES), F32)]
        + [jax.ShapeDtypeStruct((4,) + a.shape, a.dtype) for a in ride],
        scratch_shapes=[pltpu.VMEM((TM, STATES), F32), pltpu.VMEM((TM, STATES), F32),
                        pltpu.VMEM((SUB, STATES), F32), pltpu.VMEM((SUB, STATES), F32),
                        pltpu.VMEM((TM, SSM_W), F32)] + (_gather_scratch(ride) if ng else []),
        compiler_params=_params())(proj, pm, pmt, mb_re, mb_im, p8_re, p8_im, q_re, q_im, cm_re, cm_im, dskip, wglu,
                                   *ride)
    return res[:3], res[3:]


def _even_out(yp, ys, proj, x, w):
    t = x.shape[0]

    def body(yp_ref, ys_ref, z_ref, x_ref, w_ref, x1_ref, yg_ref):
        z = z_ref[...].astype(F32)
        sz = z * _sig(z)
        gp = (yp_ref[...] * sz[:, :POOL_W]).astype(BF16)
        gs = (ys_ref[...] * sz[:, POOL_W:]).astype(BF16)
        yg_ref[:, :POOL_W] = gp
        yg_ref[:, POOL_W:] = gs
        x1_ref[...] = x_ref[...] + _dot(gp, w_ref[:POOL_W, :]) + _dot(gs, w_ref[POOL_W:, :])

    return pl.pallas_call(
        body, name="even_out", grid=(t // TM_MM,),
        in_specs=[_mm_rows(POOL_W), _mm_rows(SSM_W), _mm_rows(D_MODEL, 1), _mm_rows(D_MODEL),
                  _full((D_MODEL, D_MODEL))],
        out_specs=[_mm_rows(D_MODEL), _mm_rows(D_MODEL)],
        out_shape=[jax.ShapeDtypeStruct((t, D_MODEL), F32), jax.ShapeDtypeStruct((t, D_MODEL), BF16)],
        compiler_params=_params())(yp, ys, proj, x, w)


def _phase_copies(ext, cp):
    n = cp.shape[1]
    for j in range(1, SUB):
        cp[j - 1] = ext[pl.ds(j, n), :]


def _shifted(ext, cp, off, nrows, sl, row0=0):
    q, j = divmod(off, SUB)
    if j == 0:
        return ext[pl.ds(row0 + SUB * q, nrows), sl]
    return cp[j - 1, pl.ds(row0 + SUB * q, nrows), sl]


def _conv_taps(ext, cp, w_ref, first, nrows, sl, init, row0=0):
    acc = init
    for k in range(CONV_K):
        acc = acc + w_ref[k:k + 1, sl] * _shifted(ext, cp, first(k), nrows, sl, row0)
    return acc


def _conv_fwd(q, cw, cb, lg, lb):
    t = q.shape[0]

    def body(v_ref, g_ref, hv_ref, hg_ref, z_ref, w_ref, b_ref, lg_ref, lb_ref, y_ref, cv_ref, ext, cp):
        i = pl.program_id(0)
        ext[0:HALO, :] = jnp.where(i == 0, 0.0, hv_ref[...].astype(F32) * _sig(hg_ref[...].astype(F32)))
        ext[HALO:, :] = v_ref[...].astype(F32) * _sig(g_ref[...].astype(F32))
        _phase_copies(ext, cp)

        def lanes(c, carry):
            sl = pl.ds(pl.multiple_of(c * 128, 128), 128)
            cv_ref[:, sl] = _conv_taps(ext, cp, w_ref, lambda k: k + 2, TM, sl,
                                       jnp.broadcast_to(b_ref[:, sl], (TM, 128)))
            return carry

        lax.fori_loop(0, D_MODEL // 128, lanes, 0)
        cv = cv_ref[...]
        cc = cv - jnp.mean(cv, axis=-1, keepdims=True)
        rstd = lax.rsqrt(jnp.mean(cc * cc, axis=-1, keepdims=True) + LN_EPS)
        cl = cc * rstd * lg_ref[...] + lb_ref[...]
        z = z_ref[...].astype(F32)
        y_ref[...] = (cl * _sig(cl) * z * _sig(z)).astype(BF16)

    vec = _full((1, D_MODEL))
    return pl.pallas_call(
        body, name="conv_fwd", grid=(t // TM,),
        in_specs=[_rows(D_MODEL, 0), _rows(D_MODEL, 1), _prev(HALO, D_MODEL, 0), _prev(HALO, D_MODEL, 1),
                  _rows(D_MODEL, 2), _full((HALO, D_MODEL)), vec, vec, vec],
        out_specs=[_rows(D_MODEL), _rows(D_MODEL)],
        out_shape=[jax.ShapeDtypeStruct((t, D_MODEL), BF16), jax.ShapeDtypeStruct((t, D_MODEL), F32)],
        scratch_shapes=[pltpu.VMEM((TM + HALO, D_MODEL), F32),
                        pltpu.VMEM((SUB - 1, TM + HALO - SUB, D_MODEL), F32)],
        compiler_params=_params())(q, q, q, q, q, cw, cb, lg, lb)


def _odd_out_loss(y2, x1, w, gf, tgt):
    t = x1.shape[0]

    def body(y_ref, x_ref, w_ref, g_ref, t_ref, dx_ref, loss_ref, dg_ref):
        i = pl.program_id(0)
        _zero_at_first(i, loss_ref, dg_ref)
        x2 = x_ref[...] + _dot(y_ref[...], w_ref[...])
        r = lax.rsqrt(jnp.mean(x2 * x2, axis=-1, keepdims=True) + RMS_EPS)
        n = x2 * r
        e = n * g_ref[...] - t_ref[...]
        loss_ref[...] += jnp.sum(e * e, axis=0, keepdims=True) * (0.5 / D_MODEL)
        dout = e * (1.0 / D_MODEL)
        dg_ref[...] += jnp.sum(dout * n, axis=0, keepdims=True)
        dn = dout * g_ref[...]
        dx_ref[...] = r * (dn - n * jnp.mean(dn * n, axis=-1, keepdims=True))

    vec = _full((1, D_MODEL))
    return pl.pallas_call(
        body, name="odd_out_loss", grid=(t // TM_MM,),
        in_specs=[_mm_rows(D_MODEL), _mm_rows(D_MODEL), _full((D_MODEL, D_MODEL)), vec, _mm_rows(D_MODEL)],
        out_specs=[_mm_rows(D_MODEL), vec, vec],
        out_shape=[jax.ShapeDtypeStruct((t, D_MODEL), F32), jax.ShapeDtypeStruct((1, D_MODEL), F32),
                   jax.ShapeDtypeStruct((1, D_MODEL), F32)],
        compiler_params=_params())(y2, x1, w, gf, tgt)


def _dsilu(z):
    s = _sig(z)
    return z * s, s * (1.0 + z * (1.0 - s))


def _odd_bwd_out(dx2, w, y2, cv, q, lg, lb):
    t = dx2.shape[0]

    def body(dx_ref, w_ref, y_ref, cv_ref, z_ref, lg_ref, lb_ref, dcv_ref, dz_ref, dw_ref, dlg_ref, dlb_ref):
        i = pl.program_id(0)
        _zero_at_first(i, dw_ref, dlg_ref, dlb_ref)
        dxb = dx_ref[...].astype(BF16)
        dy = _dot_nt(dxb, w_ref[...])
        dw_ref[...] += _dot_tn(y_ref[...], dxb)
        cv = cv_ref[...]
        cc = cv - jnp.mean(cv, axis=-1, keepdims=True)
        rstd = lax.rsqrt(jnp.mean(cc * cc, axis=-1, keepdims=True) + LN_EPS)
        cn = cc * rstd
        silu_c, dsilu_c = _dsilu(cn * lg_ref[...] + lb_ref[...])
        silu_z, dsilu_z = _dsilu(z_ref[...].astype(F32))
        dcl = dy * silu_z * dsilu_c
        dz_ref[...] = (dy * silu_c * dsilu_z).astype(BF16)
        dlg_ref[...] += jnp.sum(dcl * cn, axis=0, keepdims=True)
        dlb_ref[...] += jnp.sum(dcl, axis=0, keepdims=True)
        dcn = dcl * lg_ref[...]
        dcv_ref[...] = rstd * (dcn - jnp.mean(dcn, axis=-1, keepdims=True)
                               - cn * jnp.mean(dcn * cn, axis=-1, keepdims=True))

    vec = _full((1, D_MODEL))
    mat = _full((D_MODEL, D_MODEL))
    return pl.pallas_call(
        body, name="odd_bwd_out", grid=(t // TM_MM,),
        in_specs=[_mm_rows(D_MODEL), mat, _mm_rows(D_MODEL), _mm_rows(D_MODEL), _mm_rows(D_MODEL, 2), vec, vec],
        out_specs=[_mm_rows(D_MODEL), _mm_rows(D_MODEL), mat, vec, vec],
        out_shape=[jax.ShapeDtypeStruct((t, D_MODEL), F32), jax.ShapeDtypeStruct((t, D_MODEL), BF16),
                   jax.ShapeDtypeStruct((D_MODEL, D_MODEL), F32), jax.ShapeDtypeStruct((1, D_MODEL), F32),
                   jax.ShapeDtypeStruct((1, D_MODEL), F32)],
        compiler_params=_params())(dx2, w, y2, cv, q, lg, lb)


def _conv_bwd(dcv, q, cw):
    t = dcv.shape[0]
    nblk = t // TM

    def body(d_ref, dn_ref, v_ref, g_ref, hv_ref, hg_ref, w_ref,
             dv_ref, dgt_ref, dw_ref, db_ref, gext, dext, dgl, gcp, dcp):
        i = pl.program_id(0)
        last = nblk - 1
        _zero_at_first(i, dw_ref, db_ref)
        v = v_ref[...].astype(F32)
        sg = _sig(g_ref[...].astype(F32))
        gext[0:HALO, :] = jnp.where(i == 0, 0.0, hv_ref[...].astype(F32) * _sig(hg_ref[...].astype(F32)))
        gext[HALO:, :] = v * sg
        d = d_ref[...]
        dext[0:TM, :] = d
        dext[TM:, :] = jnp.where(i == last, 0.0, dn_ref[...])
        _phase_copies(gext, gcp)
        _phase_copies(dext, dcp)
        db_ref[...] += jnp.sum(d, axis=0, keepdims=True)
        def lanes(c, carry):
            sl = pl.ds(pl.multiple_of(c * 128, 128), 128)
            dgl[:, sl] = _conv_taps(dext, dcp, w_ref, lambda k: 30 - k, TM, sl, jnp.zeros((TM, 128), F32))
            return carry

        def lanes_w(c, carry):
            sl = pl.ds(pl.multiple_of(c * 128, 128), 128)
            ntile = TM // SUB
            dts = [d_ref[SUB * r:SUB * (r + 1), sl] for r in range(ntile)]
            for j in range(SUB):
                taps = [(q, SUB * q + j - 2) for q in range(5) if 0 <= SUB * q + j - 2 < CONV_K]
                sums = {k: None for _, k in taps}
                for rt in range(ntile + 4):
                    need = [(q, k) for q, k in taps if 0 <= rt - q < ntile]
                    if not need:
                        continue
                    src = gext[SUB * rt:SUB * (rt + 1), sl] if j == 0 else gcp[j - 1, SUB * rt:SUB * (rt + 1), sl]
                    for q, k in need:
                        prod = dts[rt - q] * src
                        sums[k] = prod if sums[k] is None else sums[k] + prod
                for _, k in taps:
                    dw_ref[SUB * k:SUB * (k + 1), sl] += sums[k]
            return carry

        lax.fori_loop(0, D_MODEL // 128, lanes, 0)
        lax.fori_loop(0, D_MODEL // 128, lanes_w, 0)
        dg = dgl[...]
        dv_ref[...] = (dg * sg).astype(BF16)
        dgt_ref[...] = (dg * v * sg * (1.0 - sg)).astype(BF16)

    return pl.pallas_call(
        body, name="conv_bwd", grid=(t // TM,),
        in_specs=[_rows(D_MODEL), _next(HALO, D_MODEL, t), _rows(D_MODEL, 0), _rows(D_MODEL, 1),
                  _prev(HALO, D_MODEL, 0), _prev(HALO, D_MODEL, 1), _full((HALO, D_MODEL))],
        out_specs=[_rows(D_MODEL), _rows(D_MODEL), _full((HALO * SUB, D_MODEL)), _full((1, D_MODEL))],
        out_shape=[jax.ShapeDtypeStruct((t, D_MODEL), BF16), jax.ShapeDtypeStruct((t, D_MODEL), BF16),
                   jax.ShapeDtypeStruct((HALO * SUB, D_MODEL), F32), jax.ShapeDtypeStruct((1, D_MODEL), F32)],
        scratch_shapes=[pltpu.VMEM((TM + HALO, D_MODEL), F32), pltpu.VMEM((TM + HALO, D_MODEL), F32),
                        pltpu.VMEM((TM, D_MODEL), F32),
                        pltpu.VMEM((SUB - 1, TM + HALO - SUB, D_MODEL), F32),
                        pltpu.VMEM((SUB - 1, TM + HALO - SUB, D_MODEL), F32)],
        compiler_params=_params())(dcv, dcv, q, q, q, q, cw)


def _column_segments(widths, ns):
    segs = []
    col = 0
    for p, wd in enumerate(widths):
        a = 0
        while a < wd:
            s, lo = divmod(col + a, ns)
            ln = min(wd - a, ns - lo)
            segs.append((p, a, a + ln, s, lo, lo + ln))
            a += ln
        col += wd
    return segs


def _in_bwd(dparts, w, x, g, dres, name):
    t = x.shape[0]
    widths = [p.shape[1] for p in dparts]
    npart = len(dparts)
    segs = _column_segments(widths, w.shape[2])

    def body(*refs):
        d_refs = refs[:npart]
        w_ref, x_ref, g_ref, r_ref, dx_ref, dg_ref, dw_ref = refs[npart:]
        i = pl.program_id(0)
        _zero_at_first(i, dg_ref, dw_ref)
        xv = x_ref[...]
        r = lax.rsqrt(jnp.mean(xv * xv, axis=-1, keepdims=True) + RMS_EPS)
        n = xv * r
        h = (n * g_ref[...]).astype(BF16)
        dh = None
        for p, lo, hi, s, slo, shi in segs:
            d = d_refs[p][:, lo:hi]
            part = _dot_nt(d, w_ref[s, :, slo:shi])
            dh = part if dh is None else dh + part
            dw_ref[s, :, slo:shi] += _dot_tn(h, d)
        dg_ref[...] += jnp.sum(dh * n, axis=0, keepdims=True)
        dn = dh * g_ref[...]
        dx_ref[...] = r_ref[...] + r * (dn - n * jnp.mean(dn * n, axis=-1, keepdims=True))

    vec = _full((1, D_MODEL))
    once = pl.BlockSpec(w.shape, lambda i: (0, 0, 0), pipeline_mode=pl.Buffered(1))
    return pl.pallas_call(
        body, name=name, grid=(t // TM_MM,),
        in_specs=[_mm_rows(wd) for wd in widths] + [once, _mm_rows(D_MODEL), vec, _mm_rows(D_MODEL)],
        out_specs=[_mm_rows(D_MODEL), vec, once],
        out_shape=[jax.ShapeDtypeStruct((t, D_MODEL), F32), jax.ShapeDtypeStruct((1, D_MODEL), F32),
                   jax.ShapeDtypeStruct(w.shape, F32)],
        compiler_params=_params())(*dparts, w, x, g, dres)


def _even_bwd_out(dx1, w, yg, yp, ys, proj, ride):
    t = dx1.shape[0]
    nstep = t // TM_MM

    def body(dx_ref, w_ref, yg_ref, yp_ref, ys_ref, z_ref, g_ref, dy_ref, dz_ref, dw_ref, theirs_ref,
             send_sems, recv_sems):
        i = pl.program_id(0)
        start, finish = _pair_split_phases(g_ref, theirs_ref, send_sems, recv_sems)
        pl.when(i == 0)(start)
        _zero_at_first(i, dw_ref)
        dxb = dx_ref[...].astype(BF16)
        dyg = _dot_nt(dxb, w_ref[...])
        dw_ref[...] += _dot_tn(yg_ref[...], dxb)
        silu_z, dsilu_z = _dsilu(z_ref[...].astype(F32))
        dy_ref[...] = (dyg * silu_z).astype(BF16)
        dz_ref[:, :POOL_W] = (dyg[:, :POOL_W] * yp_ref[...] * dsilu_z[:, :POOL_W]).astype(BF16)
        dz_ref[:, POOL_W:] = (dyg[:, POOL_W:] * ys_ref[...] * dsilu_z[:, POOL_W:]).astype(BF16)
        pl.when(i == nstep - 1)(finish)

    mat = _full((D_MODEL, D_MODEL))
    return pl.pallas_call(
        body, name="even_bwd_out", grid=(nstep,),
        in_specs=[_mm_rows(D_MODEL), mat, _mm_rows(D_MODEL), _mm_rows(POOL_W), _mm_rows(SSM_W), _mm_rows(D_MODEL, 1),
                  ANY],
        out_specs=[_mm_rows(D_MODEL), _mm_rows(D_MODEL), mat, ANY],
        out_shape=[jax.ShapeDtypeStruct((t, D_MODEL), BF16), jax.ShapeDtypeStruct((t, D_MODEL), BF16),
                   jax.ShapeDtypeStruct((D_MODEL, D_MODEL), F32), _pair_split_shape(ride)],
        scratch_shapes=_pair_split_scratch(ride),
        compiler_params=_params())(dx1, w, yg, yp, ys, proj, ride)


def _pool_bwd(dycat, proj, wp, ps):
    t = proj.shape[0]

    def body(dy_ref, dyn_ref, u_ref, h_ref, wp_ref, ps_ref, du_ref, dwp_ref, dps_ref):
        i = pl.program_id(0)
        last = t // TM - 1
        _zero_at_first(i, dwp_ref, dps_ref)
        pos = (i * TM + 1 + lax.broadcasted_iota(jnp.int32, (TM, 1), 0)).astype(F32)
        pos_ext = (i * TM + 1 + lax.broadcasted_iota(jnp.int32, (TM + POOL_HALO, 1), 0)).astype(F32)
        for g in range(4):
            sl = slice(128 * g, 128 * (g + 1))
            w = float(2 << g)
            u = u_ref[:, sl].astype(F32)
            halo = jnp.where(i == 0, 0.0, h_ref[:, sl].astype(F32))
            s = _pool_sums(jnp.concatenate([halo, u], axis=0), g, True)[POOL_HALO:, :]
            pooled = (s / jnp.minimum(pos, w) - u).astype(BF16)
            dy = dy_ref[:, sl].astype(F32)
            dps_ref[:, sl] += jnp.sum(dy * _dot(pooled, wp_ref[g]), axis=0, keepdims=True)
            dy_ext = jnp.concatenate([dy, jnp.where(i == last, 0.0, dyn_ref[:, sl].astype(F32))], axis=0)
            dmix = (dy_ext * ps_ref[:, sl]).astype(BF16)
            dwp_ref[g] += _dot_tn(pooled, dmix[:TM, :])
            dpool = _dot_nt(dmix, wp_ref[g])
            lead = _pool_sums(dpool / jnp.minimum(pos_ext, w), g, False)
            du_ref[:, sl] = (lead[:TM, :] - dpool[:TM, :]).astype(BF16)

    return pl.pallas_call(
        body, name="pool_bwd", grid=(t // TM,),
        in_specs=[_rows(POOL_W, 0), _next(POOL_HALO, POOL_W, t, 0), _rows(POOL_W, 0), _prev(POOL_HALO, POOL_W, 0),
                  _full((4, 128, 128)), _full((1, POOL_W))],
        out_specs=[_rows(POOL_W), _full((4, 128, 128)), _full((1, POOL_W))],
        out_shape=[jax.ShapeDtypeStruct((t, POOL_W), BF16), jax.ShapeDtypeStruct((4, 128, 128), F32),
                   jax.ShapeDtypeStruct((1, POOL_W), F32)],
        compiler_params=_params())(dycat, dycat, proj, proj, wp, ps)


def _ssm_bwd(dycat, proj, car_in_re, car_in_im, pm, pmt, mb_re, mb_im, p8_re, p8_im, q_re, q_im, qr_re, qr_im,
             cm_re, cm_im, dskip, wglu, ride):
    t = proj.shape[0]
    nblk = t // TM

    def body(dy_ref, u_ref, cin_re, cin_im, pm_ref, pmt_ref, mbre, mbim, p8re, p8im, qre, qim, qrre, qrim,
             cmre, cmim, d_ref, wg_ref, p_ref,
             du_ref, dmbre, dmbim, dcmre, dcmim, dare, daim, dd_ref, dwg_ref, got_ref,
             xs_re, xs_im, gs_re, gs_im, car_re, car_im, ent_re, ent_im, gcar_re, gcar_im, ysk, dysk,
             bounce, send_sems, recv_sems, local_sems):
        i = pl.program_id(0)
        start, finish = _chip_scatter_phases(p_ref, got_ref, bounce, send_sems, recv_sems, local_sems)
        pl.when(i == 0)(start)
        _zero_at_first(i, dmbre, dmbim, dcmre, dcmim, dare, daim, dd_ref, dwg_ref, gcar_re, gcar_im)
        us = _dot(pm_ref[...], u_ref[...])
        usb = us.astype(BF16)
        for j in range(4):
            xs_re[:, LCH * j:LCH * (j + 1)] = _dot(usb[:, 128 * j:128 * (j + 1)], mbre[j])
            xs_im[:, LCH * j:LCH * (j + 1)] = _dot(usb[:, 128 * j:128 * (j + 1)], mbim[j])
        car_re[...] = cin_re[0]
        car_im[...] = cin_im[0]
        _scan_fwd_block(xs_re, xs_im, p8re, p8im, qre, qim, car_re, car_im, ent_re, ent_im)
        for j in range(4):
            sl = slice(LCH * j, LCH * (j + 1))
            ysk[:, 128 * j:128 * (j + 1)] = (_dot_nt(xs_re[:, sl].astype(BF16), cmre[j])
                                             - _dot_nt(xs_im[:, sl].astype(BF16), cmim[j]))
        yvb = (ysk[...] + d_ref[...] * us).astype(BF16)
        gv = _dot(yvb, wg_ref[...])
        sg = _sig(gv[:, SSM_W:])
        dyss = _dot(pm_ref[...], dy_ref[...])
        dval = (dyss * sg).astype(BF16)
        dgate = (dyss * gv[:, :SSM_W] * sg * (1.0 - sg)).astype(BF16)
        dy = _dot_nt(dval, wg_ref[:, :SSM_W]) + _dot_nt(dgate, wg_ref[:, SSM_W:])
        dwg_ref[:, :SSM_W] += _dot_tn(yvb, dval)
        dwg_ref[:, SSM_W:] += _dot_tn(yvb, dgate)
        dd_ref[...] += jnp.sum(dy * us, axis=0, keepdims=True)
        dysk[...] = dy
        for j in range(4):
            sl = slice(LCH * j, LCH * (j + 1))
            dyj = dy[:, 128 * j:128 * (j + 1)].astype(BF16)
            gs_re[:, sl] = _dot(dyj, cmre[j])
            gs_im[:, sl] = -_dot(dyj, cmim[j])
            dcmre[j] += _dot_tn(dyj, xs_re[:, sl].astype(BF16))
            dcmim[j] -= _dot_tn(dyj, xs_im[:, sl].astype(BF16))
        row = lax.broadcasted_iota(jnp.int32, (SUB, SCAN_L), 0)
        for j in range(STATES // SCAN_L):
            sl = slice(SCAN_L * j, SCAN_L * (j + 1))
            are, aim = p8re[0:SUB, sl], -p8im[0:SUB, sl]

            def totals(k, v, sl=sl, are=are, aim=aim):
                r0 = pl.multiple_of((SEG_LEN - 2 - k) * SUB, SUB)
                vre, vim = _cmul_add(are, aim, v[0], v[1], gs_re[pl.ds(r0, SUB), sl], gs_im[pl.ds(r0, SUB), sl])
                gs_re[pl.ds(r0, SUB), sl] = vre
                gs_im[pl.ds(r0, SUB), sl] = vim
                return vre, vim

            top = (SEG_LEN - 1) * SUB
            fre, fim = lax.fori_loop(0, SEG_LEN - 1, totals,
                                     (gs_re[top:top + SUB, sl], gs_im[top:top + SUB, sl]), unroll=2)
            fre, fim, nre, nim = _segment_chain(fre, fim, qrre[:, sl], -qrim[:, sl],
                                                gcar_re[:, sl], gcar_im[:, sl], row, True)
            gcar_re[:, sl] = jnp.broadcast_to(fre[0:1, :], (SUB, SCAN_L))
            gcar_im[:, sl] = jnp.broadcast_to(fim[0:1, :], (SUB, SCAN_L))

            def fix(i2, acc, sl=sl, nre=nre, nim=nim):
                r0 = pl.multiple_of(i2 * SUB, SUB)
                rb = pl.multiple_of((SEG_LEN - 1 - i2) * SUB, SUB)
                gre, gim = _cmul_add(p8re[pl.ds(rb, SUB), sl], -p8im[pl.ds(rb, SUB), sl], nre, nim,
                                     gs_re[pl.ds(r0, SUB), sl], gs_im[pl.ds(r0, SUB), sl])
                gs_re[pl.ds(r0, SUB), sl] = gre
                gs_im[pl.ds(r0, SUB), sl] = gim
                rp = pl.multiple_of((i2 - 1) * SUB, SUB)
                xre, xim = xs_re[pl.ds(rp, SUB), sl], xs_im[pl.ds(rp, SUB), sl]
                return acc[0] + gre * xre + gim * xim, acc[1] + gim * xre - gre * xim

            g0re, g0im = _cmul_add(p8re[top:top + SUB, sl], -p8im[top:top + SUB, sl], nre, nim,
                                   gs_re[0:SUB, sl], gs_im[0:SUB, sl])
            gs_re[0:SUB, sl] = g0re
            gs_im[0:SUB, sl] = g0im
            ere, eim = ent_re[:, sl], ent_im[:, sl]
            acc0 = (dare[:, sl] + g0re * ere + g0im * eim, daim[:, sl] + g0im * ere - g0re * eim)
            are_acc, aim_acc = lax.fori_loop(1, SEG_LEN, fix, acc0, unroll=2)
            dare[:, sl] = are_acc
            daim[:, sl] = aim_acc
        for j in range(4):
            sl = slice(LCH * j, LCH * (j + 1))
            c4 = slice(128 * j, 128 * (j + 1))
            gre = gs_re[:, sl].astype(BF16)
            gim = gs_im[:, sl].astype(BF16)
            dmbre[j] += _dot_tn(usb[:, c4], gre)
            dmbim[j] += _dot_tn(usb[:, c4], gim)
            dysk[:, c4] = _dot_nt(gre, mbre[j]) + _dot_nt(gim, mbim[j]) + dysk[:, c4] * d_ref[:, c4]
        du_ref[...] = _dot(pmt_ref[...], dysk[...].astype(BF16)).astype(BF16)
        pl.when(i == nblk - 1)(finish)

    blk = (4, 128, LCH)
    pw = _full((SUB, STATES))
    p8 = _full((TM, STATES))
    perm = _full((TM, TM))
    car = pl.BlockSpec((1, SUB, STATES), lambda i: (nblk - 1 - i, 0, 0))
    big = lambda: pltpu.VMEM((TM, STATES), F32)
    small = lambda: pltpu.VMEM((SUB, STATES), F32)
    return pl.pallas_call(
        body, name="ssm_bwd", grid=(nblk,),
        in_specs=[_rows(SSM_W, 1, rev=nblk), _rows(SSM_W, 1, rev=nblk), car, car, perm, perm, _full(blk), _full(blk),
                  p8, p8, pw, pw, pw, pw, _full(blk), _full(blk), _full((1, SSM_W)), _full((SSM_W, 2 * SSM_W)), ANY],
        out_specs=[_rows(SSM_W, 0, rev=nblk), _full(blk), _full(blk), _full(blk), _full(blk), pw, pw,
                   _full((1, SSM_W)), _full((SSM_W, 2 * SSM_W)), ANY],
        out_shape=[jax.ShapeDtypeStruct((t, SSM_W), BF16)] + [jax.ShapeDtypeStruct(blk, F32)] * 4
        + [jax.ShapeDtypeStruct((SUB, STATES), F32)] * 2
        + [jax.ShapeDtypeStruct((1, SSM_W), F32), jax.ShapeDtypeStruct((SSM_W, 2 * SSM_W), F32),
           jax.ShapeDtypeStruct(ride.shape, ride.dtype)],
        scratch_shapes=[big(), big(), big(), big(), small(), small(), small(), small(), small(), small(),
                        pltpu.VMEM((TM, SSM_W), F32), pltpu.VMEM((TM, SSM_W), F32)] + _chip_scatter_scratch(ride),
        compiler_params=_params())(dycat, proj, car_in_re, car_in_im, pm, pmt, mb_re, mb_im, p8_re, p8_im,
                                   q_re, q_im, qr_re, qr_im, cm_re, cm_im, dskip, wglu, ride)


def _adamw(w, g, m, v, name):
    rows = w.shape[0]
    tr = 256 if rows % 256 == 0 else rows
    c1 = 1.0 / (1.0 - ADAM_B1 ** ADAM_STEP)
    c2 = 1.0 / (1.0 - ADAM_B2 ** ADAM_STEP)

    def body(w_ref, g_ref, m_ref, v_ref, d_ref, nm_ref, nv_ref):
        gv = g_ref[...]
        m = ADAM_B1 * m_ref[...] + (1.0 - ADAM_B1) * gv
        v = ADAM_B2 * v_ref[...] + (1.0 - ADAM_B2) * (gv * gv)
        nm_ref[...] = m
        nv_ref[...] = v
        d_ref[...] = -ADAM_LR * ((m * c1) / (jnp.sqrt(v * c2) + ADAM_EPS) + ADAM_WD * w_ref[...])

    spec = pl.BlockSpec((tr, D_MODEL), lambda i: (i, 0))
    shp = jax.ShapeDtypeStruct((rows, D_MODEL), F32)
    return pl.pallas_call(
        body, name=name, grid=(rows // tr,), in_specs=[spec] * 4, out_specs=[spec] * 3, out_shape=[shp] * 3,
        compiler_params=_params())(w, g, m, v)


def _core_index():
    return lax.axis_index("c").astype(jnp.int32).reshape(1)


def _pair_add(g, theirs, out_dtype, name):
    n, half, _ = theirs.shape
    br = 128
    nb = half // br

    def body(c_ref, a_ref, b_ref, o_ref):
        o_ref[...] = (a_ref[...] + b_ref[...]).astype(out_dtype)

    spec = pl.BlockSpec((1, br, D_MODEL), lambda i, j, c: (i, j, 0))
    grid_spec = pltpu.PrefetchScalarGridSpec(
        num_scalar_prefetch=1, grid=(n, nb),
        in_specs=[pl.BlockSpec((1, br, D_MODEL), lambda i, j, c: (i, c[0] * nb + j, 0)), spec], out_specs=spec)
    return pl.pallas_call(
        body, name=name, grid_spec=grid_spec, out_shape=jax.ShapeDtypeStruct(theirs.shape, out_dtype),
        compiler_params=_params(2))(_core_index(), g, theirs)


def _adamw_rows(w, m, v, g_mine, g_theirs, row0, br, name):
    rows = w.shape[0]
    b0 = row0 // br
    per_half = g_mine.shape[0] // br
    c1 = 1.0 / (1.0 - ADAM_B1 ** ADAM_STEP)
    c2 = 1.0 / (1.0 - ADAM_B2 ** ADAM_STEP)

    def body(c_ref, w_ref, gm_ref, gt_ref, m_ref, v_ref, g_ref, d_ref, nm_ref, nv_ref):
        gv = jnp.where((b0 + pl.program_id(0)) // per_half == c_ref[0], gm_ref[...], gt_ref[...])
        m = ADAM_B1 * m_ref[...] + (1.0 - ADAM_B1) * gv
        v = ADAM_B2 * v_ref[...] + (1.0 - ADAM_B2) * (gv * gv)
        g_ref[...] = gv
        nm_ref[...] = m
        nv_ref[...] = v
        d_ref[...] = -ADAM_LR * ((m * c1) / (jnp.sqrt(v * c2) + ADAM_EPS) + ADAM_WD * w_ref[...])

    spec = pl.BlockSpec((br, D_MODEL), lambda i, c: (i, 0))
    part = pl.BlockSpec((br, D_MODEL), lambda i, c: ((b0 + i) % per_half, 0))
    shp = jax.ShapeDtypeStruct((rows, D_MODEL), F32)
    grid_spec = pltpu.PrefetchScalarGridSpec(
        num_scalar_prefetch=1, grid=(rows // br,), in_specs=[spec, part, part, spec, spec], out_specs=[spec] * 4)
    return pl.pallas_call(
        body, name=name, grid_spec=grid_spec, out_shape=[shp] * 4,
        compiler_params=_params())(_core_index(), w, g_mine, g_theirs, m, v)


def _sum_lead(a, name):
    n, rows, _ = a.shape
    tr = 128 if rows % 128 == 0 else rows

    def body(a_ref, o_ref):
        acc = a_ref[0].astype(F32)
        for k in range(1, n):
            acc = acc + a_ref[k].astype(F32)
        o_ref[...] = acc

    return pl.pallas_call(
        body, name=name, grid=(rows // tr,),
        in_specs=[pl.BlockSpec((n, tr, D_MODEL), lambda i: (0, i, 0))],
        out_specs=pl.BlockSpec((tr, D_MODEL), lambda i: (i, 0)),
        out_shape=jax.ShapeDtypeStruct((rows, D_MODEL), F32), compiler_params=_params())(a)


ANY = pl.BlockSpec(memory_space=pl.ANY)


def _mesh_pos():
    return lax.axis_index("x"), lax.axis_index("y"), lax.axis_index("c")


def _gather_phases(in_refs, out_refs, bounces, send_sems, recv_sems, local_sems):
    na = len(in_refs)
    halves = [r.shape[0] // 2 for r in in_refs]
    ncopy = 3 * na
    x, y, c = _mesh_pos()
    me = 2 * x + y
    sibling = (x, y, 1 - c)
    chips = [(1 - x, y), (x, 1 - y), (1 - x, 1 - y)]
    ids = [2 * chip[0] + chip[1] for chip in chips]

    def piece(a, q, h):
        return out_refs[a].at[q, pl.ds(h * halves[a], halves[a]), :]

    def copy(s, a, q, h, to, src=None):
        return pltpu.make_async_remote_copy(
            src_ref=piece(a, q, h) if src is None else src, dst_ref=piece(a, q, h),
            send_sem=send_sems.at[s], recv_sem=recv_sems.at[s], device_id=to, device_id_type=MESH)

    def sends():
        return [copy(j * na + a, a, me, c, (*chip, c), src=in_refs[a].at[pl.ds(c * halves[a], halves[a]), :])
                for j, chip in enumerate(chips) for a in range(na)]

    def forwards():
        return [copy(ncopy + j * na + a, a, ids[j], c, sibling) for j in range(3) for a in range(na)]

    def stores():
        return [pltpu.make_async_copy(bounces[a], out_refs[a].at[me], local_sems.at[na + a]) for a in range(na)]

    def start():
        loads = [pltpu.make_async_copy(in_refs[a], bounces[a], local_sems.at[a]) for a in range(na)]
        for cp in loads:
            cp.start()
        for cp in sends():
            cp.start()
        for ld, st in zip(loads, stores()):
            ld.wait()
            st.start()

    def forward():
        fwd = forwards()
        for j in range(3):
            for a in range(na):
                copy(j * na + a, a, ids[j], c, (x, y, c)).wait_recv()
                fwd[j * na + a].start()

    def finish():
        for j in range(3):
            for a in range(na):
                copy(ncopy + j * na + a, a, ids[j], 1 - c, (x, y, c)).wait_recv()
        for cp in sends() + forwards():
            cp.wait_send()
        for cp in stores():
            cp.wait()

    return start, forward, finish


def _gather_scratch(arrs):
    ncopy = 3 * len(arrs)
    return ([pltpu.VMEM(a.shape, a.dtype) for a in arrs]
            + [pltpu.SemaphoreType.DMA((2 * ncopy,)), pltpu.SemaphoreType.DMA((2 * ncopy,)),
               pltpu.SemaphoreType.DMA((2 * len(arrs),))])


def _gather_weights(arrs):
    na = len(arrs)

    def body(*refs):
        start, forward, finish = _gather_phases(refs[:na], refs[na:2 * na], refs[2 * na:3 * na], *refs[3 * na:])
        start()
        forward()
        finish()

    return pl.pallas_call(
        body, name="gather_weights", in_specs=[ANY] * na, out_specs=[ANY] * na,
        out_shape=[jax.ShapeDtypeStruct((4,) + a.shape, a.dtype) for a in arrs],
        scratch_shapes=_gather_scratch(arrs),
        compiler_params=pltpu.CompilerParams(vmem_limit_bytes=VMEM_LIMIT),
    )(*arrs)


def _gather_all(v):
    def body(v_ref, o_ref, bounce, send_sems, recv_sems, local_sems):
        x, y, c = _mesh_pos()
        sibling = (x, y, 1 - c)
        chips = [(1 - x, y), (x, 1 - y), (1 - x, 1 - y)]

        def blk(px, py, pc):
            return o_ref.at[4 * px + 2 * py + pc]

        def copy(k, block, to, src=None):
            return pltpu.make_async_remote_copy(
                src_ref=blk(*block) if src is None else src, dst_ref=blk(*block),
                send_sem=send_sems.at[k], recv_sem=recv_sems.at[k], device_id=to, device_id_type=MESH)

        load = pltpu.make_async_copy(v_ref, bounce, local_sems.at[0])
        store = pltpu.make_async_copy(bounce, blk(x, y, c), local_sems.at[1])
        load.start()
        first = [copy(0, (x, y, c), sibling, src=v_ref)]
        first += [copy(1 + j, (x, y, c), (*chip, c), src=v_ref) for j, chip in enumerate(chips)]
        for cp in first:
            cp.start()
        load.wait()
        store.start()
        passed = [copy(4 + j, (*chip, c), sibling) for j, chip in enumerate(chips)]
        for j, chip in enumerate(chips):
            copy(1 + j, (*chip, c), (x, y, c)).wait_recv()
            passed[j].start()
        copy(0, (x, y, 1 - c), (x, y, c)).wait_recv()
        for j, chip in enumerate(chips):
            copy(4 + j, (*chip, 1 - c), (x, y, c)).wait_recv()
        for cp in first + passed:
            cp.wait_send()
        store.wait()

    return pl.pallas_call(
        body, name="gather_all", in_specs=[ANY], out_specs=ANY,
        out_shape=jax.ShapeDtypeStruct((8,) + v.shape, v.dtype),
        scratch_shapes=[pltpu.VMEM(v.shape, v.dtype), pltpu.SemaphoreType.DMA((7,)), pltpu.SemaphoreType.DMA((7,)),
                        pltpu.SemaphoreType.DMA((2,))],
    )(v)


def _pair_split_phases(g_ref, theirs_ref, send_sems, recv_sems):
    n, rows, _ = g_ref.shape
    half = rows // 2
    ch = half // COMM_CHUNKS
    x, y, c = _mesh_pos()

    def gives():
        return [pltpu.make_async_remote_copy(
            src_ref=g_ref.at[q, pl.ds((1 - c) * half + k * ch, ch), :],
            dst_ref=theirs_ref.at[q, pl.ds(k * ch, ch), :],
            send_sem=send_sems.at[q * COMM_CHUNKS + k], recv_sem=recv_sems.at[q * COMM_CHUNKS + k],
            device_id=(x, y, 1 - c), device_id_type=MESH) for q in range(n) for k in range(COMM_CHUNKS)]

    def start():
        for cp in gives():
            cp.start()

    def finish():
        for cp in gives():
            cp.wait()

    return start, finish


def _pair_split_scratch(g):
    return [pltpu.SemaphoreType.DMA((g.shape[0] * COMM_CHUNKS,)), pltpu.SemaphoreType.DMA((g.shape[0] * COMM_CHUNKS,))]


def _pair_split_shape(g):
    return jax.ShapeDtypeStruct((g.shape[0], g.shape[1] // 2, D_MODEL), g.dtype)


def _pair_split(g, name):
    def body(g_ref, theirs_ref, send_sems, recv_sems):
        start, finish = _pair_split_phases(g_ref, theirs_ref, send_sems, recv_sems)
        start()
        finish()

    return pl.pallas_call(
        body, name=name, in_specs=[ANY], out_specs=ANY, out_shape=_pair_split_shape(g),
        scratch_shapes=_pair_split_scratch(g))(g)


def _chip_scatter_phases(p_ref, o_ref, bounce, send_sems, recv_sems, local_sems):
    x, y, c = _mesh_pos()
    me = 2 * x + y
    chips = [(1 - x, y), (x, 1 - y), (1 - x, 1 - y)]

    def keep():
        return pltpu.make_async_copy(bounce, o_ref.at[me], local_sems.at[1])

    def sends():
        return [pltpu.make_async_remote_copy(
            src_ref=p_ref.at[2 * chip[0] + chip[1]], dst_ref=o_ref.at[me],
            send_sem=send_sems.at[j], recv_sem=recv_sems.at[j], device_id=(*chip, c), device_id_type=MESH)
            for j, chip in enumerate(chips)]

    def start():
        load = pltpu.make_async_copy(p_ref.at[me], bounce, local_sems.at[0])
        load.start()
        for cp in sends():
            cp.start()
        load.wait()
        keep().start()

    def finish():
        for j, chip in enumerate(chips):
            q = 2 * chip[0] + chip[1]
            pltpu.make_async_remote_copy(
                src_ref=p_ref.at[q], dst_ref=o_ref.at[q], send_sem=send_sems.at[j], recv_sem=recv_sems.at[j],
                device_id=(*chip, c), device_id_type=MESH).wait_recv()
        for cp in sends():
            cp.wait_send()
        keep().wait()

    return start, finish


def _chip_scatter_scratch(p):
    return [pltpu.VMEM(p.shape[1:], p.dtype), pltpu.SemaphoreType.DMA((3,)), pltpu.SemaphoreType.DMA((3,)),
            pltpu.SemaphoreType.DMA((2,))]


def _chip_scatter(p, name):
    def body(p_ref, o_ref, bounce, send_sems, recv_sems, local_sems):
        start, finish = _chip_scatter_phases(p_ref, o_ref, bounce, send_sems, recv_sems, local_sems)
        start()
        finish()

    return pl.pallas_call(
        body, name=name, in_specs=[ANY], out_specs=ANY, out_shape=jax.ShapeDtypeStruct(p.shape, p.dtype),
        scratch_shapes=_chip_scatter_scratch(p))(p)


def _pair_join(r, name):
    rows = r.shape[0]
    ch = rows // COMM_CHUNKS

    def body(r_ref, o_ref, send_sems, recv_sems):
        x, y, c = _mesh_pos()
        gives = [pltpu.make_async_remote_copy(
            src_ref=r_ref.at[pl.ds(k * ch, ch), :], dst_ref=o_ref.at[pl.ds(k * ch, ch), :],
            send_sem=send_sems.at[k], recv_sem=recv_sems.at[k], device_id=(x, y, 1 - c), device_id_type=MESH)
            for k in range(COMM_CHUNKS)]
        for cp in gives:
            cp.start()
        for cp in gives:
            cp.wait()

    return pl.pallas_call(
        body, name=name, in_specs=[ANY], out_specs=ANY, out_shape=jax.ShapeDtypeStruct(r.shape, r.dtype),
        scratch_shapes=[pltpu.SemaphoreType.DMA((COMM_CHUNKS,)), pltpu.SemaphoreType.DMA((COMM_CHUNKS,))],
    )(r)


SHARD_BIG = (("even_w_in", (1024, 512)), ("ssm_w_glu", (512, 256)), ("even_w_out", (256, 1024)),
             ("odd_w_in", (1024, 768)), ("odd_w_out", (256, 1024)))
SHARD_SMALL = (("odd_norm", 1), ("conv_w", CONV_K), ("conv_b", 1), ("conv_ln_g", 1), ("conv_ln_b", 1))
REP_NAMES = (("even_norm", (1024,)), ("pool_w", (4, 128, 128)), ("pool_scale", (512,)), ("ssm_log_dt", (32,)),
             ("ssm_a_re", (32, 64)), ("ssm_a_im", (32, 64)), ("ssm_b_re", (32, 64, 16)), ("ssm_b_im", (32, 64, 16)),
             ("ssm_c_re", (32, 16, 64)), ("ssm_c_im", (32, 16, 64)), ("ssm_d", (512,)), ("final_norm", (1024,)))


def _pack_rep(d):
    flat = jnp.concatenate([d[n].reshape(-1) for n, _ in REP_NAMES])
    return jnp.pad(flat, (0, REP_ROWS * D_MODEL - flat.shape[0])).reshape(REP_ROWS, D_MODEL)


def _unpack_rep(buf):
    flat = buf.reshape(-1)
    out = {}
    off = 0
    for n, shp in REP_NAMES:
        size = 1
        for s in shp:
            size *= s
        out[n] = flat[off:off + size].reshape(shp)
        off += size
    return out


def _cols_split(full, cols):
    rows = full.shape[0]
    return full.reshape(rows, 4, cols).transpose(1, 0, 2).reshape(4, -1, D_MODEL)


def _block_diag(a):
    a = a.reshape(4, 8, GROUP_DIM, N_STATE)
    eye = jnp.eye(8, dtype=a.dtype)
    return (a[:, :, :, None, :] * eye[None, :, None, :, None]).reshape(4, 128, LCH)


def _block_diag_take(m):
    m = m.reshape(4, 8, GROUP_DIM, 8, N_STATE)
    eye = jnp.eye(8, dtype=m.dtype)
    return jnp.sum(m * eye[None, :, None, :, None], axis=3).reshape(N_GROUPS, GROUP_DIM, N_STATE)


def _ssm_discretise(log_dt, a_re, a_im, b_re, b_im):
    dt = jnp.exp(log_dt)[:, None]
    mag = jnp.exp(a_re * dt)
    ang = a_im * dt
    abar_re = mag * jnp.cos(ang)
    abar_im = mag * jnp.sin(ang)
    den = a_re * a_re + a_im * a_im
    nr = abar_re - 1.0
    ni = abar_im
    k_re = (nr * a_re + ni * a_im) / den
    k_im = (ni * a_re - nr * a_im) / den
    bb_re = k_re[..., None] * b_re - k_im[..., None] * b_im
    bb_im = k_re[..., None] * b_im + k_im[..., None] * b_re
    return abar_re, abar_im, bb_re, bb_im


def _scan_tables(log_dt, a_re, a_im):
    dt = jnp.exp(log_dt)[:, None]
    lam_re = (a_re * dt).reshape(1, STATES)
    lam_im = (a_im * dt).reshape(1, STATES)

    def powers(k):
        mag = jnp.exp(k * lam_re)
        return mag * jnp.cos(k * lam_im), mag * jnp.sin(k * lam_im)

    p_re, p_im = powers((1 + jnp.arange(TM) // SUB).astype(F32)[:, None])
    q_re, q_im = powers((SEG_LEN * (1 + jnp.arange(SUB))).astype(F32)[:, None])
    return p_re, p_im, q_re, q_im


def _local_step(x, tgt, w, odd_shards=()):
    row = lambda a: a.reshape(1, -1)
    e_w_in, e_w_out, wglu = w["even_w_in"], w["even_w_out"], w["ssm_w_glu"]
    wp = w["pool_w"].astype(BF16)
    ssm_in = (w["ssm_log_dt"], w["ssm_a_re"], w["ssm_a_im"], w["ssm_b_re"], w["ssm_b_im"])
    (abar_re, abar_im, bb_re, bb_im), ssm_vjp = jax.vjp(_ssm_discretise, *ssm_in)
    mb_re = _block_diag(bb_re.transpose(0, 2, 1)).astype(BF16)
    mb_im = _block_diag(bb_im.transpose(0, 2, 1)).astype(BF16)
    cm_re = _block_diag(w["ssm_c_re"]).astype(BF16)
    cm_im = _block_diag(w["ssm_c_im"]).astype(BF16)
    p8_re, p8_im, q_re, q_im = _scan_tables(w["ssm_log_dt"], w["ssm_a_re"], w["ssm_a_im"])
    qr_re, qr_im = q_re[::-1], q_im[::-1]
    pm = _perm_matrix()
    pmt = pm.T
    g0, gf = row(w["even_norm"]), row(w["final_norm"])
    ps, dskip = row(w["pool_scale"]), row(w["ssm_d"])

    proj = _norm_in(x, g0, e_w_in, "even_in")
    yp = _pool_fwd(proj, wp, ps)
    (ys, car_re, car_im), gathered = _ssm_fwd(proj, pm, pmt, mb_re, mb_im, p8_re, p8_im, q_re, q_im, cm_re, cm_im,
                                              dskip, wglu, ride=odd_shards)
    if odd_shards:
        w = {**w, **_odd_weights(*gathered)}
    o_w_in, o_w_out, sm = w["odd_w_in"], w["odd_w_out"], w["odd_small"]
    cw = sm[1:1 + HALO]
    g1, cb, lg, lb = sm[0:1], sm[32:33], sm[33:34], sm[34:35]
    x1, yg = _even_out(yp, ys, proj, x, e_w_out)
    q = _norm_in(x1, g1, o_w_in, "odd_in")
    y2, cv = _conv_fwd(q, cw, cb, lg, lb)
    dx2, loss_lanes, d_gf = _odd_out_loss(y2, x1, o_w_out, gf, tgt)

    dcv, dz2, d_o_w_out, d_lg, d_lb = _odd_bwd_out(dx2, o_w_out, y2, cv, q, lg, lb)
    dval, dgate, d_cw, d_cb = _conv_bwd(dcv, q, cw)
    dx1, d_g1, d_o_w_in = _in_bwd([dval, dgate, dz2], o_w_in, x1, g1, dx2, "odd_in_bwd")
    g_odd = _pack_odd_grads({
        "odd_w_in": d_o_w_in, "odd_w_out": d_o_w_out, "odd_norm": d_g1.reshape(-1),
        "conv_w": d_cw.reshape(HALO, SUB, D_MODEL).sum(axis=1)[:CONV_K], "conv_b": d_cb.reshape(-1),
        "conv_ln_g": d_lg.reshape(-1), "conv_ln_b": d_lb.reshape(-1)})
    dycat, dz, d_e_w_out, theirs_odd = _even_bwd_out(dx1, e_w_out, yg, yp, ys, proj, g_odd)
    sums_odd = _pair_add(g_odd, theirs_odd, BF16, "pair_add_odd")
    dup, d_wp, d_ps = _pool_bwd(dycat, proj, wp, ps)
    (dus, d_mb_re, d_mb_im, d_cm_re, d_cm_im, da_re, da_im, d_dskip, d_wglu, got_odd) = _ssm_bwd(
        dycat, proj, car_re, car_im, pm, pmt, mb_re, mb_im, p8_re, p8_im, q_re, q_im, qr_re, qr_im,
        cm_re, cm_im, dskip, wglu, sums_odd)
    dx, d_g0, d_e_w_in = _in_bwd([dup, dus, dz], e_w_in, x, g0, dx1, "even_in_bwd")

    d_abar_re = jnp.sum(da_re, axis=0).reshape(N_GROUPS, N_STATE)
    d_abar_im = jnp.sum(da_im, axis=0).reshape(N_GROUPS, N_STATE)
    d_bb_re = _block_diag_take(d_mb_re).transpose(0, 2, 1)
    d_bb_im = _block_diag_take(d_mb_im).transpose(0, 2, 1)
    d_log_dt, d_a_re, d_a_im, d_b_re, d_b_im = ssm_vjp((d_abar_re, d_abar_im, d_bb_re, d_bb_im))

    grads = {
        "even_norm": d_g0.reshape(-1), "even_w_in": d_e_w_in, "pool_w": d_wp, "pool_scale": d_ps.reshape(-1),
        "ssm_log_dt": d_log_dt, "ssm_a_re": d_a_re, "ssm_a_im": d_a_im, "ssm_b_re": d_b_re, "ssm_b_im": d_b_im,
        "ssm_c_re": _block_diag_take(d_cm_re), "ssm_c_im": _block_diag_take(d_cm_im),
        "ssm_d": d_dskip.reshape(-1), "ssm_w_glu": d_wglu, "even_w_out": d_e_w_out, "final_norm": d_gf.reshape(-1),
    }
    return jnp.sum(loss_lanes), dx, grads, got_odd


WEIGHT_NAMES = ("even_norm", "even_w_in", "pool_w", "pool_scale", "ssm_log_dt", "ssm_a_re", "ssm_a_im",
                "ssm_b_re", "ssm_b_im", "ssm_c_re", "ssm_c_im", "ssm_d", "ssm_w_glu", "even_w_out", "odd_norm",
                "odd_w_in", "conv_w", "conv_b", "conv_ln_g", "conv_ln_b", "odd_w_out", "final_norm")
SHARDED = tuple(n for n, _ in SHARD_BIG) + tuple(n for n, _ in SHARD_SMALL)


SMALL_ROWS = 64


def _even_weights(shard, rep):
    g_in, g_glu, g_eout = _gather_weights([shard[n].astype(BF16) for n in ("even_w_in", "ssm_w_glu", "even_w_out")])
    w = dict(rep)
    w["even_w_in"] = g_in
    w["ssm_w_glu"] = g_glu.transpose(1, 0, 2).reshape(SSM_W, 2 * SSM_W)
    w["even_w_out"] = g_eout.reshape(D_MODEL, D_MODEL)
    return w


def _odd_shards(shard):
    small = jnp.concatenate([shard[n].reshape(r, 256) for n, r in SHARD_SMALL], axis=0)
    small = jnp.pad(small, ((0, SMALL_ROWS - small.shape[0]), (0, 0)))
    return [shard["odd_w_in"].astype(BF16), shard["odd_w_out"].astype(BF16), small]


def _odd_weights(g_oin, g_oout, g_small):
    return {"odd_w_in": g_oin, "odd_w_out": g_oout.reshape(D_MODEL, D_MODEL),
            "odd_small": g_small.transpose(1, 0, 2).reshape(SMALL_ROWS, D_MODEL)}


def _pack_small(d):
    small = jnp.concatenate([d[n].reshape(r, -1) for n, r in SHARD_SMALL], axis=0)
    if small.shape[1] == D_MODEL:
        small = small.reshape(35, 4, 256).transpose(1, 0, 2)
    small = small.reshape(-1, 35 * 256)
    small = jnp.pad(small, ((0, 0), (0, ROWS_SMALL * D_MODEL - 35 * 256)))
    return small.reshape(-1, ROWS_SMALL, D_MODEL)


def _unpack_small(buf):
    small = buf.reshape(-1)[:35 * 256].reshape(35, 256)
    out = {}
    off = 0
    for n, r in SHARD_SMALL:
        out[n] = small[off:off + r].reshape((r, 256) if r > 1 else (256,))
        off += r
    return out


EVEN_PACK = (("even_w_in", 0, 512), ("ssm_w_glu", 512, 128), ("even_w_out", 640, 256))
ROWS_EVEN = 1024
ODD_PACK = (("odd_w_in", 0, 768), ("odd_w_out", 768, 256))
ODD_SMALL_ROW = 1024
ROWS_ODD = 1280


def _pack_even_grads(g):
    parts = [g["even_w_in"].reshape(4, -1, D_MODEL), _cols_split(g["ssm_w_glu"], 256),
             g["even_w_out"].reshape(4, -1, D_MODEL), jnp.zeros((4, ROWS_EVEN - 896, D_MODEL), F32)]
    return jnp.concatenate(parts, axis=1)


def _pack_odd_grads(g):
    parts = [g["odd_w_in"].reshape(4, -1, D_MODEL), g["odd_w_out"].reshape(4, -1, D_MODEL), _pack_small(g),
             jnp.zeros((4, ROWS_ODD - ODD_SMALL_ROW - ROWS_SMALL, D_MODEL), F32)]
    return jnp.concatenate(parts, axis=1)


def kernel(x, even_norm, even_w_in, pool_w, pool_scale, ssm_log_dt, ssm_a_re, ssm_a_im, ssm_b_re, ssm_b_im, ssm_c_re, ssm_c_im, ssm_d, ssm_w_glu, even_w_out, odd_norm, odd_w_in, conv_w, conv_b, conv_ln_g, conv_ln_b, odd_w_out, final_norm, loss_target, m_even_norm, m_even_w_in, m_pool_w, m_pool_scale, m_ssm_log_dt, m_ssm_a_re, m_ssm_a_im, m_ssm_b_re, m_ssm_b_im, m_ssm_c_re, m_ssm_c_im, m_ssm_d, m_ssm_w_glu, m_even_w_out, m_odd_norm, m_odd_w_in, m_conv_w, m_conv_b, m_conv_ln_g, m_conv_ln_b, m_odd_w_out, m_final_norm, v_even_norm, v_even_w_in, v_pool_w, v_pool_scale, v_ssm_log_dt, v_ssm_a_re, v_ssm_a_im, v_ssm_b_re, v_ssm_b_im, v_ssm_c_re, v_ssm_c_im, v_ssm_d, v_ssm_w_glu, v_even_w_out, v_odd_norm, v_odd_w_in, v_conv_w, v_conv_b, v_conv_ln_g, v_conv_ln_b, v_odd_w_out, v_final_norm):
    ws = dict(zip(WEIGHT_NAMES, (even_norm, even_w_in, pool_w, pool_scale, ssm_log_dt, ssm_a_re, ssm_a_im, ssm_b_re,
                                 ssm_b_im, ssm_c_re, ssm_c_im, ssm_d, ssm_w_glu, even_w_out, odd_norm, odd_w_in,
                                 conv_w, conv_b, conv_ln_g, conv_ln_b, odd_w_out, final_norm)))
    ms = dict(zip(WEIGHT_NAMES, (m_even_norm, m_even_w_in, m_pool_w, m_pool_scale, m_ssm_log_dt, m_ssm_a_re,
                                 m_ssm_a_im, m_ssm_b_re, m_ssm_b_im, m_ssm_c_re, m_ssm_c_im, m_ssm_d, m_ssm_w_glu,
                                 m_even_w_out, m_odd_norm, m_odd_w_in, m_conv_w, m_conv_b, m_conv_ln_g, m_conv_ln_b,
                                 m_odd_w_out, m_final_norm)))
    vs = dict(zip(WEIGHT_NAMES, (v_even_norm, v_even_w_in, v_pool_w, v_pool_scale, v_ssm_log_dt, v_ssm_a_re,
                                 v_ssm_a_im, v_ssm_b_re, v_ssm_b_im, v_ssm_c_re, v_ssm_c_im, v_ssm_d, v_ssm_w_glu,
                                 v_even_w_out, v_odd_norm, v_odd_w_in, v_conv_w, v_conv_b, v_conv_ln_g, v_conv_ln_b,
                                 v_odd_w_out, v_final_norm)))
    lead = {n: a.shape for n, a in ws.items()}
    drop = lambda d: {n: (a[0] if n != "final_norm" else a) for n, a in d.items()}
    ws, ms, vs = drop(ws), drop(ms), drop(vs)

    shard = {n: ws[n] for n in SHARDED}
    rep = {n: ws[n] for n, _ in REP_NAMES}
    loss_part, grad_x, grads, got_odd = _local_step(x[0], loss_target[0], _even_weights(shard, rep),
                                                    _odd_shards(shard))
    loss = lax.psum(loss_part, ("x", "y", "c"))

    odd_mine = _sum_lead(got_odd, "chip_sum_odd")
    odd_theirs = _pair_join(odd_mine, "pair_join_odd")
    g_even = _pack_even_grads(grads)
    got_even = _chip_scatter(_pair_add(g_even, _pair_split(g_even, "pair_split_even"), BF16, "pair_add_even"),
                             "chip_scatter_even")
    even_mine = _sum_lead(got_even, "chip_sum_even")
    even_theirs = _pair_join(even_mine, "pair_join_even")
    outs = [{}, {}, {}, {}]
    for pack, mine, theirs in ((EVEN_PACK, even_mine, even_theirs), (ODD_PACK, odd_mine, odd_theirs)):
        for n, row0, rows in pack:
            view = lambda a: a.reshape(rows, D_MODEL)
            res = _adamw_rows(view(ws[n]), view(ms[n]), view(vs[n]), mine, theirs, row0, 128, "adamw_" + n)
            for o, r in zip(outs, res):
                o[n] = r
    small = lambda d: _pack_small({n: d[n] for n, _ in SHARD_SMALL})[0]
    res = _adamw_rows(small(ws), small(ms), small(vs), odd_mine, odd_theirs, ODD_SMALL_ROW, ROWS_SMALL, "adamw_small")
    for o, r in zip(outs, res):
        o.update(_unpack_small(r))
    g_rep = _sum_lead(_gather_all(_pack_rep({n: grads[n] for n, _ in REP_NAMES})), "rep_sum")
    res = _adamw(_pack_rep(rep), g_rep, _pack_rep({n: ms[n] for n, _ in REP_NAMES}),
                 _pack_rep({n: vs[n] for n, _ in REP_NAMES}), "adamw_rep")
    for o, r in zip(outs, (g_rep,) + tuple(res)):
        o.update(_unpack_rep(r))

    leaves = [[o[n].reshape(lead[n]) for n in WEIGHT_NAMES] for o in outs]
    return (loss, grad_x[None], *leaves[0], *leaves[1], *leaves[2], *leaves[3])
```

```python
import functools

import jax
import jax.numpy as jnp
from jax import lax
from jax.experimental import pallas as pl
from jax.experimental.pallas import tpu as pltpu

F32 = jnp.float32
BF16 = jnp.bfloat16
MESH = pl.DeviceIdType.MESH

D_MODEL = 1024
RMS_EPS = 1e-6
LN_EPS = 1e-5
N_GROUPS = 32
GROUP_DIM = 16
N_STATE = 64
STATES = N_GROUPS * N_STATE
SSM_W = 512
POOL_W = 512
CONV_K = 31
HALO = 32
POOL_HALO = 16

ADAM_LR = 0.001
ADAM_B1 = 0.9
ADAM_B2 = 0.999
ADAM_EPS = 1e-08
ADAM_WD = 0.01
ADAM_STEP = 10

TM = 256
TM_MM = 512
SUB = 8
LCH = 512
SCAN_L = 1024
VMEM_LIMIT = 56 * 1024 * 1024

ROWS_SMALL = 16
REP_ROWS = 200
COMM_CHUNKS = 4


def _params(n_axes=1):
    return pltpu.CompilerParams(dimension_semantics=("arbitrary",) * n_axes, vmem_limit_bytes=VMEM_LIMIT)


def _rows(w, cb=0, rev=None, tm=TM):
    if rev is None:
        return pl.BlockSpec((tm, w), lambda i: (i, cb))
    return pl.BlockSpec((tm, w), lambda i: (rev - 1 - i, cb))


def _mm_rows(w, cb=0):
    return _rows(w, cb, tm=TM_MM)


def _full(shape):
    n = len(shape)
    return pl.BlockSpec(shape, lambda i: (0,) * n)


def _prev(hr, w, cb=0, tm=TM):
    r = tm // hr
    return pl.BlockSpec((hr, w), lambda i: (jnp.maximum(i * r - 1, 0), cb))


def _next(hr, w, nrows, cb=0, tm=TM):
    r = tm // hr
    last = nrows // hr - 1
    return pl.BlockSpec((hr, w), lambda i: (jnp.minimum((i + 1) * r, last), cb))


def _dot(a, b):
    return jnp.dot(a, b, preferred_element_type=F32)


def _dot_nt(a, b):
    return lax.dot_general(a, b, (((1,), (1,)), ((), ())), preferred_element_type=F32)


def _dot_tn(a, b):
    return lax.dot_general(a, b, (((0,), (0,)), ((), ())), preferred_element_type=F32)


def _sig(x):
    return 1.0 / (1.0 + jnp.exp(-x))


def _zero_at_first(i, *refs):
    @pl.when(i == 0)
    def _():
        for r in refs:
            r[...] = jnp.zeros_like(r)


def _norm_in(x, g, w, name, ride=()):
    t, ns = x.shape[0], w.shape[2]
    n = 4 * ns
    ng = len(ride)
    nstep = t // TM_MM

    def body(x_ref, g_ref, w_ref, *rest):
        o_ref = rest[ng]
        i = pl.program_id(0)
        if ng:
            start, forward, finish = _gather_phases(rest[:ng], rest[ng + 1:2 * ng + 1], rest[2 * ng + 1:3 * ng + 1],
                                                    *rest[3 * ng + 1:])
            pl.when(i == 0)(start)
            pl.when(i == nstep // 2)(forward)
        xv = x_ref[...]
        r = lax.rsqrt(jnp.mean(xv * xv, axis=-1, keepdims=True) + RMS_EPS)
        h = (xv * r * g_ref[...]).astype(BF16)
        for s in range(4):
            o_ref[:, s * ns:(s + 1) * ns] = _dot(h, w_ref[s]).astype(BF16)
        if ng:
            pl.when(i == nstep - 1)(finish)

    res = pl.pallas_call(
        body, name=name, grid=(nstep,),
        in_specs=[_mm_rows(D_MODEL), _full((1, D_MODEL)), _full(w.shape)] + [ANY] * ng,
        out_specs=[_mm_rows(n)] + [ANY] * ng,
        out_shape=[jax.ShapeDtypeStruct((t, n), BF16)] + [jax.ShapeDtypeStruct((4,) + a.shape, a.dtype) for a in ride],
        scratch_shapes=_gather_scratch(ride) if ng else [],
        compiler_params=_params())(x, g, w, *ride)
    return res[0], res[1:]


def _pool_sums(ext, g, forward):
    n = ext.shape[0]
    s = ext
    for step in range(g + 1):
        k = 1 << step
        s = s + pltpu.roll(s, k if forward else n - k, 0)
    return s


def _pool_fwd(proj, wp, ps):
    t = proj.shape[0]

    def body(u_ref, h_ref, wp_ref, ps_ref, y_ref):
        i = pl.program_id(0)
        pos = (i * TM + 1 + lax.broadcasted_iota(jnp.int32, (TM, 1), 0)).astype(F32)
        for g in range(4):
            sl = slice(128 * g, 128 * (g + 1))
            u = u_ref[:, sl].astype(F32)
            halo = jnp.where(i == 0, 0.0, h_ref[:, sl].astype(F32))
            s = _pool_sums(jnp.concatenate([halo, u], axis=0), g, True)[POOL_HALO:, :]
            pooled = s / jnp.minimum(pos, float(2 << g)) - u
            y_ref[:, sl] = _dot(pooled.astype(BF16), wp_ref[g]) * ps_ref[:, sl]

    return pl.pallas_call(
        body, name="pool_fwd", grid=(t // TM,),
        in_specs=[_rows(POOL_W, 0), _prev(POOL_HALO, POOL_W, 0), _full((4, 128, 128)), _full((1, POOL_W))],
        out_specs=_rows(POOL_W), out_shape=jax.ShapeDtypeStruct((t, POOL_W), F32),
        compiler_params=_params())(proj, proj, wp, ps)


SEG_LEN = TM // SUB


def _perm_matrix():
    p = jnp.arange(TM)
    src = (p % SUB) * SEG_LEN + p // SUB
    return (src[:, None] == jnp.arange(TM)[None, :]).astype(BF16)


def _cmul_add(are, aim, vre, vim, bre, bim):
    return are * vre - aim * vim + bre, are * vim + aim * vre + bim


def _segment_chain(ere, eim, qre, qim, cin_re, cin_im, row, up):
    for sh in (1, 2, 4):
        mre, mim = (qre[SUB - sh:SUB - sh + 1, :], qim[SUB - sh:SUB - sh + 1, :]) if up else \
                   (qre[sh - 1:sh, :], qim[sh - 1:sh, :])
        keep = (row < SUB - sh) if up else (row >= sh)
        sre = jnp.where(keep, pltpu.roll(ere, SUB - sh if up else sh, 0), 0.0)
        sim = jnp.where(keep, pltpu.roll(eim, SUB - sh if up else sh, 0), 0.0)
        ere, eim = _cmul_add(mre, mim, sre, sim, ere, eim)
    ere, eim = _cmul_add(qre, qim, cin_re, cin_im, ere, eim)
    keep = (row < SUB - 1) if up else (row >= 1)
    ent_re = jnp.where(keep, pltpu.roll(ere, SUB - 1 if up else 1, 0), cin_re)
    ent_im = jnp.where(keep, pltpu.roll(eim, SUB - 1 if up else 1, 0), cin_im)
    return ere, eim, ent_re, ent_im


def _scan_fwd_block(xs_re, xs_im, p8_re, p8_im, q_re, q_im, car_re, car_im, ent_re_ref, ent_im_ref):
    row = lax.broadcasted_iota(jnp.int32, (SUB, SCAN_L), 0)
    for j in range(STATES // SCAN_L):
        sl = slice(SCAN_L * j, SCAN_L * (j + 1))
        are, aim = p8_re[0:SUB, sl], p8_im[0:SUB, sl]

        def totals(i, v, sl=sl, are=are, aim=aim):
            r0 = pl.multiple_of(i * SUB, SUB)
            vre, vim = _cmul_add(are, aim, v[0], v[1], xs_re[pl.ds(r0, SUB), sl], xs_im[pl.ds(r0, SUB), sl])
            xs_re[pl.ds(r0, SUB), sl] = vre
            xs_im[pl.ds(r0, SUB), sl] = vim
            return vre, vim

        ere, eim = lax.fori_loop(1, SEG_LEN, totals, (xs_re[0:SUB, sl], xs_im[0:SUB, sl]), unroll=2)
        ere, eim, cre, cim = _segment_chain(ere, eim, q_re[:, sl], q_im[:, sl],
                                            car_re[:, sl], car_im[:, sl], row, False)
        car_re[:, sl] = jnp.broadcast_to(ere[SUB - 1:SUB, :], (SUB, SCAN_L))
        car_im[:, sl] = jnp.broadcast_to(eim[SUB - 1:SUB, :], (SUB, SCAN_L))
        if ent_re_ref is not None:
            ent_re_ref[:, sl] = cre
            ent_im_ref[:, sl] = cim

        def fix(i, c, sl=sl, cre=cre, cim=cim):
            r0 = pl.multiple_of(i * SUB, SUB)
            vre, vim = _cmul_add(p8_re[pl.ds(r0, SUB), sl], p8_im[pl.ds(r0, SUB), sl], cre, cim,
                                 xs_re[pl.ds(r0, SUB), sl], xs_im[pl.ds(r0, SUB), sl])
            xs_re[pl.ds(r0, SUB), sl] = vre
            xs_im[pl.ds(r0, SUB), sl] = vim
            return c

        lax.fori_loop(0, SEG_LEN, fix, 0, unroll=2)


def _unpermute(pmt_ref, v):
    hi = v.astype(BF16)
    lo = (v - hi.astype(F32)).astype(BF16)
    return _dot(pmt_ref[...], hi) + _dot(pmt_ref[...], lo)


def _ssm_fwd(proj, pm, pmt, mb_re, mb_im, p8_re, p8_im, q_re, q_im, cm_re, cm_im, dskip, wglu, ride=()):
    t = proj.shape[0]
    nblk = t // TM

    ng = len(ride)

    def body(u_ref, pm_ref, pmt_ref, mbre, mbim, p8re, p8im, qre, qim, cmre, cmim, d_ref, wg_ref, *rest):
        ride_in, (y_ref, cre_ref, cim_ref), ride_out = rest[:ng], rest[ng:ng + 3], rest[ng + 3:2 * ng + 3]
        xs_re, xs_im, car_re, car_im, ysk = rest[2 * ng + 3:2 * ng + 8]
        i = pl.program_id(0)
        if ng:
            start, forward, finish = _gather_phases(ride_in, ride_out, rest[2 * ng + 8:3 * ng + 8],
                                                    *rest[3 * ng + 8:])
            pl.when(i == 0)(start)
            pl.when(i == nblk // 2)(forward)
        _zero_at_first(i, car_re, car_im)
        cre_ref[0] = car_re[...]
        cim_ref[0] = car_im[...]
        us = _dot(pm_ref[...], u_ref[...])
        usb = us.astype(BF16)
        for j in range(4):
            xs_re[:, LCH * j:LCH * (j + 1)] = _dot(usb[:, 128 * j:128 * (j + 1)], mbre[j])
            xs_im[:, LCH * j:LCH * (j + 1)] = _dot(usb[:, 128 * j:128 * (j + 1)], mbim[j])
        _scan_fwd_block(xs_re, xs_im, p8re, p8im, qre, qim, car_re, car_im, None, None)
        for j in range(4):
            sl = slice(LCH * j, LCH * (j + 1))
            ysk[:, 128 * j:128 * (j + 1)] = (_dot_nt(xs_re[:, sl].astype(BF16), cmre[j])
                                             - _dot_nt(xs_im[:, sl].astype(BF16), cmim[j]))
        yv = ysk[...] + d_ref[...] * us
        gv = _dot(yv.astype(BF16), wg_ref[...])
        y_ref[...] = _unpermute(pmt_ref, gv[:, :SSM_W] * _sig(gv[:, SSM_W:]))
        if ng:
            pl.when(i == nblk - 1)(finish)

    blk = (4, 128, LCH)
    res = pl.pallas_call(
        body, name="ssm_fwd", grid=(nblk,),
        in_specs=[_rows(SSM_W, 1), _full((TM, TM)), _full((TM, TM)), _full(blk), _full(blk),
                  _full((TM, STATES)), _full((TM, STATES)), _full((SUB, STATES)), _full((SUB, STATES)),
                  _full(blk), _full(blk), _full((1, SSM_W)), _full((SSM_W, 2 * SSM_W))] + [ANY] * ng,
        out_specs=[_rows(SSM_W), pl.BlockSpec((1, SUB, STATES), lambda i: (i, 0, 0)),
                   pl.BlockSpec((1, SUB, STATES), lambda i: (i, 0, 0))] + [ANY] * ng,
        out_shape=[jax.ShapeDtypeStruct((t, SSM_W), F32), jax.ShapeDtypeStruct((nblk, SUB, STATES), F32),
                   jax.ShapeDtypeStruct((nblk, SUB, STATES), F32)]
        + [jax.ShapeDtypeStruct((4,) + a.shape, a.dtype) for a in ride],
        scratch_shapes=[pltpu.VMEM((TM, STATES), F32), pltpu.VMEM((TM, STATES), F32),
                        pltpu.VMEM((SUB, STATES), F32), pltpu.VMEM((SUB, STATES), F32),
                        pltpu.VMEM((TM, SSM_W), F32)] + (_gather_scratch(ride) if ng else []),
        compiler_params=_params())(proj, pm, pmt, mb_re, mb_im, p8_re, p8_im, q_re, q_im, cm_re, cm_im, dskip, wglu,
                                   *ride)
    return res[:3], res[3:]


def _even_out(yp, ys, proj, x, w):
    t = x.shape[0]

    def body(yp_ref, ys_ref, z_ref, x_ref, w_ref, x1_ref, yg_ref):
        z = z_ref[...].astype(F32)
        sz = z * _sig(z)
        gp = (yp_ref[...] * sz[:, :POOL_W]).astype(BF16)
        gs = (ys_ref[...] * sz[:, POOL_W:]).astype(BF16)
        yg_ref[:, :POOL_W] = gp
        yg_ref[:, POOL_W:] = gs
        x1_ref[...] = x_ref[...] + _dot(gp, w_ref[:POOL_W, :]) + _dot(gs, w_ref[POOL_W:, :])

    return pl.pallas_call(
        body, name="even_out", grid=(t // TM_MM,),
        in_specs=[_mm_rows(POOL_W), _mm_rows(SSM_W), _mm_rows(D_MODEL, 1), _mm_rows(D_MODEL),
                  _full((D_MODEL, D_MODEL))],
        out_specs=[_mm_rows(D_MODEL), _mm_rows(D_MODEL)],
        out_shape=[jax.ShapeDtypeStruct((t, D_MODEL), F32), jax.ShapeDtypeStruct((t, D_MODEL), BF16)],
        compiler_params=_params())(yp, ys, proj, x, w)


def _phase_copies(ext, cp):
    n = cp.shape[1]
    for j in range(1, SUB):
        cp[j - 1] = ext[pl.ds(j, n), :]


def _shifted(ext, cp, off, nrows, sl, row0=0):
    q, j = divmod(off, SUB)
    if j == 0:
        return ext[pl.ds(row0 + SUB * q, nrows), sl]
    return cp[j - 1, pl.ds(row0 + SUB * q, nrows), sl]


def _conv_taps(ext, cp, w_ref, first, nrows, sl, init, row0=0):
    acc = init
    for k in range(CONV_K):
        acc = acc + w_ref[k:k + 1, sl] * _shifted(ext, cp, first(k), nrows, sl, row0)
    return acc


def _conv_fwd(q, cw, cb, lg, lb):
    t = q.shape[0]

    def body(v_ref, g_ref, hv_ref, hg_ref, z_ref, w_ref, b_ref, lg_ref, lb_ref, y_ref, cv_ref, ext, cp):
        i = pl.program_id(0)
        ext[0:HALO, :] = jnp.where(i == 0, 0.0, hv_ref[...].astype(F32) * _sig(hg_ref[...].astype(F32)))
        ext[HALO:, :] = v_ref[...].astype(F32) * _sig(g_ref[...].astype(F32))
        _phase_copies(ext, cp)

        def lanes(c, carry):
            sl = pl.ds(pl.multiple_of(c * 128, 128), 128)
            cv_ref[:, sl] = _conv_taps(ext, cp, w_ref, lambda k: k + 2, TM, sl,
                                       jnp.broadcast_to(b_ref[:, sl], (TM, 128)))
            return carry

        lax.fori_loop(0, D_MODEL // 128, lanes, 0)
        cv = cv_ref[...]
        cc = cv - jnp.mean(cv, axis=-1, keepdims=True)
        rstd = lax.rsqrt(jnp.mean(cc * cc, axis=-1, keepdims=True) + LN_EPS)
        cl = cc * rstd * lg_ref[...] + lb_ref[...]
        z = z_ref[...].astype(F32)
        y_ref[...] = (cl * _sig(cl) * z * _sig(z)).astype(BF16)

    vec = _full((1, D_MODEL))
    return pl.pallas_call(
        body, name="conv_fwd", grid=(t // TM,),
        in_specs=[_rows(D_MODEL, 0), _rows(D_MODEL, 1), _prev(HALO, D_MODEL, 0), _prev(HALO, D_MODEL, 1),
                  _rows(D_MODEL, 2), _full((HALO, D_MODEL)), vec, vec, vec],
        out_specs=[_rows(D_MODEL), _rows(D_MODEL)],
        out_shape=[jax.ShapeDtypeStruct((t, D_MODEL), BF16), jax.ShapeDtypeStruct((t, D_MODEL), F32)],
        scratch_shapes=[pltpu.VMEM((TM + HALO, D_MODEL), F32),
                        pltpu.VMEM((SUB - 1, TM + HALO - SUB, D_MODEL), F32)],
        compiler_params=_params())(q, q, q, q, q, cw, cb, lg, lb)


def _odd_out_loss(y2, x1, w, gf, tgt):
    t = x1.shape[0]

    def body(y_ref, x_ref, w_ref, g_ref, t_ref, dx_ref, loss_ref, dg_ref):
        i = pl.program_id(0)
        _zero_at_first(i, loss_ref, dg_ref)
        x2 = x_ref[...] + _dot(y_ref[...], w_ref[...])
        r = lax.rsqrt(jnp.mean(x2 * x2, axis=-1, keepdims=True) + RMS_EPS)
        n = x2 * r
        e = n * g_ref[...] - t_ref[...]
        loss_ref[...] += jnp.sum(e * e, axis=0, keepdims=True) * (0.5 / D_MODEL)
        dout = e * (1.0 / D_MODEL)
        dg_ref[...] += jnp.sum(dout * n, axis=0, keepdims=True)
        dn = dout * g_ref[...]
        dx_ref[...] = r * (dn - n * jnp.mean(dn * n, axis=-1, keepdims=True))

    vec = _full((1, D_MODEL))
    return pl.pallas_call(
        body, name="odd_out_loss", grid=(t // TM_MM,),
        in_specs=[_mm_rows(D_MODEL), _mm_rows(D_MODEL), _full((D_MODEL, D_MODEL)), vec, _mm_rows(D_MODEL)],
        out_specs=[_mm_rows(D_MODEL), vec, vec],
        out_shape=[jax.ShapeDtypeStruct((t, D_MODEL), F32), jax.ShapeDtypeStruct((1, D_MODEL), F32),
                   jax.ShapeDtypeStruct((1, D_MODEL), F32)],
        compiler_params=_params())(y2, x1, w, gf, tgt)


def _dsilu(z):
    s = _sig(z)
    return z * s, s * (1.0 + z * (1.0 - s))


def _odd_bwd_out(dx2, w, y2, cv, q, lg, lb):
    t = dx2.shape[0]

    def body(dx_ref, w_ref, y_ref, cv_ref, z_ref, lg_ref, lb_ref, dcv_ref, dz_ref, dw_ref, dlg_ref, dlb_ref):
        i = pl.program_id(0)
        _zero_at_first(i, dw_ref, dlg_ref, dlb_ref)
        dxb = dx_ref[...].astype(BF16)
        dy = _dot_nt(dxb, w_ref[...])
        dw_ref[...] += _dot_tn(y_ref[...], dxb)
        cv = cv_ref[...]
        cc = cv - jnp.mean(cv, axis=-1, keepdims=True)
        rstd = lax.rsqrt(jnp.mean(cc * cc, axis=-1, keepdims=True) + LN_EPS)
        cn = cc * rstd
        silu_c, dsilu_c = _dsilu(cn * lg_ref[...] + lb_ref[...])
        silu_z, dsilu_z = _dsilu(z_ref[...].astype(F32))
        dcl = dy * silu_z * dsilu_c
        dz_ref[...] = (dy * silu_c * dsilu_z).astype(BF16)
        dlg_ref[...] += jnp.sum(dcl * cn, axis=0, keepdims=True)
        dlb_ref[...] += jnp.sum(dcl, axis=0, keepdims=True)
        dcn = dcl * lg_ref[...]
        dcv_ref[...] = rstd * (dcn - jnp.mean(dcn, axis=-1, keepdims=True)
                               - cn * jnp.mean(dcn * cn, axis=-1, keepdims=True))

    vec = _full((1, D_MODEL))
    mat = _full((D_MODEL, D_MODEL))
    return pl.pallas_call(
        body, name="odd_bwd_out", grid=(t // TM_MM,),
        in_specs=[_mm_rows(D_MODEL), mat, _mm_rows(D_MODEL), _mm_rows(D_MODEL), _mm_rows(D_MODEL, 2), vec, vec],
        out_specs=[_mm_rows(D_MODEL), _mm_rows(D_MODEL), mat, vec, vec],
        out_shape=[jax.ShapeDtypeStruct((t, D_MODEL), F32), jax.ShapeDtypeStruct((t, D_MODEL), BF16),
                   jax.ShapeDtypeStruct((D_MODEL, D_MODEL), F32), jax.ShapeDtypeStruct((1, D_MODEL), F32),
                   jax.ShapeDtypeStruct((1, D_MODEL), F32)],
        compiler_params=_params())(dx2, w, y2, cv, q, lg, lb)


def _conv_bwd(dcv, q, cw):
    t = dcv.shape[0]
    nblk = t // TM

    def body(d_ref, dn_ref, v_ref, g_ref, hv_ref, hg_ref, w_ref,
             dv_ref, dgt_ref, dw_ref, db_ref, gext, dext, dgl, gcp, dcp):
        i = pl.program_id(0)
        last = nblk - 1
        _zero_at_first(i, dw_ref, db_ref)
        v = v_ref[...].astype(F32)
        sg = _sig(g_ref[...].astype(F32))
        gext[0:HALO, :] = jnp.where(i == 0, 0.0, hv_ref[...].astype(F32) * _sig(hg_ref[...].astype(F32)))
        gext[HALO:, :] = v * sg
        d = d_ref[...]
        dext[0:TM, :] = d
        dext[TM:, :] = jnp.where(i == last, 0.0, dn_ref[...])
        _phase_copies(gext, gcp)
        _phase_copies(dext, dcp)
        db_ref[...] += jnp.sum(d, axis=0, keepdims=True)
        def lanes(c, carry):
            sl = pl.ds(pl.multiple_of(c * 128, 128), 128)
            dgl[:, sl] = _conv_taps(dext, dcp, w_ref, lambda k: 30 - k, TM, sl, jnp.zeros((TM, 128), F32))
            return carry

        def lanes_w(c, carry):
            sl = pl.ds(pl.multiple_of(c * 128, 128), 128)
            ntile = TM // SUB
            dts = [d_ref[SUB * r:SUB * (r + 1), sl] for r in range(ntile)]
            for j in range(SUB):
                taps = [(q, SUB * q + j - 2) for q in range(5) if 0 <= SUB * q + j - 2 < CONV_K]
                sums = {k: None for _, k in taps}
                for rt in range(ntile + 4):
                    need = [(q, k) for q, k in taps if 0 <= rt - q < ntile]
                    if not need:
                        continue
                    src = gext[SUB * rt:SUB * (rt + 1), sl] if j == 0 else gcp[j - 1, SUB * rt:SUB * (rt + 1), sl]
                    for q, k in need:
                        prod = dts[rt - q] * src
                        sums[k] = prod if sums[k] is None else sums[k] + prod
                for _, k in taps:
                    dw_ref[SUB * k:SUB * (k + 1), sl] += sums[k]
            return carry

        lax.fori_loop(0, D_MODEL // 128, lanes, 0)
        lax.fori_loop(0, D_MODEL // 128, lanes_w, 0)
        dg = dgl[...]
        dv_ref[...] = (dg * sg).astype(BF16)
        dgt_ref[...] = (dg * v * sg * (1.0 - sg)).astype(BF16)

    return pl.pallas_call(
        body, name="conv_bwd", grid=(t // TM,),
        in_specs=[_rows(D_MODEL), _next(HALO, D_MODEL, t), _rows(D_MODEL, 0), _rows(D_MODEL, 1),
                  _prev(HALO, D_MODEL, 0), _prev(HALO, D_MODEL, 1), _full((HALO, D_MODEL))],
        out_specs=[_rows(D_MODEL), _rows(D_MODEL), _full((HALO * SUB, D_MODEL)), _full((1, D_MODEL))],
        out_shape=[jax.ShapeDtypeStruct((t, D_MODEL), BF16), jax.ShapeDtypeStruct((t, D_MODEL), BF16),
                   jax.ShapeDtypeStruct((HALO * SUB, D_MODEL), F32), jax.ShapeDtypeStruct((1, D_MODEL), F32)],
        scratch_shapes=[pltpu.VMEM((TM + HALO, D_MODEL), F32), pltpu.VMEM((TM + HALO, D_MODEL), F32),
                        pltpu.VMEM((TM, D_MODEL), F32),
                        pltpu.VMEM((SUB - 1, TM + HALO - SUB, D_MODEL), F32),
                        pltpu.VMEM((SUB - 1, TM + HALO - SUB, D_MODEL), F32)],
        compiler_params=_params())(dcv, dcv, q, q, q, q, cw)


def _column_segments(widths, ns):
    segs = []
    col = 0
    for p, wd in enumerate(widths):
        a = 0
        while a < wd:
            s, lo = divmod(col + a, ns)
            ln = min(wd - a, ns - lo)
            segs.append((p, a, a + ln, s, lo, lo + ln))
            a += ln
        col += wd
    return segs


def _in_bwd(dparts, w, x, g, dres, name):
    t = x.shape[0]
    widths = [p.shape[1] for p in dparts]
    npart = len(dparts)
    segs = _column_segments(widths, w.shape[2])

    def body(*refs):
        d_refs = refs[:npart]
        w_ref, x_ref, g_ref, r_ref, dx_ref, dg_ref, dw_ref = refs[npart:]
        i = pl.program_id(0)
        _zero_at_first(i, dg_ref, dw_ref)
        xv = x_ref[...]
        r = lax.rsqrt(jnp.mean(xv * xv, axis=-1, keepdims=True) + RMS_EPS)
        n = xv * r
        h = (n * g_ref[...]).astype(BF16)
        dh = None
        for p, lo, hi, s, slo, shi in segs:
            d = d_refs[p][:, lo:hi]
            part = _dot_nt(d, w_ref[s, :, slo:shi])
            dh = part if dh is None else dh + part
            dw_ref[s, :, slo:shi] += _dot_tn(h, d)
        dg_ref[...] += jnp.sum(dh * n, axis=0, keepdims=True)
        dn = dh * g_ref[...]
        dx_ref[...] = r_ref[...] + r * (dn - n * jnp.mean(dn * n, axis=-1, keepdims=True))

    vec = _full((1, D_MODEL))
    once = pl.BlockSpec(w.shape, lambda i: (0, 0, 0), pipeline_mode=pl.Buffered(1))
    return pl.pallas_call(
        body, name=name, grid=(t // TM_MM,),
        in_specs=[_mm_rows(wd) for wd in widths] + [once, _mm_rows(D_MODEL), vec, _mm_rows(D_MODEL)],
        out_specs=[_mm_rows(D_MODEL), vec, once],
        out_shape=[jax.ShapeDtypeStruct((t, D_MODEL), F32), jax.ShapeDtypeStruct((1, D_MODEL), F32),
                   jax.ShapeDtypeStruct(w.shape, F32)],
        compiler_params=_params())(*dparts, w, x, g, dres)


def _even_bwd_out(dx1, w, yg, yp, ys, proj, ride):
    t = dx1.shape[0]
    nstep = t // TM_MM

    def body(dx_ref, w_ref, yg_ref, yp_ref, ys_ref, z_ref, g_ref, dy_ref, dz_ref, dw_ref, theirs_ref,
             send_sems, recv_sems):
        i = pl.program_id(0)
        start, finish = _pair_split_phases(g_ref, theirs_ref, send_sems, recv_sems)
        pl.when(i == 0)(start)
        _zero_at_first(i, dw_ref)
        dxb = dx_ref[...].astype(BF16)
        dyg = _dot_nt(dxb, w_ref[...])
        dw_ref[...] += _dot_tn(yg_ref[...], dxb)
        silu_z, dsilu_z = _dsilu(z_ref[...].astype(F32))
        dy_ref[...] = (dyg * silu_z).astype(BF16)
        dz_ref[:, :POOL_W] = (dyg[:, :POOL_W] * yp_ref[...] * dsilu_z[:, :POOL_W]).astype(BF16)
        dz_ref[:, POOL_W:] = (dyg[:, POOL_W:] * ys_ref[...] * dsilu_z[:, POOL_W:]).astype(BF16)
        pl.when(i == nstep - 1)(finish)

    mat = _full((D_MODEL, D_MODEL))
    return pl.pallas_call(
        body, name="even_bwd_out", grid=(nstep,),
        in_specs=[_mm_rows(D_MODEL), mat, _mm_rows(D_MODEL), _mm_rows(POOL_W), _mm_rows(SSM_W), _mm_rows(D_MODEL, 1),
                  ANY],
        out_specs=[_mm_rows(D_MODEL), _mm_rows(D_MODEL), mat, ANY],
        out_shape=[jax.ShapeDtypeStruct((t, D_MODEL), BF16), jax.ShapeDtypeStruct((t, D_MODEL), BF16),
                   jax.ShapeDtypeStruct((D_MODEL, D_MODEL), F32), _pair_split_shape(ride)],
        scratch_shapes=_pair_split_scratch(ride),
        compiler_params=_params())(dx1, w, yg, yp, ys, proj, ride)


def _pool_bwd(dycat, proj, wp, ps):
    t = proj.shape[0]

    def body(dy_ref, dyn_ref, u_ref, h_ref, wp_ref, ps_ref, du_ref, dwp_ref, dps_ref):
        i = pl.program_id(0)
        last = t // TM - 1
        _zero_at_first(i, dwp_ref, dps_ref)
        pos = (i * TM + 1 + lax.broadcasted_iota(jnp.int32, (TM, 1), 0)).astype(F32)
        pos_ext = (i * TM + 1 + lax.broadcasted_iota(jnp.int32, (TM + POOL_HALO, 1), 0)).astype(F32)
        for g in range(4):
            sl = slice(128 * g, 128 * (g + 1))
            w = float(2 << g)
            u = u_ref[:, sl].astype(F32)
            halo = jnp.where(i == 0, 0.0, h_ref[:, sl].astype(F32))
            s = _pool_sums(jnp.concatenate([halo, u], axis=0), g, True)[POOL_HALO:, :]
            pooled = (s / jnp.minimum(pos, w) - u).astype(BF16)
            dy = dy_ref[:, sl].astype(F32)
            dps_ref[:, sl] += jnp.sum(dy * _dot(pooled, wp_ref[g]), axis=0, keepdims=True)
            dy_ext = jnp.concatenate([dy, jnp.where(i == last, 0.0, dyn_ref[:, sl].astype(F32))], axis=0)
            dmix = (dy_ext * ps_ref[:, sl]).astype(BF16)
            dwp_ref[g] += _dot_tn(pooled, dmix[:TM, :])
            dpool = _dot_nt(dmix, wp_ref[g])
            lead = _pool_sums(dpool / jnp.minimum(pos_ext, w), g, False)
            du_ref[:, sl] = (lead[:TM, :] - dpool[:TM, :]).astype(BF16)

    return pl.pallas_call(
        body, name="pool_bwd", grid=(t // TM,),
        in_specs=[_rows(POOL_W, 0), _next(POOL_HALO, POOL_W, t, 0), _rows(POOL_W, 0), _prev(POOL_HALO, POOL_W, 0),
                  _full((4, 128, 128)), _full((1, POOL_W))],
        out_specs=[_rows(POOL_W), _full((4, 128, 128)), _full((1, POOL_W))],
        out_shape=[jax.ShapeDtypeStruct((t, POOL_W), BF16), jax.ShapeDtypeStruct((4, 128, 128), F32),
                   jax.ShapeDtypeStruct((1, POOL_W), F32)],
        compiler_params=_params())(dycat, dycat, proj, proj, wp, ps)


def _ssm_bwd(dycat, proj, car_in_re, car_in_im, pm, pmt, mb_re, mb_im, p8_re, p8_im, q_re, q_im, qr_re, qr_im,
             cm_re, cm_im, dskip, wglu, ride):
    t = proj.shape[0]
    nblk = t // TM

    def body(dy_ref, u_ref, cin_re, cin_im, pm_ref, pmt_ref, mbre, mbim, p8re, p8im, qre, qim, qrre, qrim,
             cmre, cmim, d_ref, wg_ref, p_ref,
             du_ref, dmbre, dmbim, dcmre, dcmim, dare, daim, dd_ref, dwg_ref, got_ref,
             xs_re, xs_im, gs_re, gs_im, car_re, car_im, ent_re, ent_im, gcar_re, gcar_im, ysk, dysk,
             bounce, send_sems, recv_sems, local_sems):
        i = pl.program_id(0)
        start, finish = _chip_scatter_phases(p_ref, got_ref, bounce, send_sems, recv_sems, local_sems)
        pl.when(i == 0)(start)
        _zero_at_first(i, dmbre, dmbim, dcmre, dcmim, dare, daim, dd_ref, dwg_ref, gcar_re, gcar_im)
        us = _dot(pm_ref[...], u_ref[...])
        usb = us.astype(BF16)
        for j in range(4):
            xs_re[:, LCH * j:LCH * (j + 1)] = _dot(usb[:, 128 * j:128 * (j + 1)], mbre[j])
            xs_im[:, LCH * j:LCH * (j + 1)] = _dot(usb[:, 128 * j:128 * (j + 1)], mbim[j])
        car_re[...] = cin_re[0]
        car_im[...] = cin_im[0]
        _scan_fwd_block(xs_re, xs_im, p8re, p8im, qre, qim, car_re, car_im, ent_re, ent_im)
        for j in range(4):
            sl = slice(LCH * j, LCH * (j + 1))
            ysk[:, 128 * j:128 * (j + 1)] = (_dot_nt(xs_re[:, sl].astype(BF16), cmre[j])
                                             - _dot_nt(xs_im[:, sl].astype(BF16), cmim[j]))
        yvb = (ysk[...] + d_ref[...] * us).astype(BF16)
        gv = _dot(yvb, wg_ref[...])
        sg = _sig(gv[:, SSM_W:])
        dyss = _dot(pm_ref[...], dy_ref[...])
        dval = (dyss * sg).astype(BF16)
        dgate = (dyss * gv[:, :SSM_W] * sg * (1.0 - sg)).astype(BF16)
        dy = _dot_nt(dval, wg_ref[:, :SSM_W]) + _dot_nt(dgate, wg_ref[:, SSM_W:])
        dwg_ref[:, :SSM_W] += _dot_tn(yvb, dval)
        dwg_ref[:, SSM_W:] += _dot_tn(yvb, dgate)
        dd_ref[...] += jnp.sum(dy * us, axis=0, keepdims=True)
        dysk[...] = dy
        for j in range(4):
            sl = slice(LCH * j, LCH * (j + 1))
            dyj = dy[:, 128 * j:128 * (j + 1)].astype(BF16)
            gs_re[:, sl] = _dot(dyj, cmre[j])
            gs_im[:, sl] = -_dot(dyj, cmim[j])
            dcmre[j] += _dot_tn(dyj, xs_re[:, sl].astype(BF16))
            dcmim[j] -= _dot_tn(dyj, xs_im[:, sl].astype(BF16))
        row = lax.broadcasted_iota(jnp.int32, (SUB, SCAN_L), 0)
        for j in range(STATES // SCAN_L):
            sl = slice(SCAN_L * j, SCAN_L * (j + 1))
            are, aim = p8re[0:SUB, sl], -p8im[0:SUB, sl]

            def totals(k, v, sl=sl, are=are, aim=aim):
                r0 = pl.multiple_of((SEG_LEN - 2 - k) * SUB, SUB)
                vre, vim = _cmul_add(are, aim, v[0], v[1], gs_re[pl.ds(r0, SUB), sl], gs_im[pl.ds(r0, SUB), sl])
                gs_re[pl.ds(r0, SUB), sl] = vre
                gs_im[pl.ds(r0, SUB), sl] = vim
                return vre, vim

            top = (SEG_LEN - 1) * SUB
            fre, fim = lax.fori_loop(0, SEG_LEN - 1, totals,
                                     (gs_re[top:top + SUB, sl], gs_im[top:top + SUB, sl]), unroll=2)
            fre, fim, nre, nim = _segment_chain(fre, fim, qrre[:, sl], -qrim[:, sl],
                                                gcar_re[:, sl], gcar_im[:, sl], row, True)
            gcar_re[:, sl] = jnp.broadcast_to(fre[0:1, :], (SUB, SCAN_L))
            gcar_im[:, sl] = jnp.broadcast_to(fim[0:1, :], (SUB, SCAN_L))

            def fix(i2, acc, sl=sl, nre=nre, nim=nim):
                r0 = pl.multiple_of(i2 * SUB, SUB)
                rb = pl.multiple_of((SEG_LEN - 1 - i2) * SUB, SUB)
                gre, gim = _cmul_add(p8re[pl.ds(rb, SUB), sl], -p8im[pl.ds(rb, SUB), sl], nre, nim,
                                     gs_re[pl.ds(r0, SUB), sl], gs_im[pl.ds(r0, SUB), sl])
                gs_re[pl.ds(r0, SUB), sl] = gre
                gs_im[pl.ds(r0, SUB), sl] = gim
                rp = pl.multiple_of((i2 - 1) * SUB, SUB)
                xre, xim = xs_re[pl.ds(rp, SUB), sl], xs_im[pl.ds(rp, SUB), sl]
                return acc[0] + gre * xre + gim * xim, acc[1] + gim * xre - gre * xim

            g0re, g0im = _cmul_add(p8re[top:top + SUB, sl], -p8im[top:top + SUB, sl], nre, nim,
                                   gs_re[0:SUB, sl], gs_im[0:SUB, sl])
            gs_re[0:SUB, sl] = g0re
            gs_im[0:SUB, sl] = g0im
            ere, eim = ent_re[:, sl], ent_im[:, sl]
            acc0 = (dare[:, sl] + g0re * ere + g0im * eim, daim[:, sl] + g0im * ere - g0re * eim)
            are_acc, aim_acc = lax.fori_loop(1, SEG_LEN, fix, acc0, unroll=2)
            dare[:, sl] = are_acc
            daim[:, sl] = aim_acc
        for j in range(4):
            sl = slice(LCH * j, LCH * (j + 1))
            c4 = slice(128 * j, 128 * (j + 1))
            gre = gs_re[:, sl].astype(BF16)
            gim = gs_im[:, sl].astype(BF16)
            dmbre[j] += _dot_tn(usb[:, c4], gre)
            dmbim[j] += _dot_tn(usb[:, c4], gim)
            dysk[:, c4] = _dot_nt(gre, mbre[j]) + _dot_nt(gim, mbim[j]) + dysk[:, c4] * d_ref[:, c4]
        du_ref[...] = _dot(pmt_ref[...], dysk[...].astype(BF16)).astype(BF16)
        pl.when(i == nblk - 1)(finish)

    blk = (4, 128, LCH)
    pw = _full((SUB, STATES))
    p8 = _full((TM, STATES))
    perm = _full((TM, TM))
    car = pl.BlockSpec((1, SUB, STATES), lambda i: (nblk - 1 - i, 0, 0))
    big = lambda: pltpu.VMEM((TM, STATES), F32)
    small = lambda: pltpu.VMEM((SUB, STATES), F32)
    return pl.pallas_call(
        body, name="ssm_bwd", grid=(nblk,),
        in_specs=[_rows(SSM_W, 1, rev=nblk), _rows(SSM_W, 1, rev=nblk), car, car, perm, perm, _full(blk), _full(blk),
                  p8, p8, pw, pw, pw, pw, _full(blk), _full(blk), _full((1, SSM_W)), _full((SSM_W, 2 * SSM_W)), ANY],
        out_specs=[_rows(SSM_W, 0, rev=nblk), _full(blk), _full(blk), _full(blk), _full(blk), pw, pw,
                   _full((1, SSM_W)), _full((SSM_W, 2 * SSM_W)), ANY],
        out_shape=[jax.ShapeDtypeStruct((t, SSM_W), BF16)] + [jax.ShapeDtypeStruct(blk, F32)] * 4
        + [jax.ShapeDtypeStruct((SUB, STATES), F32)] * 2
        + [jax.ShapeDtypeStruct((1, SSM_W), F32), jax.ShapeDtypeStruct((SSM_W, 2 * SSM_W), F32),
           jax.ShapeDtypeStruct(ride.shape, ride.dtype)],
        scratch_shapes=[big(), big(), big(), big(), small(), small(), small(), small(), small(), small(),
                        pltpu.VMEM((TM, SSM_W), F32), pltpu.VMEM((TM, SSM_W), F32)] + _chip_scatter_scratch(ride),
        compiler_params=_params())(dycat, proj, car_in_re, car_in_im, pm, pmt, mb_re, mb_im, p8_re, p8_im,
                                   q_re, q_im, qr_re, qr_im, cm_re, cm_im, dskip, wglu, ride)


def _adamw(w, g, m, v, name):
    rows = w.shape[0]
    tr = 256 if rows % 256 == 0 else rows
    c1 = 1.0 / (1.0 - ADAM_B1 ** ADAM_STEP)
    c2 = 1.0 / (1.0 - ADAM_B2 ** ADAM_STEP)

    def body(w_ref, g_ref, m_ref, v_ref, d_ref, nm_ref, nv_ref):
        gv = g_ref[...]
        m = ADAM_B1 * m_ref[...] + (1.0 - ADAM_B1) * gv
        v = ADAM_B2 * v_ref[...] + (1.0 - ADAM_B2) * (gv * gv)
        nm_ref[...] = m
        nv_ref[...] = v
        d_ref[...] = -ADAM_LR * ((m * c1) / (jnp.sqrt(v * c2) + ADAM_EPS) + ADAM_WD * w_ref[...])

    spec = pl.BlockSpec((tr, D_MODEL), lambda i: (i, 0))
    shp = jax.ShapeDtypeStruct((rows, D_MODEL), F32)
    return pl.pallas_call(
        body, name=name, grid=(rows // tr,), in_specs=[spec] * 4, out_specs=[spec] * 3, out_shape=[shp] * 3,
        compiler_params=_params())(w, g, m, v)


def _core_index():
    return lax.axis_index("c").astype(jnp.int32).reshape(1)


def _pair_add(g, theirs, out_dtype, name):
    n, half, _ = theirs.shape
    br = 128
    nb = half // br

    def body(c_ref, a_ref, b_ref, o_ref):
        o_ref[...] = (a_ref[...] + b_ref[...]).astype(out_dtype)

    spec = pl.BlockSpec((1, br, D_MODEL), lambda i, j, c: (i, j, 0))
    grid_spec = pltpu.PrefetchScalarGridSpec(
        num_scalar_prefetch=1, grid=(n, nb),
        in_specs=[pl.BlockSpec((1, br, D_MODEL), lambda i, j, c: (i, c[0] * nb + j, 0)), spec], out_specs=spec)
    return pl.pallas_call(
        body, name=name, grid_spec=grid_spec, out_shape=jax.ShapeDtypeStruct(theirs.shape, out_dtype),
        compiler_params=_params(2))(_core_index(), g, theirs)


def _adamw_rows(w, m, v, g_mine, g_theirs, row0, br, name):
    rows = w.shape[0]
    b0 = row0 // br
    per_half = g_mine.shape[0] // br
    c1 = 1.0 / (1.0 - ADAM_B1 ** ADAM_STEP)
    c2 = 1.0 / (1.0 - ADAM_B2 ** ADAM_STEP)

    def body(c_ref, w_ref, gm_ref, gt_ref, m_ref, v_ref, g_ref, d_ref, nm_ref, nv_ref):
        gv = jnp.where((b0 + pl.program_id(0)) // per_half == c_ref[0], gm_ref[...], gt_ref[...])
        m = ADAM_B1 * m_ref[...] + (1.0 - ADAM_B1) * gv
        v = ADAM_B2 * v_ref[...] + (1.0 - ADAM_B2) * (gv * gv)
        g_ref[...] = gv
        nm_ref[...] = m
        nv_ref[...] = v
        d_ref[...] = -ADAM_LR * ((m * c1) / (jnp.sqrt(v * c2) + ADAM_EPS) + ADAM_WD * w_ref[...])

    spec = pl.BlockSpec((br, D_MODEL), lambda i, c: (i, 0))
    part = pl.BlockSpec((br, D_MODEL), lambda i, c: ((b0 + i) % per_half, 0))
    shp = jax.ShapeDtypeStruct((rows, D_MODEL), F32)
    grid_spec = pltpu.PrefetchScalarGridSpec(
        num_scalar_prefetch=1, grid=(rows // br,), in_specs=[spec, part, part, spec, spec], out_specs=[spec] * 4)
    return pl.pallas_call(
        body, name=name, grid_spec=grid_spec, out_shape=[shp] * 4,
        compiler_params=_params())(_core_index(), w, g_mine, g_theirs, m, v)


def _sum_lead(a, name):
    n, rows, _ = a.shape
    tr = 128 if rows % 128 == 0 else rows

    def body(a_ref, o_ref):
        acc = a_ref[0].astype(F32)
        for k in range(1, n):
            acc = acc + a_ref[k].astype(F32)
        o_ref[...] = acc

    return pl.pallas_call(
        body, name=name, grid=(rows // tr,),
        in_specs=[pl.BlockSpec((n, tr, D_MODEL), lambda i: (0, i, 0))],
        out_specs=pl.BlockSpec((tr, D_MODEL), lambda i: (i, 0)),
        out_shape=jax.ShapeDtypeStruct((rows, D_MODEL), F32), compiler_params=_params())(a)


ANY = pl.BlockSpec(memory_space=pl.ANY)


def _mesh_pos():
    return lax.axis_index("x"), lax.axis_index("y"), lax.axis_index("c")


def _gather_phases(in_refs, out_refs, bounces, send_sems, recv_sems, local_sems):
    na = len(in_refs)
    halves = [r.shape[0] // 2 for r in in_refs]
    ncopy = 3 * na
    x, y, c = _mesh_pos()
    me = 2 * x + y
    sibling = (x, y, 1 - c)
    chips = [(1 - x, y), (x, 1 - y), (1 - x, 1 - y)]
    ids = [2 * chip[0] + chip[1] for chip in chips]

    def piece(a, q, h):
        return out_refs[a].at[q, pl.ds(h * halves[a], halves[a]), :]

    def copy(s, a, q, h, to, src=None):
        return pltpu.make_async_remote_copy(
            src_ref=piece(a, q, h) if src is None else src, dst_ref=piece(a, q, h),
            send_sem=send_sems.at[s], recv_sem=recv_sems.at[s], device_id=to, device_id_type=MESH)

    def sends():
        return [copy(j * na + a, a, me, c, (*chip, c), src=in_refs[a].at[pl.ds(c * halves[a], halves[a]), :])
                for j, chip in enumerate(chips) for a in range(na)]

    def forwards():
        return [copy(ncopy + j * na + a, a, ids[j], c, sibling) for j in range(3) for a in range(na)]

    def stores():
        return [pltpu.make_async_copy(bounces[a], out_refs[a].at[me], local_sems.at[na + a]) for a in range(na)]

    def start():
        loads = [pltpu.make_async_copy(in_refs[a], bounces[a], local_sems.at[a]) for a in range(na)]
        for cp in loads:
            cp.start()
        for cp in sends():
            cp.start()
        for ld, st in zip(loads, stores()):
            ld.wait()
            st.start()

    def forward():
        fwd = forwards()
        for j in range(3):
            for a in range(na):
                copy(j * na + a, a, ids[j], c, (x, y, c)).wait_recv()
                fwd[j * na + a].start()

    def finish():
        for j in range(3):
            for a in range(na):
                copy(ncopy + j * na + a, a, ids[j], 1 - c, (x, y, c)).wait_recv()
        for cp in sends() + forwards():
            cp.wait_send()
        for cp in stores():
            cp.wait()

    return start, forward, finish


def _gather_scratch(arrs):
    ncopy = 3 * len(arrs)
    return ([pltpu.VMEM(a.shape, a.dtype) for a in arrs]
            + [pltpu.SemaphoreType.DMA((2 * ncopy,)), pltpu.SemaphoreType.DMA((2 * ncopy,)),
               pltpu.SemaphoreType.DMA((2 * len(arrs),))])


def _gather_weights(arrs):
    na = len(arrs)

    def body(*refs):
        start, forward, finish = _gather_phases(refs[:na], refs[na:2 * na], refs[2 * na:3 * na], *refs[3 * na:])
        start()
        forward()
        finish()

    return pl.pallas_call(
        body, name="gather_weights", in_specs=[ANY] * na, out_specs=[ANY] * na,
        out_shape=[jax.ShapeDtypeStruct((4,) + a.shape, a.dtype) for a in arrs],
        scratch_shapes=_gather_scratch(arrs),
        compiler_params=pltpu.CompilerParams(vmem_limit_bytes=VMEM_LIMIT),
    )(*arrs)


def _scatter_and_gather(p, v, name):
    def body(p_ref, v_ref, got_ref, o_ref, p_bounce, p_send, p_recv, p_local, bounce, send_sems, recv_sems,
             local_sems):
        start, finish = _chip_scatter_phases(p_ref, got_ref, p_bounce, p_send, p_recv, p_local)
        start()
        x, y, c = _mesh_pos()
        sibling = (x, y, 1 - c)
        chips = [(1 - x, y), (x, 1 - y), (1 - x, 1 - y)]

        def blk(px, py, pc):
            return o_ref.at[4 * px + 2 * py + pc]

        def copy(k, block, to, src=None):
            return pltpu.make_async_remote_copy(
                src_ref=blk(*block) if src is None else src, dst_ref=blk(*block),
                send_sem=send_sems.at[k], recv_sem=recv_sems.at[k], device_id=to, device_id_type=MESH)

        load = pltpu.make_async_copy(v_ref, bounce, local_sems.at[0])
        store = pltpu.make_async_copy(bounce, blk(x, y, c), local_sems.at[1])
        load.start()
        first = [copy(0, (x, y, c), sibling, src=v_ref)]
        first += [copy(1 + j, (x, y, c), (*chip, c), src=v_ref) for j, chip in enumerate(chips)]
        for cp in first:
            cp.start()
        load.wait()
        store.start()
        passed = [copy(4 + j, (*chip, c), sibling) for j, chip in enumerate(chips)]
        for j, chip in enumerate(chips):
            copy(1 + j, (*chip, c), (x, y, c)).wait_recv()
            passed[j].start()
        copy(0, (x, y, 1 - c), (x, y, c)).wait_recv()
        for j, chip in enumerate(chips):
            copy(4 + j, (*chip, 1 - c), (x, y, c)).wait_recv()
        for cp in first + passed:
            cp.wait_send()
        store.wait()
        finish()

    return pl.pallas_call(
        body, name=name, in_specs=[ANY, ANY], out_specs=[ANY, ANY],
        out_shape=[jax.ShapeDtypeStruct(p.shape, p.dtype), jax.ShapeDtypeStruct((8,) + v.shape, v.dtype)],
        scratch_shapes=_chip_scatter_scratch(p)
        + [pltpu.VMEM(v.shape, v.dtype), pltpu.SemaphoreType.DMA((7,)), pltpu.SemaphoreType.DMA((7,)),
           pltpu.SemaphoreType.DMA((2,))],
    )(p, v)


def _pair_split_phases(g_ref, theirs_ref, send_sems, recv_sems):
    n, rows, _ = g_ref.shape
    half = rows // 2
    ch = half // COMM_CHUNKS
    x, y, c = _mesh_pos()

    def gives():
        return [pltpu.make_async_remote_copy(
            src_ref=g_ref.at[q, pl.ds((1 - c) * half + k * ch, ch), :],
            dst_ref=theirs_ref.at[q, pl.ds(k * ch, ch), :],
            send_sem=send_sems.at[q * COMM_CHUNKS + k], recv_sem=recv_sems.at[q * COMM_CHUNKS + k],
            device_id=(x, y, 1 - c), device_id_type=MESH) for q in range(n) for k in range(COMM_CHUNKS)]

    def start():
        for cp in gives():
            cp.start()

    def finish():
        for cp in gives():
            cp.wait()

    return start, finish


def _pair_split_scratch(g):
    return [pltpu.SemaphoreType.DMA((g.shape[0] * COMM_CHUNKS,)), pltpu.SemaphoreType.DMA((g.shape[0] * COMM_CHUNKS,))]


def _pair_split_shape(g):
    return jax.ShapeDtypeStruct((g.shape[0], g.shape[1] // 2, D_MODEL), g.dtype)


def _pair_split(g, name):
    def body(g_ref, theirs_ref, send_sems, recv_sems):
        start, finish = _pair_split_phases(g_ref, theirs_ref, send_sems, recv_sems)
        start()
        finish()

    return pl.pallas_call(
        body, name=name, in_specs=[ANY], out_specs=ANY, out_shape=_pair_split_shape(g),
        scratch_shapes=_pair_split_scratch(g))(g)


def _chip_scatter_phases(p_ref, o_ref, bounce, send_sems, recv_sems, local_sems):
    x, y, c = _mesh_pos()
    me = 2 * x + y
    chips = [(1 - x, y), (x, 1 - y), (1 - x, 1 - y)]

    def keep():
        return pltpu.make_async_copy(bounce, o_ref.at[me], local_sems.at[1])

    def sends():
        return [pltpu.make_async_remote_copy(
            src_ref=p_ref.at[2 * chip[0] + chip[1]], dst_ref=o_ref.at[me],
            send_sem=send_sems.at[j], recv_sem=recv_sems.at[j], device_id=(*chip, c), device_id_type=MESH)
            for j, chip in enumerate(chips)]

    def start():
        load = pltpu.make_async_copy(p_ref.at[me], bounce, local_sems.at[0])
        load.start()
        for cp in sends():
            cp.start()
        load.wait()
        keep().start()

    def finish():
        for j, chip in enumerate(chips):
            q = 2 * chip[0] + chip[1]
            pltpu.make_async_remote_copy(
                src_ref=p_ref.at[q], dst_ref=o_ref.at[q], send_sem=send_sems.at[j], recv_sem=recv_sems.at[j],
                device_id=(*chip, c), device_id_type=MESH).wait_recv()
        for cp in sends():
            cp.wait_send()
        keep().wait()

    return start, finish


def _chip_scatter_scratch(p):
    return [pltpu.VMEM(p.shape[1:], p.dtype), pltpu.SemaphoreType.DMA((3,)), pltpu.SemaphoreType.DMA((3,)),
            pltpu.SemaphoreType.DMA((2,))]


def _pair_join(r, name):
    rows = r.shape[0]
    ch = rows // COMM_CHUNKS

    def body(r_ref, o_ref, send_sems, recv_sems):
        x, y, c = _mesh_pos()
        gives = [pltpu.make_async_remote_copy(
            src_ref=r_ref.at[pl.ds(k * ch, ch), :], dst_ref=o_ref.at[pl.ds(k * ch, ch), :],
            send_sem=send_sems.at[k], recv_sem=recv_sems.at[k], device_id=(x, y, 1 - c), device_id_type=MESH)
            for k in range(COMM_CHUNKS)]
        for cp in gives:
            cp.start()
        for cp in gives:
            cp.wait()

    return pl.pallas_call(
        body, name=name, in_specs=[ANY], out_specs=ANY, out_shape=jax.ShapeDtypeStruct(r.shape, r.dtype),
        scratch_shapes=[pltpu.SemaphoreType.DMA((COMM_CHUNKS,)), pltpu.SemaphoreType.DMA((COMM_CHUNKS,))],
    )(r)


SHARD_BIG = (("even_w_in", (1024, 512)), ("ssm_w_glu", (512, 256)), ("even_w_out", (256, 1024)),
             ("odd_w_in", (1024, 768)), ("odd_w_out", (256, 1024)))
SHARD_SMALL = (("odd_norm", 1), ("conv_w", CONV_K), ("conv_b", 1), ("conv_ln_g", 1), ("conv_ln_b", 1))
REP_NAMES = (("even_norm", (1024,)), ("pool_w", (4, 128, 128)), ("pool_scale", (512,)), ("ssm_log_dt", (32,)),
             ("ssm_a_re", (32, 64)), ("ssm_a_im", (32, 64)), ("ssm_b_re", (32, 64, 16)), ("ssm_b_im", (32, 64, 16)),
             ("ssm_c_re", (32, 16, 64)), ("ssm_c_im", (32, 16, 64)), ("ssm_d", (512,)), ("final_norm", (1024,)))


def _pack_rep(d):
    flat = jnp.concatenate([d[n].reshape(-1) for n, _ in REP_NAMES])
    return jnp.pad(flat, (0, REP_ROWS * D_MODEL - flat.shape[0])).reshape(REP_ROWS, D_MODEL)


def _unpack_rep(buf):
    flat = buf.reshape(-1)
    out = {}
    off = 0
    for n, shp in REP_NAMES:
        size = 1
        for s in shp:
            size *= s
        out[n] = flat[off:off + size].reshape(shp)
        off += size
    return out


def _cols_split(full, cols):
    rows = full.shape[0]
    return full.reshape(rows, 4, cols).transpose(1, 0, 2).reshape(4, -1, D_MODEL)


def _block_diag(a):
    a = a.reshape(4, 8, GROUP_DIM, N_STATE)
    eye = jnp.eye(8, dtype=a.dtype)
    return (a[:, :, :, None, :] * eye[None, :, None, :, None]).reshape(4, 128, LCH)


def _block_diag_take(m):
    m = m.reshape(4, 8, GROUP_DIM, 8, N_STATE)
    eye = jnp.eye(8, dtype=m.dtype)
    return jnp.sum(m * eye[None, :, None, :, None], axis=3).reshape(N_GROUPS, GROUP_DIM, N_STATE)


def _ssm_discretise(log_dt, a_re, a_im, b_re, b_im):
    dt = jnp.exp(log_dt)[:, None]
    mag = jnp.exp(a_re * dt)
    ang = a_im * dt
    abar_re = mag * jnp.cos(ang)
    abar_im = mag * jnp.sin(ang)
    den = a_re * a_re + a_im * a_im
    nr = abar_re - 1.0
    ni = abar_im
    k_re = (nr * a_re + ni * a_im) / den
    k_im = (ni * a_re - nr * a_im) / den
    bb_re = k_re[..., None] * b_re - k_im[..., None] * b_im
    bb_im = k_re[..., None] * b_im + k_im[..., None] * b_re
    return abar_re, abar_im, bb_re, bb_im


def _scan_tables(log_dt, a_re, a_im):
    dt = jnp.exp(log_dt)[:, None]
    lam_re = (a_re * dt).reshape(1, STATES)
    lam_im = (a_im * dt).reshape(1, STATES)

    def powers(k):
        mag = jnp.exp(k * lam_re)
        return mag * jnp.cos(k * lam_im), mag * jnp.sin(k * lam_im)

    p_re, p_im = powers((1 + jnp.arange(TM) // SUB).astype(F32)[:, None])
    q_re, q_im = powers((SEG_LEN * (1 + jnp.arange(SUB))).astype(F32)[:, None])
    return p_re, p_im, q_re, q_im


def _local_step(x, tgt, w, shard):
    row = lambda a: a.reshape(1, -1)
    (e_w_in,) = _gather_weights([shard["even_w_in"].astype(BF16)])
    wp = w["pool_w"].astype(BF16)
    ssm_in = (w["ssm_log_dt"], w["ssm_a_re"], w["ssm_a_im"], w["ssm_b_re"], w["ssm_b_im"])
    (abar_re, abar_im, bb_re, bb_im), ssm_vjp = jax.vjp(_ssm_discretise, *ssm_in)
    mb_re = _block_diag(bb_re.transpose(0, 2, 1)).astype(BF16)
    mb_im = _block_diag(bb_im.transpose(0, 2, 1)).astype(BF16)
    cm_re = _block_diag(w["ssm_c_re"]).astype(BF16)
    cm_im = _block_diag(w["ssm_c_im"]).astype(BF16)
    p8_re, p8_im, q_re, q_im = _scan_tables(w["ssm_log_dt"], w["ssm_a_re"], w["ssm_a_im"])
    qr_re, qr_im = q_re[::-1], q_im[::-1]
    pm = _perm_matrix()
    pmt = pm.T
    g0, gf = row(w["even_norm"]), row(w["final_norm"])
    ps, dskip = row(w["pool_scale"]), row(w["ssm_d"])

    proj, (g_glu, g_eout) = _norm_in(x, g0, e_w_in, "even_in",
                                     ride=[shard["ssm_w_glu"].astype(BF16), shard["even_w_out"].astype(BF16)])
    wglu = g_glu.transpose(1, 0, 2).reshape(SSM_W, 2 * SSM_W)
    e_w_out = g_eout.reshape(D_MODEL, D_MODEL)
    yp = _pool_fwd(proj, wp, ps)
    (ys, car_re, car_im), (g_oin, g_oout, g_small) = _ssm_fwd(
        proj, pm, pmt, mb_re, mb_im, p8_re, p8_im, q_re, q_im, cm_re, cm_im, dskip, wglu, ride=_odd_shards(shard))
    o_w_in, o_w_out = g_oin, g_oout.reshape(D_MODEL, D_MODEL)
    sm = g_small.transpose(1, 0, 2).reshape(SMALL_ROWS, D_MODEL)
    cw = sm[1:1 + HALO]
    g1, cb, lg, lb = sm[0:1], sm[32:33], sm[33:34], sm[34:35]
    x1, yg = _even_out(yp, ys, proj, x, e_w_out)
    q, _ = _norm_in(x1, g1, o_w_in, "odd_in")
    y2, cv = _conv_fwd(q, cw, cb, lg, lb)
    dx2, loss_lanes, d_gf = _odd_out_loss(y2, x1, o_w_out, gf, tgt)

    dcv, dz2, d_o_w_out, d_lg, d_lb = _odd_bwd_out(dx2, o_w_out, y2, cv, q, lg, lb)
    dval, dgate, d_cw, d_cb = _conv_bwd(dcv, q, cw)
    dx1, d_g1, d_o_w_in = _in_bwd([dval, dgate, dz2], o_w_in, x1, g1, dx2, "odd_in_bwd")
    g_odd = _pack_odd_grads({
        "odd_w_in": d_o_w_in, "odd_w_out": d_o_w_out, "odd_norm": d_g1.reshape(-1),
        "conv_w": d_cw.reshape(HALO, SUB, D_MODEL).sum(axis=1)[:CONV_K], "conv_b": d_cb.reshape(-1),
        "conv_ln_g": d_lg.reshape(-1), "conv_ln_b": d_lb.reshape(-1)})
    dycat, dz, d_e_w_out, theirs_odd = _even_bwd_out(dx1, e_w_out, yg, yp, ys, proj, g_odd)
    sums_odd = _pair_add(g_odd, theirs_odd, BF16, "pair_add_odd")
    dup, d_wp, d_ps = _pool_bwd(dycat, proj, wp, ps)
    (dus, d_mb_re, d_mb_im, d_cm_re, d_cm_im, da_re, da_im, d_dskip, d_wglu, got_odd) = _ssm_bwd(
        dycat, proj, car_re, car_im, pm, pmt, mb_re, mb_im, p8_re, p8_im, q_re, q_im, qr_re, qr_im,
        cm_re, cm_im, dskip, wglu, sums_odd)
    dx, d_g0, d_e_w_in = _in_bwd([dup, dus, dz], e_w_in, x, g0, dx1, "even_in_bwd")

    d_abar_re = jnp.sum(da_re, axis=0).reshape(N_GROUPS, N_STATE)
    d_abar_im = jnp.sum(da_im, axis=0).reshape(N_GROUPS, N_STATE)
    d_bb_re = _block_diag_take(d_mb_re).transpose(0, 2, 1)
    d_bb_im = _block_diag_take(d_mb_im).transpose(0, 2, 1)
    d_log_dt, d_a_re, d_a_im, d_b_re, d_b_im = ssm_vjp((d_abar_re, d_abar_im, d_bb_re, d_bb_im))

    grads = {
        "even_norm": d_g0.reshape(-1), "even_w_in": d_e_w_in, "pool_w": d_wp, "pool_scale": d_ps.reshape(-1),
        "ssm_log_dt": d_log_dt, "ssm_a_re": d_a_re, "ssm_a_im": d_a_im, "ssm_b_re": d_b_re, "ssm_b_im": d_b_im,
        "ssm_c_re": _block_diag_take(d_cm_re), "ssm_c_im": _block_diag_take(d_cm_im),
        "ssm_d": d_dskip.reshape(-1), "ssm_w_glu": d_wglu, "even_w_out": d_e_w_out, "final_norm": d_gf.reshape(-1),
    }
    return jnp.sum(loss_lanes), dx, grads, got_odd


WEIGHT_NAMES = ("even_norm", "even_w_in", "pool_w", "pool_scale", "ssm_log_dt", "ssm_a_re", "ssm_a_im",
                "ssm_b_re", "ssm_b_im", "ssm_c_re", "ssm_c_im", "ssm_d", "ssm_w_glu", "even_w_out", "odd_norm",
                "odd_w_in", "conv_w", "conv_b", "conv_ln_g", "conv_ln_b", "odd_w_out", "final_norm")
SHARDED = tuple(n for n, _ in SHARD_BIG) + tuple(n for n, _ in SHARD_SMALL)


SMALL_ROWS = 64


def _odd_shards(shard):
    small = jnp.concatenate([shard[n].reshape(r, 256) for n, r in SHARD_SMALL], axis=0)
    small = jnp.pad(small, ((0, SMALL_ROWS - small.shape[0]), (0, 0)))
    return [shard["odd_w_in"].astype(BF16), shard["odd_w_out"].astype(BF16), small]


def _pack_small(d):
    small = jnp.concatenate([d[n].reshape(r, -1) for n, r in SHARD_SMALL], axis=0)
    if small.shape[1] == D_MODEL:
        small = small.reshape(35, 4, 256).transpose(1, 0, 2)
    small = small.reshape(-1, 35 * 256)
    small = jnp.pad(small, ((0, 0), (0, ROWS_SMALL * D_MODEL - 35 * 256)))
    return small.reshape(-1, ROWS_SMALL, D_MODEL)


def _unpack_small(buf):
    small = buf.reshape(-1)[:35 * 256].reshape(35, 256)
    out = {}
    off = 0
    for n, r in SHARD_SMALL:
        out[n] = small[off:off + r].reshape((r, 256) if r > 1 else (256,))
        off += r
    return out


EVEN_PACK = (("even_w_in", 0, 512), ("ssm_w_glu", 512, 128), ("even_w_out", 640, 256))
ROWS_EVEN = 1024
ODD_PACK = (("odd_w_in", 0, 768), ("odd_w_out", 768, 256))
ODD_SMALL_ROW = 1024
ROWS_ODD = 1280


def _pack_even_grads(g):
    parts = [g["even_w_in"].reshape(4, -1, D_MODEL), _cols_split(g["ssm_w_glu"], 256),
             g["even_w_out"].reshape(4, -1, D_MODEL), jnp.zeros((4, ROWS_EVEN - 896, D_MODEL), F32)]
    return jnp.concatenate(parts, axis=1)


def _pack_odd_grads(g):
    parts = [g["odd_w_in"].reshape(4, -1, D_MODEL), g["odd_w_out"].reshape(4, -1, D_MODEL), _pack_small(g),
             jnp.zeros((4, ROWS_ODD - ODD_SMALL_ROW - ROWS_SMALL, D_MODEL), F32)]
    return jnp.concatenate(parts, axis=1)


def kernel(x, even_norm, even_w_in, pool_w, pool_scale, ssm_log_dt, ssm_a_re, ssm_a_im, ssm_b_re, ssm_b_im, ssm_c_re, ssm_c_im, ssm_d, ssm_w_glu, even_w_out, odd_norm, odd_w_in, conv_w, conv_b, conv_ln_g, conv_ln_b, odd_w_out, final_norm, loss_target, m_even_norm, m_even_w_in, m_pool_w, m_pool_scale, m_ssm_log_dt, m_ssm_a_re, m_ssm_a_im, m_ssm_b_re, m_ssm_b_im, m_ssm_c_re, m_ssm_c_im, m_ssm_d, m_ssm_w_glu, m_even_w_out, m_odd_norm, m_odd_w_in, m_conv_w, m_conv_b, m_conv_ln_g, m_conv_ln_b, m_odd_w_out, m_final_norm, v_even_norm, v_even_w_in, v_pool_w, v_pool_scale, v_ssm_log_dt, v_ssm_a_re, v_ssm_a_im, v_ssm_b_re, v_ssm_b_im, v_ssm_c_re, v_ssm_c_im, v_ssm_d, v_ssm_w_glu, v_even_w_out, v_odd_norm, v_odd_w_in, v_conv_w, v_conv_b, v_conv_ln_g, v_conv_ln_b, v_odd_w_out, v_final_norm):
    ws = dict(zip(WEIGHT_NAMES, (even_norm, even_w_in, pool_w, pool_scale, ssm_log_dt, ssm_a_re, ssm_a_im, ssm_b_re,
                                 ssm_b_im, ssm_c_re, ssm_c_im, ssm_d, ssm_w_glu, even_w_out, odd_norm, odd_w_in,
                                 conv_w, conv_b, conv_ln_g, conv_ln_b, odd_w_out, final_norm)))
    ms = dict(zip(WEIGHT_NAMES, (m_even_norm, m_even_w_in, m_pool_w, m_pool_scale, m_ssm_log_dt, m_ssm_a_re,
                                 m_ssm_a_im, m_ssm_b_re, m_ssm_b_im, m_ssm_c_re, m_ssm_c_im, m_ssm_d, m_ssm_w_glu,
                                 m_even_w_out, m_odd_norm, m_odd_w_in, m_conv_w, m_conv_b, m_conv_ln_g, m_conv_ln_b,
                                 m_odd_w_out, m_final_norm)))
    vs = dict(zip(WEIGHT_NAMES, (v_even_norm, v_even_w_in, v_pool_w, v_pool_scale, v_ssm_log_dt, v_ssm_a_re,
                                 v_ssm_a_im, v_ssm_b_re, v_ssm_b_im, v_ssm_c_re, v_ssm_c_im, v_ssm_d, v_ssm_w_glu,
                                 v_even_w_out, v_odd_norm, v_odd_w_in, v_conv_w, v_conv_b, v_conv_ln_g, v_conv_ln_b,
                                 v_odd_w_out, v_final_norm)))
    lead = {n: a.shape for n, a in ws.items()}
    drop = lambda d: {n: (a[0] if n != "final_norm" else a) for n, a in d.items()}
    ws, ms, vs = drop(ws), drop(ms), drop(vs)

    shard = {n: ws[n] for n in SHARDED}
    rep = {n: ws[n] for n, _ in REP_NAMES}
    loss_part, grad_x, grads, got_odd = _local_step(x[0], loss_target[0], rep, shard)
    loss = lax.psum(loss_part, ("x", "y", "c"))

    odd_mine = _sum_lead(got_odd, "chip_sum_odd")
    odd_theirs = _pair_join(odd_mine, "pair_join_odd")
    g_even = _pack_even_grads(grads)
    got_even, rep_parts = _scatter_and_gather(
        _pair_add(g_even, _pair_split(g_even, "pair_split_even"), BF16, "pair_add_even"),
        _pack_rep({n: grads[n] for n, _ in REP_NAMES}), "scatter_even_gather_rep")
    even_mine = _sum_lead(got_even, "chip_sum_even")
    even_theirs = _pair_join(even_mine, "pair_join_even")
    outs = [{}, {}, {}, {}]
    for pack, mine, theirs in ((EVEN_PACK, even_mine, even_theirs), (ODD_PACK, odd_mine, odd_theirs)):
        for n, row0, rows in pack:
            view = lambda a: a.reshape(rows, D_MODEL)
            res = _adamw_rows(view(ws[n]), view(ms[n]), view(vs[n]), mine, theirs, row0, 128, "adamw_" + n)
            for o, r in zip(outs, res):
                o[n] = r
    small = lambda d: _pack_small({n: d[n] for n, _ in SHARD_SMALL})[0]
    res = _adamw_rows(small(ws), small(ms), small(vs), odd_mine, odd_theirs, ODD_SMALL_ROW, ROWS_SMALL, "adamw_small")
    for o, r in zip(outs, res):
        o.update(_unpack_small(r))
    g_rep = _sum_lead(rep_parts, "rep_sum")
    res = _adamw(_pack_rep(rep), g_rep, _pack_rep({n: ms[n] for n, _ in REP_NAMES}),
                 _pack_rep({n: vs[n] for n, _ in REP_NAMES}), "adamw_rep")
    for o, r in zip(outs, (g_rep,) + tuple(res)):
        o.update(_unpack_rep(r))

    leaves = [[o[n].reshape(lead[n]) for n in WEIGHT_NAMES] for o in outs]
    return (loss, grad_x[None], *leaves[0], *leaves[1], *leaves[2], *leaves[3])
```

```python
import functools

import jax
import jax.numpy as jnp
from jax import lax
from jax.experimental import pallas as pl
from jax.experimental.pallas import tpu as pltpu

F32 = jnp.float32
BF16 = jnp.bfloat16
MESH = pl.DeviceIdType.MESH

D_MODEL = 1024
RMS_EPS = 1e-6
LN_EPS = 1e-5
N_GROUPS = 32
GROUP_DIM = 16
N_STATE = 64
STATES = N_GROUPS * N_STATE
SSM_W = 512
POOL_W = 512
CONV_K = 31
HALO = 32
POOL_HALO = 16

ADAM_LR = 0.001
ADAM_B1 = 0.9
ADAM_B2 = 0.999
ADAM_EPS = 1e-08
ADAM_WD = 0.01
ADAM_STEP = 10

TM = 256
TM_MM = 512
SUB = 8
LCH = 512
SCAN_L = 1024
VMEM_LIMIT = 56 * 1024 * 1024

ROWS_SMALL = 16
REP_ROWS = 200
COMM_CHUNKS = 4


def _params(n_axes=1):
    return pltpu.CompilerParams(dimension_semantics=("arbitrary",) * n_axes, vmem_limit_bytes=VMEM_LIMIT)


def _rows(w, cb=0, rev=None, tm=TM):
    if rev is None:
        return pl.BlockSpec((tm, w), lambda i: (i, cb))
    return pl.BlockSpec((tm, w), lambda i: (rev - 1 - i, cb))


def _mm_rows(w, cb=0):
    return _rows(w, cb, tm=TM_MM)


def _full(shape):
    n = len(shape)
    return pl.BlockSpec(shape, lambda i: (0,) * n)


def _prev(hr, w, cb=0, tm=TM):
    r = tm // hr
    return pl.BlockSpec((hr, w), lambda i: (jnp.maximum(i * r - 1, 0), cb))


def _next(hr, w, nrows, cb=0, tm=TM):
    r = tm // hr
    last = nrows // hr - 1
    return pl.BlockSpec((hr, w), lambda i: (jnp.minimum((i + 1) * r, last), cb))


def _dot(a, b):
    return jnp.dot(a, b, preferred_element_type=F32)


def _dot_nt(a, b):
    return lax.dot_general(a, b, (((1,), (1,)), ((), ())), preferred_element_type=F32)


def _dot_tn(a, b):
    return lax.dot_general(a, b, (((0,), (0,)), ((), ())), preferred_element_type=F32)


def _sig(x):
    return 1.0 / (1.0 + jnp.exp(-x))


def _zero_at_first(i, *refs):
    @pl.when(i == 0)
    def _():
        for r in refs:
            r[...] = jnp.zeros_like(r)


def _norm_in(x, g, w, name, ride=()):
    t, ns = x.shape[0], w.shape[2]
    n = 4 * ns
    ng = len(ride)
    nstep = t // TM_MM

    def body(x_ref, g_ref, w_ref, *rest):
        o_ref = rest[ng]
        i = pl.program_id(0)
        if ng:
            start, forward, finish = _gather_phases(rest[:ng], rest[ng + 1:2 * ng + 1], rest[2 * ng + 1:3 * ng + 1],
                                                    *rest[3 * ng + 1:])
            pl.when(i == 0)(start)
            pl.when(i == nstep // 2)(forward)
        xv = x_ref[...]
        r = lax.rsqrt(jnp.mean(xv * xv, axis=-1, keepdims=True) + RMS_EPS)
        h = (xv * r * g_ref[...]).astype(BF16)
        for s in range(4):
            o_ref[:, s * ns:(s + 1) * ns] = _dot(h, w_ref[s]).astype(BF16)
        if ng:
            pl.when(i == nstep - 1)(finish)

    res = pl.pallas_call(
        body, name=name, grid=(nstep,),
        in_specs=[_mm_rows(D_MODEL), _full((1, D_MODEL)), _full(w.shape)] + [ANY] * ng,
        out_specs=[_mm_rows(n)] + [ANY] * ng,
        out_shape=[jax.ShapeDtypeStruct((t, n), BF16)] + [jax.ShapeDtypeStruct((4,) + a.shape, a.dtype) for a in ride],
        scratch_shapes=_gather_scratch(ride) if ng else [],
        compiler_params=_params())(x, g, w, *ride)
    return res[0], res[1:]


def _pool_sums(ext, g, forward):
    n = ext.shape[0]
    s = ext
    for step in range(g + 1):
        k = 1 << step
        s = s + pltpu.roll(s, k if forward else n - k, 0)
    return s


def _pool_fwd(proj, wp, ps):
    t = proj.shape[0]

    def body(u_ref, h_ref, wp_ref, ps_ref, y_ref):
        i = pl.program_id(0)
        pos = (i * TM + 1 + lax.broadcasted_iota(jnp.int32, (TM, 1), 0)).astype(F32)
        for g in range(4):
            sl = slice(128 * g, 128 * (g + 1))
            u = u_ref[:, sl].astype(F32)
            halo = jnp.where(i == 0, 0.0, h_ref[:, sl].astype(F32))
            s = _pool_sums(jnp.concatenate([halo, u], axis=0), g, True)[POOL_HALO:, :]
            pooled = s / jnp.minimum(pos, float(2 << g)) - u
            y_ref[:, sl] = _dot(pooled.astype(BF16), wp_ref[g]) * ps_ref[:, sl]

    return pl.pallas_call(
        body, name="pool_fwd", grid=(t // TM,),
        in_specs=[_rows(POOL_W, 0), _prev(POOL_HALO, POOL_W, 0), _full((4, 128, 128)), _full((1, POOL_W))],
        out_specs=_rows(POOL_W), out_shape=jax.ShapeDtypeStruct((t, POOL_W), F32),
        compiler_params=_params())(proj, proj, wp, ps)


SEG_LEN = TM // SUB


def _perm_matrix():
    p = jnp.arange(TM)
    src = (p % SUB) * SEG_LEN + p // SUB
    return (src[:, None] == jnp.arange(TM)[None, :]).astype(BF16)


def _cmul_add(are, aim, vre, vim, bre, bim):
    return are * vre - aim * vim + bre, are * vim + aim * vre + bim


def _segment_chain(ere, eim, qre, qim, cin_re, cin_im, row, up):
    for sh in (1, 2, 4):
        mre, mim = (qre[SUB - sh:SUB - sh + 1, :], qim[SUB - sh:SUB - sh + 1, :]) if up else \
                   (qre[sh - 1:sh, :], qim[sh - 1:sh, :])
        keep = (row < SUB - sh) if up else (row >= sh)
        sre = jnp.where(keep, pltpu.roll(ere, SUB - sh if up else sh, 0), 0.0)
        sim = jnp.where(keep, pltpu.roll(eim, SUB - sh if up else sh, 0), 0.0)
        ere, eim = _cmul_add(mre, mim, sre, sim, ere, eim)
    ere, eim = _cmul_add(qre, qim, cin_re, cin_im, ere, eim)
    keep = (row < SUB - 1) if up else (row >= 1)
    ent_re = jnp.where(keep, pltpu.roll(ere, SUB - 1 if up else 1, 0), cin_re)
    ent_im = jnp.where(keep, pltpu.roll(eim, SUB - 1 if up else 1, 0), cin_im)
    return ere, eim, ent_re, ent_im


def _scan_fwd_block(xs_re, xs_im, p8_re, p8_im, q_re, q_im, car_re, car_im, ent_re_ref, ent_im_ref):
    row = lax.broadcasted_iota(jnp.int32, (SUB, SCAN_L), 0)
    for j in range(STATES // SCAN_L):
        sl = slice(SCAN_L * j, SCAN_L * (j + 1))
        are, aim = p8_re[0:SUB, sl], p8_im[0:SUB, sl]

        def totals(i, v, sl=sl, are=are, aim=aim):
            r0 = pl.multiple_of(i * SUB, SUB)
            vre, vim = _cmul_add(are, aim, v[0], v[1], xs_re[pl.ds(r0, SUB), sl], xs_im[pl.ds(r0, SUB), sl])
            xs_re[pl.ds(r0, SUB), sl] = vre
            xs_im[pl.ds(r0, SUB), sl] = vim
            return vre, vim

        ere, eim = lax.fori_loop(1, SEG_LEN, totals, (xs_re[0:SUB, sl], xs_im[0:SUB, sl]), unroll=2)
        ere, eim, cre, cim = _segment_chain(ere, eim, q_re[:, sl], q_im[:, sl],
                                            car_re[:, sl], car_im[:, sl], row, False)
        car_re[:, sl] = jnp.broadcast_to(ere[SUB - 1:SUB, :], (SUB, SCAN_L))
        car_im[:, sl] = jnp.broadcast_to(eim[SUB - 1:SUB, :], (SUB, SCAN_L))
        if ent_re_ref is not None:
            ent_re_ref[:, sl] = cre
            ent_im_ref[:, sl] = cim

        def fix(i, c, sl=sl, cre=cre, cim=cim):
            r0 = pl.multiple_of(i * SUB, SUB)
            vre, vim = _cmul_add(p8_re[pl.ds(r0, SUB), sl], p8_im[pl.ds(r0, SUB), sl], cre, cim,
                                 xs_re[pl.ds(r0, SUB), sl], xs_im[pl.ds(r0, SUB), sl])
            xs_re[pl.ds(r0, SUB), sl] = vre
            xs_im[pl.ds(r0, SUB), sl] = vim
            return c

        lax.fori_loop(0, SEG_LEN, fix, 0, unroll=2)


def _unpermute(pmt_ref, v):
    hi = v.astype(BF16)
    lo = (v - hi.astype(F32)).astype(BF16)
    return _dot(pmt_ref[...], hi) + _dot(pmt_ref[...], lo)


def _ssm_fwd(proj, pm, pmt, mb_re, mb_im, p8_re, p8_im, q_re, q_im, cm_re, cm_im, dskip, wglu, ride=()):
    t = proj.shape[0]
    nblk = t // TM

    ng = len(ride)

    def body(u_ref, pm_ref, pmt_ref, mbre, mbim, p8re, p8im, qre, qim, cmre, cmim, d_ref, wg_ref, *rest):
        ride_in, (y_ref, cre_ref, cim_ref), ride_out = rest[:ng], rest[ng:ng + 3], rest[ng + 3:2 * ng + 3]
        xs_re, xs_im, car_re, car_im, ysk = rest[2 * ng + 3:2 * ng + 8]
        i = pl.program_id(0)
        if ng:
            start, forward, finish = _gather_phases(ride_in, ride_out, rest[2 * ng + 8:3 * ng + 8],
                                                    *rest[3 * ng + 8:])
            pl.when(i == 0)(start)
            pl.when(i == nblk // 2)(forward)
        _zero_at_first(i, car_re, car_im)
        cre_ref[0] = car_re[...]
        cim_ref[0] = car_im[...]
        us = _dot(pm_ref[...], u_ref[...])
        usb = us.astype(BF16)
        for j in range(4):
            xs_re[:, LCH * j:LCH * (j + 1)] = _dot(usb[:, 128 * j:128 * (j + 1)], mbre[j])
            xs_im[:, LCH * j:LCH * (j + 1)] = _dot(usb[:, 128 * j:128 * (j + 1)], mbim[j])
        _scan_fwd_block(xs_re, xs_im, p8re, p8im, qre, qim, car_re, car_im, None, None)
        for j in range(4):
            sl = slice(LCH * j, LCH * (j + 1))
            ysk[:, 128 * j:128 * (j + 1)] = (_dot_nt(xs_re[:, sl].astype(BF16), cmre[j])
                                             - _dot_nt(xs_im[:, sl].astype(BF16), cmim[j]))
        yv = ysk[...] + d_ref[...] * us
        gv = _dot(yv.astype(BF16), wg_ref[...])
        y_ref[...] = _unpermute(pmt_ref, gv[:, :SSM_W] * _sig(gv[:, SSM_W:]))
        if ng:
            pl.when(i == nblk - 1)(finish)

    blk = (4, 128, LCH)
    res = pl.pallas_call(
        body, name="ssm_fwd", grid=(nblk,),
        in_specs=[_rows(SSM_W, 1), _full((TM, TM)), _full((TM, TM)), _full(blk), _full(blk),
                  _full((TM, STATES)), _full((TM, STATES)), _full((SUB, STATES)), _full((SUB, STATES)),
                  _full(blk), _full(blk), _full((1, SSM_W)), _full((SSM_W, 2 * SSM_W))] + [ANY] * ng,
        out_specs=[_rows(SSM_W), pl.BlockSpec((1, SUB, STATES), lambda i: (i, 0, 0)),
                   pl.BlockSpec((1, SUB, STATES), lambda i: (i, 0, 0))] + [ANY] * ng,
        out_shape=[jax.ShapeDtypeStruct((t, SSM_W), F32), jax.ShapeDtypeStruct((nblk, SUB, STATES), F32),
                   jax.ShapeDtypeStruct((nblk, SUB, STATES), F32)]
        + [jax.ShapeDtypeStruct((4,) + a.shape, a.dtype) for a in ride],
        scratch_shapes=[pltpu.VMEM((TM, STATES), F32), pltpu.VMEM((TM, STATES), F32),
                        pltpu.VMEM((SUB, STATES), F32), pltpu.VMEM((SUB, STATES), F32),
                        pltpu.VMEM((TM, SSM_W), F32)] + (_gather_scratch(ride) if ng else []),
        compiler_params=_params())(proj, pm, pmt, mb_re, mb_im, p8_re, p8_im, q_re, q_im, cm_re, cm_im, dskip, wglu,
                                   *ride)
    return res[:3], res[3:]


def _even_out(yp, ys, proj, x, w):
    t = x.shape[0]

    def body(yp_ref, ys_ref, z_ref, x_ref, w_ref, x1_ref, yg_ref):
        z = z_ref[...].astype(F32)
        sz = z * _sig(z)
        gp = (yp_ref[...] * sz[:, :POOL_W]).astype(BF16)
        gs = (ys_ref[...] * sz[:, POOL_W:]).astype(BF16)
        yg_ref[:, :POOL_W] = gp
        yg_ref[:, POOL_W:] = gs
        x1_ref[...] = x_ref[...] + _dot(gp, w_ref[:POOL_W, :]) + _dot(gs, w_ref[POOL_W:, :])

    return pl.pallas_call(
        body, name="even_out", grid=(t // TM_MM,),
        in_specs=[_mm_rows(POOL_W), _mm_rows(SSM_W), _mm_rows(D_MODEL, 1), _mm_rows(D_MODEL),
                  _full((D_MODEL, D_MODEL))],
        out_specs=[_mm_rows(D_MODEL), _mm_rows(D_MODEL)],
        out_shape=[jax.ShapeDtypeStruct((t, D_MODEL), F32), jax.ShapeDtypeStruct((t, D_MODEL), BF16)],
        compiler_params=_params())(yp, ys, proj, x, w)


def _phase_copies(ext, cp):
    n = cp.shape[1]
    for j in range(1, SUB):
        cp[j - 1] = ext[pl.ds(j, n), :]


def _shifted(ext, cp, off, nrows, sl, row0=0):
    q, j = divmod(off, SUB)
    if j == 0:
        return ext[pl.ds(row0 + SUB * q, nrows), sl]
    return cp[j - 1, pl.ds(row0 + SUB * q, nrows), sl]


def _conv_taps(ext, cp, w_ref, first, nrows, sl, init, row0=0):
    acc = init
    for k in range(CONV_K):
        acc = acc + w_ref[k:k + 1, sl] * _shifted(ext, cp, first(k), nrows, sl, row0)
    return acc


def _conv_fwd(q, cw, cb, lg, lb):
    t = q.shape[0]

    def body(v_ref, g_ref, hv_ref, hg_ref, z_ref, w_ref, b_ref, lg_ref, lb_ref, y_ref, cv_ref, ext, cp):
        i = pl.program_id(0)
        ext[0:HALO, :] = jnp.where(i == 0, 0.0, hv_ref[...].astype(F32) * _sig(hg_ref[...].astype(F32)))
        ext[HALO:, :] = v_ref[...].astype(F32) * _sig(g_ref[...].astype(F32))
        _phase_copies(ext, cp)

        def lanes(c, carry):
            sl = pl.ds(pl.multiple_of(c * 128, 128), 128)
            cv_ref[:, sl] = _conv_taps(ext, cp, w_ref, lambda k: k + 2, TM, sl,
                                       jnp.broadcast_to(b_ref[:, sl], (TM, 128)))
            return carry

        lax.fori_loop(0, D_MODEL // 128, lanes, 0)
        cv = cv_ref[...]
        cc = cv - jnp.mean(cv, axis=-1, keepdims=True)
        rstd = lax.rsqrt(jnp.mean(cc * cc, axis=-1, keepdims=True) + LN_EPS)
        cl = cc * rstd * lg_ref[...] + lb_ref[...]
        z = z_ref[...].astype(F32)
        y_ref[...] = (cl * _sig(cl) * z * _sig(z)).astype(BF16)

    vec = _full((1, D_MODEL))
    return pl.pallas_call(
        body, name="conv_fwd", grid=(t // TM,),
        in_specs=[_rows(D_MODEL, 0), _rows(D_MODEL, 1), _prev(HALO, D_MODEL, 0), _prev(HALO, D_MODEL, 1),
                  _rows(D_MODEL, 2), _full((HALO, D_MODEL)), vec, vec, vec],
        out_specs=[_rows(D_MODEL), _rows(D_MODEL)],
        out_shape=[jax.ShapeDtypeStruct((t, D_MODEL), BF16), jax.ShapeDtypeStruct((t, D_MODEL), F32)],
        scratch_shapes=[pltpu.VMEM((TM + HALO, D_MODEL), F32),
                        pltpu.VMEM((SUB - 1, TM + HALO - SUB, D_MODEL), F32)],
        compiler_params=_params())(q, q, q, q, q, cw, cb, lg, lb)


def _odd_out_loss(y2, x1, w, gf, tgt):
    t = x1.shape[0]

    def body(y_ref, x_ref, w_ref, g_ref, t_ref, dx_ref, loss_ref, dg_ref):
        i = pl.program_id(0)
        _zero_at_first(i, loss_ref, dg_ref)
        x2 = x_ref[...] + _dot(y_ref[...], w_ref[...])
        r = lax.rsqrt(jnp.mean(x2 * x2, axis=-1, keepdims=True) + RMS_EPS)
        n = x2 * r
        e = n * g_ref[...] - t_ref[...]
        loss_ref[...] += jnp.sum(e * e, axis=0, keepdims=True) * (0.5 / D_MODEL)
        dout = e * (1.0 / D_MODEL)
        dg_ref[...] += jnp.sum(dout * n, axis=0, keepdims=True)
        dn = dout * g_ref[...]
        dx_ref[...] = r * (dn - n * jnp.mean(dn * n, axis=-1, keepdims=True))

    vec = _full((1, D_MODEL))
    return pl.pallas_call(
        body, name="odd_out_loss", grid=(t // TM_MM,),
        in_specs=[_mm_rows(D_MODEL), _mm_rows(D_MODEL), _full((D_MODEL, D_MODEL)), vec, _mm_rows(D_MODEL)],
        out_specs=[_mm_rows(D_MODEL), vec, vec],
        out_shape=[jax.ShapeDtypeStruct((t, D_MODEL), F32), jax.ShapeDtypeStruct((1, D_MODEL), F32),
                   jax.ShapeDtypeStruct((1, D_MODEL), F32)],
        compiler_params=_params())(y2, x1, w, gf, tgt)


def _dsilu(z):
    s = _sig(z)
    return z * s, s * (1.0 + z * (1.0 - s))


def _odd_bwd_out(dx2, w, y2, cv, q, lg, lb):
    t = dx2.shape[0]

    def body(dx_ref, w_ref, y_ref, cv_ref, z_ref, lg_ref, lb_ref, dcv_ref, dz_ref, dw_ref, dlg_ref, dlb_ref):
        i = pl.program_id(0)
        _zero_at_first(i, dw_ref, dlg_ref, dlb_ref)
        dxb = dx_ref[...].astype(BF16)
        dy = _dot_nt(dxb, w_ref[...])
        dw_ref[...] += _dot_tn(y_ref[...], dxb)
        cv = cv_ref[...]
        cc = cv - jnp.mean(cv, axis=-1, keepdims=True)
        rstd = lax.rsqrt(jnp.mean(cc * cc, axis=-1, keepdims=True) + LN_EPS)
        cn = cc * rstd
        silu_c, dsilu_c = _dsilu(cn * lg_ref[...] + lb_ref[...])
        silu_z, dsilu_z = _dsilu(z_ref[...].astype(F32))
        dcl = dy * silu_z * dsilu_c
        dz_ref[...] = (dy * silu_c * dsilu_z).astype(BF16)
        dlg_ref[...] += jnp.sum(dcl * cn, axis=0, keepdims=True)
        dlb_ref[...] += jnp.sum(dcl, axis=0, keepdims=True)
        dcn = dcl * lg_ref[...]
        dcv_ref[...] = rstd * (dcn - jnp.mean(dcn, axis=-1, keepdims=True)
                               - cn * jnp.mean(dcn * cn, axis=-1, keepdims=True))

    vec = _full((1, D_MODEL))
    mat = _full((D_MODEL, D_MODEL))
    return pl.pallas_call(
        body, name="odd_bwd_out", grid=(t // TM_MM,),
        in_specs=[_mm_rows(D_MODEL), mat, _mm_rows(D_MODEL), _mm_rows(D_MODEL), _mm_rows(D_MODEL, 2), vec, vec],
        out_specs=[_mm_rows(D_MODEL), _mm_rows(D_MODEL), mat, vec, vec],
        out_shape=[jax.ShapeDtypeStruct((t, D_MODEL), F32), jax.ShapeDtypeStruct((t, D_MODEL), BF16),
                   jax.ShapeDtypeStruct((D_MODEL, D_MODEL), F32), jax.ShapeDtypeStruct((1, D_MODEL), F32),
                   jax.ShapeDtypeStruct((1, D_MODEL), F32)],
        compiler_params=_params())(dx2, w, y2, cv, q, lg, lb)


def _conv_bwd(dcv, q, cw):
    t = dcv.shape[0]
    nblk = t // TM

    def body(d_ref, dn_ref, v_ref, g_ref, hv_ref, hg_ref, w_ref,
             dv_ref, dgt_ref, dw_ref, db_ref, gext, dext, dgl, gcp, dcp):
        i = pl.program_id(0)
        last = nblk - 1
        _zero_at_first(i, dw_ref, db_ref)
        v = v_ref[...].astype(F32)
        sg = _sig(g_ref[...].astype(F32))
        gext[0:HALO, :] = jnp.where(i == 0, 0.0, hv_ref[...].astype(F32) * _sig(hg_ref[...].astype(F32)))
        gext[HALO:, :] = v * sg
        d = d_ref[...]
        dext[0:TM, :] = d
        dext[TM:, :] = jnp.where(i == last, 0.0, dn_ref[...])
        _phase_copies(gext, gcp)
        _phase_copies(dext, dcp)
        db_ref[...] += jnp.sum(d, axis=0, keepdims=True)
        def lanes(c, carry):
            sl = pl.ds(pl.multiple_of(c * 128, 128), 128)
            dgl[:, sl] = _conv_taps(dext, dcp, w_ref, lambda k: 30 - k, TM, sl, jnp.zeros((TM, 128), F32))
            return carry

        def lanes_w(c, carry):
            sl = pl.ds(pl.multiple_of(c * 128, 128), 128)
            ntile = TM // SUB
            dts = [d_ref[SUB * r:SUB * (r + 1), sl] for r in range(ntile)]
            for j in range(SUB):
                taps = [(q, SUB * q + j - 2) for q in range(5) if 0 <= SUB * q + j - 2 < CONV_K]
                sums = {k: None for _, k in taps}
                for rt in range(ntile + 4):
                    need = [(q, k) for q, k in taps if 0 <= rt - q < ntile]
                    if not need:
                        continue
                    src = gext[SUB * rt:SUB * (rt + 1), sl] if j == 0 else gcp[j - 1, SUB * rt:SUB * (rt + 1), sl]
                    for q, k in need:
                        prod = dts[rt - q] * src
                        sums[k] = prod if sums[k] is None else sums[k] + prod
                for _, k in taps:
                    dw_ref[SUB * k:SUB * (k + 1), sl] += sums[k]
            return carry

        lax.fori_loop(0, D_MODEL // 128, lanes, 0)
        lax.fori_loop(0, D_MODEL // 128, lanes_w, 0)
        dg = dgl[...]
        dv_ref[...] = (dg * sg).astype(BF16)
        dgt_ref[...] = (dg * v * sg * (1.0 - sg)).astype(BF16)

    return pl.pallas_call(
        body, name="conv_bwd", grid=(t // TM,),
        in_specs=[_rows(D_MODEL), _next(HALO, D_MODEL, t), _rows(D_MODEL, 0), _rows(D_MODEL, 1),
                  _prev(HALO, D_MODEL, 0), _prev(HALO, D_MODEL, 1), _full((HALO, D_MODEL))],
        out_specs=[_rows(D_MODEL), _rows(D_MODEL), _full((HALO * SUB, D_MODEL)), _full((1, D_MODEL))],
        out_shape=[jax.ShapeDtypeStruct((t, D_MODEL), BF16), jax.ShapeDtypeStruct((t, D_MODEL), BF16),
                   jax.ShapeDtypeStruct((HALO * SUB, D_MODEL), F32), jax.ShapeDtypeStruct((1, D_MODEL), F32)],
        scratch_shapes=[pltpu.VMEM((TM + HALO, D_MODEL), F32), pltpu.VMEM((TM + HALO, D_MODEL), F32),
                        pltpu.VMEM((TM, D_MODEL), F32),
                        pltpu.VMEM((SUB - 1, TM + HALO - SUB, D_MODEL), F32),
                        pltpu.VMEM((SUB - 1, TM + HALO - SUB, D_MODEL), F32)],
        compiler_params=_params())(dcv, dcv, q, q, q, q, cw)


def _column_segments(widths, ns):
    segs = []
    col = 0
    for p, wd in enumerate(widths):
        a = 0
        while a < wd:
            s, lo = divmod(col + a, ns)
            ln = min(wd - a, ns - lo)
            segs.append((p, a, a + ln, s, lo, lo + ln))
            a += ln
        col += wd
    return segs


def _in_bwd(dparts, w, x, g, dres, name, ride=None):
    t = x.shape[0]
    widths = [p.shape[1] for p in dparts]
    npart = len(dparts)
    segs = _column_segments(widths, w.shape[2])
    nstep = t // TM_MM

    def body(*refs):
        d_refs = refs[:npart]
        w_ref, x_ref, g_ref, r_ref = refs[npart:npart + 4]
        if ride is None:
            dx_ref, dg_ref, dw_ref = refs[npart + 4:]
        else:
            v_ref, dx_ref, dg_ref, dw_ref, o_ref = refs[npart + 4:npart + 9]
            start, forward, finish = _gather_all_phases(v_ref, o_ref, *refs[npart + 9:])
        i = pl.program_id(0)
        if ride is not None:
            pl.when(i == 0)(start)
            pl.when(i == nstep // 2)(forward)
        _zero_at_first(i, dg_ref, dw_ref)
        xv = x_ref[...]
        r = lax.rsqrt(jnp.mean(xv * xv, axis=-1, keepdims=True) + RMS_EPS)
        n = xv * r
        h = (n * g_ref[...]).astype(BF16)
        dh = None
        for p, lo, hi, s, slo, shi in segs:
            d = d_refs[p][:, lo:hi]
            part = _dot_nt(d, w_ref[s, :, slo:shi])
            dh = part if dh is None else dh + part
            dw_ref[s, :, slo:shi] += _dot_tn(h, d)
        dg_ref[...] += jnp.sum(dh * n, axis=0, keepdims=True)
        dn = dh * g_ref[...]
        dx_ref[...] = r_ref[...] + r * (dn - n * jnp.mean(dn * n, axis=-1, keepdims=True))
        if ride is not None:
            pl.when(i == nstep - 1)(finish)

    vec = _full((1, D_MODEL))
    once = pl.BlockSpec(w.shape, lambda i: (0, 0, 0), pipeline_mode=pl.Buffered(1))
    extra = [] if ride is None else [ride]
    return pl.pallas_call(
        body, name=name, grid=(nstep,),
        in_specs=[_mm_rows(wd) for wd in widths] + [once, _mm_rows(D_MODEL), vec, _mm_rows(D_MODEL)]
        + [ANY] * len(extra),
        out_specs=[_mm_rows(D_MODEL), vec, once] + [ANY] * len(extra),
        out_shape=[jax.ShapeDtypeStruct((t, D_MODEL), F32), jax.ShapeDtypeStruct((1, D_MODEL), F32),
                   jax.ShapeDtypeStruct(w.shape, F32)]
        + [jax.ShapeDtypeStruct((8,) + v.shape, v.dtype) for v in extra],
        scratch_shapes=_gather_all_scratch(ride) if extra else [],
        compiler_params=_params())(*dparts, w, x, g, dres, *extra)


def _even_bwd_out(dx1, w, yg, yp, ys, proj, ride):
    t = dx1.shape[0]
    nstep = t // TM_MM

    def body(dx_ref, w_ref, yg_ref, yp_ref, ys_ref, z_ref, g_ref, dy_ref, dz_ref, dw_ref, theirs_ref,
             send_sems, recv_sems):
        i = pl.program_id(0)
        start, finish = _pair_split_phases(g_ref, theirs_ref, send_sems, recv_sems)
        pl.when(i == 0)(start)
        _zero_at_first(i, dw_ref)
        dxb = dx_ref[...].astype(BF16)
        dyg = _dot_nt(dxb, w_ref[...])
        dw_ref[...] += _dot_tn(yg_ref[...], dxb)
        silu_z, dsilu_z = _dsilu(z_ref[...].astype(F32))
        dy_ref[...] = (dyg * silu_z).astype(BF16)
        dz_ref[:, :POOL_W] = (dyg[:, :POOL_W] * yp_ref[...] * dsilu_z[:, :POOL_W]).astype(BF16)
        dz_ref[:, POOL_W:] = (dyg[:, POOL_W:] * ys_ref[...] * dsilu_z[:, POOL_W:]).astype(BF16)
        pl.when(i == nstep - 1)(finish)

    mat = _full((D_MODEL, D_MODEL))
    return pl.pallas_call(
        body, name="even_bwd_out", grid=(nstep,),
        in_specs=[_mm_rows(D_MODEL), mat, _mm_rows(D_MODEL), _mm_rows(POOL_W), _mm_rows(SSM_W), _mm_rows(D_MODEL, 1),
                  ANY],
        out_specs=[_mm_rows(D_MODEL), _mm_rows(D_MODEL), mat, ANY],
        out_shape=[jax.ShapeDtypeStruct((t, D_MODEL), BF16), jax.ShapeDtypeStruct((t, D_MODEL), BF16),
                   jax.ShapeDtypeStruct((D_MODEL, D_MODEL), F32), _pair_split_shape(ride)],
        scratch_shapes=_pair_split_scratch(ride),
        compiler_params=_params())(dx1, w, yg, yp, ys, proj, ride)


def _pool_bwd(dycat, proj, wp, ps):
    t = proj.shape[0]

    def body(dy_ref, dyn_ref, u_ref, h_ref, wp_ref, ps_ref, du_ref, dwp_ref, dps_ref):
        i = pl.program_id(0)
        last = t // TM - 1
        _zero_at_first(i, dwp_ref, dps_ref)
        pos = (i * TM + 1 + lax.broadcasted_iota(jnp.int32, (TM, 1), 0)).astype(F32)
        pos_ext = (i * TM + 1 + lax.broadcasted_iota(jnp.int32, (TM + POOL_HALO, 1), 0)).astype(F32)
        for g in range(4):
            sl = slice(128 * g, 128 * (g + 1))
            w = float(2 << g)
            u = u_ref[:, sl].astype(F32)
            halo = jnp.where(i == 0, 0.0, h_ref[:, sl].astype(F32))
            s = _pool_sums(jnp.concatenate([halo, u], axis=0), g, True)[POOL_HALO:, :]
            pooled = (s / jnp.minimum(pos, w) - u).astype(BF16)
            dy = dy_ref[:, sl].astype(F32)
            dps_ref[:, sl] += jnp.sum(dy * _dot(pooled, wp_ref[g]), axis=0, keepdims=True)
            dy_ext = jnp.concatenate([dy, jnp.where(i == last, 0.0, dyn_ref[:, sl].astype(F32))], axis=0)
            dmix = (dy_ext * ps_ref[:, sl]).astype(BF16)
            dwp_ref[g] += _dot_tn(pooled, dmix[:TM, :])
            dpool = _dot_nt(dmix, wp_ref[g])
            lead = _pool_sums(dpool / jnp.minimum(pos_ext, w), g, False)
            du_ref[:, sl] = (lead[:TM, :] - dpool[:TM, :]).astype(BF16)

    return pl.pallas_call(
        body, name="pool_bwd", grid=(t // TM,),
        in_specs=[_rows(POOL_W, 0), _next(POOL_HALO, POOL_W, t, 0), _rows(POOL_W, 0), _prev(POOL_HALO, POOL_W, 0),
                  _full((4, 128, 128)), _full((1, POOL_W))],
        out_specs=[_rows(POOL_W), _full((4, 128, 128)), _full((1, POOL_W))],
        out_shape=[jax.ShapeDtypeStruct((t, POOL_W), BF16), jax.ShapeDtypeStruct((4, 128, 128), F32),
                   jax.ShapeDtypeStruct((1, POOL_W), F32)],
        compiler_params=_params())(dycat, dycat, proj, proj, wp, ps)


def _ssm_bwd(dycat, proj, car_in_re, car_in_im, pm, pmt, mb_re, mb_im, p8_re, p8_im, q_re, q_im, qr_re, qr_im,
             cm_re, cm_im, dskip, wglu, ride):
    t = proj.shape[0]
    nblk = t // TM

    def body(dy_ref, u_ref, cin_re, cin_im, pm_ref, pmt_ref, mbre, mbim, p8re, p8im, qre, qim, qrre, qrim,
             cmre, cmim, d_ref, wg_ref, p_ref,
             du_ref, dmbre, dmbim, dcmre, dcmim, dare, daim, dd_ref, dwg_ref, got_ref,
             xs_re, xs_im, gs_re, gs_im, car_re, car_im, ent_re, ent_im, gcar_re, gcar_im, ysk, dysk,
             bounce, send_sems, recv_sems, local_sems):
        i = pl.program_id(0)
        start, finish = _chip_scatter_phases(p_ref, got_ref, bounce, send_sems, recv_sems, local_sems)
        pl.when(i == 0)(start)
        _zero_at_first(i, dmbre, dmbim, dcmre, dcmim, dare, daim, dd_ref, dwg_ref, gcar_re, gcar_im)
        us = _dot(pm_ref[...], u_ref[...])
        usb = us.astype(BF16)
        for j in range(4):
            xs_re[:, LCH * j:LCH * (j + 1)] = _dot(usb[:, 128 * j:128 * (j + 1)], mbre[j])
            xs_im[:, LCH * j:LCH * (j + 1)] = _dot(usb[:, 128 * j:128 * (j + 1)], mbim[j])
        car_re[...] = cin_re[0]
        car_im[...] = cin_im[0]
        _scan_fwd_block(xs_re, xs_im, p8re, p8im, qre, qim, car_re, car_im, ent_re, ent_im)
        for j in range(4):
            sl = slice(LCH * j, LCH * (j + 1))
            ysk[:, 128 * j:128 * (j + 1)] = (_dot_nt(xs_re[:, sl].astype(BF16), cmre[j])
                                             - _dot_nt(xs_im[:, sl].astype(BF16), cmim[j]))
        yvb = (ysk[...] + d_ref[...] * us).astype(BF16)
        gv = _dot(yvb, wg_ref[...])
        sg = _sig(gv[:, SSM_W:])
        dyss = _dot(pm_ref[...], dy_ref[...])
        dval = (dyss * sg).astype(BF16)
        dgate = (dyss * gv[:, :SSM_W] * sg * (1.0 - sg)).astype(BF16)
        dy = _dot_nt(dval, wg_ref[:, :SSM_W]) + _dot_nt(dgate, wg_ref[:, SSM_W:])
        dwg_ref[:, :SSM_W] += _dot_tn(yvb, dval)
        dwg_ref[:, SSM_W:] += _dot_tn(yvb, dgate)
        dd_ref[...] += jnp.sum(dy * us, axis=0, keepdims=True)
        dysk[...] = dy
        for j in range(4):
            sl = slice(LCH * j, LCH * (j + 1))
            dyj = dy[:, 128 * j:128 * (j + 1)].astype(BF16)
            gs_re[:, sl] = _dot(dyj, cmre[j])
            gs_im[:, sl] = -_dot(dyj, cmim[j])
            dcmre[j] += _dot_tn(dyj, xs_re[:, sl].astype(BF16))
            dcmim[j] -= _dot_tn(dyj, xs_im[:, sl].astype(BF16))
        row = lax.broadcasted_iota(jnp.int32, (SUB, SCAN_L), 0)
        for j in range(STATES // SCAN_L):
            sl = slice(SCAN_L * j, SCAN_L * (j + 1))
            are, aim = p8re[0:SUB, sl], -p8im[0:SUB, sl]

            def totals(k, v, sl=sl, are=are, aim=aim):
                r0 = pl.multiple_of((SEG_LEN - 2 - k) * SUB, SUB)
                vre, vim = _cmul_add(are, aim, v[0], v[1], gs_re[pl.ds(r0, SUB), sl], gs_im[pl.ds(r0, SUB), sl])
                gs_re[pl.ds(r0, SUB), sl] = vre
                gs_im[pl.ds(r0, SUB), sl] = vim
                return vre, vim

            top = (SEG_LEN - 1) * SUB
            fre, fim = lax.fori_loop(0, SEG_LEN - 1, totals,
                                     (gs_re[top:top + SUB, sl], gs_im[top:top + SUB, sl]), unroll=2)
            fre, fim, nre, nim = _segment_chain(fre, fim, qrre[:, sl], -qrim[:, sl],
                                                gcar_re[:, sl], gcar_im[:, sl], row, True)
            gcar_re[:, sl] = jnp.broadcast_to(fre[0:1, :], (SUB, SCAN_L))
            gcar_im[:, sl] = jnp.broadcast_to(fim[0:1, :], (SUB, SCAN_L))

            def fix(i2, acc, sl=sl, nre=nre, nim=nim):
                r0 = pl.multiple_of(i2 * SUB, SUB)
                rb = pl.multiple_of((SEG_LEN - 1 - i2) * SUB, SUB)
                gre, gim = _cmul_add(p8re[pl.ds(rb, SUB), sl], -p8im[pl.ds(rb, SUB), sl], nre, nim,
                                     gs_re[pl.ds(r0, SUB), sl], gs_im[pl.ds(r0, SUB), sl])
                gs_re[pl.ds(r0, SUB), sl] = gre
                gs_im[pl.ds(r0, SUB), sl] = gim
                rp = pl.multiple_of((i2 - 1) * SUB, SUB)
                xre, xim = xs_re[pl.ds(rp, SUB), sl], xs_im[pl.ds(rp, SUB), sl]
                return acc[0] + gre * xre + gim * xim, acc[1] + gim * xre - gre * xim

            g0re, g0im = _cmul_add(p8re[top:top + SUB, sl], -p8im[top:top + SUB, sl], nre, nim,
                                   gs_re[0:SUB, sl], gs_im[0:SUB, sl])
            gs_re[0:SUB, sl] = g0re
            gs_im[0:SUB, sl] = g0im
            ere, eim = ent_re[:, sl], ent_im[:, sl]
            acc0 = (dare[:, sl] + g0re * ere + g0im * eim, daim[:, sl] + g0im * ere - g0re * eim)
            are_acc, aim_acc = lax.fori_loop(1, SEG_LEN, fix, acc0, unroll=2)
            dare[:, sl] = are_acc
            daim[:, sl] = aim_acc
        for j in range(4):
            sl = slice(LCH * j, LCH * (j + 1))
            c4 = slice(128 * j, 128 * (j + 1))
            gre = gs_re[:, sl].astype(BF16)
            gim = gs_im[:, sl].astype(BF16)
            dmbre[j] += _dot_tn(usb[:, c4], gre)
            dmbim[j] += _dot_tn(usb[:, c4], gim)
            dysk[:, c4] = _dot_nt(gre, mbre[j]) + _dot_nt(gim, mbim[j]) + dysk[:, c4] * d_ref[:, c4]
        du_ref[...] = _dot(pmt_ref[...], dysk[...].astype(BF16)).astype(BF16)
        pl.when(i == nblk - 1)(finish)

    blk = (4, 128, LCH)
    pw = _full((SUB, STATES))
    p8 = _full((TM, STATES))
    perm = _full((TM, TM))
    car = pl.BlockSpec((1, SUB, STATES), lambda i: (nblk - 1 - i, 0, 0))
    big = lambda: pltpu.VMEM((TM, STATES), F32)
    small = lambda: pltpu.VMEM((SUB, STATES), F32)
    return pl.pallas_call(
        body, name="ssm_bwd", grid=(nblk,),
        in_specs=[_rows(SSM_W, 1, rev=nblk), _rows(SSM_W, 1, rev=nblk), car, car, perm, perm, _full(blk), _full(blk),
                  p8, p8, pw, pw, pw, pw, _full(blk), _full(blk), _full((1, SSM_W)), _full((SSM_W, 2 * SSM_W)), ANY],
        out_specs=[_rows(SSM_W, 0, rev=nblk), _full(blk), _full(blk), _full(blk), _full(blk), pw, pw,
                   _full((1, SSM_W)), _full((SSM_W, 2 * SSM_W)), ANY],
        out_shape=[jax.ShapeDtypeStruct((t, SSM_W), BF16)] + [jax.ShapeDtypeStruct(blk, F32)] * 4
        + [jax.ShapeDtypeStruct((SUB, STATES), F32)] * 2
        + [jax.ShapeDtypeStruct((1, SSM_W), F32), jax.ShapeDtypeStruct((SSM_W, 2 * SSM_W), F32),
           jax.ShapeDtypeStruct(ride.shape, ride.dtype)],
        scratch_shapes=[big(), big(), big(), big(), small(), small(), small(), small(), small(), small(),
                        pltpu.VMEM((TM, SSM_W), F32), pltpu.VMEM((TM, SSM_W), F32)] + _chip_scatter_scratch(ride),
        compiler_params=_params())(dycat, proj, car_in_re, car_in_im, pm, pmt, mb_re, mb_im, p8_re, p8_im,
                                   q_re, q_im, qr_re, qr_im, cm_re, cm_im, dskip, wglu, ride)


def _adamw(w, g, m, v, name):
    rows = w.shape[0]
    tr = 256 if rows % 256 == 0 else rows
    c1 = 1.0 / (1.0 - ADAM_B1 ** ADAM_STEP)
    c2 = 1.0 / (1.0 - ADAM_B2 ** ADAM_STEP)

    def body(w_ref, g_ref, m_ref, v_ref, d_ref, nm_ref, nv_ref):
        gv = g_ref[...]
        m = ADAM_B1 * m_ref[...] + (1.0 - ADAM_B1) * gv
        v = ADAM_B2 * v_ref[...] + (1.0 - ADAM_B2) * (gv * gv)
        nm_ref[...] = m
        nv_ref[...] = v
        d_ref[...] = -ADAM_LR * ((m * c1) / (jnp.sqrt(v * c2) + ADAM_EPS) + ADAM_WD * w_ref[...])

    spec = pl.BlockSpec((tr, D_MODEL), lambda i: (i, 0))
    shp = jax.ShapeDtypeStruct((rows, D_MODEL), F32)
    return pl.pallas_call(
        body, name=name, grid=(rows // tr,), in_specs=[spec] * 4, out_specs=[spec] * 3, out_shape=[shp] * 3,
        compiler_params=_params())(w, g, m, v)


def _core_index():
    return lax.axis_index("c").astype(jnp.int32).reshape(1)


def _pair_add(g, theirs, out_dtype, name):
    n, half, _ = theirs.shape
    br = 128
    nb = half // br

    def body(c_ref, a_ref, b_ref, o_ref):
        o_ref[...] = (a_ref[...] + b_ref[...]).astype(out_dtype)

    spec = pl.BlockSpec((1, br, D_MODEL), lambda i, j, c: (i, j, 0))
    grid_spec = pltpu.PrefetchScalarGridSpec(
        num_scalar_prefetch=1, grid=(n, nb),
        in_specs=[pl.BlockSpec((1, br, D_MODEL), lambda i, j, c: (i, c[0] * nb + j, 0)), spec], out_specs=spec)
    return pl.pallas_call(
        body, name=name, grid_spec=grid_spec, out_shape=jax.ShapeDtypeStruct(theirs.shape, out_dtype),
        compiler_params=_params(2))(_core_index(), g, theirs)


def _adamw_rows(w, m, v, g_mine, g_theirs, row0, br, name):
    rows = w.shape[0]
    b0 = row0 // br
    per_half = g_mine.shape[0] // br
    c1 = 1.0 / (1.0 - ADAM_B1 ** ADAM_STEP)
    c2 = 1.0 / (1.0 - ADAM_B2 ** ADAM_STEP)

    def body(c_ref, w_ref, gm_ref, gt_ref, m_ref, v_ref, g_ref, d_ref, nm_ref, nv_ref):
        gv = jnp.where((b0 + pl.program_id(0)) // per_half == c_ref[0], gm_ref[...], gt_ref[...])
        m = ADAM_B1 * m_ref[...] + (1.0 - ADAM_B1) * gv
        v = ADAM_B2 * v_ref[...] + (1.0 - ADAM_B2) * (gv * gv)
        g_ref[...] = gv
        nm_ref[...] = m
        nv_ref[...] = v
        d_ref[...] = -ADAM_LR * ((m * c1) / (jnp.sqrt(v * c2) + ADAM_EPS) + ADAM_WD * w_ref[...])

    spec = pl.BlockSpec((br, D_MODEL), lambda i, c: (i, 0))
    part = pl.BlockSpec((br, D_MODEL), lambda i, c: ((b0 + i) % per_half, 0))
    shp = jax.ShapeDtypeStruct((rows, D_MODEL), F32)
    grid_spec = pltpu.PrefetchScalarGridSpec(
        num_scalar_prefetch=1, grid=(rows // br,), in_specs=[spec, part, part, spec, spec], out_specs=[spec] * 4)
    return pl.pallas_call(
        body, name=name, grid_spec=grid_spec, out_shape=[shp] * 4,
        compiler_params=_params())(_core_index(), w, g_mine, g_theirs, m, v)


def _sum_lead(a, name):
    n, rows, _ = a.shape
    tr = 128 if rows % 128 == 0 else rows

    def body(a_ref, o_ref):
        acc = a_ref[0].astype(F32)
        for k in range(1, n):
            acc = acc + a_ref[k].astype(F32)
        o_ref[...] = acc

    return pl.pallas_call(
        body, name=name, grid=(rows // tr,),
        in_specs=[pl.BlockSpec((n, tr, D_MODEL), lambda i: (0, i, 0))],
        out_specs=pl.BlockSpec((tr, D_MODEL), lambda i: (i, 0)),
        out_shape=jax.ShapeDtypeStruct((rows, D_MODEL), F32), compiler_params=_params())(a)


ANY = pl.BlockSpec(memory_space=pl.ANY)


def _mesh_pos():
    return lax.axis_index("x"), lax.axis_index("y"), lax.axis_index("c")


def _gather_phases(in_refs, out_refs, bounces, send_sems, recv_sems, local_sems):
    na = len(in_refs)
    halves = [r.shape[0] // 2 for r in in_refs]
    ncopy = 3 * na
    x, y, c = _mesh_pos()
    me = 2 * x + y
    sibling = (x, y, 1 - c)
    chips = [(1 - x, y), (x, 1 - y), (1 - x, 1 - y)]
    ids = [2 * chip[0] + chip[1] for chip in chips]

    def piece(a, q, h):
        return out_refs[a].at[q, pl.ds(h * halves[a], halves[a]), :]

    def copy(s, a, q, h, to, src=None):
        return pltpu.make_async_remote_copy(
            src_ref=piece(a, q, h) if src is None else src, dst_ref=piece(a, q, h),
            send_sem=send_sems.at[s], recv_sem=recv_sems.at[s], device_id=to, device_id_type=MESH)

    def sends():
        return [copy(j * na + a, a, me, c, (*chip, c), src=in_refs[a].at[pl.ds(c * halves[a], halves[a]), :])
                for j, chip in enumerate(chips) for a in range(na)]

    def forwards():
        return [copy(ncopy + j * na + a, a, ids[j], c, sibling) for j in range(3) for a in range(na)]

    def stores():
        return [pltpu.make_async_copy(bounces[a], out_refs[a].at[me], local_sems.at[na + a]) for a in range(na)]

    def start():
        loads = [pltpu.make_async_copy(in_refs[a], bounces[a], local_sems.at[a]) for a in range(na)]
        for cp in loads:
            cp.start()
        for cp in sends():
            cp.start()
        for ld, st in zip(loads, stores()):
            ld.wait()
            st.start()

    def forward():
        fwd = forwards()
        for j in range(3):
            for a in range(na):
                copy(j * na + a, a, ids[j], c, (x, y, c)).wait_recv()
                fwd[j * na + a].start()

    def finish():
        for j in range(3):
            for a in range(na):
                copy(ncopy + j * na + a, a, ids[j], 1 - c, (x, y, c)).wait_recv()
        for cp in sends() + forwards():
            cp.wait_send()
        for cp in stores():
            cp.wait()

    return start, forward, finish


def _gather_scratch(arrs):
    ncopy = 3 * len(arrs)
    return ([pltpu.VMEM(a.shape, a.dtype) for a in arrs]
            + [pltpu.SemaphoreType.DMA((2 * ncopy,)), pltpu.SemaphoreType.DMA((2 * ncopy,)),
               pltpu.SemaphoreType.DMA((2 * len(arrs),))])


def _gather_weights(arrs):
    na = len(arrs)

    def body(*refs):
        start, forward, finish = _gather_phases(refs[:na], refs[na:2 * na], refs[2 * na:3 * na], *refs[3 * na:])
        start()
        forward()
        finish()

    return pl.pallas_call(
        body, name="gather_weights", in_specs=[ANY] * na, out_specs=[ANY] * na,
        out_shape=[jax.ShapeDtypeStruct((4,) + a.shape, a.dtype) for a in arrs],
        scratch_shapes=_gather_scratch(arrs),
        compiler_params=pltpu.CompilerParams(vmem_limit_bytes=VMEM_LIMIT),
    )(*arrs)


def _gather_all_phases(v_ref, o_ref, bounce, send_sems, recv_sems, local_sems):
    x, y, c = _mesh_pos()
    sibling = (x, y, 1 - c)
    chips = [(1 - x, y), (x, 1 - y), (1 - x, 1 - y)]

    def blk(px, py, pc):
        return o_ref.at[4 * px + 2 * py + pc]

    def copy(k, block, to, src=None):
        return pltpu.make_async_remote_copy(
            src_ref=blk(*block) if src is None else src, dst_ref=blk(*block),
            send_sem=send_sems.at[k], recv_sem=recv_sems.at[k], device_id=to, device_id_type=MESH)

    def first():
        return ([copy(0, (x, y, c), sibling, src=v_ref)]
                + [copy(1 + j, (x, y, c), (*chip, c), src=v_ref) for j, chip in enumerate(chips)])

    def passed():
        return [copy(4 + j, (*chip, c), sibling) for j, chip in enumerate(chips)]

    def store():
        return pltpu.make_async_copy(bounce, blk(x, y, c), local_sems.at[1])

    def start():
        load = pltpu.make_async_copy(v_ref, bounce, local_sems.at[0])
        load.start()
        for cp in first():
            cp.start()
        load.wait()
        store().start()

    def forward():
        fwd = passed()
        for j, chip in enumerate(chips):
            copy(1 + j, (*chip, c), (x, y, c)).wait_recv()
            fwd[j].start()

    def finish():
        copy(0, (x, y, 1 - c), (x, y, c)).wait_recv()
        for j, chip in enumerate(chips):
            copy(4 + j, (*chip, 1 - c), (x, y, c)).wait_recv()
        for cp in first() + passed():
            cp.wait_send()
        store().wait()

    return start, forward, finish


def _gather_all_scratch(v):
    return [pltpu.VMEM(v.shape, v.dtype), pltpu.SemaphoreType.DMA((7,)), pltpu.SemaphoreType.DMA((7,)),
            pltpu.SemaphoreType.DMA((2,))]


def _scatter_and_gather(p, v, name):
    def body(p_ref, v_ref, got_ref, o_ref, p_bounce, p_send, p_recv, p_local, bounce, send_sems, recv_sems,
             local_sems):
        start, finish = _chip_scatter_phases(p_ref, got_ref, p_bounce, p_send, p_recv, p_local)
        g_start, g_forward, g_finish = _gather_all_phases(v_ref, o_ref, bounce, send_sems, recv_sems, local_sems)
        start()
        g_start()
        g_forward()
        g_finish()
        finish()

    return pl.pallas_call(
        body, name=name, in_specs=[ANY, ANY], out_specs=[ANY, ANY],
        out_shape=[jax.ShapeDtypeStruct(p.shape, p.dtype), jax.ShapeDtypeStruct((8,) + v.shape, v.dtype)],
        scratch_shapes=_chip_scatter_scratch(p) + _gather_all_scratch(v),
    )(p, v)


def _pair_split_phases(g_ref, theirs_ref, send_sems, recv_sems):
    n, rows, _ = g_ref.shape
    half = rows // 2
    ch = half // COMM_CHUNKS
    x, y, c = _mesh_pos()

    def gives():
        return [pltpu.make_async_remote_copy(
            src_ref=g_ref.at[q, pl.ds((1 - c) * half + k * ch, ch), :],
            dst_ref=theirs_ref.at[q, pl.ds(k * ch, ch), :],
            send_sem=send_sems.at[q * COMM_CHUNKS + k], recv_sem=recv_sems.at[q * COMM_CHUNKS + k],
            device_id=(x, y, 1 - c), device_id_type=MESH) for q in range(n) for k in range(COMM_CHUNKS)]

    def start():
        for cp in gives():
            cp.start()

    def finish():
        for cp in gives():
            cp.wait()

    return start, finish


def _pair_split_scratch(g):
    return [pltpu.SemaphoreType.DMA((g.shape[0] * COMM_CHUNKS,)), pltpu.SemaphoreType.DMA((g.shape[0] * COMM_CHUNKS,))]


def _pair_split_shape(g):
    return jax.ShapeDtypeStruct((g.shape[0], g.shape[1] // 2, D_MODEL), g.dtype)


def _pair_split(g, name):
    def body(g_ref, theirs_ref, send_sems, recv_sems):
        start, finish = _pair_split_phases(g_ref, theirs_ref, send_sems, recv_sems)
        start()
        finish()

    return pl.pallas_call(
        body, name=name, in_specs=[ANY], out_specs=ANY, out_shape=_pair_split_shape(g),
        scratch_shapes=_pair_split_scratch(g))(g)


def _chip_scatter_phases(p_ref, o_ref, bounce, send_sems, recv_sems, local_sems):
    x, y, c = _mesh_pos()
    me = 2 * x + y
    chips = [(1 - x, y), (x, 1 - y), (1 - x, 1 - y)]

    def keep():
        return pltpu.make_async_copy(bounce, o_ref.at[me], local_sems.at[1])

    def sends():
        return [pltpu.make_async_remote_copy(
            src_ref=p_ref.at[2 * chip[0] + chip[1]], dst_ref=o_ref.at[me],
            send_sem=send_sems.at[j], recv_sem=recv_sems.at[j], device_id=(*chip, c), device_id_type=MESH)
            for j, chip in enumerate(chips)]

    def start():
        load = pltpu.make_async_copy(p_ref.at[me], bounce, local_sems.at[0])
        load.start()
        for cp in sends():
            cp.start()
        load.wait()
        keep().start()

    def finish():
        for j, chip in enumerate(chips):
            q = 2 * chip[0] + chip[1]
            pltpu.make_async_remote_copy(
                src_ref=p_ref.at[q], dst_ref=o_ref.at[q], send_sem=send_sems.at[j], recv_sem=recv_sems.at[j],
                device_id=(*chip, c), device_id_type=MESH).wait_recv()
        for cp in sends():
            cp.wait_send()
        keep().wait()

    return start, finish


def _chip_scatter_scratch(p):
    return [pltpu.VMEM(p.shape[1:], p.dtype), pltpu.SemaphoreType.DMA((3,)), pltpu.SemaphoreType.DMA((3,)),
            pltpu.SemaphoreType.DMA((2,))]


def _pair_join(r, name):
    rows = r.shape[0]
    ch = rows // COMM_CHUNKS

    def body(r_ref, o_ref, send_sems, recv_sems):
        x, y, c = _mesh_pos()
        gives = [pltpu.make_async_remote_copy(
            src_ref=r_ref.at[pl.ds(k * ch, ch), :], dst_ref=o_ref.at[pl.ds(k * ch, ch), :],
            send_sem=send_sems.at[k], recv_sem=recv_sems.at[k], device_id=(x, y, 1 - c), device_id_type=MESH)
            for k in range(COMM_CHUNKS)]
        for cp in gives:
            cp.start()
        for cp in gives:
            cp.wait()

    return pl.pallas_call(
        body, name=name, in_specs=[ANY], out_specs=ANY, out_shape=jax.ShapeDtypeStruct(r.shape, r.dtype),
        scratch_shapes=[pltpu.SemaphoreType.DMA((COMM_CHUNKS,)), pltpu.SemaphoreType.DMA((COMM_CHUNKS,))],
    )(r)


SHARD_BIG = (("even_w_in", (1024, 512)), ("ssm_w_glu", (512, 256)), ("even_w_out", (256, 1024)),
             ("odd_w_in", (1024, 768)), ("odd_w_out", (256, 1024)))
SHARD_SMALL = (("odd_norm", 1), ("conv_w", CONV_K), ("conv_b", 1), ("conv_ln_g", 1), ("conv_ln_b", 1))
REP_NAMES = (("even_norm", (1024,)), ("pool_w", (4, 128, 128)), ("pool_scale", (512,)), ("ssm_log_dt", (32,)),
             ("ssm_a_re", (32, 64)), ("ssm_a_im", (32, 64)), ("ssm_b_re", (32, 64, 16)), ("ssm_b_im", (32, 64, 16)),
             ("ssm_c_re", (32, 16, 64)), ("ssm_c_im", (32, 16, 64)), ("ssm_d", (512,)), ("final_norm", (1024,)))


def _pack_rep(d):
    flat = jnp.concatenate([d[n].reshape(-1) for n, _ in REP_NAMES])
    return jnp.pad(flat, (0, REP_ROWS * D_MODEL - flat.shape[0])).reshape(REP_ROWS, D_MODEL)


def _unpack_rep(buf):
    flat = buf.reshape(-1)
    out = {}
    off = 0
    for n, shp in REP_NAMES:
        size = 1
        for s in shp:
            size *= s
        out[n] = flat[off:off + size].reshape(shp)
        off += size
    return out


def _cols_split(full, cols):
    rows = full.shape[0]
    return full.reshape(rows, 4, cols).transpose(1, 0, 2).reshape(4, -1, D_MODEL)


def _block_diag(a):
    a = a.reshape(4, 8, GROUP_DIM, N_STATE)
    eye = jnp.eye(8, dtype=a.dtype)
    return (a[:, :, :, None, :] * eye[None, :, None, :, None]).reshape(4, 128, LCH)


def _block_diag_take(m):
    m = m.reshape(4, 8, GROUP_DIM, 8, N_STATE)
    eye = jnp.eye(8, dtype=m.dtype)
    return jnp.sum(m * eye[None, :, None, :, None], axis=3).reshape(N_GROUPS, GROUP_DIM, N_STATE)


def _ssm_discretise(log_dt, a_re, a_im, b_re, b_im):
    dt = jnp.exp(log_dt)[:, None]
    mag = jnp.exp(a_re * dt)
    ang = a_im * dt
    abar_re = mag * jnp.cos(ang)
    abar_im = mag * jnp.sin(ang)
    den = a_re * a_re + a_im * a_im
    nr = abar_re - 1.0
    ni = abar_im
    k_re = (nr * a_re + ni * a_im) / den
    k_im = (ni * a_re - nr * a_im) / den
    bb_re = k_re[..., None] * b_re - k_im[..., None] * b_im
    bb_im = k_re[..., None] * b_im + k_im[..., None] * b_re
    return abar_re, abar_im, bb_re, bb_im


def _scan_tables(log_dt, a_re, a_im):
    dt = jnp.exp(log_dt)[:, None]
    lam_re = (a_re * dt).reshape(1, STATES)
    lam_im = (a_im * dt).reshape(1, STATES)

    def powers(k):
        mag = jnp.exp(k * lam_re)
        return mag * jnp.cos(k * lam_im), mag * jnp.sin(k * lam_im)

    p_re, p_im = powers((1 + jnp.arange(TM) // SUB).astype(F32)[:, None])
    q_re, q_im = powers((SEG_LEN * (1 + jnp.arange(SUB))).astype(F32)[:, None])
    return p_re, p_im, q_re, q_im


def _local_step(x, tgt, w, shard):
    row = lambda a: a.reshape(1, -1)
    (e_w_in,) = _gather_weights([shard["even_w_in"].astype(BF16)])
    wp = w["pool_w"].astype(BF16)
    ssm_in = (w["ssm_log_dt"], w["ssm_a_re"], w["ssm_a_im"], w["ssm_b_re"], w["ssm_b_im"])
    (abar_re, abar_im, bb_re, bb_im), ssm_vjp = jax.vjp(_ssm_discretise, *ssm_in)
    mb_re = _block_diag(bb_re.transpose(0, 2, 1)).astype(BF16)
    mb_im = _block_diag(bb_im.transpose(0, 2, 1)).astype(BF16)
    cm_re = _block_diag(w["ssm_c_re"]).astype(BF16)
    cm_im = _block_diag(w["ssm_c_im"]).astype(BF16)
    p8_re, p8_im, q_re, q_im = _scan_tables(w["ssm_log_dt"], w["ssm_a_re"], w["ssm_a_im"])
    qr_re, qr_im = q_re[::-1], q_im[::-1]
    pm = _perm_matrix()
    pmt = pm.T
    g0, gf = row(w["even_norm"]), row(w["final_norm"])
    ps, dskip = row(w["pool_scale"]), row(w["ssm_d"])

    proj, (g_glu, g_eout) = _norm_in(x, g0, e_w_in, "even_in",
                                     ride=[shard["ssm_w_glu"].astype(BF16), shard["even_w_out"].astype(BF16)])
    wglu = g_glu.transpose(1, 0, 2).reshape(SSM_W, 2 * SSM_W)
    e_w_out = g_eout.reshape(D_MODEL, D_MODEL)
    yp = _pool_fwd(proj, wp, ps)
    (ys, car_re, car_im), (g_oin, g_oout, g_small) = _ssm_fwd(
        proj, pm, pmt, mb_re, mb_im, p8_re, p8_im, q_re, q_im, cm_re, cm_im, dskip, wglu, ride=_odd_shards(shard))
    o_w_in, o_w_out = g_oin, g_oout.reshape(D_MODEL, D_MODEL)
    sm = g_small.transpose(1, 0, 2).reshape(SMALL_ROWS, D_MODEL)
    cw = sm[1:1 + HALO]
    g1, cb, lg, lb = sm[0:1], sm[32:33], sm[33:34], sm[34:35]
    x1, yg = _even_out(yp, ys, proj, x, e_w_out)
    q, _ = _norm_in(x1, g1, o_w_in, "odd_in")
    y2, cv = _conv_fwd(q, cw, cb, lg, lb)
    dx2, loss_lanes, d_gf = _odd_out_loss(y2, x1, o_w_out, gf, tgt)

    dcv, dz2, d_o_w_out, d_lg, d_lb = _odd_bwd_out(dx2, o_w_out, y2, cv, q, lg, lb)
    dval, dgate, d_cw, d_cb = _conv_bwd(dcv, q, cw)
    dx1, d_g1, d_o_w_in = _in_bwd([dval, dgate, dz2], o_w_in, x1, g1, dx2, "odd_in_bwd")
    g_odd = _pack_odd_grads({
        "odd_w_in": d_o_w_in, "odd_w_out": d_o_w_out, "odd_norm": d_g1.reshape(-1),
        "conv_w": d_cw.reshape(HALO, SUB, D_MODEL).sum(axis=1)[:CONV_K], "conv_b": d_cb.reshape(-1),
        "conv_ln_g": d_lg.reshape(-1), "conv_ln_b": d_lb.reshape(-1)})
    dycat, dz, d_e_w_out, theirs_odd = _even_bwd_out(dx1, e_w_out, yg, yp, ys, proj, g_odd)
    sums_odd = _pair_add(g_odd, theirs_odd, BF16, "pair_add_odd")
    dup, d_wp, d_ps = _pool_bwd(dycat, proj, wp, ps)
    (dus, d_mb_re, d_mb_im, d_cm_re, d_cm_im, da_re, da_im, d_dskip, d_wglu, got_odd) = _ssm_bwd(
        dycat, proj, car_re, car_im, pm, pmt, mb_re, mb_im, p8_re, p8_im, q_re, q_im, qr_re, qr_im,
        cm_re, cm_im, dskip, wglu, sums_odd)
    d_abar_re = jnp.sum(da_re, axis=0).reshape(N_GROUPS, N_STATE)
    d_abar_im = jnp.sum(da_im, axis=0).reshape(N_GROUPS, N_STATE)
    d_bb_re = _block_diag_take(d_mb_re).transpose(0, 2, 1)
    d_bb_im = _block_diag_take(d_mb_im).transpose(0, 2, 1)
    d_log_dt, d_a_re, d_a_im, d_b_re, d_b_im = ssm_vjp((d_abar_re, d_abar_im, d_bb_re, d_bb_im))
    rep_early = _pack_rep({
        "even_norm": jnp.zeros((D_MODEL,), F32), "pool_w": d_wp, "pool_scale": d_ps.reshape(-1),
        "ssm_log_dt": d_log_dt, "ssm_a_re": d_a_re, "ssm_a_im": d_a_im, "ssm_b_re": d_b_re, "ssm_b_im": d_b_im,
        "ssm_c_re": _block_diag_take(d_cm_re), "ssm_c_im": _block_diag_take(d_cm_im),
        "ssm_d": d_dskip.reshape(-1), "final_norm": d_gf.reshape(-1)})
    dx, d_g0, d_e_w_in, rep_parts = _in_bwd([dup, dus, dz], e_w_in, x, g0, dx1, "even_in_bwd", ride=rep_early)

    grads = {"even_norm": d_g0, "even_w_in": d_e_w_in, "ssm_w_glu": d_wglu, "even_w_out": d_e_w_out}
    return jnp.sum(loss_lanes), dx, grads, got_odd, rep_parts


WEIGHT_NAMES = ("even_norm", "even_w_in", "pool_w", "pool_scale", "ssm_log_dt", "ssm_a_re", "ssm_a_im",
                "ssm_b_re", "ssm_b_im", "ssm_c_re", "ssm_c_im", "ssm_d", "ssm_w_glu", "even_w_out", "odd_norm",
                "odd_w_in", "conv_w", "conv_b", "conv_ln_g", "conv_ln_b", "odd_w_out", "final_norm")
SHARDED = tuple(n for n, _ in SHARD_BIG) + tuple(n for n, _ in SHARD_SMALL)


SMALL_ROWS = 64


def _odd_shards(shard):
    small = jnp.concatenate([shard[n].reshape(r, 256) for n, r in SHARD_SMALL], axis=0)
    small = jnp.pad(small, ((0, SMALL_ROWS - small.shape[0]), (0, 0)))
    return [shard["odd_w_in"].astype(BF16), shard["odd_w_out"].astype(BF16), small]


def _pack_small(d):
    small = jnp.concatenate([d[n].reshape(r, -1) for n, r in SHARD_SMALL], axis=0)
    if small.shape[1] == D_MODEL:
        small = small.reshape(35, 4, 256).transpose(1, 0, 2)
    small = small.reshape(-1, 35 * 256)
    small = jnp.pad(small, ((0, 0), (0, ROWS_SMALL * D_MODEL - 35 * 256)))
    return small.reshape(-1, ROWS_SMALL, D_MODEL)


def _unpack_small(buf):
    small = buf.reshape(-1)[:35 * 256].reshape(35, 256)
    out = {}
    off = 0
    for n, r in SHARD_SMALL:
        out[n] = small[off:off + r].reshape((r, 256) if r > 1 else (256,))
        off += r
    return out


EVEN_PACK = (("even_w_in", 0, 512), ("ssm_w_glu", 512, 128), ("even_w_out", 640, 256))
ROWS_EVEN = 1024
ODD_PACK = (("odd_w_in", 0, 768), ("odd_w_out", 768, 256))
ODD_SMALL_ROW = 1024
ROWS_ODD = 1280


def _pack_even_grads(g):
    parts = [g["even_w_in"].reshape(4, -1, D_MODEL), _cols_split(g["ssm_w_glu"], 256),
             g["even_w_out"].reshape(4, -1, D_MODEL), jnp.zeros((4, ROWS_EVEN - 896, D_MODEL), F32)]
    return jnp.concatenate(parts, axis=1)


def _pack_odd_grads(g):
    parts = [g["odd_w_in"].reshape(4, -1, D_MODEL), g["odd_w_out"].reshape(4, -1, D_MODEL), _pack_small(g),
             jnp.zeros((4, ROWS_ODD - ODD_SMALL_ROW - ROWS_SMALL, D_MODEL), F32)]
    return jnp.concatenate(parts, axis=1)


def kernel(x, even_norm, even_w_in, pool_w, pool_scale, ssm_log_dt, ssm_a_re, ssm_a_im, ssm_b_re, ssm_b_im, ssm_c_re, ssm_c_im, ssm_d, ssm_w_glu, even_w_out, odd_norm, odd_w_in, conv_w, conv_b, conv_ln_g, conv_ln_b, odd_w_out, final_norm, loss_target, m_even_norm, m_even_w_in, m_pool_w, m_pool_scale, m_ssm_log_dt, m_ssm_a_re, m_ssm_a_im, m_ssm_b_re, m_ssm_b_im, m_ssm_c_re, m_ssm_c_im, m_ssm_d, m_ssm_w_glu, m_even_w_out, m_odd_norm, m_odd_w_in, m_conv_w, m_conv_b, m_conv_ln_g, m_conv_ln_b, m_odd_w_out, m_final_norm, v_even_norm, v_even_w_in, v_pool_w, v_pool_scale, v_ssm_log_dt, v_ssm_a_re, v_ssm_a_im, v_ssm_b_re, v_ssm_b_im, v_ssm_c_re, v_ssm_c_im, v_ssm_d, v_ssm_w_glu, v_even_w_out, v_odd_norm, v_odd_w_in, v_conv_w, v_conv_b, v_conv_ln_g, v_conv_ln_b, v_odd_w_out, v_final_norm):
    ws = dict(zip(WEIGHT_NAMES, (even_norm, even_w_in, pool_w, pool_scale, ssm_log_dt, ssm_a_re, ssm_a_im, ssm_b_re,
                                 ssm_b_im, ssm_c_re, ssm_c_im, ssm_d, ssm_w_glu, even_w_out, odd_norm, odd_w_in,
                                 conv_w, conv_b, conv_ln_g, conv_ln_b, odd_w_out, final_norm)))
    ms = dict(zip(WEIGHT_NAMES, (m_even_norm, m_even_w_in, m_pool_w, m_pool_scale, m_ssm_log_dt, m_ssm_a_re,
                                 m_ssm_a_im, m_ssm_b_re, m_ssm_b_im, m_ssm_c_re, m_ssm_c_im, m_ssm_d, m_ssm_w_glu,
                                 m_even_w_out, m_odd_norm, m_odd_w_in, m_conv_w, m_conv_b, m_conv_ln_g, m_conv_ln_b,
                                 m_odd_w_out, m_final_norm)))
    vs = dict(zip(WEIGHT_NAMES, (v_even_norm, v_even_w_in, v_pool_w, v_pool_scale, v_ssm_log_dt, v_ssm_a_re,
                                 v_ssm_a_im, v_ssm_b_re, v_ssm_b_im, v_ssm_c_re, v_ssm_c_im, v_ssm_d, v_ssm_w_glu,
                                 v_even_w_out, v_odd_norm, v_odd_w_in, v_conv_w, v_conv_b, v_conv_ln_g, v_conv_ln_b,
                                 v_odd_w_out, v_final_norm)))
    lead = {n: a.shape for n, a in ws.items()}
    drop = lambda d: {n: (a[0] if n != "final_norm" else a) for n, a in d.items()}
    ws, ms, vs = drop(ws), drop(ms), drop(vs)

    shard = {n: ws[n] for n in SHARDED}
    rep = {n: ws[n] for n, _ in REP_NAMES}
    loss_part, grad_x, grads, got_odd, rep_parts = _local_step(x[0], loss_target[0], rep, shard)
    loss = lax.psum(loss_part, ("x", "y", "c"))

    odd_mine = _sum_lead(got_odd, "chip_sum_odd")
    odd_theirs = _pair_join(odd_mine, "pair_join_odd")
    g_even = _pack_even_grads(grads)
    got_even, late_parts = _scatter_and_gather(
        _pair_add(g_even, _pair_split(g_even, "pair_split_even"), BF16, "pair_add_even"),
        jnp.pad(grads["even_norm"], ((0, SUB - 1), (0, 0))), "scatter_even_gather_late")
    even_mine = _sum_lead(got_even, "chip_sum_even")
    even_theirs = _pair_join(even_mine, "pair_join_even")
    outs = [{}, {}, {}, {}]
    for pack, mine, theirs in ((EVEN_PACK, even_mine, even_theirs), (ODD_PACK, odd_mine, odd_theirs)):
        for n, row0, rows in pack:
            view = lambda a: a.reshape(rows, D_MODEL)
            res = _adamw_rows(view(ws[n]), view(ms[n]), view(vs[n]), mine, theirs, row0, 128, "adamw_" + n)
            for o, r in zip(outs, res):
                o[n] = r
    small = lambda d: _pack_small({n: d[n] for n, _ in SHARD_SMALL})[0]
    res = _adamw_rows(small(ws), small(ms), small(vs), odd_mine, odd_theirs, ODD_SMALL_ROW, ROWS_SMALL, "adamw_small")
    for o, r in zip(outs, res):
        o.update(_unpack_small(r))
    g_rep = lax.dynamic_update_slice(_sum_lead(rep_parts, "rep_sum"), _sum_lead(late_parts, "late_sum")[0:1], (0, 0))
    res = _adamw(_pack_rep(rep), g_rep, _pack_rep({n: ms[n] for n, _ in REP_NAMES}),
                 _pack_rep({n: vs[n] for n, _ in REP_NAMES}), "adamw_rep")
    for o, r in zip(outs, (g_rep,) + tuple(res)):
        o.update(_unpack_rep(r))

    leaves = [[o[n].reshape(lead[n]) for n in WEIGHT_NAMES] for o in outs]
    return (loss, grad_x[None], *leaves[0], *leaves[1], *leaves[2], *leaves[3])
```

```python
import functools

import jax
import jax.numpy as jnp
from jax import lax
from jax.experimental import pallas as pl
from jax.experimental.pallas import tpu as pltpu

F32 = jnp.float32
BF16 = jnp.bfloat16
MESH = pl.DeviceIdType.MESH

D_MODEL = 1024
RMS_EPS = 1e-6
LN_EPS = 1e-5
N_GROUPS = 32
GROUP_DIM = 16
N_STATE = 64
STATES = N_GROUPS * N_STATE
SSM_W = 512
POOL_W = 512
CONV_K = 31
HALO = 32
POOL_HALO = 16

ADAM_LR = 0.001
ADAM_B1 = 0.9
ADAM_B2 = 0.999
ADAM_EPS = 1e-08
ADAM_WD = 0.01
ADAM_STEP = 10

TM = 256
TM_MM = 512
SUB = 8
LCH = 512
SCAN_L = 1024
VMEM_LIMIT = 56 * 1024 * 1024

ROWS_SMALL = 16
REP_ROWS = 200
COMM_CHUNKS = 4


def _params(n_axes=1):
    return pltpu.CompilerParams(dimension_semantics=("arbitrary",) * n_axes, vmem_limit_bytes=VMEM_LIMIT)


def _rows(w, cb=0, rev=None, tm=TM):
    if rev is None:
        return pl.BlockSpec((tm, w), lambda i: (i, cb))
    return pl.BlockSpec((tm, w), lambda i: (rev - 1 - i, cb))


def _mm_rows(w, cb=0):
    return _rows(w, cb, tm=TM_MM)


def _full(shape):
    n = len(shape)
    return pl.BlockSpec(shape, lambda i: (0,) * n)


def _prev(hr, w, cb=0, tm=TM):
    r = tm // hr
    return pl.BlockSpec((hr, w), lambda i: (jnp.maximum(i * r - 1, 0), cb))


def _next(hr, w, nrows, cb=0, tm=TM):
    r = tm // hr
    last = nrows // hr - 1
    return pl.BlockSpec((hr, w), lambda i: (jnp.minimum((i + 1) * r, last), cb))


def _dot(a, b):
    return jnp.dot(a, b, preferred_element_type=F32)


def _dot_nt(a, b):
    return lax.dot_general(a, b, (((1,), (1,)), ((), ())), preferred_element_type=F32)


def _dot_tn(a, b):
    return lax.dot_general(a, b, (((0,), (0,)), ((), ())), preferred_element_type=F32)


def _sig(x):
    return 1.0 / (1.0 + jnp.exp(-x))


def _zero_at_first(i, *refs):
    @pl.when(i == 0)
    def _():
        for r in refs:
            r[...] = jnp.zeros_like(r)


def _norm_in(x, g, w, name, ride=(), pool=None):
    t, ns = x.shape[0], w.shape[2]
    n = 4 * ns
    ng = len(ride)
    npool = 0 if pool is None else 1
    nstep = t // TM_MM

    def body(x_ref, g_ref, w_ref, *rest):
        pool_in, rest = rest[:2 * npool], rest[2 * npool:]
        ride_in, o_ref, rest = rest[:ng], rest[ng], rest[ng + 1:]
        yp_ref, rest = (rest[0], rest[1:]) if npool else (None, rest)
        ride_out, rest = rest[:ng], rest[ng:]
        halo_ref, rest = (rest[0], rest[1:]) if npool else (None, rest)
        i = pl.program_id(0)
        if ng:
            start, forward, finish = _gather_phases(ride_in, ride_out, rest[:ng], *rest[ng:])
            pl.when(i == 0)(start)
            pl.when(i == nstep // 2)(forward)
        xv = x_ref[...]
        r = lax.rsqrt(jnp.mean(xv * xv, axis=-1, keepdims=True) + RMS_EPS)
        h = (xv * r * g_ref[...]).astype(BF16)
        first = _dot(h, w_ref[0]).astype(BF16)
        o_ref[:, 0:ns] = first
        if npool:
            wp_ref, ps_ref = pool_in
            _zero_at_first(i, halo_ref)
            pos = (i * TM_MM + 1 + lax.broadcasted_iota(jnp.int32, (TM_MM, 1), 0)).astype(F32)
            for gi in range(4):
                sl = slice(128 * gi, 128 * (gi + 1))
                u = first[:, sl].astype(F32)
                s = _pool_sums(jnp.concatenate([halo_ref[:, sl], u], axis=0), gi, True)[POOL_HALO:, :]
                pooled = s / jnp.minimum(pos, float(2 << gi)) - u
                yp_ref[:, sl] = _dot(pooled.astype(BF16), wp_ref[gi]) * ps_ref[:, sl]
                halo_ref[:, sl] = u[TM_MM - POOL_HALO:, :]
        for s in range(1, 4):
            o_ref[:, s * ns:(s + 1) * ns] = _dot(h, w_ref[s]).astype(BF16)
        if ng:
            pl.when(i == nstep - 1)(finish)

    pool_args = [] if pool is None else list(pool)
    res = pl.pallas_call(
        body, name=name, grid=(nstep,),
        in_specs=[_mm_rows(D_MODEL), _full((1, D_MODEL)), _full(w.shape)] + [_full(a.shape) for a in pool_args]
        + [ANY] * ng,
        out_specs=[_mm_rows(n)] + [_mm_rows(POOL_W)] * npool + [ANY] * ng,
        out_shape=[jax.ShapeDtypeStruct((t, n), BF16)] + [jax.ShapeDtypeStruct((t, POOL_W), F32)] * npool
        + [jax.ShapeDtypeStruct((4,) + a.shape, a.dtype) for a in ride],
        scratch_shapes=[pltpu.VMEM((POOL_HALO, POOL_W), F32)] * npool + (_gather_scratch(ride) if ng else []),
        compiler_params=_params())(x, g, w, *pool_args, *ride)
    return res[0], (res[1] if npool else None), res[1 + npool:]


def _pool_sums(ext, g, forward):
    n = ext.shape[0]
    s = ext
    for step in range(g + 1):
        k = 1 << step
        s = s + pltpu.roll(s, k if forward else n - k, 0)
    return s


SEG_LEN = TM // SUB


def _perm_matrix():
    p = jnp.arange(TM)
    src = (p % SUB) * SEG_LEN + p // SUB
    return (src[:, None] == jnp.arange(TM)[None, :]).astype(BF16)


def _cmul_add(are, aim, vre, vim, bre, bim):
    return are * vre - aim * vim + bre, are * vim + aim * vre + bim


def _segment_chain(ere, eim, qre, qim, cin_re, cin_im, row, up):
    for sh in (1, 2, 4):
        mre, mim = (qre[SUB - sh:SUB - sh + 1, :], qim[SUB - sh:SUB - sh + 1, :]) if up else \
                   (qre[sh - 1:sh, :], qim[sh - 1:sh, :])
        keep = (row < SUB - sh) if up else (row >= sh)
        sre = jnp.where(keep, pltpu.roll(ere, SUB - sh if up else sh, 0), 0.0)
        sim = jnp.where(keep, pltpu.roll(eim, SUB - sh if up else sh, 0), 0.0)
        ere, eim = _cmul_add(mre, mim, sre, sim, ere, eim)
    ere, eim = _cmul_add(qre, qim, cin_re, cin_im, ere, eim)
    keep = (row < SUB - 1) if up else (row >= 1)
    ent_re = jnp.where(keep, pltpu.roll(ere, SUB - 1 if up else 1, 0), cin_re)
    ent_im = jnp.where(keep, pltpu.roll(eim, SUB - 1 if up else 1, 0), cin_im)
    return ere, eim, ent_re, ent_im


def _scan_fwd_block(xs_re, xs_im, p8_re, p8_im, q_re, q_im, car_re, car_im, ent_re_ref, ent_im_ref):
    row = lax.broadcasted_iota(jnp.int32, (SUB, SCAN_L), 0)
    for j in range(STATES // SCAN_L):
        sl = slice(SCAN_L * j, SCAN_L * (j + 1))
        are, aim = p8_re[0:SUB, sl], p8_im[0:SUB, sl]

        def totals(i, v, sl=sl, are=are, aim=aim):
            r0 = pl.multiple_of(i * SUB, SUB)
            vre, vim = _cmul_add(are, aim, v[0], v[1], xs_re[pl.ds(r0, SUB), sl], xs_im[pl.ds(r0, SUB), sl])
            xs_re[pl.ds(r0, SUB), sl] = vre
            xs_im[pl.ds(r0, SUB), sl] = vim
            return vre, vim

        ere, eim = lax.fori_loop(1, SEG_LEN, totals, (xs_re[0:SUB, sl], xs_im[0:SUB, sl]), unroll=2)
        ere, eim, cre, cim = _segment_chain(ere, eim, q_re[:, sl], q_im[:, sl],
                                            car_re[:, sl], car_im[:, sl], row, False)
        car_re[:, sl] = jnp.broadcast_to(ere[SUB - 1:SUB, :], (SUB, SCAN_L))
        car_im[:, sl] = jnp.broadcast_to(eim[SUB - 1:SUB, :], (SUB, SCAN_L))
        if ent_re_ref is not None:
            ent_re_ref[:, sl] = cre
            ent_im_ref[:, sl] = cim

        def fix(i, c, sl=sl, cre=cre, cim=cim):
            r0 = pl.multiple_of(i * SUB, SUB)
            vre, vim = _cmul_add(p8_re[pl.ds(r0, SUB), sl], p8_im[pl.ds(r0, SUB), sl], cre, cim,
                                 xs_re[pl.ds(r0, SUB), sl], xs_im[pl.ds(r0, SUB), sl])
            xs_re[pl.ds(r0, SUB), sl] = vre
            xs_im[pl.ds(r0, SUB), sl] = vim
            return c

        lax.fori_loop(0, SEG_LEN, fix, 0, unroll=2)


def _unpermute(pmt_ref, v):
    hi = v.astype(BF16)
    lo = (v - hi.astype(F32)).astype(BF16)
    return _dot(pmt_ref[...], hi) + _dot(pmt_ref[...], lo)


def _ssm_fwd(proj, pm, pmt, mb_re, mb_im, p8_re, p8_im, q_re, q_im, cm_re, cm_im, dskip, wglu, ride=()):
    t = proj.shape[0]
    nblk = t // TM

    ng = len(ride)

    def body(u_ref, pm_ref, pmt_ref, mbre, mbim, p8re, p8im, qre, qim, cmre, cmim, d_ref, wg_ref, *rest):
        ride_in, (y_ref, cre_ref, cim_ref), ride_out = rest[:ng], rest[ng:ng + 3], rest[ng + 3:2 * ng + 3]
        xs_re, xs_im, car_re, car_im, ysk = rest[2 * ng + 3:2 * ng + 8]
        i = pl.program_id(0)
        if ng:
            start, forward, finish = _gather_phases(ride_in, ride_out, rest[2 * ng + 8:3 * ng + 8],
                                                    *rest[3 * ng + 8:])
            pl.when(i == 0)(start)
            pl.when(i == nblk // 2)(forward)
        _zero_at_first(i, car_re, car_im)
        cre_ref[0] = car_re[...]
        cim_ref[0] = car_im[...]
        us = _dot(pm_ref[...], u_ref[...])
        usb = us.astype(BF16)
        for j in range(4):
            xs_re[:, LCH * j:LCH * (j + 1)] = _dot(usb[:, 128 * j:128 * (j + 1)], mbre[j])
            xs_im[:, LCH * j:LCH * (j + 1)] = _dot(usb[:, 128 * j:128 * (j + 1)], mbim[j])
        _scan_fwd_block(xs_re, xs_im, p8re, p8im, qre, qim, car_re, car_im, None, None)
        for j in range(4):
            sl = slice(LCH * j, LCH * (j + 1))
            ysk[:, 128 * j:128 * (j + 1)] = (_dot_nt(xs_re[:, sl].astype(BF16), cmre[j])
                                             - _dot_nt(xs_im[:, sl].astype(BF16), cmim[j]))
        yv = ysk[...] + d_ref[...] * us
        gv = _dot(yv.astype(BF16), wg_ref[...])
        y_ref[...] = _unpermute(pmt_ref, gv[:, :SSM_W] * _sig(gv[:, SSM_W:]))
        if ng:
            pl.when(i == nblk - 1)(finish)

    blk = (4, 128, LCH)
    res = pl.pallas_call(
        body, name="ssm_fwd", grid=(nblk,),
        in_specs=[_rows(SSM_W, 1), _full((TM, TM)), _full((TM, TM)), _full(blk), _full(blk),
                  _full((TM, STATES)), _full((TM, STATES)), _full((SUB, STATES)), _full((SUB, STATES)),
                  _full(blk), _full(blk), _full((1, SSM_W)), _full((SSM_W, 2 * SSM_W))] + [ANY] * ng,
        out_specs=[_rows(SSM_W), pl.BlockSpec((1, SUB, STATES), lambda i: (i, 0, 0)),
                   pl.BlockSpec((1, SUB, STATES), lambda i: (i, 0, 0))] + [ANY] * ng,
        out_shape=[jax.ShapeDtypeStruct((t, SSM_W), F32), jax.ShapeDtypeStruct((nblk, SUB, STATES), F32),
                   jax.ShapeDtypeStruct((nblk, SUB, STATES), F32)]
        + [jax.ShapeDtypeStruct((4,) + a.shape, a.dtype) for a in ride],
        scratch_shapes=[pltpu.VMEM((TM, STATES), F32), pltpu.VMEM((TM, STATES), F32),
                        pltpu.VMEM((SUB, STATES), F32), pltpu.VMEM((SUB, STATES), F32),
                        pltpu.VMEM((TM, SSM_W), F32)] + (_gather_scratch(ride) if ng else []),
        compiler_params=_params())(proj, pm, pmt, mb_re, mb_im, p8_re, p8_im, q_re, q_im, cm_re, cm_im, dskip, wglu,
                                   *ride)
    return res[:3], res[3:]


def _even_out(yp, ys, proj, x, w):
    t = x.shape[0]

    def body(yp_ref, ys_ref, z_ref, x_ref, w_ref, x1_ref, yg_ref):
        z = z_ref[...].astype(F32)
        sz = z * _sig(z)
        gp = (yp_ref[...] * sz[:, :POOL_W]).astype(BF16)
        gs = (ys_ref[...] * sz[:, POOL_W:]).astype(BF16)
        yg_ref[:, :POOL_W] = gp
        yg_ref[:, POOL_W:] = gs
        x1_ref[...] = x_ref[...] + _dot(gp, w_ref[:POOL_W, :]) + _dot(gs, w_ref[POOL_W:, :])

    return pl.pallas_call(
        body, name="even_out", grid=(t // TM_MM,),
        in_specs=[_mm_rows(POOL_W), _mm_rows(SSM_W), _mm_rows(D_MODEL, 1), _mm_rows(D_MODEL),
                  _full((D_MODEL, D_MODEL))],
        out_specs=[_mm_rows(D_MODEL), _mm_rows(D_MODEL)],
        out_shape=[jax.ShapeDtypeStruct((t, D_MODEL), F32), jax.ShapeDtypeStruct((t, D_MODEL), BF16)],
        compiler_params=_params())(yp, ys, proj, x, w)


def _phase_copies(ext, cp):
    n = cp.shape[1]
    for j in range(1, SUB):
        cp[j - 1] = ext[pl.ds(j, n), :]


def _shifted(ext, cp, off, nrows, sl, row0=0):
    q, j = divmod(off, SUB)
    if j == 0:
        return ext[pl.ds(row0 + SUB * q, nrows), sl]
    return cp[j - 1, pl.ds(row0 + SUB * q, nrows), sl]


def _conv_taps(ext, cp, w_ref, first, nrows, sl, init, row0=0):
    acc = init
    for k in range(CONV_K):
        acc = acc + w_ref[k:k + 1, sl] * _shifted(ext, cp, first(k), nrows, sl, row0)
    return acc


def _conv_fwd(q, cw, cb, lg, lb):
    t = q.shape[0]

    def body(v_ref, g_ref, hv_ref, hg_ref, z_ref, w_ref, b_ref, lg_ref, lb_ref, y_ref, cv_ref, ext, cp):
        i = pl.program_id(0)
        ext[0:HALO, :] = jnp.where(i == 0, 0.0, hv_ref[...].astype(F32) * _sig(hg_ref[...].astype(F32)))
        ext[HALO:, :] = v_ref[...].astype(F32) * _sig(g_ref[...].astype(F32))
        _phase_copies(ext, cp)

        def lanes(c, carry):
            sl = pl.ds(pl.multiple_of(c * 128, 128), 128)
            cv_ref[:, sl] = _conv_taps(ext, cp, w_ref, lambda k: k + 2, TM, sl,
                                       jnp.broadcast_to(b_ref[:, sl], (TM, 128)))
            return carry

        lax.fori_loop(0, D_MODEL // 128, lanes, 0)
        cv = cv_ref[...]
        cc = cv - jnp.mean(cv, axis=-1, keepdims=True)
        rstd = lax.rsqrt(jnp.mean(cc * cc, axis=-1, keepdims=True) + LN_EPS)
        cl = cc * rstd * lg_ref[...] + lb_ref[...]
        z = z_ref[...].astype(F32)
        y_ref[...] = (cl * _sig(cl) * z * _sig(z)).astype(BF16)

    vec = _full((1, D_MODEL))
    return pl.pallas_call(
        body, name="conv_fwd", grid=(t // TM,),
        in_specs=[_rows(D_MODEL, 0), _rows(D_MODEL, 1), _prev(HALO, D_MODEL, 0), _prev(HALO, D_MODEL, 1),
                  _rows(D_MODEL, 2), _full((HALO, D_MODEL)), vec, vec, vec],
        out_specs=[_rows(D_MODEL), _rows(D_MODEL)],
        out_shape=[jax.ShapeDtypeStruct((t, D_MODEL), BF16), jax.ShapeDtypeStruct((t, D_MODEL), F32)],
        scratch_shapes=[pltpu.VMEM((TM + HALO, D_MODEL), F32),
                        pltpu.VMEM((SUB - 1, TM + HALO - SUB, D_MODEL), F32)],
        compiler_params=_params())(q, q, q, q, q, cw, cb, lg, lb)


def _odd_out_loss(y2, x1, w, gf, tgt):
    t = x1.shape[0]

    def body(y_ref, x_ref, w_ref, g_ref, t_ref, dx_ref, loss_ref, dg_ref):
        i = pl.program_id(0)
        _zero_at_first(i, loss_ref, dg_ref)
        x2 = x_ref[...] + _dot(y_ref[...], w_ref[...])
        r = lax.rsqrt(jnp.mean(x2 * x2, axis=-1, keepdims=True) + RMS_EPS)
        n = x2 * r
        e = n * g_ref[...] - t_ref[...]
        loss_ref[...] += jnp.sum(e * e, axis=0, keepdims=True) * (0.5 / D_MODEL)
        dout = e * (1.0 / D_MODEL)
        dg_ref[...] += jnp.sum(dout * n, axis=0, keepdims=True)
        dn = dout * g_ref[...]
        dx_ref[...] = r * (dn - n * jnp.mean(dn * n, axis=-1, keepdims=True))

    vec = _full((1, D_MODEL))
    return pl.pallas_call(
        body, name="odd_out_loss", grid=(t // TM_MM,),
        in_specs=[_mm_rows(D_MODEL), _mm_rows(D_MODEL), _full((D_MODEL, D_MODEL)), vec, _mm_rows(D_MODEL)],
        out_specs=[_mm_rows(D_MODEL), vec, vec],
        out_shape=[jax.ShapeDtypeStruct((t, D_MODEL), F32), jax.ShapeDtypeStruct((1, D_MODEL), F32),
                   jax.ShapeDtypeStruct((1, D_MODEL), F32)],
        compiler_params=_params())(y2, x1, w, gf, tgt)


def _dsilu(z):
    s = _sig(z)
    return z * s, s * (1.0 + z * (1.0 - s))


def _odd_bwd_out(dx2, w, y2, cv, q, lg, lb):
    t = dx2.shape[0]

    def body(dx_ref, w_ref, y_ref, cv_ref, z_ref, lg_ref, lb_ref, dcv_ref, dz_ref, dw_ref, dlg_ref, dlb_ref):
        i = pl.program_id(0)
        _zero_at_first(i, dw_ref, dlg_ref, dlb_ref)
        dxb = dx_ref[...].astype(BF16)
        dy = _dot_nt(dxb, w_ref[...])
        dw_ref[...] += _dot_tn(y_ref[...], dxb)
        cv = cv_ref[...]
        cc = cv - jnp.mean(cv, axis=-1, keepdims=True)
        rstd = lax.rsqrt(jnp.mean(cc * cc, axis=-1, keepdims=True) + LN_EPS)
        cn = cc * rstd
        silu_c, dsilu_c = _dsilu(cn * lg_ref[...] + lb_ref[...])
        silu_z, dsilu_z = _dsilu(z_ref[...].astype(F32))
        dcl = dy * silu_z * dsilu_c
        dz_ref[...] = (dy * silu_c * dsilu_z).astype(BF16)
        dlg_ref[...] += jnp.sum(dcl * cn, axis=0, keepdims=True)
        dlb_ref[...] += jnp.sum(dcl, axis=0, keepdims=True)
        dcn = dcl * lg_ref[...]
        dcv_ref[...] = rstd * (dcn - jnp.mean(dcn, axis=-1, keepdims=True)
                               - cn * jnp.mean(dcn * cn, axis=-1, keepdims=True))

    vec = _full((1, D_MODEL))
    mat = _full((D_MODEL, D_MODEL))
    return pl.pallas_call(
        body, name="odd_bwd_out", grid=(t // TM_MM,),
        in_specs=[_mm_rows(D_MODEL), mat, _mm_rows(D_MODEL), _mm_rows(D_MODEL), _mm_rows(D_MODEL, 2), vec, vec],
        out_specs=[_mm_rows(D_MODEL), _mm_rows(D_MODEL), mat, vec, vec],
        out_shape=[jax.ShapeDtypeStruct((t, D_MODEL), F32), jax.ShapeDtypeStruct((t, D_MODEL), BF16),
                   jax.ShapeDtypeStruct((D_MODEL, D_MODEL), F32), jax.ShapeDtypeStruct((1, D_MODEL), F32),
                   jax.ShapeDtypeStruct((1, D_MODEL), F32)],
        compiler_params=_params())(dx2, w, y2, cv, q, lg, lb)


def _conv_bwd(dcv, q, cw):
    t = dcv.shape[0]
    nblk = t // TM

    def body(d_ref, dn_ref, v_ref, g_ref, hv_ref, hg_ref, w_ref,
             dv_ref, dgt_ref, dw_ref, db_ref, gext, dext, dgl, gcp, dcp):
        i = pl.program_id(0)
        last = nblk - 1
        _zero_at_first(i, dw_ref, db_ref)
        v = v_ref[...].astype(F32)
        sg = _sig(g_ref[...].astype(F32))
        gext[0:HALO, :] = jnp.where(i == 0, 0.0, hv_ref[...].astype(F32) * _sig(hg_ref[...].astype(F32)))
        gext[HALO:, :] = v * sg
        d = d_ref[...]
        dext[0:TM, :] = d
        dext[TM:, :] = jnp.where(i == last, 0.0, dn_ref[...])
        _phase_copies(gext, gcp)
        _phase_copies(dext, dcp)
        db_ref[...] += jnp.sum(d, axis=0, keepdims=True)
        def lanes(c, carry):
            sl = pl.ds(pl.multiple_of(c * 128, 128), 128)
            dgl[:, sl] = _conv_taps(dext, dcp, w_ref, lambda k: 30 - k, TM, sl, jnp.zeros((TM, 128), F32))
            return carry

        def lanes_w(c, carry):
            sl = pl.ds(pl.multiple_of(c * 128, 128), 128)
            ntile = TM // SUB
            dts = [d_ref[SUB * r:SUB * (r + 1), sl] for r in range(ntile)]
            for j in range(SUB):
                taps = [(q, SUB * q + j - 2) for q in range(5) if 0 <= SUB * q + j - 2 < CONV_K]
                sums = {k: None for _, k in taps}
                for rt in range(ntile + 4):
                    need = [(q, k) for q, k in taps if 0 <= rt - q < ntile]
                    if not need:
                        continue
                    src = gext[SUB * rt:SUB * (rt + 1), sl] if j == 0 else gcp[j - 1, SUB * rt:SUB * (rt + 1), sl]
                    for q, k in need:
                        prod = dts[rt - q] * src
                        sums[k] = prod if sums[k] is None else sums[k] + prod
                for _, k in taps:
                    dw_ref[SUB * k:SUB * (k + 1), sl] += sums[k]
            return carry

        lax.fori_loop(0, D_MODEL // 128, lanes, 0)
        lax.fori_loop(0, D_MODEL // 128, lanes_w, 0)
        dg = dgl[...]
        dv_ref[...] = (dg * sg).astype(BF16)
        dgt_ref[...] = (dg * v * sg * (1.0 - sg)).astype(BF16)

    return pl.pallas_call(
        body, name="conv_bwd", grid=(t // TM,),
        in_specs=[_rows(D_MODEL), _next(HALO, D_MODEL, t), _rows(D_MODEL, 0), _rows(D_MODEL, 1),
                  _prev(HALO, D_MODEL, 0), _prev(HALO, D_MODEL, 1), _full((HALO, D_MODEL))],
        out_specs=[_rows(D_MODEL), _rows(D_MODEL), _full((HALO * SUB, D_MODEL)), _full((1, D_MODEL))],
        out_shape=[jax.ShapeDtypeStruct((t, D_MODEL), BF16), jax.ShapeDtypeStruct((t, D_MODEL), BF16),
                   jax.ShapeDtypeStruct((HALO * SUB, D_MODEL), F32), jax.ShapeDtypeStruct((1, D_MODEL), F32)],
        scratch_shapes=[pltpu.VMEM((TM + HALO, D_MODEL), F32), pltpu.VMEM((TM + HALO, D_MODEL), F32),
                        pltpu.VMEM((TM, D_MODEL), F32),
                        pltpu.VMEM((SUB - 1, TM + HALO - SUB, D_MODEL), F32),
                        pltpu.VMEM((SUB - 1, TM + HALO - SUB, D_MODEL), F32)],
        compiler_params=_params())(dcv, dcv, q, q, q, q, cw)


def _column_segments(widths, ns):
    segs = []
    col = 0
    for p, wd in enumerate(widths):
        a = 0
        while a < wd:
            s, lo = divmod(col + a, ns)
            ln = min(wd - a, ns - lo)
            segs.append((p, a, a + ln, s, lo, lo + ln))
            a += ln
        col += wd
    return segs


def _in_bwd(dparts, w, x, g, dres, name, ride=None):
    t = x.shape[0]
    widths = [p.shape[1] for p in dparts]
    npart = len(dparts)
    segs = _column_segments(widths, w.shape[2])
    nstep = t // TM_MM

    def body(*refs):
        d_refs = refs[:npart]
        w_ref, x_ref, g_ref, r_ref = refs[npart:npart + 4]
        if ride is None:
            dx_ref, dg_ref, dw_ref = refs[npart + 4:]
        else:
            v_ref, dx_ref, dg_ref, dw_ref, o_ref = refs[npart + 4:npart + 9]
            start, forward, finish = _gather_all_phases(v_ref, o_ref, *refs[npart + 9:])
        i = pl.program_id(0)
        if ride is not None:
            pl.when(i == 0)(start)
            pl.when(i == nstep // 2)(forward)
        _zero_at_first(i, dg_ref, dw_ref)
        xv = x_ref[...]
        r = lax.rsqrt(jnp.mean(xv * xv, axis=-1, keepdims=True) + RMS_EPS)
        n = xv * r
        h = (n * g_ref[...]).astype(BF16)
        dh = None
        for p, lo, hi, s, slo, shi in segs:
            d = d_refs[p][:, lo:hi]
            part = _dot_nt(d, w_ref[s, :, slo:shi])
            dh = part if dh is None else dh + part
            dw_ref[s, :, slo:shi] += _dot_tn(h, d)
        dg_ref[...] += jnp.sum(dh * n, axis=0, keepdims=True)
        dn = dh * g_ref[...]
        dx_ref[...] = r_ref[...] + r * (dn - n * jnp.mean(dn * n, axis=-1, keepdims=True))
        if ride is not None:
            pl.when(i == nstep - 1)(finish)

    vec = _full((1, D_MODEL))
    once = pl.BlockSpec(w.shape, lambda i: (0, 0, 0), pipeline_mode=pl.Buffered(1))
    extra = [] if ride is None else [ride]
    return pl.pallas_call(
        body, name=name, grid=(nstep,),
        in_specs=[_mm_rows(wd) for wd in widths] + [once, _mm_rows(D_MODEL), vec, _mm_rows(D_MODEL)]
        + [ANY] * len(extra),
        out_specs=[_mm_rows(D_MODEL), vec, once] + [ANY] * len(extra),
        out_shape=[jax.ShapeDtypeStruct((t, D_MODEL), F32), jax.ShapeDtypeStruct((1, D_MODEL), F32),
                   jax.ShapeDtypeStruct(w.shape, F32)]
        + [jax.ShapeDtypeStruct((8,) + v.shape, v.dtype) for v in extra],
        scratch_shapes=_gather_all_scratch(ride) if extra else [],
        compiler_params=_params())(*dparts, w, x, g, dres, *extra)


def _even_bwd_out(dx1, w, yg, yp, ys, proj, ride):
    t = dx1.shape[0]
    nstep = t // TM_MM

    def body(dx_ref, w_ref, yg_ref, yp_ref, ys_ref, z_ref, g_ref, dy_ref, dz_ref, dw_ref, theirs_ref,
             send_sems, recv_sems):
        i = pl.program_id(0)
        start, finish = _pair_split_phases(g_ref, theirs_ref, send_sems, recv_sems)
        pl.when(i == 0)(start)
        _zero_at_first(i, dw_ref)
        dxb = dx_ref[...].astype(BF16)
        dyg = _dot_nt(dxb, w_ref[...])
        dw_ref[...] += _dot_tn(yg_ref[...], dxb)
        silu_z, dsilu_z = _dsilu(z_ref[...].astype(F32))
        dy_ref[...] = (dyg * silu_z).astype(BF16)
        dz_ref[:, :POOL_W] = (dyg[:, :POOL_W] * yp_ref[...] * dsilu_z[:, :POOL_W]).astype(BF16)
        dz_ref[:, POOL_W:] = (dyg[:, POOL_W:] * ys_ref[...] * dsilu_z[:, POOL_W:]).astype(BF16)
        pl.when(i == nstep - 1)(finish)

    mat = _full((D_MODEL, D_MODEL))
    return pl.pallas_call(
        body, name="even_bwd_out", grid=(nstep,),
        in_specs=[_mm_rows(D_MODEL), mat, _mm_rows(D_MODEL), _mm_rows(POOL_W), _mm_rows(SSM_W), _mm_rows(D_MODEL, 1),
                  ANY],
        out_specs=[_mm_rows(D_MODEL), _mm_rows(D_MODEL), mat, ANY],
        out_shape=[jax.ShapeDtypeStruct((t, D_MODEL), BF16), jax.ShapeDtypeStruct((t, D_MODEL), BF16),
                   jax.ShapeDtypeStruct((D_MODEL, D_MODEL), F32), _pair_split_shape(ride)],
        scratch_shapes=_pair_split_scratch(ride),
        compiler_params=_params())(dx1, w, yg, yp, ys, proj, ride)


def _pool_bwd(dycat, proj, wp, ps):
    t = proj.shape[0]

    def body(dy_ref, dyn_ref, u_ref, h_ref, wp_ref, ps_ref, du_ref, dwp_ref, dps_ref):
        i = pl.program_id(0)
        last = t // TM - 1
        _zero_at_first(i, dwp_ref, dps_ref)
        pos = (i * TM + 1 + lax.broadcasted_iota(jnp.int32, (TM, 1), 0)).astype(F32)
        pos_ext = (i * TM + 1 + lax.broadcasted_iota(jnp.int32, (TM + POOL_HALO, 1), 0)).astype(F32)
        for g in range(4):
            sl = slice(128 * g, 128 * (g + 1))
            w = float(2 << g)
            u = u_ref[:, sl].astype(F32)
            halo = jnp.where(i == 0, 0.0, h_ref[:, sl].astype(F32))
            s = _pool_sums(jnp.concatenate([halo, u], axis=0), g, True)[POOL_HALO:, :]
            pooled = (s / jnp.minimum(pos, w) - u).astype(BF16)
            dy = dy_ref[:, sl].astype(F32)
            dps_ref[:, sl] += jnp.sum(dy * _dot(pooled, wp_ref[g]), axis=0, keepdims=True)
            dy_ext = jnp.concatenate([dy, jnp.where(i == last, 0.0, dyn_ref[:, sl].astype(F32))], axis=0)
            dmix = (dy_ext * ps_ref[:, sl]).astype(BF16)
            dwp_ref[g] += _dot_tn(pooled, dmix[:TM, :])
            dpool = _dot_nt(dmix, wp_ref[g])
            lead = _pool_sums(dpool / jnp.minimum(pos_ext, w), g, False)
            du_ref[:, sl] = (lead[:TM, :] - dpool[:TM, :]).astype(BF16)

    return pl.pallas_call(
        body, name="pool_bwd", grid=(t // TM,),
        in_specs=[_rows(POOL_W, 0), _next(POOL_HALO, POOL_W, t, 0), _rows(POOL_W, 0), _prev(POOL_HALO, POOL_W, 0),
                  _full((4, 128, 128)), _full((1, POOL_W))],
        out_specs=[_rows(POOL_W), _full((4, 128, 128)), _full((1, POOL_W))],
        out_shape=[jax.ShapeDtypeStruct((t, POOL_W), BF16), jax.ShapeDtypeStruct((4, 128, 128), F32),
                   jax.ShapeDtypeStruct((1, POOL_W), F32)],
        compiler_params=_params())(dycat, dycat, proj, proj, wp, ps)


def _ssm_bwd(dycat, proj, car_in_re, car_in_im, pm, pmt, mb_re, mb_im, p8_re, p8_im, q_re, q_im, qr_re, qr_im,
             cm_re, cm_im, dskip, wglu, ride):
    t = proj.shape[0]
    nblk = t // TM

    def body(dy_ref, u_ref, cin_re, cin_im, pm_ref, pmt_ref, mbre, mbim, p8re, p8im, qre, qim, qrre, qrim,
             cmre, cmim, d_ref, wg_ref, p_ref,
             du_ref, dmbre, dmbim, dcmre, dcmim, dare, daim, dd_ref, dwg_ref, got_ref,
             xs_re, xs_im, gs_re, gs_im, car_re, car_im, ent_re, ent_im, gcar_re, gcar_im, ysk, dysk,
             bounce, send_sems, recv_sems, local_sems):
        i = pl.program_id(0)
        start, finish = _chip_scatter_phases(p_ref, got_ref, bounce, send_sems, recv_sems, local_sems)
        pl.when(i == 0)(start)
        _zero_at_first(i, dmbre, dmbim, dcmre, dcmim, dare, daim, dd_ref, dwg_ref, gcar_re, gcar_im)
        us = _dot(pm_ref[...], u_ref[...])
        usb = us.astype(BF16)
        for j in range(4):
            xs_re[:, LCH * j:LCH * (j + 1)] = _dot(usb[:, 128 * j:128 * (j + 1)], mbre[j])
            xs_im[:, LCH * j:LCH * (j + 1)] = _dot(usb[:, 128 * j:128 * (j + 1)], mbim[j])
        car_re[...] = cin_re[0]
        car_im[...] = cin_im[0]
        _scan_fwd_block(xs_re, xs_im, p8re, p8im, qre, qim, car_re, car_im, ent_re, ent_im)
        for j in range(4):
            sl = slice(LCH * j, LCH * (j + 1))
            ysk[:, 128 * j:128 * (j + 1)] = (_dot_nt(xs_re[:, sl].astype(BF16), cmre[j])
                                             - _dot_nt(xs_im[:, sl].astype(BF16), cmim[j]))
        yvb = (ysk[...] + d_ref[...] * us).astype(BF16)
        gv = _dot(yvb, wg_ref[...])
        sg = _sig(gv[:, SSM_W:])
        dyss = _dot(pm_ref[...], dy_ref[...])
        dval = (dyss * sg).astype(BF16)
        dgate = (dyss * gv[:, :SSM_W] * sg * (1.0 - sg)).astype(BF16)
        dy = _dot_nt(dval, wg_ref[:, :SSM_W]) + _dot_nt(dgate, wg_ref[:, SSM_W:])
        dwg_ref[:, :SSM_W] += _dot_tn(yvb, dval)
        dwg_ref[:, SSM_W:] += _dot_tn(yvb, dgate)
        dd_ref[...] += jnp.sum(dy * us, axis=0, keepdims=True)
        dysk[...] = dy
        for j in range(4):
            sl = slice(LCH * j, LCH * (j + 1))
            dyj = dy[:, 128 * j:128 * (j + 1)].astype(BF16)
            gs_re[:, sl] = _dot(dyj, cmre[j])
            gs_im[:, sl] = -_dot(dyj, cmim[j])
            dcmre[j] += _dot_tn(dyj, xs_re[:, sl].astype(BF16))
            dcmim[j] -= _dot_tn(dyj, xs_im[:, sl].astype(BF16))
        row = lax.broadcasted_iota(jnp.int32, (SUB, SCAN_L), 0)
        for j in range(STATES // SCAN_L):
            sl = slice(SCAN_L * j, SCAN_L * (j + 1))
            are, aim = p8re[0:SUB, sl], -p8im[0:SUB, sl]

            def totals(k, v, sl=sl, are=are, aim=aim):
                r0 = pl.multiple_of((SEG_LEN - 2 - k) * SUB, SUB)
                vre, vim = _cmul_add(are, aim, v[0], v[1], gs_re[pl.ds(r0, SUB), sl], gs_im[pl.ds(r0, SUB), sl])
                gs_re[pl.ds(r0, SUB), sl] = vre
                gs_im[pl.ds(r0, SUB), sl] = vim
                return vre, vim

            top = (SEG_LEN - 1) * SUB
            fre, fim = lax.fori_loop(0, SEG_LEN - 1, totals,
                                     (gs_re[top:top + SUB, sl], gs_im[top:top + SUB, sl]), unroll=2)
            fre, fim, nre, nim = _segment_chain(fre, fim, qrre[:, sl], -qrim[:, sl],
                                                gcar_re[:, sl], gcar_im[:, sl], row, True)
            gcar_re[:, sl] = jnp.broadcast_to(fre[0:1, :], (SUB, SCAN_L))
            gcar_im[:, sl] = jnp.broadcast_to(fim[0:1, :], (SUB, SCAN_L))

            def fix(i2, acc, sl=sl, nre=nre, nim=nim):
                r0 = pl.multiple_of(i2 * SUB, SUB)
                rb = pl.multiple_of((SEG_LEN - 1 - i2) * SUB, SUB)
                gre, gim = _cmul_add(p8re[pl.ds(rb, SUB), sl], -p8im[pl.ds(rb, SUB), sl], nre, nim,
                                     gs_re[pl.ds(r0, SUB), sl], gs_im[pl.ds(r0, SUB), sl])
                gs_re[pl.ds(r0, SUB), sl] = gre
                gs_im[pl.ds(r0, SUB), sl] = gim
                rp = pl.multiple_of((i2 - 1) * SUB, SUB)
                xre, xim = xs_re[pl.ds(rp, SUB), sl], xs_im[pl.ds(rp, SUB), sl]
                return acc[0] + gre * xre + gim * xim, acc[1] + gim * xre - gre * xim

            g0re, g0im = _cmul_add(p8re[top:top + SUB, sl], -p8im[top:top + SUB, sl], nre, nim,
                                   gs_re[0:SUB, sl], gs_im[0:SUB, sl])
            gs_re[0:SUB, sl] = g0re
            gs_im[0:SUB, sl] = g0im
            ere, eim = ent_re[:, sl], ent_im[:, sl]
            acc0 = (dare[:, sl] + g0re * ere + g0im * eim, daim[:, sl] + g0im * ere - g0re * eim)
            are_acc, aim_acc = lax.fori_loop(1, SEG_LEN, fix, acc0, unroll=2)
            dare[:, sl] = are_acc
            daim[:, sl] = aim_acc
        for j in range(4):
            sl = slice(LCH * j, LCH * (j + 1))
            c4 = slice(128 * j, 128 * (j + 1))
            gre = gs_re[:, sl].astype(BF16)
            gim = gs_im[:, sl].astype(BF16)
            dmbre[j] += _dot_tn(usb[:, c4], gre)
            dmbim[j] += _dot_tn(usb[:, c4], gim)
            dysk[:, c4] = _dot_nt(gre, mbre[j]) + _dot_nt(gim, mbim[j]) + dysk[:, c4] * d_ref[:, c4]
        du_ref[...] = _dot(pmt_ref[...], dysk[...].astype(BF16)).astype(BF16)
        pl.when(i == nblk - 1)(finish)

    blk = (4, 128, LCH)
    pw = _full((SUB, STATES))
    p8 = _full((TM, STATES))
    perm = _full((TM, TM))
    car = pl.BlockSpec((1, SUB, STATES), lambda i: (nblk - 1 - i, 0, 0))
    big = lambda: pltpu.VMEM((TM, STATES), F32)
    small = lambda: pltpu.VMEM((SUB, STATES), F32)
    return pl.pallas_call(
        body, name="ssm_bwd", grid=(nblk,),
        in_specs=[_rows(SSM_W, 1, rev=nblk), _rows(SSM_W, 1, rev=nblk), car, car, perm, perm, _full(blk), _full(blk),
                  p8, p8, pw, pw, pw, pw, _full(blk), _full(blk), _full((1, SSM_W)), _full((SSM_W, 2 * SSM_W)), ANY],
        out_specs=[_rows(SSM_W, 0, rev=nblk), _full(blk), _full(blk), _full(blk), _full(blk), pw, pw,
                   _full((1, SSM_W)), _full((SSM_W, 2 * SSM_W)), ANY],
        out_shape=[jax.ShapeDtypeStruct((t, SSM_W), BF16)] + [jax.ShapeDtypeStruct(blk, F32)] * 4
        + [jax.ShapeDtypeStruct((SUB, STATES), F32)] * 2
        + [jax.ShapeDtypeStruct((1, SSM_W), F32), jax.ShapeDtypeStruct((SSM_W, 2 * SSM_W), F32),
           jax.ShapeDtypeStruct(ride.shape, ride.dtype)],
        scratch_shapes=[big(), big(), big(), big(), small(), small(), small(), small(), small(), small(),
                        pltpu.VMEM((TM, SSM_W), F32), pltpu.VMEM((TM, SSM_W), F32)] + _chip_scatter_scratch(ride),
        compiler_params=_params())(dycat, proj, car_in_re, car_in_im, pm, pmt, mb_re, mb_im, p8_re, p8_im,
                                   q_re, q_im, qr_re, qr_im, cm_re, cm_im, dskip, wglu, ride)


def _adamw(w, g, m, v, name):
    rows = w.shape[0]
    tr = 256 if rows % 256 == 0 else rows
    c1 = 1.0 / (1.0 - ADAM_B1 ** ADAM_STEP)
    c2 = 1.0 / (1.0 - ADAM_B2 ** ADAM_STEP)

    def body(w_ref, g_ref, m_ref, v_ref, d_ref, nm_ref, nv_ref):
        gv = g_ref[...]
        m = ADAM_B1 * m_ref[...] + (1.0 - ADAM_B1) * gv
        v = ADAM_B2 * v_ref[...] + (1.0 - ADAM_B2) * (gv * gv)
        nm_ref[...] = m
        nv_ref[...] = v
        d_ref[...] = -ADAM_LR * ((m * c1) / (jnp.sqrt(v * c2) + ADAM_EPS) + ADAM_WD * w_ref[...])

    spec = pl.BlockSpec((tr, D_MODEL), lambda i: (i, 0))
    shp = jax.ShapeDtypeStruct((rows, D_MODEL), F32)
    return pl.pallas_call(
        body, name=name, grid=(rows // tr,), in_specs=[spec] * 4, out_specs=[spec] * 3, out_shape=[shp] * 3,
        compiler_params=_params())(w, g, m, v)


def _core_index():
    return lax.axis_index("c").astype(jnp.int32).reshape(1)


def _pair_add(g, theirs, out_dtype, name):
    n, half, _ = theirs.shape
    br = 128
    nb = half // br

    def body(c_ref, a_ref, b_ref, o_ref):
        o_ref[...] = (a_ref[...] + b_ref[...]).astype(out_dtype)

    spec = pl.BlockSpec((1, br, D_MODEL), lambda i, j, c: (i, j, 0))
    grid_spec = pltpu.PrefetchScalarGridSpec(
        num_scalar_prefetch=1, grid=(n, nb),
        in_specs=[pl.BlockSpec((1, br, D_MODEL), lambda i, j, c: (i, c[0] * nb + j, 0)), spec], out_specs=spec)
    return pl.pallas_call(
        body, name=name, grid_spec=grid_spec, out_shape=jax.ShapeDtypeStruct(theirs.shape, out_dtype),
        compiler_params=_params(2))(_core_index(), g, theirs)


def _adamw_rows(w, m, v, g_mine, g_theirs, row0, br, name):
    rows = w.shape[0]
    b0 = row0 // br
    per_half = g_mine.shape[0] // br
    c1 = 1.0 / (1.0 - ADAM_B1 ** ADAM_STEP)
    c2 = 1.0 / (1.0 - ADAM_B2 ** ADAM_STEP)

    def body(c_ref, w_ref, gm_ref, gt_ref, m_ref, v_ref, g_ref, d_ref, nm_ref, nv_ref):
        gv = jnp.where((b0 + pl.program_id(0)) // per_half == c_ref[0], gm_ref[...], gt_ref[...])
        m = ADAM_B1 * m_ref[...] + (1.0 - ADAM_B1) * gv
        v = ADAM_B2 * v_ref[...] + (1.0 - ADAM_B2) * (gv * gv)
        g_ref[...] = gv
        nm_ref[...] = m
        nv_ref[...] = v
        d_ref[...] = -ADAM_LR * ((m * c1) / (jnp.sqrt(v * c2) + ADAM_EPS) + ADAM_WD * w_ref[...])

    spec = pl.BlockSpec((br, D_MODEL), lambda i, c: (i, 0))
    part = pl.BlockSpec((br, D_MODEL), lambda i, c: ((b0 + i) % per_half, 0))
    shp = jax.ShapeDtypeStruct((rows, D_MODEL), F32)
    grid_spec = pltpu.PrefetchScalarGridSpec(
        num_scalar_prefetch=1, grid=(rows // br,), in_specs=[spec, part, part, spec, spec], out_specs=[spec] * 4)
    return pl.pallas_call(
        body, name=name, grid_spec=grid_spec, out_shape=[shp] * 4,
        compiler_params=_params())(_core_index(), w, g_mine, g_theirs, m, v)


def _sum_lead(a, name):
    n, rows, _ = a.shape
    tr = 128 if rows % 128 == 0 else rows

    def body(a_ref, o_ref):
        acc = a_ref[0].astype(F32)
        for k in range(1, n):
            acc = acc + a_ref[k].astype(F32)
        o_ref[...] = acc

    return pl.pallas_call(
        body, name=name, grid=(rows // tr,),
        in_specs=[pl.BlockSpec((n, tr, D_MODEL), lambda i: (0, i, 0))],
        out_specs=pl.BlockSpec((tr, D_MODEL), lambda i: (i, 0)),
        out_shape=jax.ShapeDtypeStruct((rows, D_MODEL), F32), compiler_params=_params())(a)


ANY = pl.BlockSpec(memory_space=pl.ANY)


def _mesh_pos():
    return lax.axis_index("x"), lax.axis_index("y"), lax.axis_index("c")


def _gather_phases(in_refs, out_refs, bounces, send_sems, recv_sems, local_sems):
    na = len(in_refs)
    halves = [r.shape[0] // 2 for r in in_refs]
    ncopy = 3 * na
    x, y, c = _mesh_pos()
    me = 2 * x + y
    sibling = (x, y, 1 - c)
    chips = [(1 - x, y), (x, 1 - y), (1 - x, 1 - y)]
    ids = [2 * chip[0] + chip[1] for chip in chips]

    def piece(a, q, h):
        return out_refs[a].at[q, pl.ds(h * halves[a], halves[a]), :]

    def copy(s, a, q, h, to, src=None):
        return pltpu.make_async_remote_copy(
            src_ref=piece(a, q, h) if src is None else src, dst_ref=piece(a, q, h),
            send_sem=send_sems.at[s], recv_sem=recv_sems.at[s], device_id=to, device_id_type=MESH)

    def sends():
        return [copy(j * na + a, a, me, c, (*chip, c), src=in_refs[a].at[pl.ds(c * halves[a], halves[a]), :])
                for j, chip in enumerate(chips) for a in range(na)]

    def forwards():
        return [copy(ncopy + j * na + a, a, ids[j], c, sibling) for j in range(3) for a in range(na)]

    def stores():
        return [pltpu.make_async_copy(bounces[a], out_refs[a].at[me], local_sems.at[na + a]) for a in range(na)]

    def start():
        loads = [pltpu.make_async_copy(in_refs[a], bounces[a], local_sems.at[a]) for a in range(na)]
        for cp in loads:
            cp.start()
        for cp in sends():
            cp.start()
        for ld, st in zip(loads, stores()):
            ld.wait()
            st.start()

    def forward():
        fwd = forwards()
        for j in range(3):
            for a in range(na):
                copy(j * na + a, a, ids[j], c, (x, y, c)).wait_recv()
                fwd[j * na + a].start()

    def finish():
        for j in range(3):
            for a in range(na):
                copy(ncopy + j * na + a, a, ids[j], 1 - c, (x, y, c)).wait_recv()
        for cp in sends() + forwards():
            cp.wait_send()
        for cp in stores():
            cp.wait()

    return start, forward, finish


def _gather_scratch(arrs):
    ncopy = 3 * len(arrs)
    return ([pltpu.VMEM(a.shape, a.dtype) for a in arrs]
            + [pltpu.SemaphoreType.DMA((2 * ncopy,)), pltpu.SemaphoreType.DMA((2 * ncopy,)),
               pltpu.SemaphoreType.DMA((2 * len(arrs),))])


def _gather_weights(arrs):
    na = len(arrs)

    def body(*refs):
        start, forward, finish = _gather_phases(refs[:na], refs[na:2 * na], refs[2 * na:3 * na], *refs[3 * na:])
        start()
        forward()
        finish()

    return pl.pallas_call(
        body, name="gather_weights", in_specs=[ANY] * na, out_specs=[ANY] * na,
        out_shape=[jax.ShapeDtypeStruct((4,) + a.shape, a.dtype) for a in arrs],
        scratch_shapes=_gather_scratch(arrs),
        compiler_params=pltpu.CompilerParams(vmem_limit_bytes=VMEM_LIMIT),
    )(*arrs)


def _gather_all_phases(v_ref, o_ref, bounce, send_sems, recv_sems, local_sems):
    x, y, c = _mesh_pos()
    sibling = (x, y, 1 - c)
    chips = [(1 - x, y), (x, 1 - y), (1 - x, 1 - y)]

    def blk(px, py, pc):
        return o_ref.at[4 * px + 2 * py + pc]

    def copy(k, block, to, src=None):
        return pltpu.make_async_remote_copy(
            src_ref=blk(*block) if src is None else src, dst_ref=blk(*block),
            send_sem=send_sems.at[k], recv_sem=recv_sems.at[k], device_id=to, device_id_type=MESH)

    def first():
        return ([copy(0, (x, y, c), sibling, src=v_ref)]
                + [copy(1 + j, (x, y, c), (*chip, c), src=v_ref) for j, chip in enumerate(chips)])

    def passed():
        return [copy(4 + j, (*chip, c), sibling) for j, chip in enumerate(chips)]

    def store():
        return pltpu.make_async_copy(bounce, blk(x, y, c), local_sems.at[1])

    def start():
        load = pltpu.make_async_copy(v_ref, bounce, local_sems.at[0])
        load.start()
        for cp in first():
            cp.start()
        load.wait()
        store().start()

    def forward():
        fwd = passed()
        for j, chip in enumerate(chips):
            copy(1 + j, (*chip, c), (x, y, c)).wait_recv()
            fwd[j].start()

    def finish():
        copy(0, (x, y, 1 - c), (x, y, c)).wait_recv()
        for j, chip in enumerate(chips):
            copy(4 + j, (*chip, 1 - c), (x, y, c)).wait_recv()
        for cp in first() + passed():
            cp.wait_send()
        store().wait()

    return start, forward, finish


def _gather_all_scratch(v):
    return [pltpu.VMEM(v.shape, v.dtype), pltpu.SemaphoreType.DMA((7,)), pltpu.SemaphoreType.DMA((7,)),
            pltpu.SemaphoreType.DMA((2,))]


def _scatter_and_gather(p, v, name):
    def body(p_ref, v_ref, got_ref, o_ref, p_bounce, p_send, p_recv, p_local, bounce, send_sems, recv_sems,
             local_sems):
        start, finish = _chip_scatter_phases(p_ref, got_ref, p_bounce, p_send, p_recv, p_local)
        g_start, g_forward, g_finish = _gather_all_phases(v_ref, o_ref, bounce, send_sems, recv_sems, local_sems)
        start()
        g_start()
        g_forward()
        g_finish()
        finish()

    return pl.pallas_call(
        body, name=name, in_specs=[ANY, ANY], out_specs=[ANY, ANY],
        out_shape=[jax.ShapeDtypeStruct(p.shape, p.dtype), jax.ShapeDtypeStruct((8,) + v.shape, v.dtype)],
        scratch_shapes=_chip_scatter_scratch(p) + _gather_all_scratch(v),
    )(p, v)


def _pair_split_phases(g_ref, theirs_ref, send_sems, recv_sems):
    n, rows, _ = g_ref.shape
    half = rows // 2
    ch = half // COMM_CHUNKS
    x, y, c = _mesh_pos()

    def gives():
        return [pltpu.make_async_remote_copy(
            src_ref=g_ref.at[q, pl.ds((1 - c) * half + k * ch, ch), :],
            dst_ref=theirs_ref.at[q, pl.ds(k * ch, ch), :],
            send_sem=send_sems.at[q * COMM_CHUNKS + k], recv_sem=recv_sems.at[q * COMM_CHUNKS + k],
            device_id=(x, y, 1 - c), device_id_type=MESH) for q in range(n) for k in range(COMM_CHUNKS)]

    def start():
        for cp in gives():
            cp.start()

    def finish():
        for cp in gives():
            cp.wait()

    return start, finish


def _pair_split_scratch(g):
    return [pltpu.SemaphoreType.DMA((g.shape[0] * COMM_CHUNKS,)), pltpu.SemaphoreType.DMA((g.shape[0] * COMM_CHUNKS,))]


def _pair_split_shape(g):
    return jax.ShapeDtypeStruct((g.shape[0], g.shape[1] // 2, D_MODEL), g.dtype)


def _pair_split(g, name):
    def body(g_ref, theirs_ref, send_sems, recv_sems):
        start, finish = _pair_split_phases(g_ref, theirs_ref, send_sems, recv_sems)
        start()
        finish()

    return pl.pallas_call(
        body, name=name, in_specs=[ANY], out_specs=ANY, out_shape=_pair_split_shape(g),
        scratch_shapes=_pair_split_scratch(g))(g)


def _chip_scatter_phases(p_ref, o_ref, bounce, send_sems, recv_sems, local_sems):
    x, y, c = _mesh_pos()
    me = 2 * x + y
    chips = [(1 - x, y), (x, 1 - y), (1 - x, 1 - y)]

    def keep():
        return pltpu.make_async_copy(bounce, o_ref.at[me], local_sems.at[1])

    def sends():
        return [pltpu.make_async_remote_copy(
            src_ref=p_ref.at[2 * chip[0] + chip[1]], dst_ref=o_ref.at[me],
            send_sem=send_sems.at[j], recv_sem=recv_sems.at[j], device_id=(*chip, c), device_id_type=MESH)
            for j, chip in enumerate(chips)]

    def start():
        load = pltpu.make_async_copy(p_ref.at[me], bounce, local_sems.at[0])
        load.start()
        for cp in sends():
            cp.start()
        load.wait()
        keep().start()

    def finish():
        for j, chip in enumerate(chips):
            q = 2 * chip[0] + chip[1]
            pltpu.make_async_remote_copy(
                src_ref=p_ref.at[q], dst_ref=o_ref.at[q], send_sem=send_sems.at[j], recv_sem=recv_sems.at[j],
                device_id=(*chip, c), device_id_type=MESH).wait_recv()
        for cp in sends():
            cp.wait_send()
        keep().wait()

    return start, finish


def _chip_scatter_scratch(p):
    return [pltpu.VMEM(p.shape[1:], p.dtype), pltpu.SemaphoreType.DMA((3,)), pltpu.SemaphoreType.DMA((3,)),
            pltpu.SemaphoreType.DMA((2,))]


def _pair_join(r, name):
    rows = r.shape[0]
    ch = rows // COMM_CHUNKS

    def body(r_ref, o_ref, send_sems, recv_sems):
        x, y, c = _mesh_pos()
        gives = [pltpu.make_async_remote_copy(
            src_ref=r_ref.at[pl.ds(k * ch, ch), :], dst_ref=o_ref.at[pl.ds(k * ch, ch), :],
            send_sem=send_sems.at[k], recv_sem=recv_sems.at[k], device_id=(x, y, 1 - c), device_id_type=MESH)
            for k in range(COMM_CHUNKS)]
        for cp in gives:
            cp.start()
        for cp in gives:
            cp.wait()

    return pl.pallas_call(
        body, name=name, in_specs=[ANY], out_specs=ANY, out_shape=jax.ShapeDtypeStruct(r.shape, r.dtype),
        scratch_shapes=[pltpu.SemaphoreType.DMA((COMM_CHUNKS,)), pltpu.SemaphoreType.DMA((COMM_CHUNKS,))],
    )(r)


SHARD_BIG = (("even_w_in", (1024, 512)), ("ssm_w_glu", (512, 256)), ("even_w_out", (256, 1024)),
             ("odd_w_in", (1024, 768)), ("odd_w_out", (256, 1024)))
SHARD_SMALL = (("odd_norm", 1), ("conv_w", CONV_K), ("conv_b", 1), ("conv_ln_g", 1), ("conv_ln_b", 1))
REP_NAMES = (("even_norm", (1024,)), ("pool_w", (4, 128, 128)), ("pool_scale", (512,)), ("ssm_log_dt", (32,)),
             ("ssm_a_re", (32, 64)), ("ssm_a_im", (32, 64)), ("ssm_b_re", (32, 64, 16)), ("ssm_b_im", (32, 64, 16)),
             ("ssm_c_re", (32, 16, 64)), ("ssm_c_im", (32, 16, 64)), ("ssm_d", (512,)), ("final_norm", (1024,)))


def _pack_rep(d):
    flat = jnp.concatenate([d[n].reshape(-1) for n, _ in REP_NAMES])
    return jnp.pad(flat, (0, REP_ROWS * D_MODEL - flat.shape[0])).reshape(REP_ROWS, D_MODEL)


def _unpack_rep(buf):
    flat = buf.reshape(-1)
    out = {}
    off = 0
    for n, shp in REP_NAMES:
        size = 1
        for s in shp:
            size *= s
        out[n] = flat[off:off + size].reshape(shp)
        off += size
    return out


def _cols_split(full, cols):
    rows = full.shape[0]
    return full.reshape(rows, 4, cols).transpose(1, 0, 2).reshape(4, -1, D_MODEL)


def _block_diag(a):
    a = a.reshape(4, 8, GROUP_DIM, N_STATE)
    eye = jnp.eye(8, dtype=a.dtype)
    return (a[:, :, :, None, :] * eye[None, :, None, :, None]).reshape(4, 128, LCH)


def _block_diag_take(m):
    m = m.reshape(4, 8, GROUP_DIM, 8, N_STATE)
    eye = jnp.eye(8, dtype=m.dtype)
    return jnp.sum(m * eye[None, :, None, :, None], axis=3).reshape(N_GROUPS, GROUP_DIM, N_STATE)


def _ssm_discretise(log_dt, a_re, a_im, b_re, b_im):
    dt = jnp.exp(log_dt)[:, None]
    mag = jnp.exp(a_re * dt)
    ang = a_im * dt
    abar_re = mag * jnp.cos(ang)
    abar_im = mag * jnp.sin(ang)
    den = a_re * a_re + a_im * a_im
    nr = abar_re - 1.0
    ni = abar_im
    k_re = (nr * a_re + ni * a_im) / den
    k_im = (ni * a_re - nr * a_im) / den
    bb_re = k_re[..., None] * b_re - k_im[..., None] * b_im
    bb_im = k_re[..., None] * b_im + k_im[..., None] * b_re
    return abar_re, abar_im, bb_re, bb_im


def _scan_tables(log_dt, a_re, a_im):
    dt = jnp.exp(log_dt)[:, None]
    lam_re = (a_re * dt).reshape(1, STATES)
    lam_im = (a_im * dt).reshape(1, STATES)

    def powers(k):
        mag = jnp.exp(k * lam_re)
        return mag * jnp.cos(k * lam_im), mag * jnp.sin(k * lam_im)

    p_re, p_im = powers((1 + jnp.arange(TM) // SUB).astype(F32)[:, None])
    q_re, q_im = powers((SEG_LEN * (1 + jnp.arange(SUB))).astype(F32)[:, None])
    return p_re, p_im, q_re, q_im


def _local_step(x, tgt, w, shard):
    row = lambda a: a.reshape(1, -1)
    (e_w_in,) = _gather_weights([shard["even_w_in"].astype(BF16)])
    wp = w["pool_w"].astype(BF16)
    ssm_in = (w["ssm_log_dt"], w["ssm_a_re"], w["ssm_a_im"], w["ssm_b_re"], w["ssm_b_im"])
    (abar_re, abar_im, bb_re, bb_im), ssm_vjp = jax.vjp(_ssm_discretise, *ssm_in)
    mb_re = _block_diag(bb_re.transpose(0, 2, 1)).astype(BF16)
    mb_im = _block_diag(bb_im.transpose(0, 2, 1)).astype(BF16)
    cm_re = _block_diag(w["ssm_c_re"]).astype(BF16)
    cm_im = _block_diag(w["ssm_c_im"]).astype(BF16)
    p8_re, p8_im, q_re, q_im = _scan_tables(w["ssm_log_dt"], w["ssm_a_re"], w["ssm_a_im"])
    qr_re, qr_im = q_re[::-1], q_im[::-1]
    pm = _perm_matrix()
    pmt = pm.T
    g0, gf = row(w["even_norm"]), row(w["final_norm"])
    ps, dskip = row(w["pool_scale"]), row(w["ssm_d"])

    proj, yp, (g_glu, g_eout) = _norm_in(
        x, g0, e_w_in, "even_in", ride=[shard["ssm_w_glu"].astype(BF16), shard["even_w_out"].astype(BF16)],
        pool=(wp, ps))
    wglu = g_glu.transpose(1, 0, 2).reshape(SSM_W, 2 * SSM_W)
    e_w_out = g_eout.reshape(D_MODEL, D_MODEL)
    (ys, car_re, car_im), (g_oin, g_oout, g_small) = _ssm_fwd(
        proj, pm, pmt, mb_re, mb_im, p8_re, p8_im, q_re, q_im, cm_re, cm_im, dskip, wglu, ride=_odd_shards(shard))
    o_w_in, o_w_out = g_oin, g_oout.reshape(D_MODEL, D_MODEL)
    sm = g_small.transpose(1, 0, 2).reshape(SMALL_ROWS, D_MODEL)
    cw = sm[1:1 + HALO]
    g1, cb, lg, lb = sm[0:1], sm[32:33], sm[33:34], sm[34:35]
    x1, yg = _even_out(yp, ys, proj, x, e_w_out)
    q, _, _ = _norm_in(x1, g1, o_w_in, "odd_in")
    y2, cv = _conv_fwd(q, cw, cb, lg, lb)
    dx2, loss_lanes, d_gf = _odd_out_loss(y2, x1, o_w_out, gf, tgt)

    dcv, dz2, d_o_w_out, d_lg, d_lb = _odd_bwd_out(dx2, o_w_out, y2, cv, q, lg, lb)
    dval, dgate, d_cw, d_cb = _conv_bwd(dcv, q, cw)
    dx1, d_g1, d_o_w_in = _in_bwd([dval, dgate, dz2], o_w_in, x1, g1, dx2, "odd_in_bwd")
    g_odd = _pack_odd_grads({
        "odd_w_in": d_o_w_in, "odd_w_out": d_o_w_out, "odd_norm": d_g1.reshape(-1),
        "conv_w": d_cw.reshape(HALO, SUB, D_MODEL).sum(axis=1)[:CONV_K], "conv_b": d_cb.reshape(-1),
        "conv_ln_g": d_lg.reshape(-1), "conv_ln_b": d_lb.reshape(-1)})
    dycat, dz, d_e_w_out, theirs_odd = _even_bwd_out(dx1, e_w_out, yg, yp, ys, proj, g_odd)
    sums_odd = _pair_add(g_odd, theirs_odd, BF16, "pair_add_odd")
    dup, d_wp, d_ps = _pool_bwd(dycat, proj, wp, ps)
    (dus, d_mb_re, d_mb_im, d_cm_re, d_cm_im, da_re, da_im, d_dskip, d_wglu, got_odd) = _ssm_bwd(
        dycat, proj, car_re, car_im, pm, pmt, mb_re, mb_im, p8_re, p8_im, q_re, q_im, qr_re, qr_im,
        cm_re, cm_im, dskip, wglu, sums_odd)
    d_abar_re = jnp.sum(da_re, axis=0).reshape(N_GROUPS, N_STATE)
    d_abar_im = jnp.sum(da_im, axis=0).reshape(N_GROUPS, N_STATE)
    d_bb_re = _block_diag_take(d_mb_re).transpose(0, 2, 1)
    d_bb_im = _block_diag_take(d_mb_im).transpose(0, 2, 1)
    d_log_dt, d_a_re, d_a_im, d_b_re, d_b_im = ssm_vjp((d_abar_re, d_abar_im, d_bb_re, d_bb_im))
    rep_early = _pack_rep({
        "even_norm": jnp.zeros((D_MODEL,), F32), "pool_w": d_wp, "pool_scale": d_ps.reshape(-1),
        "ssm_log_dt": d_log_dt, "ssm_a_re": d_a_re, "ssm_a_im": d_a_im, "ssm_b_re": d_b_re, "ssm_b_im": d_b_im,
        "ssm_c_re": _block_diag_take(d_cm_re), "ssm_c_im": _block_diag_take(d_cm_im),
        "ssm_d": d_dskip.reshape(-1), "final_norm": d_gf.reshape(-1)})
    dx, d_g0, d_e_w_in, rep_parts = _in_bwd([dup, dus, dz], e_w_in, x, g0, dx1, "even_in_bwd", ride=rep_early)

    grads = {"even_norm": d_g0, "even_w_in": d_e_w_in, "ssm_w_glu": d_wglu, "even_w_out": d_e_w_out}
    return jnp.sum(loss_lanes), dx, grads, got_odd, rep_parts


WEIGHT_NAMES = ("even_norm", "even_w_in", "pool_w", "pool_scale", "ssm_log_dt", "ssm_a_re", "ssm_a_im",
                "ssm_b_re", "ssm_b_im", "ssm_c_re", "ssm_c_im", "ssm_d", "ssm_w_glu", "even_w_out", "odd_norm",
                "odd_w_in", "conv_w", "conv_b", "conv_ln_g", "conv_ln_b", "odd_w_out", "final_norm")
SHARDED = tuple(n for n, _ in SHARD_BIG) + tuple(n for n, _ in SHARD_SMALL)


SMALL_ROWS = 64


def _odd_shards(shard):
    small = jnp.concatenate([shard[n].reshape(r, 256) for n, r in SHARD_SMALL], axis=0)
    small = jnp.pad(small, ((0, SMALL_ROWS - small.shape[0]), (0, 0)))
    return [shard["odd_w_in"].astype(BF16), shard["odd_w_out"].astype(BF16), small]


def _pack_small(d):
    small = jnp.concatenate([d[n].reshape(r, -1) for n, r in SHARD_SMALL], axis=0)
    if small.shape[1] == D_MODEL:
        small = small.reshape(35, 4, 256).transpose(1, 0, 2)
    small = small.reshape(-1, 35 * 256)
    small = jnp.pad(small, ((0, 0), (0, ROWS_SMALL * D_MODEL - 35 * 256)))
    return small.reshape(-1, ROWS_SMALL, D_MODEL)


def _unpack_small(buf):
    small = buf.reshape(-1)[:35 * 256].reshape(35, 256)
    out = {}
    off = 0
    for n, r in SHARD_SMALL:
        out[n] = small[off:off + r].reshape((r, 256) if r > 1 else (256,))
        off += r
    return out


EVEN_PACK = (("even_w_in", 0, 512), ("ssm_w_glu", 512, 128), ("even_w_out", 640, 256))
ROWS_EVEN = 1024
ODD_PACK = (("odd_w_in", 0, 768), ("odd_w_out", 768, 256))
ODD_SMALL_ROW = 1024
ROWS_ODD = 1280


def _pack_even_grads(g):
    parts = [g["even_w_in"].reshape(4, -1, D_MODEL), _cols_split(g["ssm_w_glu"], 256),
             g["even_w_out"].reshape(4, -1, D_MODEL), jnp.zeros((4, ROWS_EVEN - 896, D_MODEL), F32)]
    return jnp.concatenate(parts, axis=1)


def _pack_odd_grads(g):
    parts = [g["odd_w_in"].reshape(4, -1, D_MODEL), g["odd_w_out"].reshape(4, -1, D_MODEL), _pack_small(g),
             jnp.zeros((4, ROWS_ODD - ODD_SMALL_ROW - ROWS_SMALL, D_MODEL), F32)]
    return jnp.concatenate(parts, axis=1)


def kernel(x, even_norm, even_w_in, pool_w, pool_scale, ssm_log_dt, ssm_a_re, ssm_a_im, ssm_b_re, ssm_b_im, ssm_c_re, ssm_c_im, ssm_d, ssm_w_glu, even_w_out, odd_norm, odd_w_in, conv_w, conv_b, conv_ln_g, conv_ln_b, odd_w_out, final_norm, loss_target, m_even_norm, m_even_w_in, m_pool_w, m_pool_scale, m_ssm_log_dt, m_ssm_a_re, m_ssm_a_im, m_ssm_b_re, m_ssm_b_im, m_ssm_c_re, m_ssm_c_im, m_ssm_d, m_ssm_w_glu, m_even_w_out, m_odd_norm, m_odd_w_in, m_conv_w, m_conv_b, m_conv_ln_g, m_conv_ln_b, m_odd_w_out, m_final_norm, v_even_norm, v_even_w_in, v_pool_w, v_pool_scale, v_ssm_log_dt, v_ssm_a_re, v_ssm_a_im, v_ssm_b_re, v_ssm_b_im, v_ssm_c_re, v_ssm_c_im, v_ssm_d, v_ssm_w_glu, v_even_w_out, v_odd_norm, v_odd_w_in, v_conv_w, v_conv_b, v_conv_ln_g, v_conv_ln_b, v_odd_w_out, v_final_norm):
    ws = dict(zip(WEIGHT_NAMES, (even_norm, even_w_in, pool_w, pool_scale, ssm_log_dt, ssm_a_re, ssm_a_im, ssm_b_re,
                                 ssm_b_im, ssm_c_re, ssm_c_im, ssm_d, ssm_w_glu, even_w_out, odd_norm, odd_w_in,
                                 conv_w, conv_b, conv_ln_g, conv_ln_b, odd_w_out, final_norm)))
    ms = dict(zip(WEIGHT_NAMES, (m_even_norm, m_even_w_in, m_pool_w, m_pool_scale, m_ssm_log_dt, m_ssm_a_re,
                                 m_ssm_a_im, m_ssm_b_re, m_ssm_b_im, m_ssm_c_re, m_ssm_c_im, m_ssm_d, m_ssm_w_glu,
                                 m_even_w_out, m_odd_norm, m_odd_w_in, m_conv_w, m_conv_b, m_conv_ln_g, m_conv_ln_b,
                                 m_odd_w_out, m_final_norm)))
    vs = dict(zip(WEIGHT_NAMES, (v_even_norm, v_even_w_in, v_pool_w, v_pool_scale, v_ssm_log_dt, v_ssm_a_re,
                                 v_ssm_a_im, v_ssm_b_re, v_ssm_b_im, v_ssm_c_re, v_ssm_c_im, v_ssm_d, v_ssm_w_glu,
                                 v_even_w_out, v_odd_norm, v_odd_w_in, v_conv_w, v_conv_b, v_conv_ln_g, v_conv_ln_b,
                                 v_odd_w_out, v_final_norm)))
    lead = {n: a.shape for n, a in ws.items()}
    drop = lambda d: {n: (a[0] if n != "final_norm" else a) for n, a in d.items()}
    ws, ms, vs = drop(ws), drop(ms), drop(vs)

    shard = {n: ws[n] for n in SHARDED}
    rep = {n: ws[n] for n, _ in REP_NAMES}
    loss_part, grad_x, grads, got_odd, rep_parts = _local_step(x[0], loss_target[0], rep, shard)
    loss = lax.psum(loss_part, ("x", "y", "c"))

    odd_mine = _sum_lead(got_odd, "chip_sum_odd")
    odd_theirs = _pair_join(odd_mine, "pair_join_odd")
    g_even = _pack_even_grads(grads)
    got_even, late_parts = _scatter_and_gather(
        _pair_add(g_even, _pair_split(g_even, "pair_split_even"), BF16, "pair_add_even"),
        jnp.pad(grads["even_norm"], ((0, SUB - 1), (0, 0))), "scatter_even_gather_late")
    even_mine = _sum_lead(got_even, "chip_sum_even")
    even_theirs = _pair_join(even_mine, "pair_join_even")
    outs = [{}, {}, {}, {}]
    for pack, mine, theirs in ((EVEN_PACK, even_mine, even_theirs), (ODD_PACK, odd_mine, odd_theirs)):
        for n, row0, rows in pack:
            view = lambda a: a.reshape(rows, D_MODEL)
            res = _adamw_rows(view(ws[n]), view(ms[n]), view(vs[n]), mine, theirs, row0, 128, "adamw_" + n)
            for o, r in zip(outs, res):
                o[n] = r
    small = lambda d: _pack_small({n: d[n] for n, _ in SHARD_SMALL})[0]
    res = _adamw_rows(small(ws), small(ms), small(vs), odd_mine, odd_theirs, ODD_SMALL_ROW, ROWS_SMALL, "adamw_small")
    for o, r in zip(outs, res):
        o.update(_unpack_small(r))
    g_rep = lax.dynamic_update_slice(_sum_lead(rep_parts, "rep_sum"), _sum_lead(late_parts, "late_sum")[0:1], (0, 0))
    res = _adamw(_pack_rep(rep), g_rep, _pack_rep({n: ms[n] for n, _ in REP_NAMES}),
                 _pack_rep({n: vs[n] for n, _ in REP_NAMES}), "adamw_rep")
    for o, r in zip(outs, (g_rep,) + tuple(res)):
        o.update(_unpack_rep(r))

    leaves = [[o[n].reshape(lead[n]) for n in WEIGHT_NAMES] for o in outs]
    return (loss, grad_x[None], *leaves[0], *leaves[1], *leaves[2], *leaves[3])
```

```python
import functools

import jax
import jax.numpy as jnp
from jax import lax
from jax.experimental import pallas as pl
from jax.experimental.pallas import tpu as pltpu

F32 = jnp.float32
BF16 = jnp.bfloat16
MESH = pl.DeviceIdType.MESH

D_MODEL = 1024
RMS_EPS = 1e-6
LN_EPS = 1e-5
N_GROUPS = 32
GROUP_DIM = 16
N_STATE = 64
STATES = N_GROUPS * N_STATE
SSM_W = 512
POOL_W = 512
CONV_K = 31
HALO = 32
POOL_HALO = 16

ADAM_LR = 0.001
ADAM_B1 = 0.9
ADAM_B2 = 0.999
ADAM_EPS = 1e-08
ADAM_WD = 0.01
ADAM_STEP = 10

TM = 256
TM_MM = 512
SUB = 8
LCH = 512
SCAN_L = 1024
VMEM_LIMIT = 56 * 1024 * 1024

ROWS_SMALL = 16
REP_ROWS = 200
COMM_CHUNKS = 4


def _params(n_axes=1):
    return pltpu.CompilerParams(dimension_semantics=("arbitrary",) * n_axes, vmem_limit_bytes=VMEM_LIMIT)


def _rows(w, cb=0, rev=None, tm=TM):
    if rev is None:
        return pl.BlockSpec((tm, w), lambda i: (i, cb))
    return pl.BlockSpec((tm, w), lambda i: (rev - 1 - i, cb))


def _mm_rows(w, cb=0):
    return _rows(w, cb, tm=TM_MM)


def _full(shape):
    n = len(shape)
    return pl.BlockSpec(shape, lambda i: (0,) * n)


def _prev(hr, w, cb=0, tm=TM):
    r = tm // hr
    return pl.BlockSpec((hr, w), lambda i: (jnp.maximum(i * r - 1, 0), cb))


def _next(hr, w, nrows, cb=0, tm=TM):
    r = tm // hr
    last = nrows // hr - 1
    return pl.BlockSpec((hr, w), lambda i: (jnp.minimum((i + 1) * r, last), cb))


def _dot(a, b):
    return jnp.dot(a, b, preferred_element_type=F32)


def _dot_nt(a, b):
    return lax.dot_general(a, b, (((1,), (1,)), ((), ())), preferred_element_type=F32)


def _dot_tn(a, b):
    return lax.dot_general(a, b, (((0,), (0,)), ((), ())), preferred_element_type=F32)


def _sig(x):
    return 1.0 / (1.0 + jnp.exp(-x))


def _zero_at_first(i, *refs):
    @pl.when(i == 0)
    def _():
        for r in refs:
            r[...] = jnp.zeros_like(r)


def _norm_in(x, g, w, name, ride=(), pool=None):
    t, ns = x.shape[0], w.shape[2]
    n = 4 * ns
    ng = len(ride)
    npool = 0 if pool is None else 1
    nstep = t // TM_MM

    def body(x_ref, g_ref, w_ref, *rest):
        pool_in, rest = rest[:2 * npool], rest[2 * npool:]
        ride_in, o_ref, rest = rest[:ng], rest[ng], rest[ng + 1:]
        yp_ref, rest = (rest[0], rest[1:]) if npool else (None, rest)
        ride_out, rest = rest[:ng], rest[ng:]
        halo_ref, rest = (rest[0], rest[1:]) if npool else (None, rest)
        i = pl.program_id(0)
        if ng:
            start, forward, finish = _gather_phases(ride_in, ride_out, rest[:ng], *rest[ng:])
            pl.when(i == 0)(start)
            pl.when(i == nstep // 2)(forward)
        xv = x_ref[...]
        r = lax.rsqrt(jnp.mean(xv * xv, axis=-1, keepdims=True) + RMS_EPS)
        h = (xv * r * g_ref[...]).astype(BF16)
        first = _dot(h, w_ref[0]).astype(BF16)
        o_ref[:, 0:ns] = first
        if npool:
            wp_ref, ps_ref = pool_in
            _zero_at_first(i, halo_ref)
            pos = (i * TM_MM + 1 + lax.broadcasted_iota(jnp.int32, (TM_MM, 1), 0)).astype(F32)
            for gi in range(4):
                sl = slice(128 * gi, 128 * (gi + 1))
                u = first[:, sl].astype(F32)
                s = _pool_sums(jnp.concatenate([halo_ref[:, sl], u], axis=0), gi, True)[POOL_HALO:, :]
                pooled = s / jnp.minimum(pos, float(2 << gi)) - u
                yp_ref[:, sl] = _dot(pooled.astype(BF16), wp_ref[gi]) * ps_ref[:, sl]
                halo_ref[:, sl] = u[TM_MM - POOL_HALO:, :]
        for s in range(1, 4):
            o_ref[:, s * ns:(s + 1) * ns] = _dot(h, w_ref[s]).astype(BF16)
        if ng:
            pl.when(i == nstep - 1)(finish)

    pool_args = [] if pool is None else list(pool)
    res = pl.pallas_call(
        body, name=name, grid=(nstep,),
        in_specs=[_mm_rows(D_MODEL), _full((1, D_MODEL)), _full(w.shape)] + [_full(a.shape) for a in pool_args]
        + [ANY] * ng,
        out_specs=[_mm_rows(n)] + [_mm_rows(POOL_W)] * npool + [ANY] * ng,
        out_shape=[jax.ShapeDtypeStruct((t, n), BF16)] + [jax.ShapeDtypeStruct((t, POOL_W), F32)] * npool
        + [jax.ShapeDtypeStruct((4,) + a.shape, a.dtype) for a in ride],
        scratch_shapes=[pltpu.VMEM((POOL_HALO, POOL_W), F32)] * npool + (_gather_scratch(ride) if ng else []),
        compiler_params=_params())(x, g, w, *pool_args, *ride)
    return res[0], (res[1] if npool else None), res[1 + npool:]


def _pool_sums(ext, g, forward):
    n = ext.shape[0]
    s = ext
    for step in range(g + 1):
        k = 1 << step
        s = s + pltpu.roll(s, k if forward else n - k, 0)
    return s


SEG_LEN = TM // SUB


def _perm_matrix():
    p = jnp.arange(TM)
    src = (p % SUB) * SEG_LEN + p // SUB
    return (src[:, None] == jnp.arange(TM)[None, :]).astype(BF16)


def _cmul_add(are, aim, vre, vim, bre, bim):
    return are * vre - aim * vim + bre, are * vim + aim * vre + bim


def _segment_chain(ere, eim, qre, qim, cin_re, cin_im, row, up):
    for sh in (1, 2, 4):
        mre, mim = (qre[SUB - sh:SUB - sh + 1, :], qim[SUB - sh:SUB - sh + 1, :]) if up else \
                   (qre[sh - 1:sh, :], qim[sh - 1:sh, :])
        keep = (row < SUB - sh) if up else (row >= sh)
        sre = jnp.where(keep, pltpu.roll(ere, SUB - sh if up else sh, 0), 0.0)
        sim = jnp.where(keep, pltpu.roll(eim, SUB - sh if up else sh, 0), 0.0)
        ere, eim = _cmul_add(mre, mim, sre, sim, ere, eim)
    ere, eim = _cmul_add(qre, qim, cin_re, cin_im, ere, eim)
    keep = (row < SUB - 1) if up else (row >= 1)
    ent_re = jnp.where(keep, pltpu.roll(ere, SUB - 1 if up else 1, 0), cin_re)
    ent_im = jnp.where(keep, pltpu.roll(eim, SUB - 1 if up else 1, 0), cin_im)
    return ere, eim, ent_re, ent_im


def _scan_fwd_block(xs_re, xs_im, p8_re, p8_im, q_re, q_im, car_re, car_im, ent_re_ref, ent_im_ref):
    row = lax.broadcasted_iota(jnp.int32, (SUB, SCAN_L), 0)
    for j in range(STATES // SCAN_L):
        sl = slice(SCAN_L * j, SCAN_L * (j + 1))
        are, aim = p8_re[0:SUB, sl], p8_im[0:SUB, sl]

        def totals(i, v, sl=sl, are=are, aim=aim):
            r0 = pl.multiple_of(i * SUB, SUB)
            vre, vim = _cmul_add(are, aim, v[0], v[1], xs_re[pl.ds(r0, SUB), sl], xs_im[pl.ds(r0, SUB), sl])
            xs_re[pl.ds(r0, SUB), sl] = vre
            xs_im[pl.ds(r0, SUB), sl] = vim
            return vre, vim

        ere, eim = lax.fori_loop(1, SEG_LEN, totals, (xs_re[0:SUB, sl], xs_im[0:SUB, sl]), unroll=2)
        ere, eim, cre, cim = _segment_chain(ere, eim, q_re[:, sl], q_im[:, sl],
                                            car_re[:, sl], car_im[:, sl], row, False)
        car_re[:, sl] = jnp.broadcast_to(ere[SUB - 1:SUB, :], (SUB, SCAN_L))
        car_im[:, sl] = jnp.broadcast_to(eim[SUB - 1:SUB, :], (SUB, SCAN_L))
        if ent_re_ref is not None:
            ent_re_ref[:, sl] = cre
            ent_im_ref[:, sl] = cim

        def fix(i, c, sl=sl, cre=cre, cim=cim):
            r0 = pl.multiple_of(i * SUB, SUB)
            vre, vim = _cmul_add(p8_re[pl.ds(r0, SUB), sl], p8_im[pl.ds(r0, SUB), sl], cre, cim,
                                 xs_re[pl.ds(r0, SUB), sl], xs_im[pl.ds(r0, SUB), sl])
            xs_re[pl.ds(r0, SUB), sl] = vre
            xs_im[pl.ds(r0, SUB), sl] = vim
            return c

        lax.fori_loop(0, SEG_LEN, fix, 0, unroll=2)


def _unpermute(pmt_ref, v):
    hi = v.astype(BF16)
    lo = (v - hi.astype(F32)).astype(BF16)
    return _dot(pmt_ref[...], hi) + _dot(pmt_ref[...], lo)


def _ssm_fwd(proj, pm, pmt, mb_re, mb_im, p8_re, p8_im, q_re, q_im, cm_re, cm_im, dskip, wglu, ride=()):
    t = proj.shape[0]
    nblk = t // TM

    ng = len(ride)

    def body(u_ref, pm_ref, pmt_ref, mbre, mbim, p8re, p8im, qre, qim, cmre, cmim, d_ref, wg_ref, *rest):
        ride_in, (y_ref, cre_ref, cim_ref), ride_out = rest[:ng], rest[ng:ng + 3], rest[ng + 3:2 * ng + 3]
        xs_re, xs_im, car_re, car_im, ysk = rest[2 * ng + 3:2 * ng + 8]
        i = pl.program_id(0)
        if ng:
            start, forward, finish = _gather_phases(ride_in, ride_out, rest[2 * ng + 8:3 * ng + 8],
                                                    *rest[3 * ng + 8:])
            pl.when(i == 0)(start)
            pl.when(i == nblk // 2)(forward)
        _zero_at_first(i, car_re, car_im)
        cre_ref[0] = car_re[...]
        cim_ref[0] = car_im[...]
        us = _dot(pm_ref[...], u_ref[...])
        usb = us.astype(BF16)
        for j in range(4):
            xs_re[:, LCH * j:LCH * (j + 1)] = _dot(usb[:, 128 * j:128 * (j + 1)], mbre[j])
            xs_im[:, LCH * j:LCH * (j + 1)] = _dot(usb[:, 128 * j:128 * (j + 1)], mbim[j])
        _scan_fwd_block(xs_re, xs_im, p8re, p8im, qre, qim, car_re, car_im, None, None)
        for j in range(4):
            sl = slice(LCH * j, LCH * (j + 1))
            ysk[:, 128 * j:128 * (j + 1)] = (_dot_nt(xs_re[:, sl].astype(BF16), cmre[j])
                                             - _dot_nt(xs_im[:, sl].astype(BF16), cmim[j]))
        yv = ysk[...] + d_ref[...] * us
        gv = _dot(yv.astype(BF16), wg_ref[...])
        y_ref[...] = _unpermute(pmt_ref, gv[:, :SSM_W] * _sig(gv[:, SSM_W:]))
        if ng:
            pl.when(i == nblk - 1)(finish)

    blk = (4, 128, LCH)
    res = pl.pallas_call(
        body, name="ssm_fwd", grid=(nblk,),
        in_specs=[_rows(SSM_W, 1), _full((TM, TM)), _full((TM, TM)), _full(blk), _full(blk),
                  _full((TM, STATES)), _full((TM, STATES)), _full((SUB, STATES)), _full((SUB, STATES)),
                  _full(blk), _full(blk), _full((1, SSM_W)), _full((SSM_W, 2 * SSM_W))] + [ANY] * ng,
        out_specs=[_rows(SSM_W), pl.BlockSpec((1, SUB, STATES), lambda i: (i, 0, 0)),
                   pl.BlockSpec((1, SUB, STATES), lambda i: (i, 0, 0))] + [ANY] * ng,
        out_shape=[jax.ShapeDtypeStruct((t, SSM_W), F32), jax.ShapeDtypeStruct((nblk, SUB, STATES), F32),
                   jax.ShapeDtypeStruct((nblk, SUB, STATES), F32)]
        + [jax.ShapeDtypeStruct((4,) + a.shape, a.dtype) for a in ride],
        scratch_shapes=[pltpu.VMEM((TM, STATES), F32), pltpu.VMEM((TM, STATES), F32),
                        pltpu.VMEM((SUB, STATES), F32), pltpu.VMEM((SUB, STATES), F32),
                        pltpu.VMEM((TM, SSM_W), F32)] + (_gather_scratch(ride) if ng else []),
        compiler_params=_params())(proj, pm, pmt, mb_re, mb_im, p8_re, p8_im, q_re, q_im, cm_re, cm_im, dskip, wglu,
                                   *ride)
    return res[:3], res[3:]


def _even_out(yp, ys, proj, x, w):
    t = x.shape[0]

    def body(yp_ref, ys_ref, z_ref, x_ref, w_ref, x1_ref, yg_ref):
        z = z_ref[...].astype(F32)
        sz = z * _sig(z)
        gp = (yp_ref[...] * sz[:, :POOL_W]).astype(BF16)
        gs = (ys_ref[...] * sz[:, POOL_W:]).astype(BF16)
        yg_ref[:, :POOL_W] = gp
        yg_ref[:, POOL_W:] = gs
        x1_ref[...] = x_ref[...] + _dot(gp, w_ref[:POOL_W, :]) + _dot(gs, w_ref[POOL_W:, :])

    return pl.pallas_call(
        body, name="even_out", grid=(t // TM_MM,),
        in_specs=[_mm_rows(POOL_W), _mm_rows(SSM_W), _mm_rows(D_MODEL, 1), _mm_rows(D_MODEL),
                  _full((D_MODEL, D_MODEL))],
        out_specs=[_mm_rows(D_MODEL), _mm_rows(D_MODEL)],
        out_shape=[jax.ShapeDtypeStruct((t, D_MODEL), F32), jax.ShapeDtypeStruct((t, D_MODEL), BF16)],
        compiler_params=_params())(yp, ys, proj, x, w)


def _phase_copies(ext, cp):
    n = cp.shape[1]
    for j in range(1, SUB):
        cp[j - 1] = ext[pl.ds(j, n), :]


def _shifted(ext, cp, off, nrows, sl, row0=0):
    q, j = divmod(off, SUB)
    if j == 0:
        return ext[pl.ds(row0 + SUB * q, nrows), sl]
    return cp[j - 1, pl.ds(row0 + SUB * q, nrows), sl]


def _conv_taps(ext, cp, w_ref, first, nrows, sl, init, row0=0):
    acc = init
    for k in range(CONV_K):
        acc = acc + w_ref[k:k + 1, sl] * _shifted(ext, cp, first(k), nrows, sl, row0)
    return acc


def _conv_fwd(q, cw, cb, lg, lb):
    t = q.shape[0]

    def body(v_ref, g_ref, hv_ref, hg_ref, z_ref, w_ref, b_ref, lg_ref, lb_ref, y_ref, cv_ref, ext, cp):
        i = pl.program_id(0)
        ext[0:HALO, :] = jnp.where(i == 0, 0.0, hv_ref[...].astype(F32) * _sig(hg_ref[...].astype(F32)))
        ext[HALO:, :] = v_ref[...].astype(F32) * _sig(g_ref[...].astype(F32))
        _phase_copies(ext, cp)

        def lanes(c, carry):
            sl = pl.ds(pl.multiple_of(c * 128, 128), 128)
            cv_ref[:, sl] = _conv_taps(ext, cp, w_ref, lambda k: k + 2, TM, sl,
                                       jnp.broadcast_to(b_ref[:, sl], (TM, 128)))
            return carry

        lax.fori_loop(0, D_MODEL // 128, lanes, 0)
        cv = cv_ref[...]
        cc = cv - jnp.mean(cv, axis=-1, keepdims=True)
        rstd = lax.rsqrt(jnp.mean(cc * cc, axis=-1, keepdims=True) + LN_EPS)
        cl = cc * rstd * lg_ref[...] + lb_ref[...]
        z = z_ref[...].astype(F32)
        y_ref[...] = (cl * _sig(cl) * z * _sig(z)).astype(BF16)

    vec = _full((1, D_MODEL))
    return pl.pallas_call(
        body, name="conv_fwd", grid=(t // TM,),
        in_specs=[_rows(D_MODEL, 0), _rows(D_MODEL, 1), _prev(HALO, D_MODEL, 0), _prev(HALO, D_MODEL, 1),
                  _rows(D_MODEL, 2), _full((HALO, D_MODEL)), vec, vec, vec],
        out_specs=[_rows(D_MODEL), _rows(D_MODEL)],
        out_shape=[jax.ShapeDtypeStruct((t, D_MODEL), BF16), jax.ShapeDtypeStruct((t, D_MODEL), F32)],
        scratch_shapes=[pltpu.VMEM((TM + HALO, D_MODEL), F32),
                        pltpu.VMEM((SUB - 1, TM + HALO - SUB, D_MODEL), F32)],
        compiler_params=_params())(q, q, q, q, q, cw, cb, lg, lb)


def _odd_out_loss(y2, x1, w, gf, tgt):
    t = x1.shape[0]

    def body(y_ref, x_ref, w_ref, g_ref, t_ref, dx_ref, loss_ref, dg_ref):
        i = pl.program_id(0)
        _zero_at_first(i, loss_ref, dg_ref)
        x2 = x_ref[...] + _dot(y_ref[...], w_ref[...])
        r = lax.rsqrt(jnp.mean(x2 * x2, axis=-1, keepdims=True) + RMS_EPS)
        n = x2 * r
        e = n * g_ref[...] - t_ref[...]
        loss_ref[...] += jnp.sum(e * e, axis=0, keepdims=True) * (0.5 / D_MODEL)
        dout = e * (1.0 / D_MODEL)
        dg_ref[...] += jnp.sum(dout * n, axis=0, keepdims=True)
        dn = dout * g_ref[...]
        dx_ref[...] = r * (dn - n * jnp.mean(dn * n, axis=-1, keepdims=True))

    vec = _full((1, D_MODEL))
    return pl.pallas_call(
        body, name="odd_out_loss", grid=(t // TM_MM,),
        in_specs=[_mm_rows(D_MODEL), _mm_rows(D_MODEL), _full((D_MODEL, D_MODEL)), vec, _mm_rows(D_MODEL)],
        out_specs=[_mm_rows(D_MODEL), vec, vec],
        out_shape=[jax.ShapeDtypeStruct((t, D_MODEL), F32), jax.ShapeDtypeStruct((1, D_MODEL), F32),
                   jax.ShapeDtypeStruct((1, D_MODEL), F32)],
        compiler_params=_params())(y2, x1, w, gf, tgt)


def _dsilu(z):
    s = _sig(z)
    return z * s, s * (1.0 + z * (1.0 - s))


def _odd_bwd_out(dx2, w, y2, cv, q, lg, lb):
    t = dx2.shape[0]

    def body(dx_ref, w_ref, y_ref, cv_ref, z_ref, lg_ref, lb_ref, dcv_ref, dz_ref, dw_ref, dlg_ref, dlb_ref):
        i = pl.program_id(0)
        _zero_at_first(i, dw_ref, dlg_ref, dlb_ref)
        dxb = dx_ref[...].astype(BF16)
        dy = _dot_nt(dxb, w_ref[...])
        dw_ref[...] += _dot_tn(y_ref[...], dxb)
        cv = cv_ref[...]
        cc = cv - jnp.mean(cv, axis=-1, keepdims=True)
        rstd = lax.rsqrt(jnp.mean(cc * cc, axis=-1, keepdims=True) + LN_EPS)
        cn = cc * rstd
        silu_c, dsilu_c = _dsilu(cn * lg_ref[...] + lb_ref[...])
        silu_z, dsilu_z = _dsilu(z_ref[...].astype(F32))
        dcl = dy * silu_z * dsilu_c
        dz_ref[...] = (dy * silu_c * dsilu_z).astype(BF16)
        dlg_ref[...] += jnp.sum(dcl * cn, axis=0, keepdims=True)
        dlb_ref[...] += jnp.sum(dcl, axis=0, keepdims=True)
        dcn = dcl * lg_ref[...]
        dcv_ref[...] = rstd * (dcn - jnp.mean(dcn, axis=-1, keepdims=True)
                               - cn * jnp.mean(dcn * cn, axis=-1, keepdims=True))

    vec = _full((1, D_MODEL))
    mat = _full((D_MODEL, D_MODEL))
    return pl.pallas_call(
        body, name="odd_bwd_out", grid=(t // TM_MM,),
        in_specs=[_mm_rows(D_MODEL), mat, _mm_rows(D_MODEL), _mm_rows(D_MODEL), _mm_rows(D_MODEL, 2), vec, vec],
        out_specs=[_mm_rows(D_MODEL), _mm_rows(D_MODEL), mat, vec, vec],
        out_shape=[jax.ShapeDtypeStruct((t, D_MODEL), F32), jax.ShapeDtypeStruct((t, D_MODEL), BF16),
                   jax.ShapeDtypeStruct((D_MODEL, D_MODEL), F32), jax.ShapeDtypeStruct((1, D_MODEL), F32),
                   jax.ShapeDtypeStruct((1, D_MODEL), F32)],
        compiler_params=_params())(dx2, w, y2, cv, q, lg, lb)


def _conv_bwd(dcv, q, cw):
    t = dcv.shape[0]
    nblk = t // TM

    def body(d_ref, dn_ref, v_ref, g_ref, hv_ref, hg_ref, w_ref,
             dv_ref, dgt_ref, dw_ref, db_ref, gext, dext, dgl, gcp, dcp):
        i = pl.program_id(0)
        last = nblk - 1
        _zero_at_first(i, dw_ref, db_ref)
        v = v_ref[...].astype(F32)
        sg = _sig(g_ref[...].astype(F32))
        gext[0:HALO, :] = jnp.where(i == 0, 0.0, hv_ref[...].astype(F32) * _sig(hg_ref[...].astype(F32)))
        gext[HALO:, :] = v * sg
        d = d_ref[...]
        dext[0:TM, :] = d
        dext[TM:, :] = jnp.where(i == last, 0.0, dn_ref[...])
        _phase_copies(gext, gcp)
        _phase_copies(dext, dcp)
        db_ref[...] += jnp.sum(d, axis=0, keepdims=True)
        def lanes(c, carry):
            sl = pl.ds(pl.multiple_of(c * 128, 128), 128)
            dgl[:, sl] = _conv_taps(dext, dcp, w_ref, lambda k: 30 - k, TM, sl, jnp.zeros((TM, 128), F32))
            return carry

        def lanes_w(c, carry):
            sl = pl.ds(pl.multiple_of(c * 128, 128), 128)
            ntile = TM // SUB
            dts = [d_ref[SUB * r:SUB * (r + 1), sl] for r in range(ntile)]
            for j in range(SUB):
                taps = [(q, SUB * q + j - 2) for q in range(5) if 0 <= SUB * q + j - 2 < CONV_K]
                sums = {k: None for _, k in taps}
                for rt in range(ntile + 4):
                    need = [(q, k) for q, k in taps if 0 <= rt - q < ntile]
                    if not need:
                        continue
                    src = gext[SUB * rt:SUB * (rt + 1), sl] if j == 0 else gcp[j - 1, SUB * rt:SUB * (rt + 1), sl]
                    for q, k in need:
                        prod = dts[rt - q] * src
                        sums[k] = prod if sums[k] is None else sums[k] + prod
                for _, k in taps:
                    dw_ref[SUB * k:SUB * (k + 1), sl] += sums[k]
            return carry

        lax.fori_loop(0, D_MODEL // 128, lanes, 0)
        lax.fori_loop(0, D_MODEL // 128, lanes_w, 0)
        dg = dgl[...]
        dv_ref[...] = (dg * sg).astype(BF16)
        dgt_ref[...] = (dg * v * sg * (1.0 - sg)).astype(BF16)

    return pl.pallas_call(
        body, name="conv_bwd", grid=(t // TM,),
        in_specs=[_rows(D_MODEL), _next(HALO, D_MODEL, t), _rows(D_MODEL, 0), _rows(D_MODEL, 1),
                  _prev(HALO, D_MODEL, 0), _prev(HALO, D_MODEL, 1), _full((HALO, D_MODEL))],
        out_specs=[_rows(D_MODEL), _rows(D_MODEL), _full((HALO * SUB, D_MODEL)), _full((1, D_MODEL))],
        out_shape=[jax.ShapeDtypeStruct((t, D_MODEL), BF16), jax.ShapeDtypeStruct((t, D_MODEL), BF16),
                   jax.ShapeDtypeStruct((HALO * SUB, D_MODEL), F32), jax.ShapeDtypeStruct((1, D_MODEL), F32)],
        scratch_shapes=[pltpu.VMEM((TM + HALO, D_MODEL), F32), pltpu.VMEM((TM + HALO, D_MODEL), F32),
                        pltpu.VMEM((TM, D_MODEL), F32),
                        pltpu.VMEM((SUB - 1, TM + HALO - SUB, D_MODEL), F32),
                        pltpu.VMEM((SUB - 1, TM + HALO - SUB, D_MODEL), F32)],
        compiler_params=_params())(dcv, dcv, q, q, q, q, cw)


def _column_segments(widths, ns):
    segs = []
    col = 0
    for p, wd in enumerate(widths):
        a = 0
        while a < wd:
            s, lo = divmod(col + a, ns)
            ln = min(wd - a, ns - lo)
            segs.append((p, a, a + ln, s, lo, lo + ln))
            a += ln
        col += wd
    return segs


def _in_bwd(dparts, w, x, g, dres, name, ride=None):
    t = x.shape[0]
    widths = [p.shape[1] for p in dparts]
    npart = len(dparts)
    segs = _column_segments(widths, w.shape[2])
    nstep = t // TM_MM

    def body(*refs):
        d_refs = refs[:npart]
        w_ref, x_ref, g_ref, r_ref = refs[npart:npart + 4]
        if ride is None:
            dx_ref, dg_ref, dw_ref = refs[npart + 4:]
        else:
            v_ref, dx_ref, dg_ref, dw_ref, o_ref = refs[npart + 4:npart + 9]
            start, forward, finish = _gather_all_phases(v_ref, o_ref, *refs[npart + 9:])
        i = pl.program_id(0)
        if ride is not None:
            pl.when(i == 0)(start)
            pl.when(i == nstep // 2)(forward)
        _zero_at_first(i, dg_ref, dw_ref)
        xv = x_ref[...]
        r = lax.rsqrt(jnp.mean(xv * xv, axis=-1, keepdims=True) + RMS_EPS)
        n = xv * r
        h = (n * g_ref[...]).astype(BF16)
        dh = None
        for p, lo, hi, s, slo, shi in segs:
            d = d_refs[p][:, lo:hi]
            part = _dot_nt(d, w_ref[s, :, slo:shi])
            dh = part if dh is None else dh + part
            dw_ref[s, :, slo:shi] += _dot_tn(h, d)
        dg_ref[...] += jnp.sum(dh * n, axis=0, keepdims=True)
        dn = dh * g_ref[...]
        dx_ref[...] = r_ref[...] + r * (dn - n * jnp.mean(dn * n, axis=-1, keepdims=True))
        if ride is not None:
            pl.when(i == nstep - 1)(finish)

    vec = _full((1, D_MODEL))
    once = pl.BlockSpec(w.shape, lambda i: (0, 0, 0), pipeline_mode=pl.Buffered(1))
    extra = [] if ride is None else [ride]
    return pl.pallas_call(
        body, name=name, grid=(nstep,),
        in_specs=[_mm_rows(wd) for wd in widths] + [once, _mm_rows(D_MODEL), vec, _mm_rows(D_MODEL)]
        + [ANY] * len(extra),
        out_specs=[_mm_rows(D_MODEL), vec, once] + [ANY] * len(extra),
        out_shape=[jax.ShapeDtypeStruct((t, D_MODEL), F32), jax.ShapeDtypeStruct((1, D_MODEL), F32),
                   jax.ShapeDtypeStruct(w.shape, F32)]
        + [jax.ShapeDtypeStruct((8,) + v.shape, v.dtype) for v in extra],
        scratch_shapes=_gather_all_scratch(ride) if extra else [],
        compiler_params=_params())(*dparts, w, x, g, dres, *extra)


def _even_bwd_out(dx1, w, yg, yp, ys, proj, wp, ps, ride):
    t = dx1.shape[0]
    nstep = t // TM_MM
    rows = TM_MM

    def body(dx_ref, w_ref, yg_ref, yp_ref, ys_ref, z_ref, u_ref, h_ref, wp_ref, ps_ref, g_ref,
             dy_ref, dz_ref, dw_ref, du_ref, dwp_ref, dps_ref, theirs_ref, nxt, send_sems, recv_sems):
        i = pl.program_id(0)
        blk = nstep - 1 - i
        start, finish = _pair_split_phases(g_ref, theirs_ref, send_sems, recv_sems)
        pl.when(i == 0)(start)
        _zero_at_first(i, dw_ref, dwp_ref, dps_ref, nxt)
        dxb = dx_ref[...].astype(BF16)
        dyg = _dot_nt(dxb, w_ref[...])
        dw_ref[...] += _dot_tn(yg_ref[...], dxb)
        silu_z, dsilu_z = _dsilu(z_ref[...].astype(F32))
        dyb = (dyg * silu_z).astype(BF16)
        dy_ref[...] = dyb
        dz_ref[:, :POOL_W] = (dyg[:, :POOL_W] * yp_ref[...] * dsilu_z[:, :POOL_W]).astype(BF16)
        dz_ref[:, POOL_W:] = (dyg[:, POOL_W:] * ys_ref[...] * dsilu_z[:, POOL_W:]).astype(BF16)
        pos = (blk * rows + 1 + lax.broadcasted_iota(jnp.int32, (rows, 1), 0)).astype(F32)
        pos_ext = (blk * rows + 1 + lax.broadcasted_iota(jnp.int32, (rows + POOL_HALO, 1), 0)).astype(F32)
        for gi in range(4):
            sl = slice(128 * gi, 128 * (gi + 1))
            wd = float(2 << gi)
            u = u_ref[:, sl].astype(F32)
            halo = jnp.where(blk == 0, 0.0, h_ref[:, sl].astype(F32))
            s = _pool_sums(jnp.concatenate([halo, u], axis=0), gi, True)[POOL_HALO:, :]
            pooled = (s / jnp.minimum(pos, wd) - u).astype(BF16)
            dy = dyb[:, sl].astype(F32)
            dps_ref[:, sl] += jnp.sum(dy * _dot(pooled, wp_ref[gi]), axis=0, keepdims=True)
            dmix = (jnp.concatenate([dy, nxt[:, sl]], axis=0) * ps_ref[:, sl]).astype(BF16)
            dwp_ref[gi] += _dot_tn(pooled, dmix[:rows, :])
            dpool = _dot_nt(dmix, wp_ref[gi])
            lead = _pool_sums(dpool / jnp.minimum(pos_ext, wd), gi, False)
            du_ref[:, sl] = (lead[:rows, :] - dpool[:rows, :]).astype(BF16)
            nxt[:, sl] = dy[:POOL_HALO, :]
        pl.when(i == nstep - 1)(finish)

    back = lambda wdt, cb=0: _rows(wdt, cb, rev=nstep, tm=rows)
    per = rows // POOL_HALO
    halo = pl.BlockSpec((POOL_HALO, POOL_W), lambda i: (jnp.maximum((nstep - 1 - i) * per - 1, 0), 0))
    mat = _full((D_MODEL, D_MODEL))
    return pl.pallas_call(
        body, name="even_bwd_out", grid=(nstep,),
        in_specs=[back(D_MODEL), mat, back(D_MODEL), back(POOL_W), back(SSM_W), back(D_MODEL, 1),
                  back(POOL_W, 0), halo, _full((4, 128, 128)), _full((1, POOL_W)), ANY],
        out_specs=[back(D_MODEL), back(D_MODEL), mat, back(POOL_W), _full((4, 128, 128)), _full((1, POOL_W)), ANY],
        out_shape=[jax.ShapeDtypeStruct((t, D_MODEL), BF16), jax.ShapeDtypeStruct((t, D_MODEL), BF16),
                   jax.ShapeDtypeStruct((D_MODEL, D_MODEL), F32), jax.ShapeDtypeStruct((t, POOL_W), BF16),
                   jax.ShapeDtypeStruct((4, 128, 128), F32), jax.ShapeDtypeStruct((1, POOL_W), F32),
                   _pair_split_shape(ride)],
        scratch_shapes=[pltpu.VMEM((POOL_HALO, POOL_W), F32)] + _pair_split_scratch(ride),
        compiler_params=_params())(dx1, w, yg, yp, ys, proj, proj, proj, wp, ps, ride)


def _ssm_bwd(dycat, proj, car_in_re, car_in_im, pm, pmt, mb_re, mb_im, p8_re, p8_im, q_re, q_im, qr_re, qr_im,
             cm_re, cm_im, dskip, wglu, ride):
    t = proj.shape[0]
    nblk = t // TM

    def body(dy_ref, u_ref, cin_re, cin_im, pm_ref, pmt_ref, mbre, mbim, p8re, p8im, qre, qim, qrre, qrim,
             cmre, cmim, d_ref, wg_ref, p_ref,
             du_ref, dmbre, dmbim, dcmre, dcmim, dare, daim, dd_ref, dwg_ref, got_ref,
             xs_re, xs_im, gs_re, gs_im, car_re, car_im, ent_re, ent_im, gcar_re, gcar_im, ysk, dysk,
             bounce, send_sems, recv_sems, local_sems):
        i = pl.program_id(0)
        start, finish = _chip_scatter_phases(p_ref, got_ref, bounce, send_sems, recv_sems, local_sems)
        pl.when(i == 0)(start)
        _zero_at_first(i, dmbre, dmbim, dcmre, dcmim, dare, daim, dd_ref, dwg_ref, gcar_re, gcar_im)
        us = _dot(pm_ref[...], u_ref[...])
        usb = us.astype(BF16)
        for j in range(4):
            xs_re[:, LCH * j:LCH * (j + 1)] = _dot(usb[:, 128 * j:128 * (j + 1)], mbre[j])
            xs_im[:, LCH * j:LCH * (j + 1)] = _dot(usb[:, 128 * j:128 * (j + 1)], mbim[j])
        car_re[...] = cin_re[0]
        car_im[...] = cin_im[0]
        _scan_fwd_block(xs_re, xs_im, p8re, p8im, qre, qim, car_re, car_im, ent_re, ent_im)
        for j in range(4):
            sl = slice(LCH * j, LCH * (j + 1))
            ysk[:, 128 * j:128 * (j + 1)] = (_dot_nt(xs_re[:, sl].astype(BF16), cmre[j])
                                             - _dot_nt(xs_im[:, sl].astype(BF16), cmim[j]))
        yvb = (ysk[...] + d_ref[...] * us).astype(BF16)
        gv = _dot(yvb, wg_ref[...])
        sg = _sig(gv[:, SSM_W:])
        dyss = _dot(pm_ref[...], dy_ref[...])
        dval = (dyss * sg).astype(BF16)
        dgate = (dyss * gv[:, :SSM_W] * sg * (1.0 - sg)).astype(BF16)
        dy = _dot_nt(dval, wg_ref[:, :SSM_W]) + _dot_nt(dgate, wg_ref[:, SSM_W:])
        dwg_ref[:, :SSM_W] += _dot_tn(yvb, dval)
        dwg_ref[:, SSM_W:] += _dot_tn(yvb, dgate)
        dd_ref[...] += jnp.sum(dy * us, axis=0, keepdims=True)
        dysk[...] = dy
        for j in range(4):
            sl = slice(LCH * j, LCH * (j + 1))
            dyj = dy[:, 128 * j:128 * (j + 1)].astype(BF16)
            gs_re[:, sl] = _dot(dyj, cmre[j])
            gs_im[:, sl] = -_dot(dyj, cmim[j])
            dcmre[j] += _dot_tn(dyj, xs_re[:, sl].astype(BF16))
            dcmim[j] -= _dot_tn(dyj, xs_im[:, sl].astype(BF16))
        row = lax.broadcasted_iota(jnp.int32, (SUB, SCAN_L), 0)
        for j in range(STATES // SCAN_L):
            sl = slice(SCAN_L * j, SCAN_L * (j + 1))
            are, aim = p8re[0:SUB, sl], -p8im[0:SUB, sl]

            def totals(k, v, sl=sl, are=are, aim=aim):
                r0 = pl.multiple_of((SEG_LEN - 2 - k) * SUB, SUB)
                vre, vim = _cmul_add(are, aim, v[0], v[1], gs_re[pl.ds(r0, SUB), sl], gs_im[pl.ds(r0, SUB), sl])
                gs_re[pl.ds(r0, SUB), sl] = vre
                gs_im[pl.ds(r0, SUB), sl] = vim
                return vre, vim

            top = (SEG_LEN - 1) * SUB
            fre, fim = lax.fori_loop(0, SEG_LEN - 1, totals,
                                     (gs_re[top:top + SUB, sl], gs_im[top:top + SUB, sl]), unroll=2)
            fre, fim, nre, nim = _segment_chain(fre, fim, qrre[:, sl], -qrim[:, sl],
                                                gcar_re[:, sl], gcar_im[:, sl], row, True)
            gcar_re[:, sl] = jnp.broadcast_to(fre[0:1, :], (SUB, SCAN_L))
            gcar_im[:, sl] = jnp.broadcast_to(fim[0:1, :], (SUB, SCAN_L))

            def fix(i2, acc, sl=sl, nre=nre, nim=nim):
                r0 = pl.multiple_of(i2 * SUB, SUB)
                rb = pl.multiple_of((SEG_LEN - 1 - i2) * SUB, SUB)
                gre, gim = _cmul_add(p8re[pl.ds(rb, SUB), sl], -p8im[pl.ds(rb, SUB), sl], nre, nim,
                                     gs_re[pl.ds(r0, SUB), sl], gs_im[pl.ds(r0, SUB), sl])
                gs_re[pl.ds(r0, SUB), sl] = gre
                gs_im[pl.ds(r0, SUB), sl] = gim
                rp = pl.multiple_of((i2 - 1) * SUB, SUB)
                xre, xim = xs_re[pl.ds(rp, SUB), sl], xs_im[pl.ds(rp, SUB), sl]
                return acc[0] + gre * xre + gim * xim, acc[1] + gim * xre - gre * xim

            g0re, g0im = _cmul_add(p8re[top:top + SUB, sl], -p8im[top:top + SUB, sl], nre, nim,
                                   gs_re[0:SUB, sl], gs_im[0:SUB, sl])
            gs_re[0:SUB, sl] = g0re
            gs_im[0:SUB, sl] = g0im
            ere, eim = ent_re[:, sl], ent_im[:, sl]
            acc0 = (dare[:, sl] + g0re * ere + g0im * eim, daim[:, sl] + g0im * ere - g0re * eim)
            are_acc, aim_acc = lax.fori_loop(1, SEG_LEN, fix, acc0, unroll=2)
            dare[:, sl] = are_acc
            daim[:, sl] = aim_acc
        for j in range(4):
            sl = slice(LCH * j, LCH * (j + 1))
            c4 = slice(128 * j, 128 * (j + 1))
            gre = gs_re[:, sl].astype(BF16)
            gim = gs_im[:, sl].astype(BF16)
            dmbre[j] += _dot_tn(usb[:, c4], gre)
            dmbim[j] += _dot_tn(usb[:, c4], gim)
            dysk[:, c4] = _dot_nt(gre, mbre[j]) + _dot_nt(gim, mbim[j]) + dysk[:, c4] * d_ref[:, c4]
        du_ref[...] = _dot(pmt_ref[...], dysk[...].astype(BF16)).astype(BF16)
        pl.when(i == nblk - 1)(finish)

    blk = (4, 128, LCH)
    pw = _full((SUB, STATES))
    p8 = _full((TM, STATES))
    perm = _full((TM, TM))
    car = pl.BlockSpec((1, SUB, STATES), lambda i: (nblk - 1 - i, 0, 0))
    big = lambda: pltpu.VMEM((TM, STATES), F32)
    small = lambda: pltpu.VMEM((SUB, STATES), F32)
    return pl.pallas_call(
        body, name="ssm_bwd", grid=(nblk,),
        in_specs=[_rows(SSM_W, 1, rev=nblk), _rows(SSM_W, 1, rev=nblk), car, car, perm, perm, _full(blk), _full(blk),
                  p8, p8, pw, pw, pw, pw, _full(blk), _full(blk), _full((1, SSM_W)), _full((SSM_W, 2 * SSM_W)), ANY],
        out_specs=[_rows(SSM_W, 0, rev=nblk), _full(blk), _full(blk), _full(blk), _full(blk), pw, pw,
                   _full((1, SSM_W)), _full((SSM_W, 2 * SSM_W)), ANY],
        out_shape=[jax.ShapeDtypeStruct((t, SSM_W), BF16)] + [jax.ShapeDtypeStruct(blk, F32)] * 4
        + [jax.ShapeDtypeStruct((SUB, STATES), F32)] * 2
        + [jax.ShapeDtypeStruct((1, SSM_W), F32), jax.ShapeDtypeStruct((SSM_W, 2 * SSM_W), F32),
           jax.ShapeDtypeStruct(ride.shape, ride.dtype)],
        scratch_shapes=[big(), big(), big(), big(), small(), small(), small(), small(), small(), small(),
                        pltpu.VMEM((TM, SSM_W), F32), pltpu.VMEM((TM, SSM_W), F32)] + _chip_scatter_scratch(ride),
        compiler_params=_params())(dycat, proj, car_in_re, car_in_im, pm, pmt, mb_re, mb_im, p8_re, p8_im,
                                   q_re, q_im, qr_re, qr_im, cm_re, cm_im, dskip, wglu, ride)


def _adamw(w, g, m, v, name):
    rows = w.shape[0]
    tr = 256 if rows % 256 == 0 else rows
    c1 = 1.0 / (1.0 - ADAM_B1 ** ADAM_STEP)
    c2 = 1.0 / (1.0 - ADAM_B2 ** ADAM_STEP)

    def body(w_ref, g_ref, m_ref, v_ref, d_ref, nm_ref, nv_ref):
        gv = g_ref[...]
        m = ADAM_B1 * m_ref[...] + (1.0 - ADAM_B1) * gv
        v = ADAM_B2 * v_ref[...] + (1.0 - ADAM_B2) * (gv * gv)
        nm_ref[...] = m
        nv_ref[...] = v
        d_ref[...] = -ADAM_LR * ((m * c1) / (jnp.sqrt(v * c2) + ADAM_EPS) + ADAM_WD * w_ref[...])

    spec = pl.BlockSpec((tr, D_MODEL), lambda i: (i, 0))
    shp = jax.ShapeDtypeStruct((rows, D_MODEL), F32)
    return pl.pallas_call(
        body, name=name, grid=(rows // tr,), in_specs=[spec] * 4, out_specs=[spec] * 3, out_shape=[shp] * 3,
        compiler_params=_params())(w, g, m, v)


def _core_index():
    return lax.axis_index("c").astype(jnp.int32).reshape(1)


def _pair_add(g, theirs, out_dtype, name):
    n, half, _ = theirs.shape
    br = 128
    nb = half // br

    def body(c_ref, a_ref, b_ref, o_ref):
        o_ref[...] = (a_ref[...] + b_ref[...]).astype(out_dtype)

    spec = pl.BlockSpec((1, br, D_MODEL), lambda i, j, c: (i, j, 0))
    grid_spec = pltpu.PrefetchScalarGridSpec(
        num_scalar_prefetch=1, grid=(n, nb),
        in_specs=[pl.BlockSpec((1, br, D_MODEL), lambda i, j, c: (i, c[0] * nb + j, 0)), spec], out_specs=spec)
    return pl.pallas_call(
        body, name=name, grid_spec=grid_spec, out_shape=jax.ShapeDtypeStruct(theirs.shape, out_dtype),
        compiler_params=_params(2))(_core_index(), g, theirs)


def _adamw_rows(w, m, v, g_mine, g_theirs, row0, br, name):
    rows = w.shape[0]
    b0 = row0 // br
    per_half = g_mine.shape[0] // br
    c1 = 1.0 / (1.0 - ADAM_B1 ** ADAM_STEP)
    c2 = 1.0 / (1.0 - ADAM_B2 ** ADAM_STEP)

    def body(c_ref, w_ref, gm_ref, gt_ref, m_ref, v_ref, g_ref, d_ref, nm_ref, nv_ref):
        gv = jnp.where((b0 + pl.program_id(0)) // per_half == c_ref[0], gm_ref[...], gt_ref[...])
        m = ADAM_B1 * m_ref[...] + (1.0 - ADAM_B1) * gv
        v = ADAM_B2 * v_ref[...] + (1.0 - ADAM_B2) * (gv * gv)
        g_ref[...] = gv
        nm_ref[...] = m
        nv_ref[...] = v
        d_ref[...] = -ADAM_LR * ((m * c1) / (jnp.sqrt(v * c2) + ADAM_EPS) + ADAM_WD * w_ref[...])

    spec = pl.BlockSpec((br, D_MODEL), lambda i, c: (i, 0))
    part = pl.BlockSpec((br, D_MODEL), lambda i, c: ((b0 + i) % per_half, 0))
    shp = jax.ShapeDtypeStruct((rows, D_MODEL), F32)
    grid_spec = pltpu.PrefetchScalarGridSpec(
        num_scalar_prefetch=1, grid=(rows // br,), in_specs=[spec, part, part, spec, spec], out_specs=[spec] * 4)
    return pl.pallas_call(
        body, name=name, grid_spec=grid_spec, out_shape=[shp] * 4,
        compiler_params=_params())(_core_index(), w, g_mine, g_theirs, m, v)


def _sum_lead(a, name):
    n, rows, _ = a.shape
    tr = 128 if rows % 128 == 0 else rows

    def body(a_ref, o_ref):
        acc = a_ref[0].astype(F32)
        for k in range(1, n):
            acc = acc + a_ref[k].astype(F32)
        o_ref[...] = acc

    return pl.pallas_call(
        body, name=name, grid=(rows // tr,),
        in_specs=[pl.BlockSpec((n, tr, D_MODEL), lambda i: (0, i, 0))],
        out_specs=pl.BlockSpec((tr, D_MODEL), lambda i: (i, 0)),
        out_shape=jax.ShapeDtypeStruct((rows, D_MODEL), F32), compiler_params=_params())(a)


ANY = pl.BlockSpec(memory_space=pl.ANY)


def _mesh_pos():
    return lax.axis_index("x"), lax.axis_index("y"), lax.axis_index("c")


def _gather_phases(in_refs, out_refs, bounces, send_sems, recv_sems, local_sems):
    na = len(in_refs)
    halves = [r.shape[0] // 2 for r in in_refs]
    ncopy = 3 * na
    x, y, c = _mesh_pos()
    me = 2 * x + y
    sibling = (x, y, 1 - c)
    chips = [(1 - x, y), (x, 1 - y), (1 - x, 1 - y)]
    ids = [2 * chip[0] + chip[1] for chip in chips]

    def piece(a, q, h):
        return out_refs[a].at[q, pl.ds(h * halves[a], halves[a]), :]

    def copy(s, a, q, h, to, src=None):
        return pltpu.make_async_remote_copy(
            src_ref=piece(a, q, h) if src is None else src, dst_ref=piece(a, q, h),
            send_sem=send_sems.at[s], recv_sem=recv_sems.at[s], device_id=to, device_id_type=MESH)

    def sends():
        return [copy(j * na + a, a, me, c, (*chip, c), src=in_refs[a].at[pl.ds(c * halves[a], halves[a]), :])
                for j, chip in enumerate(chips) for a in range(na)]

    def forwards():
        return [copy(ncopy + j * na + a, a, ids[j], c, sibling) for j in range(3) for a in range(na)]

    def stores():
        return [pltpu.make_async_copy(bounces[a], out_refs[a].at[me], local_sems.at[na + a]) for a in range(na)]

    def start():
        loads = [pltpu.make_async_copy(in_refs[a], bounces[a], local_sems.at[a]) for a in range(na)]
        for cp in loads:
            cp.start()
        for cp in sends():
            cp.start()
        for ld, st in zip(loads, stores()):
            ld.wait()
            st.start()

    def forward():
        fwd = forwards()
        for j in range(3):
            for a in range(na):
                copy(j * na + a, a, ids[j], c, (x, y, c)).wait_recv()
                fwd[j * na + a].start()

    def finish():
        for j in range(3):
            for a in range(na):
                copy(ncopy + j * na + a, a, ids[j], 1 - c, (x, y, c)).wait_recv()
        for cp in sends() + forwards():
            cp.wait_send()
        for cp in stores():
            cp.wait()

    return start, forward, finish


def _gather_scratch(arrs):
    ncopy = 3 * len(arrs)
    return ([pltpu.VMEM(a.shape, a.dtype) for a in arrs]
            + [pltpu.SemaphoreType.DMA((2 * ncopy,)), pltpu.SemaphoreType.DMA((2 * ncopy,)),
               pltpu.SemaphoreType.DMA((2 * len(arrs),))])


def _gather_weights(arrs):
    na = len(arrs)

    def body(*refs):
        start, forward, finish = _gather_phases(refs[:na], refs[na:2 * na], refs[2 * na:3 * na], *refs[3 * na:])
        start()
        forward()
        finish()

    return pl.pallas_call(
        body, name="gather_weights", in_specs=[ANY] * na, out_specs=[ANY] * na,
        out_shape=[jax.ShapeDtypeStruct((4,) + a.shape, a.dtype) for a in arrs],
        scratch_shapes=_gather_scratch(arrs),
        compiler_params=pltpu.CompilerParams(vmem_limit_bytes=VMEM_LIMIT),
    )(*arrs)


def _gather_all_phases(v_ref, o_ref, bounce, send_sems, recv_sems, local_sems):
    x, y, c = _mesh_pos()
    sibling = (x, y, 1 - c)
    chips = [(1 - x, y), (x, 1 - y), (1 - x, 1 - y)]

    def blk(px, py, pc):
        return o_ref.at[4 * px + 2 * py + pc]

    def copy(k, block, to, src=None):
        return pltpu.make_async_remote_copy(
            src_ref=blk(*block) if src is None else src, dst_ref=blk(*block),
            send_sem=send_sems.at[k], recv_sem=recv_sems.at[k], device_id=to, device_id_type=MESH)

    def first():
        return ([copy(0, (x, y, c), sibling, src=v_ref)]
                + [copy(1 + j, (x, y, c), (*chip, c), src=v_ref) for j, chip in enumerate(chips)])

    def passed():
        return [copy(4 + j, (*chip, c), sibling) for j, chip in enumerate(chips)]

    def store():
        return pltpu.make_async_copy(bounce, blk(x, y, c), local_sems.at[1])

    def start():
        load = pltpu.make_async_copy(v_ref, bounce, local_sems.at[0])
        load.start()
        for cp in first():
            cp.start()
        load.wait()
        store().start()

    def forward():
        fwd = passed()
        for j, chip in enumerate(chips):
            copy(1 + j, (*chip, c), (x, y, c)).wait_recv()
            fwd[j].start()

    def finish():
        copy(0, (x, y, 1 - c), (x, y, c)).wait_recv()
        for j, chip in enumerate(chips):
            copy(4 + j, (*chip, 1 - c), (x, y, c)).wait_recv()
        for cp in first() + passed():
            cp.wait_send()
        store().wait()

    return start, forward, finish


def _gather_all_scratch(v):
    return [pltpu.VMEM(v.shape, v.dtype), pltpu.SemaphoreType.DMA((7,)), pltpu.SemaphoreType.DMA((7,)),
            pltpu.SemaphoreType.DMA((2,))]


def _scatter_and_gather(p, v, name):
    def body(p_ref, v_ref, got_ref, o_ref, p_bounce, p_send, p_recv, p_local, bounce, send_sems, recv_sems,
             local_sems):
        start, finish = _chip_scatter_phases(p_ref, got_ref, p_bounce, p_send, p_recv, p_local)
        g_start, g_forward, g_finish = _gather_all_phases(v_ref, o_ref, bounce, send_sems, recv_sems, local_sems)
        start()
        g_start()
        g_forward()
        g_finish()
        finish()

    return pl.pallas_call(
        body, name=name, in_specs=[ANY, ANY], out_specs=[ANY, ANY],
        out_shape=[jax.ShapeDtypeStruct(p.shape, p.dtype), jax.ShapeDtypeStruct((8,) + v.shape, v.dtype)],
        scratch_shapes=_chip_scatter_scratch(p) + _gather_all_scratch(v),
    )(p, v)


def _pair_split_phases(g_ref, theirs_ref, send_sems, recv_sems):
    n, rows, _ = g_ref.shape
    half = rows // 2
    ch = half // COMM_CHUNKS
    x, y, c = _mesh_pos()

    def gives():
        return [pltpu.make_async_remote_copy(
            src_ref=g_ref.at[q, pl.ds((1 - c) * half + k * ch, ch), :],
            dst_ref=theirs_ref.at[q, pl.ds(k * ch, ch), :],
            send_sem=send_sems.at[q * COMM_CHUNKS + k], recv_sem=recv_sems.at[q * COMM_CHUNKS + k],
            device_id=(x, y, 1 - c), device_id_type=MESH) for q in range(n) for k in range(COMM_CHUNKS)]

    def start():
        for cp in gives():
            cp.start()

    def finish():
        for cp in gives():
            cp.wait()

    return start, finish


def _pair_split_scratch(g):
    return [pltpu.SemaphoreType.DMA((g.shape[0] * COMM_CHUNKS,)), pltpu.SemaphoreType.DMA((g.shape[0] * COMM_CHUNKS,))]


def _pair_split_shape(g):
    return jax.ShapeDtypeStruct((g.shape[0], g.shape[1] // 2, D_MODEL), g.dtype)


def _pair_split(g, name):
    def body(g_ref, theirs_ref, send_sems, recv_sems):
        start, finish = _pair_split_phases(g_ref, theirs_ref, send_sems, recv_sems)
        start()
        finish()

    return pl.pallas_call(
        body, name=name, in_specs=[ANY], out_specs=ANY, out_shape=_pair_split_shape(g),
        scratch_shapes=_pair_split_scratch(g))(g)


def _chip_scatter_phases(p_ref, o_ref, bounce, send_sems, recv_sems, local_sems):
    x, y, c = _mesh_pos()
    me = 2 * x + y
    chips = [(1 - x, y), (x, 1 - y), (1 - x, 1 - y)]

    def keep():
        return pltpu.make_async_copy(bounce, o_ref.at[me], local_sems.at[1])

    def sends():
        return [pltpu.make_async_remote_copy(
            src_ref=p_ref.at[2 * chip[0] + chip[1]], dst_ref=o_ref.at[me],
            send_sem=send_sems.at[j], recv_sem=recv_sems.at[j], device_id=(*chip, c), device_id_type=MESH)
            for j, chip in enumerate(chips)]

    def start():
        load = pltpu.make_async_copy(p_ref.at[me], bounce, local_sems.at[0])
        load.start()
        for cp in sends():
            cp.start()
        load.wait()
        keep().start()

    def finish():
        for j, chip in enumerate(chips):
            q = 2 * chip[0] + chip[1]
            pltpu.make_async_remote_copy(
                src_ref=p_ref.at[q], dst_ref=o_ref.at[q], send_sem=send_sems.at[j], recv_sem=recv_sems.at[j],
                device_id=(*chip, c), device_id_type=MESH).wait_recv()
        for cp in sends():
            cp.wait_send()
        keep().wait()

    return start, finish


def _chip_scatter_scratch(p):
    return [pltpu.VMEM(p.shape[1:], p.dtype), pltpu.SemaphoreType.DMA((3,)), pltpu.SemaphoreType.DMA((3,)),
            pltpu.SemaphoreType.DMA((2,))]


def _pair_join(r, name):
    rows = r.shape[0]
    ch = rows // COMM_CHUNKS

    def body(r_ref, o_ref, send_sems, recv_sems):
        x, y, c = _mesh_pos()
        gives = [pltpu.make_async_remote_copy(
            src_ref=r_ref.at[pl.ds(k * ch, ch), :], dst_ref=o_ref.at[pl.ds(k * ch, ch), :],
            send_sem=send_sems.at[k], recv_sem=recv_sems.at[k], device_id=(x, y, 1 - c), device_id_type=MESH)
            for k in range(COMM_CHUNKS)]
        for cp in gives:
            cp.start()
        for cp in gives:
            cp.wait()

    return pl.pallas_call(
        body, name=name, in_specs=[ANY], out_specs=ANY, out_shape=jax.ShapeDtypeStruct(r.shape, r.dtype),
        scratch_shapes=[pltpu.SemaphoreType.DMA((COMM_CHUNKS,)), pltpu.SemaphoreType.DMA((COMM_CHUNKS,))],
    )(r)


SHARD_BIG = (("even_w_in", (1024, 512)), ("ssm_w_glu", (512, 256)), ("even_w_out", (256, 1024)),
             ("odd_w_in", (1024, 768)), ("odd_w_out", (256, 1024)))
SHARD_SMALL = (("odd_norm", 1), ("conv_w", CONV_K), ("conv_b", 1), ("conv_ln_g", 1), ("conv_ln_b", 1))
REP_NAMES = (("even_norm", (1024,)), ("pool_w", (4, 128, 128)), ("pool_scale", (512,)), ("ssm_log_dt", (32,)),
             ("ssm_a_re", (32, 64)), ("ssm_a_im", (32, 64)), ("ssm_b_re", (32, 64, 16)), ("ssm_b_im", (32, 64, 16)),
             ("ssm_c_re", (32, 16, 64)), ("ssm_c_im", (32, 16, 64)), ("ssm_d", (512,)), ("final_norm", (1024,)))


def _pack_rep(d):
    flat = jnp.concatenate([d[n].reshape(-1) for n, _ in REP_NAMES])
    return jnp.pad(flat, (0, REP_ROWS * D_MODEL - flat.shape[0])).reshape(REP_ROWS, D_MODEL)


def _unpack_rep(buf):
    flat = buf.reshape(-1)
    out = {}
    off = 0
    for n, shp in REP_NAMES:
        size = 1
        for s in shp:
            size *= s
        out[n] = flat[off:off + size].reshape(shp)
        off += size
    return out


def _cols_split(full, cols):
    rows = full.shape[0]
    return full.reshape(rows, 4, cols).transpose(1, 0, 2).reshape(4, -1, D_MODEL)


def _block_diag(a):
    a = a.reshape(4, 8, GROUP_DIM, N_STATE)
    eye = jnp.eye(8, dtype=a.dtype)
    return (a[:, :, :, None, :] * eye[None, :, None, :, None]).reshape(4, 128, LCH)


def _block_diag_take(m):
    m = m.reshape(4, 8, GROUP_DIM, 8, N_STATE)
    eye = jnp.eye(8, dtype=m.dtype)
    return jnp.sum(m * eye[None, :, None, :, None], axis=3).reshape(N_GROUPS, GROUP_DIM, N_STATE)


def _ssm_discretise(log_dt, a_re, a_im, b_re, b_im):
    dt = jnp.exp(log_dt)[:, None]
    mag = jnp.exp(a_re * dt)
    ang = a_im * dt
    abar_re = mag * jnp.cos(ang)
    abar_im = mag * jnp.sin(ang)
    den = a_re * a_re + a_im * a_im
    nr = abar_re - 1.0
    ni = abar_im
    k_re = (nr * a_re + ni * a_im) / den
    k_im = (ni * a_re - nr * a_im) / den
    bb_re = k_re[..., None] * b_re - k_im[..., None] * b_im
    bb_im = k_re[..., None] * b_im + k_im[..., None] * b_re
    return abar_re, abar_im, bb_re, bb_im


def _scan_tables(log_dt, a_re, a_im):
    dt = jnp.exp(log_dt)[:, None]
    lam_re = (a_re * dt).reshape(1, STATES)
    lam_im = (a_im * dt).reshape(1, STATES)

    def powers(k):
        mag = jnp.exp(k * lam_re)
        return mag * jnp.cos(k * lam_im), mag * jnp.sin(k * lam_im)

    p_re, p_im = powers((1 + jnp.arange(TM) // SUB).astype(F32)[:, None])
    q_re, q_im = powers((SEG_LEN * (1 + jnp.arange(SUB))).astype(F32)[:, None])
    return p_re, p_im, q_re, q_im


def _local_step(x, tgt, w, shard):
    row = lambda a: a.reshape(1, -1)
    (e_w_in,) = _gather_weights([shard["even_w_in"].astype(BF16)])
    wp = w["pool_w"].astype(BF16)
    ssm_in = (w["ssm_log_dt"], w["ssm_a_re"], w["ssm_a_im"], w["ssm_b_re"], w["ssm_b_im"])
    (abar_re, abar_im, bb_re, bb_im), ssm_vjp = jax.vjp(_ssm_discretise, *ssm_in)
    mb_re = _block_diag(bb_re.transpose(0, 2, 1)).astype(BF16)
    mb_im = _block_diag(bb_im.transpose(0, 2, 1)).astype(BF16)
    cm_re = _block_diag(w["ssm_c_re"]).astype(BF16)
    cm_im = _block_diag(w["ssm_c_im"]).astype(BF16)
    p8_re, p8_im, q_re, q_im = _scan_tables(w["ssm_log_dt"], w["ssm_a_re"], w["ssm_a_im"])
    qr_re, qr_im = q_re[::-1], q_im[::-1]
    pm = _perm_matrix()
    pmt = pm.T
    g0, gf = row(w["even_norm"]), row(w["final_norm"])
    ps, dskip = row(w["pool_scale"]), row(w["ssm_d"])

    proj, yp, (g_glu, g_eout) = _norm_in(
        x, g0, e_w_in, "even_in", ride=[shard["ssm_w_glu"].astype(BF16), shard["even_w_out"].astype(BF16)],
        pool=(wp, ps))
    wglu = g_glu.transpose(1, 0, 2).reshape(SSM_W, 2 * SSM_W)
    e_w_out = g_eout.reshape(D_MODEL, D_MODEL)
    (ys, car_re, car_im), (g_oin, g_oout, g_small) = _ssm_fwd(
        proj, pm, pmt, mb_re, mb_im, p8_re, p8_im, q_re, q_im, cm_re, cm_im, dskip, wglu, ride=_odd_shards(shard))
    o_w_in, o_w_out = g_oin, g_oout.reshape(D_MODEL, D_MODEL)
    sm = g_small.transpose(1, 0, 2).reshape(SMALL_ROWS, D_MODEL)
    cw = sm[1:1 + HALO]
    g1, cb, lg, lb = sm[0:1], sm[32:33], sm[33:34], sm[34:35]
    x1, yg = _even_out(yp, ys, proj, x, e_w_out)
    q, _, _ = _norm_in(x1, g1, o_w_in, "odd_in")
    y2, cv = _conv_fwd(q, cw, cb, lg, lb)
    dx2, loss_lanes, d_gf = _odd_out_loss(y2, x1, o_w_out, gf, tgt)

    dcv, dz2, d_o_w_out, d_lg, d_lb = _odd_bwd_out(dx2, o_w_out, y2, cv, q, lg, lb)
    dval, dgate, d_cw, d_cb = _conv_bwd(dcv, q, cw)
    dx1, d_g1, d_o_w_in = _in_bwd([dval, dgate, dz2], o_w_in, x1, g1, dx2, "odd_in_bwd")
    g_odd = _pack_odd_grads({
        "odd_w_in": d_o_w_in, "odd_w_out": d_o_w_out, "odd_norm": d_g1.reshape(-1),
        "conv_w": d_cw.reshape(HALO, SUB, D_MODEL).sum(axis=1)[:CONV_K], "conv_b": d_cb.reshape(-1),
        "conv_ln_g": d_lg.reshape(-1), "conv_ln_b": d_lb.reshape(-1)})
    dycat, dz, d_e_w_out, dup, d_wp, d_ps, theirs_odd = _even_bwd_out(dx1, e_w_out, yg, yp, ys, proj, wp, ps, g_odd)
    sums_odd = _pair_add(g_odd, theirs_odd, BF16, "pair_add_odd")
    (dus, d_mb_re, d_mb_im, d_cm_re, d_cm_im, da_re, da_im, d_dskip, d_wglu, got_odd) = _ssm_bwd(
        dycat, proj, car_re, car_im, pm, pmt, mb_re, mb_im, p8_re, p8_im, q_re, q_im, qr_re, qr_im,
        cm_re, cm_im, dskip, wglu, sums_odd)
    d_abar_re = jnp.sum(da_re, axis=0).reshape(N_GROUPS, N_STATE)
    d_abar_im = jnp.sum(da_im, axis=0).reshape(N_GROUPS, N_STATE)
    d_bb_re = _block_diag_take(d_mb_re).transpose(0, 2, 1)
    d_bb_im = _block_diag_take(d_mb_im).transpose(0, 2, 1)
    d_log_dt, d_a_re, d_a_im, d_b_re, d_b_im = ssm_vjp((d_abar_re, d_abar_im, d_bb_re, d_bb_im))
    rep_early = _pack_rep({
        "even_norm": jnp.zeros((D_MODEL,), F32), "pool_w": d_wp, "pool_scale": d_ps.reshape(-1),
        "ssm_log_dt": d_log_dt, "ssm_a_re": d_a_re, "ssm_a_im": d_a_im, "ssm_b_re": d_b_re, "ssm_b_im": d_b_im,
        "ssm_c_re": _block_diag_take(d_cm_re), "ssm_c_im": _block_diag_take(d_cm_im),
        "ssm_d": d_dskip.reshape(-1), "final_norm": d_gf.reshape(-1)})
    dx, d_g0, d_e_w_in, rep_parts = _in_bwd([dup, dus, dz], e_w_in, x, g0, dx1, "even_in_bwd", ride=rep_early)

    grads = {"even_norm": d_g0, "even_w_in": d_e_w_in, "ssm_w_glu": d_wglu, "even_w_out": d_e_w_out}
    return jnp.sum(loss_lanes), dx, grads, got_odd, rep_parts


WEIGHT_NAMES = ("even_norm", "even_w_in", "pool_w", "pool_scale", "ssm_log_dt", "ssm_a_re", "ssm_a_im",
                "ssm_b_re", "ssm_b_im", "ssm_c_re", "ssm_c_im", "ssm_d", "ssm_w_glu", "even_w_out", "odd_norm",
                "odd_w_in", "conv_w", "conv_b", "conv_ln_g", "conv_ln_b", "odd_w_out", "final_norm")
SHARDED = tuple(n for n, _ in SHARD_BIG) + tuple(n for n, _ in SHARD_SMALL)


SMALL_ROWS = 64


def _odd_shards(shard):
    small = jnp.concatenate([shard[n].reshape(r, 256) for n, r in SHARD_SMALL], axis=0)
    small = jnp.pad(small, ((0, SMALL_ROWS - small.shape[0]), (0, 0)))
    return [shard["odd_w_in"].astype(BF16), shard["odd_w_out"].astype(BF16), small]


def _pack_small(d):
    small = jnp.concatenate([d[n].reshape(r, -1) for n, r in SHARD_SMALL], axis=0)
    if small.shape[1] == D_MODEL:
        small = small.reshape(35, 4, 256).transpose(1, 0, 2)
    small = small.reshape(-1, 35 * 256)
    small = jnp.pad(small, ((0, 0), (0, ROWS_SMALL * D_MODEL - 35 * 256)))
    return small.reshape(-1, ROWS_SMALL, D_MODEL)


def _unpack_small(buf):
    small = buf.reshape(-1)[:35 * 256].reshape(35, 256)
    out = {}
    off = 0
    for n, r in SHARD_SMALL:
        out[n] = small[off:off + r].reshape((r, 256) if r > 1 else (256,))
        off += r
    return out


EVEN_PACK = (("even_w_in", 0, 512), ("ssm_w_glu", 512, 128), ("even_w_out", 640, 256))
ROWS_EVEN = 1024
ODD_PACK = (("odd_w_in", 0, 768), ("odd_w_out", 768, 256))
ODD_SMALL_ROW = 1024
ROWS_ODD = 1280


def _pack_even_grads(g):
    parts = [g["even_w_in"].reshape(4, -1, D_MODEL), _cols_split(g["ssm_w_glu"], 256),
             g["even_w_out"].reshape(4, -1, D_MODEL), jnp.zeros((4, ROWS_EVEN - 896, D_MODEL), F32)]
    return jnp.concatenate(parts, axis=1)


def _pack_odd_grads(g):
    parts = [g["odd_w_in"].reshape(4, -1, D_MODEL), g["odd_w_out"].reshape(4, -1, D_MODEL), _pack_small(g),
             jnp.zeros((4, ROWS_ODD - ODD_SMALL_ROW - ROWS_SMALL, D_MODEL), F32)]
    return jnp.concatenate(parts, axis=1)


def kernel(x, even_norm, even_w_in, pool_w, pool_scale, ssm_log_dt, ssm_a_re, ssm_a_im, ssm_b_re, ssm_b_im, ssm_c_re, ssm_c_im, ssm_d, ssm_w_glu, even_w_out, odd_norm, odd_w_in, conv_w, conv_b, conv_ln_g, conv_ln_b, odd_w_out, final_norm, loss_target, m_even_norm, m_even_w_in, m_pool_w, m_pool_scale, m_ssm_log_dt, m_ssm_a_re, m_ssm_a_im, m_ssm_b_re, m_ssm_b_im, m_ssm_c_re, m_ssm_c_im, m_ssm_d, m_ssm_w_glu, m_even_w_out, m_odd_norm, m_odd_w_in, m_conv_w, m_conv_b, m_conv_ln_g, m_conv_ln_b, m_odd_w_out, m_final_norm, v_even_norm, v_even_w_in, v_pool_w, v_pool_scale, v_ssm_log_dt, v_ssm_a_re, v_ssm_a_im, v_ssm_b_re, v_ssm_b_im, v_ssm_c_re, v_ssm_c_im, v_ssm_d, v_ssm_w_glu, v_even_w_out, v_odd_norm, v_odd_w_in, v_conv_w, v_conv_b, v_conv_ln_g, v_conv_ln_b, v_odd_w_out, v_final_norm):
    ws = dict(zip(WEIGHT_NAMES, (even_norm, even_w_in, pool_w, pool_scale, ssm_log_dt, ssm_a_re, ssm_a_im, ssm_b_re,
                                 ssm_b_im, ssm_c_re, ssm_c_im, ssm_d, ssm_w_glu, even_w_out, odd_norm, odd_w_in,
                                 conv_w, conv_b, conv_ln_g, conv_ln_b, odd_w_out, final_norm)))
    ms = dict(zip(WEIGHT_NAMES, (m_even_norm, m_even_w_in, m_pool_w, m_pool_scale, m_ssm_log_dt, m_ssm_a_re,
                                 m_ssm_a_im, m_ssm_b_re, m_ssm_b_im, m_ssm_c_re, m_ssm_c_im, m_ssm_d, m_ssm_w_glu,
                                 m_even_w_out, m_odd_norm, m_odd_w_in, m_conv_w, m_conv_b, m_conv_ln_g, m_conv_ln_b,
                                 m_odd_w_out, m_final_norm)))
    vs = dict(zip(WEIGHT_NAMES, (v_even_norm, v_even_w_in, v_pool_w, v_pool_scale, v_ssm_log_dt, v_ssm_a_re,
                                 v_ssm_a_im, v_ssm_b_re, v_ssm_b_im, v_ssm_c_re, v_ssm_c_im, v_ssm_d, v_ssm_w_glu,
                                 v_even_w_out, v_odd_norm, v_odd_w_in, v_conv_w, v_conv_b, v_conv_ln_g, v_conv_ln_b,
                                 v_odd_w_out, v_final_norm)))
    lead = {n: a.shape for n, a in ws.items()}
    drop = lambda d: {n: (a[0] if n != "final_norm" else a) for n, a in d.items()}
    ws, ms, vs = drop(ws), drop(ms), drop(vs)

    shard = {n: ws[n] for n in SHARDED}
    rep = {n: ws[n] for n, _ in REP_NAMES}
    loss_part, grad_x, grads, got_odd, rep_parts = _local_step(x[0], loss_target[0], rep, shard)
    loss = lax.psum(loss_part, ("x", "y", "c"))

    odd_mine = _sum_lead(got_odd, "chip_sum_odd")
    odd_theirs = _pair_join(odd_mine, "pair_join_odd")
    g_even = _pack_even_grads(grads)
    got_even, late_parts = _scatter_and_gather(
        _pair_add(g_even, _pair_split(g_even, "pair_split_even"), BF16, "pair_add_even"),
        jnp.pad(grads["even_norm"], ((0, SUB - 1), (0, 0))), "scatter_even_gather_late")
    even_mine = _sum_lead(got_even, "chip_sum_even")
    even_theirs = _pair_join(even_mine, "pair_join_even")
    outs = [{}, {}, {}, {}]
    for pack, mine, theirs in ((EVEN_PACK, even_mine, even_theirs), (ODD_PACK, odd_mine, odd_theirs)):
        for n, row0, rows in pack:
            view = lambda a: a.reshape(rows, D_MODEL)
            res = _adamw_rows(view(ws[n]), view(ms[n]), view(vs[n]), mine, theirs, row0, 128, "adamw_" + n)
            for o, r in zip(outs, res):
                o[n] = r
    small = lambda d: _pack_small({n: d[n] for n, _ in SHARD_SMALL})[0]
    res = _adamw_rows(small(ws), small(ms), small(vs), odd_mine, odd_theirs, ODD_SMALL_ROW, ROWS_SMALL, "adamw_small")
    for o, r in zip(outs, res):
        o.update(_unpack_small(r))
    g_rep = lax.dynamic_update_slice(_sum_lead(rep_parts, "rep_sum"), _sum_lead(late_parts, "late_sum")[0:1], (0, 0))
    res = _adamw(_pack_rep(rep), g_rep, _pack_rep({n: ms[n] for n, _ in REP_NAMES}),
                 _pack_rep({n: vs[n] for n, _ in REP_NAMES}), "adamw_rep")
    for o, r in zip(outs, (g_rep,) + tuple(res)):
        o.update(_unpack_rep(r))

    leaves = [[o[n].reshape(lead[n]) for n in WEIGHT_NAMES] for o in outs]
    return (loss, grad_x[None], *leaves[0], *leaves[1], *leaves[2], *leaves[3])
```

```python
import jax
import jax.numpy as jnp
from jax import lax
from jax.experimental import pallas as pl
from jax.experimental.pallas import tpu as pltpu

F32 = jnp.float32
BF16 = jnp.bfloat16
MESH = pl.DeviceIdType.MESH

D_MODEL = 1024
RMS_EPS = 1e-6
LN_EPS = 1e-5
N_GROUPS = 32
GROUP_DIM = 16
N_STATE = 64
STATES = N_GROUPS * N_STATE
SSM_W = 512
POOL_W = 512
CONV_K = 31
HALO = 32
POOL_HALO = 16

ADAM_LR = 0.001
ADAM_B1 = 0.9
ADAM_B2 = 0.999
ADAM_EPS = 1e-08
ADAM_WD = 0.01
ADAM_STEP = 10

TM = 256
TM_MM = 512
SUB = 8
LCH = 512
SCAN_L = 1024
VMEM_LIMIT = 56 * 1024 * 1024

ROWS_SMALL = 16
REP_ROWS = 200
COMM_CHUNKS = 4


def _params(n_axes=1):
    return pltpu.CompilerParams(dimension_semantics=("arbitrary",) * n_axes, vmem_limit_bytes=VMEM_LIMIT)


def _rows(w, cb=0, rev=None, tm=TM):
    if rev is None:
        return pl.BlockSpec((tm, w), lambda i: (i, cb))
    return pl.BlockSpec((tm, w), lambda i: (rev - 1 - i, cb))


def _mm_rows(w, cb=0):
    return _rows(w, cb, tm=TM_MM)


def _full(shape):
    n = len(shape)
    return pl.BlockSpec(shape, lambda i: (0,) * n)


def _prev(hr, w, cb=0, tm=TM):
    r = tm // hr
    return pl.BlockSpec((hr, w), lambda i: (jnp.maximum(i * r - 1, 0), cb))


def _next(hr, w, nrows, cb=0, tm=TM):
    r = tm // hr
    last = nrows // hr - 1
    return pl.BlockSpec((hr, w), lambda i: (jnp.minimum((i + 1) * r, last), cb))


def _dot(a, b):
    return jnp.dot(a, b, preferred_element_type=F32)


def _dot_nt(a, b):
    return lax.dot_general(a, b, (((1,), (1,)), ((), ())), preferred_element_type=F32)


def _dot_tn(a, b):
    return lax.dot_general(a, b, (((0,), (0,)), ((), ())), preferred_element_type=F32)


def _sig(x):
    return 1.0 / (1.0 + jnp.exp(-x))


def _zero_at_first(i, *refs):
    @pl.when(i == 0)
    def _():
        for r in refs:
            r[...] = jnp.zeros_like(r)


def _norm_in(x, g, w, name, ride=(), pool=None):
    t, ns = x.shape[0], w.shape[2]
    n = 4 * ns
    ng = len(ride)
    npool = 0 if pool is None else 1
    nstep = t // TM_MM

    def body(x_ref, g_ref, w_ref, *rest):
        pool_in, rest = rest[:2 * npool], rest[2 * npool:]
        ride_in, o_ref, rest = rest[:ng], rest[ng], rest[ng + 1:]
        yp_ref, rest = (rest[0], rest[1:]) if npool else (None, rest)
        ride_out, rest = rest[:ng], rest[ng:]
        halo_ref, rest = (rest[0], rest[1:]) if npool else (None, rest)
        i = pl.program_id(0)
        if ng:
            start, forward, finish = _gather_phases(ride_in, ride_out, rest[:ng], *rest[ng:])
            pl.when(i == 0)(start)
            pl.when(i == nstep // 2)(forward)
        xv = x_ref[...]
        r = lax.rsqrt(jnp.mean(xv * xv, axis=-1, keepdims=True) + RMS_EPS)
        h = (xv * r * g_ref[...]).astype(BF16)
        first = _dot(h, w_ref[0]).astype(BF16)
        o_ref[:, 0:ns] = first
        if npool:
            wp_ref, ps_ref = pool_in
            _zero_at_first(i, halo_ref)
            pos = (i * TM_MM + 1 + lax.broadcasted_iota(jnp.int32, (TM_MM, 1), 0)).astype(F32)
            for gi in range(4):
                sl = slice(128 * gi, 128 * (gi + 1))
                u = first[:, sl].astype(F32)
                s = _pool_sums(jnp.concatenate([halo_ref[:, sl], u], axis=0), gi, True)[POOL_HALO:, :]
                pooled = s / jnp.minimum(pos, float(2 << gi)) - u
                yp_ref[:, sl] = _dot(pooled.astype(BF16), wp_ref[gi]) * ps_ref[:, sl]
                halo_ref[:, sl] = u[TM_MM - POOL_HALO:, :]
        for s in range(1, 4):
            o_ref[:, s * ns:(s + 1) * ns] = _dot(h, w_ref[s]).astype(BF16)
        if ng:
            pl.when(i == nstep - 1)(finish)

    pool_args = [] if pool is None else list(pool)
    res = pl.pallas_call(
        body, name=name, grid=(nstep,),
        in_specs=[_mm_rows(D_MODEL), _full((1, D_MODEL)), _full(w.shape)] + [_full(a.shape) for a in pool_args]
        + [ANY] * ng,
        out_specs=[_mm_rows(n)] + [_mm_rows(POOL_W)] * npool + [ANY] * ng,
        out_shape=[jax.ShapeDtypeStruct((t, n), BF16)] + [jax.ShapeDtypeStruct((t, POOL_W), F32)] * npool
        + [jax.ShapeDtypeStruct((4,) + a.shape, a.dtype) for a in ride],
        scratch_shapes=[pltpu.VMEM((POOL_HALO, POOL_W), F32)] * npool + (_gather_scratch(ride) if ng else []),
        compiler_params=_params())(x, g, w, *pool_args, *ride)
    return res[0], (res[1] if npool else None), res[1 + npool:]


def _pool_sums(ext, g, forward):
    n = ext.shape[0]
    s = ext
    for step in range(g + 1):
        k = 1 << step
        s = s + pltpu.roll(s, k if forward else n - k, 0)
    return s


SEG_LEN = TM // SUB


def _perm_matrix():
    p = jnp.arange(TM)
    src = (p % SUB) * SEG_LEN + p // SUB
    return (src[:, None] == jnp.arange(TM)[None, :]).astype(BF16)


def _cmul_add(are, aim, vre, vim, bre, bim):
    return are * vre - aim * vim + bre, are * vim + aim * vre + bim


def _segment_chain(ere, eim, qre, qim, cin_re, cin_im, row, up):
    for sh in (1, 2, 4):
        mre, mim = (qre[SUB - sh:SUB - sh + 1, :], qim[SUB - sh:SUB - sh + 1, :]) if up else \
                   (qre[sh - 1:sh, :], qim[sh - 1:sh, :])
        keep = (row < SUB - sh) if up else (row >= sh)
        sre = jnp.where(keep, pltpu.roll(ere, SUB - sh if up else sh, 0), 0.0)
        sim = jnp.where(keep, pltpu.roll(eim, SUB - sh if up else sh, 0), 0.0)
        ere, eim = _cmul_add(mre, mim, sre, sim, ere, eim)
    ere, eim = _cmul_add(qre, qim, cin_re, cin_im, ere, eim)
    keep = (row < SUB - 1) if up else (row >= 1)
    ent_re = jnp.where(keep, pltpu.roll(ere, SUB - 1 if up else 1, 0), cin_re)
    ent_im = jnp.where(keep, pltpu.roll(eim, SUB - 1 if up else 1, 0), cin_im)
    return ere, eim, ent_re, ent_im


def _scan_fwd_block(xs_re, xs_im, p8_re, p8_im, q_re, q_im, car_re, car_im, ent_re_ref, ent_im_ref):
    row = lax.broadcasted_iota(jnp.int32, (SUB, SCAN_L), 0)
    for j in range(STATES // SCAN_L):
        sl = slice(SCAN_L * j, SCAN_L * (j + 1))
        are, aim = p8_re[0:SUB, sl], p8_im[0:SUB, sl]

        def totals(i, v, sl=sl, are=are, aim=aim):
            r0 = pl.multiple_of(i * SUB, SUB)
            vre, vim = _cmul_add(are, aim, v[0], v[1], xs_re[pl.ds(r0, SUB), sl], xs_im[pl.ds(r0, SUB), sl])
            xs_re[pl.ds(r0, SUB), sl] = vre
            xs_im[pl.ds(r0, SUB), sl] = vim
            return vre, vim

        ere, eim = lax.fori_loop(1, SEG_LEN, totals, (xs_re[0:SUB, sl], xs_im[0:SUB, sl]), unroll=2)
        ere, eim, cre, cim = _segment_chain(ere, eim, q_re[:, sl], q_im[:, sl],
                                            car_re[:, sl], car_im[:, sl], row, False)
        car_re[:, sl] = jnp.broadcast_to(ere[SUB - 1:SUB, :], (SUB, SCAN_L))
        car_im[:, sl] = jnp.broadcast_to(eim[SUB - 1:SUB, :], (SUB, SCAN_L))
        if ent_re_ref is not None:
            ent_re_ref[:, sl] = cre
            ent_im_ref[:, sl] = cim

        def fix(i, c, sl=sl, cre=cre, cim=cim):
            r0 = pl.multiple_of(i * SUB, SUB)
            vre, vim = _cmul_add(p8_re[pl.ds(r0, SUB), sl], p8_im[pl.ds(r0, SUB), sl], cre, cim,
                                 xs_re[pl.ds(r0, SUB), sl], xs_im[pl.ds(r0, SUB), sl])
            xs_re[pl.ds(r0, SUB), sl] = vre
            xs_im[pl.ds(r0, SUB), sl] = vim
            return c

        lax.fori_loop(0, SEG_LEN, fix, 0, unroll=2)


def _unpermute(pmt_ref, v):
    hi = v.astype(BF16)
    lo = (v - hi.astype(F32)).astype(BF16)
    return _dot(pmt_ref[...], hi) + _dot(pmt_ref[...], lo)


def _ssm_fwd(proj, pm, pmt, mb_re, mb_im, p8_re, p8_im, q_re, q_im, cm_re, cm_im, dskip, wglu, ride=()):
    t = proj.shape[0]
    nblk = t // TM

    ng = len(ride)

    def body(u_ref, pm_ref, pmt_ref, mbre, mbim, p8re, p8im, qre, qim, cmre, cmim, d_ref, wg_ref, *rest):
        ride_in, (y_ref, cre_ref, cim_ref), ride_out = rest[:ng], rest[ng:ng + 3], rest[ng + 3:2 * ng + 3]
        xs_re, xs_im, car_re, car_im, ysk = rest[2 * ng + 3:2 * ng + 8]
        i = pl.program_id(0)
        if ng:
            start, forward, finish = _gather_phases(ride_in, ride_out, rest[2 * ng + 8:3 * ng + 8],
                                                    *rest[3 * ng + 8:])
            pl.when(i == 0)(start)
            pl.when(i == nblk // 2)(forward)
        _zero_at_first(i, car_re, car_im)
        cre_ref[0] = car_re[...]
        cim_ref[0] = car_im[...]
        us = _dot(pm_ref[...], u_ref[...])
        usb = us.astype(BF16)
        for j in range(4):
            xs_re[:, LCH * j:LCH * (j + 1)] = _dot(usb[:, 128 * j:128 * (j + 1)], mbre[j])
            xs_im[:, LCH * j:LCH * (j + 1)] = _dot(usb[:, 128 * j:128 * (j + 1)], mbim[j])
        _scan_fwd_block(xs_re, xs_im, p8re, p8im, qre, qim, car_re, car_im, None, None)
        for j in range(4):
            sl = slice(LCH * j, LCH * (j + 1))
            ysk[:, 128 * j:128 * (j + 1)] = (_dot_nt(xs_re[:, sl].astype(BF16), cmre[j])
                                             - _dot_nt(xs_im[:, sl].astype(BF16), cmim[j]))
        yv = ysk[...] + d_ref[...] * us
        gv = _dot(yv.astype(BF16), wg_ref[...])
        y_ref[...] = _unpermute(pmt_ref, gv[:, :SSM_W] * _sig(gv[:, SSM_W:]))
        if ng:
            pl.when(i == nblk - 1)(finish)

    blk = (4, 128, LCH)
    res = pl.pallas_call(
        body, name="ssm_fwd", grid=(nblk,),
        in_specs=[_rows(SSM_W, 1), _full((TM, TM)), _full((TM, TM)), _full(blk), _full(blk),
                  _full((TM, STATES)), _full((TM, STATES)), _full((SUB, STATES)), _full((SUB, STATES)),
                  _full(blk), _full(blk), _full((1, SSM_W)), _full((SSM_W, 2 * SSM_W))] + [ANY] * ng,
        out_specs=[_rows(SSM_W), pl.BlockSpec((1, SUB, STATES), lambda i: (i, 0, 0)),
                   pl.BlockSpec((1, SUB, STATES), lambda i: (i, 0, 0))] + [ANY] * ng,
        out_shape=[jax.ShapeDtypeStruct((t, SSM_W), F32), jax.ShapeDtypeStruct((nblk, SUB, STATES), F32),
                   jax.ShapeDtypeStruct((nblk, SUB, STATES), F32)]
        + [jax.ShapeDtypeStruct((4,) + a.shape, a.dtype) for a in ride],
        scratch_shapes=[pltpu.VMEM((TM, STATES), F32), pltpu.VMEM((TM, STATES), F32),
                        pltpu.VMEM((SUB, STATES), F32), pltpu.VMEM((SUB, STATES), F32),
                        pltpu.VMEM((TM, SSM_W), F32)] + (_gather_scratch(ride) if ng else []),
        compiler_params=_params())(proj, pm, pmt, mb_re, mb_im, p8_re, p8_im, q_re, q_im, cm_re, cm_im, dskip, wglu,
                                   *ride)
    return res[:3], res[3:]


def _even_out(yp, ys, proj, x, w):
    t = x.shape[0]

    def body(yp_ref, ys_ref, z_ref, x_ref, w_ref, x1_ref, yg_ref):
        z = z_ref[...].astype(F32)
        sz = z * _sig(z)
        gp = (yp_ref[...] * sz[:, :POOL_W]).astype(BF16)
        gs = (ys_ref[...] * sz[:, POOL_W:]).astype(BF16)
        yg_ref[:, :POOL_W] = gp
        yg_ref[:, POOL_W:] = gs
        x1_ref[...] = x_ref[...] + _dot(gp, w_ref[:POOL_W, :]) + _dot(gs, w_ref[POOL_W:, :])

    return pl.pallas_call(
        body, name="even_out", grid=(t // TM_MM,),
        in_specs=[_mm_rows(POOL_W), _mm_rows(SSM_W), _mm_rows(D_MODEL, 1), _mm_rows(D_MODEL),
                  _full((D_MODEL, D_MODEL))],
        out_specs=[_mm_rows(D_MODEL), _mm_rows(D_MODEL)],
        out_shape=[jax.ShapeDtypeStruct((t, D_MODEL), F32), jax.ShapeDtypeStruct((t, D_MODEL), BF16)],
        compiler_params=_params())(yp, ys, proj, x, w)


def _phase_copies(ext, cp):
    n = cp.shape[1]
    for j in range(1, SUB):
        cp[j - 1] = ext[pl.ds(j, n), :]


def _shifted(ext, cp, off, nrows, sl, row0=0):
    q, j = divmod(off, SUB)
    if j == 0:
        return ext[pl.ds(row0 + SUB * q, nrows), sl]
    return cp[j - 1, pl.ds(row0 + SUB * q, nrows), sl]


def _conv_taps(ext, cp, w_ref, first, nrows, sl, init, row0=0):
    acc = init
    for k in range(CONV_K):
        acc = acc + w_ref[k:k + 1, sl] * _shifted(ext, cp, first(k), nrows, sl, row0)
    return acc


def _conv_fwd(q, cw, cb, lg, lb):
    t = q.shape[0]

    def body(v_ref, g_ref, hv_ref, hg_ref, z_ref, w_ref, b_ref, lg_ref, lb_ref, y_ref, cv_ref, ext, cp):
        i = pl.program_id(0)
        ext[0:HALO, :] = jnp.where(i == 0, 0.0, hv_ref[...].astype(F32) * _sig(hg_ref[...].astype(F32)))
        ext[HALO:, :] = v_ref[...].astype(F32) * _sig(g_ref[...].astype(F32))
        _phase_copies(ext, cp)

        def lanes(c, carry):
            sl = pl.ds(pl.multiple_of(c * 128, 128), 128)
            cv_ref[:, sl] = _conv_taps(ext, cp, w_ref, lambda k: k + 2, TM, sl,
                                       jnp.broadcast_to(b_ref[:, sl], (TM, 128)))
            return carry

        lax.fori_loop(0, D_MODEL // 128, lanes, 0)
        cv = cv_ref[...]
        cc = cv - jnp.mean(cv, axis=-1, keepdims=True)
        rstd = lax.rsqrt(jnp.mean(cc * cc, axis=-1, keepdims=True) + LN_EPS)
        cl = cc * rstd * lg_ref[...] + lb_ref[...]
        z = z_ref[...].astype(F32)
        y_ref[...] = (cl * _sig(cl) * z * _sig(z)).astype(BF16)

    vec = _full((1, D_MODEL))
    return pl.pallas_call(
        body, name="conv_fwd", grid=(t // TM,),
        in_specs=[_rows(D_MODEL, 0), _rows(D_MODEL, 1), _prev(HALO, D_MODEL, 0), _prev(HALO, D_MODEL, 1),
                  _rows(D_MODEL, 2), _full((HALO, D_MODEL)), vec, vec, vec],
        out_specs=[_rows(D_MODEL), _rows(D_MODEL)],
        out_shape=[jax.ShapeDtypeStruct((t, D_MODEL), BF16), jax.ShapeDtypeStruct((t, D_MODEL), F32)],
        scratch_shapes=[pltpu.VMEM((TM + HALO, D_MODEL), F32),
                        pltpu.VMEM((SUB - 1, TM + HALO - SUB, D_MODEL), F32)],
        compiler_params=_params())(q, q, q, q, q, cw, cb, lg, lb)


def _odd_out_loss(y2, x1, w, gf, tgt):
    t = x1.shape[0]

    def body(y_ref, x_ref, w_ref, g_ref, t_ref, dx_ref, loss_ref, dg_ref):
        i = pl.program_id(0)
        _zero_at_first(i, loss_ref, dg_ref)
        x2 = x_ref[...] + _dot(y_ref[...], w_ref[...])
        r = lax.rsqrt(jnp.mean(x2 * x2, axis=-1, keepdims=True) + RMS_EPS)
        n = x2 * r
        e = n * g_ref[...] - t_ref[...]
        loss_ref[...] += jnp.sum(e * e, axis=0, keepdims=True) * (0.5 / D_MODEL)
        dout = e * (1.0 / D_MODEL)
        dg_ref[...] += jnp.sum(dout * n, axis=0, keepdims=True)
        dn = dout * g_ref[...]
        dx_ref[...] = r * (dn - n * jnp.mean(dn * n, axis=-1, keepdims=True))

    vec = _full((1, D_MODEL))
    return pl.pallas_call(
        body, name="odd_out_loss", grid=(t // TM_MM,),
        in_specs=[_mm_rows(D_MODEL), _mm_rows(D_MODEL), _full((D_MODEL, D_MODEL)), vec, _mm_rows(D_MODEL)],
        out_specs=[_mm_rows(D_MODEL), vec, vec],
        out_shape=[jax.ShapeDtypeStruct((t, D_MODEL), F32), jax.ShapeDtypeStruct((1, D_MODEL), F32),
                   jax.ShapeDtypeStruct((1, D_MODEL), F32)],
        compiler_params=_params())(y2, x1, w, gf, tgt)


def _dsilu(z):
    s = _sig(z)
    return z * s, s * (1.0 + z * (1.0 - s))


def _odd_bwd_out(dx2, w, y2, cv, q, lg, lb):
    t = dx2.shape[0]

    def body(dx_ref, w_ref, y_ref, cv_ref, z_ref, lg_ref, lb_ref, dcv_ref, dz_ref, dw_ref, dlg_ref, dlb_ref):
        i = pl.program_id(0)
        _zero_at_first(i, dw_ref, dlg_ref, dlb_ref)
        dxb = dx_ref[...].astype(BF16)
        dy = _dot_nt(dxb, w_ref[...])
        dw_ref[...] += _dot_tn(y_ref[...], dxb)
        cv = cv_ref[...]
        cc = cv - jnp.mean(cv, axis=-1, keepdims=True)
        rstd = lax.rsqrt(jnp.mean(cc * cc, axis=-1, keepdims=True) + LN_EPS)
        cn = cc * rstd
        silu_c, dsilu_c = _dsilu(cn * lg_ref[...] + lb_ref[...])
        silu_z, dsilu_z = _dsilu(z_ref[...].astype(F32))
        dcl = dy * silu_z * dsilu_c
        dz_ref[...] = (dy * silu_c * dsilu_z).astype(BF16)
        dlg_ref[...] += jnp.sum(dcl * cn, axis=0, keepdims=True)
        dlb_ref[...] += jnp.sum(dcl, axis=0, keepdims=True)
        dcn = dcl * lg_ref[...]
        dcv_ref[...] = rstd * (dcn - jnp.mean(dcn, axis=-1, keepdims=True)
                               - cn * jnp.mean(dcn * cn, axis=-1, keepdims=True))

    vec = _full((1, D_MODEL))
    mat = _full((D_MODEL, D_MODEL))
    return pl.pallas_call(
        body, name="odd_bwd_out", grid=(t // TM_MM,),
        in_specs=[_mm_rows(D_MODEL), mat, _mm_rows(D_MODEL), _mm_rows(D_MODEL), _mm_rows(D_MODEL, 2), vec, vec],
        out_specs=[_mm_rows(D_MODEL), _mm_rows(D_MODEL), mat, vec, vec],
        out_shape=[jax.ShapeDtypeStruct((t, D_MODEL), F32), jax.ShapeDtypeStruct((t, D_MODEL), BF16),
                   jax.ShapeDtypeStruct((D_MODEL, D_MODEL), F32), jax.ShapeDtypeStruct((1, D_MODEL), F32),
                   jax.ShapeDtypeStruct((1, D_MODEL), F32)],
        compiler_params=_params())(dx2, w, y2, cv, q, lg, lb)


def _conv_bwd(dcv, q, cw):
    t = dcv.shape[0]
    nblk = t // TM

    def body(d_ref, dn_ref, v_ref, g_ref, hv_ref, hg_ref, w_ref,
             dv_ref, dgt_ref, dw_ref, db_ref, gext, dext, dgl, gcp, dcp):
        i = pl.program_id(0)
        last = nblk - 1
        _zero_at_first(i, dw_ref, db_ref)
        v = v_ref[...].astype(F32)
        sg = _sig(g_ref[...].astype(F32))
        gext[0:HALO, :] = jnp.where(i == 0, 0.0, hv_ref[...].astype(F32) * _sig(hg_ref[...].astype(F32)))
        gext[HALO:, :] = v * sg
        d = d_ref[...]
        dext[0:TM, :] = d
        dext[TM:, :] = jnp.where(i == last, 0.0, dn_ref[...])
        _phase_copies(gext, gcp)
        _phase_copies(dext, dcp)
        db_ref[...] += jnp.sum(d, axis=0, keepdims=True)
        def lanes(c, carry):
            sl = pl.ds(pl.multiple_of(c * 128, 128), 128)
            dgl[:, sl] = _conv_taps(dext, dcp, w_ref, lambda k: 30 - k, TM, sl, jnp.zeros((TM, 128), F32))
            return carry

        def lanes_w(c, carry):
            sl = pl.ds(pl.multiple_of(c * 128, 128), 128)
            ntile = TM // SUB
            dts = [d_ref[SUB * r:SUB * (r + 1), sl] for r in range(ntile)]
            for j in range(SUB):
                taps = [(q, SUB * q + j - 2) for q in range(5) if 0 <= SUB * q + j - 2 < CONV_K]
                sums = {k: None for _, k in taps}
                for rt in range(ntile + 4):
                    need = [(q, k) for q, k in taps if 0 <= rt - q < ntile]
                    if not need:
                        continue
                    src = gext[SUB * rt:SUB * (rt + 1), sl] if j == 0 else gcp[j - 1, SUB * rt:SUB * (rt + 1), sl]
                    for q, k in need:
                        prod = dts[rt - q] * src
                        sums[k] = prod if sums[k] is None else sums[k] + prod
                for _, k in taps:
                    dw_ref[SUB * k:SUB * (k + 1), sl] += sums[k]
            return carry

        lax.fori_loop(0, D_MODEL // 128, lanes, 0)
        lax.fori_loop(0, D_MODEL // 128, lanes_w, 0)
        dg = dgl[...]
        dv_ref[...] = (dg * sg).astype(BF16)
        dgt_ref[...] = (dg * v * sg * (1.0 - sg)).astype(BF16)

    return pl.pallas_call(
        body, name="conv_bwd", grid=(t // TM,),
        in_specs=[_rows(D_MODEL), _next(HALO, D_MODEL, t), _rows(D_MODEL, 0), _rows(D_MODEL, 1),
                  _prev(HALO, D_MODEL, 0), _prev(HALO, D_MODEL, 1), _full((HALO, D_MODEL))],
        out_specs=[_rows(D_MODEL), _rows(D_MODEL), _full((HALO * SUB, D_MODEL)), _full((1, D_MODEL))],
        out_shape=[jax.ShapeDtypeStruct((t, D_MODEL), BF16), jax.ShapeDtypeStruct((t, D_MODEL), BF16),
                   jax.ShapeDtypeStruct((HALO * SUB, D_MODEL), F32), jax.ShapeDtypeStruct((1, D_MODEL), F32)],
        scratch_shapes=[pltpu.VMEM((TM + HALO, D_MODEL), F32), pltpu.VMEM((TM + HALO, D_MODEL), F32),
                        pltpu.VMEM((TM, D_MODEL), F32),
                        pltpu.VMEM((SUB - 1, TM + HALO - SUB, D_MODEL), F32),
                        pltpu.VMEM((SUB - 1, TM + HALO - SUB, D_MODEL), F32)],
        compiler_params=_params())(dcv, dcv, q, q, q, q, cw)


def _column_segments(widths, ns):
    segs = []
    col = 0
    for p, wd in enumerate(widths):
        a = 0
        while a < wd:
            s, lo = divmod(col + a, ns)
            ln = min(wd - a, ns - lo)
            segs.append((p, a, a + ln, s, lo, lo + ln))
            a += ln
        col += wd
    return segs


def _in_bwd(dparts, w, x, g, dres, name, ride=None):
    t = x.shape[0]
    widths = [p.shape[1] for p in dparts]
    npart = len(dparts)
    segs = _column_segments(widths, w.shape[2])
    nstep = t // TM_MM

    def body(*refs):
        d_refs = refs[:npart]
        w_ref, x_ref, g_ref, r_ref = refs[npart:npart + 4]
        if ride is None:
            dx_ref, dg_ref, dw_ref = refs[npart + 4:]
        else:
            v_ref, j_ref, dx_ref, dg_ref, dw_ref, o_ref, jo_ref = refs[npart + 4:npart + 11]
            start, forward, finish = _gather_all_phases(v_ref, o_ref, *refs[npart + 11:npart + 15])
            j_start, j_finish = _pair_join_phases(j_ref, jo_ref, *refs[npart + 15:])
        i = pl.program_id(0)
        if ride is not None:
            pl.when(i == 0)(start)
            pl.when(i == 0)(j_start)
            pl.when(i == nstep // 2)(forward)
        _zero_at_first(i, dg_ref, dw_ref)
        xv = x_ref[...]
        r = lax.rsqrt(jnp.mean(xv * xv, axis=-1, keepdims=True) + RMS_EPS)
        n = xv * r
        h = (n * g_ref[...]).astype(BF16)
        dh = None
        for p, lo, hi, s, slo, shi in segs:
            d = d_refs[p][:, lo:hi]
            part = _dot_nt(d, w_ref[s, :, slo:shi])
            dh = part if dh is None else dh + part
            dw_ref[s, :, slo:shi] += _dot_tn(h, d)
        dg_ref[...] += jnp.sum(dh * n, axis=0, keepdims=True)
        dn = dh * g_ref[...]
        dx_ref[...] = r_ref[...] + r * (dn - n * jnp.mean(dn * n, axis=-1, keepdims=True))
        if ride is not None:
            pl.when(i == nstep - 1)(finish)
            pl.when(i == nstep - 1)(j_finish)

    vec = _full((1, D_MODEL))
    once = pl.BlockSpec(w.shape, lambda i: (0, 0, 0), pipeline_mode=pl.Buffered(1))
    extra = [] if ride is None else list(ride)
    return pl.pallas_call(
        body, name=name, grid=(nstep,),
        in_specs=[_mm_rows(wd) for wd in widths] + [once, _mm_rows(D_MODEL), vec, _mm_rows(D_MODEL)]
        + [ANY] * len(extra),
        out_specs=[_mm_rows(D_MODEL), vec, once] + [ANY] * len(extra),
        out_shape=[jax.ShapeDtypeStruct((t, D_MODEL), F32), jax.ShapeDtypeStruct((1, D_MODEL), F32),
                   jax.ShapeDtypeStruct(w.shape, F32)]
        + ([jax.ShapeDtypeStruct((8,) + ride[0].shape, ride[0].dtype),
            jax.ShapeDtypeStruct(ride[1].shape, ride[1].dtype)] if extra else []),
        scratch_shapes=(_gather_all_scratch(ride[0]) + _pair_join_scratch()) if extra else [],
        compiler_params=_params())(*dparts, w, x, g, dres, *extra)


def _even_bwd_out(dx1, w, yg, yp, ys, proj, wp, ps, ride):
    t = dx1.shape[0]
    nstep = t // TM_MM
    rows = TM_MM

    def body(dx_ref, w_ref, yg_ref, yp_ref, ys_ref, z_ref, u_ref, h_ref, wp_ref, ps_ref, g_ref,
             dy_ref, dz_ref, dw_ref, du_ref, dwp_ref, dps_ref, theirs_ref, nxt, send_sems, recv_sems):
        i = pl.program_id(0)
        blk = nstep - 1 - i
        start, finish = _pair_split_phases(g_ref, theirs_ref, send_sems, recv_sems)
        pl.when(i == 0)(start)
        _zero_at_first(i, dw_ref, dwp_ref, dps_ref, nxt)
        dxb = dx_ref[...].astype(BF16)
        dyg = _dot_nt(dxb, w_ref[...])
        dw_ref[...] += _dot_tn(yg_ref[...], dxb)
        silu_z, dsilu_z = _dsilu(z_ref[...].astype(F32))
        dyb = (dyg * silu_z).astype(BF16)
        dy_ref[...] = dyb
        dz_ref[:, :POOL_W] = (dyg[:, :POOL_W] * yp_ref[...] * dsilu_z[:, :POOL_W]).astype(BF16)
        dz_ref[:, POOL_W:] = (dyg[:, POOL_W:] * ys_ref[...] * dsilu_z[:, POOL_W:]).astype(BF16)
        pos = (blk * rows + 1 + lax.broadcasted_iota(jnp.int32, (rows, 1), 0)).astype(F32)
        pos_ext = (blk * rows + 1 + lax.broadcasted_iota(jnp.int32, (rows + POOL_HALO, 1), 0)).astype(F32)
        for gi in range(4):
            sl = slice(128 * gi, 128 * (gi + 1))
            wd = float(2 << gi)
            u = u_ref[:, sl].astype(F32)
            halo = jnp.where(blk == 0, 0.0, h_ref[:, sl].astype(F32))
            s = _pool_sums(jnp.concatenate([halo, u], axis=0), gi, True)[POOL_HALO:, :]
            pooled = (s / jnp.minimum(pos, wd) - u).astype(BF16)
            dy = dyb[:, sl].astype(F32)
            dps_ref[:, sl] += jnp.sum(dy * _dot(pooled, wp_ref[gi]), axis=0, keepdims=True)
            dmix = (jnp.concatenate([dy, nxt[:, sl]], axis=0) * ps_ref[:, sl]).astype(BF16)
            dwp_ref[gi] += _dot_tn(pooled, dmix[:rows, :])
            dpool = _dot_nt(dmix, wp_ref[gi])
            lead = _pool_sums(dpool / jnp.minimum(pos_ext, wd), gi, False)
            du_ref[:, sl] = (lead[:rows, :] - dpool[:rows, :]).astype(BF16)
            nxt[:, sl] = dy[:POOL_HALO, :]
        pl.when(i == nstep - 1)(finish)

    back = lambda wdt, cb=0: _rows(wdt, cb, rev=nstep, tm=rows)
    per = rows // POOL_HALO
    halo = pl.BlockSpec((POOL_HALO, POOL_W), lambda i: (jnp.maximum((nstep - 1 - i) * per - 1, 0), 0))
    mat = _full((D_MODEL, D_MODEL))
    return pl.pallas_call(
        body, name="even_bwd_out", grid=(nstep,),
        in_specs=[back(D_MODEL), mat, back(D_MODEL), back(POOL_W), back(SSM_W), back(D_MODEL, 1),
                  back(POOL_W, 0), halo, _full((4, 128, 128)), _full((1, POOL_W)), ANY],
        out_specs=[back(D_MODEL), back(D_MODEL), mat, back(POOL_W), _full((4, 128, 128)), _full((1, POOL_W)), ANY],
        out_shape=[jax.ShapeDtypeStruct((t, D_MODEL), BF16), jax.ShapeDtypeStruct((t, D_MODEL), BF16),
                   jax.ShapeDtypeStruct((D_MODEL, D_MODEL), F32), jax.ShapeDtypeStruct((t, POOL_W), BF16),
                   jax.ShapeDtypeStruct((4, 128, 128), F32), jax.ShapeDtypeStruct((1, POOL_W), F32),
                   _pair_split_shape(ride)],
        scratch_shapes=[pltpu.VMEM((POOL_HALO, POOL_W), F32)] + _pair_split_scratch(ride),
        compiler_params=_params())(dx1, w, yg, yp, ys, proj, proj, proj, wp, ps, ride)


def _ssm_bwd(dycat, proj, car_in_re, car_in_im, pm, pmt, mb_re, mb_im, p8_re, p8_im, q_re, q_im, qr_re, qr_im,
             cm_re, cm_im, dskip, wglu, ride):
    t = proj.shape[0]
    nblk = t // TM

    def body(dy_ref, u_ref, cin_re, cin_im, pm_ref, pmt_ref, mbre, mbim, p8re, p8im, qre, qim, qrre, qrim,
             cmre, cmim, d_ref, wg_ref, p_ref,
             du_ref, dmbre, dmbim, dcmre, dcmim, dare, daim, dd_ref, dwg_ref, got_ref,
             xs_re, xs_im, gs_re, gs_im, car_re, car_im, ent_re, ent_im, gcar_re, gcar_im, ysk, dysk,
             bounce, send_sems, recv_sems, local_sems):
        i = pl.program_id(0)
        start, finish = _chip_scatter_phases(p_ref, got_ref, bounce, send_sems, recv_sems, local_sems)
        pl.when(i == 0)(start)
        _zero_at_first(i, dmbre, dmbim, dcmre, dcmim, dare, daim, dd_ref, dwg_ref, gcar_re, gcar_im)
        us = _dot(pm_ref[...], u_ref[...])
        usb = us.astype(BF16)
        for j in range(4):
            xs_re[:, LCH * j:LCH * (j + 1)] = _dot(usb[:, 128 * j:128 * (j + 1)], mbre[j])
            xs_im[:, LCH * j:LCH * (j + 1)] = _dot(usb[:, 128 * j:128 * (j + 1)], mbim[j])
        car_re[...] = cin_re[0]
        car_im[...] = cin_im[0]
        _scan_fwd_block(xs_re, xs_im, p8re, p8im, qre, qim, car_re, car_im, ent_re, ent_im)
        for j in range(4):
            sl = slice(LCH * j, LCH * (j + 1))
            ysk[:, 128 * j:128 * (j + 1)] = (_dot_nt(xs_re[:, sl].astype(BF16), cmre[j])
                                             - _dot_nt(xs_im[:, sl].astype(BF16), cmim[j]))
        yvb = (ysk[...] + d_ref[...] * us).astype(BF16)
        gv = _dot(yvb, wg_ref[...])
        sg = _sig(gv[:, SSM_W:])
        dyss = _dot(pm_ref[...], dy_ref[...])
        dval = (dyss * sg).astype(BF16)
        dgate = (dyss * gv[:, :SSM_W] * sg * (1.0 - sg)).astype(BF16)
        dy = _dot_nt(dval, wg_ref[:, :SSM_W]) + _dot_nt(dgate, wg_ref[:, SSM_W:])
        dwg_ref[:, :SSM_W] += _dot_tn(yvb, dval)
        dwg_ref[:, SSM_W:] += _dot_tn(yvb, dgate)
        dd_ref[...] += jnp.sum(dy * us, axis=0, keepdims=True)
        dysk[...] = dy
        for j in range(4):
            sl = slice(LCH * j, LCH * (j + 1))
            dyj = dy[:, 128 * j:128 * (j + 1)].astype(BF16)
            gs_re[:, sl] = _dot(dyj, cmre[j])
            gs_im[:, sl] = -_dot(dyj, cmim[j])
            dcmre[j] += _dot_tn(dyj, xs_re[:, sl].astype(BF16))
            dcmim[j] -= _dot_tn(dyj, xs_im[:, sl].astype(BF16))
        row = lax.broadcasted_iota(jnp.int32, (SUB, SCAN_L), 0)
        for j in range(STATES // SCAN_L):
            sl = slice(SCAN_L * j, SCAN_L * (j + 1))
            are, aim = p8re[0:SUB, sl], -p8im[0:SUB, sl]

            def totals(k, v, sl=sl, are=are, aim=aim):
                r0 = pl.multiple_of((SEG_LEN - 2 - k) * SUB, SUB)
                vre, vim = _cmul_add(are, aim, v[0], v[1], gs_re[pl.ds(r0, SUB), sl], gs_im[pl.ds(r0, SUB), sl])
                gs_re[pl.ds(r0, SUB), sl] = vre
                gs_im[pl.ds(r0, SUB), sl] = vim
                return vre, vim

            top = (SEG_LEN - 1) * SUB
            fre, fim = lax.fori_loop(0, SEG_LEN - 1, totals,
                                     (gs_re[top:top + SUB, sl], gs_im[top:top + SUB, sl]), unroll=2)
            fre, fim, nre, nim = _segment_chain(fre, fim, qrre[:, sl], -qrim[:, sl],
                                                gcar_re[:, sl], gcar_im[:, sl], row, True)
            gcar_re[:, sl] = jnp.broadcast_to(fre[0:1, :], (SUB, SCAN_L))
            gcar_im[:, sl] = jnp.broadcast_to(fim[0:1, :], (SUB, SCAN_L))

            def fix(i2, acc, sl=sl, nre=nre, nim=nim):
                r0 = pl.multiple_of(i2 * SUB, SUB)
                rb = pl.multiple_of((SEG_LEN - 1 - i2) * SUB, SUB)
                gre, gim = _cmul_add(p8re[pl.ds(rb, SUB), sl], -p8im[pl.ds(rb, SUB), sl], nre, nim,
                                     gs_re[pl.ds(r0, SUB), sl], gs_im[pl.ds(r0, SUB), sl])
                gs_re[pl.ds(r0, SUB), sl] = gre
                gs_im[pl.ds(r0, SUB), sl] = gim
                rp = pl.multiple_of((i2 - 1) * SUB, SUB)
                xre, xim = xs_re[pl.ds(rp, SUB), sl], xs_im[pl.ds(rp, SUB), sl]
                return acc[0] + gre * xre + gim * xim, acc[1] + gim * xre - gre * xim

            g0re, g0im = _cmul_add(p8re[top:top + SUB, sl], -p8im[top:top + SUB, sl], nre, nim,
                                   gs_re[0:SUB, sl], gs_im[0:SUB, sl])
            gs_re[0:SUB, sl] = g0re
            gs_im[0:SUB, sl] = g0im
            ere, eim = ent_re[:, sl], ent_im[:, sl]
            acc0 = (dare[:, sl] + g0re * ere + g0im * eim, daim[:, sl] + g0im * ere - g0re * eim)
            are_acc, aim_acc = lax.fori_loop(1, SEG_LEN, fix, acc0, unroll=2)
            dare[:, sl] = are_acc
            daim[:, sl] = aim_acc
        for j in range(4):
            sl = slice(LCH * j, LCH * (j + 1))
            c4 = slice(128 * j, 128 * (j + 1))
            gre = gs_re[:, sl].astype(BF16)
            gim = gs_im[:, sl].astype(BF16)
            dmbre[j] += _dot_tn(usb[:, c4], gre)
            dmbim[j] += _dot_tn(usb[:, c4], gim)
            dysk[:, c4] = _dot_nt(gre, mbre[j]) + _dot_nt(gim, mbim[j]) + dysk[:, c4] * d_ref[:, c4]
        du_ref[...] = _dot(pmt_ref[...], dysk[...].astype(BF16)).astype(BF16)
        pl.when(i == nblk - 1)(finish)

    blk = (4, 128, LCH)
    pw = _full((SUB, STATES))
    p8 = _full((TM, STATES))
    perm = _full((TM, TM))
    car = pl.BlockSpec((1, SUB, STATES), lambda i: (nblk - 1 - i, 0, 0))
    big = lambda: pltpu.VMEM((TM, STATES), F32)
    small = lambda: pltpu.VMEM((SUB, STATES), F32)
    return pl.pallas_call(
        body, name="ssm_bwd", grid=(nblk,),
        in_specs=[_rows(SSM_W, 1, rev=nblk), _rows(SSM_W, 1, rev=nblk), car, car, perm, perm, _full(blk), _full(blk),
                  p8, p8, pw, pw, pw, pw, _full(blk), _full(blk), _full((1, SSM_W)), _full((SSM_W, 2 * SSM_W)), ANY],
        out_specs=[_rows(SSM_W, 0, rev=nblk), _full(blk), _full(blk), _full(blk), _full(blk), pw, pw,
                   _full((1, SSM_W)), _full((SSM_W, 2 * SSM_W)), ANY],
        out_shape=[jax.ShapeDtypeStruct((t, SSM_W), BF16)] + [jax.ShapeDtypeStruct(blk, F32)] * 4
        + [jax.ShapeDtypeStruct((SUB, STATES), F32)] * 2
        + [jax.ShapeDtypeStruct((1, SSM_W), F32), jax.ShapeDtypeStruct((SSM_W, 2 * SSM_W), F32),
           jax.ShapeDtypeStruct(ride.shape, ride.dtype)],
        scratch_shapes=[big(), big(), big(), big(), small(), small(), small(), small(), small(), small(),
                        pltpu.VMEM((TM, SSM_W), F32), pltpu.VMEM((TM, SSM_W), F32)] + _chip_scatter_scratch(ride),
        compiler_params=_params())(dycat, proj, car_in_re, car_in_im, pm, pmt, mb_re, mb_im, p8_re, p8_im,
                                   q_re, q_im, qr_re, qr_im, cm_re, cm_im, dskip, wglu, ride)


def _adamw(w, g, m, v, name):
    rows = w.shape[0]
    tr = 256 if rows % 256 == 0 else rows
    c1 = 1.0 / (1.0 - ADAM_B1 ** ADAM_STEP)
    c2 = 1.0 / (1.0 - ADAM_B2 ** ADAM_STEP)

    def body(w_ref, g_ref, m_ref, v_ref, d_ref, nm_ref, nv_ref):
        gv = g_ref[...]
        m = ADAM_B1 * m_ref[...] + (1.0 - ADAM_B1) * gv
        v = ADAM_B2 * v_ref[...] + (1.0 - ADAM_B2) * (gv * gv)
        nm_ref[...] = m
        nv_ref[...] = v
        d_ref[...] = -ADAM_LR * ((m * c1) / (jnp.sqrt(v * c2) + ADAM_EPS) + ADAM_WD * w_ref[...])

    spec = pl.BlockSpec((tr, D_MODEL), lambda i: (i, 0))
    shp = jax.ShapeDtypeStruct((rows, D_MODEL), F32)
    return pl.pallas_call(
        body, name=name, grid=(rows // tr,), in_specs=[spec] * 4, out_specs=[spec] * 3, out_shape=[shp] * 3,
        compiler_params=_params())(w, g, m, v)


def _core_index():
    return lax.axis_index("c").astype(jnp.int32).reshape(1)


def _pair_add(g, theirs, out_dtype, name):
    n, half, _ = theirs.shape
    br = 128
    nb = half // br

    def body(c_ref, a_ref, b_ref, o_ref):
        o_ref[...] = (a_ref[...] + b_ref[...]).astype(out_dtype)

    spec = pl.BlockSpec((1, br, D_MODEL), lambda i, j, c: (i, j, 0))
    grid_spec = pltpu.PrefetchScalarGridSpec(
        num_scalar_prefetch=1, grid=(n, nb),
        in_specs=[pl.BlockSpec((1, br, D_MODEL), lambda i, j, c: (i, c[0] * nb + j, 0)), spec], out_specs=spec)
    return pl.pallas_call(
        body, name=name, grid_spec=grid_spec, out_shape=jax.ShapeDtypeStruct(theirs.shape, out_dtype),
        compiler_params=_params(2))(_core_index(), g, theirs)


def _adamw_rows(w, m, v, g_mine, g_theirs, row0, br, name):
    rows = w.shape[0]
    b0 = row0 // br
    per_half = g_mine.shape[0] // br
    c1 = 1.0 / (1.0 - ADAM_B1 ** ADAM_STEP)
    c2 = 1.0 / (1.0 - ADAM_B2 ** ADAM_STEP)

    def body(c_ref, w_ref, gm_ref, gt_ref, m_ref, v_ref, g_ref, d_ref, nm_ref, nv_ref):
        gv = jnp.where((b0 + pl.program_id(0)) // per_half == c_ref[0], gm_ref[...], gt_ref[...])
        m = ADAM_B1 * m_ref[...] + (1.0 - ADAM_B1) * gv
        v = ADAM_B2 * v_ref[...] + (1.0 - ADAM_B2) * (gv * gv)
        g_ref[...] = gv
        nm_ref[...] = m
        nv_ref[...] = v
        d_ref[...] = -ADAM_LR * ((m * c1) / (jnp.sqrt(v * c2) + ADAM_EPS) + ADAM_WD * w_ref[...])

    spec = pl.BlockSpec((br, D_MODEL), lambda i, c: (i, 0))
    part = pl.BlockSpec((br, D_MODEL), lambda i, c: ((b0 + i) % per_half, 0))
    shp = jax.ShapeDtypeStruct((rows, D_MODEL), F32)
    grid_spec = pltpu.PrefetchScalarGridSpec(
        num_scalar_prefetch=1, grid=(rows // br,), in_specs=[spec, part, part, spec, spec], out_specs=[spec] * 4)
    return pl.pallas_call(
        body, name=name, grid_spec=grid_spec, out_shape=[shp] * 4,
        compiler_params=_params())(_core_index(), w, g_mine, g_theirs, m, v)


def _sum_lead(a, name):
    n, rows, _ = a.shape
    tr = 128 if rows % 128 == 0 else rows

    def body(a_ref, o_ref):
        acc = a_ref[0].astype(F32)
        for k in range(1, n):
            acc = acc + a_ref[k].astype(F32)
        o_ref[...] = acc

    return pl.pallas_call(
        body, name=name, grid=(rows // tr,),
        in_specs=[pl.BlockSpec((n, tr, D_MODEL), lambda i: (0, i, 0))],
        out_specs=pl.BlockSpec((tr, D_MODEL), lambda i: (i, 0)),
        out_shape=jax.ShapeDtypeStruct((rows, D_MODEL), F32), compiler_params=_params())(a)


ANY = pl.BlockSpec(memory_space=pl.ANY)


def _mesh_pos():
    return lax.axis_index("x"), lax.axis_index("y"), lax.axis_index("c")


def _gather_phases(in_refs, out_refs, bounces, send_sems, recv_sems, local_sems):
    na = len(in_refs)
    halves = [r.shape[0] // 2 for r in in_refs]
    ncopy = 3 * na
    x, y, c = _mesh_pos()
    me = 2 * x + y
    sibling = (x, y, 1 - c)
    chips = [(1 - x, y), (x, 1 - y), (1 - x, 1 - y)]
    ids = [2 * chip[0] + chip[1] for chip in chips]

    def piece(a, q, h):
        return out_refs[a].at[q, pl.ds(h * halves[a], halves[a]), :]

    def copy(s, a, q, h, to, src=None):
        return pltpu.make_async_remote_copy(
            src_ref=piece(a, q, h) if src is None else src, dst_ref=piece(a, q, h),
            send_sem=send_sems.at[s], recv_sem=recv_sems.at[s], device_id=to, device_id_type=MESH)

    def sends():
        return [copy(j * na + a, a, me, c, (*chip, c), src=in_refs[a].at[pl.ds(c * halves[a], halves[a]), :])
                for j, chip in enumerate(chips) for a in range(na)]

    def forwards():
        return [copy(ncopy + j * na + a, a, ids[j], c, sibling) for j in range(3) for a in range(na)]

    def stores():
        return [pltpu.make_async_copy(bounces[a], out_refs[a].at[me], local_sems.at[na + a]) for a in range(na)]

    def start():
        loads = [pltpu.make_async_copy(in_refs[a], bounces[a], local_sems.at[a]) for a in range(na)]
        for cp in loads:
            cp.start()
        for cp in sends():
            cp.start()
        for ld, st in zip(loads, stores()):
            ld.wait()
            st.start()

    def forward():
        fwd = forwards()
        for j in range(3):
            for a in range(na):
                copy(j * na + a, a, ids[j], c, (x, y, c)).wait_recv()
                fwd[j * na + a].start()

    def finish():
        for j in range(3):
            for a in range(na):
                copy(ncopy + j * na + a, a, ids[j], 1 - c, (x, y, c)).wait_recv()
        for cp in sends() + forwards():
            cp.wait_send()
        for cp in stores():
            cp.wait()

    return start, forward, finish


def _gather_scratch(arrs):
    ncopy = 3 * len(arrs)
    return ([pltpu.VMEM(a.shape, a.dtype) for a in arrs]
            + [pltpu.SemaphoreType.DMA((2 * ncopy,)), pltpu.SemaphoreType.DMA((2 * ncopy,)),
               pltpu.SemaphoreType.DMA((2 * len(arrs),))])


def _gather_weights(arrs):
    na = len(arrs)

    def body(*refs):
        start, forward, finish = _gather_phases(refs[:na], refs[na:2 * na], refs[2 * na:3 * na], *refs[3 * na:])
        start()
        forward()
        finish()

    return pl.pallas_call(
        body, name="gather_weights", in_specs=[ANY] * na, out_specs=[ANY] * na,
        out_shape=[jax.ShapeDtypeStruct((4,) + a.shape, a.dtype) for a in arrs],
        scratch_shapes=_gather_scratch(arrs),
        compiler_params=pltpu.CompilerParams(vmem_limit_bytes=VMEM_LIMIT),
    )(*arrs)


def _gather_all_phases(v_ref, o_ref, bounce, send_sems, recv_sems, local_sems):
    x, y, c = _mesh_pos()
    sibling = (x, y, 1 - c)
    chips = [(1 - x, y), (x, 1 - y), (1 - x, 1 - y)]

    def blk(px, py, pc):
        return o_ref.at[4 * px + 2 * py + pc]

    def copy(k, block, to, src=None):
        return pltpu.make_async_remote_copy(
            src_ref=blk(*block) if src is None else src, dst_ref=blk(*block),
            send_sem=send_sems.at[k], recv_sem=recv_sems.at[k], device_id=to, device_id_type=MESH)

    def first():
        return ([copy(0, (x, y, c), sibling, src=v_ref)]
                + [copy(1 + j, (x, y, c), (*chip, c), src=v_ref) for j, chip in enumerate(chips)])

    def passed():
        return [copy(4 + j, (*chip, c), sibling) for j, chip in enumerate(chips)]

    def store():
        return pltpu.make_async_copy(bounce, blk(x, y, c), local_sems.at[1])

    def start():
        load = pltpu.make_async_copy(v_ref, bounce, local_sems.at[0])
        load.start()
        for cp in first():
            cp.start()
        load.wait()
        store().start()

    def forward():
        fwd = passed()
        for j, chip in enumerate(chips):
            copy(1 + j, (*chip, c), (x, y, c)).wait_recv()
            fwd[j].start()

    def finish():
        copy(0, (x, y, 1 - c), (x, y, c)).wait_recv()
        for j, chip in enumerate(chips):
            copy(4 + j, (*chip, 1 - c), (x, y, c)).wait_recv()
        for cp in first() + passed():
            cp.wait_send()
        store().wait()

    return start, forward, finish


def _gather_all_scratch(v):
    return [pltpu.VMEM(v.shape, v.dtype), pltpu.SemaphoreType.DMA((7,)), pltpu.SemaphoreType.DMA((7,)),
            pltpu.SemaphoreType.DMA((2,))]


def _scatter_and_gather(p, v, name):
    def body(p_ref, v_ref, got_ref, o_ref, p_bounce, p_send, p_recv, p_local, bounce, send_sems, recv_sems,
             local_sems):
        start, finish = _chip_scatter_phases(p_ref, got_ref, p_bounce, p_send, p_recv, p_local)
        g_start, g_forward, g_finish = _gather_all_phases(v_ref, o_ref, bounce, send_sems, recv_sems, local_sems)
        start()
        g_start()
        g_forward()
        g_finish()
        finish()

    return pl.pallas_call(
        body, name=name, in_specs=[ANY, ANY], out_specs=[ANY, ANY],
        out_shape=[jax.ShapeDtypeStruct(p.shape, p.dtype), jax.ShapeDtypeStruct((8,) + v.shape, v.dtype)],
        scratch_shapes=_chip_scatter_scratch(p) + _gather_all_scratch(v),
    )(p, v)


def _pair_split_phases(g_ref, theirs_ref, send_sems, recv_sems):
    n, rows, _ = g_ref.shape
    half = rows // 2
    ch = half // COMM_CHUNKS
    x, y, c = _mesh_pos()

    def gives():
        return [pltpu.make_async_remote_copy(
            src_ref=g_ref.at[q, pl.ds((1 - c) * half + k * ch, ch), :],
            dst_ref=theirs_ref.at[q, pl.ds(k * ch, ch), :],
            send_sem=send_sems.at[q * COMM_CHUNKS + k], recv_sem=recv_sems.at[q * COMM_CHUNKS + k],
            device_id=(x, y, 1 - c), device_id_type=MESH) for q in range(n) for k in range(COMM_CHUNKS)]

    def start():
        for cp in gives():
            cp.start()

    def finish():
        for cp in gives():
            cp.wait()

    return start, finish


def _pair_split_scratch(g):
    return [pltpu.SemaphoreType.DMA((g.shape[0] * COMM_CHUNKS,)), pltpu.SemaphoreType.DMA((g.shape[0] * COMM_CHUNKS,))]


def _pair_split_shape(g):
    return jax.ShapeDtypeStruct((g.shape[0], g.shape[1] // 2, D_MODEL), g.dtype)


def _pair_split(g, name):
    def body(g_ref, theirs_ref, send_sems, recv_sems):
        start, finish = _pair_split_phases(g_ref, theirs_ref, send_sems, recv_sems)
        start()
        finish()

    return pl.pallas_call(
        body, name=name, in_specs=[ANY], out_specs=ANY, out_shape=_pair_split_shape(g),
        scratch_shapes=_pair_split_scratch(g))(g)


def _chip_scatter_phases(p_ref, o_ref, bounce, send_sems, recv_sems, local_sems):
    x, y, c = _mesh_pos()
    me = 2 * x + y
    chips = [(1 - x, y), (x, 1 - y), (1 - x, 1 - y)]

    def keep():
        return pltpu.make_async_copy(bounce, o_ref.at[me], local_sems.at[1])

    def sends():
        return [pltpu.make_async_remote_copy(
            src_ref=p_ref.at[2 * chip[0] + chip[1]], dst_ref=o_ref.at[me],
            send_sem=send_sems.at[j], recv_sem=recv_sems.at[j], device_id=(*chip, c), device_id_type=MESH)
            for j, chip in enumerate(chips)]

    def start():
        load = pltpu.make_async_copy(p_ref.at[me], bounce, local_sems.at[0])
        load.start()
        for cp in sends():
            cp.start()
        load.wait()
        keep().start()

    def finish():
        for j, chip in enumerate(chips):
            q = 2 * chip[0] + chip[1]
            pltpu.make_async_remote_copy(
                src_ref=p_ref.at[q], dst_ref=o_ref.at[q], send_sem=send_sems.at[j], recv_sem=recv_sems.at[j],
                device_id=(*chip, c), device_id_type=MESH).wait_recv()
        for cp in sends():
            cp.wait_send()
        keep().wait()

    return start, finish


def _chip_scatter_scratch(p):
    return [pltpu.VMEM(p.shape[1:], p.dtype), pltpu.SemaphoreType.DMA((3,)), pltpu.SemaphoreType.DMA((3,)),
            pltpu.SemaphoreType.DMA((2,))]


def _pair_join_phases(r_ref, o_ref, send_sems, recv_sems):
    ch = r_ref.shape[0] // COMM_CHUNKS
    x, y, c = _mesh_pos()

    def gives():
        return [pltpu.make_async_remote_copy(
            src_ref=r_ref.at[pl.ds(k * ch, ch), :], dst_ref=o_ref.at[pl.ds(k * ch, ch), :],
            send_sem=send_sems.at[k], recv_sem=recv_sems.at[k], device_id=(x, y, 1 - c), device_id_type=MESH)
            for k in range(COMM_CHUNKS)]

    def start():
        for cp in gives():
            cp.start()

    def finish():
        for cp in gives():
            cp.wait()

    return start, finish


def _pair_join_scratch():
    return [pltpu.SemaphoreType.DMA((COMM_CHUNKS,)), pltpu.SemaphoreType.DMA((COMM_CHUNKS,))]


def _pair_join(r, name):
    def body(r_ref, o_ref, send_sems, recv_sems):
        start, finish = _pair_join_phases(r_ref, o_ref, send_sems, recv_sems)
        start()
        finish()

    return pl.pallas_call(
        body, name=name, in_specs=[ANY], out_specs=ANY, out_shape=jax.ShapeDtypeStruct(r.shape, r.dtype),
        scratch_shapes=_pair_join_scratch())(r)


SHARD_BIG = (("even_w_in", (1024, 512)), ("ssm_w_glu", (512, 256)), ("even_w_out", (256, 1024)),
             ("odd_w_in", (1024, 768)), ("odd_w_out", (256, 1024)))
SHARD_SMALL = (("odd_norm", 1), ("conv_w", CONV_K), ("conv_b", 1), ("conv_ln_g", 1), ("conv_ln_b", 1))
REP_NAMES = (("even_norm", (1024,)), ("pool_w", (4, 128, 128)), ("pool_scale", (512,)), ("ssm_log_dt", (32,)),
             ("ssm_a_re", (32, 64)), ("ssm_a_im", (32, 64)), ("ssm_b_re", (32, 64, 16)), ("ssm_b_im", (32, 64, 16)),
             ("ssm_c_re", (32, 16, 64)), ("ssm_c_im", (32, 16, 64)), ("ssm_d", (512,)), ("final_norm", (1024,)))


def _pack_rep(d):
    flat = jnp.concatenate([d[n].reshape(-1) for n, _ in REP_NAMES])
    return jnp.pad(flat, (0, REP_ROWS * D_MODEL - flat.shape[0])).reshape(REP_ROWS, D_MODEL)


def _unpack_rep(buf):
    flat = buf.reshape(-1)
    out = {}
    off = 0
    for n, shp in REP_NAMES:
        size = 1
        for s in shp:
            size *= s
        out[n] = flat[off:off + size].reshape(shp)
        off += size
    return out


def _cols_split(full, cols):
    rows = full.shape[0]
    return full.reshape(rows, 4, cols).transpose(1, 0, 2).reshape(4, -1, D_MODEL)


def _block_diag(a):
    a = a.reshape(4, 8, GROUP_DIM, N_STATE)
    eye = jnp.eye(8, dtype=a.dtype)
    return (a[:, :, :, None, :] * eye[None, :, None, :, None]).reshape(4, 128, LCH)


def _block_diag_take(m):
    m = m.reshape(4, 8, GROUP_DIM, 8, N_STATE)
    eye = jnp.eye(8, dtype=m.dtype)
    return jnp.sum(m * eye[None, :, None, :, None], axis=3).reshape(N_GROUPS, GROUP_DIM, N_STATE)


def _ssm_discretise(log_dt, a_re, a_im, b_re, b_im):
    dt = jnp.exp(log_dt)[:, None]
    mag = jnp.exp(a_re * dt)
    ang = a_im * dt
    abar_re = mag * jnp.cos(ang)
    abar_im = mag * jnp.sin(ang)
    den = a_re * a_re + a_im * a_im
    nr = abar_re - 1.0
    ni = abar_im
    k_re = (nr * a_re + ni * a_im) / den
    k_im = (ni * a_re - nr * a_im) / den
    bb_re = k_re[..., None] * b_re - k_im[..., None] * b_im
    bb_im = k_re[..., None] * b_im + k_im[..., None] * b_re
    return abar_re, abar_im, bb_re, bb_im


def _scan_tables(log_dt, a_re, a_im):
    dt = jnp.exp(log_dt)[:, None]
    lam_re = (a_re * dt).reshape(1, STATES)
    lam_im = (a_im * dt).reshape(1, STATES)

    def powers(k):
        mag = jnp.exp(k * lam_re)
        return mag * jnp.cos(k * lam_im), mag * jnp.sin(k * lam_im)

    p_re, p_im = powers((1 + jnp.arange(TM) // SUB).astype(F32)[:, None])
    q_re, q_im = powers((SEG_LEN * (1 + jnp.arange(SUB))).astype(F32)[:, None])
    return p_re, p_im, q_re, q_im


def _local_step(x, tgt, w, shard):
    row = lambda a: a.reshape(1, -1)
    (e_w_in,) = _gather_weights([shard["even_w_in"].astype(BF16)])
    wp = w["pool_w"].astype(BF16)
    ssm_in = (w["ssm_log_dt"], w["ssm_a_re"], w["ssm_a_im"], w["ssm_b_re"], w["ssm_b_im"])
    (abar_re, abar_im, bb_re, bb_im), ssm_vjp = jax.vjp(_ssm_discretise, *ssm_in)
    mb_re = _block_diag(bb_re.transpose(0, 2, 1)).astype(BF16)
    mb_im = _block_diag(bb_im.transpose(0, 2, 1)).astype(BF16)
    cm_re = _block_diag(w["ssm_c_re"]).astype(BF16)
    cm_im = _block_diag(w["ssm_c_im"]).astype(BF16)
    p8_re, p8_im, q_re, q_im = _scan_tables(w["ssm_log_dt"], w["ssm_a_re"], w["ssm_a_im"])
    qr_re, qr_im = q_re[::-1], q_im[::-1]
    pm = _perm_matrix()
    pmt = pm.T
    g0, gf = row(w["even_norm"]), row(w["final_norm"])
    ps, dskip = row(w["pool_scale"]), row(w["ssm_d"])

    proj, yp, (g_glu, g_eout) = _norm_in(
        x, g0, e_w_in, "even_in", ride=[shard["ssm_w_glu"].astype(BF16), shard["even_w_out"].astype(BF16)],
        pool=(wp, ps))
    wglu = g_glu.transpose(1, 0, 2).reshape(SSM_W, 2 * SSM_W)
    e_w_out = g_eout.reshape(D_MODEL, D_MODEL)
    (ys, car_re, car_im), (g_oin, g_oout, g_small) = _ssm_fwd(
        proj, pm, pmt, mb_re, mb_im, p8_re, p8_im, q_re, q_im, cm_re, cm_im, dskip, wglu, ride=_odd_shards(shard))
    o_w_in, o_w_out = g_oin, g_oout.reshape(D_MODEL, D_MODEL)
    sm = g_small.transpose(1, 0, 2).reshape(SMALL_ROWS, D_MODEL)
    cw = sm[1:1 + HALO]
    g1, cb, lg, lb = sm[0:1], sm[32:33], sm[33:34], sm[34:35]
    x1, yg = _even_out(yp, ys, proj, x, e_w_out)
    q, _, _ = _norm_in(x1, g1, o_w_in, "odd_in")
    y2, cv = _conv_fwd(q, cw, cb, lg, lb)
    dx2, loss_lanes, d_gf = _odd_out_loss(y2, x1, o_w_out, gf, tgt)

    dcv, dz2, d_o_w_out, d_lg, d_lb = _odd_bwd_out(dx2, o_w_out, y2, cv, q, lg, lb)
    dval, dgate, d_cw, d_cb = _conv_bwd(dcv, q, cw)
    dx1, d_g1, d_o_w_in = _in_bwd([dval, dgate, dz2], o_w_in, x1, g1, dx2, "odd_in_bwd")
    g_odd = _pack_odd_grads({
        "odd_w_in": d_o_w_in, "odd_w_out": d_o_w_out, "odd_norm": d_g1.reshape(-1),
        "conv_w": d_cw.reshape(HALO, SUB, D_MODEL).sum(axis=1)[:CONV_K], "conv_b": d_cb.reshape(-1),
        "conv_ln_g": d_lg.reshape(-1), "conv_ln_b": d_lb.reshape(-1)})
    dycat, dz, d_e_w_out, dup, d_wp, d_ps, theirs_odd = _even_bwd_out(dx1, e_w_out, yg, yp, ys, proj, wp, ps, g_odd)
    sums_odd = _pair_add(g_odd, theirs_odd, BF16, "pair_add_odd")
    (dus, d_mb_re, d_mb_im, d_cm_re, d_cm_im, da_re, da_im, d_dskip, d_wglu, got_odd) = _ssm_bwd(
        dycat, proj, car_re, car_im, pm, pmt, mb_re, mb_im, p8_re, p8_im, q_re, q_im, qr_re, qr_im,
        cm_re, cm_im, dskip, wglu, sums_odd)
    d_abar_re = jnp.sum(da_re, axis=0).reshape(N_GROUPS, N_STATE)
    d_abar_im = jnp.sum(da_im, axis=0).reshape(N_GROUPS, N_STATE)
    d_bb_re = _block_diag_take(d_mb_re).transpose(0, 2, 1)
    d_bb_im = _block_diag_take(d_mb_im).transpose(0, 2, 1)
    d_log_dt, d_a_re, d_a_im, d_b_re, d_b_im = ssm_vjp((d_abar_re, d_abar_im, d_bb_re, d_bb_im))
    rep_early = _pack_rep({
        "even_norm": jnp.zeros((D_MODEL,), F32), "pool_w": d_wp, "pool_scale": d_ps.reshape(-1),
        "ssm_log_dt": d_log_dt, "ssm_a_re": d_a_re, "ssm_a_im": d_a_im, "ssm_b_re": d_b_re, "ssm_b_im": d_b_im,
        "ssm_c_re": _block_diag_take(d_cm_re), "ssm_c_im": _block_diag_take(d_cm_im),
        "ssm_d": d_dskip.reshape(-1), "final_norm": d_gf.reshape(-1)})
    odd_mine = _sum_lead(got_odd, "chip_sum_odd")
    dx, d_g0, d_e_w_in, rep_parts, odd_theirs = _in_bwd([dup, dus, dz], e_w_in, x, g0, dx1, "even_in_bwd",
                                                        ride=(rep_early, odd_mine))

    grads = {"even_norm": d_g0, "even_w_in": d_e_w_in, "ssm_w_glu": d_wglu, "even_w_out": d_e_w_out}
    return jnp.sum(loss_lanes), dx, grads, (odd_mine, odd_theirs), rep_parts


WEIGHT_NAMES = ("even_norm", "even_w_in", "pool_w", "pool_scale", "ssm_log_dt", "ssm_a_re", "ssm_a_im",
                "ssm_b_re", "ssm_b_im", "ssm_c_re", "ssm_c_im", "ssm_d", "ssm_w_glu", "even_w_out", "odd_norm",
                "odd_w_in", "conv_w", "conv_b", "conv_ln_g", "conv_ln_b", "odd_w_out", "final_norm")
SHARDED = tuple(n for n, _ in SHARD_BIG) + tuple(n for n, _ in SHARD_SMALL)


SMALL_ROWS = 64


def _odd_shards(shard):
    small = jnp.concatenate([shard[n].reshape(r, 256) for n, r in SHARD_SMALL], axis=0)
    small = jnp.pad(small, ((0, SMALL_ROWS - small.shape[0]), (0, 0)))
    return [shard["odd_w_in"].astype(BF16), shard["odd_w_out"].astype(BF16), small]


def _pack_small(d):
    small = jnp.concatenate([d[n].reshape(r, -1) for n, r in SHARD_SMALL], axis=0)
    if small.shape[1] == D_MODEL:
        small = small.reshape(35, 4, 256).transpose(1, 0, 2)
    small = small.reshape(-1, 35 * 256)
    small = jnp.pad(small, ((0, 0), (0, ROWS_SMALL * D_MODEL - 35 * 256)))
    return small.reshape(-1, ROWS_SMALL, D_MODEL)


def _unpack_small(buf):
    small = buf.reshape(-1)[:35 * 256].reshape(35, 256)
    out = {}
    off = 0
    for n, r in SHARD_SMALL:
        out[n] = small[off:off + r].reshape((r, 256) if r > 1 else (256,))
        off += r
    return out


EVEN_PACK = (("even_w_in", 0, 512), ("ssm_w_glu", 512, 128), ("even_w_out", 640, 256))
ROWS_EVEN = 1024
ODD_PACK = (("odd_w_in", 0, 768), ("odd_w_out", 768, 256))
ODD_SMALL_ROW = 1024
ROWS_ODD = 1280


def _pack_even_grads(g):
    parts = [g["even_w_in"].reshape(4, -1, D_MODEL), _cols_split(g["ssm_w_glu"], 256),
             g["even_w_out"].reshape(4, -1, D_MODEL), jnp.zeros((4, ROWS_EVEN - 896, D_MODEL), F32)]
    return jnp.concatenate(parts, axis=1)


def _pack_odd_grads(g):
    parts = [g["odd_w_in"].reshape(4, -1, D_MODEL), g["odd_w_out"].reshape(4, -1, D_MODEL), _pack_small(g),
             jnp.zeros((4, ROWS_ODD - ODD_SMALL_ROW - ROWS_SMALL, D_MODEL), F32)]
    return jnp.concatenate(parts, axis=1)


def kernel(x, even_norm, even_w_in, pool_w, pool_scale, ssm_log_dt, ssm_a_re, ssm_a_im, ssm_b_re, ssm_b_im, ssm_c_re, ssm_c_im, ssm_d, ssm_w_glu, even_w_out, odd_norm, odd_w_in, conv_w, conv_b, conv_ln_g, conv_ln_b, odd_w_out, final_norm, loss_target, m_even_norm, m_even_w_in, m_pool_w, m_pool_scale, m_ssm_log_dt, m_ssm_a_re, m_ssm_a_im, m_ssm_b_re, m_ssm_b_im, m_ssm_c_re, m_ssm_c_im, m_ssm_d, m_ssm_w_glu, m_even_w_out, m_odd_norm, m_odd_w_in, m_conv_w, m_conv_b, m_conv_ln_g, m_conv_ln_b, m_odd_w_out, m_final_norm, v_even_norm, v_even_w_in, v_pool_w, v_pool_scale, v_ssm_log_dt, v_ssm_a_re, v_ssm_a_im, v_ssm_b_re, v_ssm_b_im, v_ssm_c_re, v_ssm_c_im, v_ssm_d, v_ssm_w_glu, v_even_w_out, v_odd_norm, v_odd_w_in, v_conv_w, v_conv_b, v_conv_ln_g, v_conv_ln_b, v_odd_w_out, v_final_norm):
    ws = dict(zip(WEIGHT_NAMES, (even_norm, even_w_in, pool_w, pool_scale, ssm_log_dt, ssm_a_re, ssm_a_im, ssm_b_re,
                                 ssm_b_im, ssm_c_re, ssm_c_im, ssm_d, ssm_w_glu, even_w_out, odd_norm, odd_w_in,
                                 conv_w, conv_b, conv_ln_g, conv_ln_b, odd_w_out, final_norm)))
    ms = dict(zip(WEIGHT_NAMES, (m_even_norm, m_even_w_in, m_pool_w, m_pool_scale, m_ssm_log_dt, m_ssm_a_re,
                                 m_ssm_a_im, m_ssm_b_re, m_ssm_b_im, m_ssm_c_re, m_ssm_c_im, m_ssm_d, m_ssm_w_glu,
                                 m_even_w_out, m_odd_norm, m_odd_w_in, m_conv_w, m_conv_b, m_conv_ln_g, m_conv_ln_b,
                                 m_odd_w_out, m_final_norm)))
    vs = dict(zip(WEIGHT_NAMES, (v_even_norm, v_even_w_in, v_pool_w, v_pool_scale, v_ssm_log_dt, v_ssm_a_re,
                                 v_ssm_a_im, v_ssm_b_re, v_ssm_b_im, v_ssm_c_re, v_ssm_c_im, v_ssm_d, v_ssm_w_glu,
                                 v_even_w_out, v_odd_norm, v_odd_w_in, v_conv_w, v_conv_b, v_conv_ln_g, v_conv_ln_b,
                                 v_odd_w_out, v_final_norm)))
    lead = {n: a.shape for n, a in ws.items()}
    drop = lambda d: {n: (a[0] if n != "final_norm" else a) for n, a in d.items()}
    ws, ms, vs = drop(ws), drop(ms), drop(vs)

    shard = {n: ws[n] for n in SHARDED}
    rep = {n: ws[n] for n, _ in REP_NAMES}
    loss_part, grad_x, grads, (odd_mine, odd_theirs), rep_parts = _local_step(x[0], loss_target[0], rep, shard)
    loss = lax.psum(loss_part, ("x", "y", "c"))

    g_even = _pack_even_grads(grads)
    got_even, late_parts = _scatter_and_gather(
        _pair_add(g_even, _pair_split(g_even, "pair_split_even"), BF16, "pair_add_even"),
        jnp.pad(grads["even_norm"], ((0, SUB - 1), (0, 0))), "scatter_even_gather_late")
    even_mine = _sum_lead(got_even, "chip_sum_even")
    even_theirs = _pair_join(even_mine, "pair_join_even")
    outs = [{}, {}, {}, {}]
    for pack, mine, theirs in ((EVEN_PACK, even_mine, even_theirs), (ODD_PACK, odd_mine, odd_theirs)):
        for n, row0, rows in pack:
            view = lambda a: a.reshape(rows, D_MODEL)
            res = _adamw_rows(view(ws[n]), view(ms[n]), view(vs[n]), mine, theirs, row0, 128, "adamw_" + n)
            for o, r in zip(outs, res):
                o[n] = r
    small = lambda d: _pack_small({n: d[n] for n, _ in SHARD_SMALL})[0]
    res = _adamw_rows(small(ws), small(ms), small(vs), odd_mine, odd_theirs, ODD_SMALL_ROW, ROWS_SMALL, "adamw_small")
    for o, r in zip(outs, res):
        o.update(_unpack_small(r))
    g_rep = lax.dynamic_update_slice(_sum_lead(rep_parts, "rep_sum"), _sum_lead(late_parts, "late_sum")[0:1], (0, 0))
    res = _adamw(_pack_rep(rep), g_rep, _pack_rep({n: ms[n] for n, _ in REP_NAMES}),
                 _pack_rep({n: vs[n] for n, _ in REP_NAMES}), "adamw_rep")
    for o, r in zip(outs, (g_rep,) + tuple(res)):
        o.update(_unpack_rep(r))

    leaves = [[o[n].reshape(lead[n]) for n in WEIGHT_NAMES] for o in outs]
    return (loss, grad_x[None], *leaves[0], *leaves[1], *leaves[2], *leaves[3])
```

```python
import jax
import jax.numpy as jnp
from jax import lax
from jax.experimental import pallas as pl
from jax.experimental.pallas import tpu as pltpu

F32 = jnp.float32
BF16 = jnp.bfloat16
MESH = pl.DeviceIdType.MESH

D_MODEL = 1024
RMS_EPS = 1e-6
LN_EPS = 1e-5
N_GROUPS = 32
GROUP_DIM = 16
N_STATE = 64
STATES = N_GROUPS * N_STATE
SSM_W = 512
POOL_W = 512
CONV_K = 31
HALO = 32
POOL_HALO = 16

ADAM_LR = 0.001
ADAM_B1 = 0.9
ADAM_B2 = 0.999
ADAM_EPS = 1e-08
ADAM_WD = 0.01
ADAM_STEP = 10

TM = 256
TM_MM = 512
SUB = 8
LCH = 512
SCAN_L = 1024
VMEM_LIMIT = 56 * 1024 * 1024

ROWS_SMALL = 16
REP_ROWS = 200
COMM_CHUNKS = 4


def _params(n_axes=1):
    return pltpu.CompilerParams(dimension_semantics=("arbitrary",) * n_axes, vmem_limit_bytes=VMEM_LIMIT)


def _rows(w, cb=0, rev=None, tm=TM):
    if rev is None:
        return pl.BlockSpec((tm, w), lambda i: (i, cb))
    return pl.BlockSpec((tm, w), lambda i: (rev - 1 - i, cb))


def _mm_rows(w, cb=0):
    return _rows(w, cb, tm=TM_MM)


def _full(shape):
    n = len(shape)
    return pl.BlockSpec(shape, lambda i: (0,) * n)


def _prev(hr, w, cb=0, tm=TM):
    r = tm // hr
    return pl.BlockSpec((hr, w), lambda i: (jnp.maximum(i * r - 1, 0), cb))


def _next(hr, w, nrows, cb=0, tm=TM):
    r = tm // hr
    last = nrows // hr - 1
    return pl.BlockSpec((hr, w), lambda i: (jnp.minimum((i + 1) * r, last), cb))


def _dot(a, b):
    return jnp.dot(a, b, preferred_element_type=F32)


def _dot_nt(a, b):
    return lax.dot_general(a, b, (((1,), (1,)), ((), ())), preferred_element_type=F32)


def _dot_tn(a, b):
    return lax.dot_general(a, b, (((0,), (0,)), ((), ())), preferred_element_type=F32)


def _sig(x):
    return 1.0 / (1.0 + jnp.exp(-x))


def _zero_at_first(i, *refs):
    @pl.when(i == 0)
    def _():
        for r in refs:
            r[...] = jnp.zeros_like(r)


def _norm_in(x, g, w, name, ride=(), pool=None):
    t, ns = x.shape[0], w.shape[2]
    n = 4 * ns
    ng = len(ride)
    npool = 0 if pool is None else 1
    nstep = t // TM_MM

    def body(x_ref, g_ref, w_ref, *rest):
        pool_in, rest = rest[:2 * npool], rest[2 * npool:]
        ride_in, o_ref, rest = rest[:ng], rest[ng], rest[ng + 1:]
        yp_ref, rest = (rest[0], rest[1:]) if npool else (None, rest)
        ride_out, rest = rest[:ng], rest[ng:]
        halo_ref, rest = (rest[0], rest[1:]) if npool else (None, rest)
        i = pl.program_id(0)
        if ng:
            start, forward, finish = _gather_phases(ride_in, ride_out, rest[:ng], *rest[ng:])
            pl.when(i == 0)(start)
            pl.when(i == nstep // 2)(forward)
        xv = x_ref[...]
        r = lax.rsqrt(jnp.mean(xv * xv, axis=-1, keepdims=True) + RMS_EPS)
        h = (xv * r * g_ref[...]).astype(BF16)
        first = _dot(h, w_ref[0]).astype(BF16)
        o_ref[:, 0:ns] = first
        if npool:
            wp_ref, ps_ref = pool_in
            _zero_at_first(i, halo_ref)
            pos = (i * TM_MM + 1 + lax.broadcasted_iota(jnp.int32, (TM_MM, 1), 0)).astype(F32)
            for gi in range(4):
                sl = slice(128 * gi, 128 * (gi + 1))
                u = first[:, sl].astype(F32)
                s = _pool_sums(jnp.concatenate([halo_ref[:, sl], u], axis=0), gi, True)[POOL_HALO:, :]
                pooled = s / jnp.minimum(pos, float(2 << gi)) - u
                yp_ref[:, sl] = (_dot(pooled.astype(BF16), wp_ref[gi]) * ps_ref[:, sl]).astype(BF16)
                halo_ref[:, sl] = u[TM_MM - POOL_HALO:, :]
        for s in range(1, 4):
            o_ref[:, s * ns:(s + 1) * ns] = _dot(h, w_ref[s]).astype(BF16)
        if ng:
            pl.when(i == nstep - 1)(finish)

    pool_args = [] if pool is None else list(pool)
    res = pl.pallas_call(
        body, name=name, grid=(nstep,),
        in_specs=[_mm_rows(D_MODEL), _full((1, D_MODEL)), _full(w.shape)] + [_full(a.shape) for a in pool_args]
        + [ANY] * ng,
        out_specs=[_mm_rows(n)] + [_mm_rows(POOL_W)] * npool + [ANY] * ng,
        out_shape=[jax.ShapeDtypeStruct((t, n), BF16)] + [jax.ShapeDtypeStruct((t, POOL_W), BF16)] * npool
        + [jax.ShapeDtypeStruct((4,) + a.shape, a.dtype) for a in ride],
        scratch_shapes=[pltpu.VMEM((POOL_HALO, POOL_W), F32)] * npool + (_gather_scratch(ride) if ng else []),
        compiler_params=_params())(x, g, w, *pool_args, *ride)
    return res[0], (res[1] if npool else None), res[1 + npool:]


def _pool_sums(ext, g, forward):
    n = ext.shape[0]
    s = ext
    for step in range(g + 1):
        k = 1 << step
        s = s + pltpu.roll(s, k if forward else n - k, 0)
    return s


SEG_LEN = TM // SUB


def _perm_matrix():
    p = jnp.arange(TM)
    src = (p % SUB) * SEG_LEN + p // SUB
    return (src[:, None] == jnp.arange(TM)[None, :]).astype(BF16)


def _cmul_add(are, aim, vre, vim, bre, bim):
    return are * vre - aim * vim + bre, are * vim + aim * vre + bim


def _segment_chain(ere, eim, qre, qim, cin_re, cin_im, row, up):
    for sh in (1, 2, 4):
        mre, mim = (qre[SUB - sh:SUB - sh + 1, :], qim[SUB - sh:SUB - sh + 1, :]) if up else \
                   (qre[sh - 1:sh, :], qim[sh - 1:sh, :])
        keep = (row < SUB - sh) if up else (row >= sh)
        sre = jnp.where(keep, pltpu.roll(ere, SUB - sh if up else sh, 0), 0.0)
        sim = jnp.where(keep, pltpu.roll(eim, SUB - sh if up else sh, 0), 0.0)
        ere, eim = _cmul_add(mre, mim, sre, sim, ere, eim)
    ere, eim = _cmul_add(qre, qim, cin_re, cin_im, ere, eim)
    keep = (row < SUB - 1) if up else (row >= 1)
    ent_re = jnp.where(keep, pltpu.roll(ere, SUB - 1 if up else 1, 0), cin_re)
    ent_im = jnp.where(keep, pltpu.roll(eim, SUB - 1 if up else 1, 0), cin_im)
    return ere, eim, ent_re, ent_im


def _scan_fwd_block(xs_re, xs_im, p8_re, p8_im, q_re, q_im, car_re, car_im, ent_re_ref, ent_im_ref):
    row = lax.broadcasted_iota(jnp.int32, (SUB, SCAN_L), 0)
    for j in range(STATES // SCAN_L):
        sl = slice(SCAN_L * j, SCAN_L * (j + 1))
        are, aim = p8_re[0:SUB, sl], p8_im[0:SUB, sl]

        def totals(i, v, sl=sl, are=are, aim=aim):
            r0 = pl.multiple_of(i * SUB, SUB)
            vre, vim = _cmul_add(are, aim, v[0], v[1], xs_re[pl.ds(r0, SUB), sl], xs_im[pl.ds(r0, SUB), sl])
            xs_re[pl.ds(r0, SUB), sl] = vre
            xs_im[pl.ds(r0, SUB), sl] = vim
            return vre, vim

        ere, eim = lax.fori_loop(1, SEG_LEN, totals, (xs_re[0:SUB, sl], xs_im[0:SUB, sl]), unroll=2)
        ere, eim, cre, cim = _segment_chain(ere, eim, q_re[:, sl], q_im[:, sl],
                                            car_re[:, sl], car_im[:, sl], row, False)
        car_re[:, sl] = jnp.broadcast_to(ere[SUB - 1:SUB, :], (SUB, SCAN_L))
        car_im[:, sl] = jnp.broadcast_to(eim[SUB - 1:SUB, :], (SUB, SCAN_L))
        if ent_re_ref is not None:
            ent_re_ref[:, sl] = cre
            ent_im_ref[:, sl] = cim

        def fix(i, c, sl=sl, cre=cre, cim=cim):
            r0 = pl.multiple_of(i * SUB, SUB)
            vre, vim = _cmul_add(p8_re[pl.ds(r0, SUB), sl], p8_im[pl.ds(r0, SUB), sl], cre, cim,
                                 xs_re[pl.ds(r0, SUB), sl], xs_im[pl.ds(r0, SUB), sl])
            xs_re[pl.ds(r0, SUB), sl] = vre
            xs_im[pl.ds(r0, SUB), sl] = vim
            return c

        lax.fori_loop(0, SEG_LEN, fix, 0, unroll=2)


def _ssm_fwd(proj, pm, pmt, mb_re, mb_im, p8_re, p8_im, q_re, q_im, cm_re, cm_im, dskip, wglu, ride=()):
    t = proj.shape[0]
    nblk = t // TM

    ng = len(ride)

    def body(u_ref, pm_ref, pmt_ref, mbre, mbim, p8re, p8im, qre, qim, cmre, cmim, d_ref, wg_ref, *rest):
        ride_in, (y_ref, cre_ref, cim_ref), ride_out = rest[:ng], rest[ng:ng + 3], rest[ng + 3:2 * ng + 3]
        xs_re, xs_im, car_re, car_im, ysk = rest[2 * ng + 3:2 * ng + 8]
        i = pl.program_id(0)
        if ng:
            start, forward, finish = _gather_phases(ride_in, ride_out, rest[2 * ng + 8:3 * ng + 8],
                                                    *rest[3 * ng + 8:])
            pl.when(i == 0)(start)
            pl.when(i == nblk // 2)(forward)
        _zero_at_first(i, car_re, car_im)
        cre_ref[0] = car_re[...]
        cim_ref[0] = car_im[...]
        us = _dot(pm_ref[...], u_ref[...])
        usb = us.astype(BF16)
        for j in range(4):
            xs_re[:, LCH * j:LCH * (j + 1)] = _dot(usb[:, 128 * j:128 * (j + 1)], mbre[j])
            xs_im[:, LCH * j:LCH * (j + 1)] = _dot(usb[:, 128 * j:128 * (j + 1)], mbim[j])
        _scan_fwd_block(xs_re, xs_im, p8re, p8im, qre, qim, car_re, car_im, None, None)
        for j in range(4):
            sl = slice(LCH * j, LCH * (j + 1))
            ysk[:, 128 * j:128 * (j + 1)] = (_dot_nt(xs_re[:, sl].astype(BF16), cmre[j])
                                             - _dot_nt(xs_im[:, sl].astype(BF16), cmim[j]))
        yv = ysk[...] + d_ref[...] * us
        gv = _dot(yv.astype(BF16), wg_ref[...])
        y_ref[...] = _dot(pmt_ref[...], (gv[:, :SSM_W] * _sig(gv[:, SSM_W:])).astype(BF16)).astype(BF16)
        if ng:
            pl.when(i == nblk - 1)(finish)

    blk = (4, 128, LCH)
    res = pl.pallas_call(
        body, name="ssm_fwd", grid=(nblk,),
        in_specs=[_rows(SSM_W, 1), _full((TM, TM)), _full((TM, TM)), _full(blk), _full(blk),
                  _full((TM, STATES)), _full((TM, STATES)), _full((SUB, STATES)), _full((SUB, STATES)),
                  _full(blk), _full(blk), _full((1, SSM_W)), _full((SSM_W, 2 * SSM_W))] + [ANY] * ng,
        out_specs=[_rows(SSM_W), pl.BlockSpec((1, SUB, STATES), lambda i: (i, 0, 0)),
                   pl.BlockSpec((1, SUB, STATES), lambda i: (i, 0, 0))] + [ANY] * ng,
        out_shape=[jax.ShapeDtypeStruct((t, SSM_W), BF16), jax.ShapeDtypeStruct((nblk, SUB, STATES), F32),
                   jax.ShapeDtypeStruct((nblk, SUB, STATES), F32)]
        + [jax.ShapeDtypeStruct((4,) + a.shape, a.dtype) for a in ride],
        scratch_shapes=[pltpu.VMEM((TM, STATES), F32), pltpu.VMEM((TM, STATES), F32),
                        pltpu.VMEM((SUB, STATES), F32), pltpu.VMEM((SUB, STATES), F32),
                        pltpu.VMEM((TM, SSM_W), F32)] + (_gather_scratch(ride) if ng else []),
        compiler_params=_params())(proj, pm, pmt, mb_re, mb_im, p8_re, p8_im, q_re, q_im, cm_re, cm_im, dskip, wglu,
                                   *ride)
    return res[:3], res[3:]


def _even_out(yp, ys, proj, x, w):
    t = x.shape[0]

    def body(yp_ref, ys_ref, z_ref, x_ref, w_ref, x1_ref, yg_ref):
        z = z_ref[...].astype(F32)
        sz = z * _sig(z)
        gp = (yp_ref[...].astype(F32) * sz[:, :POOL_W]).astype(BF16)
        gs = (ys_ref[...].astype(F32) * sz[:, POOL_W:]).astype(BF16)
        yg_ref[:, :POOL_W] = gp
        yg_ref[:, POOL_W:] = gs
        x1_ref[...] = x_ref[...] + _dot(gp, w_ref[:POOL_W, :]) + _dot(gs, w_ref[POOL_W:, :])

    return pl.pallas_call(
        body, name="even_out", grid=(t // TM_MM,),
        in_specs=[_mm_rows(POOL_W), _mm_rows(SSM_W), _mm_rows(D_MODEL, 1), _mm_rows(D_MODEL),
                  _full((D_MODEL, D_MODEL))],
        out_specs=[_mm_rows(D_MODEL), _mm_rows(D_MODEL)],
        out_shape=[jax.ShapeDtypeStruct((t, D_MODEL), F32), jax.ShapeDtypeStruct((t, D_MODEL), BF16)],
        compiler_params=_params())(yp, ys, proj, x, w)


def _phase_copies(ext, cp):
    n = cp.shape[1]
    for j in range(1, SUB):
        cp[j - 1] = ext[pl.ds(j, n), :]


def _shifted(ext, cp, off, nrows, sl, row0=0):
    q, j = divmod(off, SUB)
    if j == 0:
        return ext[pl.ds(row0 + SUB * q, nrows), sl]
    return cp[j - 1, pl.ds(row0 + SUB * q, nrows), sl]


def _conv_taps(ext, cp, w_ref, first, nrows, sl, init, row0=0):
    acc = init
    for k in range(CONV_K):
        acc = acc + w_ref[k:k + 1, sl] * _shifted(ext, cp, first(k), nrows, sl, row0)
    return acc


def _conv_fwd(q, cw, cb, lg, lb):
    t = q.shape[0]

    def body(v_ref, g_ref, hv_ref, hg_ref, z_ref, w_ref, b_ref, lg_ref, lb_ref, y_ref, cv_ref, ext, cp):
        i = pl.program_id(0)
        ext[0:HALO, :] = jnp.where(i == 0, 0.0, hv_ref[...].astype(F32) * _sig(hg_ref[...].astype(F32)))
        ext[HALO:, :] = v_ref[...].astype(F32) * _sig(g_ref[...].astype(F32))
        _phase_copies(ext, cp)

        def lanes(c, carry):
            sl = pl.ds(pl.multiple_of(c * 128, 128), 128)
            cv_ref[:, sl] = _conv_taps(ext, cp, w_ref, lambda k: k + 2, TM, sl,
                                       jnp.broadcast_to(b_ref[:, sl], (TM, 128)))
            return carry

        lax.fori_loop(0, D_MODEL // 128, lanes, 0)
        cv = cv_ref[...]
        cc = cv - jnp.mean(cv, axis=-1, keepdims=True)
        rstd = lax.rsqrt(jnp.mean(cc * cc, axis=-1, keepdims=True) + LN_EPS)
        cl = cc * rstd * lg_ref[...] + lb_ref[...]
        z = z_ref[...].astype(F32)
        y_ref[...] = (cl * _sig(cl) * z * _sig(z)).astype(BF16)

    vec = _full((1, D_MODEL))
    return pl.pallas_call(
        body, name="conv_fwd", grid=(t // TM,),
        in_specs=[_rows(D_MODEL, 0), _rows(D_MODEL, 1), _prev(HALO, D_MODEL, 0), _prev(HALO, D_MODEL, 1),
                  _rows(D_MODEL, 2), _full((HALO, D_MODEL)), vec, vec, vec],
        out_specs=[_rows(D_MODEL), _rows(D_MODEL)],
        out_shape=[jax.ShapeDtypeStruct((t, D_MODEL), BF16), jax.ShapeDtypeStruct((t, D_MODEL), F32)],
        scratch_shapes=[pltpu.VMEM((TM + HALO, D_MODEL), F32),
                        pltpu.VMEM((SUB - 1, TM + HALO - SUB, D_MODEL), F32)],
        compiler_params=_params())(q, q, q, q, q, cw, cb, lg, lb)


def _odd_out_loss(y2, x1, w, gf, tgt):
    t = x1.shape[0]

    def body(y_ref, x_ref, w_ref, g_ref, t_ref, dx_ref, loss_ref, dg_ref):
        i = pl.program_id(0)
        _zero_at_first(i, loss_ref, dg_ref)
        x2 = x_ref[...] + _dot(y_ref[...], w_ref[...])
        r = lax.rsqrt(jnp.mean(x2 * x2, axis=-1, keepdims=True) + RMS_EPS)
        n = x2 * r
        e = n * g_ref[...] - t_ref[...]
        loss_ref[...] += jnp.sum(e * e, axis=0, keepdims=True) * (0.5 / D_MODEL)
        dout = e * (1.0 / D_MODEL)
        dg_ref[...] += jnp.sum(dout * n, axis=0, keepdims=True)
        dn = dout * g_ref[...]
        dx_ref[...] = r * (dn - n * jnp.mean(dn * n, axis=-1, keepdims=True))

    vec = _full((1, D_MODEL))
    return pl.pallas_call(
        body, name="odd_out_loss", grid=(t // TM_MM,),
        in_specs=[_mm_rows(D_MODEL), _mm_rows(D_MODEL), _full((D_MODEL, D_MODEL)), vec, _mm_rows(D_MODEL)],
        out_specs=[_mm_rows(D_MODEL), vec, vec],
        out_shape=[jax.ShapeDtypeStruct((t, D_MODEL), F32), jax.ShapeDtypeStruct((1, D_MODEL), F32),
                   jax.ShapeDtypeStruct((1, D_MODEL), F32)],
        compiler_params=_params())(y2, x1, w, gf, tgt)


def _dsilu(z):
    s = _sig(z)
    return z * s, s * (1.0 + z * (1.0 - s))


def _odd_bwd_out(dx2, w, y2, cv, q, lg, lb):
    t = dx2.shape[0]

    def body(dx_ref, w_ref, y_ref, cv_ref, z_ref, lg_ref, lb_ref, dcv_ref, dz_ref, dw_ref, dlg_ref, dlb_ref):
        i = pl.program_id(0)
        _zero_at_first(i, dw_ref, dlg_ref, dlb_ref)
        dxb = dx_ref[...].astype(BF16)
        dy = _dot_nt(dxb, w_ref[...])
        dw_ref[...] += _dot_tn(y_ref[...], dxb)
        cv = cv_ref[...]
        cc = cv - jnp.mean(cv, axis=-1, keepdims=True)
        rstd = lax.rsqrt(jnp.mean(cc * cc, axis=-1, keepdims=True) + LN_EPS)
        cn = cc * rstd
        silu_c, dsilu_c = _dsilu(cn * lg_ref[...] + lb_ref[...])
        silu_z, dsilu_z = _dsilu(z_ref[...].astype(F32))
        dcl = dy * silu_z * dsilu_c
        dz_ref[...] = (dy * silu_c * dsilu_z).astype(BF16)
        dlg_ref[...] += jnp.sum(dcl * cn, axis=0, keepdims=True)
        dlb_ref[...] += jnp.sum(dcl, axis=0, keepdims=True)
        dcn = dcl * lg_ref[...]
        dcv_ref[...] = rstd * (dcn - jnp.mean(dcn, axis=-1, keepdims=True)
                               - cn * jnp.mean(dcn * cn, axis=-1, keepdims=True))

    vec = _full((1, D_MODEL))
    mat = _full((D_MODEL, D_MODEL))
    return pl.pallas_call(
        body, name="odd_bwd_out", grid=(t // TM_MM,),
        in_specs=[_mm_rows(D_MODEL), mat, _mm_rows(D_MODEL), _mm_rows(D_MODEL), _mm_rows(D_MODEL, 2), vec, vec],
        out_specs=[_mm_rows(D_MODEL), _mm_rows(D_MODEL), mat, vec, vec],
        out_shape=[jax.ShapeDtypeStruct((t, D_MODEL), F32), jax.ShapeDtypeStruct((t, D_MODEL), BF16),
                   jax.ShapeDtypeStruct((D_MODEL, D_MODEL), F32), jax.ShapeDtypeStruct((1, D_MODEL), F32),
                   jax.ShapeDtypeStruct((1, D_MODEL), F32)],
        compiler_params=_params())(dx2, w, y2, cv, q, lg, lb)


def _conv_bwd(dcv, q, cw):
    t = dcv.shape[0]
    nblk = t // TM

    def body(d_ref, dn_ref, v_ref, g_ref, hv_ref, hg_ref, w_ref,
             dv_ref, dgt_ref, dw_ref, db_ref, gext, dext, dgl, gcp, dcp):
        i = pl.program_id(0)
        last = nblk - 1
        _zero_at_first(i, dw_ref, db_ref)
        v = v_ref[...].astype(F32)
        sg = _sig(g_ref[...].astype(F32))
        gext[0:HALO, :] = jnp.where(i == 0, 0.0, hv_ref[...].astype(F32) * _sig(hg_ref[...].astype(F32)))
        gext[HALO:, :] = v * sg
        d = d_ref[...]
        dext[0:TM, :] = d
        dext[TM:, :] = jnp.where(i == last, 0.0, dn_ref[...])
        _phase_copies(gext, gcp)
        _phase_copies(dext, dcp)
        db_ref[...] += jnp.sum(d, axis=0, keepdims=True)
        def lanes(c, carry):
            sl = pl.ds(pl.multiple_of(c * 128, 128), 128)
            dgl[:, sl] = _conv_taps(dext, dcp, w_ref, lambda k: 30 - k, TM, sl, jnp.zeros((TM, 128), F32))
            return carry

        def lanes_w(c, carry):
            sl = pl.ds(pl.multiple_of(c * 128, 128), 128)
            ntile = TM // SUB
            dts = [d_ref[SUB * r:SUB * (r + 1), sl] for r in range(ntile)]
            for j in range(SUB):
                taps = [(q, SUB * q + j - 2) for q in range(5) if 0 <= SUB * q + j - 2 < CONV_K]
                sums = {k: None for _, k in taps}
                for rt in range(ntile + 4):
                    need = [(q, k) for q, k in taps if 0 <= rt - q < ntile]
                    if not need:
                        continue
                    src = gext[SUB * rt:SUB * (rt + 1), sl] if j == 0 else gcp[j - 1, SUB * rt:SUB * (rt + 1), sl]
                    for q, k in need:
                        prod = dts[rt - q] * src
                        sums[k] = prod if sums[k] is None else sums[k] + prod
                for _, k in taps:
                    dw_ref[SUB * k:SUB * (k + 1), sl] += sums[k]
            return carry

        lax.fori_loop(0, D_MODEL // 128, lanes, 0)
        lax.fori_loop(0, D_MODEL // 128, lanes_w, 0)
        dg = dgl[...]
        dv_ref[...] = (dg * sg).astype(BF16)
        dgt_ref[...] = (dg * v * sg * (1.0 - sg)).astype(BF16)

    return pl.pallas_call(
        body, name="conv_bwd", grid=(t // TM,),
        in_specs=[_rows(D_MODEL), _next(HALO, D_MODEL, t), _rows(D_MODEL, 0), _rows(D_MODEL, 1),
                  _prev(HALO, D_MODEL, 0), _prev(HALO, D_MODEL, 1), _full((HALO, D_MODEL))],
        out_specs=[_rows(D_MODEL), _rows(D_MODEL), _full((HALO * SUB, D_MODEL)), _full((1, D_MODEL))],
        out_shape=[jax.ShapeDtypeStruct((t, D_MODEL), BF16), jax.ShapeDtypeStruct((t, D_MODEL), BF16),
                   jax.ShapeDtypeStruct((HALO * SUB, D_MODEL), F32), jax.ShapeDtypeStruct((1, D_MODEL), F32)],
        scratch_shapes=[pltpu.VMEM((TM + HALO, D_MODEL), F32), pltpu.VMEM((TM + HALO, D_MODEL), F32),
                        pltpu.VMEM((TM, D_MODEL), F32),
                        pltpu.VMEM((SUB - 1, TM + HALO - SUB, D_MODEL), F32),
                        pltpu.VMEM((SUB - 1, TM + HALO - SUB, D_MODEL), F32)],
        compiler_params=_params())(dcv, dcv, q, q, q, q, cw)


def _column_segments(widths, ns):
    segs = []
    col = 0
    for p, wd in enumerate(widths):
        a = 0
        while a < wd:
            s, lo = divmod(col + a, ns)
            ln = min(wd - a, ns - lo)
            segs.append((p, a, a + ln, s, lo, lo + ln))
            a += ln
        col += wd
    return segs


def _in_bwd(dparts, w, x, g, dres, name, ride=None):
    t = x.shape[0]
    widths = [p.shape[1] for p in dparts]
    npart = len(dparts)
    segs = _column_segments(widths, w.shape[2])
    nstep = t // TM_MM

    def body(*refs):
        d_refs = refs[:npart]
        w_ref, x_ref, g_ref, r_ref = refs[npart:npart + 4]
        if ride is None:
            dx_ref, dg_ref, dw_ref = refs[npart + 4:]
        else:
            v_ref, j_ref, dx_ref, dg_ref, dw_ref, o_ref, jo_ref = refs[npart + 4:npart + 11]
            start, forward, finish = _gather_all_phases(v_ref, o_ref, *refs[npart + 11:npart + 15])
            j_start, j_finish = _pair_join_phases(j_ref, jo_ref, *refs[npart + 15:])
        i = pl.program_id(0)
        if ride is not None:
            pl.when(i == 0)(start)
            pl.when(i == 0)(j_start)
            pl.when(i == nstep // 2)(forward)
        _zero_at_first(i, dg_ref, dw_ref)
        xv = x_ref[...]
        r = lax.rsqrt(jnp.mean(xv * xv, axis=-1, keepdims=True) + RMS_EPS)
        n = xv * r
        h = (n * g_ref[...]).astype(BF16)
        dh = None
        for p, lo, hi, s, slo, shi in segs:
            d = d_refs[p][:, lo:hi]
            part = _dot_nt(d, w_ref[s, :, slo:shi])
            dh = part if dh is None else dh + part
            dw_ref[s, :, slo:shi] += _dot_tn(h, d)
        dg_ref[...] += jnp.sum(dh * n, axis=0, keepdims=True)
        dn = dh * g_ref[...]
        dx_ref[...] = r_ref[...] + r * (dn - n * jnp.mean(dn * n, axis=-1, keepdims=True))
        if ride is not None:
            pl.when(i == nstep - 1)(finish)
            pl.when(i == nstep - 1)(j_finish)

    vec = _full((1, D_MODEL))
    once = pl.BlockSpec(w.shape, lambda i: (0, 0, 0), pipeline_mode=pl.Buffered(1))
    extra = [] if ride is None else list(ride)
    return pl.pallas_call(
        body, name=name, grid=(nstep,),
        in_specs=[_mm_rows(wd) for wd in widths] + [once, _mm_rows(D_MODEL), vec, _mm_rows(D_MODEL)]
        + [ANY] * len(extra),
        out_specs=[_mm_rows(D_MODEL), vec, once] + [ANY] * len(extra),
        out_shape=[jax.ShapeDtypeStruct((t, D_MODEL), F32), jax.ShapeDtypeStruct((1, D_MODEL), F32),
                   jax.ShapeDtypeStruct(w.shape, F32)]
        + ([jax.ShapeDtypeStruct((8,) + ride[0].shape, ride[0].dtype),
            jax.ShapeDtypeStruct(ride[1].shape, ride[1].dtype)] if extra else []),
        scratch_shapes=(_gather_all_scratch(ride[0]) + _pair_join_scratch()) if extra else [],
        compiler_params=_params())(*dparts, w, x, g, dres, *extra)


def _even_bwd_out(dx1, w, yg, yp, ys, proj, wp, ps, ride):
    t = dx1.shape[0]
    nstep = t // TM_MM
    rows = TM_MM

    def body(dx_ref, w_ref, yg_ref, yp_ref, ys_ref, z_ref, u_ref, h_ref, wp_ref, ps_ref, g_ref,
             dy_ref, dz_ref, dw_ref, du_ref, dwp_ref, dps_ref, theirs_ref, nxt, send_sems, recv_sems):
        i = pl.program_id(0)
        blk = nstep - 1 - i
        start, finish = _pair_split_phases(g_ref, theirs_ref, send_sems, recv_sems)
        pl.when(i == 0)(start)
        _zero_at_first(i, dw_ref, dwp_ref, dps_ref, nxt)
        dxb = dx_ref[...].astype(BF16)
        dyg = _dot_nt(dxb, w_ref[...])
        dw_ref[...] += _dot_tn(yg_ref[...], dxb)
        silu_z, dsilu_z = _dsilu(z_ref[...].astype(F32))
        dyb = (dyg * silu_z).astype(BF16)
        dy_ref[...] = dyb
        dz_ref[:, :POOL_W] = (dyg[:, :POOL_W] * yp_ref[...].astype(F32) * dsilu_z[:, :POOL_W]).astype(BF16)
        dz_ref[:, POOL_W:] = (dyg[:, POOL_W:] * ys_ref[...].astype(F32) * dsilu_z[:, POOL_W:]).astype(BF16)
        pos = (blk * rows + 1 + lax.broadcasted_iota(jnp.int32, (rows, 1), 0)).astype(F32)
        pos_ext = (blk * rows + 1 + lax.broadcasted_iota(jnp.int32, (rows + POOL_HALO, 1), 0)).astype(F32)
        for gi in range(4):
            sl = slice(128 * gi, 128 * (gi + 1))
            wd = float(2 << gi)
            u = u_ref[:, sl].astype(F32)
            halo = jnp.where(blk == 0, 0.0, h_ref[:, sl].astype(F32))
            s = _pool_sums(jnp.concatenate([halo, u], axis=0), gi, True)[POOL_HALO:, :]
            pooled = (s / jnp.minimum(pos, wd) - u).astype(BF16)
            dy = dyb[:, sl].astype(F32)
            dps_ref[:, sl] += jnp.sum(dy * _dot(pooled, wp_ref[gi]), axis=0, keepdims=True)
            dmix = (jnp.concatenate([dy, nxt[:, sl]], axis=0) * ps_ref[:, sl]).astype(BF16)
            dwp_ref[gi] += _dot_tn(pooled, dmix[:rows, :])
            dpool = _dot_nt(dmix, wp_ref[gi])
            lead = _pool_sums(dpool / jnp.minimum(pos_ext, wd), gi, False)
            du_ref[:, sl] = (lead[:rows, :] - dpool[:rows, :]).astype(BF16)
            nxt[:, sl] = dy[:POOL_HALO, :]
        pl.when(i == nstep - 1)(finish)

    back = lambda wdt, cb=0: _rows(wdt, cb, rev=nstep, tm=rows)
    per = rows // POOL_HALO
    halo = pl.BlockSpec((POOL_HALO, POOL_W), lambda i: (jnp.maximum((nstep - 1 - i) * per - 1, 0), 0))
    mat = _full((D_MODEL, D_MODEL))
    return pl.pallas_call(
        body, name="even_bwd_out", grid=(nstep,),
        in_specs=[back(D_MODEL), mat, back(D_MODEL), back(POOL_W), back(SSM_W), back(D_MODEL, 1),
                  back(POOL_W, 0), halo, _full((4, 128, 128)), _full((1, POOL_W)), ANY],
        out_specs=[back(D_MODEL), back(D_MODEL), mat, back(POOL_W), _full((4, 128, 128)), _full((1, POOL_W)), ANY],
        out_shape=[jax.ShapeDtypeStruct((t, D_MODEL), BF16), jax.ShapeDtypeStruct((t, D_MODEL), BF16),
                   jax.ShapeDtypeStruct((D_MODEL, D_MODEL), F32), jax.ShapeDtypeStruct((t, POOL_W), BF16),
                   jax.ShapeDtypeStruct((4, 128, 128), F32), jax.ShapeDtypeStruct((1, POOL_W), F32),
                   _pair_split_shape(ride)],
        scratch_shapes=[pltpu.VMEM((POOL_HALO, POOL_W), F32)] + _pair_split_scratch(ride),
        compiler_params=_params())(dx1, w, yg, yp, ys, proj, proj, proj, wp, ps, ride)


def _ssm_bwd(dycat, proj, car_in_re, car_in_im, pm, pmt, mb_re, mb_im, p8_re, p8_im, q_re, q_im, qr_re, qr_im,
             cm_re, cm_im, dskip, wglu, ride):
    t = proj.shape[0]
    nblk = t // TM

    def body(dy_ref, u_ref, cin_re, cin_im, pm_ref, pmt_ref, mbre, mbim, p8re, p8im, qre, qim, qrre, qrim,
             cmre, cmim, d_ref, wg_ref, p_ref,
             du_ref, dmbre, dmbim, dcmre, dcmim, dare, daim, dd_ref, dwg_ref, got_ref,
             xs_re, xs_im, gs_re, gs_im, car_re, car_im, ent_re, ent_im, gcar_re, gcar_im, ysk, dysk,
             bounce, send_sems, recv_sems, local_sems):
        i = pl.program_id(0)
        start, finish = _chip_scatter_phases(p_ref, got_ref, bounce, send_sems, recv_sems, local_sems)
        pl.when(i == 0)(start)
        _zero_at_first(i, dmbre, dmbim, dcmre, dcmim, dare, daim, dd_ref, dwg_ref, gcar_re, gcar_im)
        us = _dot(pm_ref[...], u_ref[...])
        usb = us.astype(BF16)
        for j in range(4):
            xs_re[:, LCH * j:LCH * (j + 1)] = _dot(usb[:, 128 * j:128 * (j + 1)], mbre[j])
            xs_im[:, LCH * j:LCH * (j + 1)] = _dot(usb[:, 128 * j:128 * (j + 1)], mbim[j])
        car_re[...] = cin_re[0]
        car_im[...] = cin_im[0]
        _scan_fwd_block(xs_re, xs_im, p8re, p8im, qre, qim, car_re, car_im, ent_re, ent_im)
        for j in range(4):
            sl = slice(LCH * j, LCH * (j + 1))
            ysk[:, 128 * j:128 * (j + 1)] = (_dot_nt(xs_re[:, sl].astype(BF16), cmre[j])
                                             - _dot_nt(xs_im[:, sl].astype(BF16), cmim[j]))
        yvb = (ysk[...] + d_ref[...] * us).astype(BF16)
        gv = _dot(yvb, wg_ref[...])
        sg = _sig(gv[:, SSM_W:])
        dyss = _dot(pm_ref[...], dy_ref[...])
        dval = (dyss * sg).astype(BF16)
        dgate = (dyss * gv[:, :SSM_W] * sg * (1.0 - sg)).astype(BF16)
        dy = _dot_nt(dval, wg_ref[:, :SSM_W]) + _dot_nt(dgate, wg_ref[:, SSM_W:])
        dwg_ref[:, :SSM_W] += _dot_tn(yvb, dval)
        dwg_ref[:, SSM_W:] += _dot_tn(yvb, dgate)
        dd_ref[...] += jnp.sum(dy * us, axis=0, keepdims=True)
        dysk[...] = dy
        for j in range(4):
            sl = slice(LCH * j, LCH * (j + 1))
            dyj = dy[:, 128 * j:128 * (j + 1)].astype(BF16)
            gs_re[:, sl] = _dot(dyj, cmre[j])
            gs_im[:, sl] = -_dot(dyj, cmim[j])
            dcmre[j] += _dot_tn(dyj, xs_re[:, sl].astype(BF16))
            dcmim[j] -= _dot_tn(dyj, xs_im[:, sl].astype(BF16))
        row = lax.broadcasted_iota(jnp.int32, (SUB, SCAN_L), 0)
        for j in range(STATES // SCAN_L):
            sl = slice(SCAN_L * j, SCAN_L * (j + 1))
            are, aim = p8re[0:SUB, sl], -p8im[0:SUB, sl]

            def totals(k, v, sl=sl, are=are, aim=aim):
                r0 = pl.multiple_of((SEG_LEN - 2 - k) * SUB, SUB)
                vre, vim = _cmul_add(are, aim, v[0], v[1], gs_re[pl.ds(r0, SUB), sl], gs_im[pl.ds(r0, SUB), sl])
                gs_re[pl.ds(r0, SUB), sl] = vre
                gs_im[pl.ds(r0, SUB), sl] = vim
                return vre, vim

            top = (SEG_LEN - 1) * SUB
            fre, fim = lax.fori_loop(0, SEG_LEN - 1, totals,
                                     (gs_re[top:top + SUB, sl], gs_im[top:top + SUB, sl]), unroll=2)
            fre, fim, nre, nim = _segment_chain(fre, fim, qrre[:, sl], -qrim[:, sl],
                                                gcar_re[:, sl], gcar_im[:, sl], row, True)
            gcar_re[:, sl] = jnp.broadcast_to(fre[0:1, :], (SUB, SCAN_L))
            gcar_im[:, sl] = jnp.broadcast_to(fim[0:1, :], (SUB, SCAN_L))

            def fix(i2, acc, sl=sl, nre=nre, nim=nim):
                r0 = pl.multiple_of(i2 * SUB, SUB)
                rb = pl.multiple_of((SEG_LEN - 1 - i2) * SUB, SUB)
                gre, gim = _cmul_add(p8re[pl.ds(rb, SUB), sl], -p8im[pl.ds(rb, SUB), sl], nre, nim,
                                     gs_re[pl.ds(r0, SUB), sl], gs_im[pl.ds(r0, SUB), sl])
                gs_re[pl.ds(r0, SUB), sl] = gre
                gs_im[pl.ds(r0, SUB), sl] = gim
                rp = pl.multiple_of((i2 - 1) * SUB, SUB)
                xre, xim = xs_re[pl.ds(rp, SUB), sl], xs_im[pl.ds(rp, SUB), sl]
                return acc[0] + gre * xre + gim * xim, acc[1] + gim * xre - gre * xim

            g0re, g0im = _cmul_add(p8re[top:top + SUB, sl], -p8im[top:top + SUB, sl], nre, nim,
                                   gs_re[0:SUB, sl], gs_im[0:SUB, sl])
            gs_re[0:SUB, sl] = g0re
            gs_im[0:SUB, sl] = g0im
            ere, eim = ent_re[:, sl], ent_im[:, sl]
            acc0 = (dare[:, sl] + g0re * ere + g0im * eim, daim[:, sl] + g0im * ere - g0re * eim)
            are_acc, aim_acc = lax.fori_loop(1, SEG_LEN, fix, acc0, unroll=2)
            dare[:, sl] = are_acc
            daim[:, sl] = aim_acc
        for j in range(4):
            sl = slice(LCH * j, LCH * (j + 1))
            c4 = slice(128 * j, 128 * (j + 1))
            gre = gs_re[:, sl].astype(BF16)
            gim = gs_im[:, sl].astype(BF16)
            dmbre[j] += _dot_tn(usb[:, c4], gre)
            dmbim[j] += _dot_tn(usb[:, c4], gim)
            dysk[:, c4] = _dot_nt(gre, mbre[j]) + _dot_nt(gim, mbim[j]) + dysk[:, c4] * d_ref[:, c4]
        du_ref[...] = _dot(pmt_ref[...], dysk[...].astype(BF16)).astype(BF16)
        pl.when(i == nblk - 1)(finish)

    blk = (4, 128, LCH)
    pw = _full((SUB, STATES))
    p8 = _full((TM, STATES))
    perm = _full((TM, TM))
    car = pl.BlockSpec((1, SUB, STATES), lambda i: (nblk - 1 - i, 0, 0))
    big = lambda: pltpu.VMEM((TM, STATES), F32)
    small = lambda: pltpu.VMEM((SUB, STATES), F32)
    return pl.pallas_call(
        body, name="ssm_bwd", grid=(nblk,),
        in_specs=[_rows(SSM_W, 1, rev=nblk), _rows(SSM_W, 1, rev=nblk), car, car, perm, perm, _full(blk), _full(blk),
                  p8, p8, pw, pw, pw, pw, _full(blk), _full(blk), _full((1, SSM_W)), _full((SSM_W, 2 * SSM_W)), ANY],
        out_specs=[_rows(SSM_W, 0, rev=nblk), _full(blk), _full(blk), _full(blk), _full(blk), pw, pw,
                   _full((1, SSM_W)), _full((SSM_W, 2 * SSM_W)), ANY],
        out_shape=[jax.ShapeDtypeStruct((t, SSM_W), BF16)] + [jax.ShapeDtypeStruct(blk, F32)] * 4
        + [jax.ShapeDtypeStruct((SUB, STATES), F32)] * 2
        + [jax.ShapeDtypeStruct((1, SSM_W), F32), jax.ShapeDtypeStruct((SSM_W, 2 * SSM_W), F32),
           jax.ShapeDtypeStruct(ride.shape, ride.dtype)],
        scratch_shapes=[big(), big(), big(), big(), small(), small(), small(), small(), small(), small(),
                        pltpu.VMEM((TM, SSM_W), F32), pltpu.VMEM((TM, SSM_W), F32)] + _chip_scatter_scratch(ride),
        compiler_params=_params())(dycat, proj, car_in_re, car_in_im, pm, pmt, mb_re, mb_im, p8_re, p8_im,
                                   q_re, q_im, qr_re, qr_im, cm_re, cm_im, dskip, wglu, ride)


def _adamw(w, g, m, v, name):
    rows = w.shape[0]
    tr = 256 if rows % 256 == 0 else rows
    c1 = 1.0 / (1.0 - ADAM_B1 ** ADAM_STEP)
    c2 = 1.0 / (1.0 - ADAM_B2 ** ADAM_STEP)

    def body(w_ref, g_ref, m_ref, v_ref, d_ref, nm_ref, nv_ref):
        gv = g_ref[...]
        m = ADAM_B1 * m_ref[...] + (1.0 - ADAM_B1) * gv
        v = ADAM_B2 * v_ref[...] + (1.0 - ADAM_B2) * (gv * gv)
        nm_ref[...] = m
        nv_ref[...] = v
        d_ref[...] = -ADAM_LR * ((m * c1) / (jnp.sqrt(v * c2) + ADAM_EPS) + ADAM_WD * w_ref[...])

    spec = pl.BlockSpec((tr, D_MODEL), lambda i: (i, 0))
    shp = jax.ShapeDtypeStruct((rows, D_MODEL), F32)
    return pl.pallas_call(
        body, name=name, grid=(rows // tr,), in_specs=[spec] * 4, out_specs=[spec] * 3, out_shape=[shp] * 3,
        compiler_params=_params())(w, g, m, v)


def _core_index():
    return lax.axis_index("c").astype(jnp.int32).reshape(1)


def _pair_add(g, theirs, out_dtype, name):
    n, half, _ = theirs.shape
    br = 128
    nb = half // br

    def body(c_ref, a_ref, b_ref, o_ref):
        o_ref[...] = (a_ref[...] + b_ref[...]).astype(out_dtype)

    spec = pl.BlockSpec((1, br, D_MODEL), lambda i, j, c: (i, j, 0))
    grid_spec = pltpu.PrefetchScalarGridSpec(
        num_scalar_prefetch=1, grid=(n, nb),
        in_specs=[pl.BlockSpec((1, br, D_MODEL), lambda i, j, c: (i, c[0] * nb + j, 0)), spec], out_specs=spec)
    return pl.pallas_call(
        body, name=name, grid_spec=grid_spec, out_shape=jax.ShapeDtypeStruct(theirs.shape, out_dtype),
        compiler_params=_params(2))(_core_index(), g, theirs)


def _adamw_rows(w, m, v, g_mine, g_theirs, row0, br, name):
    rows = w.shape[0]
    b0 = row0 // br
    per_half = g_mine.shape[0] // br
    c1 = 1.0 / (1.0 - ADAM_B1 ** ADAM_STEP)
    c2 = 1.0 / (1.0 - ADAM_B2 ** ADAM_STEP)

    def body(c_ref, w_ref, gm_ref, gt_ref, m_ref, v_ref, g_ref, d_ref, nm_ref, nv_ref):
        gv = jnp.where((b0 + pl.program_id(0)) // per_half == c_ref[0], gm_ref[...], gt_ref[...])
        m = ADAM_B1 * m_ref[...] + (1.0 - ADAM_B1) * gv
        v = ADAM_B2 * v_ref[...] + (1.0 - ADAM_B2) * (gv * gv)
        g_ref[...] = gv
        nm_ref[...] = m
        nv_ref[...] = v
        d_ref[...] = -ADAM_LR * ((m * c1) / (jnp.sqrt(v * c2) + ADAM_EPS) + ADAM_WD * w_ref[...])

    spec = pl.BlockSpec((br, D_MODEL), lambda i, c: (i, 0))
    part = pl.BlockSpec((br, D_MODEL), lambda i, c: ((b0 + i) % per_half, 0))
    shp = jax.ShapeDtypeStruct((rows, D_MODEL), F32)
    grid_spec = pltpu.PrefetchScalarGridSpec(
        num_scalar_prefetch=1, grid=(rows // br,), in_specs=[spec, part, part, spec, spec], out_specs=[spec] * 4)
    return pl.pallas_call(
        body, name=name, grid_spec=grid_spec, out_shape=[shp] * 4,
        compiler_params=_params())(_core_index(), w, g_mine, g_theirs, m, v)


def _sum_lead(a, name):
    n, rows, _ = a.shape
    tr = 128 if rows % 128 == 0 else rows

    def body(a_ref, o_ref):
        acc = a_ref[0].astype(F32)
        for k in range(1, n):
            acc = acc + a_ref[k].astype(F32)
        o_ref[...] = acc

    return pl.pallas_call(
        body, name=name, grid=(rows // tr,),
        in_specs=[pl.BlockSpec((n, tr, D_MODEL), lambda i: (0, i, 0))],
        out_specs=pl.BlockSpec((tr, D_MODEL), lambda i: (i, 0)),
        out_shape=jax.ShapeDtypeStruct((rows, D_MODEL), F32), compiler_params=_params())(a)


ANY = pl.BlockSpec(memory_space=pl.ANY)


def _mesh_pos():
    return lax.axis_index("x"), lax.axis_index("y"), lax.axis_index("c")


def _gather_phases(in_refs, out_refs, bounces, send_sems, recv_sems, local_sems):
    na = len(in_refs)
    halves = [r.shape[0] // 2 for r in in_refs]
    ncopy = 3 * na
    x, y, c = _mesh_pos()
    me = 2 * x + y
    sibling = (x, y, 1 - c)
    chips = [(1 - x, y), (x, 1 - y), (1 - x, 1 - y)]
    ids = [2 * chip[0] + chip[1] for chip in chips]

    def piece(a, q, h):
        return out_refs[a].at[q, pl.ds(h * halves[a], halves[a]), :]

    def copy(s, a, q, h, to, src=None):
        return pltpu.make_async_remote_copy(
            src_ref=piece(a, q, h) if src is None else src, dst_ref=piece(a, q, h),
            send_sem=send_sems.at[s], recv_sem=recv_sems.at[s], device_id=to, device_id_type=MESH)

    def sends():
        return [copy(j * na + a, a, me, c, (*chip, c), src=in_refs[a].at[pl.ds(c * halves[a], halves[a]), :])
                for j, chip in enumerate(chips) for a in range(na)]

    def forwards():
        return [copy(ncopy + j * na + a, a, ids[j], c, sibling) for j in range(3) for a in range(na)]

    def stores():
        return [pltpu.make_async_copy(bounces[a], out_refs[a].at[me], local_sems.at[na + a]) for a in range(na)]

    def start():
        loads = [pltpu.make_async_copy(in_refs[a], bounces[a], local_sems.at[a]) for a in range(na)]
        for cp in loads:
            cp.start()
        for cp in sends():
            cp.start()
        for ld, st in zip(loads, stores()):
            ld.wait()
            st.start()

    def forward():
        fwd = forwards()
        for j in range(3):
            for a in range(na):
                copy(j * na + a, a, ids[j], c, (x, y, c)).wait_recv()
                fwd[j * na + a].start()

    def finish():
        for j in range(3):
            for a in range(na):
                copy(ncopy + j * na + a, a, ids[j], 1 - c, (x, y, c)).wait_recv()
        for cp in sends() + forwards():
            cp.wait_send()
        for cp in stores():
            cp.wait()

    return start, forward, finish


def _gather_scratch(arrs):
    ncopy = 3 * len(arrs)
    return ([pltpu.VMEM(a.shape, a.dtype) for a in arrs]
            + [pltpu.SemaphoreType.DMA((2 * ncopy,)), pltpu.SemaphoreType.DMA((2 * ncopy,)),
               pltpu.SemaphoreType.DMA((2 * len(arrs),))])


def _gather_weights(arrs):
    na = len(arrs)

    def body(*refs):
        start, forward, finish = _gather_phases(refs[:na], refs[na:2 * na], refs[2 * na:3 * na], *refs[3 * na:])
        start()
        forward()
        finish()

    return pl.pallas_call(
        body, name="gather_weights", in_specs=[ANY] * na, out_specs=[ANY] * na,
        out_shape=[jax.ShapeDtypeStruct((4,) + a.shape, a.dtype) for a in arrs],
        scratch_shapes=_gather_scratch(arrs),
        compiler_params=pltpu.CompilerParams(vmem_limit_bytes=VMEM_LIMIT),
    )(*arrs)


def _gather_all_phases(v_ref, o_ref, bounce, send_sems, recv_sems, local_sems):
    x, y, c = _mesh_pos()
    sibling = (x, y, 1 - c)
    chips = [(1 - x, y), (x, 1 - y), (1 - x, 1 - y)]

    def blk(px, py, pc):
        return o_ref.at[4 * px + 2 * py + pc]

    def copy(k, block, to, src=None):
        return pltpu.make_async_remote_copy(
            src_ref=blk(*block) if src is None else src, dst_ref=blk(*block),
            send_sem=send_sems.at[k], recv_sem=recv_sems.at[k], device_id=to, device_id_type=MESH)

    def first():
        return ([copy(0, (x, y, c), sibling, src=v_ref)]
                + [copy(1 + j, (x, y, c), (*chip, c), src=v_ref) for j, chip in enumerate(chips)])

    def passed():
        return [copy(4 + j, (*chip, c), sibling) for j, chip in enumerate(chips)]

    def store():
        return pltpu.make_async_copy(bounce, blk(x, y, c), local_sems.at[1])

    def start():
        load = pltpu.make_async_copy(v_ref, bounce, local_sems.at[0])
        load.start()
        for cp in first():
            cp.start()
        load.wait()
        store().start()

    def forward():
        fwd = passed()
        for j, chip in enumerate(chips):
            copy(1 + j, (*chip, c), (x, y, c)).wait_recv()
            fwd[j].start()

    def finish():
        copy(0, (x, y, 1 - c), (x, y, c)).wait_recv()
        for j, chip in enumerate(chips):
            copy(4 + j, (*chip, 1 - c), (x, y, c)).wait_recv()
        for cp in first() + passed():
            cp.wait_send()
        store().wait()

    return start, forward, finish


def _gather_all_scratch(v):
    return [pltpu.VMEM(v.shape, v.dtype), pltpu.SemaphoreType.DMA((7,)), pltpu.SemaphoreType.DMA((7,)),
            pltpu.SemaphoreType.DMA((2,))]


def _scatter_and_gather(p, v, name):
    def body(p_ref, v_ref, got_ref, o_ref, p_bounce, p_send, p_recv, p_local, bounce, send_sems, recv_sems,
             local_sems):
        start, finish = _chip_scatter_phases(p_ref, got_ref, p_bounce, p_send, p_recv, p_local)
        g_start, g_forward, g_finish = _gather_all_phases(v_ref, o_ref, bounce, send_sems, recv_sems, local_sems)
        start()
        g_start()
        g_forward()
        g_finish()
        finish()

    return pl.pallas_call(
        body, name=name, in_specs=[ANY, ANY], out_specs=[ANY, ANY],
        out_shape=[jax.ShapeDtypeStruct(p.shape, p.dtype), jax.ShapeDtypeStruct((8,) + v.shape, v.dtype)],
        scratch_shapes=_chip_scatter_scratch(p) + _gather_all_scratch(v),
    )(p, v)


def _pair_split_phases(g_ref, theirs_ref, send_sems, recv_sems):
    n, rows, _ = g_ref.shape
    half = rows // 2
    ch = half // COMM_CHUNKS
    x, y, c = _mesh_pos()

    def gives():
        return [pltpu.make_async_remote_copy(
            src_ref=g_ref.at[q, pl.ds((1 - c) * half + k * ch, ch), :],
            dst_ref=theirs_ref.at[q, pl.ds(k * ch, ch), :],
            send_sem=send_sems.at[q * COMM_CHUNKS + k], recv_sem=recv_sems.at[q * COMM_CHUNKS + k],
            device_id=(x, y, 1 - c), device_id_type=MESH) for q in range(n) for k in range(COMM_CHUNKS)]

    def start():
        for cp in gives():
            cp.start()

    def finish():
        for cp in gives():
            cp.wait()

    return start, finish


def _pair_split_scratch(g):
    return [pltpu.SemaphoreType.DMA((g.shape[0] * COMM_CHUNKS,)), pltpu.SemaphoreType.DMA((g.shape[0] * COMM_CHUNKS,))]


def _pair_split_shape(g):
    return jax.ShapeDtypeStruct((g.shape[0], g.shape[1] // 2, D_MODEL), g.dtype)


def _pair_split(g, name):
    def body(g_ref, theirs_ref, send_sems, recv_sems):
        start, finish = _pair_split_phases(g_ref, theirs_ref, send_sems, recv_sems)
        start()
        finish()

    return pl.pallas_call(
        body, name=name, in_specs=[ANY], out_specs=ANY, out_shape=_pair_split_shape(g),
        scratch_shapes=_pair_split_scratch(g))(g)


def _chip_scatter_phases(p_ref, o_ref, bounce, send_sems, recv_sems, local_sems):
    x, y, c = _mesh_pos()
    me = 2 * x + y
    chips = [(1 - x, y), (x, 1 - y), (1 - x, 1 - y)]

    def keep():
        return pltpu.make_async_copy(bounce, o_ref.at[me], local_sems.at[1])

    def sends():
        return [pltpu.make_async_remote_copy(
            src_ref=p_ref.at[2 * chip[0] + chip[1]], dst_ref=o_ref.at[me],
            send_sem=send_sems.at[j], recv_sem=recv_sems.at[j], device_id=(*chip, c), device_id_type=MESH)
            for j, chip in enumerate(chips)]

    def start():
        load = pltpu.make_async_copy(p_ref.at[me], bounce, local_sems.at[0])
        load.start()
        for cp in sends():
            cp.start()
        load.wait()
        keep().start()

    def finish():
        for j, chip in enumerate(chips):
            q = 2 * chip[0] + chip[1]
            pltpu.make_async_remote_copy(
                src_ref=p_ref.at[q], dst_ref=o_ref.at[q], send_sem=send_sems.at[j], recv_sem=recv_sems.at[j],
                device_id=(*chip, c), device_id_type=MESH).wait_recv()
        for cp in sends():
            cp.wait_send()
        keep().wait()

    return start, finish


def _chip_scatter_scratch(p):
    return [pltpu.VMEM(p.shape[1:], p.dtype), pltpu.SemaphoreType.DMA((3,)), pltpu.SemaphoreType.DMA((3,)),
            pltpu.SemaphoreType.DMA((2,))]


def _pair_join_phases(r_ref, o_ref, send_sems, recv_sems):
    ch = r_ref.shape[0] // COMM_CHUNKS
    x, y, c = _mesh_pos()

    def gives():
        return [pltpu.make_async_remote_copy(
            src_ref=r_ref.at[pl.ds(k * ch, ch), :], dst_ref=o_ref.at[pl.ds(k * ch, ch), :],
            send_sem=send_sems.at[k], recv_sem=recv_sems.at[k], device_id=(x, y, 1 - c), device_id_type=MESH)
            for k in range(COMM_CHUNKS)]

    def start():
        for cp in gives():
            cp.start()

    def finish():
        for cp in gives():
            cp.wait()

    return start, finish


def _pair_join_scratch():
    return [pltpu.SemaphoreType.DMA((COMM_CHUNKS,)), pltpu.SemaphoreType.DMA((COMM_CHUNKS,))]


def _pair_join(r, name):
    def body(r_ref, o_ref, send_sems, recv_sems):
        start, finish = _pair_join_phases(r_ref, o_ref, send_sems, recv_sems)
        start()
        finish()

    return pl.pallas_call(
        body, name=name, in_specs=[ANY], out_specs=ANY, out_shape=jax.ShapeDtypeStruct(r.shape, r.dtype),
        scratch_shapes=_pair_join_scratch())(r)


SHARD_BIG = (("even_w_in", (1024, 512)), ("ssm_w_glu", (512, 256)), ("even_w_out", (256, 1024)),
             ("odd_w_in", (1024, 768)), ("odd_w_out", (256, 1024)))
SHARD_SMALL = (("odd_norm", 1), ("conv_w", CONV_K), ("conv_b", 1), ("conv_ln_g", 1), ("conv_ln_b", 1))
REP_NAMES = (("even_norm", (1024,)), ("pool_w", (4, 128, 128)), ("pool_scale", (512,)), ("ssm_log_dt", (32,)),
             ("ssm_a_re", (32, 64)), ("ssm_a_im", (32, 64)), ("ssm_b_re", (32, 64, 16)), ("ssm_b_im", (32, 64, 16)),
             ("ssm_c_re", (32, 16, 64)), ("ssm_c_im", (32, 16, 64)), ("ssm_d", (512,)), ("final_norm", (1024,)))


def _pack_rep(d):
    flat = jnp.concatenate([d[n].reshape(-1) for n, _ in REP_NAMES])
    return jnp.pad(flat, (0, REP_ROWS * D_MODEL - flat.shape[0])).reshape(REP_ROWS, D_MODEL)


def _unpack_rep(buf):
    flat = buf.reshape(-1)
    out = {}
    off = 0
    for n, shp in REP_NAMES:
        size = 1
        for s in shp:
            size *= s
        out[n] = flat[off:off + size].reshape(shp)
        off += size
    return out


def _cols_split(full, cols):
    rows = full.shape[0]
    return full.reshape(rows, 4, cols).transpose(1, 0, 2).reshape(4, -1, D_MODEL)


def _block_diag(a):
    a = a.reshape(4, 8, GROUP_DIM, N_STATE)
    eye = jnp.eye(8, dtype=a.dtype)
    return (a[:, :, :, None, :] * eye[None, :, None, :, None]).reshape(4, 128, LCH)


def _block_diag_take(m):
    m = m.reshape(4, 8, GROUP_DIM, 8, N_STATE)
    eye = jnp.eye(8, dtype=m.dtype)
    return jnp.sum(m * eye[None, :, None, :, None], axis=3).reshape(N_GROUPS, GROUP_DIM, N_STATE)


def _ssm_discretise(log_dt, a_re, a_im, b_re, b_im):
    dt = jnp.exp(log_dt)[:, None]
    mag = jnp.exp(a_re * dt)
    ang = a_im * dt
    abar_re = mag * jnp.cos(ang)
    abar_im = mag * jnp.sin(ang)
    den = a_re * a_re + a_im * a_im
    nr = abar_re - 1.0
    ni = abar_im
    k_re = (nr * a_re + ni * a_im) / den
    k_im = (ni * a_re - nr * a_im) / den
    bb_re = k_re[..., None] * b_re - k_im[..., None] * b_im
    bb_im = k_re[..., None] * b_im + k_im[..., None] * b_re
    return abar_re, abar_im, bb_re, bb_im


def _scan_tables(log_dt, a_re, a_im):
    dt = jnp.exp(log_dt)[:, None]
    lam_re = (a_re * dt).reshape(1, STATES)
    lam_im = (a_im * dt).reshape(1, STATES)

    def powers(k):
        mag = jnp.exp(k * lam_re)
        return mag * jnp.cos(k * lam_im), mag * jnp.sin(k * lam_im)

    p_re, p_im = powers((1 + jnp.arange(TM) // SUB).astype(F32)[:, None])
    q_re, q_im = powers((SEG_LEN * (1 + jnp.arange(SUB))).astype(F32)[:, None])
    return p_re, p_im, q_re, q_im


def _local_step(x, tgt, w, shard):
    row = lambda a: a.reshape(1, -1)
    (e_w_in,) = _gather_weights([shard["even_w_in"].astype(BF16)])
    wp = w["pool_w"].astype(BF16)
    ssm_in = (w["ssm_log_dt"], w["ssm_a_re"], w["ssm_a_im"], w["ssm_b_re"], w["ssm_b_im"])
    (abar_re, abar_im, bb_re, bb_im), ssm_vjp = jax.vjp(_ssm_discretise, *ssm_in)
    mb_re = _block_diag(bb_re.transpose(0, 2, 1)).astype(BF16)
    mb_im = _block_diag(bb_im.transpose(0, 2, 1)).astype(BF16)
    cm_re = _block_diag(w["ssm_c_re"]).astype(BF16)
    cm_im = _block_diag(w["ssm_c_im"]).astype(BF16)
    p8_re, p8_im, q_re, q_im = _scan_tables(w["ssm_log_dt"], w["ssm_a_re"], w["ssm_a_im"])
    qr_re, qr_im = q_re[::-1], q_im[::-1]
    pm = _perm_matrix()
    pmt = pm.T
    g0, gf = row(w["even_norm"]), row(w["final_norm"])
    ps, dskip = row(w["pool_scale"]), row(w["ssm_d"])

    proj, yp, (g_glu, g_eout) = _norm_in(
        x, g0, e_w_in, "even_in", ride=[shard["ssm_w_glu"].astype(BF16), shard["even_w_out"].astype(BF16)],
        pool=(wp, ps))
    wglu = g_glu.transpose(1, 0, 2).reshape(SSM_W, 2 * SSM_W)
    e_w_out = g_eout.reshape(D_MODEL, D_MODEL)
    (ys, car_re, car_im), (g_oin, g_oout, g_small) = _ssm_fwd(
        proj, pm, pmt, mb_re, mb_im, p8_re, p8_im, q_re, q_im, cm_re, cm_im, dskip, wglu, ride=_odd_shards(shard))
    o_w_in, o_w_out = g_oin, g_oout.reshape(D_MODEL, D_MODEL)
    sm = g_small.transpose(1, 0, 2).reshape(SMALL_ROWS, D_MODEL)
    cw = sm[1:1 + HALO]
    g1, cb, lg, lb = sm[0:1], sm[32:33], sm[33:34], sm[34:35]
    x1, yg = _even_out(yp, ys, proj, x, e_w_out)
    q, _, _ = _norm_in(x1, g1, o_w_in, "odd_in")
    y2, cv = _conv_fwd(q, cw, cb, lg, lb)
    dx2, loss_lanes, d_gf = _odd_out_loss(y2, x1, o_w_out, gf, tgt)

    dcv, dz2, d_o_w_out, d_lg, d_lb = _odd_bwd_out(dx2, o_w_out, y2, cv, q, lg, lb)
    dval, dgate, d_cw, d_cb = _conv_bwd(dcv, q, cw)
    dx1, d_g1, d_o_w_in = _in_bwd([dval, dgate, dz2], o_w_in, x1, g1, dx2, "odd_in_bwd")
    g_odd = _pack_odd_grads({
        "odd_w_in": d_o_w_in, "odd_w_out": d_o_w_out, "odd_norm": d_g1.reshape(-1),
        "conv_w": d_cw.reshape(HALO, SUB, D_MODEL).sum(axis=1)[:CONV_K], "conv_b": d_cb.reshape(-1),
        "conv_ln_g": d_lg.reshape(-1), "conv_ln_b": d_lb.reshape(-1)})
    dycat, dz, d_e_w_out, dup, d_wp, d_ps, theirs_odd = _even_bwd_out(dx1, e_w_out, yg, yp, ys, proj, wp, ps, g_odd)
    sums_odd = _pair_add(g_odd, theirs_odd, BF16, "pair_add_odd")
    (dus, d_mb_re, d_mb_im, d_cm_re, d_cm_im, da_re, da_im, d_dskip, d_wglu, got_odd) = _ssm_bwd(
        dycat, proj, car_re, car_im, pm, pmt, mb_re, mb_im, p8_re, p8_im, q_re, q_im, qr_re, qr_im,
        cm_re, cm_im, dskip, wglu, sums_odd)
    d_abar_re = jnp.sum(da_re, axis=0).reshape(N_GROUPS, N_STATE)
    d_abar_im = jnp.sum(da_im, axis=0).reshape(N_GROUPS, N_STATE)
    d_bb_re = _block_diag_take(d_mb_re).transpose(0, 2, 1)
    d_bb_im = _block_diag_take(d_mb_im).transpose(0, 2, 1)
    d_log_dt, d_a_re, d_a_im, d_b_re, d_b_im = ssm_vjp((d_abar_re, d_abar_im, d_bb_re, d_bb_im))
    rep_early = _pack_rep({
        "even_norm": jnp.zeros((D_MODEL,), F32), "pool_w": d_wp, "pool_scale": d_ps.reshape(-1),
        "ssm_log_dt": d_log_dt, "ssm_a_re": d_a_re, "ssm_a_im": d_a_im, "ssm_b_re": d_b_re, "ssm_b_im": d_b_im,
        "ssm_c_re": _block_diag_take(d_cm_re), "ssm_c_im": _block_diag_take(d_cm_im),
        "ssm_d": d_dskip.reshape(-1), "final_norm": d_gf.reshape(-1)})
    odd_mine = _sum_lead(got_odd, "chip_sum_odd")
    dx, d_g0, d_e_w_in, rep_parts, odd_theirs = _in_bwd([dup, dus, dz], e_w_in, x, g0, dx1, "even_in_bwd",
                                                        ride=(rep_early, odd_mine))

    grads = {"even_norm": d_g0, "even_w_in": d_e_w_in, "ssm_w_glu": d_wglu, "even_w_out": d_e_w_out}
    return jnp.sum(loss_lanes), dx, grads, (odd_mine, odd_theirs), rep_parts


WEIGHT_NAMES = ("even_norm", "even_w_in", "pool_w", "pool_scale", "ssm_log_dt", "ssm_a_re", "ssm_a_im",
                "ssm_b_re", "ssm_b_im", "ssm_c_re", "ssm_c_im", "ssm_d", "ssm_w_glu", "even_w_out", "odd_norm",
                "odd_w_in", "conv_w", "conv_b", "conv_ln_g", "conv_ln_b", "odd_w_out", "final_norm")
SHARDED = tuple(n for n, _ in SHARD_BIG) + tuple(n for n, _ in SHARD_SMALL)


SMALL_ROWS = 64


def _odd_shards(shard):
    small = jnp.concatenate([shard[n].reshape(r, 256) for n, r in SHARD_SMALL], axis=0)
    small = jnp.pad(small, ((0, SMALL_ROWS - small.shape[0]), (0, 0)))
    return [shard["odd_w_in"].astype(BF16), shard["odd_w_out"].astype(BF16), small]


def _pack_small(d):
    small = jnp.concatenate([d[n].reshape(r, -1) for n, r in SHARD_SMALL], axis=0)
    if small.shape[1] == D_MODEL:
        small = small.reshape(35, 4, 256).transpose(1, 0, 2)
    small = small.reshape(-1, 35 * 256)
    small = jnp.pad(small, ((0, 0), (0, ROWS_SMALL * D_MODEL - 35 * 256)))
    return small.reshape(-1, ROWS_SMALL, D_MODEL)


def _unpack_small(buf):
    small = buf.reshape(-1)[:35 * 256].reshape(35, 256)
    out = {}
    off = 0
    for n, r in SHARD_SMALL:
        out[n] = small[off:off + r].reshape((r, 256) if r > 1 else (256,))
        off += r
    return out


EVEN_PACK = (("even_w_in", 0, 512), ("ssm_w_glu", 512, 128), ("even_w_out", 640, 256))
ROWS_EVEN = 1024
ODD_PACK = (("odd_w_in", 0, 768), ("odd_w_out", 768, 256))
ODD_SMALL_ROW = 1024
ROWS_ODD = 1280


def _pack_even_grads(g):
    parts = [g["even_w_in"].reshape(4, -1, D_MODEL), _cols_split(g["ssm_w_glu"], 256),
             g["even_w_out"].reshape(4, -1, D_MODEL), jnp.zeros((4, ROWS_EVEN - 896, D_MODEL), F32)]
    return jnp.concatenate(parts, axis=1)


def _pack_odd_grads(g):
    parts = [g["odd_w_in"].reshape(4, -1, D_MODEL), g["odd_w_out"].reshape(4, -1, D_MODEL), _pack_small(g),
             jnp.zeros((4, ROWS_ODD - ODD_SMALL_ROW - ROWS_SMALL, D_MODEL), F32)]
    return jnp.concatenate(parts, axis=1)


def kernel(x, even_norm, even_w_in, pool_w, pool_scale, ssm_log_dt, ssm_a_re, ssm_a_im, ssm_b_re, ssm_b_im, ssm_c_re, ssm_c_im, ssm_d, ssm_w_glu, even_w_out, odd_norm, odd_w_in, conv_w, conv_b, conv_ln_g, conv_ln_b, odd_w_out, final_norm, loss_target, m_even_norm, m_even_w_in, m_pool_w, m_pool_scale, m_ssm_log_dt, m_ssm_a_re, m_ssm_a_im, m_ssm_b_re, m_ssm_b_im, m_ssm_c_re, m_ssm_c_im, m_ssm_d, m_ssm_w_glu, m_even_w_out, m_odd_norm, m_odd_w_in, m_conv_w, m_conv_b, m_conv_ln_g, m_conv_ln_b, m_odd_w_out, m_final_norm, v_even_norm, v_even_w_in, v_pool_w, v_pool_scale, v_ssm_log_dt, v_ssm_a_re, v_ssm_a_im, v_ssm_b_re, v_ssm_b_im, v_ssm_c_re, v_ssm_c_im, v_ssm_d, v_ssm_w_glu, v_even_w_out, v_odd_norm, v_odd_w_in, v_conv_w, v_conv_b, v_conv_ln_g, v_conv_ln_b, v_odd_w_out, v_final_norm):
    ws = dict(zip(WEIGHT_NAMES, (even_norm, even_w_in, pool_w, pool_scale, ssm_log_dt, ssm_a_re, ssm_a_im, ssm_b_re,
                                 ssm_b_im, ssm_c_re, ssm_c_im, ssm_d, ssm_w_glu, even_w_out, odd_norm, odd_w_in,
                                 conv_w, conv_b, conv_ln_g, conv_ln_b, odd_w_out, final_norm)))
    ms = dict(zip(WEIGHT_NAMES, (m_even_norm, m_even_w_in, m_pool_w, m_pool_scale, m_ssm_log_dt, m_ssm_a_re,
                                 m_ssm_a_im, m_ssm_b_re, m_ssm_b_im, m_ssm_c_re, m_ssm_c_im, m_ssm_d, m_ssm_w_glu,
                                 m_even_w_out, m_odd_norm, m_odd_w_in, m_conv_w, m_conv_b, m_conv_ln_g, m_conv_ln_b,
                                 m_odd_w_out, m_final_norm)))
    vs = dict(zip(WEIGHT_NAMES, (v_even_norm, v_even_w_in, v_pool_w, v_pool_scale, v_ssm_log_dt, v_ssm_a_re,
                                 v_ssm_a_im, v_ssm_b_re, v_ssm_b_im, v_ssm_c_re, v_ssm_c_im, v_ssm_d, v_ssm_w_glu,
                                 v_even_w_out, v_odd_norm, v_odd_w_in, v_conv_w, v_conv_b, v_conv_ln_g, v_conv_ln_b,
                                 v_odd_w_out, v_final_norm)))
    lead = {n: a.shape for n, a in ws.items()}
    drop = lambda d: {n: (a[0] if n != "final_norm" else a) for n, a in d.items()}
    ws, ms, vs = drop(ws), drop(ms), drop(vs)

    shard = {n: ws[n] for n in SHARDED}
    rep = {n: ws[n] for n, _ in REP_NAMES}
    loss_part, grad_x, grads, (odd_mine, odd_theirs), rep_parts = _local_step(x[0], loss_target[0], rep, shard)
    loss = lax.psum(loss_part, ("x", "y", "c"))

    g_even = _pack_even_grads(grads)
    got_even, late_parts = _scatter_and_gather(
        _pair_add(g_even, _pair_split(g_even, "pair_split_even"), BF16, "pair_add_even"),
        jnp.pad(grads["even_norm"], ((0, SUB - 1), (0, 0))), "scatter_even_gather_late")
    even_mine = _sum_lead(got_even, "chip_sum_even")
    even_theirs = _pair_join(even_mine, "pair_join_even")
    outs = [{}, {}, {}, {}]
    for pack, mine, theirs in ((EVEN_PACK, even_mine, even_theirs), (ODD_PACK, odd_mine, odd_theirs)):
        for n, row0, rows in pack:
            view = lambda a: a.reshape(rows, D_MODEL)
            res = _adamw_rows(view(ws[n]), view(ms[n]), view(vs[n]), mine, theirs, row0, 128, "adamw_" + n)
            for o, r in zip(outs, res):
                o[n] = r
    small = lambda d: _pack_small({n: d[n] for n, _ in SHARD_SMALL})[0]
    res = _adamw_rows(small(ws), small(ms), small(vs), odd_mine, odd_theirs, ODD_SMALL_ROW, ROWS_SMALL, "adamw_small")
    for o, r in zip(outs, res):
        o.update(_unpack_small(r))
    g_rep = lax.dynamic_update_slice(_sum_lead(rep_parts, "rep_sum"), _sum_lead(late_parts, "late_sum")[0:1], (0, 0))
    res = _adamw(_pack_rep(rep), g_rep, _pack_rep({n: ms[n] for n, _ in REP_NAMES}),
                 _pack_rep({n: vs[n] for n, _ in REP_NAMES}), "adamw_rep")
    for o, r in zip(outs, (g_rep,) + tuple(res)):
        o.update(_unpack_rep(r))

    leaves = [[o[n].reshape(lead[n]) for n in WEIGHT_NAMES] for o in outs]
    return (loss, grad_x[None], *leaves[0], *leaves[1], *leaves[2], *leaves[3])
```

```python
import jax
import jax.numpy as jnp
from jax import lax
from jax.experimental import pallas as pl
from jax.experimental.pallas import tpu as pltpu

F32 = jnp.float32
BF16 = jnp.bfloat16
MESH = pl.DeviceIdType.MESH

D_MODEL = 1024
RMS_EPS = 1e-6
LN_EPS = 1e-5
N_GROUPS = 32
GROUP_DIM = 16
N_STATE = 64
STATES = N_GROUPS * N_STATE
SSM_W = 512
POOL_W = 512
CONV_K = 31
HALO = 32
POOL_HALO = 16

ADAM_LR = 0.001
ADAM_B1 = 0.9
ADAM_B2 = 0.999
ADAM_EPS = 1e-08
ADAM_WD = 0.01
ADAM_STEP = 10

TM = 256
TM_MM = 512
SUB = 8
LCH = 512
SCAN_L = 1024
VMEM_LIMIT = 56 * 1024 * 1024

ROWS_SMALL = 16
REP_ROWS = 200
COMM_CHUNKS = 4


def _params(n_axes=1):
    return pltpu.CompilerParams(dimension_semantics=("arbitrary",) * n_axes, vmem_limit_bytes=VMEM_LIMIT)


def _rows(w, cb=0, rev=None, tm=TM):
    if rev is None:
        return pl.BlockSpec((tm, w), lambda i: (i, cb))
    return pl.BlockSpec((tm, w), lambda i: (rev - 1 - i, cb))


def _mm_rows(w, cb=0):
    return _rows(w, cb, tm=TM_MM)


def _full(shape):
    n = len(shape)
    return pl.BlockSpec(shape, lambda i: (0,) * n)


def _prev(hr, w, cb=0, tm=TM):
    r = tm // hr
    return pl.BlockSpec((hr, w), lambda i: (jnp.maximum(i * r - 1, 0), cb))


def _next(hr, w, nrows, cb=0, tm=TM):
    r = tm // hr
    last = nrows // hr - 1
    return pl.BlockSpec((hr, w), lambda i: (jnp.minimum((i + 1) * r, last), cb))


def _dot(a, b):
    return jnp.dot(a, b, preferred_element_type=F32)


def _dot_nt(a, b):
    return lax.dot_general(a, b, (((1,), (1,)), ((), ())), preferred_element_type=F32)


def _dot_tn(a, b):
    return lax.dot_general(a, b, (((0,), (0,)), ((), ())), preferred_element_type=F32)


def _sig(x):
    return 1.0 / (1.0 + jnp.exp(-x))


def _zero_at_first(i, *refs):
    @pl.when(i == 0)
    def _():
        for r in refs:
            r[...] = jnp.zeros_like(r)


def _norm_in(x, g, w, name, ride=(), pool=None):
    t, ns = x.shape[0], w.shape[2]
    n = 4 * ns
    ng = len(ride)
    npool = 0 if pool is None else 1
    nstep = t // TM_MM

    def body(x_ref, g_ref, w_ref, *rest):
        pool_in, rest = rest[:2 * npool], rest[2 * npool:]
        ride_in, o_ref, rest = rest[:ng], rest[ng], rest[ng + 1:]
        yp_ref, rest = (rest[0], rest[1:]) if npool else (None, rest)
        ride_out, rest = rest[:ng], rest[ng:]
        halo_ref, rest = (rest[0], rest[1:]) if npool else (None, rest)
        i = pl.program_id(0)
        if ng:
            start, forward, finish = _gather_phases(ride_in, ride_out, rest[:ng], *rest[ng:])
            pl.when(i == 0)(start)
            pl.when(i == nstep // 2)(forward)
        xv = x_ref[...]
        r = lax.rsqrt(jnp.mean(xv * xv, axis=-1, keepdims=True) + RMS_EPS)
        h = (xv * r * g_ref[...]).astype(BF16)
        first = _dot(h, w_ref[0]).astype(BF16)
        o_ref[:, 0:ns] = first
        if npool:
            wp_ref, ps_ref = pool_in
            _zero_at_first(i, halo_ref)
            pos = (i * TM_MM + 1 + lax.broadcasted_iota(jnp.int32, (TM_MM, 1), 0)).astype(F32)
            for gi in range(4):
                sl = slice(128 * gi, 128 * (gi + 1))
                u = first[:, sl].astype(F32)
                s = _pool_sums(jnp.concatenate([halo_ref[:, sl], u], axis=0), gi, True)[POOL_HALO:, :]
                pooled = s / jnp.minimum(pos, float(2 << gi)) - u
                yp_ref[:, sl] = (_dot(pooled.astype(BF16), wp_ref[gi]) * ps_ref[:, sl]).astype(BF16)
                halo_ref[:, sl] = u[TM_MM - POOL_HALO:, :]
        for s in range(1, 4):
            o_ref[:, s * ns:(s + 1) * ns] = _dot(h, w_ref[s]).astype(BF16)
        if ng:
            pl.when(i == nstep - 1)(finish)

    pool_args = [] if pool is None else list(pool)
    res = pl.pallas_call(
        body, name=name, grid=(nstep,),
        in_specs=[_mm_rows(D_MODEL), _full((1, D_MODEL)), _full(w.shape)] + [_full(a.shape) for a in pool_args]
        + [ANY] * ng,
        out_specs=[_mm_rows(n)] + [_mm_rows(POOL_W)] * npool + [ANY] * ng,
        out_shape=[jax.ShapeDtypeStruct((t, n), BF16)] + [jax.ShapeDtypeStruct((t, POOL_W), BF16)] * npool
        + [jax.ShapeDtypeStruct((4,) + a.shape, a.dtype) for a in ride],
        scratch_shapes=[pltpu.VMEM((POOL_HALO, POOL_W), F32)] * npool + (_gather_scratch(ride) if ng else []),
        compiler_params=_params())(x, g, w, *pool_args, *ride)
    return res[0], (res[1] if npool else None), res[1 + npool:]


def _pool_sums(ext, g, forward):
    n = ext.shape[0]
    s = ext
    for step in range(g + 1):
        k = 1 << step
        s = s + pltpu.roll(s, k if forward else n - k, 0)
    return s


SEG_LEN = TM // SUB


def _perm_matrix():
    p = jnp.arange(TM)
    src = (p % SUB) * SEG_LEN + p // SUB
    return (src[:, None] == jnp.arange(TM)[None, :]).astype(BF16)


def _cmul_add(are, aim, vre, vim, bre, bim):
    return are * vre - aim * vim + bre, are * vim + aim * vre + bim


def _segment_chain(ere, eim, qre, qim, cin_re, cin_im, row, up):
    for sh in (1, 2, 4):
        mre, mim = (qre[SUB - sh:SUB - sh + 1, :], qim[SUB - sh:SUB - sh + 1, :]) if up else \
                   (qre[sh - 1:sh, :], qim[sh - 1:sh, :])
        keep = (row < SUB - sh) if up else (row >= sh)
        sre = jnp.where(keep, pltpu.roll(ere, SUB - sh if up else sh, 0), 0.0)
        sim = jnp.where(keep, pltpu.roll(eim, SUB - sh if up else sh, 0), 0.0)
        ere, eim = _cmul_add(mre, mim, sre, sim, ere, eim)
    ere, eim = _cmul_add(qre, qim, cin_re, cin_im, ere, eim)
    keep = (row < SUB - 1) if up else (row >= 1)
    ent_re = jnp.where(keep, pltpu.roll(ere, SUB - 1 if up else 1, 0), cin_re)
    ent_im = jnp.where(keep, pltpu.roll(eim, SUB - 1 if up else 1, 0), cin_im)
    return ere, eim, ent_re, ent_im


def _scan_fwd_block(xs_re, xs_im, p8_re, p8_im, q_re, q_im, car_re, car_im, ent_re_ref, ent_im_ref):
    row = lax.broadcasted_iota(jnp.int32, (SUB, SCAN_L), 0)
    for j in range(STATES // SCAN_L):
        sl = slice(SCAN_L * j, SCAN_L * (j + 1))
        are, aim = p8_re[0:SUB, sl], p8_im[0:SUB, sl]

        def totals(i, v, sl=sl, are=are, aim=aim):
            r0 = pl.multiple_of(i * SUB, SUB)
            vre, vim = _cmul_add(are, aim, v[0], v[1], xs_re[pl.ds(r0, SUB), sl], xs_im[pl.ds(r0, SUB), sl])
            xs_re[pl.ds(r0, SUB), sl] = vre
            xs_im[pl.ds(r0, SUB), sl] = vim
            return vre, vim

        ere, eim = lax.fori_loop(1, SEG_LEN, totals, (xs_re[0:SUB, sl], xs_im[0:SUB, sl]), unroll=2)
        ere, eim, cre, cim = _segment_chain(ere, eim, q_re[:, sl], q_im[:, sl],
                                            car_re[:, sl], car_im[:, sl], row, False)
        car_re[:, sl] = jnp.broadcast_to(ere[SUB - 1:SUB, :], (SUB, SCAN_L))
        car_im[:, sl] = jnp.broadcast_to(eim[SUB - 1:SUB, :], (SUB, SCAN_L))
        if ent_re_ref is not None:
            ent_re_ref[:, sl] = cre
            ent_im_ref[:, sl] = cim

        def fix(i, c, sl=sl, cre=cre, cim=cim):
            r0 = pl.multiple_of(i * SUB, SUB)
            vre, vim = _cmul_add(p8_re[pl.ds(r0, SUB), sl], p8_im[pl.ds(r0, SUB), sl], cre, cim,
                                 xs_re[pl.ds(r0, SUB), sl], xs_im[pl.ds(r0, SUB), sl])
            xs_re[pl.ds(r0, SUB), sl] = vre
            xs_im[pl.ds(r0, SUB), sl] = vim
            return c

        lax.fori_loop(0, SEG_LEN, fix, 0, unroll=2)


def _ssm_fwd(proj, pm, pmt, mb_re, mb_im, p8_re, p8_im, q_re, q_im, cm_re, cm_im, dskip, wglu, ride=()):
    t = proj.shape[0]
    nblk = t // TM

    ng = len(ride)

    def body(u_ref, pm_ref, pmt_ref, mbre, mbim, p8re, p8im, qre, qim, cmre, cmim, d_ref, wg_ref, *rest):
        ride_in, (y_ref, cre_ref, cim_ref), ride_out = rest[:ng], rest[ng:ng + 3], rest[ng + 3:2 * ng + 3]
        xs_re, xs_im, car_re, car_im, ysk = rest[2 * ng + 3:2 * ng + 8]
        i = pl.program_id(0)
        if ng:
            start, forward, finish = _gather_phases(ride_in, ride_out, rest[2 * ng + 8:3 * ng + 8],
                                                    *rest[3 * ng + 8:])
            pl.when(i == 0)(start)
            pl.when(i == nblk // 2)(forward)
        _zero_at_first(i, car_re, car_im)
        cre_ref[0] = car_re[...]
        cim_ref[0] = car_im[...]
        us = _dot(pm_ref[...], u_ref[...])
        usb = us.astype(BF16)
        for j in range(4):
            xs_re[:, LCH * j:LCH * (j + 1)] = _dot(usb[:, 128 * j:128 * (j + 1)], mbre[j])
            xs_im[:, LCH * j:LCH * (j + 1)] = _dot(usb[:, 128 * j:128 * (j + 1)], mbim[j])
        _scan_fwd_block(xs_re, xs_im, p8re, p8im, qre, qim, car_re, car_im, None, None)
        for j in range(4):
            sl = slice(LCH * j, LCH * (j + 1))
            ysk[:, 128 * j:128 * (j + 1)] = (_dot_nt(xs_re[:, sl].astype(BF16), cmre[j])
                                             - _dot_nt(xs_im[:, sl].astype(BF16), cmim[j]))
        yv = ysk[...] + d_ref[...] * us
        gv = _dot(yv.astype(BF16), wg_ref[...])
        y_ref[...] = _dot(pmt_ref[...], (gv[:, :SSM_W] * _sig(gv[:, SSM_W:])).astype(BF16)).astype(BF16)
        if ng:
            pl.when(i == nblk - 1)(finish)

    blk = (4, 128, LCH)
    res = pl.pallas_call(
        body, name="ssm_fwd", grid=(nblk,),
        in_specs=[_rows(SSM_W, 1), _full((TM, TM)), _full((TM, TM)), _full(blk), _full(blk),
                  _full((TM, STATES)), _full((TM, STATES)), _full((SUB, STATES)), _full((SUB, STATES)),
                  _full(blk), _full(blk), _full((1, SSM_W)), _full((SSM_W, 2 * SSM_W))] + [ANY] * ng,
        out_specs=[_rows(SSM_W), pl.BlockSpec((1, SUB, STATES), lambda i: (i, 0, 0)),
                   pl.BlockSpec((1, SUB, STATES), lambda i: (i, 0, 0))] + [ANY] * ng,
        out_shape=[jax.ShapeDtypeStruct((t, SSM_W), BF16), jax.ShapeDtypeStruct((nblk, SUB, STATES), F32),
                   jax.ShapeDtypeStruct((nblk, SUB, STATES), F32)]
        + [jax.ShapeDtypeStruct((4,) + a.shape, a.dtype) for a in ride],
        scratch_shapes=[pltpu.VMEM((TM, STATES), F32), pltpu.VMEM((TM, STATES), F32),
                        pltpu.VMEM((SUB, STATES), F32), pltpu.VMEM((SUB, STATES), F32),
                        pltpu.VMEM((TM, SSM_W), F32)] + (_gather_scratch(ride) if ng else []),
        compiler_params=_params())(proj, pm, pmt, mb_re, mb_im, p8_re, p8_im, q_re, q_im, cm_re, cm_im, dskip, wglu,
                                   *ride)
    return res[:3], res[3:]


def _even_out(yp, ys, proj, x, w):
    t = x.shape[0]

    def body(yp_ref, ys_ref, z_ref, x_ref, w_ref, x1_ref, yg_ref):
        z = z_ref[...].astype(F32)
        sz = z * _sig(z)
        gp = (yp_ref[...].astype(F32) * sz[:, :POOL_W]).astype(BF16)
        gs = (ys_ref[...].astype(F32) * sz[:, POOL_W:]).astype(BF16)
        yg_ref[:, :POOL_W] = gp
        yg_ref[:, POOL_W:] = gs
        x1_ref[...] = x_ref[...] + _dot(gp, w_ref[:POOL_W, :]) + _dot(gs, w_ref[POOL_W:, :])

    return pl.pallas_call(
        body, name="even_out", grid=(t // TM_MM,),
        in_specs=[_mm_rows(POOL_W), _mm_rows(SSM_W), _mm_rows(D_MODEL, 1), _mm_rows(D_MODEL),
                  _full((D_MODEL, D_MODEL))],
        out_specs=[_mm_rows(D_MODEL), _mm_rows(D_MODEL)],
        out_shape=[jax.ShapeDtypeStruct((t, D_MODEL), F32), jax.ShapeDtypeStruct((t, D_MODEL), BF16)],
        compiler_params=_params())(yp, ys, proj, x, w)


def _phase_copies(ext, cp):
    n = cp.shape[1]
    for j in range(1, SUB):
        cp[j - 1] = ext[pl.ds(j, n), :]


def _shifted(ext, cp, off, nrows, sl, row0=0):
    q, j = divmod(off, SUB)
    if j == 0:
        return ext[pl.ds(row0 + SUB * q, nrows), sl]
    return cp[j - 1, pl.ds(row0 + SUB * q, nrows), sl]


def _conv_taps(ext, cp, w_ref, first, nrows, sl, init, row0=0):
    acc = init
    for k in range(CONV_K):
        acc = acc + w_ref[k:k + 1, sl] * _shifted(ext, cp, first(k), nrows, sl, row0)
    return acc


def _conv_fwd(q, cw, cb, lg, lb):
    t = q.shape[0]

    def body(v_ref, g_ref, hv_ref, hg_ref, z_ref, w_ref, b_ref, lg_ref, lb_ref, y_ref, cv_ref, ext, cp):
        i = pl.program_id(0)
        ext[0:HALO, :] = jnp.where(i == 0, 0.0, hv_ref[...].astype(F32) * _sig(hg_ref[...].astype(F32)))
        ext[HALO:, :] = v_ref[...].astype(F32) * _sig(g_ref[...].astype(F32))
        _phase_copies(ext, cp)

        def lanes(c, carry):
            sl = pl.ds(pl.multiple_of(c * 128, 128), 128)
            cv_ref[:, sl] = _conv_taps(ext, cp, w_ref, lambda k: k + 2, TM, sl,
                                       jnp.broadcast_to(b_ref[:, sl], (TM, 128))).astype(BF16)
            return carry

        lax.fori_loop(0, D_MODEL // 128, lanes, 0)
        cv = cv_ref[...].astype(F32)
        cc = cv - jnp.mean(cv, axis=-1, keepdims=True)
        rstd = lax.rsqrt(jnp.mean(cc * cc, axis=-1, keepdims=True) + LN_EPS)
        cl = cc * rstd * lg_ref[...] + lb_ref[...]
        z = z_ref[...].astype(F32)
        y_ref[...] = (cl * _sig(cl) * z * _sig(z)).astype(BF16)

    vec = _full((1, D_MODEL))
    return pl.pallas_call(
        body, name="conv_fwd", grid=(t // TM,),
        in_specs=[_rows(D_MODEL, 0), _rows(D_MODEL, 1), _prev(HALO, D_MODEL, 0), _prev(HALO, D_MODEL, 1),
                  _rows(D_MODEL, 2), _full((HALO, D_MODEL)), vec, vec, vec],
        out_specs=[_rows(D_MODEL), _rows(D_MODEL)],
        out_shape=[jax.ShapeDtypeStruct((t, D_MODEL), BF16), jax.ShapeDtypeStruct((t, D_MODEL), BF16)],
        scratch_shapes=[pltpu.VMEM((TM + HALO, D_MODEL), F32),
                        pltpu.VMEM((SUB - 1, TM + HALO - SUB, D_MODEL), F32)],
        compiler_params=_params())(q, q, q, q, q, cw, cb, lg, lb)


def _odd_out_loss(y2, x1, w, gf, tgt):
    t = x1.shape[0]

    def body(y_ref, x_ref, w_ref, g_ref, t_ref, dx_ref, loss_ref, dg_ref):
        i = pl.program_id(0)
        _zero_at_first(i, loss_ref, dg_ref)
        x2 = x_ref[...] + _dot(y_ref[...], w_ref[...])
        r = lax.rsqrt(jnp.mean(x2 * x2, axis=-1, keepdims=True) + RMS_EPS)
        n = x2 * r
        e = n * g_ref[...] - t_ref[...]
        loss_ref[...] += jnp.sum(e * e, axis=0, keepdims=True) * (0.5 / D_MODEL)
        dout = e * (1.0 / D_MODEL)
        dg_ref[...] += jnp.sum(dout * n, axis=0, keepdims=True)
        dn = dout * g_ref[...]
        dx_ref[...] = r * (dn - n * jnp.mean(dn * n, axis=-1, keepdims=True))

    vec = _full((1, D_MODEL))
    return pl.pallas_call(
        body, name="odd_out_loss", grid=(t // TM_MM,),
        in_specs=[_mm_rows(D_MODEL), _mm_rows(D_MODEL), _full((D_MODEL, D_MODEL)), vec, _mm_rows(D_MODEL)],
        out_specs=[_mm_rows(D_MODEL), vec, vec],
        out_shape=[jax.ShapeDtypeStruct((t, D_MODEL), F32), jax.ShapeDtypeStruct((1, D_MODEL), F32),
                   jax.ShapeDtypeStruct((1, D_MODEL), F32)],
        compiler_params=_params())(y2, x1, w, gf, tgt)


def _dsilu(z):
    s = _sig(z)
    return z * s, s * (1.0 + z * (1.0 - s))


def _odd_bwd_out(dx2, w, y2, cv, q, lg, lb):
    t = dx2.shape[0]

    def body(dx_ref, w_ref, y_ref, cv_ref, z_ref, lg_ref, lb_ref, dcv_ref, dz_ref, dw_ref, dlg_ref, dlb_ref):
        i = pl.program_id(0)
        _zero_at_first(i, dw_ref, dlg_ref, dlb_ref)
        dxb = dx_ref[...].astype(BF16)
        dy = _dot_nt(dxb, w_ref[...])
        dw_ref[...] += _dot_tn(y_ref[...], dxb)
        cv = cv_ref[...].astype(F32)
        cc = cv - jnp.mean(cv, axis=-1, keepdims=True)
        rstd = lax.rsqrt(jnp.mean(cc * cc, axis=-1, keepdims=True) + LN_EPS)
        cn = cc * rstd
        silu_c, dsilu_c = _dsilu(cn * lg_ref[...] + lb_ref[...])
        silu_z, dsilu_z = _dsilu(z_ref[...].astype(F32))
        dcl = dy * silu_z * dsilu_c
        dz_ref[...] = (dy * silu_c * dsilu_z).astype(BF16)
        dlg_ref[...] += jnp.sum(dcl * cn, axis=0, keepdims=True)
        dlb_ref[...] += jnp.sum(dcl, axis=0, keepdims=True)
        dcn = dcl * lg_ref[...]
        dcv_ref[...] = (rstd * (dcn - jnp.mean(dcn, axis=-1, keepdims=True)
                                - cn * jnp.mean(dcn * cn, axis=-1, keepdims=True))).astype(BF16)

    vec = _full((1, D_MODEL))
    mat = _full((D_MODEL, D_MODEL))
    return pl.pallas_call(
        body, name="odd_bwd_out", grid=(t // TM_MM,),
        in_specs=[_mm_rows(D_MODEL), mat, _mm_rows(D_MODEL), _mm_rows(D_MODEL), _mm_rows(D_MODEL, 2), vec, vec],
        out_specs=[_mm_rows(D_MODEL), _mm_rows(D_MODEL), mat, vec, vec],
        out_shape=[jax.ShapeDtypeStruct((t, D_MODEL), BF16), jax.ShapeDtypeStruct((t, D_MODEL), BF16),
                   jax.ShapeDtypeStruct((D_MODEL, D_MODEL), F32), jax.ShapeDtypeStruct((1, D_MODEL), F32),
                   jax.ShapeDtypeStruct((1, D_MODEL), F32)],
        compiler_params=_params())(dx2, w, y2, cv, q, lg, lb)


def _conv_bwd(dcv, q, cw):
    t = dcv.shape[0]
    nblk = t // TM

    def body(d_ref, dn_ref, v_ref, g_ref, hv_ref, hg_ref, w_ref,
             dv_ref, dgt_ref, dw_ref, db_ref, gext, dext, dgl, gcp, dcp):
        i = pl.program_id(0)
        last = nblk - 1
        _zero_at_first(i, dw_ref, db_ref)
        v = v_ref[...].astype(F32)
        sg = _sig(g_ref[...].astype(F32))
        gext[0:HALO, :] = jnp.where(i == 0, 0.0, hv_ref[...].astype(F32) * _sig(hg_ref[...].astype(F32)))
        gext[HALO:, :] = v * sg
        d = d_ref[...].astype(F32)
        dext[0:TM, :] = d
        dext[TM:, :] = jnp.where(i == last, 0.0, dn_ref[...].astype(F32))
        _phase_copies(gext, gcp)
        _phase_copies(dext, dcp)
        db_ref[...] += jnp.sum(d, axis=0, keepdims=True)
        def lanes(c, carry):
            sl = pl.ds(pl.multiple_of(c * 128, 128), 128)
            dgl[:, sl] = _conv_taps(dext, dcp, w_ref, lambda k: 30 - k, TM, sl, jnp.zeros((TM, 128), F32))
            return carry

        def lanes_w(c, carry):
            sl = pl.ds(pl.multiple_of(c * 128, 128), 128)
            ntile = TM // SUB
            dts = [dext[SUB * r:SUB * (r + 1), sl] for r in range(ntile)]
            for j in range(SUB):
                taps = [(q, SUB * q + j - 2) for q in range(5) if 0 <= SUB * q + j - 2 < CONV_K]
                sums = {k: None for _, k in taps}
                for rt in range(ntile + 4):
                    need = [(q, k) for q, k in taps if 0 <= rt - q < ntile]
                    if not need:
                        continue
                    src = gext[SUB * rt:SUB * (rt + 1), sl] if j == 0 else gcp[j - 1, SUB * rt:SUB * (rt + 1), sl]
                    for q, k in need:
                        prod = dts[rt - q] * src
                        sums[k] = prod if sums[k] is None else sums[k] + prod
                for _, k in taps:
                    dw_ref[SUB * k:SUB * (k + 1), sl] += sums[k]
            return carry

        lax.fori_loop(0, D_MODEL // 128, lanes, 0)
        lax.fori_loop(0, D_MODEL // 128, lanes_w, 0)
        dg = dgl[...]
        dv_ref[...] = (dg * sg).astype(BF16)
        dgt_ref[...] = (dg * v * sg * (1.0 - sg)).astype(BF16)

    return pl.pallas_call(
        body, name="conv_bwd", grid=(t // TM,),
        in_specs=[_rows(D_MODEL), _next(HALO, D_MODEL, t), _rows(D_MODEL, 0), _rows(D_MODEL, 1),
                  _prev(HALO, D_MODEL, 0), _prev(HALO, D_MODEL, 1), _full((HALO, D_MODEL))],
        out_specs=[_rows(D_MODEL), _rows(D_MODEL), _full((HALO * SUB, D_MODEL)), _full((1, D_MODEL))],
        out_shape=[jax.ShapeDtypeStruct((t, D_MODEL), BF16), jax.ShapeDtypeStruct((t, D_MODEL), BF16),
                   jax.ShapeDtypeStruct((HALO * SUB, D_MODEL), F32), jax.ShapeDtypeStruct((1, D_MODEL), F32)],
        scratch_shapes=[pltpu.VMEM((TM + HALO, D_MODEL), F32), pltpu.VMEM((TM + HALO, D_MODEL), F32),
                        pltpu.VMEM((TM, D_MODEL), F32),
                        pltpu.VMEM((SUB - 1, TM + HALO - SUB, D_MODEL), F32),
                        pltpu.VMEM((SUB - 1, TM + HALO - SUB, D_MODEL), F32)],
        compiler_params=_params())(dcv, dcv, q, q, q, q, cw)


def _column_segments(widths, ns):
    segs = []
    col = 0
    for p, wd in enumerate(widths):
        a = 0
        while a < wd:
            s, lo = divmod(col + a, ns)
            ln = min(wd - a, ns - lo)
            segs.append((p, a, a + ln, s, lo, lo + ln))
            a += ln
        col += wd
    return segs


def _in_bwd(dparts, w, x, g, dres, name, ride=None):
    t = x.shape[0]
    widths = [p.shape[1] for p in dparts]
    npart = len(dparts)
    segs = _column_segments(widths, w.shape[2])
    nstep = t // TM_MM

    def body(*refs):
        d_refs = refs[:npart]
        w_ref, x_ref, g_ref, r_ref = refs[npart:npart + 4]
        if ride is None:
            dx_ref, dg_ref, dw_ref = refs[npart + 4:]
        else:
            v_ref, j_ref, dx_ref, dg_ref, dw_ref, o_ref, jo_ref = refs[npart + 4:npart + 11]
            start, forward, finish = _gather_all_phases(v_ref, o_ref, *refs[npart + 11:npart + 15])
            j_start, j_finish = _pair_join_phases(j_ref, jo_ref, *refs[npart + 15:])
        i = pl.program_id(0)
        if ride is not None:
            pl.when(i == 0)(start)
            pl.when(i == 0)(j_start)
            pl.when(i == nstep // 2)(forward)
        _zero_at_first(i, dg_ref, dw_ref)
        xv = x_ref[...]
        r = lax.rsqrt(jnp.mean(xv * xv, axis=-1, keepdims=True) + RMS_EPS)
        n = xv * r
        h = (n * g_ref[...]).astype(BF16)
        dh = None
        for p, lo, hi, s, slo, shi in segs:
            d = d_refs[p][:, lo:hi]
            part = _dot_nt(d, w_ref[s, :, slo:shi])
            dh = part if dh is None else dh + part
            dw_ref[s, :, slo:shi] += _dot_tn(h, d)
        dg_ref[...] += jnp.sum(dh * n, axis=0, keepdims=True)
        dn = dh * g_ref[...]
        dx_ref[...] = r_ref[...] + r * (dn - n * jnp.mean(dn * n, axis=-1, keepdims=True))
        if ride is not None:
            pl.when(i == nstep - 1)(finish)
            pl.when(i == nstep - 1)(j_finish)

    vec = _full((1, D_MODEL))
    once = pl.BlockSpec(w.shape, lambda i: (0, 0, 0), pipeline_mode=pl.Buffered(1))
    extra = [] if ride is None else list(ride)
    return pl.pallas_call(
        body, name=name, grid=(nstep,),
        in_specs=[_mm_rows(wd) for wd in widths] + [once, _mm_rows(D_MODEL), vec, _mm_rows(D_MODEL)]
        + [ANY] * len(extra),
        out_specs=[_mm_rows(D_MODEL), vec, once] + [ANY] * len(extra),
        out_shape=[jax.ShapeDtypeStruct((t, D_MODEL), F32), jax.ShapeDtypeStruct((1, D_MODEL), F32),
                   jax.ShapeDtypeStruct(w.shape, F32)]
        + ([jax.ShapeDtypeStruct((8,) + ride[0].shape, ride[0].dtype),
            jax.ShapeDtypeStruct(ride[1].shape, ride[1].dtype)] if extra else []),
        scratch_shapes=(_gather_all_scratch(ride[0]) + _pair_join_scratch()) if extra else [],
        compiler_params=_params())(*dparts, w, x, g, dres, *extra)


def _even_bwd_out(dx1, w, yg, yp, ys, proj, wp, ps, ride):
    t = dx1.shape[0]
    nstep = t // TM_MM
    rows = TM_MM

    def body(dx_ref, w_ref, yg_ref, yp_ref, ys_ref, z_ref, u_ref, h_ref, wp_ref, ps_ref, g_ref,
             dy_ref, dz_ref, dw_ref, du_ref, dwp_ref, dps_ref, theirs_ref, nxt, send_sems, recv_sems):
        i = pl.program_id(0)
        blk = nstep - 1 - i
        start, finish = _pair_split_phases(g_ref, theirs_ref, send_sems, recv_sems)
        pl.when(i == 0)(start)
        _zero_at_first(i, dw_ref, dwp_ref, dps_ref, nxt)
        dxb = dx_ref[...].astype(BF16)
        dyg = _dot_nt(dxb, w_ref[...])
        dw_ref[...] += _dot_tn(yg_ref[...], dxb)
        silu_z, dsilu_z = _dsilu(z_ref[...].astype(F32))
        dyb = (dyg * silu_z).astype(BF16)
        dy_ref[...] = dyb
        dz_ref[:, :POOL_W] = (dyg[:, :POOL_W] * yp_ref[...].astype(F32) * dsilu_z[:, :POOL_W]).astype(BF16)
        dz_ref[:, POOL_W:] = (dyg[:, POOL_W:] * ys_ref[...].astype(F32) * dsilu_z[:, POOL_W:]).astype(BF16)
        pos = (blk * rows + 1 + lax.broadcasted_iota(jnp.int32, (rows, 1), 0)).astype(F32)
        pos_ext = (blk * rows + 1 + lax.broadcasted_iota(jnp.int32, (rows + POOL_HALO, 1), 0)).astype(F32)
        for gi in range(4):
            sl = slice(128 * gi, 128 * (gi + 1))
            wd = float(2 << gi)
            u = u_ref[:, sl].astype(F32)
            halo = jnp.where(blk == 0, 0.0, h_ref[:, sl].astype(F32))
            s = _pool_sums(jnp.concatenate([halo, u], axis=0), gi, True)[POOL_HALO:, :]
            pooled = (s / jnp.minimum(pos, wd) - u).astype(BF16)
            dy = dyb[:, sl].astype(F32)
            dps_ref[:, sl] += jnp.sum(dy * _dot(pooled, wp_ref[gi]), axis=0, keepdims=True)
            dmix = (jnp.concatenate([dy, nxt[:, sl]], axis=0) * ps_ref[:, sl]).astype(BF16)
            dwp_ref[gi] += _dot_tn(pooled, dmix[:rows, :])
            dpool = _dot_nt(dmix, wp_ref[gi])
            lead = _pool_sums(dpool / jnp.minimum(pos_ext, wd), gi, False)
            du_ref[:, sl] = (lead[:rows, :] - dpool[:rows, :]).astype(BF16)
            nxt[:, sl] = dy[:POOL_HALO, :]
        pl.when(i == nstep - 1)(finish)

    back = lambda wdt, cb=0: _rows(wdt, cb, rev=nstep, tm=rows)
    per = rows // POOL_HALO
    halo = pl.BlockSpec((POOL_HALO, POOL_W), lambda i: (jnp.maximum((nstep - 1 - i) * per - 1, 0), 0))
    mat = _full((D_MODEL, D_MODEL))
    return pl.pallas_call(
        body, name="even_bwd_out", grid=(nstep,),
        in_specs=[back(D_MODEL), mat, back(D_MODEL), back(POOL_W), back(SSM_W), back(D_MODEL, 1),
                  back(POOL_W, 0), halo, _full((4, 128, 128)), _full((1, POOL_W)), ANY],
        out_specs=[back(D_MODEL), back(D_MODEL), mat, back(POOL_W), _full((4, 128, 128)), _full((1, POOL_W)), ANY],
        out_shape=[jax.ShapeDtypeStruct((t, D_MODEL), BF16), jax.ShapeDtypeStruct((t, D_MODEL), BF16),
                   jax.ShapeDtypeStruct((D_MODEL, D_MODEL), F32), jax.ShapeDtypeStruct((t, POOL_W), BF16),
                   jax.ShapeDtypeStruct((4, 128, 128), F32), jax.ShapeDtypeStruct((1, POOL_W), F32),
                   _pair_split_shape(ride)],
        scratch_shapes=[pltpu.VMEM((POOL_HALO, POOL_W), F32)] + _pair_split_scratch(ride),
        compiler_params=_params())(dx1, w, yg, yp, ys, proj, proj, proj, wp, ps, ride)


def _ssm_bwd(dycat, proj, car_in_re, car_in_im, pm, pmt, mb_re, mb_im, p8_re, p8_im, q_re, q_im, qr_re, qr_im,
             cm_re, cm_im, dskip, wglu, ride):
    t = proj.shape[0]
    nblk = t // TM

    def body(dy_ref, u_ref, cin_re, cin_im, pm_ref, pmt_ref, mbre, mbim, p8re, p8im, qre, qim, qrre, qrim,
             cmre, cmim, d_ref, wg_ref, p_ref,
             du_ref, dmbre, dmbim, dcmre, dcmim, dare, daim, dd_ref, dwg_ref, got_ref,
             xs_re, xs_im, gs_re, gs_im, car_re, car_im, ent_re, ent_im, gcar_re, gcar_im, ysk, dysk,
             bounce, send_sems, recv_sems, local_sems):
        i = pl.program_id(0)
        start, finish = _chip_scatter_phases(p_ref, got_ref, bounce, send_sems, recv_sems, local_sems)
        pl.when(i == 0)(start)
        _zero_at_first(i, dmbre, dmbim, dcmre, dcmim, dare, daim, dd_ref, dwg_ref, gcar_re, gcar_im)
        us = _dot(pm_ref[...], u_ref[...])
        usb = us.astype(BF16)
        for j in range(4):
            xs_re[:, LCH * j:LCH * (j + 1)] = _dot(usb[:, 128 * j:128 * (j + 1)], mbre[j])
            xs_im[:, LCH * j:LCH * (j + 1)] = _dot(usb[:, 128 * j:128 * (j + 1)], mbim[j])
        car_re[...] = cin_re[0]
        car_im[...] = cin_im[0]
        _scan_fwd_block(xs_re, xs_im, p8re, p8im, qre, qim, car_re, car_im, ent_re, ent_im)
        for j in range(4):
            sl = slice(LCH * j, LCH * (j + 1))
            ysk[:, 128 * j:128 * (j + 1)] = (_dot_nt(xs_re[:, sl].astype(BF16), cmre[j])
                                             - _dot_nt(xs_im[:, sl].astype(BF16), cmim[j]))
        yvb = (ysk[...] + d_ref[...] * us).astype(BF16)
        gv = _dot(yvb, wg_ref[...])
        sg = _sig(gv[:, SSM_W:])
        dyss = _dot(pm_ref[...], dy_ref[...])
        dval = (dyss * sg).astype(BF16)
        dgate = (dyss * gv[:, :SSM_W] * sg * (1.0 - sg)).astype(BF16)
        dy = _dot_nt(dval, wg_ref[:, :SSM_W]) + _dot_nt(dgate, wg_ref[:, SSM_W:])
        dwg_ref[:, :SSM_W] += _dot_tn(yvb, dval)
        dwg_ref[:, SSM_W:] += _dot_tn(yvb, dgate)
        dd_ref[...] += jnp.sum(dy * us, axis=0, keepdims=True)
        dysk[...] = dy
        for j in range(4):
            sl = slice(LCH * j, LCH * (j + 1))
            dyj = dy[:, 128 * j:128 * (j + 1)].astype(BF16)
            gs_re[:, sl] = _dot(dyj, cmre[j])
            gs_im[:, sl] = -_dot(dyj, cmim[j])
            dcmre[j] += _dot_tn(dyj, xs_re[:, sl].astype(BF16))
            dcmim[j] -= _dot_tn(dyj, xs_im[:, sl].astype(BF16))
        row = lax.broadcasted_iota(jnp.int32, (SUB, SCAN_L), 0)
        for j in range(STATES // SCAN_L):
            sl = slice(SCAN_L * j, SCAN_L * (j + 1))
            are, aim = p8re[0:SUB, sl], -p8im[0:SUB, sl]

            def totals(k, v, sl=sl, are=are, aim=aim):
                r0 = pl.multiple_of((SEG_LEN - 2 - k) * SUB, SUB)
                vre, vim = _cmul_add(are, aim, v[0], v[1], gs_re[pl.ds(r0, SUB), sl], gs_im[pl.ds(r0, SUB), sl])
                gs_re[pl.ds(r0, SUB), sl] = vre
                gs_im[pl.ds(r0, SUB), sl] = vim
                return vre, vim

            top = (SEG_LEN - 1) * SUB
            fre, fim = lax.fori_loop(0, SEG_LEN - 1, totals,
                                     (gs_re[top:top + SUB, sl], gs_im[top:top + SUB, sl]), unroll=2)
            fre, fim, nre, nim = _segment_chain(fre, fim, qrre[:, sl], -qrim[:, sl],
                                                gcar_re[:, sl], gcar_im[:, sl], row, True)
            gcar_re[:, sl] = jnp.broadcast_to(fre[0:1, :], (SUB, SCAN_L))
            gcar_im[:, sl] = jnp.broadcast_to(fim[0:1, :], (SUB, SCAN_L))

            def fix(i2, acc, sl=sl, nre=nre, nim=nim):
                r0 = pl.multiple_of(i2 * SUB, SUB)
                rb = pl.multiple_of((SEG_LEN - 1 - i2) * SUB, SUB)
                gre, gim = _cmul_add(p8re[pl.ds(rb, SUB), sl], -p8im[pl.ds(rb, SUB), sl], nre, nim,
                                     gs_re[pl.ds(r0, SUB), sl], gs_im[pl.ds(r0, SUB), sl])
                gs_re[pl.ds(r0, SUB), sl] = gre
                gs_im[pl.ds(r0, SUB), sl] = gim
                rp = pl.multiple_of((i2 - 1) * SUB, SUB)
                xre, xim = xs_re[pl.ds(rp, SUB), sl], xs_im[pl.ds(rp, SUB), sl]
                return acc[0] + gre * xre + gim * xim, acc[1] + gim * xre - gre * xim

            g0re, g0im = _cmul_add(p8re[top:top + SUB, sl], -p8im[top:top + SUB, sl], nre, nim,
                                   gs_re[0:SUB, sl], gs_im[0:SUB, sl])
            gs_re[0:SUB, sl] = g0re
            gs_im[0:SUB, sl] = g0im
            ere, eim = ent_re[:, sl], ent_im[:, sl]
            acc0 = (dare[:, sl] + g0re * ere + g0im * eim, daim[:, sl] + g0im * ere - g0re * eim)
            are_acc, aim_acc = lax.fori_loop(1, SEG_LEN, fix, acc0, unroll=2)
            dare[:, sl] = are_acc
            daim[:, sl] = aim_acc
        for j in range(4):
            sl = slice(LCH * j, LCH * (j + 1))
            c4 = slice(128 * j, 128 * (j + 1))
            gre = gs_re[:, sl].astype(BF16)
            gim = gs_im[:, sl].astype(BF16)
            dmbre[j] += _dot_tn(usb[:, c4], gre)
            dmbim[j] += _dot_tn(usb[:, c4], gim)
            dysk[:, c4] = _dot_nt(gre, mbre[j]) + _dot_nt(gim, mbim[j]) + dysk[:, c4] * d_ref[:, c4]
        du_ref[...] = _dot(pmt_ref[...], dysk[...].astype(BF16)).astype(BF16)
        pl.when(i == nblk - 1)(finish)

    blk = (4, 128, LCH)
    pw = _full((SUB, STATES))
    p8 = _full((TM, STATES))
    perm = _full((TM, TM))
    car = pl.BlockSpec((1, SUB, STATES), lambda i: (nblk - 1 - i, 0, 0))
    big = lambda: pltpu.VMEM((TM, STATES), F32)
    small = lambda: pltpu.VMEM((SUB, STATES), F32)
    return pl.pallas_call(
        body, name="ssm_bwd", grid=(nblk,),
        in_specs=[_rows(SSM_W, 1, rev=nblk), _rows(SSM_W, 1, rev=nblk), car, car, perm, perm, _full(blk), _full(blk),
                  p8, p8, pw, pw, pw, pw, _full(blk), _full(blk), _full((1, SSM_W)), _full((SSM_W, 2 * SSM_W)), ANY],
        out_specs=[_rows(SSM_W, 0, rev=nblk), _full(blk), _full(blk), _full(blk), _full(blk), pw, pw,
                   _full((1, SSM_W)), _full((SSM_W, 2 * SSM_W)), ANY],
        out_shape=[jax.ShapeDtypeStruct((t, SSM_W), BF16)] + [jax.ShapeDtypeStruct(blk, F32)] * 4
        + [jax.ShapeDtypeStruct((SUB, STATES), F32)] * 2
        + [jax.ShapeDtypeStruct((1, SSM_W), F32), jax.ShapeDtypeStruct((SSM_W, 2 * SSM_W), F32),
           jax.ShapeDtypeStruct(ride.shape, ride.dtype)],
        scratch_shapes=[big(), big(), big(), big(), small(), small(), small(), small(), small(), small(),
                        pltpu.VMEM((TM, SSM_W), F32), pltpu.VMEM((TM, SSM_W), F32)] + _chip_scatter_scratch(ride),
        compiler_params=_params())(dycat, proj, car_in_re, car_in_im, pm, pmt, mb_re, mb_im, p8_re, p8_im,
                                   q_re, q_im, qr_re, qr_im, cm_re, cm_im, dskip, wglu, ride)


def _adamw(w, g, m, v, name):
    rows = w.shape[0]
    tr = 256 if rows % 256 == 0 else rows
    c1 = 1.0 / (1.0 - ADAM_B1 ** ADAM_STEP)
    c2 = 1.0 / (1.0 - ADAM_B2 ** ADAM_STEP)

    def body(w_ref, g_ref, m_ref, v_ref, d_ref, nm_ref, nv_ref):
        gv = g_ref[...]
        m = ADAM_B1 * m_ref[...] + (1.0 - ADAM_B1) * gv
        v = ADAM_B2 * v_ref[...] + (1.0 - ADAM_B2) * (gv * gv)
        nm_ref[...] = m
        nv_ref[...] = v
        d_ref[...] = -ADAM_LR * ((m * c1) / (jnp.sqrt(v * c2) + ADAM_EPS) + ADAM_WD * w_ref[...])

    spec = pl.BlockSpec((tr, D_MODEL), lambda i: (i, 0))
    shp = jax.ShapeDtypeStruct((rows, D_MODEL), F32)
    return pl.pallas_call(
        body, name=name, grid=(rows // tr,), in_specs=[spec] * 4, out_specs=[spec] * 3, out_shape=[shp] * 3,
        compiler_params=_params())(w, g, m, v)


def _core_index():
    return lax.axis_index("c").astype(jnp.int32).reshape(1)


def _pair_add(g, theirs, out_dtype, name):
    n, half, _ = theirs.shape
    br = 128
    nb = half // br

    def body(c_ref, a_ref, b_ref, o_ref):
        o_ref[...] = (a_ref[...] + b_ref[...]).astype(out_dtype)

    spec = pl.BlockSpec((1, br, D_MODEL), lambda i, j, c: (i, j, 0))
    grid_spec = pltpu.PrefetchScalarGridSpec(
        num_scalar_prefetch=1, grid=(n, nb),
        in_specs=[pl.BlockSpec((1, br, D_MODEL), lambda i, j, c: (i, c[0] * nb + j, 0)), spec], out_specs=spec)
    return pl.pallas_call(
        body, name=name, grid_spec=grid_spec, out_shape=jax.ShapeDtypeStruct(theirs.shape, out_dtype),
        compiler_params=_params(2))(_core_index(), g, theirs)


def _adamw_rows(w, m, v, g_mine, g_theirs, row0, br, name):
    rows = w.shape[0]
    b0 = row0 // br
    per_half = g_mine.shape[0] // br
    c1 = 1.0 / (1.0 - ADAM_B1 ** ADAM_STEP)
    c2 = 1.0 / (1.0 - ADAM_B2 ** ADAM_STEP)

    def body(c_ref, w_ref, gm_ref, gt_ref, m_ref, v_ref, g_ref, d_ref, nm_ref, nv_ref):
        gv = jnp.where((b0 + pl.program_id(0)) // per_half == c_ref[0], gm_ref[...], gt_ref[...])
        m = ADAM_B1 * m_ref[...] + (1.0 - ADAM_B1) * gv
        v = ADAM_B2 * v_ref[...] + (1.0 - ADAM_B2) * (gv * gv)
        g_ref[...] = gv
        nm_ref[...] = m
        nv_ref[...] = v
        d_ref[...] = -ADAM_LR * ((m * c1) / (jnp.sqrt(v * c2) + ADAM_EPS) + ADAM_WD * w_ref[...])

    spec = pl.BlockSpec((br, D_MODEL), lambda i, c: (i, 0))
    part = pl.BlockSpec((br, D_MODEL), lambda i, c: ((b0 + i) % per_half, 0))
    shp = jax.ShapeDtypeStruct((rows, D_MODEL), F32)
    grid_spec = pltpu.PrefetchScalarGridSpec(
        num_scalar_prefetch=1, grid=(rows // br,), in_specs=[spec, part, part, spec, spec], out_specs=[spec] * 4)
    return pl.pallas_call(
        body, name=name, grid_spec=grid_spec, out_shape=[shp] * 4,
        compiler_params=_params())(_core_index(), w, g_mine, g_theirs, m, v)


def _sum_lead(a, name):
    n, rows, _ = a.shape
    tr = 128 if rows % 128 == 0 else rows

    def body(a_ref, o_ref):
        acc = a_ref[0].astype(F32)
        for k in range(1, n):
            acc = acc + a_ref[k].astype(F32)
        o_ref[...] = acc

    return pl.pallas_call(
        body, name=name, grid=(rows // tr,),
        in_specs=[pl.BlockSpec((n, tr, D_MODEL), lambda i: (0, i, 0))],
        out_specs=pl.BlockSpec((tr, D_MODEL), lambda i: (i, 0)),
        out_shape=jax.ShapeDtypeStruct((rows, D_MODEL), F32), compiler_params=_params())(a)


ANY = pl.BlockSpec(memory_space=pl.ANY)


def _mesh_pos():
    return lax.axis_index("x"), lax.axis_index("y"), lax.axis_index("c")


def _gather_phases(in_refs, out_refs, bounces, send_sems, recv_sems, local_sems):
    na = len(in_refs)
    halves = [r.shape[0] // 2 for r in in_refs]
    ncopy = 3 * na
    x, y, c = _mesh_pos()
    me = 2 * x + y
    sibling = (x, y, 1 - c)
    chips = [(1 - x, y), (x, 1 - y), (1 - x, 1 - y)]
    ids = [2 * chip[0] + chip[1] for chip in chips]

    def piece(a, q, h):
        return out_refs[a].at[q, pl.ds(h * halves[a], halves[a]), :]

    def copy(s, a, q, h, to, src=None):
        return pltpu.make_async_remote_copy(
            src_ref=piece(a, q, h) if src is None else src, dst_ref=piece(a, q, h),
            send_sem=send_sems.at[s], recv_sem=recv_sems.at[s], device_id=to, device_id_type=MESH)

    def sends():
        return [copy(j * na + a, a, me, c, (*chip, c), src=in_refs[a].at[pl.ds(c * halves[a], halves[a]), :])
                for j, chip in enumerate(chips) for a in range(na)]

    def forwards():
        return [copy(ncopy + j * na + a, a, ids[j], c, sibling) for j in range(3) for a in range(na)]

    def stores():
        return [pltpu.make_async_copy(bounces[a], out_refs[a].at[me], local_sems.at[na + a]) for a in range(na)]

    def start():
        loads = [pltpu.make_async_copy(in_refs[a], bounces[a], local_sems.at[a]) for a in range(na)]
        for cp in loads:
            cp.start()
        for cp in sends():
            cp.start()
        for ld, st in zip(loads, stores()):
            ld.wait()
            st.start()

    def forward():
        fwd = forwards()
        for j in range(3):
            for a in range(na):
                copy(j * na + a, a, ids[j], c, (x, y, c)).wait_recv()
                fwd[j * na + a].start()

    def finish():
        for j in range(3):
            for a in range(na):
                copy(ncopy + j * na + a, a, ids[j], 1 - c, (x, y, c)).wait_recv()
        for cp in sends() + forwards():
            cp.wait_send()
        for cp in stores():
            cp.wait()

    return start, forward, finish


def _gather_scratch(arrs):
    ncopy = 3 * len(arrs)
    return ([pltpu.VMEM(a.shape, a.dtype) for a in arrs]
            + [pltpu.SemaphoreType.DMA((2 * ncopy,)), pltpu.SemaphoreType.DMA((2 * ncopy,)),
               pltpu.SemaphoreType.DMA((2 * len(arrs),))])


def _gather_weights(arrs):
    na = len(arrs)

    def body(*refs):
        start, forward, finish = _gather_phases(refs[:na], refs[na:2 * na], refs[2 * na:3 * na], *refs[3 * na:])
        start()
        forward()
        finish()

    return pl.pallas_call(
        body, name="gather_weights", in_specs=[ANY] * na, out_specs=[ANY] * na,
        out_shape=[jax.ShapeDtypeStruct((4,) + a.shape, a.dtype) for a in arrs],
        scratch_shapes=_gather_scratch(arrs),
        compiler_params=pltpu.CompilerParams(vmem_limit_bytes=VMEM_LIMIT),
    )(*arrs)


def _gather_all_phases(v_ref, o_ref, bounce, send_sems, recv_sems, local_sems):
    x, y, c = _mesh_pos()
    sibling = (x, y, 1 - c)
    chips = [(1 - x, y), (x, 1 - y), (1 - x, 1 - y)]

    def blk(px, py, pc):
        return o_ref.at[4 * px + 2 * py + pc]

    def copy(k, block, to, src=None):
        return pltpu.make_async_remote_copy(
            src_ref=blk(*block) if src is None else src, dst_ref=blk(*block),
            send_sem=send_sems.at[k], recv_sem=recv_sems.at[k], device_id=to, device_id_type=MESH)

    def first():
        return ([copy(0, (x, y, c), sibling, src=v_ref)]
                + [copy(1 + j, (x, y, c), (*chip, c), src=v_ref) for j, chip in enumerate(chips)])

    def passed():
        return [copy(4 + j, (*chip, c), sibling) for j, chip in enumerate(chips)]

    def store():
        return pltpu.make_async_copy(bounce, blk(x, y, c), local_sems.at[1])

    def start():
        load = pltpu.make_async_copy(v_ref, bounce, local_sems.at[0])
        load.start()
        for cp in first():
            cp.start()
        load.wait()
        store().start()

    def forward():
        fwd = passed()
        for j, chip in enumerate(chips):
            copy(1 + j, (*chip, c), (x, y, c)).wait_recv()
            fwd[j].start()

    def finish():
        copy(0, (x, y, 1 - c), (x, y, c)).wait_recv()
        for j, chip in enumerate(chips):
            copy(4 + j, (*chip, 1 - c), (x, y, c)).wait_recv()
        for cp in first() + passed():
            cp.wait_send()
        store().wait()

    return start, forward, finish


def _gather_all_scratch(v):
    return [pltpu.VMEM(v.shape, v.dtype), pltpu.SemaphoreType.DMA((7,)), pltpu.SemaphoreType.DMA((7,)),
            pltpu.SemaphoreType.DMA((2,))]


def _scatter_and_gather(p, v, name):
    def body(p_ref, v_ref, got_ref, o_ref, p_bounce, p_send, p_recv, p_local, bounce, send_sems, recv_sems,
             local_sems):
        start, finish = _chip_scatter_phases(p_ref, got_ref, p_bounce, p_send, p_recv, p_local)
        g_start, g_forward, g_finish = _gather_all_phases(v_ref, o_ref, bounce, send_sems, recv_sems, local_sems)
        start()
        g_start()
        g_forward()
        g_finish()
        finish()

    return pl.pallas_call(
        body, name=name, in_specs=[ANY, ANY], out_specs=[ANY, ANY],
        out_shape=[jax.ShapeDtypeStruct(p.shape, p.dtype), jax.ShapeDtypeStruct((8,) + v.shape, v.dtype)],
        scratch_shapes=_chip_scatter_scratch(p) + _gather_all_scratch(v),
    )(p, v)


def _pair_split_phases(g_ref, theirs_ref, send_sems, recv_sems):
    n, rows, _ = g_ref.shape
    half = rows // 2
    ch = half // COMM_CHUNKS
    x, y, c = _mesh_pos()

    def gives():
        return [pltpu.make_async_remote_copy(
            src_ref=g_ref.at[q, pl.ds((1 - c) * half + k * ch, ch), :],
            dst_ref=theirs_ref.at[q, pl.ds(k * ch, ch), :],
            send_sem=send_sems.at[q * COMM_CHUNKS + k], recv_sem=recv_sems.at[q * COMM_CHUNKS + k],
            device_id=(x, y, 1 - c), device_id_type=MESH) for q in range(n) for k in range(COMM_CHUNKS)]

    def start():
        for cp in gives():
            cp.start()

    def finish():
        for cp in gives():
            cp.wait()

    return start, finish


def _pair_split_scratch(g):
    return [pltpu.SemaphoreType.DMA((g.shape[0] * COMM_CHUNKS,)), pltpu.SemaphoreType.DMA((g.shape[0] * COMM_CHUNKS,))]


def _pair_split_shape(g):
    return jax.ShapeDtypeStruct((g.shape[0], g.shape[1] // 2, D_MODEL), g.dtype)


def _pair_split(g, name):
    def body(g_ref, theirs_ref, send_sems, recv_sems):
        start, finish = _pair_split_phases(g_ref, theirs_ref, send_sems, recv_sems)
        start()
        finish()

    return pl.pallas_call(
        body, name=name, in_specs=[ANY], out_specs=ANY, out_shape=_pair_split_shape(g),
        scratch_shapes=_pair_split_scratch(g))(g)


def _chip_scatter_phases(p_ref, o_ref, bounce, send_sems, recv_sems, local_sems):
    x, y, c = _mesh_pos()
    me = 2 * x + y
    chips = [(1 - x, y), (x, 1 - y), (1 - x, 1 - y)]

    def keep():
        return pltpu.make_async_copy(bounce, o_ref.at[me], local_sems.at[1])

    def sends():
        return [pltpu.make_async_remote_copy(
            src_ref=p_ref.at[2 * chip[0] + chip[1]], dst_ref=o_ref.at[me],
            send_sem=send_sems.at[j], recv_sem=recv_sems.at[j], device_id=(*chip, c), device_id_type=MESH)
            for j, chip in enumerate(chips)]

    def start():
        load = pltpu.make_async_copy(p_ref.at[me], bounce, local_sems.at[0])
        load.start()
        for cp in sends():
            cp.start()
        load.wait()
        keep().start()

    def finish():
        for j, chip in enumerate(chips):
            q = 2 * chip[0] + chip[1]
            pltpu.make_async_remote_copy(
                src_ref=p_ref.at[q], dst_ref=o_ref.at[q], send_sem=send_sems.at[j], recv_sem=recv_sems.at[j],
                device_id=(*chip, c), device_id_type=MESH).wait_recv()
        for cp in sends():
            cp.wait_send()
        keep().wait()

    return start, finish


def _chip_scatter_scratch(p):
    return [pltpu.VMEM(p.shape[1:], p.dtype), pltpu.SemaphoreType.DMA((3,)), pltpu.SemaphoreType.DMA((3,)),
            pltpu.SemaphoreType.DMA((2,))]


def _pair_join_phases(r_ref, o_ref, send_sems, recv_sems):
    ch = r_ref.shape[0] // COMM_CHUNKS
    x, y, c = _mesh_pos()

    def gives():
        return [pltpu.make_async_remote_copy(
            src_ref=r_ref.at[pl.ds(k * ch, ch), :], dst_ref=o_ref.at[pl.ds(k * ch, ch), :],
            send_sem=send_sems.at[k], recv_sem=recv_sems.at[k], device_id=(x, y, 1 - c), device_id_type=MESH)
            for k in range(COMM_CHUNKS)]

    def start():
        for cp in gives():
            cp.start()

    def finish():
        for cp in gives():
            cp.wait()

    return start, finish


def _pair_join_scratch():
    return [pltpu.SemaphoreType.DMA((COMM_CHUNKS,)), pltpu.SemaphoreType.DMA((COMM_CHUNKS,))]


def _pair_join(r, name):
    def body(r_ref, o_ref, send_sems, recv_sems):
        start, finish = _pair_join_phases(r_ref, o_ref, send_sems, recv_sems)
        start()
        finish()

    return pl.pallas_call(
        body, name=name, in_specs=[ANY], out_specs=ANY, out_shape=jax.ShapeDtypeStruct(r.shape, r.dtype),
        scratch_shapes=_pair_join_scratch())(r)


SHARD_BIG = (("even_w_in", (1024, 512)), ("ssm_w_glu", (512, 256)), ("even_w_out", (256, 1024)),
             ("odd_w_in", (1024, 768)), ("odd_w_out", (256, 1024)))
SHARD_SMALL = (("odd_norm", 1), ("conv_w", CONV_K), ("conv_b", 1), ("conv_ln_g", 1), ("conv_ln_b", 1))
REP_NAMES = (("even_norm", (1024,)), ("pool_w", (4, 128, 128)), ("pool_scale", (512,)), ("ssm_log_dt", (32,)),
             ("ssm_a_re", (32, 64)), ("ssm_a_im", (32, 64)), ("ssm_b_re", (32, 64, 16)), ("ssm_b_im", (32, 64, 16)),
             ("ssm_c_re", (32, 16, 64)), ("ssm_c_im", (32, 16, 64)), ("ssm_d", (512,)), ("final_norm", (1024,)))


def _pack_rep(d):
    flat = jnp.concatenate([d[n].reshape(-1) for n, _ in REP_NAMES])
    return jnp.pad(flat, (0, REP_ROWS * D_MODEL - flat.shape[0])).reshape(REP_ROWS, D_MODEL)


def _unpack_rep(buf):
    flat = buf.reshape(-1)
    out = {}
    off = 0
    for n, shp in REP_NAMES:
        size = 1
        for s in shp:
            size *= s
        out[n] = flat[off:off + size].reshape(shp)
        off += size
    return out


def _cols_split(full, cols):
    rows = full.shape[0]
    return full.reshape(rows, 4, cols).transpose(1, 0, 2).reshape(4, -1, D_MODEL)


def _block_diag(a):
    a = a.reshape(4, 8, GROUP_DIM, N_STATE)
    eye = jnp.eye(8, dtype=a.dtype)
    return (a[:, :, :, None, :] * eye[None, :, None, :, None]).reshape(4, 128, LCH)


def _block_diag_take(m):
    m = m.reshape(4, 8, GROUP_DIM, 8, N_STATE)
    eye = jnp.eye(8, dtype=m.dtype)
    return jnp.sum(m * eye[None, :, None, :, None], axis=3).reshape(N_GROUPS, GROUP_DIM, N_STATE)


def _ssm_discretise(log_dt, a_re, a_im, b_re, b_im):
    dt = jnp.exp(log_dt)[:, None]
    mag = jnp.exp(a_re * dt)
    ang = a_im * dt
    abar_re = mag * jnp.cos(ang)
    abar_im = mag * jnp.sin(ang)
    den = a_re * a_re + a_im * a_im
    nr = abar_re - 1.0
    ni = abar_im
    k_re = (nr * a_re + ni * a_im) / den
    k_im = (ni * a_re - nr * a_im) / den
    bb_re = k_re[..., None] * b_re - k_im[..., None] * b_im
    bb_im = k_re[..., None] * b_im + k_im[..., None] * b_re
    return abar_re, abar_im, bb_re, bb_im


def _scan_tables(log_dt, a_re, a_im):
    dt = jnp.exp(log_dt)[:, None]
    lam_re = (a_re * dt).reshape(1, STATES)
    lam_im = (a_im * dt).reshape(1, STATES)

    def powers(k):
        mag = jnp.exp(k * lam_re)
        return mag * jnp.cos(k * lam_im), mag * jnp.sin(k * lam_im)

    p_re, p_im = powers((1 + jnp.arange(TM) // SUB).astype(F32)[:, None])
    q_re, q_im = powers((SEG_LEN * (1 + jnp.arange(SUB))).astype(F32)[:, None])
    return p_re, p_im, q_re, q_im


def _local_step(x, tgt, w, shard):
    row = lambda a: a.reshape(1, -1)
    (e_w_in,) = _gather_weights([shard["even_w_in"].astype(BF16)])
    wp = w["pool_w"].astype(BF16)
    ssm_in = (w["ssm_log_dt"], w["ssm_a_re"], w["ssm_a_im"], w["ssm_b_re"], w["ssm_b_im"])
    (abar_re, abar_im, bb_re, bb_im), ssm_vjp = jax.vjp(_ssm_discretise, *ssm_in)
    mb_re = _block_diag(bb_re.transpose(0, 2, 1)).astype(BF16)
    mb_im = _block_diag(bb_im.transpose(0, 2, 1)).astype(BF16)
    cm_re = _block_diag(w["ssm_c_re"]).astype(BF16)
    cm_im = _block_diag(w["ssm_c_im"]).astype(BF16)
    p8_re, p8_im, q_re, q_im = _scan_tables(w["ssm_log_dt"], w["ssm_a_re"], w["ssm_a_im"])
    qr_re, qr_im = q_re[::-1], q_im[::-1]
    pm = _perm_matrix()
    pmt = pm.T
    g0, gf = row(w["even_norm"]), row(w["final_norm"])
    ps, dskip = row(w["pool_scale"]), row(w["ssm_d"])

    proj, yp, (g_glu, g_eout) = _norm_in(
        x, g0, e_w_in, "even_in", ride=[shard["ssm_w_glu"].astype(BF16), shard["even_w_out"].astype(BF16)],
        pool=(wp, ps))
    wglu = g_glu.transpose(1, 0, 2).reshape(SSM_W, 2 * SSM_W)
    e_w_out = g_eout.reshape(D_MODEL, D_MODEL)
    (ys, car_re, car_im), (g_oin, g_oout, g_small) = _ssm_fwd(
        proj, pm, pmt, mb_re, mb_im, p8_re, p8_im, q_re, q_im, cm_re, cm_im, dskip, wglu, ride=_odd_shards(shard))
    o_w_in, o_w_out = g_oin, g_oout.reshape(D_MODEL, D_MODEL)
    sm = g_small.transpose(1, 0, 2).reshape(SMALL_ROWS, D_MODEL)
    cw = sm[1:1 + HALO]
    g1, cb, lg, lb = sm[0:1], sm[32:33], sm[33:34], sm[34:35]
    x1, yg = _even_out(yp, ys, proj, x, e_w_out)
    q, _, _ = _norm_in(x1, g1, o_w_in, "odd_in")
    y2, cv = _conv_fwd(q, cw, cb, lg, lb)
    dx2, loss_lanes, d_gf = _odd_out_loss(y2, x1, o_w_out, gf, tgt)

    dcv, dz2, d_o_w_out, d_lg, d_lb = _odd_bwd_out(dx2, o_w_out, y2, cv, q, lg, lb)
    dval, dgate, d_cw, d_cb = _conv_bwd(dcv, q, cw)
    dx1, d_g1, d_o_w_in = _in_bwd([dval, dgate, dz2], o_w_in, x1, g1, dx2, "odd_in_bwd")
    g_odd = _pack_odd_grads({
        "odd_w_in": d_o_w_in, "odd_w_out": d_o_w_out, "odd_norm": d_g1.reshape(-1),
        "conv_w": d_cw.reshape(HALO, SUB, D_MODEL).sum(axis=1)[:CONV_K], "conv_b": d_cb.reshape(-1),
        "conv_ln_g": d_lg.reshape(-1), "conv_ln_b": d_lb.reshape(-1)})
    dycat, dz, d_e_w_out, dup, d_wp, d_ps, theirs_odd = _even_bwd_out(dx1, e_w_out, yg, yp, ys, proj, wp, ps, g_odd)
    sums_odd = _pair_add(g_odd, theirs_odd, BF16, "pair_add_odd")
    (dus, d_mb_re, d_mb_im, d_cm_re, d_cm_im, da_re, da_im, d_dskip, d_wglu, got_odd) = _ssm_bwd(
        dycat, proj, car_re, car_im, pm, pmt, mb_re, mb_im, p8_re, p8_im, q_re, q_im, qr_re, qr_im,
        cm_re, cm_im, dskip, wglu, sums_odd)
    d_abar_re = jnp.sum(da_re, axis=0).reshape(N_GROUPS, N_STATE)
    d_abar_im = jnp.sum(da_im, axis=0).reshape(N_GROUPS, N_STATE)
    d_bb_re = _block_diag_take(d_mb_re).transpose(0, 2, 1)
    d_bb_im = _block_diag_take(d_mb_im).transpose(0, 2, 1)
    d_log_dt, d_a_re, d_a_im, d_b_re, d_b_im = ssm_vjp((d_abar_re, d_abar_im, d_bb_re, d_bb_im))
    rep_early = _pack_rep({
        "even_norm": jnp.zeros((D_MODEL,), F32), "pool_w": d_wp, "pool_scale": d_ps.reshape(-1),
        "ssm_log_dt": d_log_dt, "ssm_a_re": d_a_re, "ssm_a_im": d_a_im, "ssm_b_re": d_b_re, "ssm_b_im": d_b_im,
        "ssm_c_re": _block_diag_take(d_cm_re), "ssm_c_im": _block_diag_take(d_cm_im),
        "ssm_d": d_dskip.reshape(-1), "final_norm": d_gf.reshape(-1)})
    odd_mine = _sum_lead(got_odd, "chip_sum_odd")
    dx, d_g0, d_e_w_in, rep_parts, odd_theirs = _in_bwd([dup, dus, dz], e_w_in, x, g0, dx1, "even_in_bwd",
                                                        ride=(rep_early, odd_mine))

    grads = {"even_norm": d_g0, "even_w_in": d_e_w_in, "ssm_w_glu": d_wglu, "even_w_out": d_e_w_out}
    return jnp.sum(loss_lanes), dx, grads, (odd_mine, odd_theirs), rep_parts


WEIGHT_NAMES = ("even_norm", "even_w_in", "pool_w", "pool_scale", "ssm_log_dt", "ssm_a_re", "ssm_a_im",
                "ssm_b_re", "ssm_b_im", "ssm_c_re", "ssm_c_im", "ssm_d", "ssm_w_glu", "even_w_out", "odd_norm",
                "odd_w_in", "conv_w", "conv_b", "conv_ln_g", "conv_ln_b", "odd_w_out", "final_norm")
SHARDED = tuple(n for n, _ in SHARD_BIG) + tuple(n for n, _ in SHARD_SMALL)


SMALL_ROWS = 64


def _odd_shards(shard):
    small = jnp.concatenate([shard[n].reshape(r, 256) for n, r in SHARD_SMALL], axis=0)
    small = jnp.pad(small, ((0, SMALL_ROWS - small.shape[0]), (0, 0)))
    return [shard["odd_w_in"].astype(BF16), shard["odd_w_out"].astype(BF16), small]


def _pack_small(d):
    small = jnp.concatenate([d[n].reshape(r, -1) for n, r in SHARD_SMALL], axis=0)
    if small.shape[1] == D_MODEL:
        small = small.reshape(35, 4, 256).transpose(1, 0, 2)
    small = small.reshape(-1, 35 * 256)
    small = jnp.pad(small, ((0, 0), (0, ROWS_SMALL * D_MODEL - 35 * 256)))
    return small.reshape(-1, ROWS_SMALL, D_MODEL)


def _unpack_small(buf):
    small = buf.reshape(-1)[:35 * 256].reshape(35, 256)
    out = {}
    off = 0
    for n, r in SHARD_SMALL:
        out[n] = small[off:off + r].reshape((r, 256) if r > 1 else (256,))
        off += r
    return out


EVEN_PACK = (("even_w_in", 0, 512), ("ssm_w_glu", 512, 128), ("even_w_out", 640, 256))
ROWS_EVEN = 1024
ODD_PACK = (("odd_w_in", 0, 768), ("odd_w_out", 768, 256))
ODD_SMALL_ROW = 1024
ROWS_ODD = 1280


def _pack_even_grads(g):
    parts = [g["even_w_in"].reshape(4, -1, D_MODEL), _cols_split(g["ssm_w_glu"], 256),
             g["even_w_out"].reshape(4, -1, D_MODEL), jnp.zeros((4, ROWS_EVEN - 896, D_MODEL), F32)]
    return jnp.concatenate(parts, axis=1)


def _pack_odd_grads(g):
    parts = [g["odd_w_in"].reshape(4, -1, D_MODEL), g["odd_w_out"].reshape(4, -1, D_MODEL), _pack_small(g),
             jnp.zeros((4, ROWS_ODD - ODD_SMALL_ROW - ROWS_SMALL, D_MODEL), F32)]
    return jnp.concatenate(parts, axis=1)


def kernel(x, even_norm, even_w_in, pool_w, pool_scale, ssm_log_dt, ssm_a_re, ssm_a_im, ssm_b_re, ssm_b_im, ssm_c_re, ssm_c_im, ssm_d, ssm_w_glu, even_w_out, odd_norm, odd_w_in, conv_w, conv_b, conv_ln_g, conv_ln_b, odd_w_out, final_norm, loss_target, m_even_norm, m_even_w_in, m_pool_w, m_pool_scale, m_ssm_log_dt, m_ssm_a_re, m_ssm_a_im, m_ssm_b_re, m_ssm_b_im, m_ssm_c_re, m_ssm_c_im, m_ssm_d, m_ssm_w_glu, m_even_w_out, m_odd_norm, m_odd_w_in, m_conv_w, m_conv_b, m_conv_ln_g, m_conv_ln_b, m_odd_w_out, m_final_norm, v_even_norm, v_even_w_in, v_pool_w, v_pool_scale, v_ssm_log_dt, v_ssm_a_re, v_ssm_a_im, v_ssm_b_re, v_ssm_b_im, v_ssm_c_re, v_ssm_c_im, v_ssm_d, v_ssm_w_glu, v_even_w_out, v_odd_norm, v_odd_w_in, v_conv_w, v_conv_b, v_conv_ln_g, v_conv_ln_b, v_odd_w_out, v_final_norm):
    ws = dict(zip(WEIGHT_NAMES, (even_norm, even_w_in, pool_w, pool_scale, ssm_log_dt, ssm_a_re, ssm_a_im, ssm_b_re,
                                 ssm_b_im, ssm_c_re, ssm_c_im, ssm_d, ssm_w_glu, even_w_out, odd_norm, odd_w_in,
                                 conv_w, conv_b, conv_ln_g, conv_ln_b, odd_w_out, final_norm)))
    ms = dict(zip(WEIGHT_NAMES, (m_even_norm, m_even_w_in, m_pool_w, m_pool_scale, m_ssm_log_dt, m_ssm_a_re,
                                 m_ssm_a_im, m_ssm_b_re, m_ssm_b_im, m_ssm_c_re, m_ssm_c_im, m_ssm_d, m_ssm_w_glu,
                                 m_even_w_out, m_odd_norm, m_odd_w_in, m_conv_w, m_conv_b, m_conv_ln_g, m_conv_ln_b,
                                 m_odd_w_out, m_final_norm)))
    vs = dict(zip(WEIGHT_NAMES, (v_even_norm, v_even_w_in, v_pool_w, v_pool_scale, v_ssm_log_dt, v_ssm_a_re,
                                 v_ssm_a_im, v_ssm_b_re, v_ssm_b_im, v_ssm_c_re, v_ssm_c_im, v_ssm_d, v_ssm_w_glu,
                                 v_even_w_out, v_odd_norm, v_odd_w_in, v_conv_w, v_conv_b, v_conv_ln_g, v_conv_ln_b,
                                 v_odd_w_out, v_final_norm)))
    lead = {n: a.shape for n, a in ws.items()}
    drop = lambda d: {n: (a[0] if n != "final_norm" else a) for n, a in d.items()}
    ws, ms, vs = drop(ws), drop(ms), drop(vs)

    shard = {n: ws[n] for n in SHARDED}
    rep = {n: ws[n] for n, _ in REP_NAMES}
    loss_part, grad_x, grads, (odd_mine, odd_theirs), rep_parts = _local_step(x[0], loss_target[0], rep, shard)
    loss = lax.psum(loss_part, ("x", "y", "c"))

    g_even = _pack_even_grads(grads)
    got_even, late_parts = _scatter_and_gather(
        _pair_add(g_even, _pair_split(g_even, "pair_split_even"), BF16, "pair_add_even"),
        jnp.pad(grads["even_norm"], ((0, SUB - 1), (0, 0))), "scatter_even_gather_late")
    even_mine = _sum_lead(got_even, "chip_sum_even")
    even_theirs = _pair_join(even_mine, "pair_join_even")
    outs = [{}, {}, {}, {}]
    for pack, mine, theirs in ((EVEN_PACK, even_mine, even_theirs), (ODD_PACK, odd_mine, odd_theirs)):
        for n, row0, rows in pack:
            view = lambda a: a.reshape(rows, D_MODEL)
            res = _adamw_rows(view(ws[n]), view(ms[n]), view(vs[n]), mine, theirs, row0, 128, "adamw_" + n)
            for o, r in zip(outs, res):
                o[n] = r
    small = lambda d: _pack_small({n: d[n] for n, _ in SHARD_SMALL})[0]
    res = _adamw_rows(small(ws), small(ms), small(vs), odd_mine, odd_theirs, ODD_SMALL_ROW, ROWS_SMALL, "adamw_small")
    for o, r in zip(outs, res):
        o.update(_unpack_small(r))
    g_rep = lax.dynamic_update_slice(_sum_lead(rep_parts, "rep_sum"), _sum_lead(late_parts, "late_sum")[0:1], (0, 0))
    res = _adamw(_pack_rep(rep), g_rep, _pack_rep({n: ms[n] for n, _ in REP_NAMES}),
                 _pack_rep({n: vs[n] for n, _ in REP_NAMES}), "adamw_rep")
    for o, r in zip(outs, (g_rep,) + tuple(res)):
        o.update(_unpack_rep(r))

    leaves = [[o[n].reshape(lead[n]) for n in WEIGHT_NAMES] for o in outs]
    return (loss, grad_x[None], *leaves[0], *leaves[1], *leaves[2], *leaves[3])
```

```python
import jax
import jax.numpy as jnp
from jax import lax
from jax.experimental import pallas as pl
from jax.experimental.pallas import tpu as pltpu

F32 = jnp.float32
BF16 = jnp.bfloat16
MESH = pl.DeviceIdType.MESH

D_MODEL = 1024
RMS_EPS = 1e-6
LN_EPS = 1e-5
N_GROUPS = 32
GROUP_DIM = 16
N_STATE = 64
STATES = N_GROUPS * N_STATE
SSM_W = 512
POOL_W = 512
CONV_K = 31
HALO = 32
POOL_HALO = 16

ADAM_LR = 0.001
ADAM_B1 = 0.9
ADAM_B2 = 0.999
ADAM_EPS = 1e-08
ADAM_WD = 0.01
ADAM_STEP = 10

TM = 256
TM_MM = 512
DEPTH = 3
SUB = 8
LCH = 512
SCAN_L = 1024
VMEM_LIMIT = 56 * 1024 * 1024

ROWS_SMALL = 16
REP_ROWS = 200
COMM_CHUNKS = 4


def _params(n_axes=1):
    return pltpu.CompilerParams(dimension_semantics=("arbitrary",) * n_axes, vmem_limit_bytes=VMEM_LIMIT)


def _rows(w, cb=0, rev=None, tm=TM):
    if rev is None:
        return pl.BlockSpec((tm, w), lambda i: (i, cb))
    return pl.BlockSpec((tm, w), lambda i: (rev - 1 - i, cb))


def _mm_rows(w, cb=0):
    return _rows(w, cb, tm=TM_MM)


def _full(shape):
    n = len(shape)
    return pl.BlockSpec(shape, lambda i: (0,) * n)


def _prev(hr, w, cb=0, tm=TM):
    r = tm // hr
    return pl.BlockSpec((hr, w), lambda i: (jnp.maximum(i * r - 1, 0), cb))


def _next(hr, w, nrows, cb=0, tm=TM):
    r = tm // hr
    last = nrows // hr - 1
    return pl.BlockSpec((hr, w), lambda i: (jnp.minimum((i + 1) * r, last), cb))


def _dot(a, b):
    return jnp.dot(a, b, preferred_element_type=F32)


def _dot_nt(a, b):
    return lax.dot_general(a, b, (((1,), (1,)), ((), ())), preferred_element_type=F32)


def _dot_tn(a, b):
    return lax.dot_general(a, b, (((0,), (0,)), ((), ())), preferred_element_type=F32)


def _sig(x):
    return 1.0 / (1.0 + jnp.exp(-x))


def _zero_at_first(i, *refs):
    @pl.when(i == 0)
    def _():
        for r in refs:
            r[...] = jnp.zeros_like(r)


def _norm_in(x, g, w, name, ride=(), pool=None):
    t, ns = x.shape[0], w.shape[2]
    n = 4 * ns
    ng = len(ride)
    npool = 0 if pool is None else 1
    nstep = t // TM_MM

    def body(x_ref, g_ref, w_ref, *rest):
        pool_in, rest = rest[:2 * npool], rest[2 * npool:]
        ride_in, o_ref, rest = rest[:ng], rest[ng], rest[ng + 1:]
        yp_ref, rest = (rest[0], rest[1:]) if npool else (None, rest)
        ride_out, rest = rest[:ng], rest[ng:]
        halo_ref, rest = (rest[0], rest[1:]) if npool else (None, rest)
        i = pl.program_id(0)
        if ng:
            start, forward, finish = _gather_phases(ride_in, ride_out, rest[:ng], *rest[ng:])
            pl.when(i == 0)(start)
            pl.when(i == nstep // 2)(forward)
        xv = x_ref[...]
        r = lax.rsqrt(jnp.mean(xv * xv, axis=-1, keepdims=True) + RMS_EPS)
        h = (xv * r * g_ref[...]).astype(BF16)
        first = _dot(h, w_ref[0]).astype(BF16)
        o_ref[:, 0:ns] = first
        if npool:
            wp_ref, ps_ref = pool_in
            _zero_at_first(i, halo_ref)
            pos = (i * TM_MM + 1 + lax.broadcasted_iota(jnp.int32, (TM_MM, 1), 0)).astype(F32)
            for gi in range(4):
                sl = slice(128 * gi, 128 * (gi + 1))
                u = first[:, sl].astype(F32)
                s = _pool_sums(jnp.concatenate([halo_ref[:, sl], u], axis=0), gi, True)[POOL_HALO:, :]
                pooled = s / jnp.minimum(pos, float(2 << gi)) - u
                yp_ref[:, sl] = (_dot(pooled.astype(BF16), wp_ref[gi]) * ps_ref[:, sl]).astype(BF16)
                halo_ref[:, sl] = u[TM_MM - POOL_HALO:, :]
        for s in range(1, 4):
            o_ref[:, s * ns:(s + 1) * ns] = _dot(h, w_ref[s]).astype(BF16)
        if ng:
            pl.when(i == nstep - 1)(finish)

    pool_args = [] if pool is None else list(pool)
    res = pl.pallas_call(
        body, name=name, grid=(nstep,),
        in_specs=[_mm_rows(D_MODEL), _full((1, D_MODEL)), _full(w.shape)] + [_full(a.shape) for a in pool_args]
        + [ANY] * ng,
        out_specs=[_mm_rows(n)] + [_mm_rows(POOL_W)] * npool + [ANY] * ng,
        out_shape=[jax.ShapeDtypeStruct((t, n), BF16)] + [jax.ShapeDtypeStruct((t, POOL_W), BF16)] * npool
        + [jax.ShapeDtypeStruct((4,) + a.shape, a.dtype) for a in ride],
        scratch_shapes=[pltpu.VMEM((POOL_HALO, POOL_W), F32)] * npool + (_gather_scratch(ride) if ng else []),
        compiler_params=_params())(x, g, w, *pool_args, *ride)
    return res[0], (res[1] if npool else None), res[1 + npool:]


def _pool_sums(ext, g, forward):
    n = ext.shape[0]
    s = ext
    for step in range(g + 1):
        k = 1 << step
        s = s + pltpu.roll(s, k if forward else n - k, 0)
    return s


SEG_LEN = TM // SUB


def _perm_matrix():
    p = jnp.arange(TM)
    src = (p % SUB) * SEG_LEN + p // SUB
    return (src[:, None] == jnp.arange(TM)[None, :]).astype(BF16)


def _cmul_add(are, aim, vre, vim, bre, bim):
    return are * vre - aim * vim + bre, are * vim + aim * vre + bim


def _segment_chain(ere, eim, qre, qim, cin_re, cin_im, row, up):
    for sh in (1, 2, 4):
        mre, mim = (qre[SUB - sh:SUB - sh + 1, :], qim[SUB - sh:SUB - sh + 1, :]) if up else \
                   (qre[sh - 1:sh, :], qim[sh - 1:sh, :])
        keep = (row < SUB - sh) if up else (row >= sh)
        sre = jnp.where(keep, pltpu.roll(ere, SUB - sh if up else sh, 0), 0.0)
        sim = jnp.where(keep, pltpu.roll(eim, SUB - sh if up else sh, 0), 0.0)
        ere, eim = _cmul_add(mre, mim, sre, sim, ere, eim)
    ere, eim = _cmul_add(qre, qim, cin_re, cin_im, ere, eim)
    keep = (row < SUB - 1) if up else (row >= 1)
    ent_re = jnp.where(keep, pltpu.roll(ere, SUB - 1 if up else 1, 0), cin_re)
    ent_im = jnp.where(keep, pltpu.roll(eim, SUB - 1 if up else 1, 0), cin_im)
    return ere, eim, ent_re, ent_im


def _scan_fwd_block(xs_re, xs_im, p8_re, p8_im, q_re, q_im, car_re, car_im, ent_re_ref, ent_im_ref):
    row = lax.broadcasted_iota(jnp.int32, (SUB, SCAN_L), 0)
    for j in range(STATES // SCAN_L):
        sl = slice(SCAN_L * j, SCAN_L * (j + 1))
        are, aim = p8_re[0:SUB, sl], p8_im[0:SUB, sl]

        def totals(i, v, sl=sl, are=are, aim=aim):
            r0 = pl.multiple_of(i * SUB, SUB)
            vre, vim = _cmul_add(are, aim, v[0], v[1], xs_re[pl.ds(r0, SUB), sl], xs_im[pl.ds(r0, SUB), sl])
            xs_re[pl.ds(r0, SUB), sl] = vre
            xs_im[pl.ds(r0, SUB), sl] = vim
            return vre, vim

        ere, eim = lax.fori_loop(1, SEG_LEN, totals, (xs_re[0:SUB, sl], xs_im[0:SUB, sl]), unroll=2)
        ere, eim, cre, cim = _segment_chain(ere, eim, q_re[:, sl], q_im[:, sl],
                                            car_re[:, sl], car_im[:, sl], row, False)
        car_re[:, sl] = jnp.broadcast_to(ere[SUB - 1:SUB, :], (SUB, SCAN_L))
        car_im[:, sl] = jnp.broadcast_to(eim[SUB - 1:SUB, :], (SUB, SCAN_L))
        if ent_re_ref is not None:
            ent_re_ref[:, sl] = cre
            ent_im_ref[:, sl] = cim

        def fix(i, c, sl=sl, cre=cre, cim=cim):
            r0 = pl.multiple_of(i * SUB, SUB)
            vre, vim = _cmul_add(p8_re[pl.ds(r0, SUB), sl], p8_im[pl.ds(r0, SUB), sl], cre, cim,
                                 xs_re[pl.ds(r0, SUB), sl], xs_im[pl.ds(r0, SUB), sl])
            xs_re[pl.ds(r0, SUB), sl] = vre
            xs_im[pl.ds(r0, SUB), sl] = vim
            return c

        lax.fori_loop(0, SEG_LEN, fix, 0, unroll=2)


def _ssm_fwd(proj, pm, pmt, mb_re, mb_im, p8_re, p8_im, q_re, q_im, cm_re, cm_im, dskip, wglu, ride=()):
    t = proj.shape[0]
    nblk = t // TM

    ng = len(ride)

    def body(u_ref, pm_ref, pmt_ref, mbre, mbim, p8re, p8im, qre, qim, cmre, cmim, d_ref, wg_ref, *rest):
        ride_in, (y_ref, cre_ref, cim_ref), ride_out = rest[:ng], rest[ng:ng + 3], rest[ng + 3:2 * ng + 3]
        xs_re, xs_im, car_re, car_im, ysk = rest[2 * ng + 3:2 * ng + 8]
        i = pl.program_id(0)
        if ng:
            start, forward, finish = _gather_phases(ride_in, ride_out, rest[2 * ng + 8:3 * ng + 8],
                                                    *rest[3 * ng + 8:])
            pl.when(i == 0)(start)
            pl.when(i == nblk // 2)(forward)
        _zero_at_first(i, car_re, car_im)
        cre_ref[0] = car_re[...]
        cim_ref[0] = car_im[...]
        us = _dot(pm_ref[...], u_ref[...])
        usb = us.astype(BF16)
        for j in range(4):
            xs_re[:, LCH * j:LCH * (j + 1)] = _dot(usb[:, 128 * j:128 * (j + 1)], mbre[j])
            xs_im[:, LCH * j:LCH * (j + 1)] = _dot(usb[:, 128 * j:128 * (j + 1)], mbim[j])
        _scan_fwd_block(xs_re, xs_im, p8re, p8im, qre, qim, car_re, car_im, None, None)
        for j in range(4):
            sl = slice(LCH * j, LCH * (j + 1))
            ysk[:, 128 * j:128 * (j + 1)] = (_dot_nt(xs_re[:, sl].astype(BF16), cmre[j])
                                             - _dot_nt(xs_im[:, sl].astype(BF16), cmim[j]))
        yv = ysk[...] + d_ref[...] * us
        gv = _dot(yv.astype(BF16), wg_ref[...])
        y_ref[...] = _dot(pmt_ref[...], (gv[:, :SSM_W] * _sig(gv[:, SSM_W:])).astype(BF16)).astype(BF16)
        if ng:
            pl.when(i == nblk - 1)(finish)

    blk = (4, 128, LCH)
    res = pl.pallas_call(
        body, name="ssm_fwd", grid=(nblk,),
        in_specs=[_rows(SSM_W, 1), _full((TM, TM)), _full((TM, TM)), _full(blk), _full(blk),
                  _full((TM, STATES)), _full((TM, STATES)), _full((SUB, STATES)), _full((SUB, STATES)),
                  _full(blk), _full(blk), _full((1, SSM_W)), _full((SSM_W, 2 * SSM_W))] + [ANY] * ng,
        out_specs=[_rows(SSM_W), pl.BlockSpec((1, SUB, STATES), lambda i: (i, 0, 0)),
                   pl.BlockSpec((1, SUB, STATES), lambda i: (i, 0, 0))] + [ANY] * ng,
        out_shape=[jax.ShapeDtypeStruct((t, SSM_W), BF16), jax.ShapeDtypeStruct((nblk, SUB, STATES), F32),
                   jax.ShapeDtypeStruct((nblk, SUB, STATES), F32)]
        + [jax.ShapeDtypeStruct((4,) + a.shape, a.dtype) for a in ride],
        scratch_shapes=[pltpu.VMEM((TM, STATES), F32), pltpu.VMEM((TM, STATES), F32),
                        pltpu.VMEM((SUB, STATES), F32), pltpu.VMEM((SUB, STATES), F32),
                        pltpu.VMEM((TM, SSM_W), F32)] + (_gather_scratch(ride) if ng else []),
        compiler_params=_params())(proj, pm, pmt, mb_re, mb_im, p8_re, p8_im, q_re, q_im, cm_re, cm_im, dskip, wglu,
                                   *ride)
    return res[:3], res[3:]


def _even_out(yp, ys, proj, x, w):
    t = x.shape[0]

    def body(yp_ref, ys_ref, z_ref, x_ref, w_ref, x1_ref, yg_ref):
        z = z_ref[...].astype(F32)
        sz = z * _sig(z)
        gp = (yp_ref[...].astype(F32) * sz[:, :POOL_W]).astype(BF16)
        gs = (ys_ref[...].astype(F32) * sz[:, POOL_W:]).astype(BF16)
        yg_ref[:, :POOL_W] = gp
        yg_ref[:, POOL_W:] = gs
        x1_ref[...] = x_ref[...] + _dot(gp, w_ref[:POOL_W, :]) + _dot(gs, w_ref[POOL_W:, :])

    return pl.pallas_call(
        body, name="even_out", grid=(t // TM_MM,),
        in_specs=[_mm_rows(POOL_W), _mm_rows(SSM_W), _mm_rows(D_MODEL, 1), _mm_rows(D_MODEL),
                  _full((D_MODEL, D_MODEL))],
        out_specs=[_mm_rows(D_MODEL), _mm_rows(D_MODEL)],
        out_shape=[jax.ShapeDtypeStruct((t, D_MODEL), F32), jax.ShapeDtypeStruct((t, D_MODEL), BF16)],
        compiler_params=_params())(yp, ys, proj, x, w)


def _phase_copies(ext, cp):
    n = cp.shape[1]
    for j in range(1, SUB):
        cp[j - 1] = ext[pl.ds(j, n), :]


def _shifted(ext, cp, off, nrows, sl, row0=0):
    q, j = divmod(off, SUB)
    if j == 0:
        return ext[pl.ds(row0 + SUB * q, nrows), sl]
    return cp[j - 1, pl.ds(row0 + SUB * q, nrows), sl]


def _conv_taps(ext, cp, w_ref, first, nrows, sl, init, row0=0):
    acc = init
    for k in range(CONV_K):
        acc = acc + w_ref[k:k + 1, sl] * _shifted(ext, cp, first(k), nrows, sl, row0)
    return acc


def _conv_fwd(q, cw, cb, lg, lb):
    t = q.shape[0]

    def body(v_ref, g_ref, hv_ref, hg_ref, z_ref, w_ref, b_ref, lg_ref, lb_ref, y_ref, cv_ref, ext, cp):
        i = pl.program_id(0)
        ext[0:HALO, :] = jnp.where(i == 0, 0.0, hv_ref[...].astype(F32) * _sig(hg_ref[...].astype(F32)))
        ext[HALO:, :] = v_ref[...].astype(F32) * _sig(g_ref[...].astype(F32))
        _phase_copies(ext, cp)

        def lanes(c, carry):
            sl = pl.ds(pl.multiple_of(c * 128, 128), 128)
            cv_ref[:, sl] = _conv_taps(ext, cp, w_ref, lambda k: k + 2, TM, sl,
                                       jnp.broadcast_to(b_ref[:, sl], (TM, 128)))
            return carry

        lax.fori_loop(0, D_MODEL // 128, lanes, 0)
        cv = cv_ref[...]
        cc = cv - jnp.mean(cv, axis=-1, keepdims=True)
        rstd = lax.rsqrt(jnp.mean(cc * cc, axis=-1, keepdims=True) + LN_EPS)
        cl = cc * rstd * lg_ref[...] + lb_ref[...]
        z = z_ref[...].astype(F32)
        y_ref[...] = (cl * _sig(cl) * z * _sig(z)).astype(BF16)

    vec = _full((1, D_MODEL))
    return pl.pallas_call(
        body, name="conv_fwd", grid=(t // TM,),
        in_specs=[_rows(D_MODEL, 0), _rows(D_MODEL, 1), _prev(HALO, D_MODEL, 0), _prev(HALO, D_MODEL, 1),
                  _rows(D_MODEL, 2), _full((HALO, D_MODEL)), vec, vec, vec],
        out_specs=[_rows(D_MODEL), _rows(D_MODEL)],
        out_shape=[jax.ShapeDtypeStruct((t, D_MODEL), BF16), jax.ShapeDtypeStruct((t, D_MODEL), F32)],
        scratch_shapes=[pltpu.VMEM((TM + HALO, D_MODEL), F32),
                        pltpu.VMEM((SUB - 1, TM + HALO - SUB, D_MODEL), F32)],
        compiler_params=_params())(q, q, q, q, q, cw, cb, lg, lb)


def _odd_out_loss(y2, x1, w, gf, tgt):
    t = x1.shape[0]
    nstep = t // TM_MM
    streams = (y2, x1, tgt)

    def body(y_hbm, x_hbm, t_hbm, w_ref, g_ref, dx_ref, loss_ref, dg_ref, y_buf, x_buf, t_buf, sems):
        i = pl.program_id(0)
        slot = i % DEPTH

        def fetch(blk, sl):
            rows = pl.ds(pl.multiple_of(blk * TM_MM, TM_MM), TM_MM)
            return [pltpu.make_async_copy(src.at[rows, :], buf.at[sl], sems.at[k, sl])
                    for k, (src, buf) in enumerate(((y_hbm, y_buf), (x_hbm, x_buf), (t_hbm, t_buf)))]

        @pl.when(i == 0)
        def _():
            for b in range(min(DEPTH, nstep)):
                for cp in fetch(b, b):
                    cp.start()

        _zero_at_first(i, loss_ref, dg_ref)
        for cp in fetch(i, slot):
            cp.wait()
        x2 = x_buf[slot] + _dot(y_buf[slot], w_ref[...])
        r = lax.rsqrt(jnp.mean(x2 * x2, axis=-1, keepdims=True) + RMS_EPS)
        n = x2 * r
        e = n * g_ref[...] - t_buf[slot]
        loss_ref[...] += jnp.sum(e * e, axis=0, keepdims=True) * (0.5 / D_MODEL)
        dout = e * (1.0 / D_MODEL)
        dg_ref[...] += jnp.sum(dout * n, axis=0, keepdims=True)
        dn = dout * g_ref[...]
        dx_ref[...] = r * (dn - n * jnp.mean(dn * n, axis=-1, keepdims=True))

        @pl.when(i + DEPTH < nstep)
        def _():
            for cp in fetch(i + DEPTH, slot):
                cp.start()

    vec = _full((1, D_MODEL))
    return pl.pallas_call(
        body, name="odd_out_loss", grid=(nstep,),
        in_specs=[ANY, ANY, ANY, _full((D_MODEL, D_MODEL)), vec],
        out_specs=[_mm_rows(D_MODEL), vec, vec],
        out_shape=[jax.ShapeDtypeStruct((t, D_MODEL), F32), jax.ShapeDtypeStruct((1, D_MODEL), F32),
                   jax.ShapeDtypeStruct((1, D_MODEL), F32)],
        scratch_shapes=[pltpu.VMEM((DEPTH, TM_MM, D_MODEL), a.dtype) for a in streams]
        + [pltpu.SemaphoreType.DMA((len(streams), DEPTH))],
        compiler_params=_params())(y2, x1, tgt, w, gf)


def _dsilu(z):
    s = _sig(z)
    return z * s, s * (1.0 + z * (1.0 - s))


def _odd_bwd_out(dx2, w, y2, cv, q, lg, lb):
    t = dx2.shape[0]

    def body(dx_ref, w_ref, y_ref, cv_ref, z_ref, lg_ref, lb_ref, dcv_ref, dz_ref, dw_ref, dlg_ref, dlb_ref):
        i = pl.program_id(0)
        _zero_at_first(i, dw_ref, dlg_ref, dlb_ref)
        dxb = dx_ref[...].astype(BF16)
        dy = _dot_nt(dxb, w_ref[...])
        dw_ref[...] += _dot_tn(y_ref[...], dxb)
        cv = cv_ref[...]
        cc = cv - jnp.mean(cv, axis=-1, keepdims=True)
        rstd = lax.rsqrt(jnp.mean(cc * cc, axis=-1, keepdims=True) + LN_EPS)
        cn = cc * rstd
        silu_c, dsilu_c = _dsilu(cn * lg_ref[...] + lb_ref[...])
        silu_z, dsilu_z = _dsilu(z_ref[...].astype(F32))
        dcl = dy * silu_z * dsilu_c
        dz_ref[...] = (dy * silu_c * dsilu_z).astype(BF16)
        dlg_ref[...] += jnp.sum(dcl * cn, axis=0, keepdims=True)
        dlb_ref[...] += jnp.sum(dcl, axis=0, keepdims=True)
        dcn = dcl * lg_ref[...]
        dcv_ref[...] = rstd * (dcn - jnp.mean(dcn, axis=-1, keepdims=True)
                               - cn * jnp.mean(dcn * cn, axis=-1, keepdims=True))

    vec = _full((1, D_MODEL))
    mat = _full((D_MODEL, D_MODEL))
    return pl.pallas_call(
        body, name="odd_bwd_out", grid=(t // TM_MM,),
        in_specs=[_mm_rows(D_MODEL), mat, _mm_rows(D_MODEL), _mm_rows(D_MODEL), _mm_rows(D_MODEL, 2), vec, vec],
        out_specs=[_mm_rows(D_MODEL), _mm_rows(D_MODEL), mat, vec, vec],
        out_shape=[jax.ShapeDtypeStruct((t, D_MODEL), F32), jax.ShapeDtypeStruct((t, D_MODEL), BF16),
                   jax.ShapeDtypeStruct((D_MODEL, D_MODEL), F32), jax.ShapeDtypeStruct((1, D_MODEL), F32),
                   jax.ShapeDtypeStruct((1, D_MODEL), F32)],
        compiler_params=_params())(dx2, w, y2, cv, q, lg, lb)


def _conv_bwd(dcv, q, cw):
    t = dcv.shape[0]
    nblk = t // TM

    def body(d_ref, dn_ref, v_ref, g_ref, hv_ref, hg_ref, w_ref,
             dv_ref, dgt_ref, dw_ref, db_ref, gext, dext, dgl, gcp, dcp):
        i = pl.program_id(0)
        last = nblk - 1
        _zero_at_first(i, dw_ref, db_ref)
        v = v_ref[...].astype(F32)
        sg = _sig(g_ref[...].astype(F32))
        gext[0:HALO, :] = jnp.where(i == 0, 0.0, hv_ref[...].astype(F32) * _sig(hg_ref[...].astype(F32)))
        gext[HALO:, :] = v * sg
        d = d_ref[...]
        dext[0:TM, :] = d
        dext[TM:, :] = jnp.where(i == last, 0.0, dn_ref[...])
        _phase_copies(gext, gcp)
        _phase_copies(dext, dcp)
        db_ref[...] += jnp.sum(d, axis=0, keepdims=True)
        def lanes(c, carry):
            sl = pl.ds(pl.multiple_of(c * 128, 128), 128)
            dgl[:, sl] = _conv_taps(dext, dcp, w_ref, lambda k: 30 - k, TM, sl, jnp.zeros((TM, 128), F32))
            return carry

        def lanes_w(c, carry):
            sl = pl.ds(pl.multiple_of(c * 128, 128), 128)
            ntile = TM // SUB
            dts = [d_ref[SUB * r:SUB * (r + 1), sl] for r in range(ntile)]
            for j in range(SUB):
                taps = [(q, SUB * q + j - 2) for q in range(5) if 0 <= SUB * q + j - 2 < CONV_K]
                sums = {k: None for _, k in taps}
                for rt in range(ntile + 4):
                    need = [(q, k) for q, k in taps if 0 <= rt - q < ntile]
                    if not need:
                        continue
                    src = gext[SUB * rt:SUB * (rt + 1), sl] if j == 0 else gcp[j - 1, SUB * rt:SUB * (rt + 1), sl]
                    for q, k in need:
                        prod = dts[rt - q] * src
                        sums[k] = prod if sums[k] is None else sums[k] + prod
                for _, k in taps:
                    dw_ref[SUB * k:SUB * (k + 1), sl] += sums[k]
            return carry

        lax.fori_loop(0, D_MODEL // 128, lanes, 0)
        lax.fori_loop(0, D_MODEL // 128, lanes_w, 0)
        dg = dgl[...]
        dv_ref[...] = (dg * sg).astype(BF16)
        dgt_ref[...] = (dg * v * sg * (1.0 - sg)).astype(BF16)

    return pl.pallas_call(
        body, name="conv_bwd", grid=(t // TM,),
        in_specs=[_rows(D_MODEL), _next(HALO, D_MODEL, t), _rows(D_MODEL, 0), _rows(D_MODEL, 1),
                  _prev(HALO, D_MODEL, 0), _prev(HALO, D_MODEL, 1), _full((HALO, D_MODEL))],
        out_specs=[_rows(D_MODEL), _rows(D_MODEL), _full((HALO * SUB, D_MODEL)), _full((1, D_MODEL))],
        out_shape=[jax.ShapeDtypeStruct((t, D_MODEL), BF16), jax.ShapeDtypeStruct((t, D_MODEL), BF16),
                   jax.ShapeDtypeStruct((HALO * SUB, D_MODEL), F32), jax.ShapeDtypeStruct((1, D_MODEL), F32)],
        scratch_shapes=[pltpu.VMEM((TM + HALO, D_MODEL), F32), pltpu.VMEM((TM + HALO, D_MODEL), F32),
                        pltpu.VMEM((TM, D_MODEL), F32),
                        pltpu.VMEM((SUB - 1, TM + HALO - SUB, D_MODEL), F32),
                        pltpu.VMEM((SUB - 1, TM + HALO - SUB, D_MODEL), F32)],
        compiler_params=_params())(dcv, dcv, q, q, q, q, cw)


def _column_segments(widths, ns):
    segs = []
    col = 0
    for p, wd in enumerate(widths):
        a = 0
        while a < wd:
            s, lo = divmod(col + a, ns)
            ln = min(wd - a, ns - lo)
            segs.append((p, a, a + ln, s, lo, lo + ln))
            a += ln
        col += wd
    return segs


def _in_bwd(dparts, w, x, g, dres, name, ride=None):
    t = x.shape[0]
    widths = [p.shape[1] for p in dparts]
    npart = len(dparts)
    segs = _column_segments(widths, w.shape[2])
    nstep = t // TM_MM

    def body(*refs):
        d_refs = refs[:npart]
        w_ref, x_ref, g_ref, r_ref = refs[npart:npart + 4]
        if ride is None:
            dx_ref, dg_ref, dw_ref = refs[npart + 4:]
        else:
            v_ref, j_ref, dx_ref, dg_ref, dw_ref, o_ref, jo_ref = refs[npart + 4:npart + 11]
            start, forward, finish = _gather_all_phases(v_ref, o_ref, *refs[npart + 11:npart + 15])
            j_start, j_finish = _pair_join_phases(j_ref, jo_ref, *refs[npart + 15:])
        i = pl.program_id(0)
        if ride is not None:
            pl.when(i == 0)(start)
            pl.when(i == 0)(j_start)
            pl.when(i == nstep // 2)(forward)
        _zero_at_first(i, dg_ref, dw_ref)
        xv = x_ref[...]
        r = lax.rsqrt(jnp.mean(xv * xv, axis=-1, keepdims=True) + RMS_EPS)
        n = xv * r
        h = (n * g_ref[...]).astype(BF16)
        dh = None
        for p, lo, hi, s, slo, shi in segs:
            d = d_refs[p][:, lo:hi]
            part = _dot_nt(d, w_ref[s, :, slo:shi])
            dh = part if dh is None else dh + part
            dw_ref[s, :, slo:shi] += _dot_tn(h, d)
        dg_ref[...] += jnp.sum(dh * n, axis=0, keepdims=True)
        dn = dh * g_ref[...]
        dx_ref[...] = r_ref[...] + r * (dn - n * jnp.mean(dn * n, axis=-1, keepdims=True))
        if ride is not None:
            pl.when(i == nstep - 1)(finish)
            pl.when(i == nstep - 1)(j_finish)

    vec = _full((1, D_MODEL))
    once = pl.BlockSpec(w.shape, lambda i: (0, 0, 0), pipeline_mode=pl.Buffered(1))
    extra = [] if ride is None else list(ride)
    return pl.pallas_call(
        body, name=name, grid=(nstep,),
        in_specs=[_mm_rows(wd) for wd in widths] + [once, _mm_rows(D_MODEL), vec, _mm_rows(D_MODEL)]
        + [ANY] * len(extra),
        out_specs=[_mm_rows(D_MODEL), vec, once] + [ANY] * len(extra),
        out_shape=[jax.ShapeDtypeStruct((t, D_MODEL), F32), jax.ShapeDtypeStruct((1, D_MODEL), F32),
                   jax.ShapeDtypeStruct(w.shape, F32)]
        + ([jax.ShapeDtypeStruct((8,) + ride[0].shape, ride[0].dtype),
            jax.ShapeDtypeStruct(ride[1].shape, ride[1].dtype)] if extra else []),
        scratch_shapes=(_gather_all_scratch(ride[0]) + _pair_join_scratch()) if extra else [],
        compiler_params=_params())(*dparts, w, x, g, dres, *extra)


def _even_bwd_out(dx1, w, yg, yp, ys, proj, wp, ps, ride):
    t = dx1.shape[0]
    nstep = t // TM_MM
    rows = TM_MM

    def body(dx_ref, w_ref, yg_ref, yp_ref, ys_ref, z_ref, u_ref, h_ref, wp_ref, ps_ref, g_ref,
             dy_ref, dz_ref, dw_ref, du_ref, dwp_ref, dps_ref, theirs_ref, nxt, send_sems, recv_sems):
        i = pl.program_id(0)
        blk = nstep - 1 - i
        start, finish = _pair_split_phases(g_ref, theirs_ref, send_sems, recv_sems)
        pl.when(i == 0)(start)
        _zero_at_first(i, dw_ref, dwp_ref, dps_ref, nxt)
        dxb = dx_ref[...].astype(BF16)
        dyg = _dot_nt(dxb, w_ref[...])
        dw_ref[...] += _dot_tn(yg_ref[...], dxb)
        silu_z, dsilu_z = _dsilu(z_ref[...].astype(F32))
        dyb = (dyg * silu_z).astype(BF16)
        dy_ref[...] = dyb
        dz_ref[:, :POOL_W] = (dyg[:, :POOL_W] * yp_ref[...].astype(F32) * dsilu_z[:, :POOL_W]).astype(BF16)
        dz_ref[:, POOL_W:] = (dyg[:, POOL_W:] * ys_ref[...].astype(F32) * dsilu_z[:, POOL_W:]).astype(BF16)
        pos = (blk * rows + 1 + lax.broadcasted_iota(jnp.int32, (rows, 1), 0)).astype(F32)
        pos_ext = (blk * rows + 1 + lax.broadcasted_iota(jnp.int32, (rows + POOL_HALO, 1), 0)).astype(F32)
        for gi in range(4):
            sl = slice(128 * gi, 128 * (gi + 1))
            wd = float(2 << gi)
            u = u_ref[:, sl].astype(F32)
            halo = jnp.where(blk == 0, 0.0, h_ref[:, sl].astype(F32))
            s = _pool_sums(jnp.concatenate([halo, u], axis=0), gi, True)[POOL_HALO:, :]
            pooled = (s / jnp.minimum(pos, wd) - u).astype(BF16)
            dy = dyb[:, sl].astype(F32)
            dps_ref[:, sl] += jnp.sum(dy * _dot(pooled, wp_ref[gi]), axis=0, keepdims=True)
            dmix = (jnp.concatenate([dy, nxt[:, sl]], axis=0) * ps_ref[:, sl]).astype(BF16)
            dwp_ref[gi] += _dot_tn(pooled, dmix[:rows, :])
            dpool = _dot_nt(dmix, wp_ref[gi])
            lead = _pool_sums(dpool / jnp.minimum(pos_ext, wd), gi, False)
            du_ref[:, sl] = (lead[:rows, :] - dpool[:rows, :]).astype(BF16)
            nxt[:, sl] = dy[:POOL_HALO, :]
        pl.when(i == nstep - 1)(finish)

    back = lambda wdt, cb=0: _rows(wdt, cb, rev=nstep, tm=rows)
    per = rows // POOL_HALO
    halo = pl.BlockSpec((POOL_HALO, POOL_W), lambda i: (jnp.maximum((nstep - 1 - i) * per - 1, 0), 0))
    mat = _full((D_MODEL, D_MODEL))
    return pl.pallas_call(
        body, name="even_bwd_out", grid=(nstep,),
        in_specs=[back(D_MODEL), mat, back(D_MODEL), back(POOL_W), back(SSM_W), back(D_MODEL, 1),
                  back(POOL_W, 0), halo, _full((4, 128, 128)), _full((1, POOL_W)), ANY],
        out_specs=[back(D_MODEL), back(D_MODEL), mat, back(POOL_W), _full((4, 128, 128)), _full((1, POOL_W)), ANY],
        out_shape=[jax.ShapeDtypeStruct((t, D_MODEL), BF16), jax.ShapeDtypeStruct((t, D_MODEL), BF16),
                   jax.ShapeDtypeStruct((D_MODEL, D_MODEL), F32), jax.ShapeDtypeStruct((t, POOL_W), BF16),
                   jax.ShapeDtypeStruct((4, 128, 128), F32), jax.ShapeDtypeStruct((1, POOL_W), F32),
                   _pair_split_shape(ride)],
        scratch_shapes=[pltpu.VMEM((POOL_HALO, POOL_W), F32)] + _pair_split_scratch(ride),
        compiler_params=_params())(dx1, w, yg, yp, ys, proj, proj, proj, wp, ps, ride)


def _ssm_bwd(dycat, proj, car_in_re, car_in_im, pm, pmt, mb_re, mb_im, p8_re, p8_im, q_re, q_im, qr_re, qr_im,
             cm_re, cm_im, dskip, wglu, ride):
    t = proj.shape[0]
    nblk = t // TM

    def body(dy_ref, u_ref, cin_re, cin_im, pm_ref, pmt_ref, mbre, mbim, p8re, p8im, qre, qim, qrre, qrim,
             cmre, cmim, d_ref, wg_ref, p_ref,
             du_ref, dmbre, dmbim, dcmre, dcmim, dare, daim, dd_ref, dwg_ref, got_ref,
             xs_re, xs_im, gs_re, gs_im, car_re, car_im, ent_re, ent_im, gcar_re, gcar_im, ysk, dysk,
             bounce, send_sems, recv_sems, local_sems):
        i = pl.program_id(0)
        start, finish = _chip_scatter_phases(p_ref, got_ref, bounce, send_sems, recv_sems, local_sems)
        pl.when(i == 0)(start)
        _zero_at_first(i, dmbre, dmbim, dcmre, dcmim, dare, daim, dd_ref, dwg_ref, gcar_re, gcar_im)
        us = _dot(pm_ref[...], u_ref[...])
        usb = us.astype(BF16)
        for j in range(4):
            xs_re[:, LCH * j:LCH * (j + 1)] = _dot(usb[:, 128 * j:128 * (j + 1)], mbre[j])
            xs_im[:, LCH * j:LCH * (j + 1)] = _dot(usb[:, 128 * j:128 * (j + 1)], mbim[j])
        car_re[...] = cin_re[0]
        car_im[...] = cin_im[0]
        _scan_fwd_block(xs_re, xs_im, p8re, p8im, qre, qim, car_re, car_im, ent_re, ent_im)
        for j in range(4):
            sl = slice(LCH * j, LCH * (j + 1))
            ysk[:, 128 * j:128 * (j + 1)] = (_dot_nt(xs_re[:, sl].astype(BF16), cmre[j])
                                             - _dot_nt(xs_im[:, sl].astype(BF16), cmim[j]))
        yvb = (ysk[...] + d_ref[...] * us).astype(BF16)
        gv = _dot(yvb, wg_ref[...])
        sg = _sig(gv[:, SSM_W:])
        dyss = _dot(pm_ref[...], dy_ref[...])
        dval = (dyss * sg).astype(BF16)
        dgate = (dyss * gv[:, :SSM_W] * sg * (1.0 - sg)).astype(BF16)
        dy = _dot_nt(dval, wg_ref[:, :SSM_W]) + _dot_nt(dgate, wg_ref[:, SSM_W:])
        dwg_ref[:, :SSM_W] += _dot_tn(yvb, dval)
        dwg_ref[:, SSM_W:] += _dot_tn(yvb, dgate)
        dd_ref[...] += jnp.sum(dy * us, axis=0, keepdims=True)
        dysk[...] = dy
        for j in range(4):
            sl = slice(LCH * j, LCH * (j + 1))
            dyj = dy[:, 128 * j:128 * (j + 1)].astype(BF16)
            gs_re[:, sl] = _dot(dyj, cmre[j])
            gs_im[:, sl] = -_dot(dyj, cmim[j])
            dcmre[j] += _dot_tn(dyj, xs_re[:, sl].astype(BF16))
            dcmim[j] -= _dot_tn(dyj, xs_im[:, sl].astype(BF16))
        row = lax.broadcasted_iota(jnp.int32, (SUB, SCAN_L), 0)
        for j in range(STATES // SCAN_L):
            sl = slice(SCAN_L * j, SCAN_L * (j + 1))
            are, aim = p8re[0:SUB, sl], -p8im[0:SUB, sl]

            def totals(k, v, sl=sl, are=are, aim=aim):
                r0 = pl.multiple_of((SEG_LEN - 2 - k) * SUB, SUB)
                vre, vim = _cmul_add(are, aim, v[0], v[1], gs_re[pl.ds(r0, SUB), sl], gs_im[pl.ds(r0, SUB), sl])
                gs_re[pl.ds(r0, SUB), sl] = vre
                gs_im[pl.ds(r0, SUB), sl] = vim
                return vre, vim

            top = (SEG_LEN - 1) * SUB
            fre, fim = lax.fori_loop(0, SEG_LEN - 1, totals,
                                     (gs_re[top:top + SUB, sl], gs_im[top:top + SUB, sl]), unroll=2)
            fre, fim, nre, nim = _segment_chain(fre, fim, qrre[:, sl], -qrim[:, sl],
                                                gcar_re[:, sl], gcar_im[:, sl], row, True)
            gcar_re[:, sl] = jnp.broadcast_to(fre[0:1, :], (SUB, SCAN_L))
            gcar_im[:, sl] = jnp.broadcast_to(fim[0:1, :], (SUB, SCAN_L))

            def fix(i2, acc, sl=sl, nre=nre, nim=nim):
                r0 = pl.multiple_of(i2 * SUB, SUB)
                rb = pl.multiple_of((SEG_LEN - 1 - i2) * SUB, SUB)
                gre, gim = _cmul_add(p8re[pl.ds(rb, SUB), sl], -p8im[pl.ds(rb, SUB), sl], nre, nim,
                                     gs_re[pl.ds(r0, SUB), sl], gs_im[pl.ds(r0, SUB), sl])
                gs_re[pl.ds(r0, SUB), sl] = gre
                gs_im[pl.ds(r0, SUB), sl] = gim
                rp = pl.multiple_of((i2 - 1) * SUB, SUB)
                xre, xim = xs_re[pl.ds(rp, SUB), sl], xs_im[pl.ds(rp, SUB), sl]
                return acc[0] + gre * xre + gim * xim, acc[1] + gim * xre - gre * xim

            g0re, g0im = _cmul_add(p8re[top:top + SUB, sl], -p8im[top:top + SUB, sl], nre, nim,
                                   gs_re[0:SUB, sl], gs_im[0:SUB, sl])
            gs_re[0:SUB, sl] = g0re
            gs_im[0:SUB, sl] = g0im
            ere, eim = ent_re[:, sl], ent_im[:, sl]
            acc0 = (dare[:, sl] + g0re * ere + g0im * eim, daim[:, sl] + g0im * ere - g0re * eim)
            are_acc, aim_acc = lax.fori_loop(1, SEG_LEN, fix, acc0, unroll=2)
            dare[:, sl] = are_acc
            daim[:, sl] = aim_acc
        for j in range(4):
            sl = slice(LCH * j, LCH * (j + 1))
            c4 = slice(128 * j, 128 * (j + 1))
            gre = gs_re[:, sl].astype(BF16)
            gim = gs_im[:, sl].astype(BF16)
            dmbre[j] += _dot_tn(usb[:, c4], gre)
            dmbim[j] += _dot_tn(usb[:, c4], gim)
            dysk[:, c4] = _dot_nt(gre, mbre[j]) + _dot_nt(gim, mbim[j]) + dysk[:, c4] * d_ref[:, c4]
        du_ref[...] = _dot(pmt_ref[...], dysk[...].astype(BF16)).astype(BF16)
        pl.when(i == nblk - 1)(finish)

    blk = (4, 128, LCH)
    pw = _full((SUB, STATES))
    p8 = _full((TM, STATES))
    perm = _full((TM, TM))
    car = pl.BlockSpec((1, SUB, STATES), lambda i: (nblk - 1 - i, 0, 0))
    big = lambda: pltpu.VMEM((TM, STATES), F32)
    small = lambda: pltpu.VMEM((SUB, STATES), F32)
    return pl.pallas_call(
        body, name="ssm_bwd", grid=(nblk,),
        in_specs=[_rows(SSM_W, 1, rev=nblk), _rows(SSM_W, 1, rev=nblk), car, car, perm, perm, _full(blk), _full(blk),
                  p8, p8, pw, pw, pw, pw, _full(blk), _full(blk), _full((1, SSM_W)), _full((SSM_W, 2 * SSM_W)), ANY],
        out_specs=[_rows(SSM_W, 0, rev=nblk), _full(blk), _full(blk), _full(blk), _full(blk), pw, pw,
                   _full((1, SSM_W)), _full((SSM_W, 2 * SSM_W)), ANY],
        out_shape=[jax.ShapeDtypeStruct((t, SSM_W), BF16)] + [jax.ShapeDtypeStruct(blk, F32)] * 4
        + [jax.ShapeDtypeStruct((SUB, STATES), F32)] * 2
        + [jax.ShapeDtypeStruct((1, SSM_W), F32), jax.ShapeDtypeStruct((SSM_W, 2 * SSM_W), F32),
           jax.ShapeDtypeStruct(ride.shape, ride.dtype)],
        scratch_shapes=[big(), big(), big(), big(), small(), small(), small(), small(), small(), small(),
                        pltpu.VMEM((TM, SSM_W), F32), pltpu.VMEM((TM, SSM_W), F32)] + _chip_scatter_scratch(ride),
        compiler_params=_params())(dycat, proj, car_in_re, car_in_im, pm, pmt, mb_re, mb_im, p8_re, p8_im,
                                   q_re, q_im, qr_re, qr_im, cm_re, cm_im, dskip, wglu, ride)


def _adamw(w, g, m, v, name):
    rows = w.shape[0]
    tr = 256 if rows % 256 == 0 else rows
    c1 = 1.0 / (1.0 - ADAM_B1 ** ADAM_STEP)
    c2 = 1.0 / (1.0 - ADAM_B2 ** ADAM_STEP)

    def body(w_ref, g_ref, m_ref, v_ref, d_ref, nm_ref, nv_ref):
        gv = g_ref[...]
        m = ADAM_B1 * m_ref[...] + (1.0 - ADAM_B1) * gv
        v = ADAM_B2 * v_ref[...] + (1.0 - ADAM_B2) * (gv * gv)
        nm_ref[...] = m
        nv_ref[...] = v
        d_ref[...] = -ADAM_LR * ((m * c1) / (jnp.sqrt(v * c2) + ADAM_EPS) + ADAM_WD * w_ref[...])

    spec = pl.BlockSpec((tr, D_MODEL), lambda i: (i, 0))
    shp = jax.ShapeDtypeStruct((rows, D_MODEL), F32)
    return pl.pallas_call(
        body, name=name, grid=(rows // tr,), in_specs=[spec] * 4, out_specs=[spec] * 3, out_shape=[shp] * 3,
        compiler_params=_params())(w, g, m, v)


def _core_index():
    return lax.axis_index("c").astype(jnp.int32).reshape(1)


def _pair_add(g, theirs, out_dtype, name):
    n, half, _ = theirs.shape
    br = 128
    nb = half // br

    def body(c_ref, a_ref, b_ref, o_ref):
        o_ref[...] = (a_ref[...] + b_ref[...]).astype(out_dtype)

    spec = pl.BlockSpec((1, br, D_MODEL), lambda i, j, c: (i, j, 0))
    grid_spec = pltpu.PrefetchScalarGridSpec(
        num_scalar_prefetch=1, grid=(n, nb),
        in_specs=[pl.BlockSpec((1, br, D_MODEL), lambda i, j, c: (i, c[0] * nb + j, 0)), spec], out_specs=spec)
    return pl.pallas_call(
        body, name=name, grid_spec=grid_spec, out_shape=jax.ShapeDtypeStruct(theirs.shape, out_dtype),
        compiler_params=_params(2))(_core_index(), g, theirs)


def _adamw_rows(w, m, v, g_mine, g_theirs, row0, br, name):
    rows = w.shape[0]
    b0 = row0 // br
    per_half = g_mine.shape[0] // br
    c1 = 1.0 / (1.0 - ADAM_B1 ** ADAM_STEP)
    c2 = 1.0 / (1.0 - ADAM_B2 ** ADAM_STEP)

    def body(c_ref, w_ref, gm_ref, gt_ref, m_ref, v_ref, g_ref, d_ref, nm_ref, nv_ref):
        gv = jnp.where((b0 + pl.program_id(0)) // per_half == c_ref[0], gm_ref[...], gt_ref[...])
        m = ADAM_B1 * m_ref[...] + (1.0 - ADAM_B1) * gv
        v = ADAM_B2 * v_ref[...] + (1.0 - ADAM_B2) * (gv * gv)
        g_ref[...] = gv
        nm_ref[...] = m
        nv_ref[...] = v
        d_ref[...] = -ADAM_LR * ((m * c1) / (jnp.sqrt(v * c2) + ADAM_EPS) + ADAM_WD * w_ref[...])

    spec = pl.BlockSpec((br, D_MODEL), lambda i, c: (i, 0))
    part = pl.BlockSpec((br, D_MODEL), lambda i, c: ((b0 + i) % per_half, 0))
    shp = jax.ShapeDtypeStruct((rows, D_MODEL), F32)
    grid_spec = pltpu.PrefetchScalarGridSpec(
        num_scalar_prefetch=1, grid=(rows // br,), in_specs=[spec, part, part, spec, spec], out_specs=[spec] * 4)
    return pl.pallas_call(
        body, name=name, grid_spec=grid_spec, out_shape=[shp] * 4,
        compiler_params=_params())(_core_index(), w, g_mine, g_theirs, m, v)


def _sum_lead(a, name):
    n, rows, _ = a.shape
    tr = 128 if rows % 128 == 0 else rows

    def body(a_ref, o_ref):
        acc = a_ref[0].astype(F32)
        for k in range(1, n):
            acc = acc + a_ref[k].astype(F32)
        o_ref[...] = acc

    return pl.pallas_call(
        body, name=name, grid=(rows // tr,),
        in_specs=[pl.BlockSpec((n, tr, D_MODEL), lambda i: (0, i, 0))],
        out_specs=pl.BlockSpec((tr, D_MODEL), lambda i: (i, 0)),
        out_shape=jax.ShapeDtypeStruct((rows, D_MODEL), F32), compiler_params=_params())(a)


ANY = pl.BlockSpec(memory_space=pl.ANY)


def _mesh_pos():
    return lax.axis_index("x"), lax.axis_index("y"), lax.axis_index("c")


def _gather_phases(in_refs, out_refs, bounces, send_sems, recv_sems, local_sems):
    na = len(in_refs)
    halves = [r.shape[0] // 2 for r in in_refs]
    ncopy = 3 * na
    x, y, c = _mesh_pos()
    me = 2 * x + y
    sibling = (x, y, 1 - c)
    chips = [(1 - x, y), (x, 1 - y), (1 - x, 1 - y)]
    ids = [2 * chip[0] + chip[1] for chip in chips]

    def piece(a, q, h):
        return out_refs[a].at[q, pl.ds(h * halves[a], halves[a]), :]

    def copy(s, a, q, h, to, src=None):
        return pltpu.make_async_remote_copy(
            src_ref=piece(a, q, h) if src is None else src, dst_ref=piece(a, q, h),
            send_sem=send_sems.at[s], recv_sem=recv_sems.at[s], device_id=to, device_id_type=MESH)

    def sends():
        return [copy(j * na + a, a, me, c, (*chip, c), src=in_refs[a].at[pl.ds(c * halves[a], halves[a]), :])
                for j, chip in enumerate(chips) for a in range(na)]

    def forwards():
        return [copy(ncopy + j * na + a, a, ids[j], c, sibling) for j in range(3) for a in range(na)]

    def stores():
        return [pltpu.make_async_copy(bounces[a], out_refs[a].at[me], local_sems.at[na + a]) for a in range(na)]

    def start():
        loads = [pltpu.make_async_copy(in_refs[a], bounces[a], local_sems.at[a]) for a in range(na)]
        for cp in loads:
            cp.start()
        for cp in sends():
            cp.start()
        for ld, st in zip(loads, stores()):
            ld.wait()
            st.start()

    def forward():
        fwd = forwards()
        for j in range(3):
            for a in range(na):
                copy(j * na + a, a, ids[j], c, (x, y, c)).wait_recv()
                fwd[j * na + a].start()

    def finish():
        for j in range(3):
            for a in range(na):
                copy(ncopy + j * na + a, a, ids[j], 1 - c, (x, y, c)).wait_recv()
        for cp in sends() + forwards():
            cp.wait_send()
        for cp in stores():
            cp.wait()

    return start, forward, finish


def _gather_scratch(arrs):
    ncopy = 3 * len(arrs)
    return ([pltpu.VMEM(a.shape, a.dtype) for a in arrs]
            + [pltpu.SemaphoreType.DMA((2 * ncopy,)), pltpu.SemaphoreType.DMA((2 * ncopy,)),
               pltpu.SemaphoreType.DMA((2 * len(arrs),))])


def _gather_weights(arrs):
    na = len(arrs)

    def body(*refs):
        start, forward, finish = _gather_phases(refs[:na], refs[na:2 * na], refs[2 * na:3 * na], *refs[3 * na:])
        start()
        forward()
        finish()

    return pl.pallas_call(
        body, name="gather_weights", in_specs=[ANY] * na, out_specs=[ANY] * na,
        out_shape=[jax.ShapeDtypeStruct((4,) + a.shape, a.dtype) for a in arrs],
        scratch_shapes=_gather_scratch(arrs),
        compiler_params=pltpu.CompilerParams(vmem_limit_bytes=VMEM_LIMIT),
    )(*arrs)


def _gather_all_phases(v_ref, o_ref, bounce, send_sems, recv_sems, local_sems):
    x, y, c = _mesh_pos()
    sibling = (x, y, 1 - c)
    chips = [(1 - x, y), (x, 1 - y), (1 - x, 1 - y)]

    def blk(px, py, pc):
        return o_ref.at[4 * px + 2 * py + pc]

    def copy(k, block, to, src=None):
        return pltpu.make_async_remote_copy(
            src_ref=blk(*block) if src is None else src, dst_ref=blk(*block),
            send_sem=send_sems.at[k], recv_sem=recv_sems.at[k], device_id=to, device_id_type=MESH)

    def first():
        return ([copy(0, (x, y, c), sibling, src=v_ref)]
                + [copy(1 + j, (x, y, c), (*chip, c), src=v_ref) for j, chip in enumerate(chips)])

    def passed():
        return [copy(4 + j, (*chip, c), sibling) for j, chip in enumerate(chips)]

    def store():
        return pltpu.make_async_copy(bounce, blk(x, y, c), local_sems.at[1])

    def start():
        load = pltpu.make_async_copy(v_ref, bounce, local_sems.at[0])
        load.start()
        for cp in first():
            cp.start()
        load.wait()
        store().start()

    def forward():
        fwd = passed()
        for j, chip in enumerate(chips):
            copy(1 + j, (*chip, c), (x, y, c)).wait_recv()
            fwd[j].start()

    def finish():
        copy(0, (x, y, 1 - c), (x, y, c)).wait_recv()
        for j, chip in enumerate(chips):
            copy(4 + j, (*chip, 1 - c), (x, y, c)).wait_recv()
        for cp in first() + passed():
            cp.wait_send()
        store().wait()

    return start, forward, finish


def _gather_all_scratch(v):
    return [pltpu.VMEM(v.shape, v.dtype), pltpu.SemaphoreType.DMA((7,)), pltpu.SemaphoreType.DMA((7,)),
            pltpu.SemaphoreType.DMA((2,))]


def _scatter_and_gather(p, v, name):
    def body(p_ref, v_ref, got_ref, o_ref, p_bounce, p_send, p_recv, p_local, bounce, send_sems, recv_sems,
             local_sems):
        start, finish = _chip_scatter_phases(p_ref, got_ref, p_bounce, p_send, p_recv, p_local)
        g_start, g_forward, g_finish = _gather_all_phases(v_ref, o_ref, bounce, send_sems, recv_sems, local_sems)
        start()
        g_start()
        g_forward()
        g_finish()
        finish()

    return pl.pallas_call(
        body, name=name, in_specs=[ANY, ANY], out_specs=[ANY, ANY],
        out_shape=[jax.ShapeDtypeStruct(p.shape, p.dtype), jax.ShapeDtypeStruct((8,) + v.shape, v.dtype)],
        scratch_shapes=_chip_scatter_scratch(p) + _gather_all_scratch(v),
    )(p, v)


def _pair_split_phases(g_ref, theirs_ref, send_sems, recv_sems):
    n, rows, _ = g_ref.shape
    half = rows // 2
    ch = half // COMM_CHUNKS
    x, y, c = _mesh_pos()

    def gives():
        return [pltpu.make_async_remote_copy(
            src_ref=g_ref.at[q, pl.ds((1 - c) * half + k * ch, ch), :],
            dst_ref=theirs_ref.at[q, pl.ds(k * ch, ch), :],
            send_sem=send_sems.at[q * COMM_CHUNKS + k], recv_sem=recv_sems.at[q * COMM_CHUNKS + k],
            device_id=(x, y, 1 - c), device_id_type=MESH) for q in range(n) for k in range(COMM_CHUNKS)]

    def start():
        for cp in gives():
            cp.start()

    def finish():
        for cp in gives():
            cp.wait()

    return start, finish


def _pair_split_scratch(g):
    return [pltpu.SemaphoreType.DMA((g.shape[0] * COMM_CHUNKS,)), pltpu.SemaphoreType.DMA((g.shape[0] * COMM_CHUNKS,))]


def _pair_split_shape(g):
    return jax.ShapeDtypeStruct((g.shape[0], g.shape[1] // 2, D_MODEL), g.dtype)


def _pair_split(g, name):
    def body(g_ref, theirs_ref, send_sems, recv_sems):
        start, finish = _pair_split_phases(g_ref, theirs_ref, send_sems, recv_sems)
        start()
        finish()

    return pl.pallas_call(
        body, name=name, in_specs=[ANY], out_specs=ANY, out_shape=_pair_split_shape(g),
        scratch_shapes=_pair_split_scratch(g))(g)


def _chip_scatter_phases(p_ref, o_ref, bounce, send_sems, recv_sems, local_sems):
    x, y, c = _mesh_pos()
    me = 2 * x + y
    chips = [(1 - x, y), (x, 1 - y), (1 - x, 1 - y)]

    def keep():
        return pltpu.make_async_copy(bounce, o_ref.at[me], local_sems.at[1])

    def sends():
        return [pltpu.make_async_remote_copy(
            src_ref=p_ref.at[2 * chip[0] + chip[1]], dst_ref=o_ref.at[me],
            send_sem=send_sems.at[j], recv_sem=recv_sems.at[j], device_id=(*chip, c), device_id_type=MESH)
            for j, chip in enumerate(chips)]

    def start():
        load = pltpu.make_async_copy(p_ref.at[me], bounce, local_sems.at[0])
        load.start()
        for cp in sends():
            cp.start()
        load.wait()
        keep().start()

    def finish():
        for j, chip in enumerate(chips):
            q = 2 * chip[0] + chip[1]
            pltpu.make_async_remote_copy(
                src_ref=p_ref.at[q], dst_ref=o_ref.at[q], send_sem=send_sems.at[j], recv_sem=recv_sems.at[j],
                device_id=(*chip, c), device_id_type=MESH).wait_recv()
        for cp in sends():
            cp.wait_send()
        keep().wait()

    return start, finish


def _chip_scatter_scratch(p):
    return [pltpu.VMEM(p.shape[1:], p.dtype), pltpu.SemaphoreType.DMA((3,)), pltpu.SemaphoreType.DMA((3,)),
            pltpu.SemaphoreType.DMA((2,))]


def _pair_join_phases(r_ref, o_ref, send_sems, recv_sems):
    ch = r_ref.shape[0] // COMM_CHUNKS
    x, y, c = _mesh_pos()

    def gives():
        return [pltpu.make_async_remote_copy(
            src_ref=r_ref.at[pl.ds(k * ch, ch), :], dst_ref=o_ref.at[pl.ds(k * ch, ch), :],
            send_sem=send_sems.at[k], recv_sem=recv_sems.at[k], device_id=(x, y, 1 - c), device_id_type=MESH)
            for k in range(COMM_CHUNKS)]

    def start():
        for cp in gives():
            cp.start()

    def finish():
        for cp in gives():
            cp.wait()

    return start, finish


def _pair_join_scratch():
    return [pltpu.SemaphoreType.DMA((COMM_CHUNKS,)), pltpu.SemaphoreType.DMA((COMM_CHUNKS,))]


def _pair_join(r, name):
    def body(r_ref, o_ref, send_sems, recv_sems):
        start, finish = _pair_join_phases(r_ref, o_ref, send_sems, recv_sems)
        start()
        finish()

    return pl.pallas_call(
        body, name=name, in_specs=[ANY], out_specs=ANY, out_shape=jax.ShapeDtypeStruct(r.shape, r.dtype),
        scratch_shapes=_pair_join_scratch())(r)


SHARD_BIG = (("even_w_in", (1024, 512)), ("ssm_w_glu", (512, 256)), ("even_w_out", (256, 1024)),
             ("odd_w_in", (1024, 768)), ("odd_w_out", (256, 1024)))
SHARD_SMALL = (("odd_norm", 1), ("conv_w", CONV_K), ("conv_b", 1), ("conv_ln_g", 1), ("conv_ln_b", 1))
REP_NAMES = (("even_norm", (1024,)), ("pool_w", (4, 128, 128)), ("pool_scale", (512,)), ("ssm_log_dt", (32,)),
             ("ssm_a_re", (32, 64)), ("ssm_a_im", (32, 64)), ("ssm_b_re", (32, 64, 16)), ("ssm_b_im", (32, 64, 16)),
             ("ssm_c_re", (32, 16, 64)), ("ssm_c_im", (32, 16, 64)), ("ssm_d", (512,)), ("final_norm", (1024,)))


def _pack_rep(d):
    flat = jnp.concatenate([d[n].reshape(-1) for n, _ in REP_NAMES])
    return jnp.pad(flat, (0, REP_ROWS * D_MODEL - flat.shape[0])).reshape(REP_ROWS, D_MODEL)


def _unpack_rep(buf):
    flat = buf.reshape(-1)
    out = {}
    off = 0
    for n, shp in REP_NAMES:
        size = 1
        for s in shp:
            size *= s
        out[n] = flat[off:off + size].reshape(shp)
        off += size
    return out


def _cols_split(full, cols):
    rows = full.shape[0]
    return full.reshape(rows, 4, cols).transpose(1, 0, 2).reshape(4, -1, D_MODEL)


def _block_diag(a):
    a = a.reshape(4, 8, GROUP_DIM, N_STATE)
    eye = jnp.eye(8, dtype=a.dtype)
    return (a[:, :, :, None, :] * eye[None, :, None, :, None]).reshape(4, 128, LCH)


def _block_diag_take(m):
    m = m.reshape(4, 8, GROUP_DIM, 8, N_STATE)
    eye = jnp.eye(8, dtype=m.dtype)
    return jnp.sum(m * eye[None, :, None, :, None], axis=3).reshape(N_GROUPS, GROUP_DIM, N_STATE)


def _ssm_discretise(log_dt, a_re, a_im, b_re, b_im):
    dt = jnp.exp(log_dt)[:, None]
    mag = jnp.exp(a_re * dt)
    ang = a_im * dt
    abar_re = mag * jnp.cos(ang)
    abar_im = mag * jnp.sin(ang)
    den = a_re * a_re + a_im * a_im
    nr = abar_re - 1.0
    ni = abar_im
    k_re = (nr * a_re + ni * a_im) / den
    k_im = (ni * a_re - nr * a_im) / den
    bb_re = k_re[..., None] * b_re - k_im[..., None] * b_im
    bb_im = k_re[..., None] * b_im + k_im[..., None] * b_re
    return abar_re, abar_im, bb_re, bb_im


def _scan_tables(log_dt, a_re, a_im):
    dt = jnp.exp(log_dt)[:, None]
    lam_re = (a_re * dt).reshape(1, STATES)
    lam_im = (a_im * dt).reshape(1, STATES)

    def powers(k):
        mag = jnp.exp(k * lam_re)
        return mag * jnp.cos(k * lam_im), mag * jnp.sin(k * lam_im)

    p_re, p_im = powers((1 + jnp.arange(TM) // SUB).astype(F32)[:, None])
    q_re, q_im = powers((SEG_LEN * (1 + jnp.arange(SUB))).astype(F32)[:, None])
    return p_re, p_im, q_re, q_im


def _local_step(x, tgt, w, shard):
    row = lambda a: a.reshape(1, -1)
    (e_w_in,) = _gather_weights([shard["even_w_in"].astype(BF16)])
    wp = w["pool_w"].astype(BF16)
    ssm_in = (w["ssm_log_dt"], w["ssm_a_re"], w["ssm_a_im"], w["ssm_b_re"], w["ssm_b_im"])
    (abar_re, abar_im, bb_re, bb_im), ssm_vjp = jax.vjp(_ssm_discretise, *ssm_in)
    mb_re = _block_diag(bb_re.transpose(0, 2, 1)).astype(BF16)
    mb_im = _block_diag(bb_im.transpose(0, 2, 1)).astype(BF16)
    cm_re = _block_diag(w["ssm_c_re"]).astype(BF16)
    cm_im = _block_diag(w["ssm_c_im"]).astype(BF16)
    p8_re, p8_im, q_re, q_im = _scan_tables(w["ssm_log_dt"], w["ssm_a_re"], w["ssm_a_im"])
    qr_re, qr_im = q_re[::-1], q_im[::-1]
    pm = _perm_matrix()
    pmt = pm.T
    g0, gf = row(w["even_norm"]), row(w["final_norm"])
    ps, dskip = row(w["pool_scale"]), row(w["ssm_d"])

    proj, yp, (g_glu, g_eout) = _norm_in(
        x, g0, e_w_in, "even_in", ride=[shard["ssm_w_glu"].astype(BF16), shard["even_w_out"].astype(BF16)],
        pool=(wp, ps))
    wglu = g_glu.transpose(1, 0, 2).reshape(SSM_W, 2 * SSM_W)
    e_w_out = g_eout.reshape(D_MODEL, D_MODEL)
    (ys, car_re, car_im), (g_oin, g_oout, g_small) = _ssm_fwd(
        proj, pm, pmt, mb_re, mb_im, p8_re, p8_im, q_re, q_im, cm_re, cm_im, dskip, wglu, ride=_odd_shards(shard))
    o_w_in, o_w_out = g_oin, g_oout.reshape(D_MODEL, D_MODEL)
    sm = g_small.transpose(1, 0, 2).reshape(SMALL_ROWS, D_MODEL)
    cw = sm[1:1 + HALO]
    g1, cb, lg, lb = sm[0:1], sm[32:33], sm[33:34], sm[34:35]
    x1, yg = _even_out(yp, ys, proj, x, e_w_out)
    q, _, _ = _norm_in(x1, g1, o_w_in, "odd_in")
    y2, cv = _conv_fwd(q, cw, cb, lg, lb)
    dx2, loss_lanes, d_gf = _odd_out_loss(y2, x1, o_w_out, gf, tgt)

    dcv, dz2, d_o_w_out, d_lg, d_lb = _odd_bwd_out(dx2, o_w_out, y2, cv, q, lg, lb)
    dval, dgate, d_cw, d_cb = _conv_bwd(dcv, q, cw)
    dx1, d_g1, d_o_w_in = _in_bwd([dval, dgate, dz2], o_w_in, x1, g1, dx2, "odd_in_bwd")
    g_odd = _pack_odd_grads({
        "odd_w_in": d_o_w_in, "odd_w_out": d_o_w_out, "odd_norm": d_g1.reshape(-1),
        "conv_w": d_cw.reshape(HALO, SUB, D_MODEL).sum(axis=1)[:CONV_K], "conv_b": d_cb.reshape(-1),
        "conv_ln_g": d_lg.reshape(-1), "conv_ln_b": d_lb.reshape(-1)})
    dycat, dz, d_e_w_out, dup, d_wp, d_ps, theirs_odd = _even_bwd_out(dx1, e_w_out, yg, yp, ys, proj, wp, ps, g_odd)
    sums_odd = _pair_add(g_odd, theirs_odd, BF16, "pair_add_odd")
    (dus, d_mb_re, d_mb_im, d_cm_re, d_cm_im, da_re, da_im, d_dskip, d_wglu, got_odd) = _ssm_bwd(
        dycat, proj, car_re, car_im, pm, pmt, mb_re, mb_im, p8_re, p8_im, q_re, q_im, qr_re, qr_im,
        cm_re, cm_im, dskip, wglu, sums_odd)
    d_abar_re = jnp.sum(da_re, axis=0).reshape(N_GROUPS, N_STATE)
    d_abar_im = jnp.sum(da_im, axis=0).reshape(N_GROUPS, N_STATE)
    d_bb_re = _block_diag_take(d_mb_re).transpose(0, 2, 1)
    d_bb_im = _block_diag_take(d_mb_im).transpose(0, 2, 1)
    d_log_dt, d_a_re, d_a_im, d_b_re, d_b_im = ssm_vjp((d_abar_re, d_abar_im, d_bb_re, d_bb_im))
    rep_early = _pack_rep({
        "even_norm": jnp.zeros((D_MODEL,), F32), "pool_w": d_wp, "pool_scale": d_ps.reshape(-1),
        "ssm_log_dt": d_log_dt, "ssm_a_re": d_a_re, "ssm_a_im": d_a_im, "ssm_b_re": d_b_re, "ssm_b_im": d_b_im,
        "ssm_c_re": _block_diag_take(d_cm_re), "ssm_c_im": _block_diag_take(d_cm_im),
        "ssm_d": d_dskip.reshape(-1), "final_norm": d_gf.reshape(-1)})
    odd_mine = _sum_lead(got_odd, "chip_sum_odd")
    dx, d_g0, d_e_w_in, rep_parts, odd_theirs = _in_bwd([dup, dus, dz], e_w_in, x, g0, dx1, "even_in_bwd",
                                                        ride=(rep_early, odd_mine))

    grads = {"even_norm": d_g0, "even_w_in": d_e_w_in, "ssm_w_glu": d_wglu, "even_w_out": d_e_w_out}
    return jnp.sum(loss_lanes), dx, grads, (odd_mine, odd_theirs), rep_parts


WEIGHT_NAMES = ("even_norm", "even_w_in", "pool_w", "pool_scale", "ssm_log_dt", "ssm_a_re", "ssm_a_im",
                "ssm_b_re", "ssm_b_im", "ssm_c_re", "ssm_c_im", "ssm_d", "ssm_w_glu", "even_w_out", "odd_norm",
                "odd_w_in", "conv_w", "conv_b", "conv_ln_g", "conv_ln_b", "odd_w_out", "final_norm")
SHARDED = tuple(n for n, _ in SHARD_BIG) + tuple(n for n, _ in SHARD_SMALL)


SMALL_ROWS = 64


def _odd_shards(shard):
    small = jnp.concatenate([shard[n].reshape(r, 256) for n, r in SHARD_SMALL], axis=0)
    small = jnp.pad(small, ((0, SMALL_ROWS - small.shape[0]), (0, 0)))
    return [shard["odd_w_in"].astype(BF16), shard["odd_w_out"].astype(BF16), small]


def _pack_small(d):
    small = jnp.concatenate([d[n].reshape(r, -1) for n, r in SHARD_SMALL], axis=0)
    if small.shape[1] == D_MODEL:
        small = small.reshape(35, 4, 256).transpose(1, 0, 2)
    small = small.reshape(-1, 35 * 256)
    small = jnp.pad(small, ((0, 0), (0, ROWS_SMALL * D_MODEL - 35 * 256)))
    return small.reshape(-1, ROWS_SMALL, D_MODEL)


def _unpack_small(buf):
    small = buf.reshape(-1)[:35 * 256].reshape(35, 256)
    out = {}
    off = 0
    for n, r in SHARD_SMALL:
        out[n] = small[off:off + r].reshape((r, 256) if r > 1 else (256,))
        off += r
    return out


EVEN_PACK = (("even_w_in", 0, 512), ("ssm_w_glu", 512, 128), ("even_w_out", 640, 256))
ROWS_EVEN = 1024
ODD_PACK = (("odd_w_in", 0, 768), ("odd_w_out", 768, 256))
ODD_SMALL_ROW = 1024
ROWS_ODD = 1280


def _pack_even_grads(g):
    parts = [g["even_w_in"].reshape(4, -1, D_MODEL), _cols_split(g["ssm_w_glu"], 256),
             g["even_w_out"].reshape(4, -1, D_MODEL), jnp.zeros((4, ROWS_EVEN - 896, D_MODEL), F32)]
    return jnp.concatenate(parts, axis=1)


def _pack_odd_grads(g):
    parts = [g["odd_w_in"].reshape(4, -1, D_MODEL), g["odd_w_out"].reshape(4, -1, D_MODEL), _pack_small(g),
             jnp.zeros((4, ROWS_ODD - ODD_SMALL_ROW - ROWS_SMALL, D_MODEL), F32)]
    return jnp.concatenate(parts, axis=1)


def kernel(x, even_norm, even_w_in, pool_w, pool_scale, ssm_log_dt, ssm_a_re, ssm_a_im, ssm_b_re, ssm_b_im, ssm_c_re, ssm_c_im, ssm_d, ssm_w_glu, even_w_out, odd_norm, odd_w_in, conv_w, conv_b, conv_ln_g, conv_ln_b, odd_w_out, final_norm, loss_target, m_even_norm, m_even_w_in, m_pool_w, m_pool_scale, m_ssm_log_dt, m_ssm_a_re, m_ssm_a_im, m_ssm_b_re, m_ssm_b_im, m_ssm_c_re, m_ssm_c_im, m_ssm_d, m_ssm_w_glu, m_even_w_out, m_odd_norm, m_odd_w_in, m_conv_w, m_conv_b, m_conv_ln_g, m_conv_ln_b, m_odd_w_out, m_final_norm, v_even_norm, v_even_w_in, v_pool_w, v_pool_scale, v_ssm_log_dt, v_ssm_a_re, v_ssm_a_im, v_ssm_b_re, v_ssm_b_im, v_ssm_c_re, v_ssm_c_im, v_ssm_d, v_ssm_w_glu, v_even_w_out, v_odd_norm, v_odd_w_in, v_conv_w, v_conv_b, v_conv_ln_g, v_conv_ln_b, v_odd_w_out, v_final_norm):
    ws = dict(zip(WEIGHT_NAMES, (even_norm, even_w_in, pool_w, pool_scale, ssm_log_dt, ssm_a_re, ssm_a_im, ssm_b_re,
                                 ssm_b_im, ssm_c_re, ssm_c_im, ssm_d, ssm_w_glu, even_w_out, odd_norm, odd_w_in,
                                 conv_w, conv_b, conv_ln_g, conv_ln_b, odd_w_out, final_norm)))
    ms = dict(zip(WEIGHT_NAMES, (m_even_norm, m_even_w_in, m_pool_w, m_pool_scale, m_ssm_log_dt, m_ssm_a_re,
                                 m_ssm_a_im, m_ssm_b_re, m_ssm_b_im, m_ssm_c_re, m_ssm_c_im, m_ssm_d, m_ssm_w_glu,
                                 m_even_w_out, m_odd_norm, m_odd_w_in, m_conv_w, m_conv_b, m_conv_ln_g, m_conv_ln_b,
                                 m_odd_w_out, m_final_norm)))
    vs = dict(zip(WEIGHT_NAMES, (v_even_norm, v_even_w_in, v_pool_w, v_pool_scale, v_ssm_log_dt, v_ssm_a_re,
                                 v_ssm_a_im, v_ssm_b_re, v_ssm_b_im, v_ssm_c_re, v_ssm_c_im, v_ssm_d, v_ssm_w_glu,
                                 v_even_w_out, v_odd_norm, v_odd_w_in, v_conv_w, v_conv_b, v_conv_ln_g, v_conv_ln_b,
                                 v_odd_w_out, v_final_norm)))
    lead = {n: a.shape for n, a in ws.items()}
    drop = lambda d: {n: (a[0] if n != "final_norm" else a) for n, a in d.items()}
    ws, ms, vs = drop(ws), drop(ms), drop(vs)

    shard = {n: ws[n] for n in SHARDED}
    rep = {n: ws[n] for n, _ in REP_NAMES}
    loss_part, grad_x, grads, (odd_mine, odd_theirs), rep_parts = _local_step(x[0], loss_target[0], rep, shard)
    loss = lax.psum(loss_part, ("x", "y", "c"))

    g_even = _pack_even_grads(grads)
    got_even, late_parts = _scatter_and_gather(
        _pair_add(g_even, _pair_split(g_even, "pair_split_even"), BF16, "pair_add_even"),
        jnp.pad(grads["even_norm"], ((0, SUB - 1), (0, 0))), "scatter_even_gather_late")
    even_mine = _sum_lead(got_even, "chip_sum_even")
    even_theirs = _pair_join(even_mine, "pair_join_even")
    outs = [{}, {}, {}, {}]
    for pack, mine, theirs in ((EVEN_PACK, even_mine, even_theirs), (ODD_PACK, odd_mine, odd_theirs)):
        for n, row0, rows in pack:
            view = lambda a: a.reshape(rows, D_MODEL)
            res = _adamw_rows(view(ws[n]), view(ms[n]), view(vs[n]), mine, theirs, row0, 128, "adamw_" + n)
            for o, r in zip(outs, res):
                o[n] = r
    small = lambda d: _pack_small({n: d[n] for n, _ in SHARD_SMALL})[0]
    res = _adamw_rows(small(ws), small(ms), small(vs), odd_mine, odd_theirs, ODD_SMALL_ROW, ROWS_SMALL, "adamw_small")
    for o, r in zip(outs, res):
        o.update(_unpack_small(r))
    g_rep = lax.dynamic_update_slice(_sum_lead(rep_parts, "rep_sum"), _sum_lead(late_parts, "late_sum")[0:1], (0, 0))
    res = _adamw(_pack_rep(rep), g_rep, _pack_rep({n: ms[n] for n, _ in REP_NAMES}),
                 _pack_rep({n: vs[n] for n, _ in REP_NAMES}), "adamw_rep")
    for o, r in zip(outs, (g_rep,) + tuple(res)):
        o.update(_unpack_rep(r))

    leaves = [[o[n].reshape(lead[n]) for n in WEIGHT_NAMES] for o in outs]
    return (loss, grad_x[None], *leaves[0], *leaves[1], *leaves[2], *leaves[3])
```

```python
import jax
import jax.numpy as jnp
from jax import lax
from jax.experimental import pallas as pl
from jax.experimental.pallas import tpu as pltpu

F32 = jnp.float32
BF16 = jnp.bfloat16
MESH = pl.DeviceIdType.MESH

D_MODEL = 1024
RMS_EPS = 1e-6
LN_EPS = 1e-5
N_GROUPS = 32
GROUP_DIM = 16
N_STATE = 64
STATES = N_GROUPS * N_STATE
SSM_W = 512
POOL_W = 512
CONV_K = 31
HALO = 32
POOL_HALO = 16

ADAM_LR = 0.001
ADAM_B1 = 0.9
ADAM_B2 = 0.999
ADAM_EPS = 1e-08
ADAM_WD = 0.01
ADAM_STEP = 10

TM = 256
TM_MM = 512
DEPTH = 3
SUB = 8
LCH = 512
SCAN_L = 1024
VMEM_LIMIT = 56 * 1024 * 1024

ROWS_SMALL = 16
REP_ROWS = 200
COMM_CHUNKS = 4


def _params(n_axes=1):
    return pltpu.CompilerParams(dimension_semantics=("arbitrary",) * n_axes, vmem_limit_bytes=VMEM_LIMIT)


def _rows(w, cb=0, rev=None, tm=TM):
    if rev is None:
        return pl.BlockSpec((tm, w), lambda i: (i, cb))
    return pl.BlockSpec((tm, w), lambda i: (rev - 1 - i, cb))


def _mm_rows(w, cb=0):
    return _rows(w, cb, tm=TM_MM)


def _full(shape):
    n = len(shape)
    return pl.BlockSpec(shape, lambda i: (0,) * n)


def _prev(hr, w, cb=0, tm=TM):
    r = tm // hr
    return pl.BlockSpec((hr, w), lambda i: (jnp.maximum(i * r - 1, 0), cb))


def _next(hr, w, nrows, cb=0, tm=TM):
    r = tm // hr
    last = nrows // hr - 1
    return pl.BlockSpec((hr, w), lambda i: (jnp.minimum((i + 1) * r, last), cb))


def _dot(a, b):
    return jnp.dot(a, b, preferred_element_type=F32)


def _dot_nt(a, b):
    return lax.dot_general(a, b, (((1,), (1,)), ((), ())), preferred_element_type=F32)


def _dot_tn(a, b):
    return lax.dot_general(a, b, (((0,), (0,)), ((), ())), preferred_element_type=F32)


def _sig(x):
    return 1.0 / (1.0 + jnp.exp(-x))


def _zero_at_first(i, *refs):
    @pl.when(i == 0)
    def _():
        for r in refs:
            r[...] = jnp.zeros_like(r)


def _norm_in(x, g, w, name, ride=(), pool=None):
    t, ns = x.shape[0], w.shape[2]
    n = 4 * ns
    ng = len(ride)
    npool = 0 if pool is None else 1
    nstep = t // TM_MM

    def body(x_ref, g_ref, w_ref, *rest):
        pool_in, rest = rest[:2 * npool], rest[2 * npool:]
        ride_in, o_ref, rest = rest[:ng], rest[ng], rest[ng + 1:]
        yp_ref, rest = (rest[0], rest[1:]) if npool else (None, rest)
        ride_out, rest = rest[:ng], rest[ng:]
        halo_ref, rest = (rest[0], rest[1:]) if npool else (None, rest)
        i = pl.program_id(0)
        if ng:
            start, forward, finish = _gather_phases(ride_in, ride_out, rest[:ng], *rest[ng:])
            pl.when(i == 0)(start)
            pl.when(i == nstep // 2)(forward)
        xv = x_ref[...]
        r = lax.rsqrt(jnp.mean(xv * xv, axis=-1, keepdims=True) + RMS_EPS)
        h = (xv * r * g_ref[...]).astype(BF16)
        first = _dot(h, w_ref[0]).astype(BF16)
        o_ref[:, 0:ns] = first
        if npool:
            wp_ref, ps_ref = pool_in
            _zero_at_first(i, halo_ref)
            pos = (i * TM_MM + 1 + lax.broadcasted_iota(jnp.int32, (TM_MM, 1), 0)).astype(F32)
            for gi in range(4):
                sl = slice(128 * gi, 128 * (gi + 1))
                u = first[:, sl].astype(F32)
                s = _pool_sums(jnp.concatenate([halo_ref[:, sl], u], axis=0), gi, True)[POOL_HALO:, :]
                pooled = s / jnp.minimum(pos, float(2 << gi)) - u
                yp_ref[:, sl] = (_dot(pooled.astype(BF16), wp_ref[gi]) * ps_ref[:, sl]).astype(BF16)
                halo_ref[:, sl] = u[TM_MM - POOL_HALO:, :]
        for s in range(1, 4):
            o_ref[:, s * ns:(s + 1) * ns] = _dot(h, w_ref[s]).astype(BF16)
        if ng:
            pl.when(i == nstep - 1)(finish)

    pool_args = [] if pool is None else list(pool)
    res = pl.pallas_call(
        body, name=name, grid=(nstep,),
        in_specs=[_mm_rows(D_MODEL), _full((1, D_MODEL)), _full(w.shape)] + [_full(a.shape) for a in pool_args]
        + [ANY] * ng,
        out_specs=[_mm_rows(n)] + [_mm_rows(POOL_W)] * npool + [ANY] * ng,
        out_shape=[jax.ShapeDtypeStruct((t, n), BF16)] + [jax.ShapeDtypeStruct((t, POOL_W), BF16)] * npool
        + [jax.ShapeDtypeStruct((4,) + a.shape, a.dtype) for a in ride],
        scratch_shapes=[pltpu.VMEM((POOL_HALO, POOL_W), F32)] * npool + (_gather_scratch(ride) if ng else []),
        compiler_params=_params())(x, g, w, *pool_args, *ride)
    return res[0], (res[1] if npool else None), res[1 + npool:]


def _pool_sums(ext, g, forward):
    n = ext.shape[0]
    s = ext
    for step in range(g + 1):
        k = 1 << step
        s = s + pltpu.roll(s, k if forward else n - k, 0)
    return s


SEG_LEN = TM // SUB


def _perm_matrix():
    p = jnp.arange(TM)
    src = (p % SUB) * SEG_LEN + p // SUB
    return (src[:, None] == jnp.arange(TM)[None, :]).astype(BF16)


def _cmul_add(are, aim, vre, vim, bre, bim):
    return are * vre - aim * vim + bre, are * vim + aim * vre + bim


def _segment_chain(ere, eim, qre, qim, cin_re, cin_im, row, up):
    for sh in (1, 2, 4):
        mre, mim = (qre[SUB - sh:SUB - sh + 1, :], qim[SUB - sh:SUB - sh + 1, :]) if up else \
                   (qre[sh - 1:sh, :], qim[sh - 1:sh, :])
        keep = (row < SUB - sh) if up else (row >= sh)
        sre = jnp.where(keep, pltpu.roll(ere, SUB - sh if up else sh, 0), 0.0)
        sim = jnp.where(keep, pltpu.roll(eim, SUB - sh if up else sh, 0), 0.0)
        ere, eim = _cmul_add(mre, mim, sre, sim, ere, eim)
    ere, eim = _cmul_add(qre, qim, cin_re, cin_im, ere, eim)
    keep = (row < SUB - 1) if up else (row >= 1)
    ent_re = jnp.where(keep, pltpu.roll(ere, SUB - 1 if up else 1, 0), cin_re)
    ent_im = jnp.where(keep, pltpu.roll(eim, SUB - 1 if up else 1, 0), cin_im)
    return ere, eim, ent_re, ent_im


def _scan_fwd_block(xs_re, xs_im, p8_re, p8_im, q_re, q_im, car_re, car_im, ent_re_ref, ent_im_ref):
    row = lax.broadcasted_iota(jnp.int32, (SUB, SCAN_L), 0)
    for j in range(STATES // SCAN_L):
        sl = slice(SCAN_L * j, SCAN_L * (j + 1))
        are, aim = p8_re[0:SUB, sl], p8_im[0:SUB, sl]

        def totals(i, v, sl=sl, are=are, aim=aim):
            r0 = pl.multiple_of(i * SUB, SUB)
            vre, vim = _cmul_add(are, aim, v[0], v[1], xs_re[pl.ds(r0, SUB), sl], xs_im[pl.ds(r0, SUB), sl])
            xs_re[pl.ds(r0, SUB), sl] = vre
            xs_im[pl.ds(r0, SUB), sl] = vim
            return vre, vim

        ere, eim = lax.fori_loop(1, SEG_LEN, totals, (xs_re[0:SUB, sl], xs_im[0:SUB, sl]), unroll=2)
        ere, eim, cre, cim = _segment_chain(ere, eim, q_re[:, sl], q_im[:, sl],
                                            car_re[:, sl], car_im[:, sl], row, False)
        car_re[:, sl] = jnp.broadcast_to(ere[SUB - 1:SUB, :], (SUB, SCAN_L))
        car_im[:, sl] = jnp.broadcast_to(eim[SUB - 1:SUB, :], (SUB, SCAN_L))
        if ent_re_ref is not None:
            ent_re_ref[:, sl] = cre
            ent_im_ref[:, sl] = cim

        def fix(i, c, sl=sl, cre=cre, cim=cim):
            r0 = pl.multiple_of(i * SUB, SUB)
            vre, vim = _cmul_add(p8_re[pl.ds(r0, SUB), sl], p8_im[pl.ds(r0, SUB), sl], cre, cim,
                                 xs_re[pl.ds(r0, SUB), sl], xs_im[pl.ds(r0, SUB), sl])
            xs_re[pl.ds(r0, SUB), sl] = vre
            xs_im[pl.ds(r0, SUB), sl] = vim
            return c

        lax.fori_loop(0, SEG_LEN, fix, 0, unroll=2)


def _ssm_fwd(proj, pm, pmt, mb_re, mb_im, p8_re, p8_im, q_re, q_im, cm_re, cm_im, dskip, wglu, ride=()):
    t = proj.shape[0]
    nblk = t // TM

    ng = len(ride)

    def body(u_ref, pm_ref, pmt_ref, mbre, mbim, p8re, p8im, qre, qim, cmre, cmim, d_ref, wg_ref, *rest):
        ride_in, (y_ref, cre_ref, cim_ref), ride_out = rest[:ng], rest[ng:ng + 3], rest[ng + 3:2 * ng + 3]
        xs_re, xs_im, car_re, car_im, ysk = rest[2 * ng + 3:2 * ng + 8]
        i = pl.program_id(0)
        if ng:
            start, forward, finish = _gather_phases(ride_in, ride_out, rest[2 * ng + 8:3 * ng + 8],
                                                    *rest[3 * ng + 8:])
            pl.when(i == 0)(start)
            pl.when(i == nblk // 2)(forward)
        _zero_at_first(i, car_re, car_im)
        cre_ref[0] = car_re[...]
        cim_ref[0] = car_im[...]
        us = _dot(pm_ref[...], u_ref[...])
        usb = us.astype(BF16)
        for j in range(4):
            xs_re[:, LCH * j:LCH * (j + 1)] = _dot(usb[:, 128 * j:128 * (j + 1)], mbre[j])
            xs_im[:, LCH * j:LCH * (j + 1)] = _dot(usb[:, 128 * j:128 * (j + 1)], mbim[j])
        _scan_fwd_block(xs_re, xs_im, p8re, p8im, qre, qim, car_re, car_im, None, None)
        for j in range(4):
            sl = slice(LCH * j, LCH * (j + 1))
            ysk[:, 128 * j:128 * (j + 1)] = (_dot_nt(xs_re[:, sl].astype(BF16), cmre[j])
                                             - _dot_nt(xs_im[:, sl].astype(BF16), cmim[j]))
        yv = ysk[...] + d_ref[...] * us
        gv = _dot(yv.astype(BF16), wg_ref[...])
        y_ref[...] = _dot(pmt_ref[...], (gv[:, :SSM_W] * _sig(gv[:, SSM_W:])).astype(BF16)).astype(BF16)
        if ng:
            pl.when(i == nblk - 1)(finish)

    blk = (4, 128, LCH)
    res = pl.pallas_call(
        body, name="ssm_fwd", grid=(nblk,),
        in_specs=[_rows(SSM_W, 1), _full((TM, TM)), _full((TM, TM)), _full(blk), _full(blk),
                  _full((TM, STATES)), _full((TM, STATES)), _full((SUB, STATES)), _full((SUB, STATES)),
                  _full(blk), _full(blk), _full((1, SSM_W)), _full((SSM_W, 2 * SSM_W))] + [ANY] * ng,
        out_specs=[_rows(SSM_W), pl.BlockSpec((1, SUB, STATES), lambda i: (i, 0, 0)),
                   pl.BlockSpec((1, SUB, STATES), lambda i: (i, 0, 0))] + [ANY] * ng,
        out_shape=[jax.ShapeDtypeStruct((t, SSM_W), BF16), jax.ShapeDtypeStruct((nblk, SUB, STATES), F32),
                   jax.ShapeDtypeStruct((nblk, SUB, STATES), F32)]
        + [jax.ShapeDtypeStruct((4,) + a.shape, a.dtype) for a in ride],
        scratch_shapes=[pltpu.VMEM((TM, STATES), F32), pltpu.VMEM((TM, STATES), F32),
                        pltpu.VMEM((SUB, STATES), F32), pltpu.VMEM((SUB, STATES), F32),
                        pltpu.VMEM((TM, SSM_W), F32)] + (_gather_scratch(ride) if ng else []),
        compiler_params=_params())(proj, pm, pmt, mb_re, mb_im, p8_re, p8_im, q_re, q_im, cm_re, cm_im, dskip, wglu,
                                   *ride)
    return res[:3], res[3:]


def _even_out(yp, ys, proj, x, w):
    t = x.shape[0]
    nstep = t // TM_MM

    def body(yp_hbm, ys_hbm, p_hbm, x_hbm, w_ref, x1_ref, yg_ref, yp_buf, ys_buf, z_buf, x_buf, sems):
        i = pl.program_id(0)
        slot = i % DEPTH

        def fetch(blk, sl):
            rows = pl.ds(pl.multiple_of(blk * TM_MM, TM_MM), TM_MM)
            pairs = ((yp_hbm.at[rows, :], yp_buf), (ys_hbm.at[rows, :], ys_buf),
                     (p_hbm.at[rows, pl.ds(D_MODEL, D_MODEL)], z_buf), (x_hbm.at[rows, :], x_buf))
            return [pltpu.make_async_copy(src, buf.at[sl], sems.at[k, sl]) for k, (src, buf) in enumerate(pairs)]

        @pl.when(i == 0)
        def _():
            for b in range(min(DEPTH, nstep)):
                for cp in fetch(b, b):
                    cp.start()

        for cp in fetch(i, slot):
            cp.wait()
        z = z_buf[slot].astype(F32)
        sz = z * _sig(z)
        gp = (yp_buf[slot].astype(F32) * sz[:, :POOL_W]).astype(BF16)
        gs = (ys_buf[slot].astype(F32) * sz[:, POOL_W:]).astype(BF16)
        yg_ref[:, :POOL_W] = gp
        yg_ref[:, POOL_W:] = gs
        x1_ref[...] = x_buf[slot] + _dot(gp, w_ref[:POOL_W, :]) + _dot(gs, w_ref[POOL_W:, :])

        @pl.when(i + DEPTH < nstep)
        def _():
            for cp in fetch(i + DEPTH, slot):
                cp.start()

    return pl.pallas_call(
        body, name="even_out", grid=(nstep,),
        in_specs=[ANY, ANY, ANY, ANY, _full((D_MODEL, D_MODEL))],
        out_specs=[_mm_rows(D_MODEL), _mm_rows(D_MODEL)],
        out_shape=[jax.ShapeDtypeStruct((t, D_MODEL), F32), jax.ShapeDtypeStruct((t, D_MODEL), BF16)],
        scratch_shapes=[pltpu.VMEM((DEPTH, TM_MM, POOL_W), BF16), pltpu.VMEM((DEPTH, TM_MM, SSM_W), BF16),
                        pltpu.VMEM((DEPTH, TM_MM, D_MODEL), BF16), pltpu.VMEM((DEPTH, TM_MM, D_MODEL), F32),
                        pltpu.SemaphoreType.DMA((4, DEPTH))],
        compiler_params=_params())(yp, ys, proj, x, w)


def _phase_copies(ext, cp):
    n = cp.shape[1]
    for j in range(1, SUB):
        cp[j - 1] = ext[pl.ds(j, n), :]


def _shifted(ext, cp, off, nrows, sl, row0=0):
    q, j = divmod(off, SUB)
    if j == 0:
        return ext[pl.ds(row0 + SUB * q, nrows), sl]
    return cp[j - 1, pl.ds(row0 + SUB * q, nrows), sl]


def _conv_taps(ext, cp, w_ref, first, nrows, sl, init, row0=0):
    acc = init
    for k in range(CONV_K):
        acc = acc + w_ref[k:k + 1, sl] * _shifted(ext, cp, first(k), nrows, sl, row0)
    return acc


def _conv_fwd(q, cw, cb, lg, lb):
    t = q.shape[0]

    def body(v_ref, g_ref, hv_ref, hg_ref, z_ref, w_ref, b_ref, lg_ref, lb_ref, y_ref, cv_ref, ext, cp):
        i = pl.program_id(0)
        ext[0:HALO, :] = jnp.where(i == 0, 0.0, hv_ref[...].astype(F32) * _sig(hg_ref[...].astype(F32)))
        ext[HALO:, :] = v_ref[...].astype(F32) * _sig(g_ref[...].astype(F32))
        _phase_copies(ext, cp)

        def lanes(c, carry):
            sl = pl.ds(pl.multiple_of(c * 128, 128), 128)
            cv_ref[:, sl] = _conv_taps(ext, cp, w_ref, lambda k: k + 2, TM, sl,
                                       jnp.broadcast_to(b_ref[:, sl], (TM, 128)))
            return carry

        lax.fori_loop(0, D_MODEL // 128, lanes, 0)
        cv = cv_ref[...]
        cc = cv - jnp.mean(cv, axis=-1, keepdims=True)
        rstd = lax.rsqrt(jnp.mean(cc * cc, axis=-1, keepdims=True) + LN_EPS)
        cl = cc * rstd * lg_ref[...] + lb_ref[...]
        z = z_ref[...].astype(F32)
        y_ref[...] = (cl * _sig(cl) * z * _sig(z)).astype(BF16)

    vec = _full((1, D_MODEL))
    return pl.pallas_call(
        body, name="conv_fwd", grid=(t // TM,),
        in_specs=[_rows(D_MODEL, 0), _rows(D_MODEL, 1), _prev(HALO, D_MODEL, 0), _prev(HALO, D_MODEL, 1),
                  _rows(D_MODEL, 2), _full((HALO, D_MODEL)), vec, vec, vec],
        out_specs=[_rows(D_MODEL), _rows(D_MODEL)],
        out_shape=[jax.ShapeDtypeStruct((t, D_MODEL), BF16), jax.ShapeDtypeStruct((t, D_MODEL), F32)],
        scratch_shapes=[pltpu.VMEM((TM + HALO, D_MODEL), F32),
                        pltpu.VMEM((SUB - 1, TM + HALO - SUB, D_MODEL), F32)],
        compiler_params=_params())(q, q, q, q, q, cw, cb, lg, lb)


def _odd_out_loss(y2, x1, w, gf, tgt):
    t = x1.shape[0]
    nstep = t // TM_MM
    streams = (y2, x1, tgt)

    def body(y_hbm, x_hbm, t_hbm, w_ref, g_ref, dx_ref, loss_ref, dg_ref, y_buf, x_buf, t_buf, sems):
        i = pl.program_id(0)
        slot = i % DEPTH

        def fetch(blk, sl):
            rows = pl.ds(pl.multiple_of(blk * TM_MM, TM_MM), TM_MM)
            return [pltpu.make_async_copy(src.at[rows, :], buf.at[sl], sems.at[k, sl])
                    for k, (src, buf) in enumerate(((y_hbm, y_buf), (x_hbm, x_buf), (t_hbm, t_buf)))]

        @pl.when(i == 0)
        def _():
            for b in range(min(DEPTH, nstep)):
                for cp in fetch(b, b):
                    cp.start()

        _zero_at_first(i, loss_ref, dg_ref)
        for cp in fetch(i, slot):
            cp.wait()
        x2 = x_buf[slot] + _dot(y_buf[slot], w_ref[...])
        r = lax.rsqrt(jnp.mean(x2 * x2, axis=-1, keepdims=True) + RMS_EPS)
        n = x2 * r
        e = n * g_ref[...] - t_buf[slot]
        loss_ref[...] += jnp.sum(e * e, axis=0, keepdims=True) * (0.5 / D_MODEL)
        dout = e * (1.0 / D_MODEL)
        dg_ref[...] += jnp.sum(dout * n, axis=0, keepdims=True)
        dn = dout * g_ref[...]
        dx_ref[...] = r * (dn - n * jnp.mean(dn * n, axis=-1, keepdims=True))

        @pl.when(i + DEPTH < nstep)
        def _():
            for cp in fetch(i + DEPTH, slot):
                cp.start()

    vec = _full((1, D_MODEL))
    return pl.pallas_call(
        body, name="odd_out_loss", grid=(nstep,),
        in_specs=[ANY, ANY, ANY, _full((D_MODEL, D_MODEL)), vec],
        out_specs=[_mm_rows(D_MODEL), vec, vec],
        out_shape=[jax.ShapeDtypeStruct((t, D_MODEL), F32), jax.ShapeDtypeStruct((1, D_MODEL), F32),
                   jax.ShapeDtypeStruct((1, D_MODEL), F32)],
        scratch_shapes=[pltpu.VMEM((DEPTH, TM_MM, D_MODEL), a.dtype) for a in streams]
        + [pltpu.SemaphoreType.DMA((len(streams), DEPTH))],
        compiler_params=_params())(y2, x1, tgt, w, gf)


def _dsilu(z):
    s = _sig(z)
    return z * s, s * (1.0 + z * (1.0 - s))


def _odd_bwd_out(dx2, w, y2, cv, q, lg, lb):
    t = dx2.shape[0]

    def body(dx_ref, w_ref, y_ref, cv_ref, z_ref, lg_ref, lb_ref, dcv_ref, dz_ref, dw_ref, dlg_ref, dlb_ref):
        i = pl.program_id(0)
        _zero_at_first(i, dw_ref, dlg_ref, dlb_ref)
        dxb = dx_ref[...].astype(BF16)
        dy = _dot_nt(dxb, w_ref[...])
        dw_ref[...] += _dot_tn(y_ref[...], dxb)
        cv = cv_ref[...]
        cc = cv - jnp.mean(cv, axis=-1, keepdims=True)
        rstd = lax.rsqrt(jnp.mean(cc * cc, axis=-1, keepdims=True) + LN_EPS)
        cn = cc * rstd
        silu_c, dsilu_c = _dsilu(cn * lg_ref[...] + lb_ref[...])
        silu_z, dsilu_z = _dsilu(z_ref[...].astype(F32))
        dcl = dy * silu_z * dsilu_c
        dz_ref[...] = (dy * silu_c * dsilu_z).astype(BF16)
        dlg_ref[...] += jnp.sum(dcl * cn, axis=0, keepdims=True)
        dlb_ref[...] += jnp.sum(dcl, axis=0, keepdims=True)
        dcn = dcl * lg_ref[...]
        dcv_ref[...] = rstd * (dcn - jnp.mean(dcn, axis=-1, keepdims=True)
                               - cn * jnp.mean(dcn * cn, axis=-1, keepdims=True))

    vec = _full((1, D_MODEL))
    mat = _full((D_MODEL, D_MODEL))
    return pl.pallas_call(
        body, name="odd_bwd_out", grid=(t // TM_MM,),
        in_specs=[_mm_rows(D_MODEL), mat, _mm_rows(D_MODEL), _mm_rows(D_MODEL), _mm_rows(D_MODEL, 2), vec, vec],
        out_specs=[_mm_rows(D_MODEL), _mm_rows(D_MODEL), mat, vec, vec],
        out_shape=[jax.ShapeDtypeStruct((t, D_MODEL), F32), jax.ShapeDtypeStruct((t, D_MODEL), BF16),
                   jax.ShapeDtypeStruct((D_MODEL, D_MODEL), F32), jax.ShapeDtypeStruct((1, D_MODEL), F32),
                   jax.ShapeDtypeStruct((1, D_MODEL), F32)],
        compiler_params=_params())(dx2, w, y2, cv, q, lg, lb)


def _conv_bwd(dcv, q, cw):
    t = dcv.shape[0]
    nblk = t // TM

    def body(d_ref, dn_ref, v_ref, g_ref, hv_ref, hg_ref, w_ref,
             dv_ref, dgt_ref, dw_ref, db_ref, gext, dext, dgl, gcp, dcp):
        i = pl.program_id(0)
        last = nblk - 1
        _zero_at_first(i, dw_ref, db_ref)
        v = v_ref[...].astype(F32)
        sg = _sig(g_ref[...].astype(F32))
        gext[0:HALO, :] = jnp.where(i == 0, 0.0, hv_ref[...].astype(F32) * _sig(hg_ref[...].astype(F32)))
        gext[HALO:, :] = v * sg
        d = d_ref[...]
        dext[0:TM, :] = d
        dext[TM:, :] = jnp.where(i == last, 0.0, dn_ref[...])
        _phase_copies(gext, gcp)
        _phase_copies(dext, dcp)
        db_ref[...] += jnp.sum(d, axis=0, keepdims=True)
        def lanes(c, carry):
            sl = pl.ds(pl.multiple_of(c * 128, 128), 128)
            dgl[:, sl] = _conv_taps(dext, dcp, w_ref, lambda k: 30 - k, TM, sl, jnp.zeros((TM, 128), F32))
            return carry

        def lanes_w(c, carry):
            sl = pl.ds(pl.multiple_of(c * 128, 128), 128)
            ntile = TM // SUB
            dts = [d_ref[SUB * r:SUB * (r + 1), sl] for r in range(ntile)]
            for j in range(SUB):
                taps = [(q, SUB * q + j - 2) for q in range(5) if 0 <= SUB * q + j - 2 < CONV_K]
                sums = {k: None for _, k in taps}
                for rt in range(ntile + 4):
                    need = [(q, k) for q, k in taps if 0 <= rt - q < ntile]
                    if not need:
                        continue
                    src = gext[SUB * rt:SUB * (rt + 1), sl] if j == 0 else gcp[j - 1, SUB * rt:SUB * (rt + 1), sl]
                    for q, k in need:
                        prod = dts[rt - q] * src
                        sums[k] = prod if sums[k] is None else sums[k] + prod
                for _, k in taps:
                    dw_ref[SUB * k:SUB * (k + 1), sl] += sums[k]
            return carry

        lax.fori_loop(0, D_MODEL // 128, lanes, 0)
        lax.fori_loop(0, D_MODEL // 128, lanes_w, 0)
        dg = dgl[...]
        dv_ref[...] = (dg * sg).astype(BF16)
        dgt_ref[...] = (dg * v * sg * (1.0 - sg)).astype(BF16)

    return pl.pallas_call(
        body, name="conv_bwd", grid=(t // TM,),
        in_specs=[_rows(D_MODEL), _next(HALO, D_MODEL, t), _rows(D_MODEL, 0), _rows(D_MODEL, 1),
                  _prev(HALO, D_MODEL, 0), _prev(HALO, D_MODEL, 1), _full((HALO, D_MODEL))],
        out_specs=[_rows(D_MODEL), _rows(D_MODEL), _full((HALO * SUB, D_MODEL)), _full((1, D_MODEL))],
        out_shape=[jax.ShapeDtypeStruct((t, D_MODEL), BF16), jax.ShapeDtypeStruct((t, D_MODEL), BF16),
                   jax.ShapeDtypeStruct((HALO * SUB, D_MODEL), F32), jax.ShapeDtypeStruct((1, D_MODEL), F32)],
        scratch_shapes=[pltpu.VMEM((TM + HALO, D_MODEL), F32), pltpu.VMEM((TM + HALO, D_MODEL), F32),
                        pltpu.VMEM((TM, D_MODEL), F32),
                        pltpu.VMEM((SUB - 1, TM + HALO - SUB, D_MODEL), F32),
                        pltpu.VMEM((SUB - 1, TM + HALO - SUB, D_MODEL), F32)],
        compiler_params=_params())(dcv, dcv, q, q, q, q, cw)


def _column_segments(widths, ns):
    segs = []
    col = 0
    for p, wd in enumerate(widths):
        a = 0
        while a < wd:
            s, lo = divmod(col + a, ns)
            ln = min(wd - a, ns - lo)
            segs.append((p, a, a + ln, s, lo, lo + ln))
            a += ln
        col += wd
    return segs


def _in_bwd(dparts, w, x, g, dres, name, ride=None):
    t = x.shape[0]
    widths = [p.shape[1] for p in dparts]
    npart = len(dparts)
    segs = _column_segments(widths, w.shape[2])
    nstep = t // TM_MM

    def body(*refs):
        d_refs = refs[:npart]
        w_ref, x_ref, g_ref, r_ref = refs[npart:npart + 4]
        if ride is None:
            dx_ref, dg_ref, dw_ref = refs[npart + 4:]
        else:
            v_ref, j_ref, dx_ref, dg_ref, dw_ref, o_ref, jo_ref = refs[npart + 4:npart + 11]
            start, forward, finish = _gather_all_phases(v_ref, o_ref, *refs[npart + 11:npart + 15])
            j_start, j_finish = _pair_join_phases(j_ref, jo_ref, *refs[npart + 15:])
        i = pl.program_id(0)
        if ride is not None:
            pl.when(i == 0)(start)
            pl.when(i == 0)(j_start)
            pl.when(i == nstep // 2)(forward)
        _zero_at_first(i, dg_ref, dw_ref)
        xv = x_ref[...]
        r = lax.rsqrt(jnp.mean(xv * xv, axis=-1, keepdims=True) + RMS_EPS)
        n = xv * r
        h = (n * g_ref[...]).astype(BF16)
        dh = None
        for p, lo, hi, s, slo, shi in segs:
            d = d_refs[p][:, lo:hi]
            part = _dot_nt(d, w_ref[s, :, slo:shi])
            dh = part if dh is None else dh + part
            dw_ref[s, :, slo:shi] += _dot_tn(h, d)
        dg_ref[...] += jnp.sum(dh * n, axis=0, keepdims=True)
        dn = dh * g_ref[...]
        dx_ref[...] = r_ref[...] + r * (dn - n * jnp.mean(dn * n, axis=-1, keepdims=True))
        if ride is not None:
            pl.when(i == nstep - 1)(finish)
            pl.when(i == nstep - 1)(j_finish)

    vec = _full((1, D_MODEL))
    once = pl.BlockSpec(w.shape, lambda i: (0, 0, 0), pipeline_mode=pl.Buffered(1))
    extra = [] if ride is None else list(ride)
    return pl.pallas_call(
        body, name=name, grid=(nstep,),
        in_specs=[_mm_rows(wd) for wd in widths] + [once, _mm_rows(D_MODEL), vec, _mm_rows(D_MODEL)]
        + [ANY] * len(extra),
        out_specs=[_mm_rows(D_MODEL), vec, once] + [ANY] * len(extra),
        out_shape=[jax.ShapeDtypeStruct((t, D_MODEL), F32), jax.ShapeDtypeStruct((1, D_MODEL), F32),
                   jax.ShapeDtypeStruct(w.shape, F32)]
        + ([jax.ShapeDtypeStruct((8,) + ride[0].shape, ride[0].dtype),
            jax.ShapeDtypeStruct(ride[1].shape, ride[1].dtype)] if extra else []),
        scratch_shapes=(_gather_all_scratch(ride[0]) + _pair_join_scratch()) if extra else [],
        compiler_params=_params())(*dparts, w, x, g, dres, *extra)


def _even_bwd_out(dx1, w, yg, yp, ys, proj, wp, ps, ride):
    t = dx1.shape[0]
    nstep = t // TM_MM
    rows = TM_MM

    def body(dx_ref, w_ref, yg_ref, yp_ref, ys_ref, z_ref, u_ref, h_ref, wp_ref, ps_ref, g_ref,
             dy_ref, dz_ref, dw_ref, du_ref, dwp_ref, dps_ref, theirs_ref, nxt, send_sems, recv_sems):
        i = pl.program_id(0)
        blk = nstep - 1 - i
        start, finish = _pair_split_phases(g_ref, theirs_ref, send_sems, recv_sems)
        pl.when(i == 0)(start)
        _zero_at_first(i, dw_ref, dwp_ref, dps_ref, nxt)
        dxb = dx_ref[...].astype(BF16)
        dyg = _dot_nt(dxb, w_ref[...])
        dw_ref[...] += _dot_tn(yg_ref[...], dxb)
        silu_z, dsilu_z = _dsilu(z_ref[...].astype(F32))
        dyb = (dyg * silu_z).astype(BF16)
        dy_ref[...] = dyb
        dz_ref[:, :POOL_W] = (dyg[:, :POOL_W] * yp_ref[...].astype(F32) * dsilu_z[:, :POOL_W]).astype(BF16)
        dz_ref[:, POOL_W:] = (dyg[:, POOL_W:] * ys_ref[...].astype(F32) * dsilu_z[:, POOL_W:]).astype(BF16)
        pos = (blk * rows + 1 + lax.broadcasted_iota(jnp.int32, (rows, 1), 0)).astype(F32)
        pos_ext = (blk * rows + 1 + lax.broadcasted_iota(jnp.int32, (rows + POOL_HALO, 1), 0)).astype(F32)
        for gi in range(4):
            sl = slice(128 * gi, 128 * (gi + 1))
            wd = float(2 << gi)
            u = u_ref[:, sl].astype(F32)
            halo = jnp.where(blk == 0, 0.0, h_ref[:, sl].astype(F32))
            s = _pool_sums(jnp.concatenate([halo, u], axis=0), gi, True)[POOL_HALO:, :]
            pooled = (s / jnp.minimum(pos, wd) - u).astype(BF16)
            dy = dyb[:, sl].astype(F32)
            dps_ref[:, sl] += jnp.sum(dy * _dot(pooled, wp_ref[gi]), axis=0, keepdims=True)
            dmix = (jnp.concatenate([dy, nxt[:, sl]], axis=0) * ps_ref[:, sl]).astype(BF16)
            dwp_ref[gi] += _dot_tn(pooled, dmix[:rows, :])
            dpool = _dot_nt(dmix, wp_ref[gi])
            lead = _pool_sums(dpool / jnp.minimum(pos_ext, wd), gi, False)
            du_ref[:, sl] = (lead[:rows, :] - dpool[:rows, :]).astype(BF16)
            nxt[:, sl] = dy[:POOL_HALO, :]
        pl.when(i == nstep - 1)(finish)

    back = lambda wdt, cb=0: _rows(wdt, cb, rev=nstep, tm=rows)
    per = rows // POOL_HALO
    halo = pl.BlockSpec((POOL_HALO, POOL_W), lambda i: (jnp.maximum((nstep - 1 - i) * per - 1, 0), 0))
    mat = _full((D_MODEL, D_MODEL))
    return pl.pallas_call(
        body, name="even_bwd_out", grid=(nstep,),
        in_specs=[back(D_MODEL), mat, back(D_MODEL), back(POOL_W), back(SSM_W), back(D_MODEL, 1),
                  back(POOL_W, 0), halo, _full((4, 128, 128)), _full((1, POOL_W)), ANY],
        out_specs=[back(D_MODEL), back(D_MODEL), mat, back(POOL_W), _full((4, 128, 128)), _full((1, POOL_W)), ANY],
        out_shape=[jax.ShapeDtypeStruct((t, D_MODEL), BF16), jax.ShapeDtypeStruct((t, D_MODEL), BF16),
                   jax.ShapeDtypeStruct((D_MODEL, D_MODEL), F32), jax.ShapeDtypeStruct((t, POOL_W), BF16),
                   jax.ShapeDtypeStruct((4, 128, 128), F32), jax.ShapeDtypeStruct((1, POOL_W), F32),
                   _pair_split_shape(ride)],
        scratch_shapes=[pltpu.VMEM((POOL_HALO, POOL_W), F32)] + _pair_split_scratch(ride),
        compiler_params=_params())(dx1, w, yg, yp, ys, proj, proj, proj, wp, ps, ride)


def _ssm_bwd(dycat, proj, car_in_re, car_in_im, pm, pmt, mb_re, mb_im, p8_re, p8_im, q_re, q_im, qr_re, qr_im,
             cm_re, cm_im, dskip, wglu, ride):
    t = proj.shape[0]
    nblk = t // TM

    def body(dy_ref, u_ref, cin_re, cin_im, pm_ref, pmt_ref, mbre, mbim, p8re, p8im, qre, qim, qrre, qrim,
             cmre, cmim, d_ref, wg_ref, p_ref,
             du_ref, dmbre, dmbim, dcmre, dcmim, dare, daim, dd_ref, dwg_ref, got_ref,
             xs_re, xs_im, gs_re, gs_im, car_re, car_im, ent_re, ent_im, gcar_re, gcar_im, ysk, dysk,
             bounce, send_sems, recv_sems, local_sems):
        i = pl.program_id(0)
        start, finish = _chip_scatter_phases(p_ref, got_ref, bounce, send_sems, recv_sems, local_sems)
        pl.when(i == 0)(start)
        _zero_at_first(i, dmbre, dmbim, dcmre, dcmim, dare, daim, dd_ref, dwg_ref, gcar_re, gcar_im)
        us = _dot(pm_ref[...], u_ref[...])
        usb = us.astype(BF16)
        for j in range(4):
            xs_re[:, LCH * j:LCH * (j + 1)] = _dot(usb[:, 128 * j:128 * (j + 1)], mbre[j])
            xs_im[:, LCH * j:LCH * (j + 1)] = _dot(usb[:, 128 * j:128 * (j + 1)], mbim[j])
        car_re[...] = cin_re[0]
        car_im[...] = cin_im[0]
        _scan_fwd_block(xs_re, xs_im, p8re, p8im, qre, qim, car_re, car_im, ent_re, ent_im)
        for j in range(4):
            sl = slice(LCH * j, LCH * (j + 1))
            ysk[:, 128 * j:128 * (j + 1)] = (_dot_nt(xs_re[:, sl].astype(BF16), cmre[j])
                                             - _dot_nt(xs_im[:, sl].astype(BF16), cmim[j]))
        yvb = (ysk[...] + d_ref[...] * us).astype(BF16)
        gv = _dot(yvb, wg_ref[...])
        sg = _sig(gv[:, SSM_W:])
        dyss = _dot(pm_ref[...], dy_ref[...])
        dval = (dyss * sg).astype(BF16)
        dgate = (dyss * gv[:, :SSM_W] * sg * (1.0 - sg)).astype(BF16)
        dy = _dot_nt(dval, wg_ref[:, :SSM_W]) + _dot_nt(dgate, wg_ref[:, SSM_W:])
        dwg_ref[:, :SSM_W] += _dot_tn(yvb, dval)
        dwg_ref[:, SSM_W:] += _dot_tn(yvb, dgate)
        dd_ref[...] += jnp.sum(dy * us, axis=0, keepdims=True)
        dysk[...] = dy
        for j in range(4):
            sl = slice(LCH * j, LCH * (j + 1))
            dyj = dy[:, 128 * j:128 * (j + 1)].astype(BF16)
            gs_re[:, sl] = _dot(dyj, cmre[j])
            gs_im[:, sl] = -_dot(dyj, cmim[j])
            dcmre[j] += _dot_tn(dyj, xs_re[:, sl].astype(BF16))
            dcmim[j] -= _dot_tn(dyj, xs_im[:, sl].astype(BF16))
        row = lax.broadcasted_iota(jnp.int32, (SUB, SCAN_L), 0)
        for j in range(STATES // SCAN_L):
            sl = slice(SCAN_L * j, SCAN_L * (j + 1))
            are, aim = p8re[0:SUB, sl], -p8im[0:SUB, sl]

            def totals(k, v, sl=sl, are=are, aim=aim):
                r0 = pl.multiple_of((SEG_LEN - 2 - k) * SUB, SUB)
                vre, vim = _cmul_add(are, aim, v[0], v[1], gs_re[pl.ds(r0, SUB), sl], gs_im[pl.ds(r0, SUB), sl])
                gs_re[pl.ds(r0, SUB), sl] = vre
                gs_im[pl.ds(r0, SUB), sl] = vim
                return vre, vim

            top = (SEG_LEN - 1) * SUB
            fre, fim = lax.fori_loop(0, SEG_LEN - 1, totals,
                                     (gs_re[top:top + SUB, sl], gs_im[top:top + SUB, sl]), unroll=2)
            fre, fim, nre, nim = _segment_chain(fre, fim, qrre[:, sl], -qrim[:, sl],
                                                gcar_re[:, sl], gcar_im[:, sl], row, True)
            gcar_re[:, sl] = jnp.broadcast_to(fre[0:1, :], (SUB, SCAN_L))
            gcar_im[:, sl] = jnp.broadcast_to(fim[0:1, :], (SUB, SCAN_L))

            def fix(i2, acc, sl=sl, nre=nre, nim=nim):
                r0 = pl.multiple_of(i2 * SUB, SUB)
                rb = pl.multiple_of((SEG_LEN - 1 - i2) * SUB, SUB)
                gre, gim = _cmul_add(p8re[pl.ds(rb, SUB), sl], -p8im[pl.ds(rb, SUB), sl], nre, nim,
                                     gs_re[pl.ds(r0, SUB), sl], gs_im[pl.ds(r0, SUB), sl])
                gs_re[pl.ds(r0, SUB), sl] = gre
                gs_im[pl.ds(r0, SUB), sl] = gim
                rp = pl.multiple_of((i2 - 1) * SUB, SUB)
                xre, xim = xs_re[pl.ds(rp, SUB), sl], xs_im[pl.ds(rp, SUB), sl]
                return acc[0] + gre * xre + gim * xim, acc[1] + gim * xre - gre * xim

            g0re, g0im = _cmul_add(p8re[top:top + SUB, sl], -p8im[top:top + SUB, sl], nre, nim,
                                   gs_re[0:SUB, sl], gs_im[0:SUB, sl])
            gs_re[0:SUB, sl] = g0re
            gs_im[0:SUB, sl] = g0im
            ere, eim = ent_re[:, sl], ent_im[:, sl]
            acc0 = (dare[:, sl] + g0re * ere + g0im * eim, daim[:, sl] + g0im * ere - g0re * eim)
            are_acc, aim_acc = lax.fori_loop(1, SEG_LEN, fix, acc0, unroll=2)
            dare[:, sl] = are_acc
            daim[:, sl] = aim_acc
        for j in range(4):
            sl = slice(LCH * j, LCH * (j + 1))
            c4 = slice(128 * j, 128 * (j + 1))
            gre = gs_re[:, sl].astype(BF16)
            gim = gs_im[:, sl].astype(BF16)
            dmbre[j] += _dot_tn(usb[:, c4], gre)
            dmbim[j] += _dot_tn(usb[:, c4], gim)
            dysk[:, c4] = _dot_nt(gre, mbre[j]) + _dot_nt(gim, mbim[j]) + dysk[:, c4] * d_ref[:, c4]
        du_ref[...] = _dot(pmt_ref[...], dysk[...].astype(BF16)).astype(BF16)
        pl.when(i == nblk - 1)(finish)

    blk = (4, 128, LCH)
    pw = _full((SUB, STATES))
    p8 = _full((TM, STATES))
    perm = _full((TM, TM))
    car = pl.BlockSpec((1, SUB, STATES), lambda i: (nblk - 1 - i, 0, 0))
    big = lambda: pltpu.VMEM((TM, STATES), F32)
    small = lambda: pltpu.VMEM((SUB, STATES), F32)
    return pl.pallas_call(
        body, name="ssm_bwd", grid=(nblk,),
        in_specs=[_rows(SSM_W, 1, rev=nblk), _rows(SSM_W, 1, rev=nblk), car, car, perm, perm, _full(blk), _full(blk),
                  p8, p8, pw, pw, pw, pw, _full(blk), _full(blk), _full((1, SSM_W)), _full((SSM_W, 2 * SSM_W)), ANY],
        out_specs=[_rows(SSM_W, 0, rev=nblk), _full(blk), _full(blk), _full(blk), _full(blk), pw, pw,
                   _full((1, SSM_W)), _full((SSM_W, 2 * SSM_W)), ANY],
        out_shape=[jax.ShapeDtypeStruct((t, SSM_W), BF16)] + [jax.ShapeDtypeStruct(blk, F32)] * 4
        + [jax.ShapeDtypeStruct((SUB, STATES), F32)] * 2
        + [jax.ShapeDtypeStruct((1, SSM_W), F32), jax.ShapeDtypeStruct((SSM_W, 2 * SSM_W), F32),
           jax.ShapeDtypeStruct(ride.shape, ride.dtype)],
        scratch_shapes=[big(), big(), big(), big(), small(), small(), small(), small(), small(), small(),
                        pltpu.VMEM((TM, SSM_W), F32), pltpu.VMEM((TM, SSM_W), F32)] + _chip_scatter_scratch(ride),
        compiler_params=_params())(dycat, proj, car_in_re, car_in_im, pm, pmt, mb_re, mb_im, p8_re, p8_im,
                                   q_re, q_im, qr_re, qr_im, cm_re, cm_im, dskip, wglu, ride)


def _adamw(w, g, m, v, name):
    rows = w.shape[0]
    tr = 256 if rows % 256 == 0 else rows
    c1 = 1.0 / (1.0 - ADAM_B1 ** ADAM_STEP)
    c2 = 1.0 / (1.0 - ADAM_B2 ** ADAM_STEP)

    def body(w_ref, g_ref, m_ref, v_ref, d_ref, nm_ref, nv_ref):
        gv = g_ref[...]
        m = ADAM_B1 * m_ref[...] + (1.0 - ADAM_B1) * gv
        v = ADAM_B2 * v_ref[...] + (1.0 - ADAM_B2) * (gv * gv)
        nm_ref[...] = m
        nv_ref[...] = v
        d_ref[...] = -ADAM_LR * ((m * c1) / (jnp.sqrt(v * c2) + ADAM_EPS) + ADAM_WD * w_ref[...])

    spec = pl.BlockSpec((tr, D_MODEL), lambda i: (i, 0))
    shp = jax.ShapeDtypeStruct((rows, D_MODEL), F32)
    return pl.pallas_call(
        body, name=name, grid=(rows // tr,), in_specs=[spec] * 4, out_specs=[spec] * 3, out_shape=[shp] * 3,
        compiler_params=_params())(w, g, m, v)


def _core_index():
    return lax.axis_index("c").astype(jnp.int32).reshape(1)


def _pair_add(g, theirs, out_dtype, name):
    n, half, _ = theirs.shape
    br = 128
    nb = half // br

    def body(c_ref, a_ref, b_ref, o_ref):
        o_ref[...] = (a_ref[...] + b_ref[...]).astype(out_dtype)

    spec = pl.BlockSpec((1, br, D_MODEL), lambda i, j, c: (i, j, 0))
    grid_spec = pltpu.PrefetchScalarGridSpec(
        num_scalar_prefetch=1, grid=(n, nb),
        in_specs=[pl.BlockSpec((1, br, D_MODEL), lambda i, j, c: (i, c[0] * nb + j, 0)), spec], out_specs=spec)
    return pl.pallas_call(
        body, name=name, grid_spec=grid_spec, out_shape=jax.ShapeDtypeStruct(theirs.shape, out_dtype),
        compiler_params=_params(2))(_core_index(), g, theirs)


def _adamw_rows(w, m, v, g_mine, g_theirs, row0, br, name):
    rows = w.shape[0]
    b0 = row0 // br
    per_half = g_mine.shape[0] // br
    c1 = 1.0 / (1.0 - ADAM_B1 ** ADAM_STEP)
    c2 = 1.0 / (1.0 - ADAM_B2 ** ADAM_STEP)

    def body(c_ref, w_ref, gm_ref, gt_ref, m_ref, v_ref, g_ref, d_ref, nm_ref, nv_ref):
        gv = jnp.where((b0 + pl.program_id(0)) // per_half == c_ref[0], gm_ref[...], gt_ref[...])
        m = ADAM_B1 * m_ref[...] + (1.0 - ADAM_B1) * gv
        v = ADAM_B2 * v_ref[...] + (1.0 - ADAM_B2) * (gv * gv)
        g_ref[...] = gv
        nm_ref[...] = m
        nv_ref[...] = v
        d_ref[...] = -ADAM_LR * ((m * c1) / (jnp.sqrt(v * c2) + ADAM_EPS) + ADAM_WD * w_ref[...])

    spec = pl.BlockSpec((br, D_MODEL), lambda i, c: (i, 0))
    part = pl.BlockSpec((br, D_MODEL), lambda i, c: ((b0 + i) % per_half, 0))
    shp = jax.ShapeDtypeStruct((rows, D_MODEL), F32)
    grid_spec = pltpu.PrefetchScalarGridSpec(
        num_scalar_prefetch=1, grid=(rows // br,), in_specs=[spec, part, part, spec, spec], out_specs=[spec] * 4)
    return pl.pallas_call(
        body, name=name, grid_spec=grid_spec, out_shape=[shp] * 4,
        compiler_params=_params())(_core_index(), w, g_mine, g_theirs, m, v)


def _sum_lead(a, name):
    n, rows, _ = a.shape
    tr = 128 if rows % 128 == 0 else rows

    def body(a_ref, o_ref):
        acc = a_ref[0].astype(F32)
        for k in range(1, n):
            acc = acc + a_ref[k].astype(F32)
        o_ref[...] = acc

    return pl.pallas_call(
        body, name=name, grid=(rows // tr,),
        in_specs=[pl.BlockSpec((n, tr, D_MODEL), lambda i: (0, i, 0))],
        out_specs=pl.BlockSpec((tr, D_MODEL), lambda i: (i, 0)),
        out_shape=jax.ShapeDtypeStruct((rows, D_MODEL), F32), compiler_params=_params())(a)


ANY = pl.BlockSpec(memory_space=pl.ANY)


def _mesh_pos():
    return lax.axis_index("x"), lax.axis_index("y"), lax.axis_index("c")


def _gather_phases(in_refs, out_refs, bounces, send_sems, recv_sems, local_sems):
    na = len(in_refs)
    halves = [r.shape[0] // 2 for r in in_refs]
    ncopy = 3 * na
    x, y, c = _mesh_pos()
    me = 2 * x + y
    sibling = (x, y, 1 - c)
    chips = [(1 - x, y), (x, 1 - y), (1 - x, 1 - y)]
    ids = [2 * chip[0] + chip[1] for chip in chips]

    def piece(a, q, h):
        return out_refs[a].at[q, pl.ds(h * halves[a], halves[a]), :]

    def copy(s, a, q, h, to, src=None):
        return pltpu.make_async_remote_copy(
            src_ref=piece(a, q, h) if src is None else src, dst_ref=piece(a, q, h),
            send_sem=send_sems.at[s], recv_sem=recv_sems.at[s], device_id=to, device_id_type=MESH)

    def sends():
        return [copy(j * na + a, a, me, c, (*chip, c), src=in_refs[a].at[pl.ds(c * halves[a], halves[a]), :])
                for j, chip in enumerate(chips) for a in range(na)]

    def forwards():
        return [copy(ncopy + j * na + a, a, ids[j], c, sibling) for j in range(3) for a in range(na)]

    def stores():
        return [pltpu.make_async_copy(bounces[a], out_refs[a].at[me], local_sems.at[na + a]) for a in range(na)]

    def start():
        loads = [pltpu.make_async_copy(in_refs[a], bounces[a], local_sems.at[a]) for a in range(na)]
        for cp in loads:
            cp.start()
        for cp in sends():
            cp.start()
        for ld, st in zip(loads, stores()):
            ld.wait()
            st.start()

    def forward():
        fwd = forwards()
        for j in range(3):
            for a in range(na):
                copy(j * na + a, a, ids[j], c, (x, y, c)).wait_recv()
                fwd[j * na + a].start()

    def finish():
        for j in range(3):
            for a in range(na):
                copy(ncopy + j * na + a, a, ids[j], 1 - c, (x, y, c)).wait_recv()
        for cp in sends() + forwards():
            cp.wait_send()
        for cp in stores():
            cp.wait()

    return start, forward, finish


def _gather_scratch(arrs):
    ncopy = 3 * len(arrs)
    return ([pltpu.VMEM(a.shape, a.dtype) for a in arrs]
            + [pltpu.SemaphoreType.DMA((2 * ncopy,)), pltpu.SemaphoreType.DMA((2 * ncopy,)),
               pltpu.SemaphoreType.DMA((2 * len(arrs),))])


def _gather_weights(arrs):
    na = len(arrs)

    def body(*refs):
        start, forward, finish = _gather_phases(refs[:na], refs[na:2 * na], refs[2 * na:3 * na], *refs[3 * na:])
        start()
        forward()
        finish()

    return pl.pallas_call(
        body, name="gather_weights", in_specs=[ANY] * na, out_specs=[ANY] * na,
        out_shape=[jax.ShapeDtypeStruct((4,) + a.shape, a.dtype) for a in arrs],
        scratch_shapes=_gather_scratch(arrs),
        compiler_params=pltpu.CompilerParams(vmem_limit_bytes=VMEM_LIMIT),
    )(*arrs)


def _gather_all_phases(v_ref, o_ref, bounce, send_sems, recv_sems, local_sems):
    x, y, c = _mesh_pos()
    sibling = (x, y, 1 - c)
    chips = [(1 - x, y), (x, 1 - y), (1 - x, 1 - y)]

    def blk(px, py, pc):
        return o_ref.at[4 * px + 2 * py + pc]

    def copy(k, block, to, src=None):
        return pltpu.make_async_remote_copy(
            src_ref=blk(*block) if src is None else src, dst_ref=blk(*block),
            send_sem=send_sems.at[k], recv_sem=recv_sems.at[k], device_id=to, device_id_type=MESH)

    def first():
        return ([copy(0, (x, y, c), sibling, src=v_ref)]
                + [copy(1 + j, (x, y, c), (*chip, c), src=v_ref) for j, chip in enumerate(chips)])

    def passed():
        return [copy(4 + j, (*chip, c), sibling) for j, chip in enumerate(chips)]

    def store():
        return pltpu.make_async_copy(bounce, blk(x, y, c), local_sems.at[1])

    def start():
        load = pltpu.make_async_copy(v_ref, bounce, local_sems.at[0])
        load.start()
        for cp in first():
            cp.start()
        load.wait()
        store().start()

    def forward():
        fwd = passed()
        for j, chip in enumerate(chips):
            copy(1 + j, (*chip, c), (x, y, c)).wait_recv()
            fwd[j].start()

    def finish():
        copy(0, (x, y, 1 - c), (x, y, c)).wait_recv()
        for j, chip in enumerate(chips):
            copy(4 + j, (*chip, 1 - c), (x, y, c)).wait_recv()
        for cp in first() + passed():
            cp.wait_send()
        store().wait()

    return start, forward, finish


def _gather_all_scratch(v):
    return [pltpu.VMEM(v.shape, v.dtype), pltpu.SemaphoreType.DMA((7,)), pltpu.SemaphoreType.DMA((7,)),
            pltpu.SemaphoreType.DMA((2,))]


def _scatter_and_gather(p, v, name):
    def body(p_ref, v_ref, got_ref, o_ref, p_bounce, p_send, p_recv, p_local, bounce, send_sems, recv_sems,
             local_sems):
        start, finish = _chip_scatter_phases(p_ref, got_ref, p_bounce, p_send, p_recv, p_local)
        g_start, g_forward, g_finish = _gather_all_phases(v_ref, o_ref, bounce, send_sems, recv_sems, local_sems)
        start()
        g_start()
        g_forward()
        g_finish()
        finish()

    return pl.pallas_call(
        body, name=name, in_specs=[ANY, ANY], out_specs=[ANY, ANY],
        out_shape=[jax.ShapeDtypeStruct(p.shape, p.dtype), jax.ShapeDtypeStruct((8,) + v.shape, v.dtype)],
        scratch_shapes=_chip_scatter_scratch(p) + _gather_all_scratch(v),
    )(p, v)


def _pair_split_phases(g_ref, theirs_ref, send_sems, recv_sems):
    n, rows, _ = g_ref.shape
    half = rows // 2
    ch = half // COMM_CHUNKS
    x, y, c = _mesh_pos()

    def gives():
        return [pltpu.make_async_remote_copy(
            src_ref=g_ref.at[q, pl.ds((1 - c) * half + k * ch, ch), :],
            dst_ref=theirs_ref.at[q, pl.ds(k * ch, ch), :],
            send_sem=send_sems.at[q * COMM_CHUNKS + k], recv_sem=recv_sems.at[q * COMM_CHUNKS + k],
            device_id=(x, y, 1 - c), device_id_type=MESH) for q in range(n) for k in range(COMM_CHUNKS)]

    def start():
        for cp in gives():
            cp.start()

    def finish():
        for cp in gives():
            cp.wait()

    return start, finish


def _pair_split_scratch(g):
    return [pltpu.SemaphoreType.DMA((g.shape[0] * COMM_CHUNKS,)), pltpu.SemaphoreType.DMA((g.shape[0] * COMM_CHUNKS,))]


def _pair_split_shape(g):
    return jax.ShapeDtypeStruct((g.shape[0], g.shape[1] // 2, D_MODEL), g.dtype)


def _pair_split(g, name):
    def body(g_ref, theirs_ref, send_sems, recv_sems):
        start, finish = _pair_split_phases(g_ref, theirs_ref, send_sems, recv_sems)
        start()
        finish()

    return pl.pallas_call(
        body, name=name, in_specs=[ANY], out_specs=ANY, out_shape=_pair_split_shape(g),
        scratch_shapes=_pair_split_scratch(g))(g)


def _chip_scatter_phases(p_ref, o_ref, bounce, send_sems, recv_sems, local_sems):
    x, y, c = _mesh_pos()
    me = 2 * x + y
    chips = [(1 - x, y), (x, 1 - y), (1 - x, 1 - y)]

    def keep():
        return pltpu.make_async_copy(bounce, o_ref.at[me], local_sems.at[1])

    def sends():
        return [pltpu.make_async_remote_copy(
            src_ref=p_ref.at[2 * chip[0] + chip[1]], dst_ref=o_ref.at[me],
            send_sem=send_sems.at[j], recv_sem=recv_sems.at[j], device_id=(*chip, c), device_id_type=MESH)
            for j, chip in enumerate(chips)]

    def start():
        load = pltpu.make_async_copy(p_ref.at[me], bounce, local_sems.at[0])
        load.start()
        for cp in sends():
            cp.start()
        load.wait()
        keep().start()

    def finish():
        for j, chip in enumerate(chips):
            q = 2 * chip[0] + chip[1]
            pltpu.make_async_remote_copy(
                src_ref=p_ref.at[q], dst_ref=o_ref.at[q], send_sem=send_sems.at[j], recv_sem=recv_sems.at[j],
                device_id=(*chip, c), device_id_type=MESH).wait_recv()
        for cp in sends():
            cp.wait_send()
        keep().wait()

    return start, finish


def _chip_scatter_scratch(p):
    return [pltpu.VMEM(p.shape[1:], p.dtype), pltpu.SemaphoreType.DMA((3,)), pltpu.SemaphoreType.DMA((3,)),
            pltpu.SemaphoreType.DMA((2,))]


def _pair_join_phases(r_ref, o_ref, send_sems, recv_sems):
    ch = r_ref.shape[0] // COMM_CHUNKS
    x, y, c = _mesh_pos()

    def gives():
        return [pltpu.make_async_remote_copy(
            src_ref=r_ref.at[pl.ds(k * ch, ch), :], dst_ref=o_ref.at[pl.ds(k * ch, ch), :],
            send_sem=send_sems.at[k], recv_sem=recv_sems.at[k], device_id=(x, y, 1 - c), device_id_type=MESH)
            for k in range(COMM_CHUNKS)]

    def start():
        for cp in gives():
            cp.start()

    def finish():
        for cp in gives():
            cp.wait()

    return start, finish


def _pair_join_scratch():
    return [pltpu.SemaphoreType.DMA((COMM_CHUNKS,)), pltpu.SemaphoreType.DMA((COMM_CHUNKS,))]


def _pair_join(r, name):
    def body(r_ref, o_ref, send_sems, recv_sems):
        start, finish = _pair_join_phases(r_ref, o_ref, send_sems, recv_sems)
        start()
        finish()

    return pl.pallas_call(
        body, name=name, in_specs=[ANY], out_specs=ANY, out_shape=jax.ShapeDtypeStruct(r.shape, r.dtype),
        scratch_shapes=_pair_join_scratch())(r)


SHARD_BIG = (("even_w_in", (1024, 512)), ("ssm_w_glu", (512, 256)), ("even_w_out", (256, 1024)),
             ("odd_w_in", (1024, 768)), ("odd_w_out", (256, 1024)))
SHARD_SMALL = (("odd_norm", 1), ("conv_w", CONV_K), ("conv_b", 1), ("conv_ln_g", 1), ("conv_ln_b", 1))
REP_NAMES = (("even_norm", (1024,)), ("pool_w", (4, 128, 128)), ("pool_scale", (512,)), ("ssm_log_dt", (32,)),
             ("ssm_a_re", (32, 64)), ("ssm_a_im", (32, 64)), ("ssm_b_re", (32, 64, 16)), ("ssm_b_im", (32, 64, 16)),
             ("ssm_c_re", (32, 16, 64)), ("ssm_c_im", (32, 16, 64)), ("ssm_d", (512,)), ("final_norm", (1024,)))


def _pack_rep(d):
    flat = jnp.concatenate([d[n].reshape(-1) for n, _ in REP_NAMES])
    return jnp.pad(flat, (0, REP_ROWS * D_MODEL - flat.shape[0])).reshape(REP_ROWS, D_MODEL)


def _unpack_rep(buf):
    flat = buf.reshape(-1)
    out = {}
    off = 0
    for n, shp in REP_NAMES:
        size = 1
        for s in shp:
            size *= s
        out[n] = flat[off:off + size].reshape(shp)
        off += size
    return out


def _cols_split(full, cols):
    rows = full.shape[0]
    return full.reshape(rows, 4, cols).transpose(1, 0, 2).reshape(4, -1, D_MODEL)


def _block_diag(a):
    a = a.reshape(4, 8, GROUP_DIM, N_STATE)
    eye = jnp.eye(8, dtype=a.dtype)
    return (a[:, :, :, None, :] * eye[None, :, None, :, None]).reshape(4, 128, LCH)


def _block_diag_take(m):
    m = m.reshape(4, 8, GROUP_DIM, 8, N_STATE)
    eye = jnp.eye(8, dtype=m.dtype)
    return jnp.sum(m * eye[None, :, None, :, None], axis=3).reshape(N_GROUPS, GROUP_DIM, N_STATE)


def _ssm_discretise(log_dt, a_re, a_im, b_re, b_im):
    dt = jnp.exp(log_dt)[:, None]
    mag = jnp.exp(a_re * dt)
    ang = a_im * dt
    abar_re = mag * jnp.cos(ang)
    abar_im = mag * jnp.sin(ang)
    den = a_re * a_re + a_im * a_im
    nr = abar_re - 1.0
    ni = abar_im
    k_re = (nr * a_re + ni * a_im) / den
    k_im = (ni * a_re - nr * a_im) / den
    bb_re = k_re[..., None] * b_re - k_im[..., None] * b_im
    bb_im = k_re[..., None] * b_im + k_im[..., None] * b_re
    return abar_re, abar_im, bb_re, bb_im


def _scan_tables(log_dt, a_re, a_im):
    dt = jnp.exp(log_dt)[:, None]
    lam_re = (a_re * dt).reshape(1, STATES)
    lam_im = (a_im * dt).reshape(1, STATES)

    def powers(k):
        mag = jnp.exp(k * lam_re)
        return mag * jnp.cos(k * lam_im), mag * jnp.sin(k * lam_im)

    p_re, p_im = powers((1 + jnp.arange(TM) // SUB).astype(F32)[:, None])
    q_re, q_im = powers((SEG_LEN * (1 + jnp.arange(SUB))).astype(F32)[:, None])
    return p_re, p_im, q_re, q_im


def _local_step(x, tgt, w, shard):
    row = lambda a: a.reshape(1, -1)
    (e_w_in,) = _gather_weights([shard["even_w_in"].astype(BF16)])
    wp = w["pool_w"].astype(BF16)
    ssm_in = (w["ssm_log_dt"], w["ssm_a_re"], w["ssm_a_im"], w["ssm_b_re"], w["ssm_b_im"])
    (abar_re, abar_im, bb_re, bb_im), ssm_vjp = jax.vjp(_ssm_discretise, *ssm_in)
    mb_re = _block_diag(bb_re.transpose(0, 2, 1)).astype(BF16)
    mb_im = _block_diag(bb_im.transpose(0, 2, 1)).astype(BF16)
    cm_re = _block_diag(w["ssm_c_re"]).astype(BF16)
    cm_im = _block_diag(w["ssm_c_im"]).astype(BF16)
    p8_re, p8_im, q_re, q_im = _scan_tables(w["ssm_log_dt"], w["ssm_a_re"], w["ssm_a_im"])
    qr_re, qr_im = q_re[::-1], q_im[::-1]
    pm = _perm_matrix()
    pmt = pm.T
    g0, gf = row(w["even_norm"]), row(w["final_norm"])
    ps, dskip = row(w["pool_scale"]), row(w["ssm_d"])

    proj, yp, (g_glu, g_eout) = _norm_in(
        x, g0, e_w_in, "even_in", ride=[shard["ssm_w_glu"].astype(BF16), shard["even_w_out"].astype(BF16)],
        pool=(wp, ps))
    wglu = g_glu.transpose(1, 0, 2).reshape(SSM_W, 2 * SSM_W)
    e_w_out = g_eout.reshape(D_MODEL, D_MODEL)
    (ys, car_re, car_im), (g_oin, g_oout, g_small) = _ssm_fwd(
        proj, pm, pmt, mb_re, mb_im, p8_re, p8_im, q_re, q_im, cm_re, cm_im, dskip, wglu, ride=_odd_shards(shard))
    o_w_in, o_w_out = g_oin, g_oout.reshape(D_MODEL, D_MODEL)
    sm = g_small.transpose(1, 0, 2).reshape(SMALL_ROWS, D_MODEL)
    cw = sm[1:1 + HALO]
    g1, cb, lg, lb = sm[0:1], sm[32:33], sm[33:34], sm[34:35]
    x1, yg = _even_out(yp, ys, proj, x, e_w_out)
    q, _, _ = _norm_in(x1, g1, o_w_in, "odd_in")
    y2, cv = _conv_fwd(q, cw, cb, lg, lb)
    dx2, loss_lanes, d_gf = _odd_out_loss(y2, x1, o_w_out, gf, tgt)

    dcv, dz2, d_o_w_out, d_lg, d_lb = _odd_bwd_out(dx2, o_w_out, y2, cv, q, lg, lb)
    dval, dgate, d_cw, d_cb = _conv_bwd(dcv, q, cw)
    dx1, d_g1, d_o_w_in = _in_bwd([dval, dgate, dz2], o_w_in, x1, g1, dx2, "odd_in_bwd")
    g_odd = _pack_odd_grads({
        "odd_w_in": d_o_w_in, "odd_w_out": d_o_w_out, "odd_norm": d_g1.reshape(-1),
        "conv_w": d_cw.reshape(HALO, SUB, D_MODEL).sum(axis=1)[:CONV_K], "conv_b": d_cb.reshape(-1),
        "conv_ln_g": d_lg.reshape(-1), "conv_ln_b": d_lb.reshape(-1)})
    dycat, dz, d_e_w_out, dup, d_wp, d_ps, theirs_odd = _even_bwd_out(dx1, e_w_out, yg, yp, ys, proj, wp, ps, g_odd)
    sums_odd = _pair_add(g_odd, theirs_odd, BF16, "pair_add_odd")
    (dus, d_mb_re, d_mb_im, d_cm_re, d_cm_im, da_re, da_im, d_dskip, d_wglu, got_odd) = _ssm_bwd(
        dycat, proj, car_re, car_im, pm, pmt, mb_re, mb_im, p8_re, p8_im, q_re, q_im, qr_re, qr_im,
        cm_re, cm_im, dskip, wglu, sums_odd)
    d_abar_re = jnp.sum(da_re, axis=0).reshape(N_GROUPS, N_STATE)
    d_abar_im = jnp.sum(da_im, axis=0).reshape(N_GROUPS, N_STATE)
    d_bb_re = _block_diag_take(d_mb_re).transpose(0, 2, 1)
    d_bb_im = _block_diag_take(d_mb_im).transpose(0, 2, 1)
    d_log_dt, d_a_re, d_a_im, d_b_re, d_b_im = ssm_vjp((d_abar_re, d_abar_im, d_bb_re, d_bb_im))
    rep_early = _pack_rep({
        "even_norm": jnp.zeros((D_MODEL,), F32), "pool_w": d_wp, "pool_scale": d_ps.reshape(-1),
        "ssm_log_dt": d_log_dt, "ssm_a_re": d_a_re, "ssm_a_im": d_a_im, "ssm_b_re": d_b_re, "ssm_b_im": d_b_im,
        "ssm_c_re": _block_diag_take(d_cm_re), "ssm_c_im": _block_diag_take(d_cm_im),
        "ssm_d": d_dskip.reshape(-1), "final_norm": d_gf.reshape(-1)})
    odd_mine = _sum_lead(got_odd, "chip_sum_odd")
    dx, d_g0, d_e_w_in, rep_parts, odd_theirs = _in_bwd([dup, dus, dz], e_w_in, x, g0, dx1, "even_in_bwd",
                                                        ride=(rep_early, odd_mine))

    grads = {"even_norm": d_g0, "even_w_in": d_e_w_in, "ssm_w_glu": d_wglu, "even_w_out": d_e_w_out}
    return jnp.sum(loss_lanes), dx, grads, (odd_mine, odd_theirs), rep_parts


WEIGHT_NAMES = ("even_norm", "even_w_in", "pool_w", "pool_scale", "ssm_log_dt", "ssm_a_re", "ssm_a_im",
                "ssm_b_re", "ssm_b_im", "ssm_c_re", "ssm_c_im", "ssm_d", "ssm_w_glu", "even_w_out", "odd_norm",
                "odd_w_in", "conv_w", "conv_b", "conv_ln_g", "conv_ln_b", "odd_w_out", "final_norm")
SHARDED = tuple(n for n, _ in SHARD_BIG) + tuple(n for n, _ in SHARD_SMALL)


SMALL_ROWS = 64


def _odd_shards(shard):
    small = jnp.concatenate([shard[n].reshape(r, 256) for n, r in SHARD_SMALL], axis=0)
    small = jnp.pad(small, ((0, SMALL_ROWS - small.shape[0]), (0, 0)))
    return [shard["odd_w_in"].astype(BF16), shard["odd_w_out"].astype(BF16), small]


def _pack_small(d):
    small = jnp.concatenate([d[n].reshape(r, -1) for n, r in SHARD_SMALL], axis=0)
    if small.shape[1] == D_MODEL:
        small = small.reshape(35, 4, 256).transpose(1, 0, 2)
    small = small.reshape(-1, 35 * 256)
    small = jnp.pad(small, ((0, 0), (0, ROWS_SMALL * D_MODEL - 35 * 256)))
    return small.reshape(-1, ROWS_SMALL, D_MODEL)


def _unpack_small(buf):
    small = buf.reshape(-1)[:35 * 256].reshape(35, 256)
    out = {}
    off = 0
    for n, r in SHARD_SMALL:
        out[n] = small[off:off + r].reshape((r, 256) if r > 1 else (256,))
        off += r
    return out


EVEN_PACK = (("even_w_in", 0, 512), ("ssm_w_glu", 512, 128), ("even_w_out", 640, 256))
ROWS_EVEN = 1024
ODD_PACK = (("odd_w_in", 0, 768), ("odd_w_out", 768, 256))
ODD_SMALL_ROW = 1024
ROWS_ODD = 1280


def _pack_even_grads(g):
    parts = [g["even_w_in"].reshape(4, -1, D_MODEL), _cols_split(g["ssm_w_glu"], 256),
             g["even_w_out"].reshape(4, -1, D_MODEL), jnp.zeros((4, ROWS_EVEN - 896, D_MODEL), F32)]
    return jnp.concatenate(parts, axis=1)


def _pack_odd_grads(g):
    parts = [g["odd_w_in"].reshape(4, -1, D_MODEL), g["odd_w_out"].reshape(4, -1, D_MODEL), _pack_small(g),
             jnp.zeros((4, ROWS_ODD - ODD_SMALL_ROW - ROWS_SMALL, D_MODEL), F32)]
    return jnp.concatenate(parts, axis=1)


def kernel(x, even_norm, even_w_in, pool_w, pool_scale, ssm_log_dt, ssm_a_re, ssm_a_im, ssm_b_re, ssm_b_im, ssm_c_re, ssm_c_im, ssm_d, ssm_w_glu, even_w_out, odd_norm, odd_w_in, conv_w, conv_b, conv_ln_g, conv_ln_b, odd_w_out, final_norm, loss_target, m_even_norm, m_even_w_in, m_pool_w, m_pool_scale, m_ssm_log_dt, m_ssm_a_re, m_ssm_a_im, m_ssm_b_re, m_ssm_b_im, m_ssm_c_re, m_ssm_c_im, m_ssm_d, m_ssm_w_glu, m_even_w_out, m_odd_norm, m_odd_w_in, m_conv_w, m_conv_b, m_conv_ln_g, m_conv_ln_b, m_odd_w_out, m_final_norm, v_even_norm, v_even_w_in, v_pool_w, v_pool_scale, v_ssm_log_dt, v_ssm_a_re, v_ssm_a_im, v_ssm_b_re, v_ssm_b_im, v_ssm_c_re, v_ssm_c_im, v_ssm_d, v_ssm_w_glu, v_even_w_out, v_odd_norm, v_odd_w_in, v_conv_w, v_conv_b, v_conv_ln_g, v_conv_ln_b, v_odd_w_out, v_final_norm):
    ws = dict(zip(WEIGHT_NAMES, (even_norm, even_w_in, pool_w, pool_scale, ssm_log_dt, ssm_a_re, ssm_a_im, ssm_b_re,
                                 ssm_b_im, ssm_c_re, ssm_c_im, ssm_d, ssm_w_glu, even_w_out, odd_norm, odd_w_in,
                                 conv_w, conv_b, conv_ln_g, conv_ln_b, odd_w_out, final_norm)))
    ms = dict(zip(WEIGHT_NAMES, (m_even_norm, m_even_w_in, m_pool_w, m_pool_scale, m_ssm_log_dt, m_ssm_a_re,
                                 m_ssm_a_im, m_ssm_b_re, m_ssm_b_im, m_ssm_c_re, m_ssm_c_im, m_ssm_d, m_ssm_w_glu,
                                 m_even_w_out, m_odd_norm, m_odd_w_in, m_conv_w, m_conv_b, m_conv_ln_g, m_conv_ln_b,
                                 m_odd_w_out, m_final_norm)))
    vs = dict(zip(WEIGHT_NAMES, (v_even_norm, v_even_w_in, v_pool_w, v_pool_scale, v_ssm_log_dt, v_ssm_a_re,
                                 v_ssm_a_im, v_ssm_b_re, v_ssm_b_im, v_ssm_c_re, v_ssm_c_im, v_ssm_d, v_ssm_w_glu,
                                 v_even_w_out, v_odd_norm, v_odd_w_in, v_conv_w, v_conv_b, v_conv_ln_g, v_conv_ln_b,
                                 v_odd_w_out, v_final_norm)))
    lead = {n: a.shape for n, a in ws.items()}
    drop = lambda d: {n: (a[0] if n != "final_norm" else a) for n, a in d.items()}
    ws, ms, vs = drop(ws), drop(ms), drop(vs)

    shard = {n: ws[n] for n in SHARDED}
    rep = {n: ws[n] for n, _ in REP_NAMES}
    loss_part, grad_x, grads, (odd_mine, odd_theirs), rep_parts = _local_step(x[0], loss_target[0], rep, shard)
    loss = lax.psum(loss_part, ("x", "y", "c"))

    g_even = _pack_even_grads(grads)
    got_even, late_parts = _scatter_and_gather(
        _pair_add(g_even, _pair_split(g_even, "pair_split_even"), BF16, "pair_add_even"),
        jnp.pad(grads["even_norm"], ((0, SUB - 1), (0, 0))), "scatter_even_gather_late")
    even_mine = _sum_lead(got_even, "chip_sum_even")
    even_theirs = _pair_join(even_mine, "pair_join_even")
    outs = [{}, {}, {}, {}]
    for pack, mine, theirs in ((EVEN_PACK, even_mine, even_theirs), (ODD_PACK, odd_mine, odd_theirs)):
        for n, row0, rows in pack:
            view = lambda a: a.reshape(rows, D_MODEL)
            res = _adamw_rows(view(ws[n]), view(ms[n]), view(vs[n]), mine, theirs, row0, 128, "adamw_" + n)
            for o, r in zip(outs, res):
                o[n] = r
    small = lambda d: _pack_small({n: d[n] for n, _ in SHARD_SMALL})[0]
    res = _adamw_rows(small(ws), small(ms), small(vs), odd_mine, odd_theirs, ODD_SMALL_ROW, ROWS_SMALL, "adamw_small")
    for o, r in zip(outs, res):
        o.update(_unpack_small(r))
    g_rep = lax.dynamic_update_slice(_sum_lead(rep_parts, "rep_sum"), _sum_lead(late_parts, "late_sum")[0:1], (0, 0))
    res = _adamw(_pack_rep(rep), g_rep, _pack_rep({n: ms[n] for n, _ in REP_NAMES}),
                 _pack_rep({n: vs[n] for n, _ in REP_NAMES}), "adamw_rep")
    for o, r in zip(outs, (g_rep,) + tuple(res)):
        o.update(_unpack_rep(r))

    leaves = [[o[n].reshape(lead[n]) for n in WEIGHT_NAMES] for o in outs]
    return (loss, grad_x[None], *leaves[0], *leaves[1], *leaves[2], *leaves[3])
```
